```python
import math
import jax, jax.numpy as jnp
from jax import lax
import numpy as np

D_MODEL = 1024
BATCH = 8
SEQ = 4096
DEPTH = 4

CHUNK = 64
N_MIXERS = 2
N_LAYERS_A = (DEPTH + 1) // 2
N_LAYERS_B = DEPTH // 2
FOX_HEADS = 16
FOX_HEAD_DIM = D_MODEL // FOX_HEADS
Q_BLOCK = 128
S5_GROUP = 16
S5_GROUPS = D_MODEL // S5_GROUP
S5_STATE = 64
D_FF = 256 * ((8 * D_MODEL // 3 + 255) // 256)
ALPHA = (2.0 * DEPTH) ** 0.25
BETA = (8.0 * DEPTH) ** -0.25
LN_EPS = 1e-5
NEG_INF = -1e30

kernel_name = "fox_s5_macaron_deepnorm_hybrid"


def layer_norm(x, g, b):
    xf = x.astype(jnp.float32)
    mu = jnp.mean(xf, axis=-1, keepdims=True)
    var = jnp.mean(jnp.square(xf - mu), axis=-1, keepdims=True)
    y = (xf - mu) * lax.rsqrt(var + LN_EPS) * g.astype(jnp.float32) + b.astype(jnp.float32)
    return y.astype(x.dtype)


def swiglu(x, w_in, w_out):
    gate, up = jnp.split(x @ w_in, 2, axis=-1)
    return (jax.nn.silu(gate) * up) @ w_out


def fox_attention(x, w_in, b_f, w_o):
    bsz, seq, d = x.shape
    proj = x @ w_in
    q = proj[..., :d].reshape(bsz, seq, FOX_HEADS, FOX_HEAD_DIM)
    k = proj[..., d:2 * d].reshape(bsz, seq, FOX_HEADS, FOX_HEAD_DIM)
    v = proj[..., 2 * d:3 * d].reshape(bsz, seq, FOX_HEADS, FOX_HEAD_DIM)
    f_logit = proj[..., 3 * d:].astype(jnp.float32) + b_f.astype(jnp.float32)
    log_f = jax.nn.log_sigmoid(f_logit)
    cum = jnp.cumsum(log_f, axis=1).transpose(0, 2, 1)
    n_blk = seq // Q_BLOCK
    q_blocks = q.reshape(bsz, n_blk, Q_BLOCK, FOX_HEADS, FOX_HEAD_DIM).transpose(1, 0, 2, 3, 4)
    c_blocks = cum.reshape(bsz, FOX_HEADS, n_blk, Q_BLOCK).transpose(2, 0, 1, 3)
    starts = jnp.arange(n_blk, dtype=jnp.int32) * Q_BLOCK
    k_pos = jnp.arange(seq, dtype=jnp.int32)
    scale = 1.0 / math.sqrt(FOX_HEAD_DIM)

    def one_block(args):
        q_i, c_i, start = args
        s = jnp.einsum('bqhd,bkhd->bhqk', q_i, k).astype(jnp.float32) * scale
        s = s + (c_i[..., :, None] - cum[:, :, None, :])
        q_pos = start + jnp.arange(Q_BLOCK, dtype=jnp.int32)
        mask = k_pos[None, :] <= q_pos[:, None]
        s = jnp.where(mask, s, NEG_INF)
        p = jax.nn.softmax(s, axis=-1).astype(v.dtype)
        return jnp.einsum('bhqk,bkhd->bqhd', p, v)

    o = lax.map(one_block, (q_blocks, c_blocks, starts))
    o = o.transpose(1, 0, 2, 3, 4).reshape(bsz, seq, d)
    return o @ w_o


def s5_mixer(x, a_re, a_im, log_dt, b_re, b_im, c_re, c_im, d_skip, w_out):
    bsz, seq, d = x.shape
    u = x.reshape(bsz, seq, S5_GROUPS, S5_GROUP).astype(jnp.float32)
    lr = a_re.astype(jnp.float32)
    li = a_im.astype(jnp.float32)
    dt = jnp.exp(log_dt.astype(jnp.float32))[:, None]
    mag = jnp.exp(lr * dt)
    ang = li * dt
    lb_re = mag * jnp.cos(ang)
    lb_im = mag * jnp.sin(ang)
    den = lr * lr + li * li
    nr = lb_re - 1.0
    ni = lb_im
    z_re = (nr * lr + ni * li) / den
    z_im = (ni * lr - nr * li) / den
    br = b_re.astype(jnp.float32)
    bi = b_im.astype(jnp.float32)
    bb_re = z_re[..., None] * br - z_im[..., None] * bi
    bb_im = z_re[..., None] * bi + z_im[..., None] * br
    bu_re = jnp.einsum('bsgh,gph->bsgp', u, bb_re)
    bu_im = jnp.einsum('bsgh,gph->bsgp', u, bb_im)
    a_re_s = jnp.broadcast_to(lb_re[None, None], (1, seq, S5_GROUPS, S5_STATE))
    a_im_s = jnp.broadcast_to(lb_im[None, None], (1, seq, S5_GROUPS, S5_STATE))

    def combine(e1, e2):
        ar1, ai1, hr1, hi1 = e1
        ar2, ai2, hr2, hi2 = e2
        return (ar2 * ar1 - ai2 * ai1,
                ar2 * ai1 + ai2 * ar1,
                ar2 * hr1 - ai2 * hi1 + hr2,
                ar2 * hi1 + ai2 * hr1 + hi2)

    _, _, h_re, h_im = lax.associative_scan(combine, (a_re_s, a_im_s, bu_re, bu_im), axis=1)
    y = (jnp.einsum('ghp,bsgp->bsgh', c_re.astype(jnp.float32), h_re)
         - jnp.einsum('ghp,bsgp->bsgh', c_im.astype(jnp.float32), h_im)
         + d_skip.astype(jnp.float32) * u)
    y = jax.nn.gelu(y.reshape(bsz, seq, d)).astype(x.dtype)
    val, gate = jnp.split(y @ w_out, 2, axis=-1)
    return val * jax.nn.sigmoid(gate)


def _fwd_setup_inputs(seed: int = 0) -> dict:
    key = jax.random.key(seed)
    ks = jax.random.split(key, 32)
    f32 = jnp.float32
    D, F, H = D_MODEL, D_FF, FOX_HEADS
    G, P, GH = S5_GROUPS, S5_STATE, S5_GROUP

    def nrm(k, shape, fan_in, scale=1.0):
        return jax.random.normal(k, shape, f32) * (scale * fan_in ** -0.5)

    def gain(k, shape):
        return 1.0 + 0.02 * jax.random.normal(k, shape, f32)

    def bias(k, shape):
        return 0.02 * jax.random.normal(k, shape, f32)

    x = jax.random.normal(ks[0], (BATCH, SEQ, D), f32)
    ffn1_w_in = nrm(ks[1], (DEPTH, D, 2 * F), D)
    ffn1_w_out = nrm(ks[2], (DEPTH, F, D), F, BETA)
    ln1_g = gain(ks[3], (DEPTH, D))
    ln1_b = bias(ks[4], (DEPTH, D))
    lnm_g = gain(ks[5], (DEPTH, D))
    lnm_b = bias(ks[6], (DEPTH, D))
    ffn2_w_in = nrm(ks[7], (DEPTH, D, 2 * F), D)
    ffn2_w_out = nrm(ks[8], (DEPTH, F, D), F, BETA)
    ln2_g = gain(ks[9], (DEPTH, D))
    ln2_b = bias(ks[10], (DEPTH, D))

    fox_w_in = jnp.concatenate([
        nrm(ks[11], (N_LAYERS_A, D, 2 * D), D),
        nrm(ks[12], (N_LAYERS_A, D, D), D, BETA),
        nrm(ks[13], (N_LAYERS_A, D, H), D, 0.5),
    ], axis=-1)
    fox_b_f = (jnp.linspace(1.0, 6.0, H, dtype=f32)[None, :]
               + 0.1 * jax.random.normal(ks[14], (N_LAYERS_A, H), f32))
    fox_w_o = nrm(ks[15], (N_LAYERS_A, D, D), D, BETA)

    s5_a_re = -0.5 * jnp.exp(0.05 * jax.random.normal(ks[16], (N_LAYERS_B, G, P), f32))
    s5_a_im = (jnp.pi * jnp.arange(P, dtype=f32))[None, None, :] \
        + 1e-3 * jax.random.normal(ks[17], (N_LAYERS_B, G, P), f32)
    s5_log_dt = jax.random.uniform(ks[18], (N_LAYERS_B, G), f32,
                                   minval=math.log(1e-3), maxval=math.log(1e-1))
    s5_b_re = nrm(ks[19], (N_LAYERS_B, G, P, GH), 2 * GH)
    s5_b_im = nrm(ks[20], (N_LAYERS_B, G, P, GH), 2 * GH)
    s5_c_re = nrm(ks[21], (N_LAYERS_B, G, GH, P), 2 * P)
    s5_c_im = nrm(ks[22], (N_LAYERS_B, G, GH, P), 2 * P)
    s5_d = jax.random.normal(ks[23], (N_LAYERS_B, G, GH), f32)
    s5_w_out = jnp.concatenate([
        nrm(ks[24], (N_LAYERS_B, D, D), D, BETA),
        nrm(ks[25], (N_LAYERS_B, D, D), D),
    ], axis=-1)

    return {"x": x,
            "ffn1_w_in": ffn1_w_in, "ffn1_w_out": ffn1_w_out, "ln1_g": ln1_g, "ln1_b": ln1_b,
            "lnm_g": lnm_g, "lnm_b": lnm_b,
            "ffn2_w_in": ffn2_w_in, "ffn2_w_out": ffn2_w_out, "ln2_g": ln2_g, "ln2_b": ln2_b,
            "fox_w_in": fox_w_in, "fox_b_f": fox_b_f, "fox_w_o": fox_w_o,
            "s5_a_re": s5_a_re, "s5_a_im": s5_a_im, "s5_log_dt": s5_log_dt,
            "s5_b_re": s5_b_re, "s5_b_im": s5_b_im, "s5_c_re": s5_c_re, "s5_c_im": s5_c_im,
            "s5_d": s5_d, "s5_w_out": s5_w_out}


def _fwd_reference(x, ffn1_w_in, ffn1_w_out, ln1_g, ln1_b, lnm_g, lnm_b,
              ffn2_w_in, ffn2_w_out, ln2_g, ln2_b,
              fox_w_in, fox_b_f, fox_w_o,
              s5_a_re, s5_a_im, s5_log_dt, s5_b_re, s5_b_im, s5_c_re, s5_c_im,
              s5_d, s5_w_out):
    for i in range(DEPTH):
        x = layer_norm(ALPHA * x + 0.5 * swiglu(x, ffn1_w_in[i], ffn1_w_out[i]), ln1_g[i], ln1_b[i])
        j = i // N_MIXERS
        if i % N_MIXERS == 0:
            m = fox_attention(x, fox_w_in[j], fox_b_f[j], fox_w_o[j])
        else:
            m = s5_mixer(x, s5_a_re[j], s5_a_im[j], s5_log_dt[j], s5_b_re[j], s5_b_im[j],
                         s5_c_re[j], s5_c_im[j], s5_d[j], s5_w_out[j])
        x = layer_norm(ALPHA * x + m, lnm_g[i], lnm_b[i])
        x = layer_norm(ALPHA * x + 0.5 * swiglu(x, ffn2_w_in[i], ffn2_w_out[i]), ln2_g[i], ln2_b[i])
    return x


import jax as _jax
import jax.numpy as _jnp

TWIN_FORMAT = 'train_step'
FWD_PARAMS = ['x', 'ffn1_w_in', 'ffn1_w_out', 'ln1_g', 'ln1_b', 'lnm_g', 'lnm_b', 'ffn2_w_in', 'ffn2_w_out', 'ln2_g', 'ln2_b', 'fox_w_in', 'fox_b_f', 'fox_w_o', 's5_a_re', 's5_a_im', 's5_log_dt', 's5_b_re', 's5_b_im', 's5_c_re', 's5_c_im', 's5_d', 's5_w_out']
TWIN_WEIGHTS = ['ffn1_w_in', 'ffn1_w_out', 'ln1_g', 'ln1_b', 'lnm_g', 'lnm_b', 'ffn2_w_in', 'ffn2_w_out', 'ln2_g', 'ln2_b', 'fox_w_in', 'fox_b_f', 'fox_w_o', 's5_a_re', 's5_a_im', 's5_log_dt', 's5_b_re', 's5_b_im', 's5_c_re', 's5_c_im', 's5_d', 's5_w_out']
TWIN_DIFF_INPUT = 'x'
TWIN_INPUTS = ['x', 'ffn1_w_in', 'ffn1_w_out', 'ln1_g', 'ln1_b', 'lnm_g', 'lnm_b', 'ffn2_w_in', 'ffn2_w_out', 'ln2_g', 'ln2_b', 'fox_w_in', 'fox_b_f', 'fox_w_o', 's5_a_re', 's5_a_im', 's5_log_dt', 's5_b_re', 's5_b_im', 's5_c_re', 's5_c_im', 's5_d', 's5_w_out', 'loss_target', 'm_ffn1_w_in', 'm_ffn1_w_out', 'm_ln1_g', 'm_ln1_b', 'm_lnm_g', 'm_lnm_b', 'm_ffn2_w_in', 'm_ffn2_w_out', 'm_ln2_g', 'm_ln2_b', 'm_fox_w_in', 'm_fox_b_f', 'm_fox_w_o', 'm_s5_a_re', 'm_s5_a_im', 'm_s5_log_dt', 'm_s5_b_re', 'm_s5_b_im', 'm_s5_c_re', 'm_s5_c_im', 'm_s5_d', 'm_s5_w_out', 'v_ffn1_w_in', 'v_ffn1_w_out', 'v_ln1_g', 'v_ln1_b', 'v_lnm_g', 'v_lnm_b', 'v_ffn2_w_in', 'v_ffn2_w_out', 'v_ln2_g', 'v_ln2_b', 'v_fox_w_in', 'v_fox_b_f', 'v_fox_w_o', 'v_s5_a_re', 'v_s5_a_im', 'v_s5_log_dt', 'v_s5_b_re', 'v_s5_b_im', 'v_s5_c_re', 'v_s5_c_im', 'v_s5_d', 'v_s5_w_out']
TWIN_OUTPUTS = ['loss', 'grad_x', 'grad_ffn1_w_in', 'grad_ffn1_w_out', 'grad_ln1_g', 'grad_ln1_b', 'grad_lnm_g', 'grad_lnm_b', 'grad_ffn2_w_in', 'grad_ffn2_w_out', 'grad_ln2_g', 'grad_ln2_b', 'grad_fox_w_in', 'grad_fox_b_f', 'grad_fox_w_o', 'grad_s5_a_re', 'grad_s5_a_im', 'grad_s5_log_dt', 'grad_s5_b_re', 'grad_s5_b_im', 'grad_s5_c_re', 'grad_s5_c_im', 'grad_s5_d', 'grad_s5_w_out', 'delta_ffn1_w_in', 'delta_ffn1_w_out', 'delta_ln1_g', 'delta_ln1_b', 'delta_lnm_g', 'delta_lnm_b', 'delta_ffn2_w_in', 'delta_ffn2_w_out', 'delta_ln2_g', 'delta_ln2_b', 'delta_fox_w_in', 'delta_fox_b_f', 'delta_fox_w_o', 'delta_s5_a_re', 'delta_s5_a_im', 'delta_s5_log_dt', 'delta_s5_b_re', 'delta_s5_b_im', 'delta_s5_c_re', 'delta_s5_c_im', 'delta_s5_d', 'delta_s5_w_out', 'new_m_ffn1_w_in', 'new_m_ffn1_w_out', 'new_m_ln1_g', 'new_m_ln1_b', 'new_m_lnm_g', 'new_m_lnm_b', 'new_m_ffn2_w_in', 'new_m_ffn2_w_out', 'new_m_ln2_g', 'new_m_ln2_b', 'new_m_fox_w_in', 'new_m_fox_b_f', 'new_m_fox_w_o', 'new_m_s5_a_re', 'new_m_s5_a_im', 'new_m_s5_log_dt', 'new_m_s5_b_re', 'new_m_s5_b_im', 'new_m_s5_c_re', 'new_m_s5_c_im', 'new_m_s5_d', 'new_m_s5_w_out', 'new_v_ffn1_w_in', 'new_v_ffn1_w_out', 'new_v_ln1_g', 'new_v_ln1_b', 'new_v_lnm_g', 'new_v_lnm_b', 'new_v_ffn2_w_in', 'new_v_ffn2_w_out', 'new_v_ln2_g', 'new_v_ln2_b', 'new_v_fox_w_in', 'new_v_fox_b_f', 'new_v_fox_w_o', 'new_v_s5_a_re', 'new_v_s5_a_im', 'new_v_s5_log_dt', 'new_v_s5_b_re', 'new_v_s5_b_im', 'new_v_s5_c_re', 'new_v_s5_c_im', 'new_v_s5_d', 'new_v_s5_w_out']
TWIN_LEAF_KINDS = {'loss': 'loss', 'grad_x': 'grad_x', 'grad_ffn1_w_in': 'grad_w', 'grad_ffn1_w_out': 'grad_w', 'grad_ln1_g': 'grad_w', 'grad_ln1_b': 'grad_w', 'grad_lnm_g': 'grad_w', 'grad_lnm_b': 'grad_w', 'grad_ffn2_w_in': 'grad_w', 'grad_ffn2_w_out': 'grad_w', 'grad_ln2_g': 'grad_w', 'grad_ln2_b': 'grad_w', 'grad_fox_w_in': 'grad_w', 'grad_fox_b_f': 'grad_w', 'grad_fox_w_o': 'grad_w', 'grad_s5_a_re': 'grad_w', 'grad_s5_a_im': 'grad_w', 'grad_s5_log_dt': 'grad_w', 'grad_s5_b_re': 'grad_w', 'grad_s5_b_im': 'grad_w', 'grad_s5_c_re': 'grad_w', 'grad_s5_c_im': 'grad_w', 'grad_s5_d': 'grad_w', 'grad_s5_w_out': 'grad_w', 'delta_ffn1_w_in': 'delta_w', 'delta_ffn1_w_out': 'delta_w', 'delta_ln1_g': 'delta_w', 'delta_ln1_b': 'delta_w', 'delta_lnm_g': 'delta_w', 'delta_lnm_b': 'delta_w', 'delta_ffn2_w_in': 'delta_w', 'delta_ffn2_w_out': 'delta_w', 'delta_ln2_g': 'delta_w', 'delta_ln2_b': 'delta_w', 'delta_fox_w_in': 'delta_w', 'delta_fox_b_f': 'delta_w', 'delta_fox_w_o': 'delta_w', 'delta_s5_a_re': 'delta_w', 'delta_s5_a_im': 'delta_w', 'delta_s5_log_dt': 'delta_w', 'delta_s5_b_re': 'delta_w', 'delta_s5_b_im': 'delta_w', 'delta_s5_c_re': 'delta_w', 'delta_s5_c_im': 'delta_w', 'delta_s5_d': 'delta_w', 'delta_s5_w_out': 'delta_w', 'new_m_ffn1_w_in': 'new_m', 'new_m_ffn1_w_out': 'new_m', 'new_m_ln1_g': 'new_m', 'new_m_ln1_b': 'new_m', 'new_m_lnm_g': 'new_m', 'new_m_lnm_b': 'new_m', 'new_m_ffn2_w_in': 'new_m', 'new_m_ffn2_w_out': 'new_m', 'new_m_ln2_g': 'new_m', 'new_m_ln2_b': 'new_m', 'new_m_fox_w_in': 'new_m', 'new_m_fox_b_f': 'new_m', 'new_m_fox_w_o': 'new_m', 'new_m_s5_a_re': 'new_m', 'new_m_s5_a_im': 'new_m', 'new_m_s5_log_dt': 'new_m', 'new_m_s5_b_re': 'new_m', 'new_m_s5_b_im': 'new_m', 'new_m_s5_c_re': 'new_m', 'new_m_s5_c_im': 'new_m', 'new_m_s5_d': 'new_m', 'new_m_s5_w_out': 'new_m', 'new_v_ffn1_w_in': 'new_v', 'new_v_ffn1_w_out': 'new_v', 'new_v_ln1_g': 'new_v', 'new_v_ln1_b': 'new_v', 'new_v_lnm_g': 'new_v', 'new_v_lnm_b': 'new_v', 'new_v_ffn2_w_in': 'new_v', 'new_v_ffn2_w_out': 'new_v', 'new_v_ln2_g': 'new_v', 'new_v_ln2_b': 'new_v', 'new_v_fox_w_in': 'new_v', 'new_v_fox_b_f': 'new_v', 'new_v_fox_w_o': 'new_v', 'new_v_s5_a_re': 'new_v', 'new_v_s5_a_im': 'new_v', 'new_v_s5_log_dt': 'new_v', 'new_v_s5_b_re': 'new_v', 'new_v_s5_b_im': 'new_v', 'new_v_s5_c_re': 'new_v', 'new_v_s5_c_im': 'new_v', 'new_v_s5_d': 'new_v', 'new_v_s5_w_out': 'new_v'}


def _forward(args):
    return _fwd_reference(*[args[k] for k in FWD_PARAMS])


def _output_shape():
    def fwd():
        inp = _fwd_setup_inputs(0)
        return _fwd_reference(*[inp[k] for k in FWD_PARAMS])
    out = _jax.eval_shape(fwd)
    return out.shape, out.dtype

N_MICROBATCH = 1
ADAM_LR = 0.001
ADAM_B1 = 0.9
ADAM_B2 = 0.999
ADAM_EPS = 1e-08
ADAM_WD = 0.01
ADAM_STEP = 10
PER_EXAMPLE_BATCH_AXIS = {'x': 0, 'loss_target': 0}
SHARED_INPUTS = []
_WEIGHT_DTYPES = {'ffn1_w_in': _jnp.float32, 'ffn1_w_out': _jnp.float32, 'ln1_g': _jnp.float32, 'ln1_b': _jnp.float32, 'lnm_g': _jnp.float32, 'lnm_b': _jnp.float32, 'ffn2_w_in': _jnp.float32, 'ffn2_w_out': _jnp.float32, 'ln2_g': _jnp.float32, 'ln2_b': _jnp.float32, 'fox_w_in': _jnp.float32, 'fox_b_f': _jnp.float32, 'fox_w_o': _jnp.float32, 's5_a_re': _jnp.float32, 's5_a_im': _jnp.float32, 's5_log_dt': _jnp.float32, 's5_b_re': _jnp.float32, 's5_b_im': _jnp.float32, 's5_c_re': _jnp.float32, 's5_c_im': _jnp.float32, 's5_d': _jnp.float32, 's5_w_out': _jnp.float32}
MOMENT_SCALE = {'ffn1_w_in': 8.170821e-03, 'ffn1_w_out': 3.173540e-02, 'ln1_g': 1.207806e+00, 'ln1_b': 5.591849e-01, 'lnm_g': 1.215339e+00, 'lnm_b': 5.570332e-01, 'ffn2_w_in': 8.161408e-03, 'ffn2_w_out': 3.171535e-02, 'ln2_g': 1.615271e+01, 'ln2_b': 1.441108e+00, 'fox_w_in': 9.129353e-03, 'fox_b_f': 2.572587e-02, 'fox_w_o': 1.444380e-02, 's5_a_re': 9.450437e-04, 's5_a_im': 9.264944e-04, 's5_log_dt': 9.353399e-01, 's5_b_re': 5.713001e-04, 's5_b_im': 5.764562e-04, 's5_c_re': 1.146907e-03, 's5_c_im': 1.160994e-03, 's5_d': 2.606371e-02, 's5_w_out': 4.084557e-02}


def _to_microbatches(a, axis):
    t = _jnp.moveaxis(a, axis, 0)
    t = t.reshape((N_MICROBATCH, t.shape[0] // N_MICROBATCH) + t.shape[1:])
    return _jnp.moveaxis(t, 1, axis + 1)


def setup_inputs(seed: int = 0) -> dict:
    inp = _fwd_setup_inputs(seed)
    key = _jax.random.fold_in(_jax.random.key(seed), 7919)
    shape, _ = _output_shape()
    out = dict(inp)
    out["loss_target"] = _jax.random.normal(_jax.random.fold_in(key, 0), shape, _jnp.float32)
    for i, name in enumerate(TWIN_WEIGHTS):
        w = inp[name].astype(_jnp.float32)
        if MOMENT_SCALE is None:
            s = _jnp.sqrt(_jnp.mean(_jnp.square(w)) + 1e-30)
        else:
            s = MOMENT_SCALE[name]
        km, kv = _jax.random.split(_jax.random.fold_in(key, i + 1))
        out[name] = w
        out["m_" + name] = s * _jax.random.normal(km, w.shape, _jnp.float32)
        out["v_" + name] = (s * s) * _jax.random.uniform(kv, w.shape, _jnp.float32, 0.5, 1.5)
    if N_MICROBATCH > 1:
        for name, axis in PER_EXAMPLE_BATCH_AXIS.items():
            out[name] = _to_microbatches(out[name], axis)
    return {'x': out['x'], 'ffn1_w_in': out['ffn1_w_in'], 'ffn1_w_out': out['ffn1_w_out'], 'ln1_g': out['ln1_g'], 'ln1_b': out['ln1_b'], 'lnm_g': out['lnm_g'], 'lnm_b': out['lnm_b'], 'ffn2_w_in': out['ffn2_w_in'], 'ffn2_w_out': out['ffn2_w_out'], 'ln2_g': out['ln2_g'], 'ln2_b': out['ln2_b'], 'fox_w_in': out['fox_w_in'], 'fox_b_f': out['fox_b_f'], 'fox_w_o': out['fox_w_o'], 's5_a_re': out['s5_a_re'], 's5_a_im': out['s5_a_im'], 's5_log_dt': out['s5_log_dt'], 's5_b_re': out['s5_b_re'], 's5_b_im': out['s5_b_im'], 's5_c_re': out['s5_c_re'], 's5_c_im': out['s5_c_im'], 's5_d': out['s5_d'], 's5_w_out': out['s5_w_out'], 'loss_target': out['loss_target'], 'm_ffn1_w_in': out['m_ffn1_w_in'], 'm_ffn1_w_out': out['m_ffn1_w_out'], 'm_ln1_g': out['m_ln1_g'], 'm_ln1_b': out['m_ln1_b'], 'm_lnm_g': out['m_lnm_g'], 'm_lnm_b': out['m_lnm_b'], 'm_ffn2_w_in': out['m_ffn2_w_in'], 'm_ffn2_w_out': out['m_ffn2_w_out'], 'm_ln2_g': out['m_ln2_g'], 'm_ln2_b': out['m_ln2_b'], 'm_fox_w_in': out['m_fox_w_in'], 'm_fox_b_f': out['m_fox_b_f'], 'm_fox_w_o': out['m_fox_w_o'], 'm_s5_a_re': out['m_s5_a_re'], 'm_s5_a_im': out['m_s5_a_im'], 'm_s5_log_dt': out['m_s5_log_dt'], 'm_s5_b_re': out['m_s5_b_re'], 'm_s5_b_im': out['m_s5_b_im'], 'm_s5_c_re': out['m_s5_c_re'], 'm_s5_c_im': out['m_s5_c_im'], 'm_s5_d': out['m_s5_d'], 'm_s5_w_out': out['m_s5_w_out'], 'v_ffn1_w_in': out['v_ffn1_w_in'], 'v_ffn1_w_out': out['v_ffn1_w_out'], 'v_ln1_g': out['v_ln1_g'], 'v_ln1_b': out['v_ln1_b'], 'v_lnm_g': out['v_lnm_g'], 'v_lnm_b': out['v_lnm_b'], 'v_ffn2_w_in': out['v_ffn2_w_in'], 'v_ffn2_w_out': out['v_ffn2_w_out'], 'v_ln2_g': out['v_ln2_g'], 'v_ln2_b': out['v_ln2_b'], 'v_fox_w_in': out['v_fox_w_in'], 'v_fox_b_f': out['v_fox_b_f'], 'v_fox_w_o': out['v_fox_w_o'], 'v_s5_a_re': out['v_s5_a_re'], 'v_s5_a_im': out['v_s5_a_im'], 'v_s5_log_dt': out['v_s5_log_dt'], 'v_s5_b_re': out['v_s5_b_re'], 'v_s5_b_im': out['v_s5_b_im'], 'v_s5_c_re': out['v_s5_c_re'], 'v_s5_c_im': out['v_s5_c_im'], 'v_s5_d': out['v_s5_d'], 'v_s5_w_out': out['v_s5_w_out']}


def _loss(weights, diff, rest, loss_target):
    with _jax.named_scope("forward"):
        args = {**rest, TWIN_DIFF_INPUT: diff, **{k: w.astype(_WEIGHT_DTYPES[k]) for k, w in weights.items()}}
        y = _forward(args)
    with _jax.named_scope("loss_head"):
        err = _jnp.square(y.astype(_jnp.float32) - loss_target)
        return 0.5 * _jnp.sum(_jnp.mean(err, axis=-1)) if err.ndim else 0.5 * err


def _adamw(w, g, m, v):
    m = ADAM_B1 * m + (1.0 - ADAM_B1) * g
    v = ADAM_B2 * v + (1.0 - ADAM_B2) * _jnp.square(g)
    m_hat = m / (1.0 - ADAM_B1 ** ADAM_STEP)
    v_hat = v / (1.0 - ADAM_B2 ** ADAM_STEP)
    delta = -ADAM_LR * (m_hat / (_jnp.sqrt(v_hat) + ADAM_EPS) + ADAM_WD * w)
    return delta, m, v


def reference(x, ffn1_w_in, ffn1_w_out, ln1_g, ln1_b, lnm_g, lnm_b, ffn2_w_in, ffn2_w_out, ln2_g, ln2_b, fox_w_in, fox_b_f, fox_w_o, s5_a_re, s5_a_im, s5_log_dt, s5_b_re, s5_b_im, s5_c_re, s5_c_im, s5_d, s5_w_out, loss_target, m_ffn1_w_in, m_ffn1_w_out, m_ln1_g, m_ln1_b, m_lnm_g, m_lnm_b, m_ffn2_w_in, m_ffn2_w_out, m_ln2_g, m_ln2_b, m_fox_w_in, m_fox_b_f, m_fox_w_o, m_s5_a_re, m_s5_a_im, m_s5_log_dt, m_s5_b_re, m_s5_b_im, m_s5_c_re, m_s5_c_im, m_s5_d, m_s5_w_out, v_ffn1_w_in, v_ffn1_w_out, v_ln1_g, v_ln1_b, v_lnm_g, v_lnm_b, v_ffn2_w_in, v_ffn2_w_out, v_ln2_g, v_ln2_b, v_fox_w_in, v_fox_b_f, v_fox_w_o, v_s5_a_re, v_s5_a_im, v_s5_log_dt, v_s5_b_re, v_s5_b_im, v_s5_c_re, v_s5_c_im, v_s5_d, v_s5_w_out):
    given = dict(x=x, ffn1_w_in=ffn1_w_in, ffn1_w_out=ffn1_w_out, ln1_g=ln1_g, ln1_b=ln1_b, lnm_g=lnm_g, lnm_b=lnm_b, ffn2_w_in=ffn2_w_in, ffn2_w_out=ffn2_w_out, ln2_g=ln2_g, ln2_b=ln2_b, fox_w_in=fox_w_in, fox_b_f=fox_b_f, fox_w_o=fox_w_o, s5_a_re=s5_a_re, s5_a_im=s5_a_im, s5_log_dt=s5_log_dt, s5_b_re=s5_b_re, s5_b_im=s5_b_im, s5_c_re=s5_c_re, s5_c_im=s5_c_im, s5_d=s5_d, s5_w_out=s5_w_out, loss_target=loss_target, m_ffn1_w_in=m_ffn1_w_in, m_ffn1_w_out=m_ffn1_w_out, m_ln1_g=m_ln1_g, m_ln1_b=m_ln1_b, m_lnm_g=m_lnm_g, m_lnm_b=m_lnm_b, m_ffn2_w_in=m_ffn2_w_in, m_ffn2_w_out=m_ffn2_w_out, m_ln2_g=m_ln2_g, m_ln2_b=m_ln2_b, m_fox_w_in=m_fox_w_in, m_fox_b_f=m_fox_b_f, m_fox_w_o=m_fox_w_o, m_s5_a_re=m_s5_a_re, m_s5_a_im=m_s5_a_im, m_s5_log_dt=m_s5_log_dt, m_s5_b_re=m_s5_b_re, m_s5_b_im=m_s5_b_im, m_s5_c_re=m_s5_c_re, m_s5_c_im=m_s5_c_im, m_s5_d=m_s5_d, m_s5_w_out=m_s5_w_out, v_ffn1_w_in=v_ffn1_w_in, v_ffn1_w_out=v_ffn1_w_out, v_ln1_g=v_ln1_g, v_ln1_b=v_ln1_b, v_lnm_g=v_lnm_g, v_lnm_b=v_lnm_b, v_ffn2_w_in=v_ffn2_w_in, v_ffn2_w_out=v_ffn2_w_out, v_ln2_g=v_ln2_g, v_ln2_b=v_ln2_b, v_fox_w_in=v_fox_w_in, v_fox_b_f=v_fox_b_f, v_fox_w_o=v_fox_w_o, v_s5_a_re=v_s5_a_re, v_s5_a_im=v_s5_a_im, v_s5_log_dt=v_s5_log_dt, v_s5_b_re=v_s5_b_re, v_s5_b_im=v_s5_b_im, v_s5_c_re=v_s5_c_re, v_s5_c_im=v_s5_c_im, v_s5_d=v_s5_d, v_s5_w_out=v_s5_w_out)
    weights = {n: given[n] for n in TWIN_WEIGHTS}
    shared = {n: given[n] for n in SHARED_INPUTS}
    per_example = {n: given[n] for n in ['x']}
    grad_fn = _jax.value_and_grad(_loss, argnums=(0, 1))

    def one_microbatch(ex, loss_target):
        ex = dict(ex)
        diff = ex.pop(TWIN_DIFF_INPUT)
        return grad_fn(weights, diff, {**shared, **ex}, loss_target)

    if N_MICROBATCH == 1:
        loss, (grad_w, grad_x) = one_microbatch(per_example, given["loss_target"])
    else:
        def body(carry, xs):
            loss_sum, grad_sum = carry
            l_k, (gw_k, gx_k) = one_microbatch(xs[0], xs[1])
            with _jax.named_scope("update"):
                return (loss_sum + l_k, _jax.tree.map(_jnp.add, grad_sum, gw_k)), gx_k

        init = (_jnp.zeros((), _jnp.float32), _jax.tree.map(_jnp.zeros_like, weights))
        (loss, grad_w), grad_x = _jax.lax.scan(body, init, (per_example, given["loss_target"]))
    with _jax.named_scope("update"):
        delta_w, new_m, new_v = {}, {}, {}
        for n in TWIN_WEIGHTS:
            delta_w[n], new_m[n], new_v[n] = _adamw(weights[n], grad_w[n], given["m_" + n], given["v_" + n])
    return (loss, grad_x, *[grad_w[n] for n in TWIN_WEIGHTS], *[delta_w[n] for n in TWIN_WEIGHTS],
            *[new_m[n] for n in TWIN_WEIGHTS], *[new_v[n] for n in TWIN_WEIGHTS])
```

```python
import functools
import math

import jax
import jax.numpy as jnp
from jax import lax
from jax.experimental import pallas as pl
from jax.experimental.pallas import tpu as pltpu

F32 = jnp.float32
BF16 = jnp.bfloat16
LN_EPS = 1e-5
NEG_INF = -1e30
ADAM_LR = 0.001
ADAM_B1 = 0.9
ADAM_B2 = 0.999
ADAM_EPS = 1e-08
ADAM_WD = 0.01
ADAM_STEP = 10
S5_GROUP = 16
SCAN_SEGMENTS = 8
V7X_SUBLANES = 8
V7X_LANES = 128
VMEM_LIMIT = 56 * 1024 * 1024
N_CHIPS = 4
N_DEV = 8
MESH = pl.DeviceIdType.MESH
ANY = pl.BlockSpec(memory_space=pl.ANY)


def _cp(n_grid, kaxis=None):
    sem = tuple("arbitrary" if (kaxis is None or i == kaxis) else "parallel" for i in range(n_grid))
    return pltpu.CompilerParams(dimension_semantics=sem, vmem_limit_bytes=VMEM_LIMIT)


def _tile(n, pref, mult=V7X_SUBLANES):
    if n <= pref:
        return n
    for t in range(pref, 0, -1):
        if n % t == 0 and t % mult == 0:
            return t
    return n


_CONTRACT = {"nn": ((1,), (0,)), "nt": ((1,), (1,)), "tn": ((0,), (0,))}


def _mm(name, a, b, *, mode, grid, kaxis, a_blk, a_map, b_blk, b_map, o_shape, o_blk, o_map, o_dtype=F32, scale=None):
    nk = 1 if kaxis is None else grid[kaxis]
    assert kaxis is None or kaxis == len(grid) - 1
    dims = (_CONTRACT[mode], ((), ()))
    use_acc = nk > 1 and o_dtype != F32
    acc_shape = tuple(d for d in o_blk if d is not None)

    def body(a_ref, b_ref, o_ref, *scratch):
        p = lax.dot_general(a_ref[...].astype(BF16), b_ref[...].astype(BF16), dims, preferred_element_type=F32)
        if nk == 1:
            if scale is not None:
                p = p * scale
            o_ref[...] = p.astype(o_dtype)
            return
        acc = scratch[0] if use_acc else o_ref
        k = pl.program_id(kaxis)

        @pl.when(k == 0)
        def _():
            acc[...] = p

        @pl.when(k > 0)
        def _():
            acc[...] += p

        if use_acc or scale is not None:
            @pl.when(k == nk - 1)
            def _():
                r = acc[...]
                if scale is not None:
                    r = r * scale
                o_ref[...] = r.astype(o_dtype)

    return pl.pallas_call(
        body, grid=grid, name=name,
        in_specs=[pl.BlockSpec(a_blk, a_map), pl.BlockSpec(b_blk, b_map)],
        out_specs=pl.BlockSpec(o_blk, o_map),
        out_shape=jax.ShapeDtypeStruct(o_shape, o_dtype),
        scratch_shapes=[pltpu.VMEM(acc_shape, F32)] if use_acc else [],
        compiler_params=_cp(len(grid), kaxis),
    )(a, b)


def _mm_shards_nn(name, a, w4, o_dtype):
    t, k = a.shape
    s, _, n = w4.shape
    tm = _tile(t, 512)
    return _mm(name, a, w4, mode="nn", grid=(s, t // tm), kaxis=None,
               a_blk=(tm, k), a_map=lambda j, i: (i, 0),
               b_blk=(None, k, n), b_map=lambda j, i: (j, 0, 0),
               o_shape=(t, s * n), o_blk=(tm, n), o_map=lambda j, i: (i, j), o_dtype=o_dtype)


def _mm_shards_nt(name, g, w4):
    t = g.shape[0]
    s, k, n = w4.shape
    tm = _tile(t, 512)
    return _mm(name, g, w4, mode="nt", grid=(t // tm, s), kaxis=1,
               a_blk=(tm, n), a_map=lambda i, kk: (i, kk),
               b_blk=(None, k, n), b_map=lambda i, kk: (kk, 0, 0),
               o_shape=(t, k), o_blk=(tm, k), o_map=lambda i, kk: (i, 0))


def _mm_shards_tn(name, a, g, s):
    t, k = a.shape
    n = g.shape[1] // s
    tk = _tile(t, 512)
    return _mm(name, a, g, mode="tn", grid=(s, t // tk), kaxis=1,
               a_blk=(tk, k), a_map=lambda j, kk: (kk, 0),
               b_blk=(tk, n), b_map=lambda j, kk: (kk, j),
               o_shape=(s, k, n), o_blk=(None, k, n), o_map=lambda j, kk: (j, 0, 0))


def _mm_nn(name, a, w, o_dtype=F32, tn=None):
    t, k = a.shape
    n = w.shape[1]
    tm = _tile(t, 512)
    tn = n if tn is None else tn
    return _mm(name, a, w, mode="nn", grid=(n // tn, t // tm), kaxis=None,
               a_blk=(tm, k), a_map=lambda j, i: (i, 0),
               b_blk=(k, tn), b_map=lambda j, i: (0, j),
               o_shape=(t, n), o_blk=(tm, tn), o_map=lambda j, i: (i, j), o_dtype=o_dtype)


def _mm_nt(name, g, w, o_dtype=F32, tn=None, scale=None):
    t, k = g.shape
    n = w.shape[0]
    tm = _tile(t, 512)
    tn = n if tn is None else tn
    return _mm(name, g, w, mode="nt", grid=(n // tn, t // tm), kaxis=None,
               a_blk=(tm, k), a_map=lambda j, i: (i, 0),
               b_blk=(tn, k), b_map=lambda j, i: (j, 0),
               o_shape=(t, n), o_blk=(tm, tn), o_map=lambda j, i: (i, j), o_dtype=o_dtype, scale=scale)


def _mm_tn(name, a, g, tm=None, tn=None, scale=None):
    t, m = a.shape
    n = g.shape[1]
    tk = _tile(t, 512)
    tm = m if tm is None else tm
    tn = n if tn is None else tn
    return _mm(name, a, g, mode="tn", grid=(m // tm, n // tn, t // tk), kaxis=2,
               a_blk=(tk, tm), a_map=lambda i, j, kk: (kk, i),
               b_blk=(tk, tn), b_map=lambda i, j, kk: (kk, j),
               o_shape=(m, n), o_blk=(tm, tn), o_map=lambda i, j, kk: (i, j), scale=scale)


def _sigmoid(x):
    return 1.0 / (1.0 + jnp.exp(-x))


def _rows_call(name, body, t, tm, ins, in_cols, outs, acc_outs=()):
    in_specs = []
    for x, c in zip(ins, in_cols):
        if x.shape[0] == 1:
            in_specs.append(pl.BlockSpec((1, c), lambda i: (0, 0)))
        else:
            in_specs.append(pl.BlockSpec((tm, c), lambda i: (i, 0)))
    out_specs = [pl.BlockSpec((tm, s.shape[1]), lambda i: (i, 0)) for s in outs]
    out_specs += [pl.BlockSpec((1, s.shape[1]), lambda i: (0, 0)) for s in acc_outs]
    return pl.pallas_call(
        body, grid=(t // tm,), name=name, in_specs=in_specs, out_specs=out_specs,
        out_shape=list(outs) + list(acc_outs), compiler_params=_cp(1),
    )(*ins)


def _ln_fwd(name, alpha, x, r, coef, g, b):
    t, d = x.shape
    tm = _tile(t, 256)

    def body(x_ref, r_ref, g_ref, b_ref, y_ref, xh_ref, rs_ref):
        z = alpha * x_ref[...] + coef * r_ref[...]
        mu = jnp.mean(z, axis=-1, keepdims=True)
        zc = z - mu
        var = jnp.mean(zc * zc, axis=-1, keepdims=True)
        rstd = lax.rsqrt(var + LN_EPS)
        xh = zc * rstd
        y_ref[...] = xh * g_ref[...] + b_ref[...]
        xh_ref[...] = xh
        rs_ref[...] = rstd

    sd = jax.ShapeDtypeStruct
    return _rows_call(name, body, t, tm, [x, r, g, b], [d, d, d, d],
                      [sd((t, d), F32), sd((t, d), F32), sd((t, 1), F32)])


def _ln_bwd(name, terms, xhat, rstd, g):
    t, d = xhat.shape
    tm = _tile(t, 256)
    n = len(terms)
    coefs = [c for _, c in terms]

    def body(*refs):
        t_refs = refs[:n]
        xh_ref, rs_ref, g_ref, dz_ref, dg_ref, db_ref = refs[n:]
        dy = coefs[0] * t_refs[0][...]
        for c, r in zip(coefs[1:], t_refs[1:]):
            dy = dy + c * r[...]
        xh = xh_ref[...]
        dxh = dy * g_ref[...]
        m1 = jnp.mean(dxh, axis=-1, keepdims=True)
        m2 = jnp.mean(dxh * xh, axis=-1, keepdims=True)
        dz_ref[...] = rs_ref[...] * (dxh - m1 - xh * m2)
        pg = jnp.sum(dy * xh, axis=0, keepdims=True)
        pb = jnp.sum(dy, axis=0, keepdims=True)
        i = pl.program_id(0)

        @pl.when(i == 0)
        def _():
            dg_ref[...] = pg
            db_ref[...] = pb

        @pl.when(i > 0)
        def _():
            dg_ref[...] += pg
            db_ref[...] += pb

    sd = jax.ShapeDtypeStruct
    arrs = [a for a, _ in terms] + [xhat, rstd, g]
    cols = [d] * n + [d, 1, d]
    return _rows_call(name, body, t, tm, arrs, cols, [sd((t, d), F32)], [sd((1, d), F32), sd((1, d), F32)])


def _lincomb(name, terms):
    t, d = terms[0][0].shape
    tm = _tile(t, 256)
    coefs = [c for _, c in terms]
    n = len(terms)

    def body(*refs):
        acc = coefs[0] * refs[0][...]
        for c, r in zip(coefs[1:], refs[1:n]):
            acc = acc + c * r[...]
        refs[n][...] = acc

    return _rows_call(name, body, t, tm, [a for a, _ in terms], [d] * n, [jax.ShapeDtypeStruct((t, d), F32)])[0]


def _loss_sum(name, y, tgt):
    t, d = y.shape
    tm = _tile(t, 256)

    def body(y_ref, t_ref, o_ref):
        e = y_ref[...] - t_ref[...]
        s = jnp.sum(jnp.sum(e * e, axis=1, keepdims=True), axis=0, keepdims=True)
        i = pl.program_id(0)

        @pl.when(i == 0)
        def _():
            o_ref[...] = s

        @pl.when(i > 0)
        def _():
            o_ref[...] += s

    return _rows_call(name, body, t, tm, [y, tgt], [d, d], [], [jax.ShapeDtypeStruct((1, 1), F32)])[0]


def _swiglu_fwd(name, h):
    t, f2 = h.shape
    f = f2 // 2
    tm = _tile(t, 256)

    def body(h_ref, a_ref):
        g = h_ref[:, :f].astype(F32)
        u = h_ref[:, f:].astype(F32)
        a_ref[...] = (g * _sigmoid(g) * u).astype(BF16)

    return _rows_call(name, body, t, tm, [h], [f2], [jax.ShapeDtypeStruct((t, f), BF16)])[0]


def _swiglu_bwd(name, da, h):
    t, f2 = h.shape
    f = f2 // 2
    tm = _tile(t, 256)

    def body(da_ref, h_ref, dh_ref):
        g = h_ref[:, :f].astype(F32)
        u = h_ref[:, f:].astype(F32)
        d = da_ref[...].astype(F32)
        sg = _sigmoid(g)
        dh_ref[:, :f] = (d * u * sg * (1.0 + g * (1.0 - sg))).astype(BF16)
        dh_ref[:, f:] = (d * g * sg).astype(BF16)

    return _rows_call(name, body, t, tm, [da, h], [f, f2], [jax.ShapeDtypeStruct((t, f2), BF16)])[0]


_GELU_C = math.sqrt(2.0 / math.pi)


def _s5_act_fwd(name, ych, u, dvec):
    t, d = u.shape
    tm = _tile(t, 256)

    def body(y_ref, u_ref, d_ref, p_ref, a_ref):
        y = y_ref[...] + d_ref[...] * u_ref[...]
        p_ref[...] = y
        a_ref[...] = (0.5 * y * (1.0 + jnp.tanh(_GELU_C * (y + 0.044715 * y * y * y)))).astype(BF16)

    sd = jax.ShapeDtypeStruct
    return _rows_call(name, body, t, tm, [ych, u, dvec], [d, d, d], [sd((t, d), F32), sd((t, d), BF16)])


def _s5_act_bwd(name, dact, ypre, u, dvec):
    t, d = u.shape
    tm = _tile(t, 256)

    def body(da_ref, y_ref, u_ref, d_ref, dy_ref, ds_ref, dd_ref):
        y = y_ref[...]
        th = jnp.tanh(_GELU_C * (y + 0.044715 * y * y * y))
        dg = 0.5 * (1.0 + th) + 0.5 * y * (1.0 - th * th) * _GELU_C * (1.0 + 3.0 * 0.044715 * y * y)
        dy = da_ref[...] * dg
        dy_ref[...] = dy
        ds_ref[...] = dy * d_ref[...]
        pd = jnp.sum(dy * u_ref[...], axis=0, keepdims=True)
        i = pl.program_id(0)

        @pl.when(i == 0)
        def _():
            dd_ref[...] = pd

        @pl.when(i > 0)
        def _():
            dd_ref[...] += pd

    sd = jax.ShapeDtypeStruct
    return _rows_call(name, body, t, tm, [dact, ypre, u, dvec], [d, d, d, d],
                      [sd((t, d), F32), sd((t, d), F32)], [sd((1, d), F32)])


def _glu_fwd(name, vg):
    t, d2 = vg.shape
    d = d2 // 2
    tm = _tile(t, 256)

    def body(vg_ref, m_ref):
        m_ref[...] = vg_ref[:, :d] * _sigmoid(vg_ref[:, d:])

    return _rows_call(name, body, t, tm, [vg], [d2], [jax.ShapeDtypeStruct((t, d), F32)])[0]


def _glu_bwd(name, dm, vg):
    t, d2 = vg.shape
    d = d2 // 2
    tm = _tile(t, 256)

    def body(dm_ref, vg_ref, o_ref):
        sg = _sigmoid(vg_ref[:, d:])
        g = dm_ref[...]
        o_ref[:, :d] = (g * sg).astype(BF16)
        o_ref[:, d:] = (g * vg_ref[:, :d] * sg * (1.0 - sg)).astype(BF16)

    return _rows_call(name, body, t, tm, [dm, vg], [d, d2], [jax.ShapeDtypeStruct((t, d2), BF16)])[0]


def _adamw(name, w, g, m, v):
    r, c = w.shape
    tr = _tile(r, max(V7X_SUBLANES, (1 << 20) // (4 * c) // V7X_SUBLANES * V7X_SUBLANES))

    def body(w_ref, g_ref, m_ref, v_ref, d_ref, nm_ref, nv_ref):
        gg = g_ref[...]
        nm = ADAM_B1 * m_ref[...] + (1.0 - ADAM_B1) * gg
        nv = ADAM_B2 * v_ref[...] + (1.0 - ADAM_B2) * (gg * gg)
        m_hat = nm / (1.0 - ADAM_B1 ** ADAM_STEP)
        v_hat = nv / (1.0 - ADAM_B2 ** ADAM_STEP)
        d_ref[...] = -ADAM_LR * (m_hat / (jnp.sqrt(v_hat) + ADAM_EPS) + ADAM_WD * w_ref[...])
        nm_ref[...] = nm
        nv_ref[...] = nv

    sd = jax.ShapeDtypeStruct((r, c), F32)
    return _rows_call(name, body, r, tr, [w, g, m, v], [c] * 4, [sd, sd, sd])


def _sum_leading(name, a):
    n, r, c = a.shape
    tr = _tile(r, 512)

    def body(a_ref, o_ref):
        acc = a_ref[0]
        for k in range(1, n):
            acc = acc + a_ref[k]
        o_ref[...] = acc

    return pl.pallas_call(
        body, grid=(r // tr,), name=name,
        in_specs=[pl.BlockSpec((n, tr, c), lambda i: (0, i, 0))],
        out_specs=pl.BlockSpec((tr, c), lambda i: (i, 0)),
        out_shape=jax.ShapeDtypeStruct((r, c), F32), compiler_params=_cp(1),
    )(a)


def _split3(x):
    hi = x.astype(BF16)
    r1 = x - hi.astype(F32)
    mid = r1.astype(BF16)
    lo = (r1 - mid.astype(F32)).astype(BF16)
    return hi, mid, lo


def _tri_sum(tri, x):
    dims = (((1,), (0,)), ((), ()))
    hi, mid, lo = _split3(x)
    out = lax.dot_general(tri, lo, dims, preferred_element_type=F32)
    out = out + lax.dot_general(tri, mid, dims, preferred_element_type=F32)
    return out + lax.dot_general(tri, hi, dims, preferred_element_type=F32)


def _fox_cumsum(name, fl, bf):
    t, h = fl.shape
    tb = _tile(t, 512)

    def body(fl_ref, bf_ref, c_ref, carry):
        i = pl.program_id(0)

        @pl.when(i == 0)
        def _():
            carry[...] = jnp.zeros_like(carry)

        x = fl_ref[...] + bf_ref[...]
        lf = jnp.minimum(x, 0.0) - jnp.log(1.0 + jnp.exp(-jnp.abs(x)))
        row = lax.broadcasted_iota(jnp.int32, (tb, tb), 0)
        col = lax.broadcasted_iota(jnp.int32, (tb, tb), 1)
        tri = jnp.where(row >= col, 1.0, 0.0).astype(BF16)
        c_ref[...] = _tri_sum(tri, lf) + carry[...]
        carry[...] += jnp.sum(lf, axis=0, keepdims=True)

    return pl.pallas_call(
        body, grid=(t // tb,), name=name,
        in_specs=[pl.BlockSpec((tb, h), lambda i: (i, 0)), pl.BlockSpec((1, h), lambda i: (0, 0))],
        out_specs=pl.BlockSpec((tb, h), lambda i: (i, 0)),
        out_shape=jax.ShapeDtypeStruct((t, h), F32),
        scratch_shapes=[pltpu.VMEM((1, h), F32)], compiler_params=_cp(1),
    )(fl, bf)


def _fox_cumsum_bwd(name, dcum, fl, bf):
    t, h = fl.shape
    tb = _tile(t, 512)
    nb = t // tb

    def body(dc_ref, fl_ref, bf_ref, df_ref, db_ref, carry):
        i = pl.program_id(0)

        @pl.when(i == 0)
        def _():
            carry[...] = jnp.zeros_like(carry)

        dc = dc_ref[...]
        row = lax.broadcasted_iota(jnp.int32, (tb, tb), 0)
        col = lax.broadcasted_iota(jnp.int32, (tb, tb), 1)
        tri = jnp.where(row <= col, 1.0, 0.0).astype(BF16)
        dlf = _tri_sum(tri, dc) + carry[...]
        carry[...] += jnp.sum(dc, axis=0, keepdims=True)
        x = fl_ref[...] + bf_ref[...]
        df = dlf / (1.0 + jnp.exp(x))
        df_ref[...] = df
        pb = jnp.sum(df, axis=0, keepdims=True)

        @pl.when(i == 0)
        def _():
            db_ref[...] = pb

        @pl.when(i > 0)
        def _():
            db_ref[...] += pb

    rev = lambda i: (nb - 1 - i, 0)
    return pl.pallas_call(
        body, grid=(nb,), name=name,
        in_specs=[pl.BlockSpec((tb, h), rev), pl.BlockSpec((tb, h), rev), pl.BlockSpec((1, h), lambda i: (0, 0))],
        out_specs=[pl.BlockSpec((tb, h), rev), pl.BlockSpec((1, h), lambda i: (0, 0))],
        out_shape=[jax.ShapeDtypeStruct((t, h), F32), jax.ShapeDtypeStruct((1, h), F32)],
        scratch_shapes=[pltpu.VMEM((1, h), F32)], compiler_params=_cp(1),
    )(dcum, fl, bf)


_NT = (((1,), (1,)), ((), ()))
_TN = (((0,), (0,)), ((), ()))
_NN = (((1,), (0,)), ((), ()))


def _causal_mask(s, i, j, tq, tk):
    row = lax.broadcasted_iota(jnp.int32, (tq, tk), 0) + i * tq
    col = lax.broadcasted_iota(jnp.int32, (tq, tk), 1) + j * tk
    return jnp.where(col <= row, s, NEG_INF)


def _attn_fwd(name, q, k, v, ccol, crow):
    nh, nb, tb, hd = q.shape

    def body(q_ref, k_ref, v_ref, cc_ref, cr_ref, o_ref, lse_ref):
        i = pl.program_id(1)
        qi = q_ref[...]
        cc = cc_ref[...]

        def step(j, carry):
            m, l, acc = carry
            s = lax.dot_general(qi, k_ref[j], _NT, preferred_element_type=F32) + cc - cr_ref[j]
            s = _causal_mask(s, i, j, tb, tb)
            m_new = jnp.maximum(m, jnp.max(s, axis=1, keepdims=True))
            p = jnp.exp(s - m_new)
            a = jnp.exp(m - m_new)
            l = a * l + jnp.sum(p, axis=1, keepdims=True)
            acc = a * acc + lax.dot_general(p.astype(BF16), v_ref[j], _NN, preferred_element_type=F32)
            return m_new, l, acc

        init = (jnp.full((tb, 1), NEG_INF, F32), jnp.zeros((tb, 1), F32), jnp.zeros((tb, hd), F32))
        m, l, acc = lax.fori_loop(0, i + 1, step, init)
        o_ref[...] = acc / l
        lse_ref[...] = m + jnp.log(l)

    blk = lambda *s: pl.BlockSpec(s, lambda h, i: (h, i, 0, 0))
    head = lambda *s: pl.BlockSpec(s, lambda h, i: (h, 0, 0, 0))
    return pl.pallas_call(
        body, grid=(nh, nb), name=name,
        in_specs=[blk(None, None, tb, hd), head(None, nb, tb, hd), head(None, nb, tb, hd),
                  blk(None, None, tb, 1), head(None, nb, 1, tb)],
        out_specs=[blk(None, None, tb, hd), blk(None, None, tb, 1)],
        out_shape=[jax.ShapeDtypeStruct((nh, nb, tb, hd), F32), jax.ShapeDtypeStruct((nh, nb, tb, 1), F32)],
        compiler_params=_cp(2),
    )(q, k, v, ccol, crow)


def _attn_bwd(name, q, k, v, ccol, crow, o, lse, do, scale):
    nh, nb, tb, hd = q.shape

    def body(q_ref, k_ref, v_ref, cc_ref, cr_ref, o_ref, lse_ref, do_ref, dq_ref, dk_ref, dv_ref, dr_ref, dc_ref):
        j = pl.program_id(1)

        @pl.when(j == 0)
        def _():
            dq_ref[...] = jnp.zeros_like(dq_ref)
            dr_ref[...] = jnp.zeros_like(dr_ref)

        kj = k_ref[...]
        vj = v_ref[...]
        cr = cr_ref[...]

        def step(i, carry):
            dk, dv, dc = carry
            qi = q_ref[i]
            doi = do_ref[i]
            dob = doi.astype(BF16)
            di = jnp.sum(doi * o_ref[i], axis=1, keepdims=True)
            s = lax.dot_general(qi, kj, _NT, preferred_element_type=F32) + cc_ref[i] - cr
            s = _causal_mask(s, i, j, tb, tb)
            p = jnp.exp(s - lse_ref[i])
            dv = dv + lax.dot_general(p.astype(BF16), dob, _TN, preferred_element_type=F32)
            dp = lax.dot_general(dob, vj, _NT, preferred_element_type=F32)
            ds = p * (dp - di)
            dsb = ds.astype(BF16)
            dk = dk + lax.dot_general(dsb, qi, _TN, preferred_element_type=F32)
            dq_ref[i] += lax.dot_general(dsb, kj, _NN, preferred_element_type=F32) * scale
            dr_ref[i] += jnp.sum(ds, axis=1, keepdims=True)
            dc = dc + jnp.sum(ds, axis=0, keepdims=True)
            return dk, dv, dc

        init = (jnp.zeros((tb, hd), F32), jnp.zeros((tb, hd), F32), jnp.zeros((1, tb), F32))
        dk, dv, dc = lax.fori_loop(j, nb, step, init)
        dk_ref[...] = dk
        dv_ref[...] = dv
        dc_ref[...] = dc

    blk = lambda *s: pl.BlockSpec(s, lambda h, j: (h, j, 0, 0))
    head = lambda *s: pl.BlockSpec(s, lambda h, j: (h, 0, 0, 0))
    sd = jax.ShapeDtypeStruct
    return pl.pallas_call(
        body, grid=(nh, nb), name=name,
        in_specs=[head(None, nb, tb, hd), blk(None, None, tb, hd), blk(None, None, tb, hd),
                  head(None, nb, tb, 1), blk(None, None, 1, tb),
                  head(None, nb, tb, hd), head(None, nb, tb, 1), head(None, nb, tb, hd)],
        out_specs=[head(None, nb, tb, hd), blk(None, None, tb, hd), blk(None, None, tb, hd),
                   head(None, nb, tb, 1), blk(None, None, 1, tb)],
        out_shape=[sd((nh, nb, tb, hd), F32), sd((nh, nb, tb, hd), F32), sd((nh, nb, tb, hd), F32),
                   sd((nh, nb, tb, 1), F32), sd((nh, nb, 1, tb), F32)],
        compiler_params=_cp(2),
    )(q, k, v, ccol, crow, o, lse, do)


def _cmul(ar, ai, br, bi):
    return ar * br - ai * bi, ar * bi + ai * br


def _s5_scan(name, lam, xin, hs=None):
    reverse = hs is not None
    _, seg, ns, w = xin.shape
    assert ns == SCAN_SEGMENTS
    wb = V7X_LANES
    nsq = seg.bit_length() - 1
    assert (1 << nsq) == seg

    def body(*refs):
        if reverse:
            lam_ref, x_ref, h_ref, o_ref, dl_ref = refs
        else:
            lam_ref, x_ref, o_ref = refs
        lr = jnp.broadcast_to(lam_ref[0], (ns, wb))
        li = jnp.broadcast_to(lam_ref[1], (ns, wb))
        if reverse:
            li = -li
        zero = jnp.zeros((ns, wb), F32)
        at = (lambda n: seg - 1 - n) if reverse else (lambda n: n)

        def local(n, c):
            r = at(n)
            mr, mi = _cmul(lr, li, c[0], c[1])
            nr = mr + x_ref[0, r]
            ni = mi + x_ref[1, r]
            o_ref[0, r] = nr
            o_ref[1, r] = ni
            return nr, ni

        er, ei = lax.fori_loop(0, seg, local, (zero, zero))
        pr, pi = lr, li
        for _ in range(nsq):
            pr, pi = _cmul(pr, pi, pr, pi)
        sub = lax.broadcasted_iota(jnp.int32, (ns, wb), 0)

        def shifted(a, sh):
            if reverse:
                return jnp.where(sub < ns - sh, pltpu.roll(a, ns - sh, 0), 0.0)
            return jnp.where(sub >= sh, pltpu.roll(a, sh, 0), 0.0)

        xr, xi = er, ei
        sh = 1
        while sh < ns:
            tr, ti = _cmul(pr, pi, shifted(xr, sh), shifted(xi, sh))
            xr, xi = xr + tr, xi + ti
            pr, pi = _cmul(pr, pi, pr, pi)
            sh *= 2
        cr, ci = shifted(xr, 1), shifted(xi, 1)

        def fix(r, q):
            tr, ti = _cmul(q[0], q[1], cr, ci)
            gr = o_ref[0, r] + tr
            gi = o_ref[1, r] + ti
            o_ref[0, r] = gr
            o_ref[1, r] = gi
            return gr, gi

        if not reverse:
            def fixup(n, q):
                fix(n, q)
                return _cmul(q[0], q[1], lr, li)

            lax.fori_loop(0, seg, fixup, (lr, li))
            return

        def fixup_acc(n, c):
            qr, qi, ar, ai = c
            r = seg - 1 - n
            gr, gi = fix(r, (qr, qi))
            hr = h_ref[0, r - 1]
            hi = h_ref[1, r - 1]
            qr, qi = _cmul(qr, qi, lr, li)
            return qr, qi, ar + gr * hr + gi * hi, ai + gi * hr - gr * hi

        qr, qi, ar, ai = lax.fori_loop(0, seg - 1, fixup_acc, (lr, li, zero, zero))
        gr, gi = fix(0, (qr, qi))
        hr = jnp.where(sub >= 1, pltpu.roll(h_ref[0, seg - 1], 1, 0), 0.0)
        hi = jnp.where(sub >= 1, pltpu.roll(h_ref[1, seg - 1], 1, 0), 0.0)
        dl_ref[0] = ar + gr * hr + gi * hi
        dl_ref[1] = ai + gi * hr - gr * hi

    big = pl.BlockSpec((2, seg, ns, wb), lambda j: (0, 0, 0, j))
    lam_spec = pl.BlockSpec((2, 1, wb), lambda j: (0, 0, j))
    sd = jax.ShapeDtypeStruct
    if reverse:
        return pl.pallas_call(
            body, grid=(w // wb,), name=name, in_specs=[lam_spec, big, big],
            out_specs=[big, pl.BlockSpec((2, ns, wb), lambda j: (0, 0, j))],
            out_shape=[sd(xin.shape, F32), sd((2, ns, w), F32)], compiler_params=_cp(1),
        )(lam, xin, hs)
    return pl.pallas_call(
        body, grid=(w // wb,), name=name, in_specs=[lam_spec, big], out_specs=big,
        out_shape=sd(xin.shape, F32), compiler_params=_cp(1),
    )(lam, xin)


def _place():
    x, y, c = lax.axis_index("x"), lax.axis_index("y"), lax.axis_index("c")
    chips = [(1 - x, y), (x, 1 - y), (1 - x, 1 - y)]
    return x, y, c, chips


def _comm_params():
    return pltpu.CompilerParams(vmem_limit_bytes=VMEM_LIMIT)


def _gather_shards(name, ws):
    n = len(ws)

    def body(*refs):
        ins, outs = refs[:n], refs[n:2 * n]
        send_sems, recv_sems, local_sems = refs[2 * n:]
        x, y, c, chips = _place()
        my = 2 * x + y
        sibling = (x, y, 1 - c)

        def rows(t, half):
            h = ws[t].shape[0] // 2
            return pl.ds(half * h, h)

        def copy(t, k, src, dst, to):
            return pltpu.make_async_remote_copy(src_ref=src, dst_ref=dst, send_sem=send_sems.at[t, k],
                                                recv_sem=recv_sems.at[t, k], device_id=to, device_id_type=MESH)

        local = [pltpu.make_async_copy(ins[t], outs[t].at[my], local_sems.at[t]) for t in range(n)]
        for cp in local:
            cp.start()
        sent = []
        for t in range(n):
            for k, chip in enumerate(chips):
                sent.append(copy(t, k, ins[t].at[rows(t, c)], outs[t].at[my, rows(t, c)], (*chip, c)))
                sent[-1].start()
        for k, chip in enumerate(chips):
            shard = 2 * chip[0] + chip[1]
            for t in range(n):
                landed = outs[t].at[shard, rows(t, c)]
                copy(t, k, landed, landed, (*chip, c)).wait_recv()
                sent.append(copy(t, 3 + k, landed, landed, sibling))
                sent[-1].start()
        for k, chip in enumerate(chips):
            shard = 2 * chip[0] + chip[1]
            for t in range(n):
                other = outs[t].at[shard, rows(t, 1 - c)]
                copy(t, 3 + k, other, other, sibling).wait_recv()
        for cp in sent:
            cp.wait_send()
        for cp in local:
            cp.wait()

    return pl.pallas_call(
        body, name=name, in_specs=[ANY] * n, out_specs=[ANY] * n,
        out_shape=[jax.ShapeDtypeStruct((N_CHIPS,) + w.shape, w.dtype) for w in ws],
        scratch_shapes=[pltpu.SemaphoreType.DMA((n, 6)), pltpu.SemaphoreType.DMA((n, 6)), pltpu.SemaphoreType.DMA((n,))],
        compiler_params=_comm_params(),
    )(*ws)


def _swap_halves(name, gs):
    n = len(gs)

    def body(*refs):
        ins, outs = refs[:n], refs[n:2 * n]
        send_sems, recv_sems = refs[2 * n:]
        x, y, c, _ = _place()
        cps = []
        for t in range(n):
            h = gs[t].shape[1] // 2
            cps.append(pltpu.make_async_remote_copy(
                src_ref=ins[t].at[:, pl.ds((1 - c) * h, h), :], dst_ref=outs[t], send_sem=send_sems.at[t],
                recv_sem=recv_sems.at[t], device_id=(x, y, 1 - c), device_id_type=MESH))
            cps[-1].start()
        for cp in cps:
            cp.wait()

    return pl.pallas_call(
        body, name=name, in_specs=[ANY] * n, out_specs=[ANY] * n,
        out_shape=[jax.ShapeDtypeStruct((g.shape[0], g.shape[1] // 2, g.shape[2]), g.dtype) for g in gs],
        scratch_shapes=[pltpu.SemaphoreType.DMA((n,)), pltpu.SemaphoreType.DMA((n,))],
        compiler_params=_comm_params(),
    )(*gs)


def _scatter_partials(name, ps):
    n = len(ps)

    def body(*refs):
        ins, outs = refs[:n], refs[n:2 * n]
        send_sems, recv_sems = refs[2 * n:]
        x, y, c, chips = _place()
        cps = []
        for t in range(n):
            for k, chip in enumerate(chips):
                shard = 2 * chip[0] + chip[1]
                cps.append(pltpu.make_async_remote_copy(
                    src_ref=ins[t].at[shard], dst_ref=outs[t].at[k], send_sem=send_sems.at[t, k],
                    recv_sem=recv_sems.at[t, k], device_id=(*chip, c), device_id_type=MESH))
                cps[-1].start()
        for cp in cps:
            cp.wait()

    return pl.pallas_call(
        body, name=name, in_specs=[ANY] * n, out_specs=[ANY] * n,
        out_shape=[jax.ShapeDtypeStruct((3,) + p.shape[1:], p.dtype) for p in ps],
        scratch_shapes=[pltpu.SemaphoreType.DMA((n, 3)), pltpu.SemaphoreType.DMA((n, 3))],
        compiler_params=_comm_params(),
    )(*ps)


def _join_halves(name, fs):
    n = len(fs)

    def body(*refs):
        ins, outs = refs[:n], refs[n:2 * n]
        send_sems, recv_sems, local_sems = refs[2 * n:]
        x, y, c, _ = _place()
        cps, local = [], []
        for t in range(n):
            h = fs[t].shape[0]
            mine = outs[t].at[pl.ds(c * h, h)]
            local.append(pltpu.make_async_copy(ins[t], mine, local_sems.at[t]))
            local[-1].start()
            cps.append(pltpu.make_async_remote_copy(
                src_ref=ins[t], dst_ref=mine, send_sem=send_sems.at[t], recv_sem=recv_sems.at[t],
                device_id=(x, y, 1 - c), device_id_type=MESH))
            cps[-1].start()
        for t in range(n):
            h = fs[t].shape[0]
            theirs = outs[t].at[pl.ds((1 - c) * h, h)]
            pltpu.make_async_remote_copy(
                src_ref=ins[t], dst_ref=theirs, send_sem=send_sems.at[t], recv_sem=recv_sems.at[t],
                device_id=(x, y, 1 - c), device_id_type=MESH).wait_recv()
        for cp in cps:
            cp.wait_send()
        for cp in local:
            cp.wait()

    return pl.pallas_call(
        body, name=name, in_specs=[ANY] * n, out_specs=[ANY] * n,
        out_shape=[jax.ShapeDtypeStruct((2 * f.shape[0],) + f.shape[1:], f.dtype) for f in fs],
        scratch_shapes=[pltpu.SemaphoreType.DMA((n,)), pltpu.SemaphoreType.DMA((n,)), pltpu.SemaphoreType.DMA((n,))],
        compiler_params=_comm_params(),
    )(*fs)


def _gather_all(name, v):
    m_per = v.shape[0]

    def body(x_ref, out_ref, send_sems, recv_sems, local_sem):
        x, y, c, chips = _place()
        me, sibling = (x, y, c), (x, y, 1 - c)

        def rows(px, py, pc):
            return out_ref.at[pl.ds((4 * px + 2 * py + pc) * m_per, m_per), :]

        def copy(k, block, to, src=None):
            return pltpu.make_async_remote_copy(
                src_ref=rows(*block) if src is None else src, dst_ref=rows(*block), send_sem=send_sems.at[k],
                recv_sem=recv_sems.at[k], device_id=to, device_id_type=MESH)

        mine = pltpu.make_async_copy(x_ref, rows(*me), local_sem)
        mine.start()
        first = [copy(0, me, sibling, src=x_ref)]
        first += [copy(1 + j, me, (*chip, c), src=x_ref) for j, chip in enumerate(chips)]
        for cp in first:
            cp.start()
        passed = [copy(4 + j, (*chip, c), sibling) for j, chip in enumerate(chips)]
        for j, chip in enumerate(chips):
            copy(1 + j, (*chip, c), me).wait_recv()
            passed[j].start()
        copy(0, sibling, me).wait_recv()
        for j, chip in enumerate(chips):
            copy(4 + j, (*chip, 1 - c), me).wait_recv()
        for cp in first + passed:
            cp.wait_send()
        mine.wait()

    return pl.pallas_call(
        body, name=name, in_specs=[ANY], out_specs=ANY,
        out_shape=jax.ShapeDtypeStruct((N_DEV * m_per, v.shape[1]), v.dtype),
        scratch_shapes=[pltpu.SemaphoreType.DMA((7,)), pltpu.SemaphoreType.DMA((7,)), pltpu.SemaphoreType.DMA],
        compiler_params=_comm_params(),
    )(v)


def _pair_sum(name, g, recv, place):
    s, r, c = g.shape
    h = r // 2
    tr = _tile(h, max(V7X_SUBLANES * 2, (1 << 20) // (4 * c) // 16 * 16), 16)
    nb = h // tr

    def body(pref, g_ref, r_ref, p_ref, pb_ref):
        v = g_ref[...] + r_ref[...]
        p_ref[...] = v
        pb_ref[...] = v.astype(BF16)

    spec = pl.BlockSpec((None, tr, c), lambda k, i, pref: (k, i, 0))
    return pl.pallas_call(
        body, name=name,
        grid_spec=pltpu.PrefetchScalarGridSpec(
            num_scalar_prefetch=1, grid=(s, nb),
            in_specs=[pl.BlockSpec((None, tr, c), lambda k, i, pref: (k, pref[1] * nb + i, 0)), spec],
            out_specs=[spec, spec]),
        out_shape=[jax.ShapeDtypeStruct((s, h, c), F32), jax.ShapeDtypeStruct((s, h, c), BF16)],
        compiler_params=_cp(2),
    )(place, g, recv)


def _chip_sum(name, p, recv, place):
    s, h, c = p.shape
    tr = _tile(h, max(V7X_SUBLANES * 2, (1 << 20) // (4 * c) // 16 * 16), 16)

    def body(pref, p_ref, r_ref, o_ref):
        acc = p_ref[...]
        for k in range(3):
            acc = acc + r_ref[k].astype(F32)
        o_ref[...] = acc

    return pl.pallas_call(
        body, name=name,
        grid_spec=pltpu.PrefetchScalarGridSpec(
            num_scalar_prefetch=1, grid=(h // tr,),
            in_specs=[pl.BlockSpec((None, tr, c), lambda i, pref: (pref[0], i, 0)),
                      pl.BlockSpec((3, tr, c), lambda i, pref: (0, i, 0))],
            out_specs=pl.BlockSpec((tr, c), lambda i, pref: (i, 0))),
        out_shape=jax.ShapeDtypeStruct((h, c), F32),
        compiler_params=_cp(1),
    )(place, p, recv)


def _ffn_fwd(tag, alpha, x, w_in4, w_out, g, b):
    h = _mm_shards_nn(f"{tag}_in", x, w_in4, BF16)
    a = _swiglu_fwd(f"{tag}_act", h)
    f = _mm_nn(f"{tag}_out", a, w_out)
    y, xhat, rstd = _ln_fwd(f"{tag}_ln", alpha, x, f, 0.5, g, b)
    return y, (x, h, a, xhat, rstd)


def _ffn_bwd(tag, alpha, terms, saved, w_in4, w_out, g):
    x, h, a, xhat, rstd = saved
    dz, dg, db = _ln_bwd(f"{tag}_ln_bwd", terms, xhat, rstd, g)
    d_wout = _mm_tn(f"{tag}_dwout", a, dz, tm=_tile(a.shape[1], 1408, V7X_LANES), scale=0.5)
    da = _mm_nt(f"{tag}_da", dz, w_out, o_dtype=BF16, tn=_tile(w_out.shape[0], 1408, V7X_LANES), scale=0.5)
    dh = _swiglu_bwd(f"{tag}_act_bwd", da, h)
    d_win4 = _mm_shards_tn(f"{tag}_dwin", x, dh, w_in4.shape[0])
    dx = _mm_shards_nt(f"{tag}_dx", dh, w_in4)
    return [(dz, alpha), (dx, 1.0)], d_win4, d_wout, dg, db


def _heads(a, nh, tb):
    t = a.shape[0]
    return a.reshape(t, nh, -1).transpose(1, 0, 2).reshape(nh, t // tb, tb, -1)


def _unheads(a):
    nh, nb, tb, hd = a.shape
    return a.reshape(nh, nb * tb, hd).transpose(1, 0, 2).reshape(nb * tb, nh * hd)


def _fox_fwd(tag, x, w_pad, bf, w_o):
    t, d = x.shape
    nh = bf.shape[1]
    hd = d // nh
    tb = _tile(t, 512)
    scale = 1.0 / math.sqrt(hd)
    proj = _mm_nn(f"{tag}_proj", x, w_pad, tn=_tile(w_pad.shape[1], 640, V7X_LANES))
    q = _heads((proj[:, :d].astype(BF16) * scale).astype(BF16), nh, tb)
    k = _heads(proj[:, d:2 * d].astype(BF16), nh, tb)
    v = _heads(proj[:, 2 * d:3 * d].astype(BF16), nh, tb)
    fl = proj[:, 3 * d:3 * d + nh]
    cum = _fox_cumsum(f"{tag}_cum", fl, bf)
    ccol = cum.T.reshape(nh, t // tb, tb, 1)
    crow = cum.T.reshape(nh, t // tb, 1, tb)
    o, lse = _attn_fwd(f"{tag}_attn", q, k, v, ccol, crow)
    o2 = _unheads(o)
    m = _mm_nn(f"{tag}_oproj", o2, w_o)
    return m, (x, q, k, v, ccol, crow, o, lse, o2, fl, scale)


def _fox_bwd(tag, dm, saved, w_pad, bf, w_o):
    x, q, k, v, ccol, crow, o, lse, o2, fl, scale = saved
    t, d = x.shape
    nh, nb, tb, hd = q.shape
    d_wo = _mm_tn(f"{tag}_dwo", o2, dm)
    do2 = _mm_nt(f"{tag}_do", dm, w_o)
    do = _heads(do2, nh, tb)
    dq, dk, dv, drow, dcol = _attn_bwd(f"{tag}_attn_bwd", q, k, v, ccol, crow, o, lse, do, scale)
    dcum = (drow.reshape(nh, t) - dcol.reshape(nh, t)).T
    dfl, dbf = _fox_cumsum_bwd(f"{tag}_cum_bwd", dcum, fl, bf)
    pad = w_pad.shape[1] - 3 * d - nh
    dproj = jnp.concatenate([_unheads(dq).astype(BF16), _unheads(dk).astype(BF16), _unheads(dv).astype(BF16),
                             dfl.astype(BF16), jnp.zeros((t, pad), BF16)], axis=1)
    d_wpad = _mm_tn(f"{tag}_dwin", x, dproj, tn=_tile(w_pad.shape[1], 640, V7X_LANES))
    dx = _mm_nt(f"{tag}_dx", dproj, w_pad)
    return dx, d_wpad, dbf, d_wo


def _to_segments(a):
    t, d = a.shape
    return a.reshape(SCAN_SEGMENTS, t // SCAN_SEGMENTS, d).transpose(1, 0, 2).reshape(t, d)


def _from_segments(a):
    t, d = a.shape
    return a.reshape(t // SCAN_SEGMENTS, SCAN_SEGMENTS, d).transpose(1, 0, 2).reshape(t, d)


def _s5_discretise(a_re, a_im, log_dt, b_re, b_im):
    dt = jnp.exp(log_dt)[:, None]
    mag = jnp.exp(a_re * dt)
    ang = a_im * dt
    lb_re = mag * jnp.cos(ang)
    lb_im = mag * jnp.sin(ang)
    den = a_re * a_re + a_im * a_im
    nr = lb_re - 1.0
    ni = lb_im
    z_re = (nr * a_re + ni * a_im) / den
    z_im = (ni * a_re - nr * a_im) / den
    bb_re = z_re[..., None] * b_re - z_im[..., None] * b_im
    bb_im = z_re[..., None] * b_im + z_im[..., None] * b_re
    return lb_re, lb_im, bb_re, bb_im


S5_BLOCK_GROUPS = 8


def _blockdiag_in(bb):
    g, p, h = bb.shape
    e = jnp.eye(S5_BLOCK_GROUPS, dtype=bb.dtype)
    b4 = bb.reshape(g // S5_BLOCK_GROUPS, S5_BLOCK_GROUPS, p, h)
    return jnp.einsum("jgph,gf->jghfp", b4, e).reshape(g // S5_BLOCK_GROUPS, S5_BLOCK_GROUPS * h, S5_BLOCK_GROUPS * p)


def _blockdiag_in_grad(d):
    nj, gh, gp = d.shape
    h, p = gh // S5_BLOCK_GROUPS, gp // S5_BLOCK_GROUPS
    e = jnp.eye(S5_BLOCK_GROUPS, dtype=d.dtype)
    d6 = d.reshape(nj, S5_BLOCK_GROUPS, h, S5_BLOCK_GROUPS, p)
    return jnp.einsum("jghfp,gf->jgph", d6, e).reshape(nj * S5_BLOCK_GROUPS, p, h)


def _blockdiag_out(cc):
    g, h, p = cc.shape
    e = jnp.eye(S5_BLOCK_GROUPS, dtype=cc.dtype)
    c4 = cc.reshape(g // S5_BLOCK_GROUPS, S5_BLOCK_GROUPS, h, p)
    return jnp.einsum("jghp,gf->jfpgh", c4, e).reshape(g // S5_BLOCK_GROUPS, S5_BLOCK_GROUPS * p, S5_BLOCK_GROUPS * h)


def _blockdiag_out_grad(d):
    nj, gp, gh = d.shape
    h, p = gh // S5_BLOCK_GROUPS, gp // S5_BLOCK_GROUPS
    e = jnp.eye(S5_BLOCK_GROUPS, dtype=d.dtype)
    d6 = d.reshape(nj, S5_BLOCK_GROUPS, p, S5_BLOCK_GROUPS, h)
    return jnp.einsum("jfpgh,gf->jghp", d6, e).reshape(nj * S5_BLOCK_GROUPS, h, p)


def _s5_fwd(tag, x, prm, w_out4):
    a_re, a_im, log_dt, b_re, b_im, c_re, c_im, d_skip = prm
    t, d = x.shape
    g, p = a_re.shape
    w = g * p
    nj = g // S5_BLOCK_GROUPS
    cw, sw = S5_BLOCK_GROUPS * S5_GROUP, S5_BLOCK_GROUPS * p
    seg = t // SCAN_SEGMENTS
    tm = _tile(t, 512)
    lb_re, lb_im, bb_re, bb_im = _s5_discretise(a_re, a_im, log_dt, b_re, b_im)
    lam = jnp.stack([lb_re.reshape(1, w), lb_im.reshape(1, w)])
    bs = jnp.stack([_blockdiag_in(bb_re), _blockdiag_in(bb_im)]).astype(BF16)
    cs = jnp.stack([_blockdiag_out(c_re), -_blockdiag_out(c_im)]).astype(BF16)
    dvec = d_skip.reshape(1, d)
    u = _to_segments(x)
    bu = _mm(f"{tag}_bu", u, bs, mode="nn", grid=(2, nj, t // tm), kaxis=None,
             a_blk=(tm, cw), a_map=lambda r, j, i: (i, j),
             b_blk=(None, None, cw, sw), b_map=lambda r, j, i: (r, j, 0, 0),
             o_shape=(2, t, w), o_blk=(None, tm, sw), o_map=lambda r, j, i: (r, i, j))
    hs = _s5_scan(f"{tag}_scan", lam, bu.reshape(2, seg, SCAN_SEGMENTS, w)).reshape(2, t, w)
    ych = _mm(f"{tag}_ch", hs, cs, mode="nn", grid=(nj, t // tm, 2), kaxis=2,
              a_blk=(None, tm, sw), a_map=lambda j, i, r: (r, i, j),
              b_blk=(None, None, sw, cw), b_map=lambda j, i, r: (r, j, 0, 0),
              o_shape=(t, d), o_blk=(tm, cw), o_map=lambda j, i, r: (i, j))
    ypre, act = _s5_act_fwd(f"{tag}_act", ych, u, dvec)
    vg = _mm_shards_nn(f"{tag}_wout", act, w_out4, F32)
    m = _from_segments(_glu_fwd(f"{tag}_glu", vg))
    return m, (u, lam, bs, cs, dvec, hs, ypre, act, vg)


def _s5_bwd(tag, dm, saved, prm, w_out4):
    a_re, a_im, log_dt, b_re, b_im, c_re, c_im, d_skip = prm
    u, lam, bs, cs, dvec, hs, ypre, act, vg = saved
    t, d = u.shape
    g, p = a_re.shape
    w = g * p
    nj = g // S5_BLOCK_GROUPS
    cw, sw = S5_BLOCK_GROUPS * S5_GROUP, S5_BLOCK_GROUPS * p
    seg = t // SCAN_SEGMENTS
    tm = _tile(t, 512)
    dvg = _glu_bwd(f"{tag}_glu_bwd", _to_segments(dm), vg)
    d_wout4 = _mm_shards_tn(f"{tag}_dwout", act, dvg, w_out4.shape[0])
    dact = _mm_shards_nt(f"{tag}_dact", dvg, w_out4)
    dypre, duskip, dd = _s5_act_bwd(f"{tag}_act_bwd", dact, ypre, u, dvec)
    dh = _mm(f"{tag}_dh", dypre, cs, mode="nt", grid=(2, nj, t // tm), kaxis=None,
             a_blk=(tm, cw), a_map=lambda r, j, i: (i, j),
             b_blk=(None, None, sw, cw), b_map=lambda r, j, i: (r, j, 0, 0),
             o_shape=(2, t, w), o_blk=(None, tm, sw), o_map=lambda r, j, i: (r, i, j))
    dcs = _mm(f"{tag}_dc", hs, dypre, mode="tn", grid=(2, nj, t // tm), kaxis=2,
              a_blk=(None, tm, sw), a_map=lambda r, j, i: (r, i, j),
              b_blk=(tm, cw), b_map=lambda r, j, i: (i, j),
              o_shape=(2, nj, sw, cw), o_blk=(None, None, sw, cw), o_map=lambda r, j, i: (r, j, 0, 0))
    gs, dlam8 = _s5_scan(f"{tag}_scan_bwd", lam, dh.reshape(2, seg, SCAN_SEGMENTS, w),
                         hs.reshape(2, seg, SCAN_SEGMENTS, w))
    gs = gs.reshape(2, t, w)
    du = _mm(f"{tag}_du", gs, bs, mode="nt", grid=(nj, t // tm, 2), kaxis=2,
             a_blk=(None, tm, sw), a_map=lambda j, i, r: (r, i, j),
             b_blk=(None, None, cw, sw), b_map=lambda j, i, r: (r, j, 0, 0),
             o_shape=(t, d), o_blk=(tm, cw), o_map=lambda j, i, r: (i, j))
    dbs = _mm(f"{tag}_db", u, gs, mode="tn", grid=(2, nj, t // tm), kaxis=2,
              a_blk=(tm, cw), a_map=lambda r, j, i: (i, j),
              b_blk=(None, tm, sw), b_map=lambda r, j, i: (r, i, j),
              o_shape=(2, nj, cw, sw), o_blk=(None, None, cw, sw), o_map=lambda r, j, i: (r, j, 0, 0))
    dx = _from_segments(du + duskip)
    dlam = jnp.sum(dlam8, axis=1).reshape(2, g, p)
    small = dict(dlb_re=dlam[0], dlb_im=dlam[1],
                 dbb_re=_blockdiag_in_grad(dbs[0]), dbb_im=_blockdiag_in_grad(dbs[1]),
                 dc_re=_blockdiag_out_grad(dcs[0]), dc_im=-_blockdiag_out_grad(dcs[1]),
                 dd=dd.reshape(g, S5_GROUP))
    return dx, d_wout4, small


def _pack(pieces):
    flat = jnp.concatenate([p.reshape(-1).astype(F32) for p in pieces])
    n = flat.shape[0]
    unit = V7X_SUBLANES * V7X_LANES
    total = -(-n // unit) * unit
    return jnp.pad(flat, (0, total - n)).reshape(total // V7X_LANES, V7X_LANES)


def _unpack(buf, shapes):
    flat = buf.reshape(-1)
    out, off = [], 0
    for s in shapes:
        n = math.prod(s)
        out.append(flat[off:off + n].reshape(s))
        off += n
    return out


def kernel(x, ffn1_w_in, ffn1_w_out, ln1_g, ln1_b, lnm_g, lnm_b, ffn2_w_in, ffn2_w_out, ln2_g, ln2_b, fox_w_in, fox_b_f, fox_w_o, s5_a_re, s5_a_im, s5_log_dt, s5_b_re, s5_b_im, s5_c_re, s5_c_im, s5_d, s5_w_out, loss_target, m_ffn1_w_in, m_ffn1_w_out, m_ln1_g, m_ln1_b, m_lnm_g, m_lnm_b, m_ffn2_w_in, m_ffn2_w_out, m_ln2_g, m_ln2_b, m_fox_w_in, m_fox_b_f, m_fox_w_o, m_s5_a_re, m_s5_a_im, m_s5_log_dt, m_s5_b_re, m_s5_b_im, m_s5_c_re, m_s5_c_im, m_s5_d, m_s5_w_out, v_ffn1_w_in, v_ffn1_w_out, v_ln1_g, v_ln1_b, v_lnm_g, v_lnm_b, v_ffn2_w_in, v_ffn2_w_out, v_ln2_g, v_ln2_b, v_fox_w_in, v_fox_b_f, v_fox_w_o, v_s5_a_re, v_s5_a_im, v_s5_log_dt, v_s5_b_re, v_s5_b_im, v_s5_c_re, v_s5_c_im, v_s5_d, v_s5_w_out):
    big_names = ["ffn1_w_in", "ffn1_w_out", "ffn2_w_in", "ffn2_w_out", "fox_w_in", "fox_w_o", "s5_w_out"]
    small_names = ["ln1_g", "ln1_b", "lnm_g", "lnm_b", "ln2_g", "ln2_b", "fox_b_f", "s5_a_re", "s5_a_im", "s5_log_dt",
                   "s5_b_re", "s5_b_im", "s5_c_re", "s5_c_im", "s5_d"]
    out_order = ["ffn1_w_in", "ffn1_w_out", "ln1_g", "ln1_b", "lnm_g", "lnm_b", "ffn2_w_in", "ffn2_w_out", "ln2_g",
                 "ln2_b", "fox_w_in", "fox_b_f", "fox_w_o", "s5_a_re", "s5_a_im", "s5_log_dt", "s5_b_re", "s5_b_im",
                 "s5_c_re", "s5_c_im", "s5_d", "s5_w_out"]
    env = dict(locals())
    w = {n: env[n] for n in out_order}
    mom = {n: env["m_" + n] for n in out_order}
    vel = {n: env["v_" + n] for n in out_order}

    depth, d = ln1_g.shape
    t = x.shape[1]
    alpha = (2.0 * depth) ** 0.25
    x0 = x.reshape(t, d)
    tgt = loss_target.reshape(t, d)
    my_x, my_y, my_c = lax.axis_index("x"), lax.axis_index("y"), lax.axis_index("c")
    place = jnp.stack([2 * my_x + my_y, my_c]).astype(jnp.int32)

    flat = {n: w[n].reshape(-1, w[n].shape[-1]) for n in big_names}
    gathered = _gather_shards("gather_weights", [flat[n].astype(BF16) for n in big_names])
    full = dict(zip(big_names, gathered))

    def layer_cols(name, i):
        rows = w[name].shape[1]
        return full[name][:, i * rows:(i + 1) * rows, :]

    def layer_rows(name, i):
        rows = w[name].shape[1]
        return full[name][:, i * rows:(i + 1) * rows, :].reshape(N_CHIPS * rows, -1)

    nh = fox_b_f.shape[1]
    fox_cols = 3 * d + nh
    fox_pad = -(-fox_cols // (5 * V7X_LANES)) * (5 * V7X_LANES)

    def fox_wpad(j):
        w4 = layer_cols("fox_w_in", j)
        wf = w4.transpose(1, 0, 2).reshape(d, fox_cols)
        return jnp.pad(wf, ((0, 0), (0, fox_pad - fox_cols)))

    def s5_params(j):
        return (s5_a_re[j], s5_a_im[j], s5_log_dt[j], s5_b_re[j], s5_b_im[j], s5_c_re[j], s5_c_im[j], s5_d[j])

    saved = []
    h = x0
    for i in range(depth):
        j = i // 2
        h, s1 = _ffn_fwd(f"l{i}_ffn1", alpha, h, layer_cols("ffn1_w_in", i), layer_rows("ffn1_w_out", i),
                         ln1_g[i:i + 1], ln1_b[i:i + 1])
        if i % 2 == 0:
            m, sm = _fox_fwd(f"l{i}_fox", h, fox_wpad(j), fox_b_f[j:j + 1], layer_rows("fox_w_o", j))
        else:
            m, sm = _s5_fwd(f"l{i}_s5", h, s5_params(j), layer_cols("s5_w_out", j))
        h, xhat_m, rstd_m = _ln_fwd(f"l{i}_lnm", alpha, h, m, 1.0, lnm_g[i:i + 1], lnm_b[i:i + 1])
        h, s2 = _ffn_fwd(f"l{i}_ffn2", alpha, h, layer_cols("ffn2_w_in", i), layer_rows("ffn2_w_out", i),
                         ln2_g[i:i + 1], ln2_b[i:i + 1])
        saved.append((s1, sm, (xhat_m, rstd_m), s2))
    loss_part = _loss_sum("loss", h, tgt) * (0.5 / d)

    gbig = {n: [None] * w[n].shape[0] for n in big_names}
    gsmall = {n: [None] * w[n].shape[0] for n in small_names}
    s5_cot = [None] * s5_a_re.shape[0]
    terms = [(h, 1.0 / d), (tgt, -1.0 / d)]
    for i in reversed(range(depth)):
        j = i // 2
        s1, sm, (xhat_m, rstd_m), s2 = saved[i]
        terms, dwin, dwout, dg, db = _ffn_bwd(f"l{i}_ffn2", alpha, terms, s2, layer_cols("ffn2_w_in", i),
                                              layer_rows("ffn2_w_out", i), ln2_g[i:i + 1])
        gbig["ffn2_w_in"][i], gbig["ffn2_w_out"][i] = dwin, dwout.reshape(N_CHIPS, -1, d)
        gsmall["ln2_g"][i], gsmall["ln2_b"][i] = dg, db
        dz, dg, db = _ln_bwd(f"l{i}_lnm_bwd", terms, xhat_m, rstd_m, lnm_g[i:i + 1])
        gsmall["lnm_g"][i], gsmall["lnm_b"][i] = dg, db
        if i % 2 == 0:
            dx, d_wpad, dbf, d_wo = _fox_bwd(f"l{i}_fox", dz, sm, fox_wpad(j), fox_b_f[j:j + 1], layer_rows("fox_w_o", j))
            gbig["fox_w_in"][j] = d_wpad[:, :fox_cols].reshape(d, N_CHIPS, -1).transpose(1, 0, 2)
            gbig["fox_w_o"][j] = d_wo.reshape(N_CHIPS, -1, d)
            gsmall["fox_b_f"][j] = dbf
        else:
            dx, d_wout4, s5_cot[j] = _s5_bwd(f"l{i}_s5", dz, sm, s5_params(j), layer_cols("s5_w_out", j))
            gbig["s5_w_out"][j] = d_wout4
        terms = [(dz, alpha), (dx, 1.0)]
        terms, dwin, dwout, dg, db = _ffn_bwd(f"l{i}_ffn1", alpha, terms, s1, layer_cols("ffn1_w_in", i),
                                              layer_rows("ffn1_w_out", i), ln1_g[i:i + 1])
        gbig["ffn1_w_in"][i], gbig["ffn1_w_out"][i] = dwin, dwout.reshape(N_CHIPS, -1, d)
        gsmall["ln1_g"][i], gsmall["ln1_b"][i] = dg, db
    grad_x = _lincomb("grad_x", terms).reshape(x.shape)

    g4 = [jnp.concatenate(gbig[n], axis=1) for n in big_names]
    from_sibling = _swap_halves("grad_pair_swap", g4)
    pair = [_pair_sum(f"grad_pair_sum_{n}", g, r, place) for n, g, r in zip(big_names, g4, from_sibling)]
    from_chips = _scatter_partials("grad_chip_scatter", [pb for _, pb in pair])
    halves = [_chip_sum(f"grad_chip_sum_{n}", p, r, place) for n, (p, _), r in zip(big_names, pair, from_chips)]
    reduced = dict(zip(big_names, _join_halves("grad_join", halves)))

    cot_names = ["dlb_re", "dlb_im", "dbb_re", "dbb_im", "dc_re", "dc_im", "dd"]
    ln_names = ["ln1_g", "ln1_b", "lnm_g", "lnm_b", "ln2_g", "ln2_b"]
    pieces = [loss_part] + [jnp.concatenate(gsmall[n], axis=0) for n in ln_names + ["fox_b_f"]]
    pieces += [jnp.stack([s5_cot[j][n] for j in range(len(s5_cot))]) for n in cot_names]
    shapes = [p.shape for p in pieces]
    mine = _pack(pieces)
    everyone = _gather_all("small_gather", mine).reshape(N_DEV, *mine.shape)
    summed = _unpack(_sum_leading("small_sum", everyone), shapes)
    loss = summed[0].reshape(())
    gs_final = dict(zip(ln_names + ["fox_b_f"], summed[1:8]))
    cot = dict(zip(cot_names, summed[8:]))
    prm_names = ["s5_a_re", "s5_a_im", "s5_log_dt", "s5_b_re", "s5_b_im"]
    _, disc_vjp = jax.vjp(jax.vmap(_s5_discretise), *[w[n] for n in prm_names])
    for n, gval in zip(prm_names, disc_vjp((cot["dlb_re"], cot["dlb_im"], cot["dbb_re"], cot["dbb_im"]))):
        gs_final[n] = gval
    gs_final["s5_c_re"], gs_final["s5_c_im"], gs_final["s5_d"] = cot["dc_re"], cot["dc_im"], cot["dd"]

    grads, deltas, new_m, new_v = {}, {}, {}, {}
    for n in big_names:
        grads[n] = reduced[n].reshape(w[n].shape)
        dl, nm, nv = _adamw(f"adamw_{n}", flat[n], reduced[n], mom[n].reshape(flat[n].shape), vel[n].reshape(flat[n].shape))
        deltas[n], new_m[n], new_v[n] = dl.reshape(w[n].shape), nm.reshape(w[n].shape), nv.reshape(w[n].shape)
    small_shapes = [w[n].shape for n in small_names]
    for n in small_names:
        grads[n] = gs_final[n].reshape(w[n].shape)
    packed = [_pack([src[n] for n in small_names]) for src in (w, grads, mom, vel)]
    for dst, buf in zip((deltas, new_m, new_v), _adamw("adamw_small", *packed)):
        for n, val in zip(small_names, _unpack(buf, small_shapes)):
            dst[n] = val
    return (loss, grad_x, *[grads[n] for n in out_order], *[deltas[n] for n in out_order],
            *[new_m[n] for n in out_order], *[new_v[n] for n in out_order])
```

```python
import functools
import math

import jax
import jax.numpy as jnp
from jax import lax
from jax.experimental import pallas as pl
from jax.experimental.pallas import tpu as pltpu

F32 = jnp.float32
BF16 = jnp.bfloat16
LN_EPS = 1e-5
NEG_INF = -1e30
ADAM_LR = 0.001
ADAM_B1 = 0.9
ADAM_B2 = 0.999
ADAM_EPS = 1e-08
ADAM_WD = 0.01
ADAM_STEP = 10
S5_GROUP = 16
SCAN_SEGMENTS = 8
V7X_SUBLANES = 8
V7X_LANES = 128
VMEM_LIMIT = 56 * 1024 * 1024
N_CHIPS = 4
N_DEV = 8
MESH = pl.DeviceIdType.MESH
ANY = pl.BlockSpec(memory_space=pl.ANY)


def _cp(n_grid, kaxis=None):
    sem = tuple("arbitrary" if (kaxis is None or i == kaxis) else "parallel" for i in range(n_grid))
    return pltpu.CompilerParams(dimension_semantics=sem, vmem_limit_bytes=VMEM_LIMIT)


def _tile(n, pref, mult=V7X_SUBLANES):
    if n <= pref:
        return n
    for t in range(pref, 0, -1):
        if n % t == 0 and t % mult == 0:
            return t
    return n


_CONTRACT = {"nn": ((1,), (0,)), "nt": ((1,), (1,)), "tn": ((0,), (0,))}


def _mm(name, a, b, *, mode, grid, kaxis, a_blk, a_map, b_blk, b_map, o_shape, o_blk, o_map, o_dtype=F32, scale=None):
    nk = 1 if kaxis is None else grid[kaxis]
    assert kaxis is None or kaxis == len(grid) - 1
    dims = (_CONTRACT[mode], ((), ()))
    use_acc = nk > 1 and o_dtype != F32
    acc_shape = tuple(d for d in o_blk if d is not None)

    def body(a_ref, b_ref, o_ref, *scratch):
        p = lax.dot_general(a_ref[...].astype(BF16), b_ref[...].astype(BF16), dims, preferred_element_type=F32)
        if nk == 1:
            if scale is not None:
                p = p * scale
            o_ref[...] = p.astype(o_dtype)
            return
        acc = scratch[0] if use_acc else o_ref
        k = pl.program_id(kaxis)

        @pl.when(k == 0)
        def _():
            acc[...] = p

        @pl.when(k > 0)
        def _():
            acc[...] += p

        if use_acc or scale is not None:
            @pl.when(k == nk - 1)
            def _():
                r = acc[...]
                if scale is not None:
                    r = r * scale
                o_ref[...] = r.astype(o_dtype)

    return pl.pallas_call(
        body, grid=grid, name=name,
        in_specs=[pl.BlockSpec(a_blk, a_map), pl.BlockSpec(b_blk, b_map)],
        out_specs=pl.BlockSpec(o_blk, o_map),
        out_shape=jax.ShapeDtypeStruct(o_shape, o_dtype),
        scratch_shapes=[pltpu.VMEM(acc_shape, F32)] if use_acc else [],
        compiler_params=_cp(len(grid), kaxis),
    )(a, b)


def _mm_shards_nn(name, a, w4, o_dtype):
    t, k = a.shape
    s, _, n = w4.shape
    tm = _tile(t, 512)
    return _mm(name, a, w4, mode="nn", grid=(s, t // tm), kaxis=None,
               a_blk=(tm, k), a_map=lambda j, i: (i, 0),
               b_blk=(None, k, n), b_map=lambda j, i: (j, 0, 0),
               o_shape=(t, s * n), o_blk=(tm, n), o_map=lambda j, i: (i, j), o_dtype=o_dtype)


def _mm_shards_nt(name, g, w4):
    t = g.shape[0]
    s, k, n = w4.shape
    tm = _tile(t, 512)
    return _mm(name, g, w4, mode="nt", grid=(t // tm, s), kaxis=1,
               a_blk=(tm, n), a_map=lambda i, kk: (i, kk),
               b_blk=(None, k, n), b_map=lambda i, kk: (kk, 0, 0),
               o_shape=(t, k), o_blk=(tm, k), o_map=lambda i, kk: (i, 0))


def _mm_shards_tn(name, a, g, s):
    t, k = a.shape
    n = g.shape[1] // s
    tk = _tile(t, 512)
    return _mm(name, a, g, mode="tn", grid=(s, t // tk), kaxis=1,
               a_blk=(tk, k), a_map=lambda j, kk: (kk, 0),
               b_blk=(tk, n), b_map=lambda j, kk: (kk, j),
               o_shape=(s, k, n), o_blk=(None, k, n), o_map=lambda j, kk: (j, 0, 0))


def _mm_nn(name, a, w, o_dtype=F32, tn=None):
    t, k = a.shape
    n = w.shape[1]
    tm = _tile(t, 512)
    tn = n if tn is None else tn
    return _mm(name, a, w, mode="nn", grid=(n // tn, t // tm), kaxis=None,
               a_blk=(tm, k), a_map=lambda j, i: (i, 0),
               b_blk=(k, tn), b_map=lambda j, i: (0, j),
               o_shape=(t, n), o_blk=(tm, tn), o_map=lambda j, i: (i, j), o_dtype=o_dtype)


def _mm_nt(name, g, w, o_dtype=F32, tn=None, scale=None):
    t, k = g.shape
    n = w.shape[0]
    tm = _tile(t, 512)
    tn = n if tn is None else tn
    return _mm(name, g, w, mode="nt", grid=(n // tn, t // tm), kaxis=None,
               a_blk=(tm, k), a_map=lambda j, i: (i, 0),
               b_blk=(tn, k), b_map=lambda j, i: (j, 0),
               o_shape=(t, n), o_blk=(tm, tn), o_map=lambda j, i: (i, j), o_dtype=o_dtype, scale=scale)


def _mm_tn(name, a, g, tm=None, tn=None, scale=None):
    t, m = a.shape
    n = g.shape[1]
    tk = _tile(t, 512)
    tm = m if tm is None else tm
    tn = n if tn is None else tn
    return _mm(name, a, g, mode="tn", grid=(m // tm, n // tn, t // tk), kaxis=2,
               a_blk=(tk, tm), a_map=lambda i, j, kk: (kk, i),
               b_blk=(tk, tn), b_map=lambda i, j, kk: (kk, j),
               o_shape=(m, n), o_blk=(tm, tn), o_map=lambda i, j, kk: (i, j), scale=scale)


def _sigmoid(x):
    return 1.0 / (1.0 + jnp.exp(-x))


def _rows_call(name, body, t, tm, ins, in_cols, outs, acc_outs=()):
    in_specs = []
    for x, c in zip(ins, in_cols):
        if x.shape[0] == 1:
            in_specs.append(pl.BlockSpec((1, c), lambda i: (0, 0)))
        else:
            in_specs.append(pl.BlockSpec((tm, c), lambda i: (i, 0)))
    out_specs = [pl.BlockSpec((tm, s.shape[1]), lambda i: (i, 0)) for s in outs]
    out_specs += [pl.BlockSpec((1, s.shape[1]), lambda i: (0, 0)) for s in acc_outs]
    return pl.pallas_call(
        body, grid=(t // tm,), name=name, in_specs=in_specs, out_specs=out_specs,
        out_shape=list(outs) + list(acc_outs), compiler_params=_cp(1),
    )(*ins)


def _ln_fwd(name, alpha, x, r, coef, g, b):
    t, d = x.shape
    tm = _tile(t, 256)

    def body(x_ref, r_ref, g_ref, b_ref, y_ref, xh_ref, rs_ref):
        z = alpha * x_ref[...] + coef * r_ref[...]
        mu = jnp.mean(z, axis=-1, keepdims=True)
        zc = z - mu
        var = jnp.mean(zc * zc, axis=-1, keepdims=True)
        rstd = lax.rsqrt(var + LN_EPS)
        xh = zc * rstd
        y_ref[...] = xh * g_ref[...] + b_ref[...]
        xh_ref[...] = xh
        rs_ref[...] = rstd

    sd = jax.ShapeDtypeStruct
    return _rows_call(name, body, t, tm, [x, r, g, b], [d, d, d, d],
                      [sd((t, d), F32), sd((t, d), F32), sd((t, 1), F32)])


def _ln_bwd(name, terms, xhat, rstd, g):
    t, d = xhat.shape
    tm = _tile(t, 256)
    n = len(terms)
    coefs = [c for _, c in terms]

    def body(*refs):
        t_refs = refs[:n]
        xh_ref, rs_ref, g_ref, dz_ref, dg_ref, db_ref = refs[n:]
        dy = coefs[0] * t_refs[0][...]
        for c, r in zip(coefs[1:], t_refs[1:]):
            dy = dy + c * r[...]
        xh = xh_ref[...]
        dxh = dy * g_ref[...]
        m1 = jnp.mean(dxh, axis=-1, keepdims=True)
        m2 = jnp.mean(dxh * xh, axis=-1, keepdims=True)
        dz_ref[...] = rs_ref[...] * (dxh - m1 - xh * m2)
        pg = jnp.sum(dy * xh, axis=0, keepdims=True)
        pb = jnp.sum(dy, axis=0, keepdims=True)
        i = pl.program_id(0)

        @pl.when(i == 0)
        def _():
            dg_ref[...] = pg
            db_ref[...] = pb

        @pl.when(i > 0)
        def _():
            dg_ref[...] += pg
            db_ref[...] += pb

    sd = jax.ShapeDtypeStruct
    arrs = [a for a, _ in terms] + [xhat, rstd, g]
    cols = [d] * n + [d, 1, d]
    return _rows_call(name, body, t, tm, arrs, cols, [sd((t, d), F32)], [sd((1, d), F32), sd((1, d), F32)])


def _lincomb(name, terms):
    t, d = terms[0][0].shape
    tm = _tile(t, 256)
    coefs = [c for _, c in terms]
    n = len(terms)

    def body(*refs):
        acc = coefs[0] * refs[0][...]
        for c, r in zip(coefs[1:], refs[1:n]):
            acc = acc + c * r[...]
        refs[n][...] = acc

    return _rows_call(name, body, t, tm, [a for a, _ in terms], [d] * n, [jax.ShapeDtypeStruct((t, d), F32)])[0]


def _loss_sum(name, y, tgt):
    t, d = y.shape
    tm = _tile(t, 256)

    def body(y_ref, t_ref, o_ref):
        e = y_ref[...] - t_ref[...]
        s = jnp.sum(jnp.sum(e * e, axis=1, keepdims=True), axis=0, keepdims=True)
        i = pl.program_id(0)

        @pl.when(i == 0)
        def _():
            o_ref[...] = s

        @pl.when(i > 0)
        def _():
            o_ref[...] += s

    return _rows_call(name, body, t, tm, [y, tgt], [d, d], [], [jax.ShapeDtypeStruct((1, 1), F32)])[0]


def _swiglu_fwd(name, h):
    t, f2 = h.shape
    f = f2 // 2
    tm = _tile(t, 256)

    def body(h_ref, a_ref):
        g = h_ref[:, :f].astype(F32)
        u = h_ref[:, f:].astype(F32)
        a_ref[...] = (g * _sigmoid(g) * u).astype(BF16)

    return _rows_call(name, body, t, tm, [h], [f2], [jax.ShapeDtypeStruct((t, f), BF16)])[0]


def _swiglu_bwd(name, da, h):
    t, f2 = h.shape
    f = f2 // 2
    tm = _tile(t, 256)

    def body(da_ref, h_ref, dh_ref):
        g = h_ref[:, :f].astype(F32)
        u = h_ref[:, f:].astype(F32)
        d = da_ref[...].astype(F32)
        sg = _sigmoid(g)
        dh_ref[:, :f] = (d * u * sg * (1.0 + g * (1.0 - sg))).astype(BF16)
        dh_ref[:, f:] = (d * g * sg).astype(BF16)

    return _rows_call(name, body, t, tm, [da, h], [f, f2], [jax.ShapeDtypeStruct((t, f2), BF16)])[0]


_GELU_C = math.sqrt(2.0 / math.pi)


def _s5_act_fwd(name, ych, u, dvec):
    t, d = u.shape
    tm = _tile(t, 256)

    def body(y_ref, u_ref, d_ref, p_ref, a_ref):
        y = y_ref[...] + d_ref[...] * u_ref[...]
        p_ref[...] = y
        a_ref[...] = (0.5 * y * (1.0 + jnp.tanh(_GELU_C * (y + 0.044715 * y * y * y)))).astype(BF16)

    sd = jax.ShapeDtypeStruct
    return _rows_call(name, body, t, tm, [ych, u, dvec], [d, d, d], [sd((t, d), F32), sd((t, d), BF16)])


def _s5_act_bwd(name, dact, ypre, u, dvec):
    t, d = u.shape
    tm = _tile(t, 256)

    def body(da_ref, y_ref, u_ref, d_ref, dy_ref, ds_ref, dd_ref):
        y = y_ref[...]
        th = jnp.tanh(_GELU_C * (y + 0.044715 * y * y * y))
        dg = 0.5 * (1.0 + th) + 0.5 * y * (1.0 - th * th) * _GELU_C * (1.0 + 3.0 * 0.044715 * y * y)
        dy = da_ref[...] * dg
        dy_ref[...] = dy
        ds_ref[...] = dy * d_ref[...]
        pd = jnp.sum(dy * u_ref[...], axis=0, keepdims=True)
        i = pl.program_id(0)

        @pl.when(i == 0)
        def _():
            dd_ref[...] = pd

        @pl.when(i > 0)
        def _():
            dd_ref[...] += pd

    sd = jax.ShapeDtypeStruct
    return _rows_call(name, body, t, tm, [dact, ypre, u, dvec], [d, d, d, d],
                      [sd((t, d), F32), sd((t, d), F32)], [sd((1, d), F32)])


def _glu_fwd(name, vg):
    t, d2 = vg.shape
    d = d2 // 2
    tm = _tile(t, 256)

    def body(vg_ref, m_ref):
        m_ref[...] = vg_ref[:, :d] * _sigmoid(vg_ref[:, d:])

    return _rows_call(name, body, t, tm, [vg], [d2], [jax.ShapeDtypeStruct((t, d), F32)])[0]


def _glu_bwd(name, dm, vg):
    t, d2 = vg.shape
    d = d2 // 2
    tm = _tile(t, 256)

    def body(dm_ref, vg_ref, o_ref):
        sg = _sigmoid(vg_ref[:, d:])
        g = dm_ref[...]
        o_ref[:, :d] = (g * sg).astype(BF16)
        o_ref[:, d:] = (g * vg_ref[:, :d] * sg * (1.0 - sg)).astype(BF16)

    return _rows_call(name, body, t, tm, [dm, vg], [d, d2], [jax.ShapeDtypeStruct((t, d2), BF16)])[0]


def _adamw(name, w, g, m, v):
    r, c = w.shape
    tr = _tile(r, max(V7X_SUBLANES, (1 << 20) // (4 * c) // V7X_SUBLANES * V7X_SUBLANES))

    def body(w_ref, g_ref, m_ref, v_ref, d_ref, nm_ref, nv_ref):
        gg = g_ref[...]
        nm = ADAM_B1 * m_ref[...] + (1.0 - ADAM_B1) * gg
        nv = ADAM_B2 * v_ref[...] + (1.0 - ADAM_B2) * (gg * gg)
        m_hat = nm / (1.0 - ADAM_B1 ** ADAM_STEP)
        v_hat = nv / (1.0 - ADAM_B2 ** ADAM_STEP)
        d_ref[...] = -ADAM_LR * (m_hat / (jnp.sqrt(v_hat) + ADAM_EPS) + ADAM_WD * w_ref[...])
        nm_ref[...] = nm
        nv_ref[...] = nv

    sd = jax.ShapeDtypeStruct((r, c), F32)
    return _rows_call(name, body, r, tr, [w, g, m, v], [c] * 4, [sd, sd, sd])


def _adamw_join(name, w, mine, theirs, m, v, place):
    r, c = w.shape
    h = r // 2
    tr = _tile(h, max(V7X_SUBLANES, (1 << 19) // (4 * c) // V7X_SUBLANES * V7X_SUBLANES))
    nb = h // tr

    def body(pref, w_ref, a_ref, b_ref, m_ref, v_ref, g_ref, d_ref, nm_ref, nv_ref):
        gg = jnp.where(pl.program_id(0) == pref[1], a_ref[...], b_ref[...])
        nm = ADAM_B1 * m_ref[...] + (1.0 - ADAM_B1) * gg
        nv = ADAM_B2 * v_ref[...] + (1.0 - ADAM_B2) * (gg * gg)
        m_hat = nm / (1.0 - ADAM_B1 ** ADAM_STEP)
        v_hat = nv / (1.0 - ADAM_B2 ** ADAM_STEP)
        g_ref[...] = gg
        d_ref[...] = -ADAM_LR * (m_hat / (jnp.sqrt(v_hat) + ADAM_EPS) + ADAM_WD * w_ref[...])
        nm_ref[...] = nm
        nv_ref[...] = nv

    full = pl.BlockSpec((tr, c), lambda hf, i, pref: (hf * nb + i, 0))
    sd = jax.ShapeDtypeStruct((r, c), F32)
    return pl.pallas_call(
        body, name=name,
        grid_spec=pltpu.PrefetchScalarGridSpec(
            num_scalar_prefetch=1, grid=(2, nb),
            in_specs=[full,
                      pl.BlockSpec((tr, c), lambda hf, i, pref: (jnp.where(hf == pref[1], i, 0), 0)),
                      pl.BlockSpec((tr, c), lambda hf, i, pref: (jnp.where(hf == pref[1], 0, i), 0)),
                      full, full],
            out_specs=[full, full, full, full]),
        out_shape=[sd, sd, sd, sd],
        compiler_params=_cp(2),
    )(place, w, mine, theirs, m, v)


def _sum_leading(name, a):
    n, r, c = a.shape
    tr = _tile(r, 512)

    def body(a_ref, o_ref):
        acc = a_ref[0]
        for k in range(1, n):
            acc = acc + a_ref[k]
        o_ref[...] = acc

    return pl.pallas_call(
        body, grid=(r // tr,), name=name,
        in_specs=[pl.BlockSpec((n, tr, c), lambda i: (0, i, 0))],
        out_specs=pl.BlockSpec((tr, c), lambda i: (i, 0)),
        out_shape=jax.ShapeDtypeStruct((r, c), F32), compiler_params=_cp(1),
    )(a)


def _split3(x):
    hi = x.astype(BF16)
    r1 = x - hi.astype(F32)
    mid = r1.astype(BF16)
    lo = (r1 - mid.astype(F32)).astype(BF16)
    return hi, mid, lo


def _tri_sum(tri, x):
    dims = (((1,), (0,)), ((), ()))
    hi, mid, lo = _split3(x)
    out = lax.dot_general(tri, lo, dims, preferred_element_type=F32)
    out = out + lax.dot_general(tri, mid, dims, preferred_element_type=F32)
    return out + lax.dot_general(tri, hi, dims, preferred_element_type=F32)


def _fox_cumsum(name, fl, bf):
    t, h = fl.shape
    tb = _tile(t, 512)

    def body(fl_ref, bf_ref, c_ref, carry):
        i = pl.program_id(0)

        @pl.when(i == 0)
        def _():
            carry[...] = jnp.zeros_like(carry)

        x = fl_ref[...] + bf_ref[...]
        lf = jnp.minimum(x, 0.0) - jnp.log(1.0 + jnp.exp(-jnp.abs(x)))
        row = lax.broadcasted_iota(jnp.int32, (tb, tb), 0)
        col = lax.broadcasted_iota(jnp.int32, (tb, tb), 1)
        tri = jnp.where(row >= col, 1.0, 0.0).astype(BF16)
        c_ref[...] = _tri_sum(tri, lf) + carry[...]
        carry[...] += jnp.sum(lf, axis=0, keepdims=True)

    return pl.pallas_call(
        body, grid=(t // tb,), name=name,
        in_specs=[pl.BlockSpec((tb, h), lambda i: (i, 0)), pl.BlockSpec((1, h), lambda i: (0, 0))],
        out_specs=pl.BlockSpec((tb, h), lambda i: (i, 0)),
        out_shape=jax.ShapeDtypeStruct((t, h), F32),
        scratch_shapes=[pltpu.VMEM((1, h), F32)], compiler_params=_cp(1),
    )(fl, bf)


def _fox_cumsum_bwd(name, dcum, fl, bf):
    t, h = fl.shape
    tb = _tile(t, 512)
    nb = t // tb

    def body(dc_ref, fl_ref, bf_ref, df_ref, db_ref, carry):
        i = pl.program_id(0)

        @pl.when(i == 0)
        def _():
            carry[...] = jnp.zeros_like(carry)

        dc = dc_ref[...]
        row = lax.broadcasted_iota(jnp.int32, (tb, tb), 0)
        col = lax.broadcasted_iota(jnp.int32, (tb, tb), 1)
        tri = jnp.where(row <= col, 1.0, 0.0).astype(BF16)
        dlf = _tri_sum(tri, dc) + carry[...]
        carry[...] += jnp.sum(dc, axis=0, keepdims=True)
        x = fl_ref[...] + bf_ref[...]
        df = dlf / (1.0 + jnp.exp(x))
        df_ref[...] = df
        pb = jnp.sum(df, axis=0, keepdims=True)

        @pl.when(i == 0)
        def _():
            db_ref[...] = pb

        @pl.when(i > 0)
        def _():
            db_ref[...] += pb

    rev = lambda i: (nb - 1 - i, 0)
    return pl.pallas_call(
        body, grid=(nb,), name=name,
        in_specs=[pl.BlockSpec((tb, h), rev), pl.BlockSpec((tb, h), rev), pl.BlockSpec((1, h), lambda i: (0, 0))],
        out_specs=[pl.BlockSpec((tb, h), rev), pl.BlockSpec((1, h), lambda i: (0, 0))],
        out_shape=[jax.ShapeDtypeStruct((t, h), F32), jax.ShapeDtypeStruct((1, h), F32)],
        scratch_shapes=[pltpu.VMEM((1, h), F32)], compiler_params=_cp(1),
    )(dcum, fl, bf)


_NT = (((1,), (1,)), ((), ()))
_TN = (((0,), (0,)), ((), ()))
_NN = (((1,), (0,)), ((), ()))


def _causal_mask(s, tb):
    row = lax.broadcasted_iota(jnp.int32, (tb, tb), 0)
    col = lax.broadcasted_iota(jnp.int32, (tb, tb), 1)
    return jnp.where(col <= row, s, NEG_INF)


def _attn_fwd(name, q, k, v, ccol, crow):
    nh, nb, tb, hd = q.shape

    def body(q_ref, k_ref, v_ref, cc_ref, cr_ref, o_ref, lse_ref):
        i = pl.program_id(1)
        qi = q_ref[...]
        cc = cc_ref[...]

        def step(j, carry, diagonal=False):
            m, l, acc = carry
            s = lax.dot_general(qi, k_ref[j], _NT, preferred_element_type=F32) + cc - cr_ref[j]
            if diagonal:
                s = _causal_mask(s, tb)
            m_new = jnp.maximum(m, jnp.max(s, axis=1, keepdims=True))
            p = jnp.exp(s - m_new)
            a = jnp.exp(m - m_new)
            l = a * l + jnp.sum(p, axis=1, keepdims=True)
            acc = a * acc + lax.dot_general(p.astype(BF16), v_ref[j], _NN, preferred_element_type=F32)
            return m_new, l, acc

        init = (jnp.full((tb, 1), NEG_INF, F32), jnp.zeros((tb, 1), F32), jnp.zeros((tb, hd), F32))
        m, l, acc = step(i, lax.fori_loop(0, i, step, init), diagonal=True)
        o_ref[...] = acc / l
        lse_ref[...] = m + jnp.log(l)

    blk = lambda *s: pl.BlockSpec(s, lambda h, i: (h, i, 0, 0))
    head = lambda *s: pl.BlockSpec(s, lambda h, i: (h, 0, 0, 0))
    return pl.pallas_call(
        body, grid=(nh, nb), name=name,
        in_specs=[blk(None, None, tb, hd), head(None, nb, tb, hd), head(None, nb, tb, hd),
                  blk(None, None, tb, 1), head(None, nb, 1, tb)],
        out_specs=[blk(None, None, tb, hd), blk(None, None, tb, 1)],
        out_shape=[jax.ShapeDtypeStruct((nh, nb, tb, hd), F32), jax.ShapeDtypeStruct((nh, nb, tb, 1), F32)],
        compiler_params=_cp(2),
    )(q, k, v, ccol, crow)


def _attn_bwd(name, q, k, v, ccol, crow, o, lse, do, scale):
    nh, nb, tb, hd = q.shape

    def body(q_ref, k_ref, v_ref, cc_ref, cr_ref, o_ref, lse_ref, do_ref, dq_ref, dk_ref, dv_ref, dr_ref, dc_ref):
        j = pl.program_id(1)

        @pl.when(j == 0)
        def _():
            dq_ref[...] = jnp.zeros_like(dq_ref)
            dr_ref[...] = jnp.zeros_like(dr_ref)

        kj = k_ref[...]
        vj = v_ref[...]
        cr = cr_ref[...]

        def step(i, carry, diagonal=False):
            dk, dv, dc = carry
            qi = q_ref[i]
            doi = do_ref[i]
            dob = doi.astype(BF16)
            di = jnp.sum(doi * o_ref[i], axis=1, keepdims=True)
            s = lax.dot_general(qi, kj, _NT, preferred_element_type=F32) + cc_ref[i] - cr
            if diagonal:
                s = _causal_mask(s, tb)
            p = jnp.exp(s - lse_ref[i])
            dv = dv + lax.dot_general(p.astype(BF16), dob, _TN, preferred_element_type=F32)
            dp = lax.dot_general(dob, vj, _NT, preferred_element_type=F32)
            ds = p * (dp - di)
            dsb = ds.astype(BF16)
            dk = dk + lax.dot_general(dsb, qi, _TN, preferred_element_type=F32)
            dq_ref[i] += lax.dot_general(dsb, kj, _NN, preferred_element_type=F32) * scale
            dr_ref[i] += jnp.sum(ds, axis=1, keepdims=True)
            dc = dc + jnp.sum(ds, axis=0, keepdims=True)
            return dk, dv, dc

        init = (jnp.zeros((tb, hd), F32), jnp.zeros((tb, hd), F32), jnp.zeros((1, tb), F32))
        dk, dv, dc = lax.fori_loop(j + 1, nb, step, step(j, init, diagonal=True))
        dk_ref[...] = dk
        dv_ref[...] = dv
        dc_ref[...] = dc

    blk = lambda *s: pl.BlockSpec(s, lambda h, j: (h, j, 0, 0))
    head = lambda *s: pl.BlockSpec(s, lambda h, j: (h, 0, 0, 0))
    sd = jax.ShapeDtypeStruct
    return pl.pallas_call(
        body, grid=(nh, nb), name=name,
        in_specs=[head(None, nb, tb, hd), blk(None, None, tb, hd), blk(None, None, tb, hd),
                  head(None, nb, tb, 1), blk(None, None, 1, tb),
                  head(None, nb, tb, hd), head(None, nb, tb, 1), head(None, nb, tb, hd)],
        out_specs=[head(None, nb, tb, hd), blk(None, None, tb, hd), blk(None, None, tb, hd),
                   head(None, nb, tb, 1), blk(None, None, 1, tb)],
        out_shape=[sd((nh, nb, tb, hd), F32), sd((nh, nb, tb, hd), F32), sd((nh, nb, tb, hd), F32),
                   sd((nh, nb, tb, 1), F32), sd((nh, nb, 1, tb), F32)],
        compiler_params=_cp(2),
    )(q, k, v, ccol, crow, o, lse, do)


def _cmul(ar, ai, br, bi):
    return ar * br - ai * bi, ar * bi + ai * br


def _s5_scan(name, lam, xin, hs=None):
    reverse = hs is not None
    _, seg, ns, w = xin.shape
    assert ns == SCAN_SEGMENTS
    wb = min(w, 2 * V7X_LANES)
    nsq = seg.bit_length() - 1
    assert (1 << nsq) == seg

    def body(*refs):
        if reverse:
            lam_ref, x_ref, h_ref, o_ref, dl_ref = refs
        else:
            lam_ref, x_ref, o_ref = refs
        lr = jnp.broadcast_to(lam_ref[0], (ns, wb))
        li = jnp.broadcast_to(lam_ref[1], (ns, wb))
        if reverse:
            li = -li
        zero = jnp.zeros((ns, wb), F32)
        at = (lambda n: seg - 1 - n) if reverse else (lambda n: n)

        def local(n, c):
            r = at(n)
            mr, mi = _cmul(lr, li, c[0], c[1])
            nr = mr + x_ref[0, r]
            ni = mi + x_ref[1, r]
            o_ref[0, r] = nr
            o_ref[1, r] = ni
            return nr, ni

        er, ei = lax.fori_loop(0, seg, local, (zero, zero))
        pr, pi = lr, li
        for _ in range(nsq):
            pr, pi = _cmul(pr, pi, pr, pi)
        sub = lax.broadcasted_iota(jnp.int32, (ns, wb), 0)

        def shifted(a, sh):
            if reverse:
                return jnp.where(sub < ns - sh, pltpu.roll(a, ns - sh, 0), 0.0)
            return jnp.where(sub >= sh, pltpu.roll(a, sh, 0), 0.0)

        xr, xi = er, ei
        sh = 1
        while sh < ns:
            tr, ti = _cmul(pr, pi, shifted(xr, sh), shifted(xi, sh))
            xr, xi = xr + tr, xi + ti
            pr, pi = _cmul(pr, pi, pr, pi)
            sh *= 2
        cr, ci = shifted(xr, 1), shifted(xi, 1)

        def fix(r, q):
            tr, ti = _cmul(q[0], q[1], cr, ci)
            gr = o_ref[0, r] + tr
            gi = o_ref[1, r] + ti
            o_ref[0, r] = gr
            o_ref[1, r] = gi
            return gr, gi

        if not reverse:
            def fixup(n, q):
                fix(n, q)
                return _cmul(q[0], q[1], lr, li)

            lax.fori_loop(0, seg, fixup, (lr, li))
            return

        def fixup_acc(n, c):
            qr, qi, ar, ai = c
            r = seg - 1 - n
            gr, gi = fix(r, (qr, qi))
            hr = h_ref[0, r - 1]
            hi = h_ref[1, r - 1]
            qr, qi = _cmul(qr, qi, lr, li)
            return qr, qi, ar + gr * hr + gi * hi, ai + gi * hr - gr * hi

        qr, qi, ar, ai = lax.fori_loop(0, seg - 1, fixup_acc, (lr, li, zero, zero))
        gr, gi = fix(0, (qr, qi))
        hr = jnp.where(sub >= 1, pltpu.roll(h_ref[0, seg - 1], 1, 0), 0.0)
        hi = jnp.where(sub >= 1, pltpu.roll(h_ref[1, seg - 1], 1, 0), 0.0)
        dl_ref[0] = ar + gr * hr + gi * hi
        dl_ref[1] = ai + gi * hr - gr * hi

    big = pl.BlockSpec((2, seg, ns, wb), lambda j: (0, 0, 0, j))
    lam_spec = pl.BlockSpec((2, 1, wb), lambda j: (0, 0, j))
    sd = jax.ShapeDtypeStruct
    if reverse:
        return pl.pallas_call(
            body, grid=(w // wb,), name=name, in_specs=[lam_spec, big, big],
            out_specs=[big, pl.BlockSpec((2, ns, wb), lambda j: (0, 0, j))],
            out_shape=[sd(xin.shape, F32), sd((2, ns, w), F32)], compiler_params=_cp(1),
        )(lam, xin, hs)
    return pl.pallas_call(
        body, grid=(w // wb,), name=name, in_specs=[lam_spec, big], out_specs=big,
        out_shape=sd(xin.shape, F32), compiler_params=_cp(1),
    )(lam, xin)


def _place():
    x, y, c = lax.axis_index("x"), lax.axis_index("y"), lax.axis_index("c")
    chips = [(1 - x, y), (x, 1 - y), (1 - x, 1 - y)]
    return x, y, c, chips


def _comm_params():
    return pltpu.CompilerParams(vmem_limit_bytes=VMEM_LIMIT)


def _cast_place(name, w, place):
    r, c = w.shape
    tr = _tile(r, max(16, (1 << 20) // (4 * c) // 16 * 16), 16)

    def body(pref, w_ref, o_ref):
        o_ref[...] = w_ref[...].astype(BF16)

    return pl.pallas_call(
        body, name=name,
        grid_spec=pltpu.PrefetchScalarGridSpec(
            num_scalar_prefetch=1, grid=(r // tr,),
            in_specs=[pl.BlockSpec((tr, c), lambda i, pref: (i, 0))],
            out_specs=pl.BlockSpec((None, tr, c), lambda i, pref: (pref[0], i, 0))),
        out_shape=jax.ShapeDtypeStruct((N_CHIPS, r, c), BF16),
        compiler_params=_cp(1),
    )(place, w)


def _gather_shards(name, bufs):
    n = len(bufs)

    def body(*refs):
        outs = refs[n:2 * n]
        send_sems, recv_sems = refs[2 * n:]
        x, y, c, chips = _place()
        my = 2 * x + y
        sibling = (x, y, 1 - c)

        def part(t, shard, half):
            h = bufs[t].shape[1] // 2
            return outs[t].at[shard, pl.ds(half * h, h)]

        def copy(t, k, ref, to):
            return pltpu.make_async_remote_copy(src_ref=ref, dst_ref=ref, send_sem=send_sems.at[t, k],
                                                recv_sem=recv_sems.at[t, k], device_id=to, device_id_type=MESH)

        sent = []
        for t in range(n):
            for k, chip in enumerate(chips):
                sent.append(copy(t, k, part(t, my, c), (*chip, c)))
                sent[-1].start()
        for k, chip in enumerate(chips):
            shard = 2 * chip[0] + chip[1]
            for t in range(n):
                copy(t, k, part(t, shard, c), (*chip, c)).wait_recv()
                sent.append(copy(t, 3 + k, part(t, shard, c), sibling))
                sent[-1].start()
        for k, chip in enumerate(chips):
            shard = 2 * chip[0] + chip[1]
            for t in range(n):
                copy(t, 3 + k, part(t, shard, 1 - c), sibling).wait_recv()
        for cp in sent:
            cp.wait_send()

    return pl.pallas_call(
        body, name=name, in_specs=[ANY] * n, out_specs=[ANY] * n,
        out_shape=[jax.ShapeDtypeStruct(b.shape, b.dtype) for b in bufs],
        input_output_aliases={t: t for t in range(n)},
        scratch_shapes=[pltpu.SemaphoreType.DMA((n, 6)), pltpu.SemaphoreType.DMA((n, 6))],
        compiler_params=_comm_params(),
    )(*bufs)


def _swap_halves(name, gs):
    n = len(gs)

    def body(*refs):
        ins, outs = refs[:n], refs[n:2 * n]
        send_sems, recv_sems = refs[2 * n:]
        x, y, c, _ = _place()
        cps = []
        for t in range(n):
            h = gs[t].shape[1] // 2
            cps.append(pltpu.make_async_remote_copy(
                src_ref=ins[t].at[:, pl.ds((1 - c) * h, h), :], dst_ref=outs[t], send_sem=send_sems.at[t],
                recv_sem=recv_sems.at[t], device_id=(x, y, 1 - c), device_id_type=MESH))
            cps[-1].start()
        for cp in cps:
            cp.wait()

    return pl.pallas_call(
        body, name=name, in_specs=[ANY] * n, out_specs=[ANY] * n,
        out_shape=[jax.ShapeDtypeStruct((g.shape[0], g.shape[1] // 2, g.shape[2]), g.dtype) for g in gs],
        scratch_shapes=[pltpu.SemaphoreType.DMA((n,)), pltpu.SemaphoreType.DMA((n,))],
        compiler_params=_comm_params(),
    )(*gs)


def _scatter_partials(name, ps):
    n = len(ps)

    def body(*refs):
        ins, outs = refs[:n], refs[n:2 * n]
        send_sems, recv_sems = refs[2 * n:]
        x, y, c, chips = _place()
        cps = []
        for t in range(n):
            for k, chip in enumerate(chips):
                shard = 2 * chip[0] + chip[1]
                cps.append(pltpu.make_async_remote_copy(
                    src_ref=ins[t].at[shard], dst_ref=outs[t].at[k], send_sem=send_sems.at[t, k],
                    recv_sem=recv_sems.at[t, k], device_id=(*chip, c), device_id_type=MESH))
                cps[-1].start()
        for cp in cps:
            cp.wait()

    return pl.pallas_call(
        body, name=name, in_specs=[ANY] * n, out_specs=[ANY] * n,
        out_shape=[jax.ShapeDtypeStruct((3,) + p.shape[1:], p.dtype) for p in ps],
        scratch_shapes=[pltpu.SemaphoreType.DMA((n, 3)), pltpu.SemaphoreType.DMA((n, 3))],
        compiler_params=_comm_params(),
    )(*ps)


def _send_half(name, fs):
    n = len(fs)

    def body(*refs):
        ins, outs = refs[:n], refs[n:2 * n]
        send_sems, recv_sems = refs[2 * n:]
        x, y, c, _ = _place()
        cps = []
        for t in range(n):
            cps.append(pltpu.make_async_remote_copy(
                src_ref=ins[t], dst_ref=outs[t], send_sem=send_sems.at[t], recv_sem=recv_sems.at[t],
                device_id=(x, y, 1 - c), device_id_type=MESH))
            cps[-1].start()
        for cp in cps:
            cp.wait()

    return pl.pallas_call(
        body, name=name, in_specs=[ANY] * n, out_specs=[ANY] * n,
        out_shape=[jax.ShapeDtypeStruct(f.shape, f.dtype) for f in fs],
        scratch_shapes=[pltpu.SemaphoreType.DMA((n,)), pltpu.SemaphoreType.DMA((n,))],
        compiler_params=_comm_params(),
    )(*fs)


def _gather_all(name, v):
    m_per = v.shape[0]

    def body(x_ref, out_ref, send_sems, recv_sems, local_sem):
        x, y, c, chips = _place()
        me, sibling = (x, y, c), (x, y, 1 - c)

        def rows(px, py, pc):
            return out_ref.at[pl.ds((4 * px + 2 * py + pc) * m_per, m_per), :]

        def copy(k, block, to, src=None):
            return pltpu.make_async_remote_copy(
                src_ref=rows(*block) if src is None else src, dst_ref=rows(*block), send_sem=send_sems.at[k],
                recv_sem=recv_sems.at[k], device_id=to, device_id_type=MESH)

        mine = pltpu.make_async_copy(x_ref, rows(*me), local_sem)
        mine.start()
        first = [copy(0, me, sibling, src=x_ref)]
        first += [copy(1 + j, me, (*chip, c), src=x_ref) for j, chip in enumerate(chips)]
        for cp in first:
            cp.start()
        passed = [copy(4 + j, (*chip, c), sibling) for j, chip in enumerate(chips)]
        for j, chip in enumerate(chips):
            copy(1 + j, (*chip, c), me).wait_recv()
            passed[j].start()
        copy(0, sibling, me).wait_recv()
        for j, chip in enumerate(chips):
            copy(4 + j, (*chip, 1 - c), me).wait_recv()
        for cp in first + passed:
            cp.wait_send()
        mine.wait()

    return pl.pallas_call(
        body, name=name, in_specs=[ANY], out_specs=ANY,
        out_shape=jax.ShapeDtypeStruct((N_DEV * m_per, v.shape[1]), v.dtype),
        scratch_shapes=[pltpu.SemaphoreType.DMA((7,)), pltpu.SemaphoreType.DMA((7,)), pltpu.SemaphoreType.DMA],
        compiler_params=_comm_params(),
    )(v)


def _pair_sum(name, g, recv, place):
    s, r, c = g.shape
    h = r // 2
    tr = _tile(h, max(V7X_SUBLANES * 2, (1 << 20) // (4 * c) // 16 * 16), 16)
    nb = h // tr

    def body(pref, g_ref, r_ref, p_ref, pb_ref):
        v = g_ref[...] + r_ref[...]
        p_ref[...] = v
        pb_ref[...] = v.astype(BF16)

    spec = pl.BlockSpec((None, tr, c), lambda k, i, pref: (k, i, 0))
    return pl.pallas_call(
        body, name=name,
        grid_spec=pltpu.PrefetchScalarGridSpec(
            num_scalar_prefetch=1, grid=(s, nb),
            in_specs=[pl.BlockSpec((None, tr, c), lambda k, i, pref: (k, pref[1] * nb + i, 0)), spec],
            out_specs=[spec, spec]),
        out_shape=[jax.ShapeDtypeStruct((s, h, c), F32), jax.ShapeDtypeStruct((s, h, c), BF16)],
        compiler_params=_cp(2),
    )(place, g, recv)


def _chip_sum(name, p, recv, place):
    s, h, c = p.shape
    tr = _tile(h, max(V7X_SUBLANES * 2, (1 << 20) // (4 * c) // 16 * 16), 16)

    def body(pref, p_ref, r_ref, o_ref):
        acc = p_ref[...]
        for k in range(3):
            acc = acc + r_ref[k].astype(F32)
        o_ref[...] = acc

    return pl.pallas_call(
        body, name=name,
        grid_spec=pltpu.PrefetchScalarGridSpec(
            num_scalar_prefetch=1, grid=(h // tr,),
            in_specs=[pl.BlockSpec((None, tr, c), lambda i, pref: (pref[0], i, 0)),
                      pl.BlockSpec((3, tr, c), lambda i, pref: (0, i, 0))],
            out_specs=pl.BlockSpec((tr, c), lambda i, pref: (i, 0))),
        out_shape=jax.ShapeDtypeStruct((h, c), F32),
        compiler_params=_cp(1),
    )(place, p, recv)


def _ffn_fwd(tag, alpha, x, w_in4, w_out, g, b):
    h = _mm_shards_nn(f"{tag}_in", x, w_in4, BF16)
    a = _swiglu_fwd(f"{tag}_act", h)
    f = _mm_nn(f"{tag}_out", a, w_out)
    y, xhat, rstd = _ln_fwd(f"{tag}_ln", alpha, x, f, 0.5, g, b)
    return y, (x, h, a, xhat, rstd)


def _ffn_bwd(tag, alpha, terms, saved, w_in4, w_out, g):
    x, h, a, xhat, rstd = saved
    dz, dg, db = _ln_bwd(f"{tag}_ln_bwd", terms, xhat, rstd, g)
    d_wout = _mm_tn(f"{tag}_dwout", a, dz, tm=_tile(a.shape[1], 1408, V7X_LANES), scale=0.5)
    da = _mm_nt(f"{tag}_da", dz, w_out, o_dtype=BF16, tn=_tile(w_out.shape[0], 1408, V7X_LANES), scale=0.5)
    dh = _swiglu_bwd(f"{tag}_act_bwd", da, h)
    d_win4 = _mm_shards_tn(f"{tag}_dwin", x, dh, w_in4.shape[0])
    dx = _mm_shards_nt(f"{tag}_dx", dh, w_in4)
    return [(dz, alpha), (dx, 1.0)], d_win4, d_wout, dg, db


def _heads(a, nh, tb):
    t = a.shape[0]
    return a.reshape(t, nh, -1).transpose(1, 0, 2).reshape(nh, t // tb, tb, -1)


def _unheads(a):
    nh, nb, tb, hd = a.shape
    return a.reshape(nh, nb * tb, hd).transpose(1, 0, 2).reshape(nb * tb, nh * hd)


def _fox_fwd(tag, x, w_pad, bf, w_o):
    t, d = x.shape
    nh = bf.shape[1]
    hd = d // nh
    tb = _tile(t, 512)
    scale = 1.0 / math.sqrt(hd)
    proj = _mm_nn(f"{tag}_proj", x, w_pad, tn=_tile(w_pad.shape[1], 640, V7X_LANES))
    q = _heads((proj[:, :d].astype(BF16) * scale).astype(BF16), nh, tb)
    k = _heads(proj[:, d:2 * d].astype(BF16), nh, tb)
    v = _heads(proj[:, 2 * d:3 * d].astype(BF16), nh, tb)
    fl = proj[:, 3 * d:3 * d + nh]
    cum = _fox_cumsum(f"{tag}_cum", fl, bf)
    ccol = cum.T.reshape(nh, t // tb, tb, 1)
    crow = cum.T.reshape(nh, t // tb, 1, tb)
    o, lse = _attn_fwd(f"{tag}_attn", q, k, v, ccol, crow)
    o2 = _unheads(o)
    m = _mm_nn(f"{tag}_oproj", o2, w_o)
    return m, (x, q, k, v, ccol, crow, o, lse, o2, fl, scale)


def _fox_bwd(tag, dm, saved, w_pad, bf, w_o):
    x, q, k, v, ccol, crow, o, lse, o2, fl, scale = saved
    t, d = x.shape
    nh, nb, tb, hd = q.shape
    d_wo = _mm_tn(f"{tag}_dwo", o2, dm)
    do2 = _mm_nt(f"{tag}_do", dm, w_o)
    do = _heads(do2, nh, tb)
    dq, dk, dv, drow, dcol = _attn_bwd(f"{tag}_attn_bwd", q, k, v, ccol, crow, o, lse, do, scale)
    dcum = (drow.reshape(nh, t) - dcol.reshape(nh, t)).T
    dfl, dbf = _fox_cumsum_bwd(f"{tag}_cum_bwd", dcum, fl, bf)
    pad = w_pad.shape[1] - 3 * d - nh
    dproj = jnp.concatenate([_unheads(dq).astype(BF16), _unheads(dk).astype(BF16), _unheads(dv).astype(BF16),
                             dfl.astype(BF16), jnp.zeros((t, pad), BF16)], axis=1)
    d_wpad = _mm_tn(f"{tag}_dwin", x, dproj, tn=_tile(w_pad.shape[1], 640, V7X_LANES))
    dx = _mm_nt(f"{tag}_dx", dproj, w_pad)
    return dx, d_wpad, dbf, d_wo


def _to_segments(a):
    t, d = a.shape
    return a.reshape(SCAN_SEGMENTS, t // SCAN_SEGMENTS, d).transpose(1, 0, 2).reshape(t, d)


def _from_segments(a):
    t, d = a.shape
    return a.reshape(t // SCAN_SEGMENTS, SCAN_SEGMENTS, d).transpose(1, 0, 2).reshape(t, d)


def _s5_discretise(a_re, a_im, log_dt, b_re, b_im):
    dt = jnp.exp(log_dt)[:, None]
    mag = jnp.exp(a_re * dt)
    ang = a_im * dt
    lb_re = mag * jnp.cos(ang)
    lb_im = mag * jnp.sin(ang)
    den = a_re * a_re + a_im * a_im
    nr = lb_re - 1.0
    ni = lb_im
    z_re = (nr * a_re + ni * a_im) / den
    z_im = (ni * a_re - nr * a_im) / den
    bb_re = z_re[..., None] * b_re - z_im[..., None] * b_im
    bb_im = z_re[..., None] * b_im + z_im[..., None] * b_re
    return lb_re, lb_im, bb_re, bb_im


S5_BLOCK_GROUPS = 8


def _blockdiag_in(bb):
    g, p, h = bb.shape
    e = jnp.eye(S5_BLOCK_GROUPS, dtype=bb.dtype)
    b4 = bb.reshape(g // S5_BLOCK_GROUPS, S5_BLOCK_GROUPS, p, h)
    return jnp.einsum("jgph,gf->jghfp", b4, e).reshape(g // S5_BLOCK_GROUPS, S5_BLOCK_GROUPS * h, S5_BLOCK_GROUPS * p)


def _blockdiag_in_grad(d):
    nj, gh, gp = d.shape
    h, p = gh // S5_BLOCK_GROUPS, gp // S5_BLOCK_GROUPS
    e = jnp.eye(S5_BLOCK_GROUPS, dtype=d.dtype)
    d6 = d.reshape(nj, S5_BLOCK_GROUPS, h, S5_BLOCK_GROUPS, p)
    return jnp.einsum("jghfp,gf->jgph", d6, e).reshape(nj * S5_BLOCK_GROUPS, p, h)


def _blockdiag_out(cc):
    g, h, p = cc.shape
    e = jnp.eye(S5_BLOCK_GROUPS, dtype=cc.dtype)
    c4 = cc.reshape(g // S5_BLOCK_GROUPS, S5_BLOCK_GROUPS, h, p)
    return jnp.einsum("jghp,gf->jfpgh", c4, e).reshape(g // S5_BLOCK_GROUPS, S5_BLOCK_GROUPS * p, S5_BLOCK_GROUPS * h)


def _blockdiag_out_grad(d):
    nj, gp, gh = d.shape
    h, p = gh // S5_BLOCK_GROUPS, gp // S5_BLOCK_GROUPS
    e = jnp.eye(S5_BLOCK_GROUPS, dtype=d.dtype)
    d6 = d.reshape(nj, S5_BLOCK_GROUPS, p, S5_BLOCK_GROUPS, h)
    return jnp.einsum("jfpgh,gf->jghp", d6, e).reshape(nj * S5_BLOCK_GROUPS, h, p)


def _s5_fwd(tag, x, prm, w_out4):
    a_re, a_im, log_dt, b_re, b_im, c_re, c_im, d_skip = prm
    t, d = x.shape
    g, p = a_re.shape
    w = g * p
    nj = g // S5_BLOCK_GROUPS
    cw, sw = S5_BLOCK_GROUPS * S5_GROUP, S5_BLOCK_GROUPS * p
    seg = t // SCAN_SEGMENTS
    tm = _tile(t, 4096)
    lb_re, lb_im, bb_re, bb_im = _s5_discretise(a_re, a_im, log_dt, b_re, b_im)
    lam = jnp.stack([lb_re.reshape(1, w), lb_im.reshape(1, w)])
    bs = jnp.stack([_blockdiag_in(bb_re), _blockdiag_in(bb_im)]).astype(BF16)
    cs = jnp.stack([_blockdiag_out(c_re), -_blockdiag_out(c_im)]).astype(BF16)
    dvec = d_skip.reshape(1, d)
    u = _to_segments(x)
    bu = _mm(f"{tag}_bu", u, bs, mode="nn", grid=(2, nj, t // tm), kaxis=None,
             a_blk=(tm, cw), a_map=lambda r, j, i: (i, j),
             b_blk=(None, None, cw, sw), b_map=lambda r, j, i: (r, j, 0, 0),
             o_shape=(2, t, w), o_blk=(None, tm, sw), o_map=lambda r, j, i: (r, i, j))
    hs = _s5_scan(f"{tag}_scan", lam, bu.reshape(2, seg, SCAN_SEGMENTS, w)).reshape(2, t, w)
    ych = _mm(f"{tag}_ch", hs, cs, mode="nn", grid=(nj, t // tm, 2), kaxis=2,
              a_blk=(None, tm, sw), a_map=lambda j, i, r: (r, i, j),
              b_blk=(None, None, sw, cw), b_map=lambda j, i, r: (r, j, 0, 0),
              o_shape=(t, d), o_blk=(tm, cw), o_map=lambda j, i, r: (i, j))
    ypre, act = _s5_act_fwd(f"{tag}_act", ych, u, dvec)
    vg = _mm_shards_nn(f"{tag}_wout", act, w_out4, F32)
    m = _from_segments(_glu_fwd(f"{tag}_glu", vg))
    return m, (u, lam, bs, cs, dvec, hs, ypre, act, vg)


def _s5_bwd(tag, dm, saved, prm, w_out4):
    a_re, a_im, log_dt, b_re, b_im, c_re, c_im, d_skip = prm
    u, lam, bs, cs, dvec, hs, ypre, act, vg = saved
    t, d = u.shape
    g, p = a_re.shape
    w = g * p
    nj = g // S5_BLOCK_GROUPS
    cw, sw = S5_BLOCK_GROUPS * S5_GROUP, S5_BLOCK_GROUPS * p
    seg = t // SCAN_SEGMENTS
    tm = _tile(t, 4096)
    dvg = _glu_bwd(f"{tag}_glu_bwd", _to_segments(dm), vg)
    d_wout4 = _mm_shards_tn(f"{tag}_dwout", act, dvg, w_out4.shape[0])
    dact = _mm_shards_nt(f"{tag}_dact", dvg, w_out4)
    dypre, duskip, dd = _s5_act_bwd(f"{tag}_act_bwd", dact, ypre, u, dvec)
    dh = _mm(f"{tag}_dh", dypre, cs, mode="nt", grid=(2, nj, t // tm), kaxis=None,
             a_blk=(tm, cw), a_map=lambda r, j, i: (i, j),
             b_blk=(None, None, sw, cw), b_map=lambda r, j, i: (r, j, 0, 0),
             o_shape=(2, t, w), o_blk=(None, tm, sw), o_map=lambda r, j, i: (r, i, j))
    dcs = _mm(f"{tag}_dc", hs, dypre, mode="tn", grid=(2, nj, t // tm), kaxis=2,
              a_blk=(None, tm, sw), a_map=lambda r, j, i: (r, i, j),
              b_blk=(tm, cw), b_map=lambda r, j, i: (i, j),
              o_shape=(2, nj, sw, cw), o_blk=(None, None, sw, cw), o_map=lambda r, j, i: (r, j, 0, 0))
    gs, dlam8 = _s5_scan(f"{tag}_scan_bwd", lam, dh.reshape(2, seg, SCAN_SEGMENTS, w),
                         hs.reshape(2, seg, SCAN_SEGMENTS, w))
    gs = gs.reshape(2, t, w)
    du = _mm(f"{tag}_du", gs, bs, mode="nt", grid=(nj, t // tm, 2), kaxis=2,
             a_blk=(None, tm, sw), a_map=lambda j, i, r: (r, i, j),
             b_blk=(None, None, cw, sw), b_map=lambda j, i, r: (r, j, 0, 0),
             o_shape=(t, d), o_blk=(tm, cw), o_map=lambda j, i, r: (i, j))
    dbs = _mm(f"{tag}_db", u, gs, mode="tn", grid=(2, nj, t // tm), kaxis=2,
              a_blk=(tm, cw), a_map=lambda r, j, i: (i, j),
              b_blk=(None, tm, sw), b_map=lambda r, j, i: (r, i, j),
              o_shape=(2, nj, cw, sw), o_blk=(None, None, cw, sw), o_map=lambda r, j, i: (r, j, 0, 0))
    dx = _from_segments(du + duskip)
    dlam = jnp.sum(dlam8, axis=1).reshape(2, g, p)
    small = dict(dlb_re=dlam[0], dlb_im=dlam[1],
                 dbb_re=_blockdiag_in_grad(dbs[0]), dbb_im=_blockdiag_in_grad(dbs[1]),
                 dc_re=_blockdiag_out_grad(dcs[0]), dc_im=-_blockdiag_out_grad(dcs[1]),
                 dd=dd.reshape(g, S5_GROUP))
    return dx, d_wout4, small


def _pack(pieces):
    flat = jnp.concatenate([p.reshape(-1).astype(F32) for p in pieces])
    n = flat.shape[0]
    unit = V7X_SUBLANES * V7X_LANES
    total = -(-n // unit) * unit
    return jnp.pad(flat, (0, total - n)).reshape(total // V7X_LANES, V7X_LANES)


def _unpack(buf, shapes):
    flat = buf.reshape(-1)
    out, off = [], 0
    for s in shapes:
        n = math.prod(s)
        out.append(flat[off:off + n].reshape(s))
        off += n
    return out


def kernel(x, ffn1_w_in, ffn1_w_out, ln1_g, ln1_b, lnm_g, lnm_b, ffn2_w_in, ffn2_w_out, ln2_g, ln2_b, fox_w_in, fox_b_f, fox_w_o, s5_a_re, s5_a_im, s5_log_dt, s5_b_re, s5_b_im, s5_c_re, s5_c_im, s5_d, s5_w_out, loss_target, m_ffn1_w_in, m_ffn1_w_out, m_ln1_g, m_ln1_b, m_lnm_g, m_lnm_b, m_ffn2_w_in, m_ffn2_w_out, m_ln2_g, m_ln2_b, m_fox_w_in, m_fox_b_f, m_fox_w_o, m_s5_a_re, m_s5_a_im, m_s5_log_dt, m_s5_b_re, m_s5_b_im, m_s5_c_re, m_s5_c_im, m_s5_d, m_s5_w_out, v_ffn1_w_in, v_ffn1_w_out, v_ln1_g, v_ln1_b, v_lnm_g, v_lnm_b, v_ffn2_w_in, v_ffn2_w_out, v_ln2_g, v_ln2_b, v_fox_w_in, v_fox_b_f, v_fox_w_o, v_s5_a_re, v_s5_a_im, v_s5_log_dt, v_s5_b_re, v_s5_b_im, v_s5_c_re, v_s5_c_im, v_s5_d, v_s5_w_out):
    big_names = ["ffn1_w_in", "ffn1_w_out", "ffn2_w_in", "ffn2_w_out", "fox_w_in", "fox_w_o", "s5_w_out"]
    small_names = ["ln1_g", "ln1_b", "lnm_g", "lnm_b", "ln2_g", "ln2_b", "fox_b_f", "s5_a_re", "s5_a_im", "s5_log_dt",
                   "s5_b_re", "s5_b_im", "s5_c_re", "s5_c_im", "s5_d"]
    out_order = ["ffn1_w_in", "ffn1_w_out", "ln1_g", "ln1_b", "lnm_g", "lnm_b", "ffn2_w_in", "ffn2_w_out", "ln2_g",
                 "ln2_b", "fox_w_in", "fox_b_f", "fox_w_o", "s5_a_re", "s5_a_im", "s5_log_dt", "s5_b_re", "s5_b_im",
                 "s5_c_re", "s5_c_im", "s5_d", "s5_w_out"]
    env = dict(locals())
    w = {n: env[n] for n in out_order}
    mom = {n: env["m_" + n] for n in out_order}
    vel = {n: env["v_" + n] for n in out_order}

    depth, d = ln1_g.shape
    t = x.shape[1]
    alpha = (2.0 * depth) ** 0.25
    x0 = x.reshape(t, d)
    tgt = loss_target.reshape(t, d)
    my_x, my_y, my_c = lax.axis_index("x"), lax.axis_index("y"), lax.axis_index("c")
    place = jnp.stack([2 * my_x + my_y, my_c]).astype(jnp.int32)

    flat = {n: w[n].reshape(-1, w[n].shape[-1]) for n in big_names}
    gathered = _gather_shards("gather_weights", [_cast_place(f"cast_{n}", flat[n], place) for n in big_names])
    full = dict(zip(big_names, gathered))

    def layer_cols(name, i):
        rows = w[name].shape[1]
        return full[name][:, i * rows:(i + 1) * rows, :]

    def layer_rows(name, i):
        rows = w[name].shape[1]
        return full[name][:, i * rows:(i + 1) * rows, :].reshape(N_CHIPS * rows, -1)

    nh = fox_b_f.shape[1]
    fox_cols = 3 * d + nh
    fox_pad = -(-fox_cols // (5 * V7X_LANES)) * (5 * V7X_LANES)

    def fox_wpad(j):
        w4 = layer_cols("fox_w_in", j)
        wf = w4.transpose(1, 0, 2).reshape(d, fox_cols)
        return jnp.pad(wf, ((0, 0), (0, fox_pad - fox_cols)))

    def s5_params(j):
        return (s5_a_re[j], s5_a_im[j], s5_log_dt[j], s5_b_re[j], s5_b_im[j], s5_c_re[j], s5_c_im[j], s5_d[j])

    saved = []
    h = x0
    for i in range(depth):
        j = i // 2
        h, s1 = _ffn_fwd(f"l{i}_ffn1", alpha, h, layer_cols("ffn1_w_in", i), layer_rows("ffn1_w_out", i),
                         ln1_g[i:i + 1], ln1_b[i:i + 1])
        if i % 2 == 0:
            m, sm = _fox_fwd(f"l{i}_fox", h, fox_wpad(j), fox_b_f[j:j + 1], layer_rows("fox_w_o", j))
        else:
            m, sm = _s5_fwd(f"l{i}_s5", h, s5_params(j), layer_cols("s5_w_out", j))
        h, xhat_m, rstd_m = _ln_fwd(f"l{i}_lnm", alpha, h, m, 1.0, lnm_g[i:i + 1], lnm_b[i:i + 1])
        h, s2 = _ffn_fwd(f"l{i}_ffn2", alpha, h, layer_cols("ffn2_w_in", i), layer_rows("ffn2_w_out", i),
                         ln2_g[i:i + 1], ln2_b[i:i + 1])
        saved.append((s1, sm, (xhat_m, rstd_m), s2))
    loss_part = _loss_sum("loss", h, tgt) * (0.5 / d)

    gbig = {n: [None] * w[n].shape[0] for n in big_names}
    gsmall = {n: [None] * w[n].shape[0] for n in small_names}
    s5_cot = [None] * s5_a_re.shape[0]
    terms = [(h, 1.0 / d), (tgt, -1.0 / d)]
    for i in reversed(range(depth)):
        j = i // 2
        s1, sm, (xhat_m, rstd_m), s2 = saved[i]
        terms, dwin, dwout, dg, db = _ffn_bwd(f"l{i}_ffn2", alpha, terms, s2, layer_cols("ffn2_w_in", i),
                                              layer_rows("ffn2_w_out", i), ln2_g[i:i + 1])
        gbig["ffn2_w_in"][i], gbig["ffn2_w_out"][i] = dwin, dwout.reshape(N_CHIPS, -1, d)
        gsmall["ln2_g"][i], gsmall["ln2_b"][i] = dg, db
        dz, dg, db = _ln_bwd(f"l{i}_lnm_bwd", terms, xhat_m, rstd_m, lnm_g[i:i + 1])
        gsmall["lnm_g"][i], gsmall["lnm_b"][i] = dg, db
        if i % 2 == 0:
            dx, d_wpad, dbf, d_wo = _fox_bwd(f"l{i}_fox", dz, sm, fox_wpad(j), fox_b_f[j:j + 1], layer_rows("fox_w_o", j))
            gbig["fox_w_in"][j] = d_wpad[:, :fox_cols].reshape(d, N_CHIPS, -1).transpose(1, 0, 2)
            gbig["fox_w_o"][j] = d_wo.reshape(N_CHIPS, -1, d)
            gsmall["fox_b_f"][j] = dbf
        else:
            dx, d_wout4, s5_cot[j] = _s5_bwd(f"l{i}_s5", dz, sm, s5_params(j), layer_cols("s5_w_out", j))
            gbig["s5_w_out"][j] = d_wout4
        terms = [(dz, alpha), (dx, 1.0)]
        terms, dwin, dwout, dg, db = _ffn_bwd(f"l{i}_ffn1", alpha, terms, s1, layer_cols("ffn1_w_in", i),
                                              layer_rows("ffn1_w_out", i), ln1_g[i:i + 1])
        gbig["ffn1_w_in"][i], gbig["ffn1_w_out"][i] = dwin, dwout.reshape(N_CHIPS, -1, d)
        gsmall["ln1_g"][i], gsmall["ln1_b"][i] = dg, db
    grad_x = _lincomb("grad_x", terms).reshape(x.shape)

    g4 = [jnp.concatenate(gbig[n], axis=1) for n in big_names]
    from_sibling = _swap_halves("grad_pair_swap", g4)
    pair = [_pair_sum(f"grad_pair_sum_{n}", g, r, place) for n, g, r in zip(big_names, g4, from_sibling)]
    from_chips = _scatter_partials("grad_chip_scatter", [pb for _, pb in pair])
    halves = [_chip_sum(f"grad_chip_sum_{n}", p, r, place) for n, (p, _), r in zip(big_names, pair, from_chips)]
    theirs = _send_half("grad_send_half", halves)

    cot_names = ["dlb_re", "dlb_im", "dbb_re", "dbb_im", "dc_re", "dc_im", "dd"]
    ln_names = ["ln1_g", "ln1_b", "lnm_g", "lnm_b", "ln2_g", "ln2_b"]
    pieces = [loss_part] + [jnp.concatenate(gsmall[n], axis=0) for n in ln_names + ["fox_b_f"]]
    pieces += [jnp.stack([s5_cot[j][n] for j in range(len(s5_cot))]) for n in cot_names]
    shapes = [p.shape for p in pieces]
    mine = _pack(pieces)
    everyone = _gather_all("small_gather", mine).reshape(N_DEV, *mine.shape)
    summed = _unpack(_sum_leading("small_sum", everyone), shapes)
    loss = summed[0].reshape(())
    gs_final = dict(zip(ln_names + ["fox_b_f"], summed[1:8]))
    cot = dict(zip(cot_names, summed[8:]))
    prm_names = ["s5_a_re", "s5_a_im", "s5_log_dt", "s5_b_re", "s5_b_im"]
    _, disc_vjp = jax.vjp(jax.vmap(_s5_discretise), *[w[n] for n in prm_names])
    for n, gval in zip(prm_names, disc_vjp((cot["dlb_re"], cot["dlb_im"], cot["dbb_re"], cot["dbb_im"]))):
        gs_final[n] = gval
    gs_final["s5_c_re"], gs_final["s5_c_im"], gs_final["s5_d"] = cot["dc_re"], cot["dc_im"], cot["dd"]

    grads, deltas, new_m, new_v = {}, {}, {}, {}
    for n, mine_h, their_h in zip(big_names, halves, theirs):
        res = _adamw_join(f"adamw_{n}", flat[n], mine_h, their_h, mom[n].reshape(flat[n].shape),
                          vel[n].reshape(flat[n].shape), place)
        grads[n], deltas[n], new_m[n], new_v[n] = [a.reshape(w[n].shape) for a in res]
    small_shapes = [w[n].shape for n in small_names]
    for n in small_names:
        grads[n] = gs_final[n].reshape(w[n].shape)
    packed = [_pack([src[n] for n in small_names]) for src in (w, grads, mom, vel)]
    for dst, buf in zip((deltas, new_m, new_v), _adamw("adamw_small", *packed)):
        for n, val in zip(small_names, _unpack(buf, small_shapes)):
            dst[n] = val
    return (loss, grad_x, *[grads[n] for n in out_order], *[deltas[n] for n in out_order],
            *[new_m[n] for n in out_order], *[new_v[n] for n in out_order])
```

```python
import functools
import math

import jax
import jax.numpy as jnp
from jax import lax
from jax.experimental import pallas as pl
from jax.experimental.pallas import tpu as pltpu

F32 = jnp.float32
BF16 = jnp.bfloat16
LN_EPS = 1e-5
NEG_INF = -1e30
ADAM_LR = 0.001
ADAM_B1 = 0.9
ADAM_B2 = 0.999
ADAM_EPS = 1e-08
ADAM_WD = 0.01
ADAM_STEP = 10
S5_GROUP = 16
SCAN_SEGMENTS = 8
V7X_SUBLANES = 8
V7X_LANES = 128
VMEM_LIMIT = 56 * 1024 * 1024
N_CHIPS = 4
N_DEV = 8
MESH = pl.DeviceIdType.MESH
ANY = pl.BlockSpec(memory_space=pl.ANY)


def _cp(n_grid, kaxis=None):
    sem = tuple("arbitrary" if (kaxis is None or i == kaxis) else "parallel" for i in range(n_grid))
    return pltpu.CompilerParams(dimension_semantics=sem, vmem_limit_bytes=VMEM_LIMIT)


def _tile(n, pref, mult=V7X_SUBLANES):
    if n <= pref:
        return n
    for t in range(pref, 0, -1):
        if n % t == 0 and t % mult == 0:
            return t
    return n


_CONTRACT = {"nn": ((1,), (0,)), "nt": ((1,), (1,)), "tn": ((0,), (0,))}


def _mm(name, a, b, *, mode, grid, kaxis, a_blk, a_map, b_blk, b_map, o_shape, o_blk, o_map, o_dtype=F32, scale=None,
        into=None):
    nk = 1 if kaxis is None else grid[kaxis]
    assert kaxis is None or kaxis == len(grid) - 1
    dims = (_CONTRACT[mode], ((), ()))
    use_acc = nk > 1 and o_dtype != F32
    acc_shape = tuple(d for d in o_blk if d is not None)

    def body(a_ref, b_ref, *rest):
        o_ref, scratch = (rest[1], rest[2:]) if into is not None else (rest[0], rest[1:])
        p = lax.dot_general(a_ref[...].astype(BF16), b_ref[...].astype(BF16), dims, preferred_element_type=F32)
        if nk == 1:
            if scale is not None:
                p = p * scale
            o_ref[...] = p.astype(o_dtype)
            return
        acc = scratch[0] if use_acc else o_ref
        k = pl.program_id(kaxis)

        @pl.when(k == 0)
        def _():
            acc[...] = p

        @pl.when(k > 0)
        def _():
            acc[...] += p

        if use_acc or scale is not None:
            @pl.when(k == nk - 1)
            def _():
                r = acc[...]
                if scale is not None:
                    r = r * scale
                o_ref[...] = r.astype(o_dtype)

    in_specs = [pl.BlockSpec(a_blk, a_map), pl.BlockSpec(b_blk, b_map)]
    args = [a, b]
    if into is not None:
        assert into.shape == tuple(o_shape) and into.dtype == o_dtype
        in_specs.append(ANY)
        args.append(into)
    return pl.pallas_call(
        body, grid=grid, name=name, in_specs=in_specs,
        out_specs=pl.BlockSpec(o_blk, o_map),
        out_shape=jax.ShapeDtypeStruct(o_shape, o_dtype),
        input_output_aliases={2: 0} if into is not None else {},
        scratch_shapes=[pltpu.VMEM(acc_shape, F32)] if use_acc else [],
        compiler_params=_cp(len(grid), kaxis),
    )(*args)


def _mm_shards_nn(name, a, wall, layer, o_dtype):
    t, k = a.shape
    _, s, _, n = wall.shape
    tm = _tile(t, 512)
    return _mm(name, a, wall, mode="nn", grid=(s, t // tm), kaxis=None,
               a_blk=(tm, k), a_map=lambda j, i: (i, 0),
               b_blk=(None, None, k, n), b_map=lambda j, i: (layer, j, 0, 0),
               o_shape=(t, s * n), o_blk=(tm, n), o_map=lambda j, i: (i, j), o_dtype=o_dtype)


def _mm_shards_nt(name, g, wall, layer):
    t = g.shape[0]
    _, s, k, n = wall.shape
    tm = _tile(t, 512)
    return _mm(name, g, wall, mode="nt", grid=(t // tm, s), kaxis=1,
               a_blk=(tm, n), a_map=lambda i, kk: (i, kk),
               b_blk=(None, None, k, n), b_map=lambda i, kk: (layer, kk, 0, 0),
               o_shape=(t, k), o_blk=(tm, k), o_map=lambda i, kk: (i, 0))


def _mm_shards_tn(name, a, g, layer, shape, into):
    t, k = a.shape
    _, s, _, n = shape
    tk = _tile(t, 512)
    return _mm(name, a, g, mode="tn", grid=(s, t // tk), kaxis=1,
               a_blk=(tk, k), a_map=lambda j, kk: (kk, 0),
               b_blk=(tk, n), b_map=lambda j, kk: (kk, j),
               o_shape=shape, o_blk=(None, None, k, n), o_map=lambda j, kk: (layer, j, 0, 0), into=into)


def _mm_nn(name, a, w, o_dtype=F32, tn=None):
    t, k = a.shape
    n = w.shape[1]
    tm = _tile(t, 512)
    tn = n if tn is None else tn
    return _mm(name, a, w, mode="nn", grid=(n // tn, t // tm), kaxis=None,
               a_blk=(tm, k), a_map=lambda j, i: (i, 0),
               b_blk=(k, tn), b_map=lambda j, i: (0, j),
               o_shape=(t, n), o_blk=(tm, tn), o_map=lambda j, i: (i, j), o_dtype=o_dtype)


def _mm_nt(name, g, w, layer=None, o_dtype=F32):
    t, k = g.shape
    n = w.shape[-2]
    tm = _tile(t, 512)
    b_blk, b_map = ((n, k), lambda i: (0, 0)) if layer is None else ((None, n, k), lambda i: (layer, 0, 0))
    return _mm(name, g, w, mode="nt", grid=(t // tm,), kaxis=None,
               a_blk=(tm, k), a_map=lambda i: (i, 0), b_blk=b_blk, b_map=b_map,
               o_shape=(t, n), o_blk=(tm, n), o_map=lambda i: (i, 0), o_dtype=o_dtype)


def _mm_tn(name, a, g, tm=None, tn=None, scale=None, layer=None, shape=None, into=None):
    t, m = a.shape
    n = g.shape[1]
    tk = _tile(t, 512)
    tm = m if tm is None else tm
    tn = n if tn is None else tn
    if layer is None:
        o_shape, o_blk, o_map = (m, n), (tm, tn), lambda i, j, kk: (i, j)
    else:
        o_shape, o_blk, o_map = shape, (None, tm, tn), lambda i, j, kk: (layer, i, j)
    return _mm(name, a, g, mode="tn", grid=(m // tm, n // tn, t // tk), kaxis=2,
               a_blk=(tk, tm), a_map=lambda i, j, kk: (kk, i),
               b_blk=(tk, tn), b_map=lambda i, j, kk: (kk, j),
               o_shape=o_shape, o_blk=o_blk, o_map=o_map, scale=scale, into=into)


def _sigmoid(x):
    return 1.0 / (1.0 + jnp.exp(-x))


def _rows_call(name, body, t, tm, ins, in_cols, outs, acc_outs=()):
    in_specs = []
    for x, c in zip(ins, in_cols):
        if x.shape[0] == 1:
            in_specs.append(pl.BlockSpec((1, c), lambda i: (0, 0)))
        else:
            in_specs.append(pl.BlockSpec((tm, c), lambda i: (i, 0)))
    out_specs = [pl.BlockSpec((tm, s.shape[1]), lambda i: (i, 0)) for s in outs]
    out_specs += [pl.BlockSpec((1, s.shape[1]), lambda i: (0, 0)) for s in acc_outs]
    return pl.pallas_call(
        body, grid=(t // tm,), name=name, in_specs=in_specs, out_specs=out_specs,
        out_shape=list(outs) + list(acc_outs), compiler_params=_cp(1),
    )(*ins)


def _ln_fwd(name, alpha, x, r, coef, g, b):
    t, d = x.shape
    tm = _tile(t, 256)

    def body(x_ref, r_ref, g_ref, b_ref, y_ref, xh_ref, rs_ref):
        z = alpha * x_ref[...] + coef * r_ref[...]
        mu = jnp.mean(z, axis=-1, keepdims=True)
        zc = z - mu
        var = jnp.mean(zc * zc, axis=-1, keepdims=True)
        rstd = lax.rsqrt(var + LN_EPS)
        xh = zc * rstd
        y_ref[...] = xh * g_ref[...] + b_ref[...]
        xh_ref[...] = xh
        rs_ref[...] = rstd

    sd = jax.ShapeDtypeStruct
    return _rows_call(name, body, t, tm, [x, r, g, b], [d, d, d, d],
                      [sd((t, d), F32), sd((t, d), F32), sd((t, 1), F32)])


def _ln_bwd(name, terms, xhat, rstd, g):
    t, d = xhat.shape
    tm = _tile(t, 256)
    n = len(terms)
    coefs = [c for _, c in terms]

    def body(*refs):
        t_refs = refs[:n]
        xh_ref, rs_ref, g_ref, dz_ref, dg_ref, db_ref = refs[n:]
        dy = coefs[0] * t_refs[0][...]
        for c, r in zip(coefs[1:], t_refs[1:]):
            dy = dy + c * r[...]
        xh = xh_ref[...]
        dxh = dy * g_ref[...]
        m1 = jnp.mean(dxh, axis=-1, keepdims=True)
        m2 = jnp.mean(dxh * xh, axis=-1, keepdims=True)
        dz_ref[...] = rs_ref[...] * (dxh - m1 - xh * m2)
        pg = jnp.sum(dy * xh, axis=0, keepdims=True)
        pb = jnp.sum(dy, axis=0, keepdims=True)
        i = pl.program_id(0)

        @pl.when(i == 0)
        def _():
            dg_ref[...] = pg
            db_ref[...] = pb

        @pl.when(i > 0)
        def _():
            dg_ref[...] += pg
            db_ref[...] += pb

    sd = jax.ShapeDtypeStruct
    arrs = [a for a, _ in terms] + [xhat, rstd, g]
    cols = [d] * n + [d, 1, d]
    return _rows_call(name, body, t, tm, arrs, cols, [sd((t, d), F32)], [sd((1, d), F32), sd((1, d), F32)])


def _lincomb(name, terms):
    t, d = terms[0][0].shape
    tm = _tile(t, 256)
    coefs = [c for _, c in terms]
    n = len(terms)

    def body(*refs):
        acc = coefs[0] * refs[0][...]
        for c, r in zip(coefs[1:], refs[1:n]):
            acc = acc + c * r[...]
        refs[n][...] = acc

    return _rows_call(name, body, t, tm, [a for a, _ in terms], [d] * n, [jax.ShapeDtypeStruct((t, d), F32)])[0]


def _loss_sum(name, y, tgt):
    t, d = y.shape
    tm = _tile(t, 256)

    def body(y_ref, t_ref, o_ref):
        e = y_ref[...] - t_ref[...]
        s = jnp.sum(jnp.sum(e * e, axis=1, keepdims=True), axis=0, keepdims=True)
        i = pl.program_id(0)

        @pl.when(i == 0)
        def _():
            o_ref[...] = s

        @pl.when(i > 0)
        def _():
            o_ref[...] += s

    return _rows_call(name, body, t, tm, [y, tgt], [d, d], [], [jax.ShapeDtypeStruct((1, 1), F32)])[0]


def _ffn_in(name, x, wall, layer):
    t, k = x.shape
    n = wall.shape[3]
    tm = _tile(t, 512)

    def body(x_ref, wg_ref, wu_ref, h_ref, a_ref):
        xb = x_ref[...].astype(BF16)
        g = lax.dot_general(xb, wg_ref[...], _NN, preferred_element_type=F32)
        u = lax.dot_general(xb, wu_ref[...], _NN, preferred_element_type=F32)
        h_ref[0] = g.astype(BF16)
        h_ref[1] = u.astype(BF16)
        a_ref[...] = (g * _sigmoid(g) * u).astype(BF16)

    return pl.pallas_call(
        body, grid=(2, t // tm), name=name,
        in_specs=[pl.BlockSpec((tm, k), lambda j, i: (i, 0)),
                  pl.BlockSpec((None, None, k, n), lambda j, i: (layer, j, 0, 0)),
                  pl.BlockSpec((None, None, k, n), lambda j, i: (layer, 2 + j, 0, 0))],
        out_specs=[pl.BlockSpec((2, tm, n), lambda j, i: (0, i, j)), pl.BlockSpec((tm, n), lambda j, i: (i, j))],
        out_shape=[jax.ShapeDtypeStruct((2, t, 2 * n), BF16), jax.ShapeDtypeStruct((t, 2 * n), BF16)],
        compiler_params=_cp(2),
    )(x, wall, wall)


def _ffn_da(name, dz, w3, layer, h):
    t, k = dz.shape
    f = w3.shape[1]
    n = f // 2
    tm = _tile(t, 512)

    def body(dz_ref, w_ref, g_ref, u_ref, dh_ref):
        d = 0.5 * lax.dot_general(dz_ref[...].astype(BF16), w_ref[...], _NT, preferred_element_type=F32)
        g = g_ref[...].astype(F32)
        u = u_ref[...].astype(F32)
        sg = _sigmoid(g)
        dh_ref[0] = (d * u * sg * (1.0 + g * (1.0 - sg))).astype(BF16)
        dh_ref[1] = (d * g * sg).astype(BF16)

    return pl.pallas_call(
        body, grid=(2, t // tm), name=name,
        in_specs=[pl.BlockSpec((tm, k), lambda j, i: (i, 0)),
                  pl.BlockSpec((None, n, k), lambda j, i: (layer, j, 0)),
                  pl.BlockSpec((None, tm, n), lambda j, i: (0, i, j)),
                  pl.BlockSpec((None, tm, n), lambda j, i: (1, i, j))],
        out_specs=pl.BlockSpec((2, tm, n), lambda j, i: (0, i, j)),
        out_shape=jax.ShapeDtypeStruct((2, t, f), BF16),
        compiler_params=_cp(2),
    )(dz, w3, h, h)


def _mm_ln(name, alpha, coef, a, w3, layer, x, g, b):
    t, k = a.shape
    d = w3.shape[2]
    tm = _tile(t, 512)

    def body(a_ref, w_ref, x_ref, g_ref, b_ref, y_ref, xh_ref, rs_ref):
        f = lax.dot_general(a_ref[...].astype(BF16), w_ref[...], _NN, preferred_element_type=F32)
        z = alpha * x_ref[...] + coef * f
        mu = jnp.mean(z, axis=-1, keepdims=True)
        zc = z - mu
        var = jnp.mean(zc * zc, axis=-1, keepdims=True)
        rstd = lax.rsqrt(var + LN_EPS)
        xh = zc * rstd
        y_ref[...] = xh * g_ref[...] + b_ref[...]
        xh_ref[...] = xh
        rs_ref[...] = rstd

    row = lambda c: pl.BlockSpec((tm, c), lambda i: (i, 0))
    vec = pl.BlockSpec((1, d), lambda i: (0, 0))
    sd = jax.ShapeDtypeStruct
    return pl.pallas_call(
        body, grid=(t // tm,), name=name,
        in_specs=[row(k), pl.BlockSpec((None, k, d), lambda i: (layer, 0, 0)), row(d), vec, vec],
        out_specs=[row(d), row(d), row(1)],
        out_shape=[sd((t, d), F32), sd((t, d), F32), sd((t, 1), F32)],
        compiler_params=_cp(1),
    )(a, w3, x, g, b)


_GELU_C = math.sqrt(2.0 / math.pi)


def _s5_act_fwd(name, ych, u, dvec):
    t, d = u.shape
    tm = _tile(t, 256)

    def body(y_ref, u_ref, d_ref, p_ref, a_ref):
        y = y_ref[...] + d_ref[...] * u_ref[...]
        p_ref[...] = y
        a_ref[...] = (0.5 * y * (1.0 + jnp.tanh(_GELU_C * (y + 0.044715 * y * y * y)))).astype(BF16)

    sd = jax.ShapeDtypeStruct
    return _rows_call(name, body, t, tm, [ych, u, dvec], [d, d, d], [sd((t, d), F32), sd((t, d), BF16)])


def _s5_act_bwd(name, dact, ypre, u, dvec):
    t, d = u.shape
    tm = _tile(t, 256)

    def body(da_ref, y_ref, u_ref, d_ref, dy_ref, ds_ref, dd_ref):
        y = y_ref[...]
        th = jnp.tanh(_GELU_C * (y + 0.044715 * y * y * y))
        dg = 0.5 * (1.0 + th) + 0.5 * y * (1.0 - th * th) * _GELU_C * (1.0 + 3.0 * 0.044715 * y * y)
        dy = da_ref[...] * dg
        dy_ref[...] = dy
        ds_ref[...] = dy * d_ref[...]
        pd = jnp.sum(dy * u_ref[...], axis=0, keepdims=True)
        i = pl.program_id(0)

        @pl.when(i == 0)
        def _():
            dd_ref[...] = pd

        @pl.when(i > 0)
        def _():
            dd_ref[...] += pd

    sd = jax.ShapeDtypeStruct
    return _rows_call(name, body, t, tm, [dact, ypre, u, dvec], [d, d, d, d],
                      [sd((t, d), F32), sd((t, d), F32)], [sd((1, d), F32)])


def _glu_fwd(name, vg):
    t, d2 = vg.shape
    d = d2 // 2
    tm = _tile(t, 256)

    def body(vg_ref, m_ref):
        m_ref[...] = vg_ref[:, :d] * _sigmoid(vg_ref[:, d:])

    return _rows_call(name, body, t, tm, [vg], [d2], [jax.ShapeDtypeStruct((t, d), F32)])[0]


def _glu_bwd(name, dm, vg):
    t, d2 = vg.shape
    d = d2 // 2
    tm = _tile(t, 256)

    def body(dm_ref, vg_ref, o_ref):
        sg = _sigmoid(vg_ref[:, d:])
        g = dm_ref[...]
        o_ref[:, :d] = (g * sg).astype(BF16)
        o_ref[:, d:] = (g * vg_ref[:, :d] * sg * (1.0 - sg)).astype(BF16)

    return _rows_call(name, body, t, tm, [dm, vg], [d, d2], [jax.ShapeDtypeStruct((t, d2), BF16)])[0]


def _adamw(name, w, g, m, v):
    r, c = w.shape
    tr = _tile(r, max(V7X_SUBLANES, (1 << 20) // (4 * c) // V7X_SUBLANES * V7X_SUBLANES))

    def body(w_ref, g_ref, m_ref, v_ref, d_ref, nm_ref, nv_ref):
        gg = g_ref[...]
        nm = ADAM_B1 * m_ref[...] + (1.0 - ADAM_B1) * gg
        nv = ADAM_B2 * v_ref[...] + (1.0 - ADAM_B2) * (gg * gg)
        m_hat = nm / (1.0 - ADAM_B1 ** ADAM_STEP)
        v_hat = nv / (1.0 - ADAM_B2 ** ADAM_STEP)
        d_ref[...] = -ADAM_LR * (m_hat / (jnp.sqrt(v_hat) + ADAM_EPS) + ADAM_WD * w_ref[...])
        nm_ref[...] = nm
        nv_ref[...] = nv

    sd = jax.ShapeDtypeStruct((r, c), F32)
    return _rows_call(name, body, r, tr, [w, g, m, v], [c] * 4, [sd, sd, sd])


def _adamw_join(name, w, mine, theirs, m, v, place):
    nl, r, c = w.shape
    h = r // 2
    tr = _tile(h, max(V7X_SUBLANES, (1 << 19) // (4 * c) // V7X_SUBLANES * V7X_SUBLANES))
    nb = h // tr

    def body(pref, w_ref, a_ref, b_ref, m_ref, v_ref, g_ref, d_ref, nm_ref, nv_ref):
        gg = jnp.where(pl.program_id(1) == pref[1], a_ref[...], b_ref[...])
        nm = ADAM_B1 * m_ref[...] + (1.0 - ADAM_B1) * gg
        nv = ADAM_B2 * v_ref[...] + (1.0 - ADAM_B2) * (gg * gg)
        m_hat = nm / (1.0 - ADAM_B1 ** ADAM_STEP)
        v_hat = nv / (1.0 - ADAM_B2 ** ADAM_STEP)
        g_ref[...] = gg
        d_ref[...] = -ADAM_LR * (m_hat / (jnp.sqrt(v_hat) + ADAM_EPS) + ADAM_WD * w_ref[...])
        nm_ref[...] = nm
        nv_ref[...] = nv

    full = pl.BlockSpec((None, tr, c), lambda l, hf, i, pref: (l, hf * nb + i, 0))
    sd = jax.ShapeDtypeStruct((nl, r, c), F32)
    return pl.pallas_call(
        body, name=name,
        grid_spec=pltpu.PrefetchScalarGridSpec(
            num_scalar_prefetch=1, grid=(nl, 2, nb),
            in_specs=[full,
                      pl.BlockSpec((None, tr, c), lambda l, hf, i, pref: (l, jnp.where(hf == pref[1], i, 0), 0)),
                      pl.BlockSpec((None, tr, c), lambda l, hf, i, pref: (l, jnp.where(hf == pref[1], 0, i), 0)),
                      full, full],
            out_specs=[full, full, full, full]),
        out_shape=[sd, sd, sd, sd],
        compiler_params=_cp(3),
    )(place, w, mine, theirs, m, v)


def _sum_leading(name, a):
    n, r, c = a.shape
    tr = _tile(r, 512)

    def body(a_ref, o_ref):
        acc = a_ref[0]
        for k in range(1, n):
            acc = acc + a_ref[k]
        o_ref[...] = acc

    return pl.pallas_call(
        body, grid=(r // tr,), name=name,
        in_specs=[pl.BlockSpec((n, tr, c), lambda i: (0, i, 0))],
        out_specs=pl.BlockSpec((tr, c), lambda i: (i, 0)),
        out_shape=jax.ShapeDtypeStruct((r, c), F32), compiler_params=_cp(1),
    )(a)


def _split3(x):
    hi = x.astype(BF16)
    r1 = x - hi.astype(F32)
    mid = r1.astype(BF16)
    lo = (r1 - mid.astype(F32)).astype(BF16)
    return hi, mid, lo


def _tri_sum(tri, x):
    dims = (((1,), (0,)), ((), ()))
    hi, mid, lo = _split3(x)
    out = lax.dot_general(tri, lo, dims, preferred_element_type=F32)
    out = out + lax.dot_general(tri, mid, dims, preferred_element_type=F32)
    return out + lax.dot_general(tri, hi, dims, preferred_element_type=F32)


def _fox_cumsum(name, fl, bf):
    t, h = fl.shape
    tb = _tile(t, 512)

    def body(fl_ref, bf_ref, c_ref, carry):
        i = pl.program_id(0)

        @pl.when(i == 0)
        def _():
            carry[...] = jnp.zeros_like(carry)

        x = fl_ref[...] + bf_ref[...]
        lf = jnp.minimum(x, 0.0) - jnp.log(1.0 + jnp.exp(-jnp.abs(x)))
        row = lax.broadcasted_iota(jnp.int32, (tb, tb), 0)
        col = lax.broadcasted_iota(jnp.int32, (tb, tb), 1)
        tri = jnp.where(row >= col, 1.0, 0.0).astype(BF16)
        c_ref[...] = _tri_sum(tri, lf) + carry[...]
        carry[...] += jnp.sum(lf, axis=0, keepdims=True)

    return pl.pallas_call(
        body, grid=(t // tb,), name=name,
        in_specs=[pl.BlockSpec((tb, h), lambda i: (i, 0)), pl.BlockSpec((1, h), lambda i: (0, 0))],
        out_specs=pl.BlockSpec((tb, h), lambda i: (i, 0)),
        out_shape=jax.ShapeDtypeStruct((t, h), F32),
        scratch_shapes=[pltpu.VMEM((1, h), F32)], compiler_params=_cp(1),
    )(fl, bf)


def _fox_cumsum_bwd(name, dcum, fl, bf):
    t, h = fl.shape
    tb = _tile(t, 512)
    nb = t // tb

    def body(dc_ref, fl_ref, bf_ref, df_ref, db_ref, carry):
        i = pl.program_id(0)

        @pl.when(i == 0)
        def _():
            carry[...] = jnp.zeros_like(carry)

        dc = dc_ref[...]
        row = lax.broadcasted_iota(jnp.int32, (tb, tb), 0)
        col = lax.broadcasted_iota(jnp.int32, (tb, tb), 1)
        tri = jnp.where(row <= col, 1.0, 0.0).astype(BF16)
        dlf = _tri_sum(tri, dc) + carry[...]
        carry[...] += jnp.sum(dc, axis=0, keepdims=True)
        x = fl_ref[...] + bf_ref[...]
        df = dlf / (1.0 + jnp.exp(x))
        df_ref[...] = df
        pb = jnp.sum(df, axis=0, keepdims=True)

        @pl.when(i == 0)
        def _():
            db_ref[...] = pb

        @pl.when(i > 0)
        def _():
            db_ref[...] += pb

    rev = lambda i: (nb - 1 - i, 0)
    return pl.pallas_call(
        body, grid=(nb,), name=name,
        in_specs=[pl.BlockSpec((tb, h), rev), pl.BlockSpec((tb, h), rev), pl.BlockSpec((1, h), lambda i: (0, 0))],
        out_specs=[pl.BlockSpec((tb, h), rev), pl.BlockSpec((1, h), lambda i: (0, 0))],
        out_shape=[jax.ShapeDtypeStruct((t, h), F32), jax.ShapeDtypeStruct((1, h), F32)],
        scratch_shapes=[pltpu.VMEM((1, h), F32)], compiler_params=_cp(1),
    )(dcum, fl, bf)


_NT = (((1,), (1,)), ((), ()))
_TN = (((0,), (0,)), ((), ()))
_NN = (((1,), (0,)), ((), ()))


def _causal_mask(s, tb):
    row = lax.broadcasted_iota(jnp.int32, (tb, tb), 0)
    col = lax.broadcasted_iota(jnp.int32, (tb, tb), 1)
    return jnp.where(col <= row, s, NEG_INF)


def _attn_fwd(name, q, k, v, ccol, crow):
    nh, nb, tb, hd = q.shape

    def body(q_ref, k_ref, v_ref, cc_ref, cr_ref, o_ref, lse_ref):
        i = pl.program_id(1)
        qi = q_ref[...]
        cc = cc_ref[...]

        def step(j, carry, diagonal=False):
            m, l, acc = carry
            s = lax.dot_general(qi, k_ref[j], _NT, preferred_element_type=F32) + cc - cr_ref[j]
            if diagonal:
                s = _causal_mask(s, tb)
            m_new = jnp.maximum(m, jnp.max(s, axis=1, keepdims=True))
            p = jnp.exp(s - m_new)
            a = jnp.exp(m - m_new)
            l = a * l + jnp.sum(p, axis=1, keepdims=True)
            acc = a * acc + lax.dot_general(p.astype(BF16), v_ref[j], _NN, preferred_element_type=F32)
            return m_new, l, acc

        init = (jnp.full((tb, 1), NEG_INF, F32), jnp.zeros((tb, 1), F32), jnp.zeros((tb, hd), F32))
        m, l, acc = step(i, lax.fori_loop(0, i, step, init), diagonal=True)
        o_ref[...] = acc / l
        lse_ref[...] = m + jnp.log(l)

    blk = lambda *s: pl.BlockSpec(s, lambda h, i: (h, i, 0, 0))
    head = lambda *s: pl.BlockSpec(s, lambda h, i: (h, 0, 0, 0))
    return pl.pallas_call(
        body, grid=(nh, nb), name=name,
        in_specs=[blk(None, None, tb, hd), head(None, nb, tb, hd), head(None, nb, tb, hd),
                  blk(None, None, tb, 1), head(None, nb, 1, tb)],
        out_specs=[blk(None, None, tb, hd), blk(None, None, tb, 1)],
        out_shape=[jax.ShapeDtypeStruct((nh, nb, tb, hd), F32), jax.ShapeDtypeStruct((nh, nb, tb, 1), F32)],
        compiler_params=_cp(2),
    )(q, k, v, ccol, crow)


def _attn_bwd(name, q, k, v, ccol, crow, o, lse, do, scale):
    nh, nb, tb, hd = q.shape

    def body(q_ref, k_ref, v_ref, cc_ref, cr_ref, o_ref, lse_ref, do_ref, dq_ref, dk_ref, dv_ref, dr_ref, dc_ref):
        j = pl.program_id(1)

        @pl.when(j == 0)
        def _():
            dq_ref[...] = jnp.zeros_like(dq_ref)
            dr_ref[...] = jnp.zeros_like(dr_ref)

        kj = k_ref[...]
        vj = v_ref[...]
        cr = cr_ref[...]

        def step(i, carry, diagonal=False):
            dk, dv, dc = carry
            qi = q_ref[i]
            doi = do_ref[i]
            dob = doi.astype(BF16)
            di = jnp.sum(doi * o_ref[i], axis=1, keepdims=True)
            s = lax.dot_general(qi, kj, _NT, preferred_element_type=F32) + cc_ref[i] - cr
            if diagonal:
                s = _causal_mask(s, tb)
            p = jnp.exp(s - lse_ref[i])
            dv = dv + lax.dot_general(p.astype(BF16), dob, _TN, preferred_element_type=F32)
            dp = lax.dot_general(dob, vj, _NT, preferred_element_type=F32)
            ds = p * (dp - di)
            dsb = ds.astype(BF16)
            dk = dk + lax.dot_general(dsb, qi, _TN, preferred_element_type=F32)
            dq_ref[i] += lax.dot_general(dsb, kj, _NN, preferred_element_type=F32) * scale
            dr_ref[i] += jnp.sum(ds, axis=1, keepdims=True)
            dc = dc + jnp.sum(ds, axis=0, keepdims=True)
            return dk, dv, dc

        init = (jnp.zeros((tb, hd), F32), jnp.zeros((tb, hd), F32), jnp.zeros((1, tb), F32))
        dk, dv, dc = lax.fori_loop(j + 1, nb, step, step(j, init, diagonal=True))
        dk_ref[...] = dk
        dv_ref[...] = dv
        dc_ref[...] = dc

    blk = lambda *s: pl.BlockSpec(s, lambda h, j: (h, j, 0, 0))
    head = lambda *s: pl.BlockSpec(s, lambda h, j: (h, 0, 0, 0))
    sd = jax.ShapeDtypeStruct
    return pl.pallas_call(
        body, grid=(nh, nb), name=name,
        in_specs=[head(None, nb, tb, hd), blk(None, None, tb, hd), blk(None, None, tb, hd),
                  head(None, nb, tb, 1), blk(None, None, 1, tb),
                  head(None, nb, tb, hd), head(None, nb, tb, 1), head(None, nb, tb, hd)],
        out_specs=[head(None, nb, tb, hd), blk(None, None, tb, hd), blk(None, None, tb, hd),
                   head(None, nb, tb, 1), blk(None, None, 1, tb)],
        out_shape=[sd((nh, nb, tb, hd), F32), sd((nh, nb, tb, hd), F32), sd((nh, nb, tb, hd), F32),
                   sd((nh, nb, tb, 1), F32), sd((nh, nb, 1, tb), F32)],
        compiler_params=_cp(2),
    )(q, k, v, ccol, crow, o, lse, do)


def _cmul(ar, ai, br, bi):
    return ar * br - ai * bi, ar * bi + ai * br


def _s5_scan(name, lam, xin, hs=None):
    reverse = hs is not None
    _, seg, ns, w = xin.shape
    assert ns == SCAN_SEGMENTS
    wb = min(w, 2 * V7X_LANES)
    nsq = seg.bit_length() - 1
    assert (1 << nsq) == seg

    def body(*refs):
        if reverse:
            lam_ref, x_ref, h_ref, o_ref, dl_ref = refs
        else:
            lam_ref, x_ref, o_ref = refs
        lr = jnp.broadcast_to(lam_ref[0], (ns, wb))
        li = jnp.broadcast_to(lam_ref[1], (ns, wb))
        if reverse:
            li = -li
        zero = jnp.zeros((ns, wb), F32)
        at = (lambda n: seg - 1 - n) if reverse else (lambda n: n)

        def local(n, c):
            r = at(n)
            mr, mi = _cmul(lr, li, c[0], c[1])
            nr = mr + x_ref[0, r]
            ni = mi + x_ref[1, r]
            o_ref[0, r] = nr
            o_ref[1, r] = ni
            return nr, ni

        er, ei = lax.fori_loop(0, seg, local, (zero, zero))
        pr, pi = lr, li
        for _ in range(nsq):
            pr, pi = _cmul(pr, pi, pr, pi)
        sub = lax.broadcasted_iota(jnp.int32, (ns, wb), 0)

        def shifted(a, sh):
            if reverse:
                return jnp.where(sub < ns - sh, pltpu.roll(a, ns - sh, 0), 0.0)
            return jnp.where(sub >= sh, pltpu.roll(a, sh, 0), 0.0)

        xr, xi = er, ei
        sh = 1
        while sh < ns:
            tr, ti = _cmul(pr, pi, shifted(xr, sh), shifted(xi, sh))
            xr, xi = xr + tr, xi + ti
            pr, pi = _cmul(pr, pi, pr, pi)
            sh *= 2
        cr, ci = shifted(xr, 1), shifted(xi, 1)

        def fix(r, q):
            tr, ti = _cmul(q[0], q[1], cr, ci)
            gr = o_ref[0, r] + tr
            gi = o_ref[1, r] + ti
            o_ref[0, r] = gr
            o_ref[1, r] = gi
            return gr, gi

        if not reverse:
            def fixup(n, q):
                fix(n, q)
                return _cmul(q[0], q[1], lr, li)

            lax.fori_loop(0, seg, fixup, (lr, li))
            return

        def fixup_acc(n, c):
            qr, qi, ar, ai = c
            r = seg - 1 - n
            gr, gi = fix(r, (qr, qi))
            hr = h_ref[0, r - 1]
            hi = h_ref[1, r - 1]
            qr, qi = _cmul(qr, qi, lr, li)
            return qr, qi, ar + gr * hr + gi * hi, ai + gi * hr - gr * hi

        qr, qi, ar, ai = lax.fori_loop(0, seg - 1, fixup_acc, (lr, li, zero, zero))
        gr, gi = fix(0, (qr, qi))
        hr = jnp.where(sub >= 1, pltpu.roll(h_ref[0, seg - 1], 1, 0), 0.0)
        hi = jnp.where(sub >= 1, pltpu.roll(h_ref[1, seg - 1], 1, 0), 0.0)
        dl_ref[0] = ar + gr * hr + gi * hi
        dl_ref[1] = ai + gi * hr - gr * hi

    big = pl.BlockSpec((2, seg, ns, wb), lambda j: (0, 0, 0, j))
    lam_spec = pl.BlockSpec((2, 1, wb), lambda j: (0, 0, j))
    sd = jax.ShapeDtypeStruct
    if reverse:
        return pl.pallas_call(
            body, grid=(w // wb,), name=name, in_specs=[lam_spec, big, big],
            out_specs=[big, pl.BlockSpec((2, ns, wb), lambda j: (0, 0, j))],
            out_shape=[sd(xin.shape, F32), sd((2, ns, w), F32)], compiler_params=_cp(1),
        )(lam, xin, hs)
    return pl.pallas_call(
        body, grid=(w // wb,), name=name, in_specs=[lam_spec, big], out_specs=big,
        out_shape=sd(xin.shape, F32), compiler_params=_cp(1),
    )(lam, xin)


def _place():
    x, y, c = lax.axis_index("x"), lax.axis_index("y"), lax.axis_index("c")
    chips = [(1 - x, y), (x, 1 - y), (1 - x, 1 - y)]
    return x, y, c, chips


def _comm_params():
    return pltpu.CompilerParams(vmem_limit_bytes=VMEM_LIMIT)


def _cast_place(name, w, place):
    nl, r, c = w.shape
    tr = _tile(r, max(16, (1 << 20) // (4 * c) // 16 * 16), 16)

    def body(pref, w_ref, o_ref):
        o_ref[...] = w_ref[...].astype(BF16)

    return pl.pallas_call(
        body, name=name,
        grid_spec=pltpu.PrefetchScalarGridSpec(
            num_scalar_prefetch=1, grid=(nl, r // tr),
            in_specs=[pl.BlockSpec((None, tr, c), lambda l, i, pref: (l, i, 0))],
            out_specs=pl.BlockSpec((None, None, tr, c), lambda l, i, pref: (l, pref[0], i, 0))),
        out_shape=jax.ShapeDtypeStruct((nl, N_CHIPS, r, c), BF16),
        compiler_params=_cp(2),
    )(place, w)


def _gather_shards(name, bufs):
    n = len(bufs)

    def body(*refs):
        outs = refs[n:2 * n]
        send_sems, recv_sems = refs[2 * n:]
        x, y, c, chips = _place()
        my = 2 * x + y
        sibling = (x, y, 1 - c)

        def part(t, shard, half):
            h = bufs[t].shape[2] // 2
            return outs[t].at[:, shard, pl.ds(half * h, h)]

        def copy(t, k, ref, to):
            return pltpu.make_async_remote_copy(src_ref=ref, dst_ref=ref, send_sem=send_sems.at[t, k],
                                                recv_sem=recv_sems.at[t, k], device_id=to, device_id_type=MESH)

        sent = []
        for t in range(n):
            for k, chip in enumerate(chips):
                sent.append(copy(t, k, part(t, my, c), (*chip, c)))
                sent[-1].start()
        for k, chip in enumerate(chips):
            shard = 2 * chip[0] + chip[1]
            for t in range(n):
                copy(t, k, part(t, shard, c), (*chip, c)).wait_recv()
                sent.append(copy(t, 3 + k, part(t, shard, c), sibling))
                sent[-1].start()
        for k, chip in enumerate(chips):
            shard = 2 * chip[0] + chip[1]
            for t in range(n):
                copy(t, 3 + k, part(t, shard, 1 - c), sibling).wait_recv()
        for cp in sent:
            cp.wait_send()

    return pl.pallas_call(
        body, name=name, in_specs=[ANY] * n, out_specs=[ANY] * n,
        out_shape=[jax.ShapeDtypeStruct(b.shape, b.dtype) for b in bufs],
        input_output_aliases={t: t for t in range(n)},
        scratch_shapes=[pltpu.SemaphoreType.DMA((n, 6)), pltpu.SemaphoreType.DMA((n, 6))],
        compiler_params=_comm_params(),
    )(*bufs)


def _swap_halves(name, gs):
    n = len(gs)

    def body(*refs):
        ins, outs = refs[:n], refs[n:2 * n]
        send_sems, recv_sems = refs[2 * n:]
        x, y, c, _ = _place()
        cps = []
        for t in range(n):
            h = gs[t].shape[2] // 2
            cps.append(pltpu.make_async_remote_copy(
                src_ref=ins[t].at[:, :, pl.ds((1 - c) * h, h), :], dst_ref=outs[t], send_sem=send_sems.at[t],
                recv_sem=recv_sems.at[t], device_id=(x, y, 1 - c), device_id_type=MESH))
            cps[-1].start()
        for cp in cps:
            cp.wait()

    return pl.pallas_call(
        body, name=name, in_specs=[ANY] * n, out_specs=[ANY] * n,
        out_shape=[jax.ShapeDtypeStruct(g.shape[:2] + (g.shape[2] // 2, g.shape[3]), g.dtype) for g in gs],
        scratch_shapes=[pltpu.SemaphoreType.DMA((n,)), pltpu.SemaphoreType.DMA((n,))],
        compiler_params=_comm_params(),
    )(*gs)


def _scatter_partials(name, ps):
    n = len(ps)

    def body(*refs):
        ins, outs = refs[:n], refs[n:2 * n]
        send_sems, recv_sems = refs[2 * n:]
        x, y, c, chips = _place()
        cps = []
        for t in range(n):
            for k, chip in enumerate(chips):
                shard = 2 * chip[0] + chip[1]
                cps.append(pltpu.make_async_remote_copy(
                    src_ref=ins[t].at[:, shard], dst_ref=outs[t].at[k], send_sem=send_sems.at[t, k],
                    recv_sem=recv_sems.at[t, k], device_id=(*chip, c), device_id_type=MESH))
                cps[-1].start()
        for cp in cps:
            cp.wait()

    return pl.pallas_call(
        body, name=name, in_specs=[ANY] * n, out_specs=[ANY] * n,
        out_shape=[jax.ShapeDtypeStruct((3, p.shape[0]) + p.shape[2:], p.dtype) for p in ps],
        scratch_shapes=[pltpu.SemaphoreType.DMA((n, 3)), pltpu.SemaphoreType.DMA((n, 3))],
        compiler_params=_comm_params(),
    )(*ps)


def _send_half(name, fs):
    n = len(fs)

    def body(*refs):
        ins, outs = refs[:n], refs[n:2 * n]
        send_sems, recv_sems = refs[2 * n:]
        x, y, c, _ = _place()
        cps = []
        for t in range(n):
            cps.append(pltpu.make_async_remote_copy(
                src_ref=ins[t], dst_ref=outs[t], send_sem=send_sems.at[t], recv_sem=recv_sems.at[t],
                device_id=(x, y, 1 - c), device_id_type=MESH))
            cps[-1].start()
        for cp in cps:
            cp.wait()

    return pl.pallas_call(
        body, name=name, in_specs=[ANY] * n, out_specs=[ANY] * n,
        out_shape=[jax.ShapeDtypeStruct(f.shape, f.dtype) for f in fs],
        scratch_shapes=[pltpu.SemaphoreType.DMA((n,)), pltpu.SemaphoreType.DMA((n,))],
        compiler_params=_comm_params(),
    )(*fs)


def _gather_all(name, v):
    m_per = v.shape[0]

    def body(x_ref, out_ref, send_sems, recv_sems, local_sem):
        x, y, c, chips = _place()
        me, sibling = (x, y, c), (x, y, 1 - c)

        def rows(px, py, pc):
            return out_ref.at[pl.ds((4 * px + 2 * py + pc) * m_per, m_per), :]

        def copy(k, block, to, src=None):
            return pltpu.make_async_remote_copy(
                src_ref=rows(*block) if src is None else src, dst_ref=rows(*block), send_sem=send_sems.at[k],
                recv_sem=recv_sems.at[k], device_id=to, device_id_type=MESH)

        mine = pltpu.make_async_copy(x_ref, rows(*me), local_sem)
        mine.start()
        first = [copy(0, me, sibling, src=x_ref)]
        first += [copy(1 + j, me, (*chip, c), src=x_ref) for j, chip in enumerate(chips)]
        for cp in first:
            cp.start()
        passed = [copy(4 + j, (*chip, c), sibling) for j, chip in enumerate(chips)]
        for j, chip in enumerate(chips):
            copy(1 + j, (*chip, c), me).wait_recv()
            passed[j].start()
        copy(0, sibling, me).wait_recv()
        for j, chip in enumerate(chips):
            copy(4 + j, (*chip, 1 - c), me).wait_recv()
        for cp in first + passed:
            cp.wait_send()
        mine.wait()

    return pl.pallas_call(
        body, name=name, in_specs=[ANY], out_specs=ANY,
        out_shape=jax.ShapeDtypeStruct((N_DEV * m_per, v.shape[1]), v.dtype),
        scratch_shapes=[pltpu.SemaphoreType.DMA((7,)), pltpu.SemaphoreType.DMA((7,)), pltpu.SemaphoreType.DMA],
        compiler_params=_comm_params(),
    )(v)


def _pair_sum(name, g, recv, place):
    s, r, c = g.shape
    h = r // 2
    tr = _tile(h, max(V7X_SUBLANES * 2, (1 << 20) // (4 * c) // 16 * 16), 16)
    nb = h // tr

    def body(pref, g_ref, r_ref, p_ref, pb_ref):
        v = g_ref[...] + r_ref[...]
        p_ref[...] = v
        pb_ref[...] = v.astype(BF16)

    spec = pl.BlockSpec((None, tr, c), lambda k, i, pref: (k, i, 0))
    return pl.pallas_call(
        body, name=name,
        grid_spec=pltpu.PrefetchScalarGridSpec(
            num_scalar_prefetch=1, grid=(s, nb),
            in_specs=[pl.BlockSpec((None, tr, c), lambda k, i, pref: (k, pref[1] * nb + i, 0)), spec],
            out_specs=[spec, spec]),
        out_shape=[jax.ShapeDtypeStruct((s, h, c), F32), jax.ShapeDtypeStruct((s, h, c), BF16)],
        compiler_params=_cp(2),
    )(place, g, recv)


def _chip_sum(name, p, recv, place):
    nl, s, h, c = p.shape
    tr = _tile(h, max(V7X_SUBLANES * 2, (1 << 20) // (4 * c) // 16 * 16), 16)

    def body(pref, p_ref, r_ref, o_ref):
        acc = p_ref[...]
        for k in range(3):
            acc = acc + r_ref[k].astype(F32)
        o_ref[...] = acc

    return pl.pallas_call(
        body, name=name,
        grid_spec=pltpu.PrefetchScalarGridSpec(
            num_scalar_prefetch=1, grid=(nl, h // tr),
            in_specs=[pl.BlockSpec((None, None, tr, c), lambda l, i, pref: (l, pref[0], i, 0)),
                      pl.BlockSpec((3, None, tr, c), lambda l, i, pref: (0, l, i, 0))],
            out_specs=pl.BlockSpec((None, tr, c), lambda l, i, pref: (l, i, 0))),
        out_shape=jax.ShapeDtypeStruct((nl, h, c), F32),
        compiler_params=_cp(2),
    )(place, p, recv)


def _rows_view(wall):
    nl, s, r, c = wall.shape
    return wall.reshape(nl, s * r, c)


def _ffn_fwd(tag, alpha, x, w_in, w_out3, layer, g, b):
    h, a = _ffn_in(f"{tag}_in", x, w_in, layer)
    y, xhat, rstd = _mm_ln(f"{tag}_out", alpha, 0.5, a, w_out3, layer, x, g, b)
    return y, (x, h, a, xhat, rstd)


def _ffn_bwd(tag, alpha, terms, saved, w_in, w_out3, layer, g, g_win, g_wout3):
    x, h, a, xhat, rstd = saved
    t = x.shape[0]
    _, s, k, n = w_in.shape
    tm = _tile(t, 512)
    dz, dg, db = _ln_bwd(f"{tag}_ln_bwd", terms, xhat, rstd, g)
    g_wout3 = _mm_tn(f"{tag}_dwout", a, dz, tm=n, scale=0.5, layer=layer, shape=w_out3.shape, into=g_wout3)
    dh = _ffn_da(f"{tag}_da", dz, w_out3, layer, h)
    g_win = _mm(f"{tag}_dwin", x, dh, mode="tn", grid=(s, t // tm), kaxis=1,
                a_blk=(tm, k), a_map=lambda j, kk: (kk, 0),
                b_blk=(None, tm, n), b_map=lambda j, kk: (j // 2, kk, j % 2),
                o_shape=w_in.shape, o_blk=(None, None, k, n), o_map=lambda j, kk: (layer, j, 0, 0), into=g_win)
    dx = _mm(f"{tag}_dx", dh, w_in, mode="nt", grid=(t // tm, s), kaxis=1,
             a_blk=(None, tm, n), a_map=lambda i, kk: (kk // 2, i, kk % 2),
             b_blk=(None, None, k, n), b_map=lambda i, kk: (layer, kk, 0, 0),
             o_shape=(t, k), o_blk=(tm, k), o_map=lambda i, kk: (i, 0))
    return [(dz, alpha), (dx, 1.0)], g_win, g_wout3, dg, db


def _heads(a, nh, tb):
    t = a.shape[0]
    return a.reshape(t, nh, -1).transpose(1, 0, 2).reshape(nh, t // tb, tb, -1)


def _unheads(a):
    nh, nb, tb, hd = a.shape
    return a.reshape(nh, nb * tb, hd).transpose(1, 0, 2).reshape(nb * tb, nh * hd)


def _fox_fwd(tag, alpha, x, w_pad, bf, w_o3, layer, g, b):
    t, d = x.shape
    nh = bf.shape[1]
    hd = d // nh
    tb = _tile(t, 512)
    scale = 1.0 / math.sqrt(hd)
    proj = _mm_nn(f"{tag}_proj", x, w_pad, tn=_tile(w_pad.shape[1], 640, V7X_LANES))
    q = _heads((proj[:, :d].astype(BF16) * scale).astype(BF16), nh, tb)
    k = _heads(proj[:, d:2 * d].astype(BF16), nh, tb)
    v = _heads(proj[:, 2 * d:3 * d].astype(BF16), nh, tb)
    fl = proj[:, 3 * d:3 * d + nh]
    cum = _fox_cumsum(f"{tag}_cum", fl, bf)
    ccol = cum.T.reshape(nh, t // tb, tb, 1)
    crow = cum.T.reshape(nh, t // tb, 1, tb)
    o, lse = _attn_fwd(f"{tag}_attn", q, k, v, ccol, crow)
    o2 = _unheads(o)
    y, xhat, rstd = _mm_ln(f"{tag}_oproj", alpha, 1.0, o2, w_o3, layer, x, g, b)
    return y, xhat, rstd, (x, q, k, v, ccol, crow, o, lse, o2, fl, scale)


def _fox_bwd(tag, dm, saved, w_pad, bf, w_o3, layer, g_wo3):
    x, q, k, v, ccol, crow, o, lse, o2, fl, scale = saved
    t, d = x.shape
    nh, nb, tb, hd = q.shape
    g_wo3 = _mm_tn(f"{tag}_dwo", o2, dm, layer=layer, shape=w_o3.shape, into=g_wo3)
    do2 = _mm_nt(f"{tag}_do", dm, w_o3, layer=layer)
    do = _heads(do2, nh, tb)
    dq, dk, dv, drow, dcol = _attn_bwd(f"{tag}_attn_bwd", q, k, v, ccol, crow, o, lse, do, scale)
    dcum = (drow.reshape(nh, t) - dcol.reshape(nh, t)).T
    dfl, dbf = _fox_cumsum_bwd(f"{tag}_cum_bwd", dcum, fl, bf)
    pad = w_pad.shape[1] - 3 * d - nh
    dproj = jnp.concatenate([_unheads(dq).astype(BF16), _unheads(dk).astype(BF16), _unheads(dv).astype(BF16),
                             dfl.astype(BF16), jnp.zeros((t, pad), BF16)], axis=1)
    d_wpad = _mm_tn(f"{tag}_dwin", x, dproj, tn=_tile(w_pad.shape[1], 640, V7X_LANES))
    dx = _mm_nt(f"{tag}_dx", dproj, w_pad)
    return dx, d_wpad, dbf, g_wo3


def _to_segments(a):
    t, d = a.shape
    return a.reshape(SCAN_SEGMENTS, t // SCAN_SEGMENTS, d).transpose(1, 0, 2).reshape(t, d)


def _from_segments(a):
    t, d = a.shape
    return a.reshape(t // SCAN_SEGMENTS, SCAN_SEGMENTS, d).transpose(1, 0, 2).reshape(t, d)


def _s5_discretise(a_re, a_im, log_dt, b_re, b_im):
    dt = jnp.exp(log_dt)[:, None]
    mag = jnp.exp(a_re * dt)
    ang = a_im * dt
    lb_re = mag * jnp.cos(ang)
    lb_im = mag * jnp.sin(ang)
    den = a_re * a_re + a_im * a_im
    nr = lb_re - 1.0
    ni = lb_im
    z_re = (nr * a_re + ni * a_im) / den
    z_im = (ni * a_re - nr * a_im) / den
    bb_re = z_re[..., None] * b_re - z_im[..., None] * b_im
    bb_im = z_re[..., None] * b_im + z_im[..., None] * b_re
    return lb_re, lb_im, bb_re, bb_im


S5_BLOCK_GROUPS = 8


def _blockdiag_in(bb):
    g, p, h = bb.shape
    e = jnp.eye(S5_BLOCK_GROUPS, dtype=bb.dtype)
    b4 = bb.reshape(g // S5_BLOCK_GROUPS, S5_BLOCK_GROUPS, p, h)
    return jnp.einsum("jgph,gf->jghfp", b4, e).reshape(g // S5_BLOCK_GROUPS, S5_BLOCK_GROUPS * h, S5_BLOCK_GROUPS * p)


def _blockdiag_in_grad(d):
    nj, gh, gp = d.shape
    h, p = gh // S5_BLOCK_GROUPS, gp // S5_BLOCK_GROUPS
    e = jnp.eye(S5_BLOCK_GROUPS, dtype=d.dtype)
    d6 = d.reshape(nj, S5_BLOCK_GROUPS, h, S5_BLOCK_GROUPS, p)
    return jnp.einsum("jghfp,gf->jgph", d6, e).reshape(nj * S5_BLOCK_GROUPS, p, h)


def _blockdiag_out(cc):
    g, h, p = cc.shape
    e = jnp.eye(S5_BLOCK_GROUPS, dtype=cc.dtype)
    c4 = cc.reshape(g // S5_BLOCK_GROUPS, S5_BLOCK_GROUPS, h, p)
    return jnp.einsum("jghp,gf->jfpgh", c4, e).reshape(g // S5_BLOCK_GROUPS, S5_BLOCK_GROUPS * p, S5_BLOCK_GROUPS * h)


def _blockdiag_out_grad(d):
    nj, gp, gh = d.shape
    h, p = gh // S5_BLOCK_GROUPS, gp // S5_BLOCK_GROUPS
    e = jnp.eye(S5_BLOCK_GROUPS, dtype=d.dtype)
    d6 = d.reshape(nj, S5_BLOCK_GROUPS, p, S5_BLOCK_GROUPS, h)
    return jnp.einsum("jfpgh,gf->jghp", d6, e).reshape(nj * S5_BLOCK_GROUPS, h, p)


def _s5_fwd(tag, x, prm, w_out, layer):
    a_re, a_im, log_dt, b_re, b_im, c_re, c_im, d_skip = prm
    t, d = x.shape
    g, p = a_re.shape
    w = g * p
    nj = g // S5_BLOCK_GROUPS
    cw, sw = S5_BLOCK_GROUPS * S5_GROUP, S5_BLOCK_GROUPS * p
    seg = t // SCAN_SEGMENTS
    tm = _tile(t, 4096)
    lb_re, lb_im, bb_re, bb_im = _s5_discretise(a_re, a_im, log_dt, b_re, b_im)
    lam = jnp.stack([lb_re.reshape(1, w), lb_im.reshape(1, w)])
    bs = jnp.stack([_blockdiag_in(bb_re), _blockdiag_in(bb_im)]).astype(BF16)
    cs = jnp.stack([_blockdiag_out(c_re), -_blockdiag_out(c_im)]).astype(BF16)
    dvec = d_skip.reshape(1, d)
    u = _to_segments(x)
    bu = _mm(f"{tag}_bu", u, bs, mode="nn", grid=(2, nj, t // tm), kaxis=None,
             a_blk=(tm, cw), a_map=lambda r, j, i: (i, j),
             b_blk=(None, None, cw, sw), b_map=lambda r, j, i: (r, j, 0, 0),
             o_shape=(2, t, w), o_blk=(None, tm, sw), o_map=lambda r, j, i: (r, i, j))
    hs = _s5_scan(f"{tag}_scan", lam, bu.reshape(2, seg, SCAN_SEGMENTS, w)).reshape(2, t, w)
    ych = _mm(f"{tag}_ch", hs, cs, mode="nn", grid=(nj, t // tm, 2), kaxis=2,
              a_blk=(None, tm, sw), a_map=lambda j, i, r: (r, i, j),
              b_blk=(None, None, sw, cw), b_map=lambda j, i, r: (r, j, 0, 0),
              o_shape=(t, d), o_blk=(tm, cw), o_map=lambda j, i, r: (i, j))
    ypre, act = _s5_act_fwd(f"{tag}_act", ych, u, dvec)
    vg = _mm_shards_nn(f"{tag}_wout", act, w_out, layer, F32)
    m = _from_segments(_glu_fwd(f"{tag}_glu", vg))
    return m, (u, lam, bs, cs, dvec, hs, ypre, act, vg)


def _s5_bwd(tag, dm, saved, prm, w_out, layer, g_wout):
    a_re, a_im, log_dt, b_re, b_im, c_re, c_im, d_skip = prm
    u, lam, bs, cs, dvec, hs, ypre, act, vg = saved
    t, d = u.shape
    g, p = a_re.shape
    w = g * p
    nj = g // S5_BLOCK_GROUPS
    cw, sw = S5_BLOCK_GROUPS * S5_GROUP, S5_BLOCK_GROUPS * p
    seg = t // SCAN_SEGMENTS
    tm = _tile(t, 4096)
    dvg = _glu_bwd(f"{tag}_glu_bwd", _to_segments(dm), vg)
    g_wout = _mm_shards_tn(f"{tag}_dwout", act, dvg, layer, w_out.shape, g_wout)
    dact = _mm_shards_nt(f"{tag}_dact", dvg, w_out, layer)
    dypre, duskip, dd = _s5_act_bwd(f"{tag}_act_bwd", dact, ypre, u, dvec)
    dh = _mm(f"{tag}_dh", dypre, cs, mode="nt", grid=(2, nj, t // tm), kaxis=None,
             a_blk=(tm, cw), a_map=lambda r, j, i: (i, j),
             b_blk=(None, None, sw, cw), b_map=lambda r, j, i: (r, j, 0, 0),
             o_shape=(2, t, w), o_blk=(None, tm, sw), o_map=lambda r, j, i: (r, i, j))
    dcs = _mm(f"{tag}_dc", hs, dypre, mode="tn", grid=(2, nj, t // tm), kaxis=2,
              a_blk=(None, tm, sw), a_map=lambda r, j, i: (r, i, j),
              b_blk=(tm, cw), b_map=lambda r, j, i: (i, j),
              o_shape=(2, nj, sw, cw), o_blk=(None, None, sw, cw), o_map=lambda r, j, i: (r, j, 0, 0))
    gs, dlam8 = _s5_scan(f"{tag}_scan_bwd", lam, dh.reshape(2, seg, SCAN_SEGMENTS, w),
                         hs.reshape(2, seg, SCAN_SEGMENTS, w))
    gs = gs.reshape(2, t, w)
    du = _mm(f"{tag}_du", gs, bs, mode="nt", grid=(nj, t // tm, 2), kaxis=2,
             a_blk=(None, tm, sw), a_map=lambda j, i, r: (r, i, j),
             b_blk=(None, None, cw, sw), b_map=lambda j, i, r: (r, j, 0, 0),
             o_shape=(t, d), o_blk=(tm, cw), o_map=lambda j, i, r: (i, j))
    dbs = _mm(f"{tag}_db", u, gs, mode="tn", grid=(2, nj, t // tm), kaxis=2,
              a_blk=(tm, cw), a_map=lambda r, j, i: (i, j),
              b_blk=(None, tm, sw), b_map=lambda r, j, i: (r, i, j),
              o_shape=(2, nj, cw, sw), o_blk=(None, None, cw, sw), o_map=lambda r, j, i: (r, j, 0, 0))
    dx = _from_segments(du + duskip)
    dlam = jnp.sum(dlam8, axis=1).reshape(2, g, p)
    small = dict(dlb_re=dlam[0], dlb_im=dlam[1],
                 dbb_re=_blockdiag_in_grad(dbs[0]), dbb_im=_blockdiag_in_grad(dbs[1]),
                 dc_re=_blockdiag_out_grad(dcs[0]), dc_im=-_blockdiag_out_grad(dcs[1]),
                 dd=dd.reshape(g, S5_GROUP))
    return dx, g_wout, small


def _pack(pieces):
    flat = jnp.concatenate([p.reshape(-1).astype(F32) for p in pieces])
    n = flat.shape[0]
    unit = V7X_SUBLANES * V7X_LANES
    total = -(-n // unit) * unit
    return jnp.pad(flat, (0, total - n)).reshape(total // V7X_LANES, V7X_LANES)


def _unpack(buf, shapes):
    flat = buf.reshape(-1)
    out, off = [], 0
    for s in shapes:
        n = math.prod(s)
        out.append(flat[off:off + n].reshape(s))
        off += n
    return out


def kernel(x, ffn1_w_in, ffn1_w_out, ln1_g, ln1_b, lnm_g, lnm_b, ffn2_w_in, ffn2_w_out, ln2_g, ln2_b, fox_w_in, fox_b_f, fox_w_o, s5_a_re, s5_a_im, s5_log_dt, s5_b_re, s5_b_im, s5_c_re, s5_c_im, s5_d, s5_w_out, loss_target, m_ffn1_w_in, m_ffn1_w_out, m_ln1_g, m_ln1_b, m_lnm_g, m_lnm_b, m_ffn2_w_in, m_ffn2_w_out, m_ln2_g, m_ln2_b, m_fox_w_in, m_fox_b_f, m_fox_w_o, m_s5_a_re, m_s5_a_im, m_s5_log_dt, m_s5_b_re, m_s5_b_im, m_s5_c_re, m_s5_c_im, m_s5_d, m_s5_w_out, v_ffn1_w_in, v_ffn1_w_out, v_ln1_g, v_ln1_b, v_lnm_g, v_lnm_b, v_ffn2_w_in, v_ffn2_w_out, v_ln2_g, v_ln2_b, v_fox_w_in, v_fox_b_f, v_fox_w_o, v_s5_a_re, v_s5_a_im, v_s5_log_dt, v_s5_b_re, v_s5_b_im, v_s5_c_re, v_s5_c_im, v_s5_d, v_s5_w_out):
    big_names = ["ffn1_w_in", "ffn1_w_out", "ffn2_w_in", "ffn2_w_out", "fox_w_in", "fox_w_o", "s5_w_out"]
    small_names = ["ln1_g", "ln1_b", "lnm_g", "lnm_b", "ln2_g", "ln2_b", "fox_b_f", "s5_a_re", "s5_a_im", "s5_log_dt",
                   "s5_b_re", "s5_b_im", "s5_c_re", "s5_c_im", "s5_d"]
    out_order = ["ffn1_w_in", "ffn1_w_out", "ln1_g", "ln1_b", "lnm_g", "lnm_b", "ffn2_w_in", "ffn2_w_out", "ln2_g",
                 "ln2_b", "fox_w_in", "fox_b_f", "fox_w_o", "s5_a_re", "s5_a_im", "s5_log_dt", "s5_b_re", "s5_b_im",
                 "s5_c_re", "s5_c_im", "s5_d", "s5_w_out"]
    env = dict(locals())
    w = {n: env[n] for n in out_order}
    mom = {n: env["m_" + n] for n in out_order}
    vel = {n: env["v_" + n] for n in out_order}

    depth, d = ln1_g.shape
    t = x.shape[1]
    alpha = (2.0 * depth) ** 0.25
    x0 = x.reshape(t, d)
    tgt = loss_target.reshape(t, d)
    my_x, my_y, my_c = lax.axis_index("x"), lax.axis_index("y"), lax.axis_index("c")
    place = jnp.stack([2 * my_x + my_y, my_c]).astype(jnp.int32)

    gathered = _gather_shards("gather_weights", [_cast_place(f"cast_{n}", w[n], place) for n in big_names])
    full = dict(zip(big_names, gathered))
    rows3 = {n: _rows_view(full[n]) for n in ("ffn1_w_out", "ffn2_w_out", "fox_w_o")}

    nh = fox_b_f.shape[1]
    fox_cols = 3 * d + nh
    fox_pad = -(-fox_cols // (5 * V7X_LANES)) * (5 * V7X_LANES)

    def fox_wpad(j):
        wf = full["fox_w_in"][j].transpose(1, 0, 2).reshape(d, fox_cols)
        return jnp.pad(wf, ((0, 0), (0, fox_pad - fox_cols)))

    def s5_params(j):
        return (s5_a_re[j], s5_a_im[j], s5_log_dt[j], s5_b_re[j], s5_b_im[j], s5_c_re[j], s5_c_im[j], s5_d[j])

    saved = []
    h = x0
    for i in range(depth):
        j = i // 2
        h, s1 = _ffn_fwd(f"l{i}_ffn1", alpha, h, full["ffn1_w_in"], rows3["ffn1_w_out"], i,
                         ln1_g[i:i + 1], ln1_b[i:i + 1])
        if i % 2 == 0:
            h, xhat_m, rstd_m, sm = _fox_fwd(f"l{i}_fox", alpha, h, fox_wpad(j), fox_b_f[j:j + 1], rows3["fox_w_o"], j,
                                             lnm_g[i:i + 1], lnm_b[i:i + 1])
        else:
            m, sm = _s5_fwd(f"l{i}_s5", h, s5_params(j), full["s5_w_out"], j)
            h, xhat_m, rstd_m = _ln_fwd(f"l{i}_lnm", alpha, h, m, 1.0, lnm_g[i:i + 1], lnm_b[i:i + 1])
        h, s2 = _ffn_fwd(f"l{i}_ffn2", alpha, h, full["ffn2_w_in"], rows3["ffn2_w_out"], i,
                         ln2_g[i:i + 1], ln2_b[i:i + 1])
        saved.append((s1, sm, (xhat_m, rstd_m), s2))
    loss_part = _loss_sum("loss", h, tgt) * (0.5 / d)

    gbuf = {n: None for n in big_names}
    gfox_in = [None] * fox_w_in.shape[0]
    gsmall = {n: [None] * w[n].shape[0] for n in small_names}
    s5_cot = [None] * s5_a_re.shape[0]
    terms = [(h, 1.0 / d), (tgt, -1.0 / d)]
    for i in reversed(range(depth)):
        j = i // 2
        s1, sm, (xhat_m, rstd_m), s2 = saved[i]
        terms, gbuf["ffn2_w_in"], gbuf["ffn2_w_out"], dg, db = _ffn_bwd(
            f"l{i}_ffn2", alpha, terms, s2, full["ffn2_w_in"], rows3["ffn2_w_out"], i, ln2_g[i:i + 1],
            gbuf["ffn2_w_in"], gbuf["ffn2_w_out"])
        gsmall["ln2_g"][i], gsmall["ln2_b"][i] = dg, db
        dz, dg, db = _ln_bwd(f"l{i}_lnm_bwd", terms, xhat_m, rstd_m, lnm_g[i:i + 1])
        gsmall["lnm_g"][i], gsmall["lnm_b"][i] = dg, db
        if i % 2 == 0:
            dx, d_wpad, dbf, gbuf["fox_w_o"] = _fox_bwd(f"l{i}_fox", dz, sm, fox_wpad(j), fox_b_f[j:j + 1],
                                                       rows3["fox_w_o"], j, gbuf["fox_w_o"])
            gfox_in[j] = d_wpad[:, :fox_cols].reshape(d, N_CHIPS, -1).transpose(1, 0, 2)
            gsmall["fox_b_f"][j] = dbf
        else:
            dx, gbuf["s5_w_out"], s5_cot[j] = _s5_bwd(f"l{i}_s5", dz, sm, s5_params(j), full["s5_w_out"], j,
                                                      gbuf["s5_w_out"])
        terms = [(dz, alpha), (dx, 1.0)]
        terms, gbuf["ffn1_w_in"], gbuf["ffn1_w_out"], dg, db = _ffn_bwd(
            f"l{i}_ffn1", alpha, terms, s1, full["ffn1_w_in"], rows3["ffn1_w_out"], i, ln1_g[i:i + 1],
            gbuf["ffn1_w_in"], gbuf["ffn1_w_out"])
        gsmall["ln1_g"][i], gsmall["ln1_b"][i] = dg, db
    grad_x = _lincomb("grad_x", terms).reshape(x.shape)
    gbuf["fox_w_in"] = jnp.stack(gfox_in)

    g4 = [gbuf[n].reshape(full[n].shape) for n in big_names]
    from_sibling = _swap_halves("grad_pair_swap", g4)
    pair = []
    for n, g, r in zip(big_names, g4, from_sibling):
        p, pb = _pair_sum(f"grad_pair_sum_{n}", g.reshape((-1,) + g.shape[2:]), r.reshape((-1,) + r.shape[2:]), place)
        pair.append((p.reshape(r.shape), pb.reshape(r.shape)))
    from_chips = _scatter_partials("grad_chip_scatter", [pb for _, pb in pair])
    halves = [_chip_sum(f"grad_chip_sum_{n}", p, r, place) for n, (p, _), r in zip(big_names, pair, from_chips)]
    theirs = _send_half("grad_send_half", halves)

    cot_names = ["dlb_re", "dlb_im", "dbb_re", "dbb_im", "dc_re", "dc_im", "dd"]
    ln_names = ["ln1_g", "ln1_b", "lnm_g", "lnm_b", "ln2_g", "ln2_b"]
    pieces = [loss_part] + [jnp.concatenate(gsmall[n], axis=0) for n in ln_names + ["fox_b_f"]]
    pieces += [jnp.stack([s5_cot[j][n] for j in range(len(s5_cot))]) for n in cot_names]
    shapes = [p.shape for p in pieces]
    mine = _pack(pieces)
    everyone = _gather_all("small_gather", mine).reshape(N_DEV, *mine.shape)
    summed = _unpack(_sum_leading("small_sum", everyone), shapes)
    loss = summed[0].reshape(())
    gs_final = dict(zip(ln_names + ["fox_b_f"], summed[1:8]))
    cot = dict(zip(cot_names, summed[8:]))
    prm_names = ["s5_a_re", "s5_a_im", "s5_log_dt", "s5_b_re", "s5_b_im"]
    _, disc_vjp = jax.vjp(jax.vmap(_s5_discretise), *[w[n] for n in prm_names])
    for n, gval in zip(prm_names, disc_vjp((cot["dlb_re"], cot["dlb_im"], cot["dbb_re"], cot["dbb_im"]))):
        gs_final[n] = gval
    gs_final["s5_c_re"], gs_final["s5_c_im"], gs_final["s5_d"] = cot["dc_re"], cot["dc_im"], cot["dd"]

    grads, deltas, new_m, new_v = {}, {}, {}, {}
    for n, mine_h, their_h in zip(big_names, halves, theirs):
        grads[n], deltas[n], new_m[n], new_v[n] = _adamw_join(f"adamw_{n}", w[n], mine_h, their_h, mom[n], vel[n], place)
    small_shapes = [w[n].shape for n in small_names]
    for n in small_names:
        grads[n] = gs_final[n].reshape(w[n].shape)
    packed = [_pack([src[n] for n in small_names]) for src in (w, grads, mom, vel)]
    for dst, buf in zip((deltas, new_m, new_v), _adamw("adamw_small", *packed)):
        for n, val in zip(small_names, _unpack(buf, small_shapes)):
            dst[n] = val
    return (loss, grad_x, *[grads[n] for n in out_order], *[deltas[n] for n in out_order],
            *[new_m[n] for n in out_order], *[new_v[n] for n in out_order])
```

```python
import functools
import math

import jax
import jax.numpy as jnp
from jax import lax
from jax.experimental import pallas as pl
from jax.experimental.pallas import tpu as pltpu

F32 = jnp.float32
BF16 = jnp.bfloat16
LN_EPS = 1e-5
NEG_INF = -1e30
ADAM_LR = 0.001
ADAM_B1 = 0.9
ADAM_B2 = 0.999
ADAM_EPS = 1e-08
ADAM_WD = 0.01
ADAM_STEP = 10
S5_GROUP = 16
SCAN_SEGMENTS = 8
V7X_SUBLANES = 8
V7X_LANES = 128
VMEM_LIMIT = 56 * 1024 * 1024
N_CHIPS = 4
N_DEV = 8
MESH = pl.DeviceIdType.MESH
ANY = pl.BlockSpec(memory_space=pl.ANY)


def _cp(n_grid, kaxis=None):
    sem = tuple("arbitrary" if (kaxis is None or i == kaxis) else "parallel" for i in range(n_grid))
    return pltpu.CompilerParams(dimension_semantics=sem, vmem_limit_bytes=VMEM_LIMIT)


def _tile(n, pref, mult=V7X_SUBLANES):
    if n <= pref:
        return n
    for t in range(pref, 0, -1):
        if n % t == 0 and t % mult == 0:
            return t
    return n


_CONTRACT = {"nn": ((1,), (0,)), "nt": ((1,), (1,)), "tn": ((0,), (0,))}


def _mm(name, a, b, *, mode, grid, kaxis, a_blk, a_map, b_blk, b_map, o_shape, o_blk, o_map, o_dtype=F32, scale=None,
        into=None):
    nk = 1 if kaxis is None else grid[kaxis]
    assert kaxis is None or kaxis == len(grid) - 1
    dims = (_CONTRACT[mode], ((), ()))
    use_acc = nk > 1 and o_dtype != F32
    acc_shape = tuple(d for d in o_blk if d is not None)

    def body(a_ref, b_ref, *rest):
        o_ref, scratch = (rest[1], rest[2:]) if into is not None else (rest[0], rest[1:])
        p = lax.dot_general(a_ref[...].astype(BF16), b_ref[...].astype(BF16), dims, preferred_element_type=F32)
        if nk == 1:
            if scale is not None:
                p = p * scale
            o_ref[...] = p.astype(o_dtype)
            return
        acc = scratch[0] if use_acc else o_ref
        k = pl.program_id(kaxis)

        @pl.when(k == 0)
        def _():
            acc[...] = p

        @pl.when(k > 0)
        def _():
            acc[...] += p

        if use_acc or scale is not None:
            @pl.when(k == nk - 1)
            def _():
                r = acc[...]
                if scale is not None:
                    r = r * scale
                o_ref[...] = r.astype(o_dtype)

    in_specs = [pl.BlockSpec(a_blk, a_map), pl.BlockSpec(b_blk, b_map)]
    args = [a, b]
    if into is not None:
        assert into.shape == tuple(o_shape) and into.dtype == o_dtype
        in_specs.append(ANY)
        args.append(into)
    return pl.pallas_call(
        body, grid=grid, name=name, in_specs=in_specs,
        out_specs=pl.BlockSpec(o_blk, o_map),
        out_shape=jax.ShapeDtypeStruct(o_shape, o_dtype),
        input_output_aliases={2: 0} if into is not None else {},
        scratch_shapes=[pltpu.VMEM(acc_shape, F32)] if use_acc else [],
        compiler_params=_cp(len(grid), kaxis),
    )(*args)


def _mm_shards_nn(name, a, wall, layer, o_dtype):
    t, k = a.shape
    _, s, _, n = wall.shape
    tm = _tile(t, 512)
    return _mm(name, a, wall, mode="nn", grid=(s, t // tm), kaxis=None,
               a_blk=(tm, k), a_map=lambda j, i: (i, 0),
               b_blk=(None, None, k, n), b_map=lambda j, i: (layer, j, 0, 0),
               o_shape=(t, s * n), o_blk=(tm, n), o_map=lambda j, i: (i, j), o_dtype=o_dtype)


def _mm_shards_nt(name, g, wall, layer):
    t = g.shape[0]
    _, s, k, n = wall.shape
    tm = _tile(t, 512)
    return _mm(name, g, wall, mode="nt", grid=(t // tm, s), kaxis=1,
               a_blk=(tm, n), a_map=lambda i, kk: (i, kk),
               b_blk=(None, None, k, n), b_map=lambda i, kk: (layer, kk, 0, 0),
               o_shape=(t, k), o_blk=(tm, k), o_map=lambda i, kk: (i, 0))


def _mm_shards_tn(name, a, g, layer, shape, into):
    t, k = a.shape
    _, s, _, n = shape
    tk = _tile(t, 512)
    return _mm(name, a, g, mode="tn", grid=(s, t // tk), kaxis=1,
               a_blk=(tk, k), a_map=lambda j, kk: (kk, 0),
               b_blk=(tk, n), b_map=lambda j, kk: (kk, j),
               o_shape=shape, o_blk=(None, None, k, n), o_map=lambda j, kk: (layer, j, 0, 0), into=into)


def _mm_nn(name, a, w, o_dtype=F32, tn=None):
    t, k = a.shape
    n = w.shape[1]
    tm = _tile(t, 512)
    tn = n if tn is None else tn
    return _mm(name, a, w, mode="nn", grid=(n // tn, t // tm), kaxis=None,
               a_blk=(tm, k), a_map=lambda j, i: (i, 0),
               b_blk=(k, tn), b_map=lambda j, i: (0, j),
               o_shape=(t, n), o_blk=(tm, tn), o_map=lambda j, i: (i, j), o_dtype=o_dtype)


def _mm_nt(name, g, w, layer=None, o_dtype=F32):
    t, k = g.shape
    n = w.shape[-2]
    tm = _tile(t, 512)
    b_blk, b_map = ((n, k), lambda i: (0, 0)) if layer is None else ((None, n, k), lambda i: (layer, 0, 0))
    return _mm(name, g, w, mode="nt", grid=(t // tm,), kaxis=None,
               a_blk=(tm, k), a_map=lambda i: (i, 0), b_blk=b_blk, b_map=b_map,
               o_shape=(t, n), o_blk=(tm, n), o_map=lambda i: (i, 0), o_dtype=o_dtype)


def _mm_tn(name, a, g, tm=None, tn=None, scale=None, layer=None, shape=None, into=None):
    t, m = a.shape
    n = g.shape[1]
    tk = _tile(t, 512)
    tm = m if tm is None else tm
    tn = n if tn is None else tn
    if layer is None:
        o_shape, o_blk, o_map = (m, n), (tm, tn), lambda i, j, kk: (i, j)
    else:
        o_shape, o_blk, o_map = shape, (None, tm, tn), lambda i, j, kk: (layer, i, j)
    return _mm(name, a, g, mode="tn", grid=(m // tm, n // tn, t // tk), kaxis=2,
               a_blk=(tk, tm), a_map=lambda i, j, kk: (kk, i),
               b_blk=(tk, tn), b_map=lambda i, j, kk: (kk, j),
               o_shape=o_shape, o_blk=o_blk, o_map=o_map, scale=scale, into=into)


def _sigmoid(x):
    return 1.0 / (1.0 + jnp.exp(-x))


def _rows_call(name, body, t, tm, ins, in_cols, outs, acc_outs=()):
    in_specs = []
    for x, c in zip(ins, in_cols):
        if x.shape[0] == 1:
            in_specs.append(pl.BlockSpec((1, c), lambda i: (0, 0)))
        else:
            in_specs.append(pl.BlockSpec((tm, c), lambda i: (i, 0)))
    out_specs = [pl.BlockSpec((tm, s.shape[1]), lambda i: (i, 0)) for s in outs]
    out_specs += [pl.BlockSpec((1, s.shape[1]), lambda i: (0, 0)) for s in acc_outs]
    return pl.pallas_call(
        body, grid=(t // tm,), name=name, in_specs=in_specs, out_specs=out_specs,
        out_shape=list(outs) + list(acc_outs), compiler_params=_cp(1),
    )(*ins)


def _ln_fwd(name, alpha, x, r, coef, g, b):
    t, d = x.shape
    tm = _tile(t, 256)

    def body(x_ref, r_ref, g_ref, b_ref, y_ref, xh_ref, rs_ref):
        z = alpha * x_ref[...] + coef * r_ref[...]
        mu = jnp.mean(z, axis=-1, keepdims=True)
        zc = z - mu
        var = jnp.mean(zc * zc, axis=-1, keepdims=True)
        rstd = lax.rsqrt(var + LN_EPS)
        xh = zc * rstd
        y_ref[...] = xh * g_ref[...] + b_ref[...]
        xh_ref[...] = xh
        rs_ref[...] = rstd

    sd = jax.ShapeDtypeStruct
    return _rows_call(name, body, t, tm, [x, r, g, b], [d, d, d, d],
                      [sd((t, d), F32), sd((t, d), F32), sd((t, 1), F32)])


def _ln_bwd(name, terms, xhat, rstd, g):
    t, d = xhat.shape
    tm = _tile(t, 256)
    n = len(terms)
    coefs = [c for _, c in terms]

    def body(*refs):
        t_refs = refs[:n]
        xh_ref, rs_ref, g_ref, dz_ref, dg_ref, db_ref = refs[n:]
        dy = coefs[0] * t_refs[0][...]
        for c, r in zip(coefs[1:], t_refs[1:]):
            dy = dy + c * r[...]
        xh = xh_ref[...]
        dxh = dy * g_ref[...]
        m1 = jnp.mean(dxh, axis=-1, keepdims=True)
        m2 = jnp.mean(dxh * xh, axis=-1, keepdims=True)
        dz_ref[...] = rs_ref[...] * (dxh - m1 - xh * m2)
        pg = jnp.sum(dy * xh, axis=0, keepdims=True)
        pb = jnp.sum(dy, axis=0, keepdims=True)
        i = pl.program_id(0)

        @pl.when(i == 0)
        def _():
            dg_ref[...] = pg
            db_ref[...] = pb

        @pl.when(i > 0)
        def _():
            dg_ref[...] += pg
            db_ref[...] += pb

    sd = jax.ShapeDtypeStruct
    arrs = [a for a, _ in terms] + [xhat, rstd, g]
    cols = [d] * n + [d, 1, d]
    return _rows_call(name, body, t, tm, arrs, cols, [sd((t, d), F32)], [sd((1, d), F32), sd((1, d), F32)])


def _lincomb(name, terms):
    t, d = terms[0][0].shape
    tm = _tile(t, 256)
    coefs = [c for _, c in terms]
    n = len(terms)

    def body(*refs):
        acc = coefs[0] * refs[0][...]
        for c, r in zip(coefs[1:], refs[1:n]):
            acc = acc + c * r[...]
        refs[n][...] = acc

    return _rows_call(name, body, t, tm, [a for a, _ in terms], [d] * n, [jax.ShapeDtypeStruct((t, d), F32)])[0]


def _loss_sum(name, y, tgt):
    t, d = y.shape
    tm = _tile(t, 256)

    def body(y_ref, t_ref, o_ref):
        e = y_ref[...] - t_ref[...]
        s = jnp.sum(jnp.sum(e * e, axis=1, keepdims=True), axis=0, keepdims=True)
        i = pl.program_id(0)

        @pl.when(i == 0)
        def _():
            o_ref[...] = s

        @pl.when(i > 0)
        def _():
            o_ref[...] += s

    return _rows_call(name, body, t, tm, [y, tgt], [d, d], [], [jax.ShapeDtypeStruct((1, 1), F32)])[0]


def _ffn_in(name, x, wall, layer):
    t, k = x.shape
    n = wall.shape[3]
    tm = _tile(t, 512)

    def body(x_ref, wg_ref, wu_ref, h_ref, a_ref):
        xb = x_ref[...].astype(BF16)
        g = lax.dot_general(xb, wg_ref[...], _NN, preferred_element_type=F32)
        u = lax.dot_general(xb, wu_ref[...], _NN, preferred_element_type=F32)
        h_ref[0] = g.astype(BF16)
        h_ref[1] = u.astype(BF16)
        a_ref[...] = (g * _sigmoid(g) * u).astype(BF16)

    return pl.pallas_call(
        body, grid=(2, t // tm), name=name,
        in_specs=[pl.BlockSpec((tm, k), lambda j, i: (i, 0)),
                  pl.BlockSpec((None, None, k, n), lambda j, i: (layer, j, 0, 0)),
                  pl.BlockSpec((None, None, k, n), lambda j, i: (layer, 2 + j, 0, 0))],
        out_specs=[pl.BlockSpec((2, tm, n), lambda j, i: (0, i, j)), pl.BlockSpec((tm, n), lambda j, i: (i, j))],
        out_shape=[jax.ShapeDtypeStruct((2, t, 2 * n), BF16), jax.ShapeDtypeStruct((t, 2 * n), BF16)],
        compiler_params=_cp(2),
    )(x, wall, wall)


def _ffn_da(name, dz, w3, layer, h):
    t, k = dz.shape
    f = w3.shape[1]
    n = f // 2
    tm = _tile(t, 512)

    def body(dz_ref, w_ref, g_ref, u_ref, dh_ref):
        d = 0.5 * lax.dot_general(dz_ref[...].astype(BF16), w_ref[...], _NT, preferred_element_type=F32)
        g = g_ref[...].astype(F32)
        u = u_ref[...].astype(F32)
        sg = _sigmoid(g)
        dh_ref[0] = (d * u * sg * (1.0 + g * (1.0 - sg))).astype(BF16)
        dh_ref[1] = (d * g * sg).astype(BF16)

    return pl.pallas_call(
        body, grid=(2, t // tm), name=name,
        in_specs=[pl.BlockSpec((tm, k), lambda j, i: (i, 0)),
                  pl.BlockSpec((None, n, k), lambda j, i: (layer, j, 0)),
                  pl.BlockSpec((None, tm, n), lambda j, i: (0, i, j)),
                  pl.BlockSpec((None, tm, n), lambda j, i: (1, i, j))],
        out_specs=pl.BlockSpec((2, tm, n), lambda j, i: (0, i, j)),
        out_shape=jax.ShapeDtypeStruct((2, t, f), BF16),
        compiler_params=_cp(2),
    )(dz, w3, h, h)


def _mm_ln(name, alpha, coef, a, w3, layer, x, g, b):
    t, k = a.shape
    d = w3.shape[2]
    tm = _tile(t, 512)

    def body(a_ref, w_ref, x_ref, g_ref, b_ref, y_ref, xh_ref, rs_ref):
        f = lax.dot_general(a_ref[...].astype(BF16), w_ref[...], _NN, preferred_element_type=F32)
        z = alpha * x_ref[...] + coef * f
        mu = jnp.mean(z, axis=-1, keepdims=True)
        zc = z - mu
        var = jnp.mean(zc * zc, axis=-1, keepdims=True)
        rstd = lax.rsqrt(var + LN_EPS)
        xh = zc * rstd
        y_ref[...] = xh * g_ref[...] + b_ref[...]
        xh_ref[...] = xh
        rs_ref[...] = rstd

    row = lambda c: pl.BlockSpec((tm, c), lambda i: (i, 0))
    vec = pl.BlockSpec((1, d), lambda i: (0, 0))
    sd = jax.ShapeDtypeStruct
    return pl.pallas_call(
        body, grid=(t // tm,), name=name,
        in_specs=[row(k), pl.BlockSpec((None, k, d), lambda i: (layer, 0, 0)), row(d), vec, vec],
        out_specs=[row(d), row(d), row(1)],
        out_shape=[sd((t, d), F32), sd((t, d), F32), sd((t, 1), F32)],
        compiler_params=_cp(1),
    )(a, w3, x, g, b)


_GELU_C = math.sqrt(2.0 / math.pi)


def _s5_act_fwd(name, ych, u, dvec):
    t, d = u.shape
    tm = _tile(t, 256)

    def body(y_ref, u_ref, d_ref, p_ref, a_ref):
        y = y_ref[...] + d_ref[...] * u_ref[...]
        p_ref[...] = y
        a_ref[...] = (0.5 * y * (1.0 + jnp.tanh(_GELU_C * (y + 0.044715 * y * y * y)))).astype(BF16)

    sd = jax.ShapeDtypeStruct
    return _rows_call(name, body, t, tm, [ych, u, dvec], [d, d, d], [sd((t, d), F32), sd((t, d), BF16)])


def _s5_act_bwd(name, dact, ypre, u, dvec):
    t, d = u.shape
    tm = _tile(t, 256)

    def body(da_ref, y_ref, u_ref, d_ref, dy_ref, ds_ref, dd_ref):
        y = y_ref[...]
        th = jnp.tanh(_GELU_C * (y + 0.044715 * y * y * y))
        dg = 0.5 * (1.0 + th) + 0.5 * y * (1.0 - th * th) * _GELU_C * (1.0 + 3.0 * 0.044715 * y * y)
        dy = da_ref[...] * dg
        dy_ref[...] = dy
        ds_ref[...] = dy * d_ref[...]
        pd = jnp.sum(dy * u_ref[...], axis=0, keepdims=True)
        i = pl.program_id(0)

        @pl.when(i == 0)
        def _():
            dd_ref[...] = pd

        @pl.when(i > 0)
        def _():
            dd_ref[...] += pd

    sd = jax.ShapeDtypeStruct
    return _rows_call(name, body, t, tm, [dact, ypre, u, dvec], [d, d, d, d],
                      [sd((t, d), F32), sd((t, d), F32)], [sd((1, d), F32)])


def _glu_fwd(name, vg):
    t, d2 = vg.shape
    d = d2 // 2
    tm = _tile(t, 256)

    def body(vg_ref, m_ref):
        m_ref[...] = vg_ref[:, :d] * _sigmoid(vg_ref[:, d:])

    return _rows_call(name, body, t, tm, [vg], [d2], [jax.ShapeDtypeStruct((t, d), F32)])[0]


def _glu_bwd(name, dm, vg):
    t, d2 = vg.shape
    d = d2 // 2
    tm = _tile(t, 256)

    def body(dm_ref, vg_ref, o_ref):
        sg = _sigmoid(vg_ref[:, d:])
        g = dm_ref[...]
        o_ref[:, :d] = (g * sg).astype(BF16)
        o_ref[:, d:] = (g * vg_ref[:, :d] * sg * (1.0 - sg)).astype(BF16)

    return _rows_call(name, body, t, tm, [dm, vg], [d, d2], [jax.ShapeDtypeStruct((t, d2), BF16)])[0]


def _adamw(name, w, g, m, v):
    r, c = w.shape
    tr = _tile(r, max(V7X_SUBLANES, (1 << 20) // (4 * c) // V7X_SUBLANES * V7X_SUBLANES))

    def body(w_ref, g_ref, m_ref, v_ref, d_ref, nm_ref, nv_ref):
        gg = g_ref[...]
        nm = ADAM_B1 * m_ref[...] + (1.0 - ADAM_B1) * gg
        nv = ADAM_B2 * v_ref[...] + (1.0 - ADAM_B2) * (gg * gg)
        m_hat = nm / (1.0 - ADAM_B1 ** ADAM_STEP)
        v_hat = nv / (1.0 - ADAM_B2 ** ADAM_STEP)
        d_ref[...] = -ADAM_LR * (m_hat / (jnp.sqrt(v_hat) + ADAM_EPS) + ADAM_WD * w_ref[...])
        nm_ref[...] = nm
        nv_ref[...] = nv

    sd = jax.ShapeDtypeStruct((r, c), F32)
    return _rows_call(name, body, r, tr, [w, g, m, v], [c] * 4, [sd, sd, sd])


def _my_shard():
    return 2 * lax.axis_index("x") + lax.axis_index("y")


def _my_core():
    return lax.axis_index("c")


def _adamw_join(name, w, mine, theirs, m, v):
    nl, r, c = w.shape
    h = r // 2
    tr = _tile(h, max(V7X_SUBLANES, (1 << 19) // (4 * c) // V7X_SUBLANES * V7X_SUBLANES))
    nb = h // tr

    def body(w_ref, a_ref, b_ref, m_ref, v_ref, g_ref, d_ref, nm_ref, nv_ref):
        gg = jnp.where(pl.program_id(1) == _my_core(), a_ref[...], b_ref[...])
        nm = ADAM_B1 * m_ref[...] + (1.0 - ADAM_B1) * gg
        nv = ADAM_B2 * v_ref[...] + (1.0 - ADAM_B2) * (gg * gg)
        m_hat = nm / (1.0 - ADAM_B1 ** ADAM_STEP)
        v_hat = nv / (1.0 - ADAM_B2 ** ADAM_STEP)
        g_ref[...] = gg
        d_ref[...] = -ADAM_LR * (m_hat / (jnp.sqrt(v_hat) + ADAM_EPS) + ADAM_WD * w_ref[...])
        nm_ref[...] = nm
        nv_ref[...] = nv

    full = pl.BlockSpec((None, tr, c), lambda l, hf, i: (l, hf * nb + i, 0))
    sd = jax.ShapeDtypeStruct((nl, r, c), F32)
    return pl.pallas_call(
        body, name=name, grid=(nl, 2, nb),
        in_specs=[full,
                  pl.BlockSpec((None, tr, c), lambda l, hf, i: (l, jnp.where(hf == _my_core(), i, 0), 0)),
                  pl.BlockSpec((None, tr, c), lambda l, hf, i: (l, jnp.where(hf == _my_core(), 0, i), 0)),
                  full, full],
        out_specs=[full, full, full, full],
        out_shape=[sd, sd, sd, sd],
        compiler_params=_cp(3),
    )(w, mine, theirs, m, v)


def _sum_leading(name, a):
    n, r, c = a.shape
    tr = _tile(r, 512)

    def body(a_ref, o_ref):
        acc = a_ref[0]
        for k in range(1, n):
            acc = acc + a_ref[k]
        o_ref[...] = acc

    return pl.pallas_call(
        body, grid=(r // tr,), name=name,
        in_specs=[pl.BlockSpec((n, tr, c), lambda i: (0, i, 0))],
        out_specs=pl.BlockSpec((tr, c), lambda i: (i, 0)),
        out_shape=jax.ShapeDtypeStruct((r, c), F32), compiler_params=_cp(1),
    )(a)


def _split3(x):
    hi = x.astype(BF16)
    r1 = x - hi.astype(F32)
    mid = r1.astype(BF16)
    lo = (r1 - mid.astype(F32)).astype(BF16)
    return hi, mid, lo


def _tri_sum(tri, x):
    dims = (((1,), (0,)), ((), ()))
    hi, mid, lo = _split3(x)
    out = lax.dot_general(tri, lo, dims, preferred_element_type=F32)
    out = out + lax.dot_general(tri, mid, dims, preferred_element_type=F32)
    return out + lax.dot_general(tri, hi, dims, preferred_element_type=F32)


def _fox_cumsum(name, fl, bf):
    t, h = fl.shape
    tb = _tile(t, 512)

    def body(fl_ref, bf_ref, c_ref, carry):
        i = pl.program_id(0)

        @pl.when(i == 0)
        def _():
            carry[...] = jnp.zeros_like(carry)

        x = fl_ref[...] + bf_ref[...]
        lf = jnp.minimum(x, 0.0) - jnp.log(1.0 + jnp.exp(-jnp.abs(x)))
        row = lax.broadcasted_iota(jnp.int32, (tb, tb), 0)
        col = lax.broadcasted_iota(jnp.int32, (tb, tb), 1)
        tri = jnp.where(row >= col, 1.0, 0.0).astype(BF16)
        c_ref[...] = _tri_sum(tri, lf) + carry[...]
        carry[...] += jnp.sum(lf, axis=0, keepdims=True)

    return pl.pallas_call(
        body, grid=(t // tb,), name=name,
        in_specs=[pl.BlockSpec((tb, h), lambda i: (i, 0)), pl.BlockSpec((1, h), lambda i: (0, 0))],
        out_specs=pl.BlockSpec((tb, h), lambda i: (i, 0)),
        out_shape=jax.ShapeDtypeStruct((t, h), F32),
        scratch_shapes=[pltpu.VMEM((1, h), F32)], compiler_params=_cp(1),
    )(fl, bf)


def _fox_cumsum_bwd(name, dcum, fl, bf):
    t, h = fl.shape
    tb = _tile(t, 512)
    nb = t // tb

    def body(dc_ref, fl_ref, bf_ref, df_ref, db_ref, carry):
        i = pl.program_id(0)

        @pl.when(i == 0)
        def _():
            carry[...] = jnp.zeros_like(carry)

        dc = dc_ref[...]
        row = lax.broadcasted_iota(jnp.int32, (tb, tb), 0)
        col = lax.broadcasted_iota(jnp.int32, (tb, tb), 1)
        tri = jnp.where(row <= col, 1.0, 0.0).astype(BF16)
        dlf = _tri_sum(tri, dc) + carry[...]
        carry[...] += jnp.sum(dc, axis=0, keepdims=True)
        x = fl_ref[...] + bf_ref[...]
        df = dlf / (1.0 + jnp.exp(x))
        df_ref[...] = df
        pb = jnp.sum(df, axis=0, keepdims=True)

        @pl.when(i == 0)
        def _():
            db_ref[...] = pb

        @pl.when(i > 0)
        def _():
            db_ref[...] += pb

    rev = lambda i: (nb - 1 - i, 0)
    return pl.pallas_call(
        body, grid=(nb,), name=name,
        in_specs=[pl.BlockSpec((tb, h), rev), pl.BlockSpec((tb, h), rev), pl.BlockSpec((1, h), lambda i: (0, 0))],
        out_specs=[pl.BlockSpec((tb, h), rev), pl.BlockSpec((1, h), lambda i: (0, 0))],
        out_shape=[jax.ShapeDtypeStruct((t, h), F32), jax.ShapeDtypeStruct((1, h), F32)],
        scratch_shapes=[pltpu.VMEM((1, h), F32)], compiler_params=_cp(1),
    )(dcum, fl, bf)


_NT = (((1,), (1,)), ((), ()))
_TN = (((0,), (0,)), ((), ()))
_NN = (((1,), (0,)), ((), ()))


def _causal_mask(s, tb):
    row = lax.broadcasted_iota(jnp.int32, (tb, tb), 0)
    col = lax.broadcasted_iota(jnp.int32, (tb, tb), 1)
    return jnp.where(col <= row, s, NEG_INF)


def _attn_fwd(name, q, k, v, ccol, crow):
    nh, nb, tb, hd = q.shape

    def body(q_ref, k_ref, v_ref, cc_ref, cr_ref, o_ref, lse_ref):
        i = pl.program_id(1)
        qi = q_ref[...]
        cc = cc_ref[...]

        def step(j, carry, diagonal=False):
            m, l, acc = carry
            s = lax.dot_general(qi, k_ref[j], _NT, preferred_element_type=F32) + cc - cr_ref[j]
            if diagonal:
                s = _causal_mask(s, tb)
            m_new = jnp.maximum(m, jnp.max(s, axis=1, keepdims=True))
            p = jnp.exp(s - m_new)
            a = jnp.exp(m - m_new)
            l = a * l + jnp.sum(p, axis=1, keepdims=True)
            acc = a * acc + lax.dot_general(p.astype(BF16), v_ref[j], _NN, preferred_element_type=F32)
            return m_new, l, acc

        init = (jnp.full((tb, 1), NEG_INF, F32), jnp.zeros((tb, 1), F32), jnp.zeros((tb, hd), F32))
        m, l, acc = step(i, lax.fori_loop(0, i, step, init), diagonal=True)
        o_ref[...] = acc / l
        lse_ref[...] = m + jnp.log(l)

    blk = lambda *s: pl.BlockSpec(s, lambda h, i: (h, i, 0, 0))
    head = lambda *s: pl.BlockSpec(s, lambda h, i: (h, 0, 0, 0))
    return pl.pallas_call(
        body, grid=(nh, nb), name=name,
        in_specs=[blk(None, None, tb, hd), head(None, nb, tb, hd), head(None, nb, tb, hd),
                  blk(None, None, tb, 1), head(None, nb, 1, tb)],
        out_specs=[blk(None, None, tb, hd), blk(None, None, tb, 1)],
        out_shape=[jax.ShapeDtypeStruct((nh, nb, tb, hd), F32), jax.ShapeDtypeStruct((nh, nb, tb, 1), F32)],
        compiler_params=_cp(2),
    )(q, k, v, ccol, crow)


def _attn_bwd(name, q, k, v, ccol, crow, o, lse, do, scale):
    nh, nb, tb, hd = q.shape

    def body(q_ref, k_ref, v_ref, cc_ref, cr_ref, o_ref, lse_ref, do_ref, dq_ref, dk_ref, dv_ref, dr_ref, dc_ref):
        j = pl.program_id(1)

        @pl.when(j == 0)
        def _():
            dq_ref[...] = jnp.zeros_like(dq_ref)
            dr_ref[...] = jnp.zeros_like(dr_ref)

        kj = k_ref[...]
        vj = v_ref[...]
        cr = cr_ref[...]

        def step(i, carry, diagonal=False):
            dk, dv, dc = carry
            qi = q_ref[i]
            doi = do_ref[i]
            dob = doi.astype(BF16)
            di = jnp.sum(doi * o_ref[i], axis=1, keepdims=True)
            s = lax.dot_general(qi, kj, _NT, preferred_element_type=F32) + cc_ref[i] - cr
            if diagonal:
                s = _causal_mask(s, tb)
            p = jnp.exp(s - lse_ref[i])
            dv = dv + lax.dot_general(p.astype(BF16), dob, _TN, preferred_element_type=F32)
            dp = lax.dot_general(dob, vj, _NT, preferred_element_type=F32)
            ds = p * (dp - di)
            dsb = ds.astype(BF16)
            dk = dk + lax.dot_general(dsb, qi, _TN, preferred_element_type=F32)
            dq_ref[i] += lax.dot_general(dsb, kj, _NN, preferred_element_type=F32) * scale
            dr_ref[i] += jnp.sum(ds, axis=1, keepdims=True)
            dc = dc + jnp.sum(ds, axis=0, keepdims=True)
            return dk, dv, dc

        init = (jnp.zeros((tb, hd), F32), jnp.zeros((tb, hd), F32), jnp.zeros((1, tb), F32))
        dk, dv, dc = lax.fori_loop(j + 1, nb, step, step(j, init, diagonal=True))
        dk_ref[...] = dk
        dv_ref[...] = dv
        dc_ref[...] = dc

    blk = lambda *s: pl.BlockSpec(s, lambda h, j: (h, j, 0, 0))
    head = lambda *s: pl.BlockSpec(s, lambda h, j: (h, 0, 0, 0))
    sd = jax.ShapeDtypeStruct
    return pl.pallas_call(
        body, grid=(nh, nb), name=name,
        in_specs=[head(None, nb, tb, hd), blk(None, None, tb, hd), blk(None, None, tb, hd),
                  head(None, nb, tb, 1), blk(None, None, 1, tb),
                  head(None, nb, tb, hd), head(None, nb, tb, 1), head(None, nb, tb, hd)],
        out_specs=[head(None, nb, tb, hd), blk(None, None, tb, hd), blk(None, None, tb, hd),
                   head(None, nb, tb, 1), blk(None, None, 1, tb)],
        out_shape=[sd((nh, nb, tb, hd), F32), sd((nh, nb, tb, hd), F32), sd((nh, nb, tb, hd), F32),
                   sd((nh, nb, tb, 1), F32), sd((nh, nb, 1, tb), F32)],
        compiler_params=_cp(2),
    )(q, k, v, ccol, crow, o, lse, do)


def _cmul(ar, ai, br, bi):
    return ar * br - ai * bi, ar * bi + ai * br


def _s5_scan(name, lam, xin, hs=None):
    reverse = hs is not None
    _, seg, ns, w = xin.shape
    assert ns == SCAN_SEGMENTS
    wb = min(w, 2 * V7X_LANES)
    nsq = seg.bit_length() - 1
    assert (1 << nsq) == seg

    def body(*refs):
        if reverse:
            lam_ref, x_ref, h_ref, o_ref, dl_ref = refs
        else:
            lam_ref, x_ref, o_ref = refs
        lr = jnp.broadcast_to(lam_ref[0], (ns, wb))
        li = jnp.broadcast_to(lam_ref[1], (ns, wb))
        if reverse:
            li = -li
        zero = jnp.zeros((ns, wb), F32)
        at = (lambda n: seg - 1 - n) if reverse else (lambda n: n)

        def local(n, c):
            r = at(n)
            mr, mi = _cmul(lr, li, c[0], c[1])
            nr = mr + x_ref[0, r]
            ni = mi + x_ref[1, r]
            o_ref[0, r] = nr
            o_ref[1, r] = ni
            return nr, ni

        er, ei = lax.fori_loop(0, seg, local, (zero, zero))
        pr, pi = lr, li
        for _ in range(nsq):
            pr, pi = _cmul(pr, pi, pr, pi)
        sub = lax.broadcasted_iota(jnp.int32, (ns, wb), 0)

        def shifted(a, sh):
            if reverse:
                return jnp.where(sub < ns - sh, pltpu.roll(a, ns - sh, 0), 0.0)
            return jnp.where(sub >= sh, pltpu.roll(a, sh, 0), 0.0)

        xr, xi = er, ei
        sh = 1
        while sh < ns:
            tr, ti = _cmul(pr, pi, shifted(xr, sh), shifted(xi, sh))
            xr, xi = xr + tr, xi + ti
            pr, pi = _cmul(pr, pi, pr, pi)
            sh *= 2
        cr, ci = shifted(xr, 1), shifted(xi, 1)

        def fix(r, q):
            tr, ti = _cmul(q[0], q[1], cr, ci)
            gr = o_ref[0, r] + tr
            gi = o_ref[1, r] + ti
            o_ref[0, r] = gr
            o_ref[1, r] = gi
            return gr, gi

        if not reverse:
            def fixup(n, q):
                fix(n, q)
                return _cmul(q[0], q[1], lr, li)

            lax.fori_loop(0, seg, fixup, (lr, li))
            return

        def fixup_acc(n, c):
            qr, qi, ar, ai = c
            r = seg - 1 - n
            gr, gi = fix(r, (qr, qi))
            hr = h_ref[0, r - 1]
            hi = h_ref[1, r - 1]
            qr, qi = _cmul(qr, qi, lr, li)
            return qr, qi, ar + gr * hr + gi * hi, ai + gi * hr - gr * hi

        qr, qi, ar, ai = lax.fori_loop(0, seg - 1, fixup_acc, (lr, li, zero, zero))
        gr, gi = fix(0, (qr, qi))
        hr = jnp.where(sub >= 1, pltpu.roll(h_ref[0, seg - 1], 1, 0), 0.0)
        hi = jnp.where(sub >= 1, pltpu.roll(h_ref[1, seg - 1], 1, 0), 0.0)
        dl_ref[0] = ar + gr * hr + gi * hi
        dl_ref[1] = ai + gi * hr - gr * hi

    big = pl.BlockSpec((2, seg, ns, wb), lambda j: (0, 0, 0, j))
    lam_spec = pl.BlockSpec((2, 1, wb), lambda j: (0, 0, j))
    sd = jax.ShapeDtypeStruct
    if reverse:
        return pl.pallas_call(
            body, grid=(w // wb,), name=name, in_specs=[lam_spec, big, big],
            out_specs=[big, pl.BlockSpec((2, ns, wb), lambda j: (0, 0, j))],
            out_shape=[sd(xin.shape, F32), sd((2, ns, w), F32)], compiler_params=_cp(1),
        )(lam, xin, hs)
    return pl.pallas_call(
        body, grid=(w // wb,), name=name, in_specs=[lam_spec, big], out_specs=big,
        out_shape=sd(xin.shape, F32), compiler_params=_cp(1),
    )(lam, xin)


def _place():
    x, y, c = lax.axis_index("x"), lax.axis_index("y"), lax.axis_index("c")
    chips = [(1 - x, y), (x, 1 - y), (1 - x, 1 - y)]
    return x, y, c, chips


def _comm_params():
    return pltpu.CompilerParams(vmem_limit_bytes=VMEM_LIMIT)


def _cast_place(name, w):
    nl, r, c = w.shape
    tr = _tile(r, max(16, (1 << 20) // (4 * c) // 16 * 16), 16)

    def body(w_ref, o_ref):
        o_ref[...] = w_ref[...].astype(BF16)

    return pl.pallas_call(
        body, name=name, grid=(nl, r // tr),
        in_specs=[pl.BlockSpec((None, tr, c), lambda l, i: (l, i, 0))],
        out_specs=pl.BlockSpec((None, None, tr, c), lambda l, i: (l, _my_shard(), i, 0)),
        out_shape=jax.ShapeDtypeStruct((nl, N_CHIPS, r, c), BF16),
        compiler_params=_cp(2),
    )(w)


def _gather_shards(name, bufs):
    n = len(bufs)

    def body(*refs):
        outs = refs[n:2 * n]
        send_sems, recv_sems = refs[2 * n:]
        x, y, c, chips = _place()
        my = 2 * x + y
        sibling = (x, y, 1 - c)

        def part(t, shard, half):
            h = bufs[t].shape[2] // 2
            return outs[t].at[:, shard, pl.ds(half * h, h)]

        def copy(t, k, ref, to):
            return pltpu.make_async_remote_copy(src_ref=ref, dst_ref=ref, send_sem=send_sems.at[t, k],
                                                recv_sem=recv_sems.at[t, k], device_id=to, device_id_type=MESH)

        sent = []
        for t in range(n):
            for k, chip in enumerate(chips):
                sent.append(copy(t, k, part(t, my, c), (*chip, c)))
                sent[-1].start()
        for k, chip in enumerate(chips):
            shard = 2 * chip[0] + chip[1]
            for t in range(n):
                copy(t, k, part(t, shard, c), (*chip, c)).wait_recv()
                sent.append(copy(t, 3 + k, part(t, shard, c), sibling))
                sent[-1].start()
        for k, chip in enumerate(chips):
            shard = 2 * chip[0] + chip[1]
            for t in range(n):
                copy(t, 3 + k, part(t, shard, 1 - c), sibling).wait_recv()
        for cp in sent:
            cp.wait_send()

    return pl.pallas_call(
        body, name=name, in_specs=[ANY] * n, out_specs=[ANY] * n,
        out_shape=[jax.ShapeDtypeStruct(b.shape, b.dtype) for b in bufs],
        input_output_aliases={t: t for t in range(n)},
        scratch_shapes=[pltpu.SemaphoreType.DMA((n, 6)), pltpu.SemaphoreType.DMA((n, 6))],
        compiler_params=_comm_params(),
    )(*bufs)


HBM_SPEC = pl.BlockSpec(memory_space=pltpu.HBM)
SEM_SPEC = pl.BlockSpec(memory_space=pltpu.SEMAPHORE)


def _split_params():
    return pltpu.CompilerParams(has_side_effects=pltpu.SideEffectType.DATAFLOW_SIDE_EFFECTING)


def _gather_start(name, bufs, groups):
    n, ng = len(bufs), len(groups)

    def body(*refs):
        sems = refs[n:n + 2 * ng]
        outs = refs[n + 2 * ng:]
        x, y, c, chips = _place()
        my = 2 * x + y
        for gi, group in enumerate(groups):
            for idx, (t, layer) in enumerate(group):
                block = outs[t].at[layer, my]
                for k, chip in enumerate(chips):
                    pltpu.make_async_remote_copy(
                        src_ref=block, dst_ref=block, send_sem=sems[2 * gi].at[3 * idx + k],
                        recv_sem=sems[2 * gi + 1].at[3 * idx + k], device_id=(*chip, c), device_id_type=MESH).start()

    sem_shapes = []
    for group in groups:
        sem_shapes += [pltpu.SemaphoreType.DMA((3 * len(group),))] * 2
    res = pl.pallas_call(
        body, name=name, in_specs=[HBM_SPEC] * n,
        out_specs=[SEM_SPEC] * (2 * ng) + [HBM_SPEC] * n,
        out_shape=sem_shapes + [pltpu.HBM(b.shape, b.dtype) for b in bufs],
        input_output_aliases={t: 2 * ng + t for t in range(n)},
        compiler_params=_split_params(),
    )(*[pltpu.with_memory_space_constraint(b, pltpu.HBM) for b in bufs])
    sems = [(res[2 * gi], res[2 * gi + 1]) for gi in range(ng)]
    return sems, list(res[2 * ng:])


def _gather_wait(name, bufs, send_sems, recv_sems, after, group):
    n = len(bufs)

    def body(*refs):
        ss, rs = refs[n], refs[n + 1]
        outs = refs[n + 3:]
        x, y, c, chips = _place()
        my = 2 * x + y
        for idx, (t, layer) in enumerate(group):
            for k, chip in enumerate(chips):
                cp = pltpu.make_async_remote_copy(
                    src_ref=outs[t].at[layer, my], dst_ref=outs[t].at[layer, 2 * chip[0] + chip[1]],
                    send_sem=ss.at[3 * idx + k], recv_sem=rs.at[3 * idx + k], device_id=(*chip, c), device_id_type=MESH)
                cp.wait_send()
                cp.wait_recv()

    return list(pl.pallas_call(
        body, name=name, in_specs=[HBM_SPEC] * n + [SEM_SPEC, SEM_SPEC, ANY],
        out_specs=[HBM_SPEC] * n,
        out_shape=[pltpu.HBM(b.shape, b.dtype) for b in bufs],
        input_output_aliases={t: t for t in range(n)},
        compiler_params=_split_params(),
    )(*bufs, send_sems, recv_sems, after))


def _swap_halves(name, gs):
    n = len(gs)

    def body(*refs):
        ins, outs = refs[:n], refs[n:2 * n]
        send_sems, recv_sems = refs[2 * n:]
        x, y, c, _ = _place()
        cps = []
        for t in range(n):
            h = gs[t].shape[2] // 2
            cps.append(pltpu.make_async_remote_copy(
                src_ref=ins[t].at[:, :, pl.ds((1 - c) * h, h), :], dst_ref=outs[t], send_sem=send_sems.at[t],
                recv_sem=recv_sems.at[t], device_id=(x, y, 1 - c), device_id_type=MESH))
            cps[-1].start()
        for cp in cps:
            cp.wait()

    return pl.pallas_call(
        body, name=name, in_specs=[ANY] * n, out_specs=[ANY] * n,
        out_shape=[jax.ShapeDtypeStruct(g.shape[:2] + (g.shape[2] // 2, g.shape[3]), g.dtype) for g in gs],
        scratch_shapes=[pltpu.SemaphoreType.DMA((n,)), pltpu.SemaphoreType.DMA((n,))],
        compiler_params=_comm_params(),
    )(*gs)


def _scatter_partials(name, ps):
    n = len(ps)

    def body(*refs):
        ins, outs = refs[:n], refs[n:2 * n]
        send_sems, recv_sems = refs[2 * n:]
        x, y, c, chips = _place()
        cps = []
        for t in range(n):
            for k, chip in enumerate(chips):
                shard = 2 * chip[0] + chip[1]
                cps.append(pltpu.make_async_remote_copy(
                    src_ref=ins[t].at[:, shard], dst_ref=outs[t].at[k], send_sem=send_sems.at[t, k],
                    recv_sem=recv_sems.at[t, k], device_id=(*chip, c), device_id_type=MESH))
                cps[-1].start()
        for cp in cps:
            cp.wait()

    return pl.pallas_call(
        body, name=name, in_specs=[ANY] * n, out_specs=[ANY] * n,
        out_shape=[jax.ShapeDtypeStruct((3, p.shape[0]) + p.shape[2:], p.dtype) for p in ps],
        scratch_shapes=[pltpu.SemaphoreType.DMA((n, 3)), pltpu.SemaphoreType.DMA((n, 3))],
        compiler_params=_comm_params(),
    )(*ps)


def _send_half(name, fs):
    n = len(fs)

    def body(*refs):
        ins, outs = refs[:n], refs[n:2 * n]
        send_sems, recv_sems = refs[2 * n:]
        x, y, c, _ = _place()
        cps = []
        for t in range(n):
            cps.append(pltpu.make_async_remote_copy(
                src_ref=ins[t], dst_ref=outs[t], send_sem=send_sems.at[t], recv_sem=recv_sems.at[t],
                device_id=(x, y, 1 - c), device_id_type=MESH))
            cps[-1].start()
        for cp in cps:
            cp.wait()

    return pl.pallas_call(
        body, name=name, in_specs=[ANY] * n, out_specs=[ANY] * n,
        out_shape=[jax.ShapeDtypeStruct(f.shape, f.dtype) for f in fs],
        scratch_shapes=[pltpu.SemaphoreType.DMA((n,)), pltpu.SemaphoreType.DMA((n,))],
        compiler_params=_comm_params(),
    )(*fs)


def _gather_all(name, v):
    m_per = v.shape[0]

    def body(x_ref, out_ref, send_sems, recv_sems, local_sem):
        x, y, c, chips = _place()
        me, sibling = (x, y, c), (x, y, 1 - c)

        def rows(px, py, pc):
            return out_ref.at[pl.ds((4 * px + 2 * py + pc) * m_per, m_per), :]

        def copy(k, block, to, src=None):
            return pltpu.make_async_remote_copy(
                src_ref=rows(*block) if src is None else src, dst_ref=rows(*block), send_sem=send_sems.at[k],
                recv_sem=recv_sems.at[k], device_id=to, device_id_type=MESH)

        mine = pltpu.make_async_copy(x_ref, rows(*me), local_sem)
        mine.start()
        first = [copy(0, me, sibling, src=x_ref)]
        first += [copy(1 + j, me, (*chip, c), src=x_ref) for j, chip in enumerate(chips)]
        for cp in first:
            cp.start()
        passed = [copy(4 + j, (*chip, c), sibling) for j, chip in enumerate(chips)]
        for j, chip in enumerate(chips):
            copy(1 + j, (*chip, c), me).wait_recv()
            passed[j].start()
        copy(0, sibling, me).wait_recv()
        for j, chip in enumerate(chips):
            copy(4 + j, (*chip, 1 - c), me).wait_recv()
        for cp in first + passed:
            cp.wait_send()
        mine.wait()

    return pl.pallas_call(
        body, name=name, in_specs=[ANY], out_specs=ANY,
        out_shape=jax.ShapeDtypeStruct((N_DEV * m_per, v.shape[1]), v.dtype),
        scratch_shapes=[pltpu.SemaphoreType.DMA((7,)), pltpu.SemaphoreType.DMA((7,)), pltpu.SemaphoreType.DMA],
        compiler_params=_comm_params(),
    )(v)


def _pair_sum(name, g, recv):
    s, r, c = g.shape
    h = r // 2
    tr = _tile(h, max(V7X_SUBLANES * 2, (1 << 20) // (4 * c) // 16 * 16), 16)
    nb = h // tr

    def body(g_ref, r_ref, p_ref, pb_ref):
        v = g_ref[...] + r_ref[...]
        p_ref[...] = v
        pb_ref[...] = v.astype(BF16)

    spec = pl.BlockSpec((None, tr, c), lambda k, i: (k, i, 0))
    return pl.pallas_call(
        body, name=name, grid=(s, nb),
        in_specs=[pl.BlockSpec((None, tr, c), lambda k, i: (k, _my_core() * nb + i, 0)), spec],
        out_specs=[spec, spec],
        out_shape=[jax.ShapeDtypeStruct((s, h, c), F32), jax.ShapeDtypeStruct((s, h, c), BF16)],
        compiler_params=_cp(2),
    )(g, recv)


def _chip_sum(name, p, recv):
    nl, s, h, c = p.shape
    tr = _tile(h, max(V7X_SUBLANES * 2, (1 << 20) // (4 * c) // 16 * 16), 16)

    def body(p_ref, r_ref, o_ref):
        acc = p_ref[...]
        for k in range(3):
            acc = acc + r_ref[k].astype(F32)
        o_ref[...] = acc

    return pl.pallas_call(
        body, name=name, grid=(nl, h // tr),
        in_specs=[pl.BlockSpec((None, None, tr, c), lambda l, i: (l, _my_shard(), i, 0)),
                  pl.BlockSpec((3, None, tr, c), lambda l, i: (0, l, i, 0))],
        out_specs=pl.BlockSpec((None, tr, c), lambda l, i: (l, i, 0)),
        out_shape=jax.ShapeDtypeStruct((nl, h, c), F32),
        compiler_params=_cp(2),
    )(p, recv)


def _rows_view(wall):
    nl, s, r, c = wall.shape
    return wall.reshape(nl, s * r, c)


def _ffn_fwd(tag, alpha, x, w_in, w_out3, layer, g, b):
    h, a = _ffn_in(f"{tag}_in", x, w_in, layer)
    y, xhat, rstd = _mm_ln(f"{tag}_out", alpha, 0.5, a, w_out3, layer, x, g, b)
    return y, (x, h, a, xhat, rstd)


def _ffn_bwd(tag, alpha, terms, saved, w_in, w_out3, layer, g, g_win, g_wout3):
    x, h, a, xhat, rstd = saved
    t = x.shape[0]
    _, s, k, n = w_in.shape
    tm = _tile(t, 512)
    dz, dg, db = _ln_bwd(f"{tag}_ln_bwd", terms, xhat, rstd, g)
    g_wout3 = _mm_tn(f"{tag}_dwout", a, dz, tm=n, scale=0.5, layer=layer, shape=w_out3.shape, into=g_wout3)
    dh = _ffn_da(f"{tag}_da", dz, w_out3, layer, h)
    g_win = _mm(f"{tag}_dwin", x, dh, mode="tn", grid=(s, t // tm), kaxis=1,
                a_blk=(tm, k), a_map=lambda j, kk: (kk, 0),
                b_blk=(None, tm, n), b_map=lambda j, kk: (j // 2, kk, j % 2),
                o_shape=w_in.shape, o_blk=(None, None, k, n), o_map=lambda j, kk: (layer, j, 0, 0), into=g_win)
    dx = _mm(f"{tag}_dx", dh, w_in, mode="nt", grid=(t // tm, s), kaxis=1,
             a_blk=(None, tm, n), a_map=lambda i, kk: (kk // 2, i, kk % 2),
             b_blk=(None, None, k, n), b_map=lambda i, kk: (layer, kk, 0, 0),
             o_shape=(t, k), o_blk=(tm, k), o_map=lambda i, kk: (i, 0))
    return [(dz, alpha), (dx, 1.0)], g_win, g_wout3, dg, db


def _heads(a, nh, tb):
    t = a.shape[0]
    return a.reshape(t, nh, -1).transpose(1, 0, 2).reshape(nh, t // tb, tb, -1)


def _unheads(a):
    nh, nb, tb, hd = a.shape
    return a.reshape(nh, nb * tb, hd).transpose(1, 0, 2).reshape(nb * tb, nh * hd)


def _fox_fwd(tag, alpha, x, w_pad, bf, w_o3, layer, g, b):
    t, d = x.shape
    nh = bf.shape[1]
    hd = d // nh
    tb = _tile(t, 512)
    scale = 1.0 / math.sqrt(hd)
    proj = _mm_nn(f"{tag}_proj", x, w_pad, tn=_tile(w_pad.shape[1], 640, V7X_LANES))
    q = _heads((proj[:, :d].astype(BF16) * scale).astype(BF16), nh, tb)
    k = _heads(proj[:, d:2 * d].astype(BF16), nh, tb)
    v = _heads(proj[:, 2 * d:3 * d].astype(BF16), nh, tb)
    fl = proj[:, 3 * d:3 * d + nh]
    cum = _fox_cumsum(f"{tag}_cum", fl, bf)
    ccol = cum.T.reshape(nh, t // tb, tb, 1)
    crow = cum.T.reshape(nh, t // tb, 1, tb)
    o, lse = _attn_fwd(f"{tag}_attn", q, k, v, ccol, crow)
    o2 = _unheads(o)
    y, xhat, rstd = _mm_ln(f"{tag}_oproj", alpha, 1.0, o2, w_o3, layer, x, g, b)
    return y, xhat, rstd, (x, q, k, v, ccol, crow, o, lse, o2, fl, scale)


def _fox_bwd(tag, dm, saved, w_pad, bf, w_o3, layer, g_wo3):
    x, q, k, v, ccol, crow, o, lse, o2, fl, scale = saved
    t, d = x.shape
    nh, nb, tb, hd = q.shape
    g_wo3 = _mm_tn(f"{tag}_dwo", o2, dm, layer=layer, shape=w_o3.shape, into=g_wo3)
    do2 = _mm_nt(f"{tag}_do", dm, w_o3, layer=layer)
    do = _heads(do2, nh, tb)
    dq, dk, dv, drow, dcol = _attn_bwd(f"{tag}_attn_bwd", q, k, v, ccol, crow, o, lse, do, scale)
    dcum = (drow.reshape(nh, t) - dcol.reshape(nh, t)).T
    dfl, dbf = _fox_cumsum_bwd(f"{tag}_cum_bwd", dcum, fl, bf)
    pad = w_pad.shape[1] - 3 * d - nh
    dproj = jnp.concatenate([_unheads(dq).astype(BF16), _unheads(dk).astype(BF16), _unheads(dv).astype(BF16),
                             dfl.astype(BF16), jnp.zeros((t, pad), BF16)], axis=1)
    d_wpad = _mm_tn(f"{tag}_dwin", x, dproj, tn=_tile(w_pad.shape[1], 640, V7X_LANES))
    dx = _mm_nt(f"{tag}_dx", dproj, w_pad)
    return dx, d_wpad, dbf, g_wo3


def _to_segments(a):
    t, d = a.shape
    return a.reshape(SCAN_SEGMENTS, t // SCAN_SEGMENTS, d).transpose(1, 0, 2).reshape(t, d)


def _from_segments(a):
    t, d = a.shape
    return a.reshape(t // SCAN_SEGMENTS, SCAN_SEGMENTS, d).transpose(1, 0, 2).reshape(t, d)


def _s5_discretise(a_re, a_im, log_dt, b_re, b_im):
    dt = jnp.exp(log_dt)[:, None]
    mag = jnp.exp(a_re * dt)
    ang = a_im * dt
    lb_re = mag * jnp.cos(ang)
    lb_im = mag * jnp.sin(ang)
    den = a_re * a_re + a_im * a_im
    nr = lb_re - 1.0
    ni = lb_im
    z_re = (nr * a_re + ni * a_im) / den
    z_im = (ni * a_re - nr * a_im) / den
    bb_re = z_re[..., None] * b_re - z_im[..., None] * b_im
    bb_im = z_re[..., None] * b_im + z_im[..., None] * b_re
    return lb_re, lb_im, bb_re, bb_im


S5_BLOCK_GROUPS = 8


def _blockdiag_in(bb):
    g, p, h = bb.shape
    e = jnp.eye(S5_BLOCK_GROUPS, dtype=bb.dtype)
    b4 = bb.reshape(g // S5_BLOCK_GROUPS, S5_BLOCK_GROUPS, p, h)
    return jnp.einsum("jgph,gf->jghfp", b4, e).reshape(g // S5_BLOCK_GROUPS, S5_BLOCK_GROUPS * h, S5_BLOCK_GROUPS * p)


def _blockdiag_in_grad(d):
    nj, gh, gp = d.shape
    h, p = gh // S5_BLOCK_GROUPS, gp // S5_BLOCK_GROUPS
    e = jnp.eye(S5_BLOCK_GROUPS, dtype=d.dtype)
    d6 = d.reshape(nj, S5_BLOCK_GROUPS, h, S5_BLOCK_GROUPS, p)
    return jnp.einsum("jghfp,gf->jgph", d6, e).reshape(nj * S5_BLOCK_GROUPS, p, h)


def _blockdiag_out(cc):
    g, h, p = cc.shape
    e = jnp.eye(S5_BLOCK_GROUPS, dtype=cc.dtype)
    c4 = cc.reshape(g // S5_BLOCK_GROUPS, S5_BLOCK_GROUPS, h, p)
    return jnp.einsum("jghp,gf->jfpgh", c4, e).reshape(g // S5_BLOCK_GROUPS, S5_BLOCK_GROUPS * p, S5_BLOCK_GROUPS * h)


def _blockdiag_out_grad(d):
    nj, gp, gh = d.shape
    h, p = gh // S5_BLOCK_GROUPS, gp // S5_BLOCK_GROUPS
    e = jnp.eye(S5_BLOCK_GROUPS, dtype=d.dtype)
    d6 = d.reshape(nj, S5_BLOCK_GROUPS, p, S5_BLOCK_GROUPS, h)
    return jnp.einsum("jfpgh,gf->jghp", d6, e).reshape(nj * S5_BLOCK_GROUPS, h, p)


def _s5_fwd(tag, x, prm, w_out, layer):
    a_re, a_im, log_dt, b_re, b_im, c_re, c_im, d_skip = prm
    t, d = x.shape
    g, p = a_re.shape
    w = g * p
    nj = g // S5_BLOCK_GROUPS
    cw, sw = S5_BLOCK_GROUPS * S5_GROUP, S5_BLOCK_GROUPS * p
    seg = t // SCAN_SEGMENTS
    tm = _tile(t, 4096)
    lb_re, lb_im, bb_re, bb_im = _s5_discretise(a_re, a_im, log_dt, b_re, b_im)
    lam = jnp.stack([lb_re.reshape(1, w), lb_im.reshape(1, w)])
    bs = jnp.stack([_blockdiag_in(bb_re), _blockdiag_in(bb_im)]).astype(BF16)
    cs = jnp.stack([_blockdiag_out(c_re), -_blockdiag_out(c_im)]).astype(BF16)
    dvec = d_skip.reshape(1, d)
    u = _to_segments(x)
    bu = _mm(f"{tag}_bu", u, bs, mode="nn", grid=(2, nj, t // tm), kaxis=None,
             a_blk=(tm, cw), a_map=lambda r, j, i: (i, j),
             b_blk=(None, None, cw, sw), b_map=lambda r, j, i: (r, j, 0, 0),
             o_shape=(2, t, w), o_blk=(None, tm, sw), o_map=lambda r, j, i: (r, i, j))
    hs = _s5_scan(f"{tag}_scan", lam, bu.reshape(2, seg, SCAN_SEGMENTS, w)).reshape(2, t, w)
    ych = _mm(f"{tag}_ch", hs, cs, mode="nn", grid=(nj, t // tm, 2), kaxis=2,
              a_blk=(None, tm, sw), a_map=lambda j, i, r: (r, i, j),
              b_blk=(None, None, sw, cw), b_map=lambda j, i, r: (r, j, 0, 0),
              o_shape=(t, d), o_blk=(tm, cw), o_map=lambda j, i, r: (i, j))
    ypre, act = _s5_act_fwd(f"{tag}_act", ych, u, dvec)
    vg = _mm_shards_nn(f"{tag}_wout", act, w_out, layer, F32)
    m = _from_segments(_glu_fwd(f"{tag}_glu", vg))
    return m, (u, lam, bs, cs, dvec, hs, ypre, act, vg)


def _s5_bwd(tag, dm, saved, prm, w_out, layer, g_wout):
    a_re, a_im, log_dt, b_re, b_im, c_re, c_im, d_skip = prm
    u, lam, bs, cs, dvec, hs, ypre, act, vg = saved
    t, d = u.shape
    g, p = a_re.shape
    w = g * p
    nj = g // S5_BLOCK_GROUPS
    cw, sw = S5_BLOCK_GROUPS * S5_GROUP, S5_BLOCK_GROUPS * p
    seg = t // SCAN_SEGMENTS
    tm = _tile(t, 4096)
    dvg = _glu_bwd(f"{tag}_glu_bwd", _to_segments(dm), vg)
    g_wout = _mm_shards_tn(f"{tag}_dwout", act, dvg, layer, w_out.shape, g_wout)
    dact = _mm_shards_nt(f"{tag}_dact", dvg, w_out, layer)
    dypre, duskip, dd = _s5_act_bwd(f"{tag}_act_bwd", dact, ypre, u, dvec)
    dh = _mm(f"{tag}_dh", dypre, cs, mode="nt", grid=(2, nj, t // tm), kaxis=None,
             a_blk=(tm, cw), a_map=lambda r, j, i: (i, j),
             b_blk=(None, None, sw, cw), b_map=lambda r, j, i: (r, j, 0, 0),
             o_shape=(2, t, w), o_blk=(None, tm, sw), o_map=lambda r, j, i: (r, i, j))
    dcs = _mm(f"{tag}_dc", hs, dypre, mode="tn", grid=(2, nj, t // tm), kaxis=2,
              a_blk=(None, tm, sw), a_map=lambda r, j, i: (r, i, j),
              b_blk=(tm, cw), b_map=lambda r, j, i: (i, j),
              o_shape=(2, nj, sw, cw), o_blk=(None, None, sw, cw), o_map=lambda r, j, i: (r, j, 0, 0))
    gs, dlam8 = _s5_scan(f"{tag}_scan_bwd", lam, dh.reshape(2, seg, SCAN_SEGMENTS, w),
                         hs.reshape(2, seg, SCAN_SEGMENTS, w))
    gs = gs.reshape(2, t, w)
    du = _mm(f"{tag}_du", gs, bs, mode="nt", grid=(nj, t // tm, 2), kaxis=2,
             a_blk=(None, tm, sw), a_map=lambda j, i, r: (r, i, j),
             b_blk=(None, None, cw, sw), b_map=lambda j, i, r: (r, j, 0, 0),
             o_shape=(t, d), o_blk=(tm, cw), o_map=lambda j, i, r: (i, j))
    dbs = _mm(f"{tag}_db", u, gs, mode="tn", grid=(2, nj, t // tm), kaxis=2,
              a_blk=(tm, cw), a_map=lambda r, j, i: (i, j),
              b_blk=(None, tm, sw), b_map=lambda r, j, i: (r, i, j),
              o_shape=(2, nj, cw, sw), o_blk=(None, None, cw, sw), o_map=lambda r, j, i: (r, j, 0, 0))
    dx = _from_segments(du + duskip)
    dlam = jnp.sum(dlam8, axis=1).reshape(2, g, p)
    small = dict(dlb_re=dlam[0], dlb_im=dlam[1],
                 dbb_re=_blockdiag_in_grad(dbs[0]), dbb_im=_blockdiag_in_grad(dbs[1]),
                 dc_re=_blockdiag_out_grad(dcs[0]), dc_im=-_blockdiag_out_grad(dcs[1]),
                 dd=dd.reshape(g, S5_GROUP))
    return dx, g_wout, small


def _pack(pieces):
    flat = jnp.concatenate([p.reshape(-1).astype(F32) for p in pieces])
    n = flat.shape[0]
    unit = V7X_SUBLANES * V7X_LANES
    total = -(-n // unit) * unit
    return jnp.pad(flat, (0, total - n)).reshape(total // V7X_LANES, V7X_LANES)


def _unpack(buf, shapes):
    flat = buf.reshape(-1)
    out, off = [], 0
    for s in shapes:
        n = math.prod(s)
        out.append(flat[off:off + n].reshape(s))
        off += n
    return out


def kernel(x, ffn1_w_in, ffn1_w_out, ln1_g, ln1_b, lnm_g, lnm_b, ffn2_w_in, ffn2_w_out, ln2_g, ln2_b, fox_w_in, fox_b_f, fox_w_o, s5_a_re, s5_a_im, s5_log_dt, s5_b_re, s5_b_im, s5_c_re, s5_c_im, s5_d, s5_w_out, loss_target, m_ffn1_w_in, m_ffn1_w_out, m_ln1_g, m_ln1_b, m_lnm_g, m_lnm_b, m_ffn2_w_in, m_ffn2_w_out, m_ln2_g, m_ln2_b, m_fox_w_in, m_fox_b_f, m_fox_w_o, m_s5_a_re, m_s5_a_im, m_s5_log_dt, m_s5_b_re, m_s5_b_im, m_s5_c_re, m_s5_c_im, m_s5_d, m_s5_w_out, v_ffn1_w_in, v_ffn1_w_out, v_ln1_g, v_ln1_b, v_lnm_g, v_lnm_b, v_ffn2_w_in, v_ffn2_w_out, v_ln2_g, v_ln2_b, v_fox_w_in, v_fox_b_f, v_fox_w_o, v_s5_a_re, v_s5_a_im, v_s5_log_dt, v_s5_b_re, v_s5_b_im, v_s5_c_re, v_s5_c_im, v_s5_d, v_s5_w_out):
    big_names = ["ffn1_w_in", "ffn1_w_out", "ffn2_w_in", "ffn2_w_out", "fox_w_in", "fox_w_o", "s5_w_out"]
    small_names = ["ln1_g", "ln1_b", "lnm_g", "lnm_b", "ln2_g", "ln2_b", "fox_b_f", "s5_a_re", "s5_a_im", "s5_log_dt",
                   "s5_b_re", "s5_b_im", "s5_c_re", "s5_c_im", "s5_d"]
    out_order = ["ffn1_w_in", "ffn1_w_out", "ln1_g", "ln1_b", "lnm_g", "lnm_b", "ffn2_w_in", "ffn2_w_out", "ln2_g",
                 "ln2_b", "fox_w_in", "fox_b_f", "fox_w_o", "s5_a_re", "s5_a_im", "s5_log_dt", "s5_b_re", "s5_b_im",
                 "s5_c_re", "s5_c_im", "s5_d", "s5_w_out"]
    env = dict(locals())
    w = {n: env[n] for n in out_order}
    mom = {n: env["m_" + n] for n in out_order}
    vel = {n: env["v_" + n] for n in out_order}

    depth, d = ln1_g.shape
    t = x.shape[1]
    alpha = (2.0 * depth) ** 0.25
    x0 = x.reshape(t, d)
    tgt = loss_target.reshape(t, d)

    tix = {n: k for k, n in enumerate(big_names)}
    groups = []
    for i in range(depth):
        j = i // 2
        groups.append([(tix["ffn1_w_in"], i), (tix["ffn1_w_out"], i)])
        mixer = [(tix["fox_w_in"], j), (tix["fox_w_o"], j)] if i % 2 == 0 else [(tix["s5_w_out"], j)]
        groups.append(mixer + [(tix["ffn2_w_in"], i), (tix["ffn2_w_out"], i)])
    sems, bufs = _gather_start("gather_start", [_cast_place(f"cast_{n}", w[n]) for n in big_names], groups)
    full, rows3 = {}, {}

    def arrive(gi, after):
        nonlocal bufs
        bufs = _gather_wait(f"gather_wait_{gi}", bufs, sems[gi][0], sems[gi][1], after, groups[gi])
        full.update(zip(big_names, bufs))
        rows3.update({n: _rows_view(full[n]) for n in ("ffn1_w_out", "ffn2_w_out", "fox_w_o")})

    nh = fox_b_f.shape[1]
    fox_cols = 3 * d + nh
    fox_pad = -(-fox_cols // (5 * V7X_LANES)) * (5 * V7X_LANES)

    def fox_wpad(j):
        wf = full["fox_w_in"][j].transpose(1, 0, 2).reshape(d, fox_cols)
        return jnp.pad(wf, ((0, 0), (0, fox_pad - fox_cols)))

    def s5_params(j):
        return (s5_a_re[j], s5_a_im[j], s5_log_dt[j], s5_b_re[j], s5_b_im[j], s5_c_re[j], s5_c_im[j], s5_d[j])

    saved = []
    h = x0
    for i in range(depth):
        j = i // 2
        arrive(2 * i, h)
        h, s1 = _ffn_fwd(f"l{i}_ffn1", alpha, h, full["ffn1_w_in"], rows3["ffn1_w_out"], i,
                         ln1_g[i:i + 1], ln1_b[i:i + 1])
        arrive(2 * i + 1, h)
        if i % 2 == 0:
            h, xhat_m, rstd_m, sm = _fox_fwd(f"l{i}_fox", alpha, h, fox_wpad(j), fox_b_f[j:j + 1], rows3["fox_w_o"], j,
                                             lnm_g[i:i + 1], lnm_b[i:i + 1])
        else:
            m, sm = _s5_fwd(f"l{i}_s5", h, s5_params(j), full["s5_w_out"], j)
            h, xhat_m, rstd_m = _ln_fwd(f"l{i}_lnm", alpha, h, m, 1.0, lnm_g[i:i + 1], lnm_b[i:i + 1])
        h, s2 = _ffn_fwd(f"l{i}_ffn2", alpha, h, full["ffn2_w_in"], rows3["ffn2_w_out"], i,
                         ln2_g[i:i + 1], ln2_b[i:i + 1])
        saved.append((s1, sm, (xhat_m, rstd_m), s2))
    loss_part = _loss_sum("loss", h, tgt) * (0.5 / d)

    gbuf = {n: None for n in big_names}
    gfox_in = [None] * fox_w_in.shape[0]
    gsmall = {n: [None] * w[n].shape[0] for n in small_names}
    s5_cot = [None] * s5_a_re.shape[0]
    terms = [(h, 1.0 / d), (tgt, -1.0 / d)]
    for i in reversed(range(depth)):
        j = i // 2
        s1, sm, (xhat_m, rstd_m), s2 = saved[i]
        terms, gbuf["ffn2_w_in"], gbuf["ffn2_w_out"], dg, db = _ffn_bwd(
            f"l{i}_ffn2", alpha, terms, s2, full["ffn2_w_in"], rows3["ffn2_w_out"], i, ln2_g[i:i + 1],
            gbuf["ffn2_w_in"], gbuf["ffn2_w_out"])
        gsmall["ln2_g"][i], gsmall["ln2_b"][i] = dg, db
        dz, dg, db = _ln_bwd(f"l{i}_lnm_bwd", terms, xhat_m, rstd_m, lnm_g[i:i + 1])
        gsmall["lnm_g"][i], gsmall["lnm_b"][i] = dg, db
        if i % 2 == 0:
            dx, d_wpad, dbf, gbuf["fox_w_o"] = _fox_bwd(f"l{i}_fox", dz, sm, fox_wpad(j), fox_b_f[j:j + 1],
                                                       rows3["fox_w_o"], j, gbuf["fox_w_o"])
            gfox_in[j] = d_wpad[:, :fox_cols].reshape(d, N_CHIPS, -1).transpose(1, 0, 2)
            gsmall["fox_b_f"][j] = dbf
        else:
            dx, gbuf["s5_w_out"], s5_cot[j] = _s5_bwd(f"l{i}_s5", dz, sm, s5_params(j), full["s5_w_out"], j,
                                                      gbuf["s5_w_out"])
        terms = [(dz, alpha), (dx, 1.0)]
        terms, gbuf["ffn1_w_in"], gbuf["ffn1_w_out"], dg, db = _ffn_bwd(
            f"l{i}_ffn1", alpha, terms, s1, full["ffn1_w_in"], rows3["ffn1_w_out"], i, ln1_g[i:i + 1],
            gbuf["ffn1_w_in"], gbuf["ffn1_w_out"])
        gsmall["ln1_g"][i], gsmall["ln1_b"][i] = dg, db
    grad_x = _lincomb("grad_x", terms).reshape(x.shape)
    gbuf["fox_w_in"] = jnp.stack(gfox_in)

    g4 = [gbuf[n].reshape(full[n].shape) for n in big_names]
    from_sibling = _swap_halves("grad_pair_swap", g4)
    pair = []
    for n, g, r in zip(big_names, g4, from_sibling):
        p, pb = _pair_sum(f"grad_pair_sum_{n}", g.reshape((-1,) + g.shape[2:]), r.reshape((-1,) + r.shape[2:]))
        pair.append((p.reshape(r.shape), pb.reshape(r.shape)))
    from_chips = _scatter_partials("grad_chip_scatter", [pb for _, pb in pair])
    halves = [_chip_sum(f"grad_chip_sum_{n}", p, r) for n, (p, _), r in zip(big_names, pair, from_chips)]
    theirs = _send_half("grad_send_half", halves)

    cot_names = ["dlb_re", "dlb_im", "dbb_re", "dbb_im", "dc_re", "dc_im", "dd"]
    ln_names = ["ln1_g", "ln1_b", "lnm_g", "lnm_b", "ln2_g", "ln2_b"]
    pieces = [loss_part] + [jnp.concatenate(gsmall[n], axis=0) for n in ln_names + ["fox_b_f"]]
    pieces += [jnp.stack([s5_cot[j][n] for j in range(len(s5_cot))]) for n in cot_names]
    shapes = [p.shape for p in pieces]
    mine = _pack(pieces)
    everyone = _gather_all("small_gather", mine).reshape(N_DEV, *mine.shape)
    summed = _unpack(_sum_leading("small_sum", everyone), shapes)
    loss = summed[0].reshape(())
    gs_final = dict(zip(ln_names + ["fox_b_f"], summed[1:8]))
    cot = dict(zip(cot_names, summed[8:]))
    prm_names = ["s5_a_re", "s5_a_im", "s5_log_dt", "s5_b_re", "s5_b_im"]
    _, disc_vjp = jax.vjp(jax.vmap(_s5_discretise), *[w[n] for n in prm_names])
    for n, gval in zip(prm_names, disc_vjp((cot["dlb_re"], cot["dlb_im"], cot["dbb_re"], cot["dbb_im"]))):
        gs_final[n] = gval
    gs_final["s5_c_re"], gs_final["s5_c_im"], gs_final["s5_d"] = cot["dc_re"], cot["dc_im"], cot["dd"]

    grads, deltas, new_m, new_v = {}, {}, {}, {}
    for n, mine_h, their_h in zip(big_names, halves, theirs):
        grads[n], deltas[n], new_m[n], new_v[n] = _adamw_join(f"adamw_{n}", w[n], mine_h, their_h, mom[n], vel[n])
    small_shapes = [w[n].shape for n in small_names]
    for n in small_names:
        grads[n] = gs_final[n].reshape(w[n].shape)
    packed = [_pack([src[n] for n in small_names]) for src in (w, grads, mom, vel)]
    for dst, buf in zip((deltas, new_m, new_v), _adamw("adamw_small", *packed)):
        for n, val in zip(small_names, _unpack(buf, small_shapes)):
            dst[n] = val
    return (loss, grad_x, *[grads[n] for n in out_order], *[deltas[n] for n in out_order],
            *[new_m[n] for n in out_order], *[new_v[n] for n in out_order])
```

```python
import functools
import math

import jax
import jax.numpy as jnp
from jax import lax
from jax.experimental import pallas as pl
from jax.experimental.pallas import tpu as pltpu

F32 = jnp.float32
BF16 = jnp.bfloat16
LN_EPS = 1e-5
NEG_INF = -1e30
ADAM_LR = 0.001
ADAM_B1 = 0.9
ADAM_B2 = 0.999
ADAM_EPS = 1e-08
ADAM_WD = 0.01
ADAM_STEP = 10
S5_GROUP = 16
SCAN_SEGMENTS = 8
V7X_SUBLANES = 8
V7X_LANES = 128
VMEM_LIMIT = 56 * 1024 * 1024
N_CHIPS = 4
N_DEV = 8
MESH = pl.DeviceIdType.MESH
ANY = pl.BlockSpec(memory_space=pl.ANY)


def _cp(n_grid, kaxis=None):
    sem = tuple("arbitrary" if (kaxis is None or i == kaxis) else "parallel" for i in range(n_grid))
    return pltpu.CompilerParams(dimension_semantics=sem, vmem_limit_bytes=VMEM_LIMIT)


def _tile(n, pref, mult=V7X_SUBLANES):
    if n <= pref:
        return n
    for t in range(pref, 0, -1):
        if n % t == 0 and t % mult == 0:
            return t
    return n


_CONTRACT = {"nn": ((1,), (0,)), "nt": ((1,), (1,)), "tn": ((0,), (0,))}


def _mm(name, a, b, *, mode, grid, kaxis, a_blk, a_map, b_blk, b_map, o_shape, o_blk, o_map, o_dtype=F32, scale=None,
        into=None):
    nk = 1 if kaxis is None else grid[kaxis]
    assert kaxis is None or kaxis == len(grid) - 1
    dims = (_CONTRACT[mode], ((), ()))
    use_acc = nk > 1 and o_dtype != F32
    acc_shape = tuple(d for d in o_blk if d is not None)

    def body(a_ref, b_ref, *rest):
        o_ref, scratch = (rest[1], rest[2:]) if into is not None else (rest[0], rest[1:])
        p = lax.dot_general(a_ref[...].astype(BF16), b_ref[...].astype(BF16), dims, preferred_element_type=F32)
        if nk == 1:
            if scale is not None:
                p = p * scale
            o_ref[...] = p.astype(o_dtype)
            return
        acc = scratch[0] if use_acc else o_ref
        k = pl.program_id(kaxis)

        @pl.when(k == 0)
        def _():
            acc[...] = p

        @pl.when(k > 0)
        def _():
            acc[...] += p

        if use_acc or scale is not None:
            @pl.when(k == nk - 1)
            def _():
                r = acc[...]
                if scale is not None:
                    r = r * scale
                o_ref[...] = r.astype(o_dtype)

    in_specs = [pl.BlockSpec(a_blk, a_map), pl.BlockSpec(b_blk, b_map)]
    args = [a, b]
    if into is not None:
        assert into.shape == tuple(o_shape) and into.dtype == o_dtype
        in_specs.append(ANY)
        args.append(into)
    return pl.pallas_call(
        body, grid=grid, name=name, in_specs=in_specs,
        out_specs=pl.BlockSpec(o_blk, o_map),
        out_shape=jax.ShapeDtypeStruct(o_shape, o_dtype),
        input_output_aliases={2: 0} if into is not None else {},
        scratch_shapes=[pltpu.VMEM(acc_shape, F32)] if use_acc else [],
        compiler_params=_cp(len(grid), kaxis),
    )(*args)


def _mm_shards_nn(name, a, wall, layer, o_dtype):
    t, k = a.shape
    _, s, _, n = wall.shape
    tm = _tile(t, 512)
    return _mm(name, a, wall, mode="nn", grid=(s, t // tm), kaxis=None,
               a_blk=(tm, k), a_map=lambda j, i: (i, 0),
               b_blk=(None, None, k, n), b_map=lambda j, i: (layer, j, 0, 0),
               o_shape=(t, s * n), o_blk=(tm, n), o_map=lambda j, i: (i, j), o_dtype=o_dtype)


def _mm_shards_nt(name, g, wall, layer):
    t = g.shape[0]
    _, s, k, n = wall.shape
    tm = _tile(t, 512)
    return _mm(name, g, wall, mode="nt", grid=(t // tm, s), kaxis=1,
               a_blk=(tm, n), a_map=lambda i, kk: (i, kk),
               b_blk=(None, None, k, n), b_map=lambda i, kk: (layer, kk, 0, 0),
               o_shape=(t, k), o_blk=(tm, k), o_map=lambda i, kk: (i, 0))


def _mm_shards_tn(name, a, g, layer, into):
    t, k = a.shape
    _, s, _, n = into.shape
    tk = _tile(t, 512)
    return _mm(name, a, g, mode="tn", grid=(s, t // tk), kaxis=1,
               a_blk=(tk, k), a_map=lambda j, kk: (kk, 0),
               b_blk=(tk, n), b_map=lambda j, kk: (kk, j),
               o_shape=into.shape, o_blk=(None, None, k, n), o_map=lambda j, kk: (layer, j, 0, 0),
               o_dtype=into.dtype, into=into)


def _mm_nn(name, a, w, o_dtype=F32, tn=None):
    t, k = a.shape
    n = w.shape[1]
    tm = _tile(t, 512)
    tn = n if tn is None else tn
    return _mm(name, a, w, mode="nn", grid=(n // tn, t // tm), kaxis=None,
               a_blk=(tm, k), a_map=lambda j, i: (i, 0),
               b_blk=(k, tn), b_map=lambda j, i: (0, j),
               o_shape=(t, n), o_blk=(tm, tn), o_map=lambda j, i: (i, j), o_dtype=o_dtype)


def _mm_nt(name, g, w, layer=None, o_dtype=F32):
    t, k = g.shape
    n = w.shape[-2]
    tm = _tile(t, 512)
    b_blk, b_map = ((n, k), lambda i: (0, 0)) if layer is None else ((None, n, k), lambda i: (layer, 0, 0))
    return _mm(name, g, w, mode="nt", grid=(t // tm,), kaxis=None,
               a_blk=(tm, k), a_map=lambda i: (i, 0), b_blk=b_blk, b_map=b_map,
               o_shape=(t, n), o_blk=(tm, n), o_map=lambda i: (i, 0), o_dtype=o_dtype)


def _mm_tn(name, a, g, tm=None, tn=None, scale=None, layer=None, into=None):
    t, m = a.shape
    n = g.shape[1]
    tk = _tile(t, 512)
    tm = m if tm is None else tm
    tn = n if tn is None else tn
    if layer is None:
        o_shape, o_blk, o_map, o_dtype = (m, n), (tm, tn), lambda i, j, kk: (i, j), F32
    else:
        o_shape, o_blk, o_map, o_dtype = into.shape, (None, tm, tn), lambda i, j, kk: (layer, i, j), into.dtype
    return _mm(name, a, g, mode="tn", grid=(m // tm, n // tn, t // tk), kaxis=2,
               a_blk=(tk, tm), a_map=lambda i, j, kk: (kk, i),
               b_blk=(tk, tn), b_map=lambda i, j, kk: (kk, j),
               o_shape=o_shape, o_blk=o_blk, o_map=o_map, o_dtype=o_dtype, scale=scale, into=into)


def _sigmoid(x):
    return 1.0 / (1.0 + jnp.exp(-x))


def _rows_call(name, body, t, tm, ins, in_cols, outs, acc_outs=()):
    in_specs = []
    for x, c in zip(ins, in_cols):
        if x.shape[0] == 1:
            in_specs.append(pl.BlockSpec((1, c), lambda i: (0, 0)))
        else:
            in_specs.append(pl.BlockSpec((tm, c), lambda i: (i, 0)))
    out_specs = [pl.BlockSpec((tm, s.shape[1]), lambda i: (i, 0)) for s in outs]
    out_specs += [pl.BlockSpec((1, s.shape[1]), lambda i: (0, 0)) for s in acc_outs]
    return pl.pallas_call(
        body, grid=(t // tm,), name=name, in_specs=in_specs, out_specs=out_specs,
        out_shape=list(outs) + list(acc_outs), compiler_params=_cp(1),
    )(*ins)


def _ln_fwd(name, alpha, x, r, coef, g, b):
    t, d = x.shape
    tm = _tile(t, 256)

    def body(x_ref, r_ref, g_ref, b_ref, y_ref, xh_ref, rs_ref):
        z = alpha * x_ref[...] + coef * r_ref[...]
        mu = jnp.mean(z, axis=-1, keepdims=True)
        zc = z - mu
        var = jnp.mean(zc * zc, axis=-1, keepdims=True)
        rstd = lax.rsqrt(var + LN_EPS)
        xh = zc * rstd
        y_ref[...] = xh * g_ref[...] + b_ref[...]
        xh_ref[...] = xh
        rs_ref[...] = rstd

    sd = jax.ShapeDtypeStruct
    return _rows_call(name, body, t, tm, [x, r, g, b], [d, d, d, d],
                      [sd((t, d), F32), sd((t, d), F32), sd((t, 1), F32)])


def _ln_bwd(name, terms, xhat, rstd, g):
    t, d = xhat.shape
    tm = _tile(t, 256)
    n = len(terms)
    coefs = [c for _, c in terms]

    def body(*refs):
        t_refs = refs[:n]
        xh_ref, rs_ref, g_ref, dz_ref, dg_ref, db_ref = refs[n:]
        dy = coefs[0] * t_refs[0][...]
        for c, r in zip(coefs[1:], t_refs[1:]):
            dy = dy + c * r[...]
        xh = xh_ref[...]
        dxh = dy * g_ref[...]
        m1 = jnp.mean(dxh, axis=-1, keepdims=True)
        m2 = jnp.mean(dxh * xh, axis=-1, keepdims=True)
        dz_ref[...] = rs_ref[...] * (dxh - m1 - xh * m2)
        pg = jnp.sum(dy * xh, axis=0, keepdims=True)
        pb = jnp.sum(dy, axis=0, keepdims=True)
        i = pl.program_id(0)

        @pl.when(i == 0)
        def _():
            dg_ref[...] = pg
            db_ref[...] = pb

        @pl.when(i > 0)
        def _():
            dg_ref[...] += pg
            db_ref[...] += pb

    sd = jax.ShapeDtypeStruct
    arrs = [a for a, _ in terms] + [xhat, rstd, g]
    cols = [d] * n + [d, 1, d]
    return _rows_call(name, body, t, tm, arrs, cols, [sd((t, d), F32)], [sd((1, d), F32), sd((1, d), F32)])


def _lincomb(name, terms):
    t, d = terms[0][0].shape
    tm = _tile(t, 256)
    coefs = [c for _, c in terms]
    n = len(terms)

    def body(*refs):
        acc = coefs[0] * refs[0][...]
        for c, r in zip(coefs[1:], refs[1:n]):
            acc = acc + c * r[...]
        refs[n][...] = acc

    return _rows_call(name, body, t, tm, [a for a, _ in terms], [d] * n, [jax.ShapeDtypeStruct((t, d), F32)])[0]


def _loss_sum(name, y, tgt):
    t, d = y.shape
    tm = _tile(t, 256)

    def body(y_ref, t_ref, o_ref):
        e = y_ref[...] - t_ref[...]
        s = jnp.sum(jnp.sum(e * e, axis=1, keepdims=True), axis=0, keepdims=True)
        i = pl.program_id(0)

        @pl.when(i == 0)
        def _():
            o_ref[...] = s

        @pl.when(i > 0)
        def _():
            o_ref[...] += s

    return _rows_call(name, body, t, tm, [y, tgt], [d, d], [], [jax.ShapeDtypeStruct((1, 1), F32)])[0]


def _ffn_in(name, x, wall, layer):
    t, k = x.shape
    n = wall.shape[3]
    tm = _tile(t, 512)

    def body(x_ref, wg_ref, wu_ref, h_ref, a_ref):
        xb = x_ref[...].astype(BF16)
        g = lax.dot_general(xb, wg_ref[...], _NN, preferred_element_type=F32)
        u = lax.dot_general(xb, wu_ref[...], _NN, preferred_element_type=F32)
        h_ref[0] = g.astype(BF16)
        h_ref[1] = u.astype(BF16)
        a_ref[...] = (g * _sigmoid(g) * u).astype(BF16)

    return pl.pallas_call(
        body, grid=(2, t // tm), name=name,
        in_specs=[pl.BlockSpec((tm, k), lambda j, i: (i, 0)),
                  pl.BlockSpec((None, None, k, n), lambda j, i: (layer, j, 0, 0)),
                  pl.BlockSpec((None, None, k, n), lambda j, i: (layer, 2 + j, 0, 0))],
        out_specs=[pl.BlockSpec((2, tm, n), lambda j, i: (0, i, j)), pl.BlockSpec((tm, n), lambda j, i: (i, j))],
        out_shape=[jax.ShapeDtypeStruct((2, t, 2 * n), BF16), jax.ShapeDtypeStruct((t, 2 * n), BF16)],
        compiler_params=_cp(2),
    )(x, wall, wall)


def _ffn_da(name, dz, w3, layer, h):
    t, k = dz.shape
    f = w3.shape[1]
    n = f // 2
    tm = _tile(t, 512)

    def body(dz_ref, w_ref, g_ref, u_ref, dh_ref):
        d = 0.5 * lax.dot_general(dz_ref[...].astype(BF16), w_ref[...], _NT, preferred_element_type=F32)
        g = g_ref[...].astype(F32)
        u = u_ref[...].astype(F32)
        sg = _sigmoid(g)
        dh_ref[0] = (d * u * sg * (1.0 + g * (1.0 - sg))).astype(BF16)
        dh_ref[1] = (d * g * sg).astype(BF16)

    return pl.pallas_call(
        body, grid=(2, t // tm), name=name,
        in_specs=[pl.BlockSpec((tm, k), lambda j, i: (i, 0)),
                  pl.BlockSpec((None, n, k), lambda j, i: (layer, j, 0)),
                  pl.BlockSpec((None, tm, n), lambda j, i: (0, i, j)),
                  pl.BlockSpec((None, tm, n), lambda j, i: (1, i, j))],
        out_specs=pl.BlockSpec((2, tm, n), lambda j, i: (0, i, j)),
        out_shape=jax.ShapeDtypeStruct((2, t, f), BF16),
        compiler_params=_cp(2),
    )(dz, w3, h, h)


def _mm_ln(name, alpha, coef, a, w3, layer, x, g, b):
    t, k = a.shape
    d = w3.shape[2]
    tm = _tile(t, 512)

    def body(a_ref, w_ref, x_ref, g_ref, b_ref, y_ref, xh_ref, rs_ref):
        f = lax.dot_general(a_ref[...].astype(BF16), w_ref[...], _NN, preferred_element_type=F32)
        z = alpha * x_ref[...] + coef * f
        mu = jnp.mean(z, axis=-1, keepdims=True)
        zc = z - mu
        var = jnp.mean(zc * zc, axis=-1, keepdims=True)
        rstd = lax.rsqrt(var + LN_EPS)
        xh = zc * rstd
        y_ref[...] = xh * g_ref[...] + b_ref[...]
        xh_ref[...] = xh
        rs_ref[...] = rstd

    row = lambda c: pl.BlockSpec((tm, c), lambda i: (i, 0))
    vec = pl.BlockSpec((1, d), lambda i: (0, 0))
    sd = jax.ShapeDtypeStruct
    return pl.pallas_call(
        body, grid=(t // tm,), name=name,
        in_specs=[row(k), pl.BlockSpec((None, k, d), lambda i: (layer, 0, 0)), row(d), vec, vec],
        out_specs=[row(d), row(d), row(1)],
        out_shape=[sd((t, d), F32), sd((t, d), F32), sd((t, 1), F32)],
        compiler_params=_cp(1),
    )(a, w3, x, g, b)


_GELU_C = math.sqrt(2.0 / math.pi)


def _s5_act_fwd(name, ych, u, dvec):
    t, d = u.shape
    tm = _tile(t, 256)

    def body(y_ref, u_ref, d_ref, p_ref, a_ref):
        y = y_ref[...] + d_ref[...] * u_ref[...]
        p_ref[...] = y
        a_ref[...] = (0.5 * y * (1.0 + jnp.tanh(_GELU_C * (y + 0.044715 * y * y * y)))).astype(BF16)

    sd = jax.ShapeDtypeStruct
    return _rows_call(name, body, t, tm, [ych, u, dvec], [d, d, d], [sd((t, d), F32), sd((t, d), BF16)])


def _s5_act_bwd(name, dact, ypre, u, dvec):
    t, d = u.shape
    tm = _tile(t, 256)

    def body(da_ref, y_ref, u_ref, d_ref, dy_ref, ds_ref, dd_ref):
        y = y_ref[...]
        th = jnp.tanh(_GELU_C * (y + 0.044715 * y * y * y))
        dg = 0.5 * (1.0 + th) + 0.5 * y * (1.0 - th * th) * _GELU_C * (1.0 + 3.0 * 0.044715 * y * y)
        dy = da_ref[...] * dg
        dy_ref[...] = dy
        ds_ref[...] = dy * d_ref[...]
        pd = jnp.sum(dy * u_ref[...], axis=0, keepdims=True)
        i = pl.program_id(0)

        @pl.when(i == 0)
        def _():
            dd_ref[...] = pd

        @pl.when(i > 0)
        def _():
            dd_ref[...] += pd

    sd = jax.ShapeDtypeStruct
    return _rows_call(name, body, t, tm, [dact, ypre, u, dvec], [d, d, d, d],
                      [sd((t, d), F32), sd((t, d), F32)], [sd((1, d), F32)])


def _glu_fwd(name, vg):
    t, d2 = vg.shape
    d = d2 // 2
    tm = _tile(t, 256)

    def body(vg_ref, m_ref):
        m_ref[...] = vg_ref[:, :d] * _sigmoid(vg_ref[:, d:])

    return _rows_call(name, body, t, tm, [vg], [d2], [jax.ShapeDtypeStruct((t, d), F32)])[0]


def _glu_bwd(name, dm, vg):
    t, d2 = vg.shape
    d = d2 // 2
    tm = _tile(t, 256)

    def body(dm_ref, vg_ref, o_ref):
        sg = _sigmoid(vg_ref[:, d:])
        g = dm_ref[...]
        o_ref[:, :d] = (g * sg).astype(BF16)
        o_ref[:, d:] = (g * vg_ref[:, :d] * sg * (1.0 - sg)).astype(BF16)

    return _rows_call(name, body, t, tm, [dm, vg], [d, d2], [jax.ShapeDtypeStruct((t, d2), BF16)])[0]


def _adamw(name, w, g, m, v):
    r, c = w.shape
    tr = _tile(r, max(V7X_SUBLANES, (1 << 20) // (4 * c) // V7X_SUBLANES * V7X_SUBLANES))

    def body(w_ref, g_ref, m_ref, v_ref, d_ref, nm_ref, nv_ref):
        gg = g_ref[...]
        nm = ADAM_B1 * m_ref[...] + (1.0 - ADAM_B1) * gg
        nv = ADAM_B2 * v_ref[...] + (1.0 - ADAM_B2) * (gg * gg)
        m_hat = nm / (1.0 - ADAM_B1 ** ADAM_STEP)
        v_hat = nv / (1.0 - ADAM_B2 ** ADAM_STEP)
        d_ref[...] = -ADAM_LR * (m_hat / (jnp.sqrt(v_hat) + ADAM_EPS) + ADAM_WD * w_ref[...])
        nm_ref[...] = nm
        nv_ref[...] = nv

    sd = jax.ShapeDtypeStruct((r, c), F32)
    return _rows_call(name, body, r, tr, [w, g, m, v], [c] * 4, [sd, sd, sd])


def _my_shard():
    return 2 * lax.axis_index("x") + lax.axis_index("y")


def _my_core():
    return lax.axis_index("c")


def _adamw_join(name, w, mine, theirs, m, v):
    nl, r, c = w.shape
    h = r // 2
    tr = _tile(h, max(V7X_SUBLANES, (1 << 19) // (4 * c) // V7X_SUBLANES * V7X_SUBLANES))
    nb = h // tr

    def body(w_ref, a_ref, b_ref, m_ref, v_ref, g_ref, d_ref, nm_ref, nv_ref):
        gg = jnp.where(pl.program_id(1) == _my_core(), a_ref[...], b_ref[...])
        nm = ADAM_B1 * m_ref[...] + (1.0 - ADAM_B1) * gg
        nv = ADAM_B2 * v_ref[...] + (1.0 - ADAM_B2) * (gg * gg)
        m_hat = nm / (1.0 - ADAM_B1 ** ADAM_STEP)
        v_hat = nv / (1.0 - ADAM_B2 ** ADAM_STEP)
        g_ref[...] = gg
        d_ref[...] = -ADAM_LR * (m_hat / (jnp.sqrt(v_hat) + ADAM_EPS) + ADAM_WD * w_ref[...])
        nm_ref[...] = nm
        nv_ref[...] = nv

    full = pl.BlockSpec((None, tr, c), lambda l, hf, i: (l, hf * nb + i, 0))
    sd = jax.ShapeDtypeStruct((nl, r, c), F32)
    return pl.pallas_call(
        body, name=name, grid=(nl, 2, nb),
        in_specs=[full,
                  pl.BlockSpec((None, tr, c), lambda l, hf, i: (l, jnp.where(hf == _my_core(), i, 0), 0)),
                  pl.BlockSpec((None, tr, c), lambda l, hf, i: (l, jnp.where(hf == _my_core(), 0, i), 0)),
                  full, full],
        out_specs=[full, full, full, full],
        out_shape=[sd, sd, sd, sd],
        compiler_params=_cp(3),
    )(w, mine, theirs, m, v)


def _sum_leading(name, a):
    n, r, c = a.shape
    tr = _tile(r, 512)

    def body(a_ref, o_ref):
        acc = a_ref[0]
        for k in range(1, n):
            acc = acc + a_ref[k]
        o_ref[...] = acc

    return pl.pallas_call(
        body, grid=(r // tr,), name=name,
        in_specs=[pl.BlockSpec((n, tr, c), lambda i: (0, i, 0))],
        out_specs=pl.BlockSpec((tr, c), lambda i: (i, 0)),
        out_shape=jax.ShapeDtypeStruct((r, c), F32), compiler_params=_cp(1),
    )(a)


def _split3(x):
    hi = x.astype(BF16)
    r1 = x - hi.astype(F32)
    mid = r1.astype(BF16)
    lo = (r1 - mid.astype(F32)).astype(BF16)
    return hi, mid, lo


def _tri_sum(tri, x):
    dims = (((1,), (0,)), ((), ()))
    hi, mid, lo = _split3(x)
    out = lax.dot_general(tri, lo, dims, preferred_element_type=F32)
    out = out + lax.dot_general(tri, mid, dims, preferred_element_type=F32)
    return out + lax.dot_general(tri, hi, dims, preferred_element_type=F32)


def _fox_cumsum(name, fl, bf):
    t, h = fl.shape
    tb = _tile(t, 512)

    def body(fl_ref, bf_ref, c_ref, carry):
        i = pl.program_id(0)

        @pl.when(i == 0)
        def _():
            carry[...] = jnp.zeros_like(carry)

        x = fl_ref[...] + bf_ref[...]
        lf = jnp.minimum(x, 0.0) - jnp.log(1.0 + jnp.exp(-jnp.abs(x)))
        row = lax.broadcasted_iota(jnp.int32, (tb, tb), 0)
        col = lax.broadcasted_iota(jnp.int32, (tb, tb), 1)
        tri = jnp.where(row >= col, 1.0, 0.0).astype(BF16)
        c_ref[...] = _tri_sum(tri, lf) + carry[...]
        carry[...] += jnp.sum(lf, axis=0, keepdims=True)

    return pl.pallas_call(
        body, grid=(t // tb,), name=name,
        in_specs=[pl.BlockSpec((tb, h), lambda i: (i, 0)), pl.BlockSpec((1, h), lambda i: (0, 0))],
        out_specs=pl.BlockSpec((tb, h), lambda i: (i, 0)),
        out_shape=jax.ShapeDtypeStruct((t, h), F32),
        scratch_shapes=[pltpu.VMEM((1, h), F32)], compiler_params=_cp(1),
    )(fl, bf)


def _fox_cumsum_bwd(name, dcum, fl, bf):
    t, h = fl.shape
    tb = _tile(t, 512)
    nb = t // tb

    def body(dc_ref, fl_ref, bf_ref, df_ref, db_ref, carry):
        i = pl.program_id(0)

        @pl.when(i == 0)
        def _():
            carry[...] = jnp.zeros_like(carry)

        dc = dc_ref[...]
        row = lax.broadcasted_iota(jnp.int32, (tb, tb), 0)
        col = lax.broadcasted_iota(jnp.int32, (tb, tb), 1)
        tri = jnp.where(row <= col, 1.0, 0.0).astype(BF16)
        dlf = _tri_sum(tri, dc) + carry[...]
        carry[...] += jnp.sum(dc, axis=0, keepdims=True)
        x = fl_ref[...] + bf_ref[...]
        df = dlf / (1.0 + jnp.exp(x))
        df_ref[...] = df
        pb = jnp.sum(df, axis=0, keepdims=True)

        @pl.when(i == 0)
        def _():
            db_ref[...] = pb

        @pl.when(i > 0)
        def _():
            db_ref[...] += pb

    rev = lambda i: (nb - 1 - i, 0)
    return pl.pallas_call(
        body, grid=(nb,), name=name,
        in_specs=[pl.BlockSpec((tb, h), rev), pl.BlockSpec((tb, h), rev), pl.BlockSpec((1, h), lambda i: (0, 0))],
        out_specs=[pl.BlockSpec((tb, h), rev), pl.BlockSpec((1, h), lambda i: (0, 0))],
        out_shape=[jax.ShapeDtypeStruct((t, h), F32), jax.ShapeDtypeStruct((1, h), F32)],
        scratch_shapes=[pltpu.VMEM((1, h), F32)], compiler_params=_cp(1),
    )(dcum, fl, bf)


_NT = (((1,), (1,)), ((), ()))
_TN = (((0,), (0,)), ((), ()))
_NN = (((1,), (0,)), ((), ()))


def _causal_mask(s, tb):
    row = lax.broadcasted_iota(jnp.int32, (tb, tb), 0)
    col = lax.broadcasted_iota(jnp.int32, (tb, tb), 1)
    return jnp.where(col <= row, s, NEG_INF)


def _attn_fwd(name, q, k, v, ccol, crow):
    nh, nb, tb, hd = q.shape

    def body(q_ref, k_ref, v_ref, cc_ref, cr_ref, o_ref, lse_ref):
        i = pl.program_id(1)
        qi = q_ref[...]
        cc = cc_ref[...]

        def step(j, carry, diagonal=False):
            m, l, acc = carry
            s = lax.dot_general(qi, k_ref[j], _NT, preferred_element_type=F32) + cc - cr_ref[j]
            if diagonal:
                s = _causal_mask(s, tb)
            m_new = jnp.maximum(m, jnp.max(s, axis=1, keepdims=True))
            p = jnp.exp(s - m_new)
            a = jnp.exp(m - m_new)
            l = a * l + jnp.sum(p, axis=1, keepdims=True)
            acc = a * acc + lax.dot_general(p.astype(BF16), v_ref[j], _NN, preferred_element_type=F32)
            return m_new, l, acc

        init = (jnp.full((tb, 1), NEG_INF, F32), jnp.zeros((tb, 1), F32), jnp.zeros((tb, hd), F32))
        m, l, acc = step(i, lax.fori_loop(0, i, step, init), diagonal=True)
        o_ref[...] = acc / l
        lse_ref[...] = m + jnp.log(l)

    blk = lambda *s: pl.BlockSpec(s, lambda h, i: (h, i, 0, 0))
    head = lambda *s: pl.BlockSpec(s, lambda h, i: (h, 0, 0, 0))
    return pl.pallas_call(
        body, grid=(nh, nb), name=name,
        in_specs=[blk(None, None, tb, hd), head(None, nb, tb, hd), head(None, nb, tb, hd),
                  blk(None, None, tb, 1), head(None, nb, 1, tb)],
        out_specs=[blk(None, None, tb, hd), blk(None, None, tb, 1)],
        out_shape=[jax.ShapeDtypeStruct((nh, nb, tb, hd), F32), jax.ShapeDtypeStruct((nh, nb, tb, 1), F32)],
        compiler_params=_cp(2),
    )(q, k, v, ccol, crow)


def _attn_bwd(name, q, k, v, ccol, crow, o, lse, do, scale):
    nh, nb, tb, hd = q.shape

    def body(q_ref, k_ref, v_ref, cc_ref, cr_ref, o_ref, lse_ref, do_ref, dq_ref, dk_ref, dv_ref, dr_ref, dc_ref):
        j = pl.program_id(1)

        @pl.when(j == 0)
        def _():
            dq_ref[...] = jnp.zeros_like(dq_ref)
            dr_ref[...] = jnp.zeros_like(dr_ref)

        kj = k_ref[...]
        vj = v_ref[...]
        cr = cr_ref[...]

        def step(i, carry, diagonal=False):
            dk, dv, dc = carry
            qi = q_ref[i]
            doi = do_ref[i]
            dob = doi.astype(BF16)
            di = jnp.sum(doi * o_ref[i], axis=1, keepdims=True)
            s = lax.dot_general(qi, kj, _NT, preferred_element_type=F32) + cc_ref[i] - cr
            if diagonal:
                s = _causal_mask(s, tb)
            p = jnp.exp(s - lse_ref[i])
            dv = dv + lax.dot_general(p.astype(BF16), dob, _TN, preferred_element_type=F32)
            dp = lax.dot_general(dob, vj, _NT, preferred_element_type=F32)
            ds = p * (dp - di)
            dsb = ds.astype(BF16)
            dk = dk + lax.dot_general(dsb, qi, _TN, preferred_element_type=F32)
            dq_ref[i] += lax.dot_general(dsb, kj, _NN, preferred_element_type=F32) * scale
            dr_ref[i] += jnp.sum(ds, axis=1, keepdims=True)
            dc = dc + jnp.sum(ds, axis=0, keepdims=True)
            return dk, dv, dc

        init = (jnp.zeros((tb, hd), F32), jnp.zeros((tb, hd), F32), jnp.zeros((1, tb), F32))
        dk, dv, dc = lax.fori_loop(j + 1, nb, step, step(j, init, diagonal=True))
        dk_ref[...] = dk
        dv_ref[...] = dv
        dc_ref[...] = dc

    blk = lambda *s: pl.BlockSpec(s, lambda h, j: (h, j, 0, 0))
    head = lambda *s: pl.BlockSpec(s, lambda h, j: (h, 0, 0, 0))
    sd = jax.ShapeDtypeStruct
    return pl.pallas_call(
        body, grid=(nh, nb), name=name,
        in_specs=[head(None, nb, tb, hd), blk(None, None, tb, hd), blk(None, None, tb, hd),
                  head(None, nb, tb, 1), blk(None, None, 1, tb),
                  head(None, nb, tb, hd), head(None, nb, tb, 1), head(None, nb, tb, hd)],
        out_specs=[head(None, nb, tb, hd), blk(None, None, tb, hd), blk(None, None, tb, hd),
                   head(None, nb, tb, 1), blk(None, None, 1, tb)],
        out_shape=[sd((nh, nb, tb, hd), F32), sd((nh, nb, tb, hd), F32), sd((nh, nb, tb, hd), F32),
                   sd((nh, nb, tb, 1), F32), sd((nh, nb, 1, tb), F32)],
        compiler_params=_cp(2),
    )(q, k, v, ccol, crow, o, lse, do)


def _cmul(ar, ai, br, bi):
    return ar * br - ai * bi, ar * bi + ai * br


def _s5_scan(name, lam, xin, hs=None):
    reverse = hs is not None
    _, seg, ns, w = xin.shape
    assert ns == SCAN_SEGMENTS
    wb = min(w, 2 * V7X_LANES)
    nsq = seg.bit_length() - 1
    assert (1 << nsq) == seg

    def body(*refs):
        if reverse:
            lam_ref, x_ref, h_ref, o_ref, dl_ref = refs
        else:
            lam_ref, x_ref, o_ref = refs
        lr = jnp.broadcast_to(lam_ref[0], (ns, wb))
        li = jnp.broadcast_to(lam_ref[1], (ns, wb))
        if reverse:
            li = -li
        zero = jnp.zeros((ns, wb), F32)
        at = (lambda n: seg - 1 - n) if reverse else (lambda n: n)

        def local(n, c):
            r = at(n)
            mr, mi = _cmul(lr, li, c[0], c[1])
            nr = mr + x_ref[0, r]
            ni = mi + x_ref[1, r]
            o_ref[0, r] = nr
            o_ref[1, r] = ni
            return nr, ni

        er, ei = lax.fori_loop(0, seg, local, (zero, zero))
        pr, pi = lr, li
        for _ in range(nsq):
            pr, pi = _cmul(pr, pi, pr, pi)
        sub = lax.broadcasted_iota(jnp.int32, (ns, wb), 0)

        def shifted(a, sh):
            if reverse:
                return jnp.where(sub < ns - sh, pltpu.roll(a, ns - sh, 0), 0.0)
            return jnp.where(sub >= sh, pltpu.roll(a, sh, 0), 0.0)

        xr, xi = er, ei
        sh = 1
        while sh < ns:
            tr, ti = _cmul(pr, pi, shifted(xr, sh), shifted(xi, sh))
            xr, xi = xr + tr, xi + ti
            pr, pi = _cmul(pr, pi, pr, pi)
            sh *= 2
        cr, ci = shifted(xr, 1), shifted(xi, 1)

        def fix(r, q):
            tr, ti = _cmul(q[0], q[1], cr, ci)
            gr = o_ref[0, r] + tr
            gi = o_ref[1, r] + ti
            o_ref[0, r] = gr
            o_ref[1, r] = gi
            return gr, gi

        if not reverse:
            def fixup(n, q):
                fix(n, q)
                return _cmul(q[0], q[1], lr, li)

            lax.fori_loop(0, seg, fixup, (lr, li))
            return

        def fixup_acc(n, c):
            qr, qi, ar, ai = c
            r = seg - 1 - n
            gr, gi = fix(r, (qr, qi))
            hr = h_ref[0, r - 1]
            hi = h_ref[1, r - 1]
            qr, qi = _cmul(qr, qi, lr, li)
            return qr, qi, ar + gr * hr + gi * hi, ai + gi * hr - gr * hi

        qr, qi, ar, ai = lax.fori_loop(0, seg - 1, fixup_acc, (lr, li, zero, zero))
        gr, gi = fix(0, (qr, qi))
        hr = jnp.where(sub >= 1, pltpu.roll(h_ref[0, seg - 1], 1, 0), 0.0)
        hi = jnp.where(sub >= 1, pltpu.roll(h_ref[1, seg - 1], 1, 0), 0.0)
        dl_ref[0] = ar + gr * hr + gi * hi
        dl_ref[1] = ai + gi * hr - gr * hi

    big = pl.BlockSpec((2, seg, ns, wb), lambda j: (0, 0, 0, j))
    lam_spec = pl.BlockSpec((2, 1, wb), lambda j: (0, 0, j))
    sd = jax.ShapeDtypeStruct
    if reverse:
        return pl.pallas_call(
            body, grid=(w // wb,), name=name, in_specs=[lam_spec, big, big],
            out_specs=[big, pl.BlockSpec((2, ns, wb), lambda j: (0, 0, j))],
            out_shape=[sd(xin.shape, F32), sd((2, ns, w), F32)], compiler_params=_cp(1),
        )(lam, xin, hs)
    return pl.pallas_call(
        body, grid=(w // wb,), name=name, in_specs=[lam_spec, big], out_specs=big,
        out_shape=sd(xin.shape, F32), compiler_params=_cp(1),
    )(lam, xin)


def _place():
    x, y, c = lax.axis_index("x"), lax.axis_index("y"), lax.axis_index("c")
    chips = [(1 - x, y), (x, 1 - y), (1 - x, 1 - y)]
    return x, y, c, chips


def _comm_params():
    return pltpu.CompilerParams(vmem_limit_bytes=VMEM_LIMIT)


def _cast_place(name, w):
    nl, r, c = w.shape
    tr = _tile(r, max(16, (1 << 20) // (4 * c) // 16 * 16), 16)

    def body(w_ref, o_ref):
        o_ref[...] = w_ref[...].astype(BF16)

    return pl.pallas_call(
        body, name=name, grid=(nl, r // tr),
        in_specs=[pl.BlockSpec((None, tr, c), lambda l, i: (l, i, 0))],
        out_specs=pl.BlockSpec((None, None, tr, c), lambda l, i: (l, _my_shard(), i, 0)),
        out_shape=jax.ShapeDtypeStruct((nl, N_CHIPS, r, c), BF16),
        compiler_params=_cp(2),
    )(w)


def _gather_shards(name, bufs):
    n = len(bufs)

    def body(*refs):
        outs = refs[n:2 * n]
        send_sems, recv_sems = refs[2 * n:]
        x, y, c, chips = _place()
        my = 2 * x + y
        sibling = (x, y, 1 - c)

        def part(t, shard, half):
            h = bufs[t].shape[2] // 2
            return outs[t].at[:, shard, pl.ds(half * h, h)]

        def copy(t, k, ref, to):
            return pltpu.make_async_remote_copy(src_ref=ref, dst_ref=ref, send_sem=send_sems.at[t, k],
                                                recv_sem=recv_sems.at[t, k], device_id=to, device_id_type=MESH)

        sent = []
        for t in range(n):
            for k, chip in enumerate(chips):
                sent.append(copy(t, k, part(t, my, c), (*chip, c)))
                sent[-1].start()
        for k, chip in enumerate(chips):
            shard = 2 * chip[0] + chip[1]
            for t in range(n):
                copy(t, k, part(t, shard, c), (*chip, c)).wait_recv()
                sent.append(copy(t, 3 + k, part(t, shard, c), sibling))
                sent[-1].start()
        for k, chip in enumerate(chips):
            shard = 2 * chip[0] + chip[1]
            for t in range(n):
                copy(t, 3 + k, part(t, shard, 1 - c), sibling).wait_recv()
        for cp in sent:
            cp.wait_send()

    return pl.pallas_call(
        body, name=name, in_specs=[ANY] * n, out_specs=[ANY] * n,
        out_shape=[jax.ShapeDtypeStruct(b.shape, b.dtype) for b in bufs],
        input_output_aliases={t: t for t in range(n)},
        scratch_shapes=[pltpu.SemaphoreType.DMA((n, 6)), pltpu.SemaphoreType.DMA((n, 6))],
        compiler_params=_comm_params(),
    )(*bufs)


HBM_SPEC = pl.BlockSpec(memory_space=pltpu.HBM)
SEM_SPEC = pl.BlockSpec(memory_space=pltpu.SEMAPHORE)


def _split_params():
    return pltpu.CompilerParams(has_side_effects=pltpu.SideEffectType.DATAFLOW_SIDE_EFFECTING)


def _gather_start(name, bufs, groups):
    n, ng = len(bufs), len(groups)

    def body(*refs):
        sems = refs[n:n + 2 * ng]
        outs = refs[n + 2 * ng:]
        x, y, c, chips = _place()
        my = 2 * x + y
        for gi, group in enumerate(groups):
            for idx, (t, layer) in enumerate(group):
                block = outs[t].at[layer, my]
                for k, chip in enumerate(chips):
                    pltpu.make_async_remote_copy(
                        src_ref=block, dst_ref=block, send_sem=sems[2 * gi].at[3 * idx + k],
                        recv_sem=sems[2 * gi + 1].at[3 * idx + k], device_id=(*chip, c), device_id_type=MESH).start()

    sem_shapes = []
    for group in groups:
        sem_shapes += [pltpu.SemaphoreType.DMA((3 * len(group),))] * 2
    res = pl.pallas_call(
        body, name=name, in_specs=[HBM_SPEC] * n,
        out_specs=[SEM_SPEC] * (2 * ng) + [HBM_SPEC] * n,
        out_shape=sem_shapes + [pltpu.HBM(b.shape, b.dtype) for b in bufs],
        input_output_aliases={t: 2 * ng + t for t in range(n)},
        compiler_params=_split_params(),
    )(*[pltpu.with_memory_space_constraint(b, pltpu.HBM) for b in bufs])
    sems = [(res[2 * gi], res[2 * gi + 1]) for gi in range(ng)]
    return sems, list(res[2 * ng:])


def _gather_wait(name, bufs, send_sems, recv_sems, after, group):
    n = len(bufs)

    def body(*refs):
        ss, rs = refs[n], refs[n + 1]
        outs = refs[n + 3:]
        x, y, c, chips = _place()
        my = 2 * x + y
        for idx, (t, layer) in enumerate(group):
            for k, chip in enumerate(chips):
                cp = pltpu.make_async_remote_copy(
                    src_ref=outs[t].at[layer, my], dst_ref=outs[t].at[layer, 2 * chip[0] + chip[1]],
                    send_sem=ss.at[3 * idx + k], recv_sem=rs.at[3 * idx + k], device_id=(*chip, c), device_id_type=MESH)
                cp.wait_send()
                cp.wait_recv()

    return list(pl.pallas_call(
        body, name=name, in_specs=[HBM_SPEC] * n + [SEM_SPEC, SEM_SPEC, ANY],
        out_specs=[HBM_SPEC] * n,
        out_shape=[pltpu.HBM(b.shape, b.dtype) for b in bufs],
        input_output_aliases={t: t for t in range(n)},
        compiler_params=_split_params(),
    )(*bufs, send_sems, recv_sems, after))


N_PARTS = 7


def _scatter_items(send, rx, items, c, chips, x, y):
    my = 2 * x + y
    out = []
    for i, (k, layer) in enumerate(items):
        h = send[k].shape[2] // 2
        for kk, chip in enumerate(chips):
            shard = 2 * chip[0] + chip[1]
            for hf in (0, 1):
                out.append((send[k].at[layer, shard, pl.ds(hf * h, h)], rx[k].at[2 * kk + c, layer],
                            N_PARTS * i + 2 * kk + hf, N_PARTS * i + 2 * kk + c, (*chip, hf)))
        out.append((send[k].at[layer, my, pl.ds((1 - c) * h, h)], rx[k].at[N_PARTS - 1, layer],
                    N_PARTS * i + N_PARTS - 1, N_PARTS * i + N_PARTS - 1, (x, y, 1 - c)))
    return out


def _scatter_start(name, send, rx, items):
    n = len(send)
    m = N_PARTS * len(items)

    def body(*refs):
        ssem, rsem = refs[2 * n], refs[2 * n + 1]
        s_out, r_out = refs[2 * n + 2:3 * n + 2], refs[3 * n + 2:]
        x, y, c, chips = _place()
        for src, dst, si, ri, to in _scatter_items(s_out, r_out, items, c, chips, x, y):
            pltpu.make_async_remote_copy(src_ref=src, dst_ref=dst, send_sem=ssem.at[si], recv_sem=rsem.at[ri],
                                         device_id=to, device_id_type=MESH).start()

    res = pl.pallas_call(
        body, name=name, in_specs=[HBM_SPEC] * (2 * n),
        out_specs=[SEM_SPEC, SEM_SPEC] + [HBM_SPEC] * (2 * n),
        out_shape=[pltpu.SemaphoreType.DMA((m,)), pltpu.SemaphoreType.DMA((m,))]
        + [pltpu.HBM(b.shape, b.dtype) for b in list(send) + list(rx)],
        input_output_aliases={t: 2 + t for t in range(2 * n)},
        compiler_params=_split_params(),
    )(*[pltpu.with_memory_space_constraint(b, pltpu.HBM) for b in list(send) + list(rx)])
    return (res[0], res[1]), list(res[2:2 + n]), list(res[2 + n:])


def _scatter_wait(name, send, rx, ssem, rsem, items):
    n = len(send)

    def body(*refs):
        ss, rs = refs[2 * n], refs[2 * n + 1]
        s_out, r_out = refs[2 * n + 2:3 * n + 2], refs[3 * n + 2:]
        x, y, c, chips = _place()
        for i, (src, dst, si, ri, to) in enumerate(_scatter_items(s_out, r_out, items, c, chips, x, y)):
            arrival = i % N_PARTS
            landed = r_out[items[i // N_PARTS][0]].at[arrival, items[i // N_PARTS][1]]
            cp = pltpu.make_async_remote_copy(src_ref=src, dst_ref=landed, send_sem=ss.at[si],
                                              recv_sem=rs.at[N_PARTS * (i // N_PARTS) + arrival],
                                              device_id=to, device_id_type=MESH)
            cp.wait_send()
            cp.wait_recv()

    res = pl.pallas_call(
        body, name=name, in_specs=[HBM_SPEC] * (2 * n) + [SEM_SPEC, SEM_SPEC],
        out_specs=[HBM_SPEC] * (2 * n),
        out_shape=[pltpu.HBM(b.shape, b.dtype) for b in list(send) + list(rx)],
        input_output_aliases={t: t for t in range(2 * n)},
        compiler_params=_split_params(),
    )(*send, *rx, ssem, rsem)
    return list(res[:n]), list(res[n:])


def _chip_sum(name, g, rx):
    nl, _, r, c = g.shape
    h = r // 2
    tr = _tile(h, max(V7X_SUBLANES * 2, (1 << 19) // (2 * c) // 16 * 16), 16)
    nb = h // tr

    def body(g_ref, r_ref, o_ref):
        acc = g_ref[...].astype(F32)
        for k in range(N_PARTS):
            acc = acc + r_ref[k].astype(F32)
        o_ref[...] = acc

    return pl.pallas_call(
        body, name=name, grid=(nl, nb),
        in_specs=[pl.BlockSpec((None, None, tr, c), lambda l, i: (l, _my_shard(), _my_core() * nb + i, 0)),
                  pl.BlockSpec((N_PARTS, None, tr, c), lambda l, i: (0, l, i, 0))],
        out_specs=pl.BlockSpec((None, tr, c), lambda l, i: (l, i, 0)),
        out_shape=jax.ShapeDtypeStruct((nl, h, c), F32),
        compiler_params=_cp(2),
    )(g, rx)


def _send_half(name, fs):
    n = len(fs)

    def body(*refs):
        ins, outs = refs[:n], refs[n:2 * n]
        send_sems, recv_sems = refs[2 * n:]
        x, y, c, _ = _place()
        cps = []
        for t in range(n):
            cps.append(pltpu.make_async_remote_copy(
                src_ref=ins[t], dst_ref=outs[t], send_sem=send_sems.at[t], recv_sem=recv_sems.at[t],
                device_id=(x, y, 1 - c), device_id_type=MESH))
            cps[-1].start()
        for cp in cps:
            cp.wait()

    return pl.pallas_call(
        body, name=name, in_specs=[ANY] * n, out_specs=[ANY] * n,
        out_shape=[jax.ShapeDtypeStruct(f.shape, f.dtype) for f in fs],
        scratch_shapes=[pltpu.SemaphoreType.DMA((n,)), pltpu.SemaphoreType.DMA((n,))],
        compiler_params=_comm_params(),
    )(*fs)


def _gather_all(name, v):
    m_per = v.shape[0]

    def body(x_ref, out_ref, send_sems, recv_sems, local_sem):
        x, y, c, chips = _place()
        me, sibling = (x, y, c), (x, y, 1 - c)

        def rows(px, py, pc):
            return out_ref.at[pl.ds((4 * px + 2 * py + pc) * m_per, m_per), :]

        def copy(k, block, to, src=None):
            return pltpu.make_async_remote_copy(
                src_ref=rows(*block) if src is None else src, dst_ref=rows(*block), send_sem=send_sems.at[k],
                recv_sem=recv_sems.at[k], device_id=to, device_id_type=MESH)

        mine = pltpu.make_async_copy(x_ref, rows(*me), local_sem)
        mine.start()
        first = [copy(0, me, sibling, src=x_ref)]
        first += [copy(1 + j, me, (*chip, c), src=x_ref) for j, chip in enumerate(chips)]
        for cp in first:
            cp.start()
        passed = [copy(4 + j, (*chip, c), sibling) for j, chip in enumerate(chips)]
        for j, chip in enumerate(chips):
            copy(1 + j, (*chip, c), me).wait_recv()
            passed[j].start()
        copy(0, sibling, me).wait_recv()
        for j, chip in enumerate(chips):
            copy(4 + j, (*chip, 1 - c), me).wait_recv()
        for cp in first + passed:
            cp.wait_send()
        mine.wait()

    return pl.pallas_call(
        body, name=name, in_specs=[ANY], out_specs=ANY,
        out_shape=jax.ShapeDtypeStruct((N_DEV * m_per, v.shape[1]), v.dtype),
        scratch_shapes=[pltpu.SemaphoreType.DMA((7,)), pltpu.SemaphoreType.DMA((7,)), pltpu.SemaphoreType.DMA],
        compiler_params=_comm_params(),
    )(v)


def _rows_view(wall):
    nl, s, r, c = wall.shape
    return wall.reshape(nl, s * r, c)


def _ffn_fwd(tag, alpha, x, w_in, w_out3, layer, g, b):
    h, a = _ffn_in(f"{tag}_in", x, w_in, layer)
    y, xhat, rstd = _mm_ln(f"{tag}_out", alpha, 0.5, a, w_out3, layer, x, g, b)
    return y, (x, h, a, xhat, rstd)


def _ffn_bwd(tag, alpha, terms, saved, w_in, w_out3, layer, g, g_win, g_wout3):
    x, h, a, xhat, rstd = saved
    t = x.shape[0]
    _, s, k, n = w_in.shape
    tm = _tile(t, 512)
    dz, dg, db = _ln_bwd(f"{tag}_ln_bwd", terms, xhat, rstd, g)
    g_wout3 = _mm_tn(f"{tag}_dwout", a, dz, tm=n, scale=0.5, layer=layer, into=g_wout3)
    dh = _ffn_da(f"{tag}_da", dz, w_out3, layer, h)
    g_win = _mm(f"{tag}_dwin", x, dh, mode="tn", grid=(s, t // tm), kaxis=1,
                a_blk=(tm, k), a_map=lambda j, kk: (kk, 0),
                b_blk=(None, tm, n), b_map=lambda j, kk: (j // 2, kk, j % 2),
                o_shape=w_in.shape, o_blk=(None, None, k, n), o_map=lambda j, kk: (layer, j, 0, 0),
                o_dtype=g_win.dtype, into=g_win)
    dx = _mm(f"{tag}_dx", dh, w_in, mode="nt", grid=(t // tm, s), kaxis=1,
             a_blk=(None, tm, n), a_map=lambda i, kk: (kk // 2, i, kk % 2),
             b_blk=(None, None, k, n), b_map=lambda i, kk: (layer, kk, 0, 0),
             o_shape=(t, k), o_blk=(tm, k), o_map=lambda i, kk: (i, 0))
    return [(dz, alpha), (dx, 1.0)], g_win, g_wout3, dg, db


def _heads(a, nh, tb):
    t = a.shape[0]
    return a.reshape(t, nh, -1).transpose(1, 0, 2).reshape(nh, t // tb, tb, -1)


def _unheads(a):
    nh, nb, tb, hd = a.shape
    return a.reshape(nh, nb * tb, hd).transpose(1, 0, 2).reshape(nb * tb, nh * hd)


def _fox_fwd(tag, alpha, x, w_pad, bf, w_o3, layer, g, b):
    t, d = x.shape
    nh = bf.shape[1]
    hd = d // nh
    tb = _tile(t, 512)
    scale = 1.0 / math.sqrt(hd)
    proj = _mm_nn(f"{tag}_proj", x, w_pad, tn=_tile(w_pad.shape[1], 640, V7X_LANES))
    q = _heads((proj[:, :d].astype(BF16) * scale).astype(BF16), nh, tb)
    k = _heads(proj[:, d:2 * d].astype(BF16), nh, tb)
    v = _heads(proj[:, 2 * d:3 * d].astype(BF16), nh, tb)
    fl = proj[:, 3 * d:3 * d + nh]
    cum = _fox_cumsum(f"{tag}_cum", fl, bf)
    ccol = cum.T.reshape(nh, t // tb, tb, 1)
    crow = cum.T.reshape(nh, t // tb, 1, tb)
    o, lse = _attn_fwd(f"{tag}_attn", q, k, v, ccol, crow)
    o2 = _unheads(o)
    y, xhat, rstd = _mm_ln(f"{tag}_oproj", alpha, 1.0, o2, w_o3, layer, x, g, b)
    return y, xhat, rstd, (x, q, k, v, ccol, crow, o, lse, o2, fl, scale)


def _fox_bwd(tag, dm, saved, w_pad, bf, w_o3, layer, g_wo3):
    x, q, k, v, ccol, crow, o, lse, o2, fl, scale = saved
    t, d = x.shape
    nh, nb, tb, hd = q.shape
    g_wo3 = _mm_tn(f"{tag}_dwo", o2, dm, layer=layer, into=g_wo3)
    do2 = _mm_nt(f"{tag}_do", dm, w_o3, layer=layer)
    do = _heads(do2, nh, tb)
    dq, dk, dv, drow, dcol = _attn_bwd(f"{tag}_attn_bwd", q, k, v, ccol, crow, o, lse, do, scale)
    dcum = (drow.reshape(nh, t) - dcol.reshape(nh, t)).T
    dfl, dbf = _fox_cumsum_bwd(f"{tag}_cum_bwd", dcum, fl, bf)
    pad = w_pad.shape[1] - 3 * d - nh
    dproj = jnp.concatenate([_unheads(dq).astype(BF16), _unheads(dk).astype(BF16), _unheads(dv).astype(BF16),
                             dfl.astype(BF16), jnp.zeros((t, pad), BF16)], axis=1)
    d_wpad = _mm_tn(f"{tag}_dwin", x, dproj, tn=_tile(w_pad.shape[1], 640, V7X_LANES))
    dx = _mm_nt(f"{tag}_dx", dproj, w_pad)
    return dx, d_wpad, dbf, g_wo3


def _to_segments(a):
    t, d = a.shape
    return a.reshape(SCAN_SEGMENTS, t // SCAN_SEGMENTS, d).transpose(1, 0, 2).reshape(t, d)


def _from_segments(a):
    t, d = a.shape
    return a.reshape(t // SCAN_SEGMENTS, SCAN_SEGMENTS, d).transpose(1, 0, 2).reshape(t, d)


def _s5_discretise(a_re, a_im, log_dt, b_re, b_im):
    dt = jnp.exp(log_dt)[:, None]
    mag = jnp.exp(a_re * dt)
    ang = a_im * dt
    lb_re = mag * jnp.cos(ang)
    lb_im = mag * jnp.sin(ang)
    den = a_re * a_re + a_im * a_im
    nr = lb_re - 1.0
    ni = lb_im
    z_re = (nr * a_re + ni * a_im) / den
    z_im = (ni * a_re - nr * a_im) / den
    bb_re = z_re[..., None] * b_re - z_im[..., None] * b_im
    bb_im = z_re[..., None] * b_im + z_im[..., None] * b_re
    return lb_re, lb_im, bb_re, bb_im


S5_BLOCK_GROUPS = 8


def _blockdiag_in(bb):
    g, p, h = bb.shape
    e = jnp.eye(S5_BLOCK_GROUPS, dtype=bb.dtype)
    b4 = bb.reshape(g // S5_BLOCK_GROUPS, S5_BLOCK_GROUPS, p, h)
    return jnp.einsum("jgph,gf->jghfp", b4, e).reshape(g // S5_BLOCK_GROUPS, S5_BLOCK_GROUPS * h, S5_BLOCK_GROUPS * p)


def _blockdiag_in_grad(d):
    nj, gh, gp = d.shape
    h, p = gh // S5_BLOCK_GROUPS, gp // S5_BLOCK_GROUPS
    e = jnp.eye(S5_BLOCK_GROUPS, dtype=d.dtype)
    d6 = d.reshape(nj, S5_BLOCK_GROUPS, h, S5_BLOCK_GROUPS, p)
    return jnp.einsum("jghfp,gf->jgph", d6, e).reshape(nj * S5_BLOCK_GROUPS, p, h)


def _blockdiag_out(cc):
    g, h, p = cc.shape
    e = jnp.eye(S5_BLOCK_GROUPS, dtype=cc.dtype)
    c4 = cc.reshape(g // S5_BLOCK_GROUPS, S5_BLOCK_GROUPS, h, p)
    return jnp.einsum("jghp,gf->jfpgh", c4, e).reshape(g // S5_BLOCK_GROUPS, S5_BLOCK_GROUPS * p, S5_BLOCK_GROUPS * h)


def _blockdiag_out_grad(d):
    nj, gp, gh = d.shape
    h, p = gh // S5_BLOCK_GROUPS, gp // S5_BLOCK_GROUPS
    e = jnp.eye(S5_BLOCK_GROUPS, dtype=d.dtype)
    d6 = d.reshape(nj, S5_BLOCK_GROUPS, p, S5_BLOCK_GROUPS, h)
    return jnp.einsum("jfpgh,gf->jghp", d6, e).reshape(nj * S5_BLOCK_GROUPS, h, p)


def _s5_fwd(tag, x, prm, w_out, layer):
    a_re, a_im, log_dt, b_re, b_im, c_re, c_im, d_skip = prm
    t, d = x.shape
    g, p = a_re.shape
    w = g * p
    nj = g // S5_BLOCK_GROUPS
    cw, sw = S5_BLOCK_GROUPS * S5_GROUP, S5_BLOCK_GROUPS * p
    seg = t // SCAN_SEGMENTS
    tm = _tile(t, 4096)
    lb_re, lb_im, bb_re, bb_im = _s5_discretise(a_re, a_im, log_dt, b_re, b_im)
    lam = jnp.stack([lb_re.reshape(1, w), lb_im.reshape(1, w)])
    bs = jnp.stack([_blockdiag_in(bb_re), _blockdiag_in(bb_im)]).astype(BF16)
    cs = jnp.stack([_blockdiag_out(c_re), -_blockdiag_out(c_im)]).astype(BF16)
    dvec = d_skip.reshape(1, d)
    u = _to_segments(x)
    bu = _mm(f"{tag}_bu", u, bs, mode="nn", grid=(2, nj, t // tm), kaxis=None,
             a_blk=(tm, cw), a_map=lambda r, j, i: (i, j),
             b_blk=(None, None, cw, sw), b_map=lambda r, j, i: (r, j, 0, 0),
             o_shape=(2, t, w), o_blk=(None, tm, sw), o_map=lambda r, j, i: (r, i, j))
    hs = _s5_scan(f"{tag}_scan", lam, bu.reshape(2, seg, SCAN_SEGMENTS, w)).reshape(2, t, w)
    ych = _mm(f"{tag}_ch", hs, cs, mode="nn", grid=(nj, t // tm, 2), kaxis=2,
              a_blk=(None, tm, sw), a_map=lambda j, i, r: (r, i, j),
              b_blk=(None, None, sw, cw), b_map=lambda j, i, r: (r, j, 0, 0),
              o_shape=(t, d), o_blk=(tm, cw), o_map=lambda j, i, r: (i, j))
    ypre, act = _s5_act_fwd(f"{tag}_act", ych, u, dvec)
    vg = _mm_shards_nn(f"{tag}_wout", act, w_out, layer, F32)
    m = _from_segments(_glu_fwd(f"{tag}_glu", vg))
    return m, (u, lam, bs, cs, dvec, hs, ypre, act, vg)


def _s5_bwd(tag, dm, saved, prm, w_out, layer, g_wout):
    a_re, a_im, log_dt, b_re, b_im, c_re, c_im, d_skip = prm
    u, lam, bs, cs, dvec, hs, ypre, act, vg = saved
    t, d = u.shape
    g, p = a_re.shape
    w = g * p
    nj = g // S5_BLOCK_GROUPS
    cw, sw = S5_BLOCK_GROUPS * S5_GROUP, S5_BLOCK_GROUPS * p
    seg = t // SCAN_SEGMENTS
    tm = _tile(t, 4096)
    dvg = _glu_bwd(f"{tag}_glu_bwd", _to_segments(dm), vg)
    g_wout = _mm_shards_tn(f"{tag}_dwout", act, dvg, layer, g_wout)
    dact = _mm_shards_nt(f"{tag}_dact", dvg, w_out, layer)
    dypre, duskip, dd = _s5_act_bwd(f"{tag}_act_bwd", dact, ypre, u, dvec)
    dh = _mm(f"{tag}_dh", dypre, cs, mode="nt", grid=(2, nj, t // tm), kaxis=None,
             a_blk=(tm, cw), a_map=lambda r, j, i: (i, j),
             b_blk=(None, None, sw, cw), b_map=lambda r, j, i: (r, j, 0, 0),
             o_shape=(2, t, w), o_blk=(None, tm, sw), o_map=lambda r, j, i: (r, i, j))
    dcs = _mm(f"{tag}_dc", hs, dypre, mode="tn", grid=(2, nj, t // tm), kaxis=2,
              a_blk=(None, tm, sw), a_map=lambda r, j, i: (r, i, j),
              b_blk=(tm, cw), b_map=lambda r, j, i: (i, j),
              o_shape=(2, nj, sw, cw), o_blk=(None, None, sw, cw), o_map=lambda r, j, i: (r, j, 0, 0))
    gs, dlam8 = _s5_scan(f"{tag}_scan_bwd", lam, dh.reshape(2, seg, SCAN_SEGMENTS, w),
                         hs.reshape(2, seg, SCAN_SEGMENTS, w))
    gs = gs.reshape(2, t, w)
    du = _mm(f"{tag}_du", gs, bs, mode="nt", grid=(nj, t // tm, 2), kaxis=2,
             a_blk=(None, tm, sw), a_map=lambda j, i, r: (r, i, j),
             b_blk=(None, None, cw, sw), b_map=lambda j, i, r: (r, j, 0, 0),
             o_shape=(t, d), o_blk=(tm, cw), o_map=lambda j, i, r: (i, j))
    dbs = _mm(f"{tag}_db", u, gs, mode="tn", grid=(2, nj, t // tm), kaxis=2,
              a_blk=(tm, cw), a_map=lambda r, j, i: (i, j),
              b_blk=(None, tm, sw), b_map=lambda r, j, i: (r, i, j),
              o_shape=(2, nj, cw, sw), o_blk=(None, None, cw, sw), o_map=lambda r, j, i: (r, j, 0, 0))
    dx = _from_segments(du + duskip)
    dlam = jnp.sum(dlam8, axis=1).reshape(2, g, p)
    small = dict(dlb_re=dlam[0], dlb_im=dlam[1],
                 dbb_re=_blockdiag_in_grad(dbs[0]), dbb_im=_blockdiag_in_grad(dbs[1]),
                 dc_re=_blockdiag_out_grad(dcs[0]), dc_im=-_blockdiag_out_grad(dcs[1]),
                 dd=dd.reshape(g, S5_GROUP))
    return dx, g_wout, small


def _pack(pieces):
    flat = jnp.concatenate([p.reshape(-1).astype(F32) for p in pieces])
    n = flat.shape[0]
    unit = V7X_SUBLANES * V7X_LANES
    total = -(-n // unit) * unit
    return jnp.pad(flat, (0, total - n)).reshape(total // V7X_LANES, V7X_LANES)


def _unpack(buf, shapes):
    flat = buf.reshape(-1)
    out, off = [], 0
    for s in shapes:
        n = math.prod(s)
        out.append(flat[off:off + n].reshape(s))
        off += n
    return out


def kernel(x, ffn1_w_in, ffn1_w_out, ln1_g, ln1_b, lnm_g, lnm_b, ffn2_w_in, ffn2_w_out, ln2_g, ln2_b, fox_w_in, fox_b_f, fox_w_o, s5_a_re, s5_a_im, s5_log_dt, s5_b_re, s5_b_im, s5_c_re, s5_c_im, s5_d, s5_w_out, loss_target, m_ffn1_w_in, m_ffn1_w_out, m_ln1_g, m_ln1_b, m_lnm_g, m_lnm_b, m_ffn2_w_in, m_ffn2_w_out, m_ln2_g, m_ln2_b, m_fox_w_in, m_fox_b_f, m_fox_w_o, m_s5_a_re, m_s5_a_im, m_s5_log_dt, m_s5_b_re, m_s5_b_im, m_s5_c_re, m_s5_c_im, m_s5_d, m_s5_w_out, v_ffn1_w_in, v_ffn1_w_out, v_ln1_g, v_ln1_b, v_lnm_g, v_lnm_b, v_ffn2_w_in, v_ffn2_w_out, v_ln2_g, v_ln2_b, v_fox_w_in, v_fox_b_f, v_fox_w_o, v_s5_a_re, v_s5_a_im, v_s5_log_dt, v_s5_b_re, v_s5_b_im, v_s5_c_re, v_s5_c_im, v_s5_d, v_s5_w_out):
    big_names = ["ffn1_w_in", "ffn1_w_out", "ffn2_w_in", "ffn2_w_out", "fox_w_in", "fox_w_o", "s5_w_out"]
    small_names = ["ln1_g", "ln1_b", "lnm_g", "lnm_b", "ln2_g", "ln2_b", "fox_b_f", "s5_a_re", "s5_a_im", "s5_log_dt",
                   "s5_b_re", "s5_b_im", "s5_c_re", "s5_c_im", "s5_d"]
    out_order = ["ffn1_w_in", "ffn1_w_out", "ln1_g", "ln1_b", "lnm_g", "lnm_b", "ffn2_w_in", "ffn2_w_out", "ln2_g",
                 "ln2_b", "fox_w_in", "fox_b_f", "fox_w_o", "s5_a_re", "s5_a_im", "s5_log_dt", "s5_b_re", "s5_b_im",
                 "s5_c_re", "s5_c_im", "s5_d", "s5_w_out"]
    env = dict(locals())
    w = {n: env[n] for n in out_order}
    mom = {n: env["m_" + n] for n in out_order}
    vel = {n: env["v_" + n] for n in out_order}

    depth, d = ln1_g.shape
    t = x.shape[1]
    alpha = (2.0 * depth) ** 0.25
    x0 = x.reshape(t, d)
    tgt = loss_target.reshape(t, d)

    tix = {n: k for k, n in enumerate(big_names)}
    groups = []
    for i in range(depth):
        j = i // 2
        groups.append([(tix["ffn1_w_in"], i), (tix["ffn1_w_out"], i)])
        mixer = [(tix["fox_w_in"], j), (tix["fox_w_o"], j)] if i % 2 == 0 else [(tix["s5_w_out"], j)]
        groups.append(mixer + [(tix["ffn2_w_in"], i), (tix["ffn2_w_out"], i)])
    sems, bufs = _gather_start("gather_start", [_cast_place(f"cast_{n}", w[n]) for n in big_names], groups)
    full, rows3 = {}, {}

    def arrive(gi, after):
        nonlocal bufs
        bufs = _gather_wait(f"gather_wait_{gi}", bufs, sems[gi][0], sems[gi][1], after, groups[gi])
        full.update(zip(big_names, bufs))
        rows3.update({n: _rows_view(full[n]) for n in ("ffn1_w_out", "ffn2_w_out", "fox_w_o")})

    nh = fox_b_f.shape[1]
    fox_cols = 3 * d + nh
    fox_pad = -(-fox_cols // (5 * V7X_LANES)) * (5 * V7X_LANES)

    def fox_wpad(j):
        wf = full["fox_w_in"][j].transpose(1, 0, 2).reshape(d, fox_cols)
        return jnp.pad(wf, ((0, 0), (0, fox_pad - fox_cols)))

    def s5_params(j):
        return (s5_a_re[j], s5_a_im[j], s5_log_dt[j], s5_b_re[j], s5_b_im[j], s5_c_re[j], s5_c_im[j], s5_d[j])

    saved = []
    h = x0
    for i in range(depth):
        j = i // 2
        arrive(2 * i, h)
        h, s1 = _ffn_fwd(f"l{i}_ffn1", alpha, h, full["ffn1_w_in"], rows3["ffn1_w_out"], i,
                         ln1_g[i:i + 1], ln1_b[i:i + 1])
        arrive(2 * i + 1, h)
        if i % 2 == 0:
            h, xhat_m, rstd_m, sm = _fox_fwd(f"l{i}_fox", alpha, h, fox_wpad(j), fox_b_f[j:j + 1], rows3["fox_w_o"], j,
                                             lnm_g[i:i + 1], lnm_b[i:i + 1])
        else:
            m, sm = _s5_fwd(f"l{i}_s5", h, s5_params(j), full["s5_w_out"], j)
            h, xhat_m, rstd_m = _ln_fwd(f"l{i}_lnm", alpha, h, m, 1.0, lnm_g[i:i + 1], lnm_b[i:i + 1])
        h, s2 = _ffn_fwd(f"l{i}_ffn2", alpha, h, full["ffn2_w_in"], rows3["ffn2_w_out"], i,
                         ln2_g[i:i + 1], ln2_b[i:i + 1])
        saved.append((s1, sm, (xhat_m, rstd_m), s2))
    loss_part = _loss_sum("loss", h, tgt) * (0.5 / d)

    fox_in_names = [f"fox_w_in_l{j}" for j in range(fox_w_in.shape[0])]
    gshape = {n: full[n].shape for n in big_names if n != "fox_w_in"}
    gshape.update({n: (1,) + full["fox_w_in"].shape[1:] for n in fox_in_names})
    gbuf = {n: lax.empty(s, BF16) for n, s in gshape.items()}
    rxbuf = {n: lax.empty((N_PARTS, s[0], s[2] // 2, s[3]), BF16) for n, s in gshape.items()}
    pending = []

    def scatter(tag, pairs):
        names = list(dict.fromkeys(n for n, _ in pairs))
        items = [(names.index(n), layer) for n, layer in pairs]
        sem, send, rx = _scatter_start(f"scatter_start_{tag}", [gbuf[n] for n in names], [rxbuf[n] for n in names], items)
        gbuf.update(zip(names, send))
        rxbuf.update(zip(names, rx))
        pending.append((tag, names, items, sem))

    gsmall = {n: [None] * w[n].shape[0] for n in small_names}
    s5_cot = [None] * s5_a_re.shape[0]
    terms = [(h, 1.0 / d), (tgt, -1.0 / d)]
    for i in reversed(range(depth)):
        j = i // 2
        s1, sm, (xhat_m, rstd_m), s2 = saved[i]
        terms, gbuf["ffn2_w_in"], g3, dg, db = _ffn_bwd(
            f"l{i}_ffn2", alpha, terms, s2, full["ffn2_w_in"], rows3["ffn2_w_out"], i, ln2_g[i:i + 1],
            gbuf["ffn2_w_in"], _rows_view(gbuf["ffn2_w_out"]))
        gbuf["ffn2_w_out"] = g3.reshape(gshape["ffn2_w_out"])
        gsmall["ln2_g"][i], gsmall["ln2_b"][i] = dg, db
        scatter(f"l{i}_ffn2", [("ffn2_w_in", i), ("ffn2_w_out", i)])
        dz, dg, db = _ln_bwd(f"l{i}_lnm_bwd", terms, xhat_m, rstd_m, lnm_g[i:i + 1])
        gsmall["lnm_g"][i], gsmall["lnm_b"][i] = dg, db
        if i % 2 == 0:
            dx, d_wpad, dbf, g3 = _fox_bwd(f"l{i}_fox", dz, sm, fox_wpad(j), fox_b_f[j:j + 1],
                                           rows3["fox_w_o"], j, _rows_view(gbuf["fox_w_o"]))
            gbuf["fox_w_o"] = g3.reshape(gshape["fox_w_o"])
            gbuf[fox_in_names[j]] = d_wpad[:, :fox_cols].reshape(d, N_CHIPS, -1).transpose(1, 0, 2)[None].astype(BF16)
            gsmall["fox_b_f"][j] = dbf
            scatter(f"l{i}_fox", [("fox_w_o", j), (fox_in_names[j], 0)])
        else:
            dx, gbuf["s5_w_out"], s5_cot[j] = _s5_bwd(f"l{i}_s5", dz, sm, s5_params(j), full["s5_w_out"], j,
                                                      gbuf["s5_w_out"])
            scatter(f"l{i}_s5", [("s5_w_out", j)])
        terms = [(dz, alpha), (dx, 1.0)]
        terms, gbuf["ffn1_w_in"], g3, dg, db = _ffn_bwd(
            f"l{i}_ffn1", alpha, terms, s1, full["ffn1_w_in"], rows3["ffn1_w_out"], i, ln1_g[i:i + 1],
            gbuf["ffn1_w_in"], _rows_view(gbuf["ffn1_w_out"]))
        gbuf["ffn1_w_out"] = g3.reshape(gshape["ffn1_w_out"])
        gsmall["ln1_g"][i], gsmall["ln1_b"][i] = dg, db
        scatter(f"l{i}_ffn1", [("ffn1_w_in", i), ("ffn1_w_out", i)])
    grad_x = _lincomb("grad_x", terms).reshape(x.shape)

    for tag, names, items, sem in pending:
        send, rx = _scatter_wait(f"scatter_wait_{tag}", [gbuf[n] for n in names], [rxbuf[n] for n in names],
                                 sem[0], sem[1], items)
        gbuf.update(zip(names, send))
        rxbuf.update(zip(names, rx))
    half = {n: _chip_sum(f"grad_chip_sum_{n}", gbuf[n], rxbuf[n]) for n in gshape}
    half["fox_w_in"] = jnp.concatenate([half[n] for n in fox_in_names], axis=0)
    halves = [half[n] for n in big_names]
    theirs = _send_half("grad_send_half", halves)

    cot_names = ["dlb_re", "dlb_im", "dbb_re", "dbb_im", "dc_re", "dc_im", "dd"]
    ln_names = ["ln1_g", "ln1_b", "lnm_g", "lnm_b", "ln2_g", "ln2_b"]
    pieces = [loss_part] + [jnp.concatenate(gsmall[n], axis=0) for n in ln_names + ["fox_b_f"]]
    pieces += [jnp.stack([s5_cot[j][n] for j in range(len(s5_cot))]) for n in cot_names]
    shapes = [p.shape for p in pieces]
    mine = _pack(pieces)
    everyone = _gather_all("small_gather", mine).reshape(N_DEV, *mine.shape)
    summed = _unpack(_sum_leading("small_sum", everyone), shapes)
    loss = summed[0].reshape(())
    gs_final = dict(zip(ln_names + ["fox_b_f"], summed[1:8]))
    cot = dict(zip(cot_names, summed[8:]))
    prm_names = ["s5_a_re", "s5_a_im", "s5_log_dt", "s5_b_re", "s5_b_im"]
    _, disc_vjp = jax.vjp(jax.vmap(_s5_discretise), *[w[n] for n in prm_names])
    for n, gval in zip(prm_names, disc_vjp((cot["dlb_re"], cot["dlb_im"], cot["dbb_re"], cot["dbb_im"]))):
        gs_final[n] = gval
    gs_final["s5_c_re"], gs_final["s5_c_im"], gs_final["s5_d"] = cot["dc_re"], cot["dc_im"], cot["dd"]

    grads, deltas, new_m, new_v = {}, {}, {}, {}
    for n, mine_h, their_h in zip(big_names, halves, theirs):
        grads[n], deltas[n], new_m[n], new_v[n] = _adamw_join(f"adamw_{n}", w[n], mine_h, their_h, mom[n], vel[n])
    small_shapes = [w[n].shape for n in small_names]
    for n in small_names:
        grads[n] = gs_final[n].reshape(w[n].shape)
    packed = [_pack([src[n] for n in small_names]) for src in (w, grads, mom, vel)]
    for dst, buf in zip((deltas, new_m, new_v), _adamw("adamw_small", *packed)):
        for n, val in zip(small_names, _unpack(buf, small_shapes)):
            dst[n] = val
    return (loss, grad_x, *[grads[n] for n in out_order], *[deltas[n] for n in out_order],
            *[new_m[n] for n in out_order], *[new_v[n] for n in out_order])
```

```python
import functools
import math

import jax
import jax.numpy as jnp
from jax import lax
from jax.experimental import pallas as pl
from jax.experimental.pallas import tpu as pltpu

F32 = jnp.float32
BF16 = jnp.bfloat16
LN_EPS = 1e-5
NEG_INF = -1e30
ADAM_LR = 0.001
ADAM_B1 = 0.9
ADAM_B2 = 0.999
ADAM_EPS = 1e-08
ADAM_WD = 0.01
ADAM_STEP = 10
S5_GROUP = 16
SCAN_SEGMENTS = 8
V7X_SUBLANES = 8
V7X_LANES = 128
VMEM_LIMIT = 56 * 1024 * 1024
N_CHIPS = 4
N_DEV = 8
MESH = pl.DeviceIdType.MESH
ANY = pl.BlockSpec(memory_space=pl.ANY)


def _cp(n_grid, kaxis=None):
    sem = tuple("arbitrary" if (kaxis is None or i == kaxis) else "parallel" for i in range(n_grid))
    return pltpu.CompilerParams(dimension_semantics=sem, vmem_limit_bytes=VMEM_LIMIT)


def _tile(n, pref, mult=V7X_SUBLANES):
    if n <= pref:
        return n
    for t in range(pref, 0, -1):
        if n % t == 0 and t % mult == 0:
            return t
    return n


_CONTRACT = {"nn": ((1,), (0,)), "nt": ((1,), (1,)), "tn": ((0,), (0,))}


def _mm(name, a, b, *, mode, grid, kaxis, a_blk, a_map, b_blk, b_map, o_shape, o_blk, o_map, o_dtype=F32, scale=None,
        into=None):
    nk = 1 if kaxis is None else grid[kaxis]
    assert kaxis is None or kaxis == len(grid) - 1
    dims = (_CONTRACT[mode], ((), ()))
    use_acc = nk > 1 and o_dtype != F32
    acc_shape = tuple(d for d in o_blk if d is not None)

    def body(a_ref, b_ref, *rest):
        o_ref, scratch = (rest[1], rest[2:]) if into is not None else (rest[0], rest[1:])
        p = lax.dot_general(a_ref[...].astype(BF16), b_ref[...].astype(BF16), dims, preferred_element_type=F32)
        if nk == 1:
            if scale is not None:
                p = p * scale
            o_ref[...] = p.astype(o_dtype)
            return
        acc = scratch[0] if use_acc else o_ref
        k = pl.program_id(kaxis)

        @pl.when(k == 0)
        def _():
            acc[...] = p

        @pl.when(k > 0)
        def _():
            acc[...] += p

        if use_acc or scale is not None:
            @pl.when(k == nk - 1)
            def _():
                r = acc[...]
                if scale is not None:
                    r = r * scale
                o_ref[...] = r.astype(o_dtype)

    in_specs = [pl.BlockSpec(a_blk, a_map), pl.BlockSpec(b_blk, b_map)]
    args = [a, b]
    if into is not None:
        assert into.shape == tuple(o_shape) and into.dtype == o_dtype
        in_specs.append(ANY)
        args.append(into)
    return pl.pallas_call(
        body, grid=grid, name=name, in_specs=in_specs,
        out_specs=pl.BlockSpec(o_blk, o_map),
        out_shape=jax.ShapeDtypeStruct(o_shape, o_dtype),
        input_output_aliases={2: 0} if into is not None else {},
        scratch_shapes=[pltpu.VMEM(acc_shape, F32)] if use_acc else [],
        compiler_params=_cp(len(grid), kaxis),
    )(*args)


def _mm_shards_nn(name, a, wall, layer, o_dtype):
    t, k = a.shape
    _, s, _, n = wall.shape
    tm = _tile(t, 512)
    return _mm(name, a, wall, mode="nn", grid=(s, t // tm), kaxis=None,
               a_blk=(tm, k), a_map=lambda j, i: (i, 0),
               b_blk=(None, None, k, n), b_map=lambda j, i: (layer, j, 0, 0),
               o_shape=(t, s * n), o_blk=(tm, n), o_map=lambda j, i: (i, j), o_dtype=o_dtype)


def _mm_shards_nt(name, g, wall, layer):
    t = g.shape[0]
    _, s, k, n = wall.shape
    tm = _tile(t, 512)
    return _mm(name, g, wall, mode="nt", grid=(t // tm, s), kaxis=1,
               a_blk=(tm, n), a_map=lambda i, kk: (i, kk),
               b_blk=(None, None, k, n), b_map=lambda i, kk: (layer, kk, 0, 0),
               o_shape=(t, k), o_blk=(tm, k), o_map=lambda i, kk: (i, 0))


def _mm_shards_tn(name, a, g, layer, into):
    t, k = a.shape
    _, s, _, n = into.shape
    tk = _tile(t, 512)
    return _mm(name, a, g, mode="tn", grid=(s, t // tk), kaxis=1,
               a_blk=(tk, k), a_map=lambda j, kk: (kk, 0),
               b_blk=(tk, n), b_map=lambda j, kk: (kk, j),
               o_shape=into.shape, o_blk=(None, None, k, n), o_map=lambda j, kk: (layer, j, 0, 0),
               o_dtype=into.dtype, into=into)


def _mm_nn(name, a, w, o_dtype=F32, tn=None):
    t, k = a.shape
    n = w.shape[1]
    tm = _tile(t, 512)
    tn = n if tn is None else tn
    return _mm(name, a, w, mode="nn", grid=(n // tn, t // tm), kaxis=None,
               a_blk=(tm, k), a_map=lambda j, i: (i, 0),
               b_blk=(k, tn), b_map=lambda j, i: (0, j),
               o_shape=(t, n), o_blk=(tm, tn), o_map=lambda j, i: (i, j), o_dtype=o_dtype)


def _mm_nt(name, g, w, layer=None, o_dtype=F32):
    t, k = g.shape
    n = w.shape[-2]
    tm = _tile(t, 512)
    b_blk, b_map = ((n, k), lambda i: (0, 0)) if layer is None else ((None, n, k), lambda i: (layer, 0, 0))
    return _mm(name, g, w, mode="nt", grid=(t // tm,), kaxis=None,
               a_blk=(tm, k), a_map=lambda i: (i, 0), b_blk=b_blk, b_map=b_map,
               o_shape=(t, n), o_blk=(tm, n), o_map=lambda i: (i, 0), o_dtype=o_dtype)


def _mm_tn(name, a, g, tm=None, tn=None, scale=None, layer=None, into=None):
    t, m = a.shape
    n = g.shape[1]
    tk = _tile(t, 512)
    tm = m if tm is None else tm
    tn = n if tn is None else tn
    if layer is None:
        o_shape, o_blk, o_map, o_dtype = (m, n), (tm, tn), lambda i, j, kk: (i, j), F32
    else:
        o_shape, o_blk, o_map, o_dtype = into.shape, (None, tm, tn), lambda i, j, kk: (layer, i, j), into.dtype
    return _mm(name, a, g, mode="tn", grid=(m // tm, n // tn, t // tk), kaxis=2,
               a_blk=(tk, tm), a_map=lambda i, j, kk: (kk, i),
               b_blk=(tk, tn), b_map=lambda i, j, kk: (kk, j),
               o_shape=o_shape, o_blk=o_blk, o_map=o_map, o_dtype=o_dtype, scale=scale, into=into)


def _sigmoid(x):
    return 1.0 / (1.0 + jnp.exp(-x))


def _rows_call(name, body, t, tm, ins, in_cols, outs, acc_outs=()):
    in_specs = []
    for x, c in zip(ins, in_cols):
        if x.shape[0] == 1:
            in_specs.append(pl.BlockSpec((1, c), lambda i: (0, 0)))
        else:
            in_specs.append(pl.BlockSpec((tm, c), lambda i: (i, 0)))
    out_specs = [pl.BlockSpec((tm, s.shape[1]), lambda i: (i, 0)) for s in outs]
    out_specs += [pl.BlockSpec((1, s.shape[1]), lambda i: (0, 0)) for s in acc_outs]
    return pl.pallas_call(
        body, grid=(t // tm,), name=name, in_specs=in_specs, out_specs=out_specs,
        out_shape=list(outs) + list(acc_outs), compiler_params=_cp(1),
    )(*ins)


def _ln_fwd(name, alpha, x, r, coef, g, b):
    t, d = x.shape
    tm = _tile(t, 256)

    def body(x_ref, r_ref, g_ref, b_ref, y_ref, xh_ref, rs_ref):
        z = alpha * x_ref[...] + coef * r_ref[...]
        mu = jnp.mean(z, axis=-1, keepdims=True)
        zc = z - mu
        var = jnp.mean(zc * zc, axis=-1, keepdims=True)
        rstd = lax.rsqrt(var + LN_EPS)
        xh = zc * rstd
        y_ref[...] = xh * g_ref[...] + b_ref[...]
        xh_ref[...] = xh
        rs_ref[...] = rstd

    sd = jax.ShapeDtypeStruct
    return _rows_call(name, body, t, tm, [x, r, g, b], [d, d, d, d],
                      [sd((t, d), F32), sd((t, d), F32), sd((t, 1), F32)])


def _ln_bwd(name, terms, xhat, rstd, g):
    t, d = xhat.shape
    tm = _tile(t, 256)
    n = len(terms)
    coefs = [c for _, c in terms]

    def body(*refs):
        t_refs = refs[:n]
        xh_ref, rs_ref, g_ref, dz_ref, dg_ref, db_ref = refs[n:]
        dy = coefs[0] * t_refs[0][...]
        for c, r in zip(coefs[1:], t_refs[1:]):
            dy = dy + c * r[...]
        xh = xh_ref[...]
        dxh = dy * g_ref[...]
        m1 = jnp.mean(dxh, axis=-1, keepdims=True)
        m2 = jnp.mean(dxh * xh, axis=-1, keepdims=True)
        dz_ref[...] = rs_ref[...] * (dxh - m1 - xh * m2)
        pg = jnp.sum(dy * xh, axis=0, keepdims=True)
        pb = jnp.sum(dy, axis=0, keepdims=True)
        i = pl.program_id(0)

        @pl.when(i == 0)
        def _():
            dg_ref[...] = pg
            db_ref[...] = pb

        @pl.when(i > 0)
        def _():
            dg_ref[...] += pg
            db_ref[...] += pb

    sd = jax.ShapeDtypeStruct
    arrs = [a for a, _ in terms] + [xhat, rstd, g]
    cols = [d] * n + [d, 1, d]
    return _rows_call(name, body, t, tm, arrs, cols, [sd((t, d), F32)], [sd((1, d), F32), sd((1, d), F32)])


def _lincomb(name, terms):
    t, d = terms[0][0].shape
    tm = _tile(t, 256)
    coefs = [c for _, c in terms]
    n = len(terms)

    def body(*refs):
        acc = coefs[0] * refs[0][...]
        for c, r in zip(coefs[1:], refs[1:n]):
            acc = acc + c * r[...]
        refs[n][...] = acc

    return _rows_call(name, body, t, tm, [a for a, _ in terms], [d] * n, [jax.ShapeDtypeStruct((t, d), F32)])[0]


def _loss_sum(name, y, tgt):
    t, d = y.shape
    tm = _tile(t, 256)

    def body(y_ref, t_ref, o_ref):
        e = y_ref[...] - t_ref[...]
        s = jnp.sum(jnp.sum(e * e, axis=1, keepdims=True), axis=0, keepdims=True)
        i = pl.program_id(0)

        @pl.when(i == 0)
        def _():
            o_ref[...] = s

        @pl.when(i > 0)
        def _():
            o_ref[...] += s

    return _rows_call(name, body, t, tm, [y, tgt], [d, d], [], [jax.ShapeDtypeStruct((1, 1), F32)])[0]


def _ffn_in(name, x, wall, layer):
    t, k = x.shape
    n = wall.shape[3]
    tm = _tile(t, 512)

    def body(x_ref, wg_ref, wu_ref, h_ref, a_ref):
        xb = x_ref[...].astype(BF16)
        g = lax.dot_general(xb, wg_ref[...], _NN, preferred_element_type=F32)
        u = lax.dot_general(xb, wu_ref[...], _NN, preferred_element_type=F32)
        h_ref[0] = g.astype(BF16)
        h_ref[1] = u.astype(BF16)
        a_ref[...] = (g * _sigmoid(g) * u).astype(BF16)

    return pl.pallas_call(
        body, grid=(2, t // tm), name=name,
        in_specs=[pl.BlockSpec((tm, k), lambda j, i: (i, 0)),
                  pl.BlockSpec((None, None, k, n), lambda j, i: (layer, j, 0, 0)),
                  pl.BlockSpec((None, None, k, n), lambda j, i: (layer, 2 + j, 0, 0))],
        out_specs=[pl.BlockSpec((2, tm, n), lambda j, i: (0, i, j)), pl.BlockSpec((tm, n), lambda j, i: (i, j))],
        out_shape=[jax.ShapeDtypeStruct((2, t, 2 * n), BF16), jax.ShapeDtypeStruct((t, 2 * n), BF16)],
        compiler_params=_cp(2),
    )(x, wall, wall)


def _ffn_da(name, dz, w3, layer, h):
    t, k = dz.shape
    f = w3.shape[1]
    n = f // 2
    tm = _tile(t, 512)

    def body(dz_ref, w_ref, g_ref, u_ref, dh_ref):
        d = 0.5 * lax.dot_general(dz_ref[...].astype(BF16), w_ref[...], _NT, preferred_element_type=F32)
        g = g_ref[...].astype(F32)
        u = u_ref[...].astype(F32)
        sg = _sigmoid(g)
        dh_ref[0] = (d * u * sg * (1.0 + g * (1.0 - sg))).astype(BF16)
        dh_ref[1] = (d * g * sg).astype(BF16)

    return pl.pallas_call(
        body, grid=(2, t // tm), name=name,
        in_specs=[pl.BlockSpec((tm, k), lambda j, i: (i, 0)),
                  pl.BlockSpec((None, n, k), lambda j, i: (layer, j, 0)),
                  pl.BlockSpec((None, tm, n), lambda j, i: (0, i, j)),
                  pl.BlockSpec((None, tm, n), lambda j, i: (1, i, j))],
        out_specs=pl.BlockSpec((2, tm, n), lambda j, i: (0, i, j)),
        out_shape=jax.ShapeDtypeStruct((2, t, f), BF16),
        compiler_params=_cp(2),
    )(dz, w3, h, h)


def _mm_ln(name, alpha, coef, a, w3, layer, x, g, b):
    t, k = a.shape
    d = w3.shape[2]
    tm = _tile(t, 512)

    def body(a_ref, w_ref, x_ref, g_ref, b_ref, y_ref, xh_ref, rs_ref):
        f = lax.dot_general(a_ref[...].astype(BF16), w_ref[...], _NN, preferred_element_type=F32)
        z = alpha * x_ref[...] + coef * f
        mu = jnp.mean(z, axis=-1, keepdims=True)
        zc = z - mu
        var = jnp.mean(zc * zc, axis=-1, keepdims=True)
        rstd = lax.rsqrt(var + LN_EPS)
        xh = zc * rstd
        y_ref[...] = xh * g_ref[...] + b_ref[...]
        xh_ref[...] = xh
        rs_ref[...] = rstd

    row = lambda c: pl.BlockSpec((tm, c), lambda i: (i, 0))
    vec = pl.BlockSpec((1, d), lambda i: (0, 0))
    sd = jax.ShapeDtypeStruct
    return pl.pallas_call(
        body, grid=(t // tm,), name=name,
        in_specs=[row(k), pl.BlockSpec((None, k, d), lambda i: (layer, 0, 0)), row(d), vec, vec],
        out_specs=[row(d), row(d), row(1)],
        out_shape=[sd((t, d), F32), sd((t, d), F32), sd((t, 1), F32)],
        compiler_params=_cp(1),
    )(a, w3, x, g, b)


_GELU_C = math.sqrt(2.0 / math.pi)


def _s5_act_fwd(name, ych, u, dvec):
    t, d = u.shape
    tm = _tile(t, 256)

    def body(y_ref, u_ref, d_ref, p_ref, a_ref):
        y = y_ref[...] + d_ref[...] * u_ref[...]
        p_ref[...] = y
        a_ref[...] = (0.5 * y * (1.0 + jnp.tanh(_GELU_C * (y + 0.044715 * y * y * y)))).astype(BF16)

    sd = jax.ShapeDtypeStruct
    return _rows_call(name, body, t, tm, [ych, u, dvec], [d, d, d], [sd((t, d), F32), sd((t, d), BF16)])


def _s5_act_bwd(name, dact, ypre, u, dvec):
    t, d = u.shape
    tm = _tile(t, 256)

    def body(da_ref, y_ref, u_ref, d_ref, dy_ref, ds_ref, dd_ref):
        y = y_ref[...]
        th = jnp.tanh(_GELU_C * (y + 0.044715 * y * y * y))
        dg = 0.5 * (1.0 + th) + 0.5 * y * (1.0 - th * th) * _GELU_C * (1.0 + 3.0 * 0.044715 * y * y)
        dy = da_ref[...] * dg
        dy_ref[...] = dy
        ds_ref[...] = dy * d_ref[...]
        pd = jnp.sum(dy * u_ref[...], axis=0, keepdims=True)
        i = pl.program_id(0)

        @pl.when(i == 0)
        def _():
            dd_ref[...] = pd

        @pl.when(i > 0)
        def _():
            dd_ref[...] += pd

    sd = jax.ShapeDtypeStruct
    return _rows_call(name, body, t, tm, [dact, ypre, u, dvec], [d, d, d, d],
                      [sd((t, d), F32), sd((t, d), F32)], [sd((1, d), F32)])


def _glu_fwd(name, vg):
    t, d2 = vg.shape
    d = d2 // 2
    tm = _tile(t, 256)

    def body(vg_ref, m_ref):
        m_ref[...] = vg_ref[:, :d] * _sigmoid(vg_ref[:, d:])

    return _rows_call(name, body, t, tm, [vg], [d2], [jax.ShapeDtypeStruct((t, d), F32)])[0]


def _glu_bwd(name, dm, vg):
    t, d2 = vg.shape
    d = d2 // 2
    tm = _tile(t, 256)

    def body(dm_ref, vg_ref, o_ref):
        sg = _sigmoid(vg_ref[:, d:])
        g = dm_ref[...]
        o_ref[:, :d] = (g * sg).astype(BF16)
        o_ref[:, d:] = (g * vg_ref[:, :d] * sg * (1.0 - sg)).astype(BF16)

    return _rows_call(name, body, t, tm, [dm, vg], [d, d2], [jax.ShapeDtypeStruct((t, d2), BF16)])[0]


def _adamw(name, w, g, m, v):
    r, c = w.shape
    tr = _tile(r, max(V7X_SUBLANES, (1 << 20) // (4 * c) // V7X_SUBLANES * V7X_SUBLANES))

    def body(w_ref, g_ref, m_ref, v_ref, d_ref, nm_ref, nv_ref):
        gg = g_ref[...]
        nm = ADAM_B1 * m_ref[...] + (1.0 - ADAM_B1) * gg
        nv = ADAM_B2 * v_ref[...] + (1.0 - ADAM_B2) * (gg * gg)
        m_hat = nm / (1.0 - ADAM_B1 ** ADAM_STEP)
        v_hat = nv / (1.0 - ADAM_B2 ** ADAM_STEP)
        d_ref[...] = -ADAM_LR * (m_hat / (jnp.sqrt(v_hat) + ADAM_EPS) + ADAM_WD * w_ref[...])
        nm_ref[...] = nm
        nv_ref[...] = nv

    sd = jax.ShapeDtypeStruct((r, c), F32)
    return _rows_call(name, body, r, tr, [w, g, m, v], [c] * 4, [sd, sd, sd])


def _my_shard():
    return 2 * lax.axis_index("x") + lax.axis_index("y")


def _my_core():
    return lax.axis_index("c")


def _adamw_join(name, w, mine, theirs, m, v):
    nl, r, c = w.shape
    h = r // 2
    tr = _tile(h, max(V7X_SUBLANES, (1 << 19) // (4 * c) // V7X_SUBLANES * V7X_SUBLANES))
    nb = h // tr

    def body(w_ref, a_ref, b_ref, m_ref, v_ref, g_ref, d_ref, nm_ref, nv_ref):
        gg = jnp.where(pl.program_id(1) == _my_core(), a_ref[...], b_ref[...])
        nm = ADAM_B1 * m_ref[...] + (1.0 - ADAM_B1) * gg
        nv = ADAM_B2 * v_ref[...] + (1.0 - ADAM_B2) * (gg * gg)
        m_hat = nm / (1.0 - ADAM_B1 ** ADAM_STEP)
        v_hat = nv / (1.0 - ADAM_B2 ** ADAM_STEP)
        g_ref[...] = gg
        d_ref[...] = -ADAM_LR * (m_hat / (jnp.sqrt(v_hat) + ADAM_EPS) + ADAM_WD * w_ref[...])
        nm_ref[...] = nm
        nv_ref[...] = nv

    full = pl.BlockSpec((None, tr, c), lambda l, hf, i: (l, hf * nb + i, 0))
    sd = jax.ShapeDtypeStruct((nl, r, c), F32)
    return pl.pallas_call(
        body, name=name, grid=(nl, 2, nb),
        in_specs=[full,
                  pl.BlockSpec((None, tr, c), lambda l, hf, i: (l, jnp.where(hf == _my_core(), i, 0), 0)),
                  pl.BlockSpec((None, tr, c), lambda l, hf, i: (l, jnp.where(hf == _my_core(), 0, i), 0)),
                  full, full],
        out_specs=[full, full, full, full],
        out_shape=[sd, sd, sd, sd],
        compiler_params=_cp(3),
    )(w, mine, theirs, m, v)


def _sum_leading(name, a):
    n, r, c = a.shape
    tr = _tile(r, 512)

    def body(a_ref, o_ref):
        acc = a_ref[0]
        for k in range(1, n):
            acc = acc + a_ref[k]
        o_ref[...] = acc

    return pl.pallas_call(
        body, grid=(r // tr,), name=name,
        in_specs=[pl.BlockSpec((n, tr, c), lambda i: (0, i, 0))],
        out_specs=pl.BlockSpec((tr, c), lambda i: (i, 0)),
        out_shape=jax.ShapeDtypeStruct((r, c), F32), compiler_params=_cp(1),
    )(a)


def _split3(x):
    hi = x.astype(BF16)
    r1 = x - hi.astype(F32)
    mid = r1.astype(BF16)
    lo = (r1 - mid.astype(F32)).astype(BF16)
    return hi, mid, lo


def _tri_sum(tri, x):
    dims = (((1,), (0,)), ((), ()))
    hi, mid, lo = _split3(x)
    out = lax.dot_general(tri, lo, dims, preferred_element_type=F32)
    out = out + lax.dot_general(tri, mid, dims, preferred_element_type=F32)
    return out + lax.dot_general(tri, hi, dims, preferred_element_type=F32)


def _fox_cumsum(name, fl, bf):
    t, h = fl.shape
    tb = _tile(t, 512)

    def body(fl_ref, bf_ref, c_ref, carry):
        i = pl.program_id(0)

        @pl.when(i == 0)
        def _():
            carry[...] = jnp.zeros_like(carry)

        x = fl_ref[...] + bf_ref[...]
        lf = jnp.minimum(x, 0.0) - jnp.log(1.0 + jnp.exp(-jnp.abs(x)))
        row = lax.broadcasted_iota(jnp.int32, (tb, tb), 0)
        col = lax.broadcasted_iota(jnp.int32, (tb, tb), 1)
        tri = jnp.where(row >= col, 1.0, 0.0).astype(BF16)
        c_ref[...] = _tri_sum(tri, lf) + carry[...]
        carry[...] += jnp.sum(lf, axis=0, keepdims=True)

    return pl.pallas_call(
        body, grid=(t // tb,), name=name,
        in_specs=[pl.BlockSpec((tb, h), lambda i: (i, 0)), pl.BlockSpec((1, h), lambda i: (0, 0))],
        out_specs=pl.BlockSpec((tb, h), lambda i: (i, 0)),
        out_shape=jax.ShapeDtypeStruct((t, h), F32),
        scratch_shapes=[pltpu.VMEM((1, h), F32)], compiler_params=_cp(1),
    )(fl, bf)


def _fox_cumsum_bwd(name, dcum, fl, bf):
    t, h = fl.shape
    tb = _tile(t, 512)
    nb = t // tb

    def body(dc_ref, fl_ref, bf_ref, df_ref, db_ref, carry):
        i = pl.program_id(0)

        @pl.when(i == 0)
        def _():
            carry[...] = jnp.zeros_like(carry)

        dc = dc_ref[...]
        row = lax.broadcasted_iota(jnp.int32, (tb, tb), 0)
        col = lax.broadcasted_iota(jnp.int32, (tb, tb), 1)
        tri = jnp.where(row <= col, 1.0, 0.0).astype(BF16)
        dlf = _tri_sum(tri, dc) + carry[...]
        carry[...] += jnp.sum(dc, axis=0, keepdims=True)
        x = fl_ref[...] + bf_ref[...]
        df = dlf / (1.0 + jnp.exp(x))
        df_ref[...] = df
        pb = jnp.sum(df, axis=0, keepdims=True)

        @pl.when(i == 0)
        def _():
            db_ref[...] = pb

        @pl.when(i > 0)
        def _():
            db_ref[...] += pb

    rev = lambda i: (nb - 1 - i, 0)
    return pl.pallas_call(
        body, grid=(nb,), name=name,
        in_specs=[pl.BlockSpec((tb, h), rev), pl.BlockSpec((tb, h), rev), pl.BlockSpec((1, h), lambda i: (0, 0))],
        out_specs=[pl.BlockSpec((tb, h), rev), pl.BlockSpec((1, h), lambda i: (0, 0))],
        out_shape=[jax.ShapeDtypeStruct((t, h), F32), jax.ShapeDtypeStruct((1, h), F32)],
        scratch_shapes=[pltpu.VMEM((1, h), F32)], compiler_params=_cp(1),
    )(dcum, fl, bf)


_NT = (((1,), (1,)), ((), ()))
_TN = (((0,), (0,)), ((), ()))
_NN = (((1,), (0,)), ((), ()))


def _causal_mask(s, tb):
    row = lax.broadcasted_iota(jnp.int32, (tb, tb), 0)
    col = lax.broadcasted_iota(jnp.int32, (tb, tb), 1)
    return jnp.where(col <= row, s, NEG_INF)


def _first_head_lanes(hd):
    return lax.broadcasted_iota(jnp.int32, (1, 2 * hd), 1) < hd


def _attn_fwd(name, qkv, ccol, crow, nh):
    nb, tb, d3 = qkv.shape
    d = d3 // 3
    hd = d // nh
    lanes = 2 * hd
    assert lanes == V7X_LANES
    scale = 1.0 / math.sqrt(hd)

    def body(q_ref, k_ref, v_ref, cc_ref, cr_ref, o_ref, lse_ref):
        i = pl.program_id(1)
        first = _first_head_lanes(hd)
        q = q_ref[...] * scale
        res = []
        for hh in (0, 1):
            qh = jnp.where(first if hh == 0 else jnp.logical_not(first), q, jnp.zeros_like(q))
            cc = cc_ref[:, hh:hh + 1]

            def step(j, carry, diagonal=False, qh=qh, cc=cc, hh=hh):
                m, l, acc = carry
                s = lax.dot_general(qh, k_ref[j], _NT, preferred_element_type=F32) + cc - cr_ref[j][hh:hh + 1, :]
                if diagonal:
                    s = _causal_mask(s, tb)
                m_new = jnp.maximum(m, jnp.max(s, axis=1, keepdims=True))
                p = jnp.exp(s - m_new)
                a = jnp.exp(m - m_new)
                l = a * l + jnp.sum(p, axis=1, keepdims=True)
                acc = a * acc + lax.dot_general(p.astype(BF16), v_ref[j], _NN, preferred_element_type=F32)
                return m_new, l, acc

            init = (jnp.full((tb, 1), NEG_INF, F32), jnp.zeros((tb, 1), F32), jnp.zeros((tb, lanes), F32))
            m, l, acc = step(i, lax.fori_loop(0, i, step, init), diagonal=True)
            res.append((acc / l, m + jnp.log(l)))
        o_ref[...] = jnp.where(first, res[0][0], res[1][0])
        lse_ref[:, 0:1] = res[0][1]
        lse_ref[:, 1:2] = res[1][1]

    kb, vb = d // lanes, 2 * d // lanes
    return pl.pallas_call(
        body, grid=(nh // 2, nb), name=name,
        in_specs=[pl.BlockSpec((None, tb, lanes), lambda h, i: (i, 0, h)),
                  pl.BlockSpec((nb, tb, lanes), lambda h, i: (0, 0, kb + h)),
                  pl.BlockSpec((nb, tb, lanes), lambda h, i: (0, 0, vb + h)),
                  pl.BlockSpec((None, None, tb, 2), lambda h, i: (h, i, 0, 0)),
                  pl.BlockSpec((None, nb, 2, tb), lambda h, i: (h, 0, 0, 0))],
        out_specs=[pl.BlockSpec((None, tb, lanes), lambda h, i: (i, 0, h)),
                   pl.BlockSpec((None, None, tb, 2), lambda h, i: (h, i, 0, 0))],
        out_shape=[jax.ShapeDtypeStruct((nb, tb, d), F32), jax.ShapeDtypeStruct((nh // 2, nb, tb, 2), F32)],
        compiler_params=_cp(2),
    )(qkv, qkv, qkv, ccol, crow)


def _attn_bwd(name, qkv, ccol, crow, o, lse, do, nh):
    nb, tb, d3 = qkv.shape
    d = d3 // 3
    hd = d // nh
    lanes = 2 * hd
    scale = 1.0 / math.sqrt(hd)

    def body(q_ref, k_ref, v_ref, cc_ref, cr_ref, o_ref, lse_ref, do_ref, dq_ref, dk_ref, dv_ref, dr_ref, dc_ref, dq_acc):
        j = pl.program_id(1)

        @pl.when(j == 0)
        def _():
            dq_acc[...] = jnp.zeros_like(dq_acc)
            dr_ref[...] = jnp.zeros_like(dr_ref)

        first = _first_head_lanes(hd)
        kj = k_ref[...]
        vj = v_ref[...]
        dk = jnp.zeros((tb, lanes), F32)
        dv = jnp.zeros((tb, lanes), F32)
        for hh in (0, 1):
            mine = first if hh == 0 else jnp.logical_not(first)
            cr = cr_ref[hh:hh + 1, :]

            def step(i, carry, diagonal=False, mine=mine, cr=cr, hh=hh):
                dk, dv, dc = carry
                qi = q_ref[i] * scale
                qh = jnp.where(mine, qi, jnp.zeros_like(qi))
                doh = jnp.where(mine, do_ref[i], 0.0)
                dob = doh.astype(BF16)
                di = jnp.sum(doh * o_ref[i], axis=1, keepdims=True)
                s = lax.dot_general(qh, kj, _NT, preferred_element_type=F32) + cc_ref[i][:, hh:hh + 1] - cr
                if diagonal:
                    s = _causal_mask(s, tb)
                p = jnp.exp(s - lse_ref[i][:, hh:hh + 1])
                dv = dv + lax.dot_general(p.astype(BF16), dob, _TN, preferred_element_type=F32)
                dp = lax.dot_general(dob, vj, _NT, preferred_element_type=F32)
                ds = p * (dp - di)
                dsb = ds.astype(BF16)
                dk = dk + lax.dot_general(dsb, qh, _TN, preferred_element_type=F32)
                dq = lax.dot_general(dsb, kj, _NN, preferred_element_type=F32) * scale
                dq_acc[i] += jnp.where(mine, dq, 0.0)
                dr_ref[i, :, hh:hh + 1] += jnp.sum(ds, axis=1, keepdims=True)
                dc = dc + jnp.sum(ds, axis=0, keepdims=True)
                return dk, dv, dc

            dk, dv, dc = lax.fori_loop(j + 1, nb, step, step(j, (dk, dv, jnp.zeros((1, tb), F32)), diagonal=True))
            dc_ref[hh:hh + 1, :] = dc
        dk_ref[...] = dk.astype(BF16)
        dv_ref[...] = dv.astype(BF16)

        @pl.when(j == nb - 1)
        def _():
            dq_ref[...] = dq_acc[...].astype(BF16)

    kb, vb = d // lanes, 2 * d // lanes
    whole = lambda c: pl.BlockSpec((nb, tb, lanes), lambda h, j: (0, 0, c + h))
    block = lambda c: pl.BlockSpec((None, tb, lanes), lambda h, j: (j, 0, c + h))
    cols = pl.BlockSpec((None, nb, tb, 2), lambda h, j: (h, 0, 0, 0))
    rows = pl.BlockSpec((None, None, 2, tb), lambda h, j: (h, j, 0, 0))
    sd = jax.ShapeDtypeStruct
    return pl.pallas_call(
        body, grid=(nh // 2, nb), name=name,
        in_specs=[whole(0), block(kb), block(vb), cols, rows, whole(0), cols, whole(0)],
        out_specs=[whole(0), block(0), block(0), cols, rows],
        out_shape=[sd((nb, tb, d), BF16), sd((nb, tb, d), BF16), sd((nb, tb, d), BF16),
                   sd((nh // 2, nb, tb, 2), F32), sd((nh // 2, nb, 2, tb), F32)],
        scratch_shapes=[pltpu.VMEM((nb, tb, lanes), F32)],
        compiler_params=_cp(2),
    )(qkv, qkv, qkv, ccol, crow, o, lse, do)


def _cmul(ar, ai, br, bi):
    return ar * br - ai * bi, ar * bi + ai * br


def _s5_scan(name, lam, xin, hs=None):
    reverse = hs is not None
    _, seg, ns, w = xin.shape
    assert ns == SCAN_SEGMENTS
    wb = min(w, 2 * V7X_LANES)
    nsq = seg.bit_length() - 1
    assert (1 << nsq) == seg

    def body(*refs):
        if reverse:
            lam_ref, x_ref, h_ref, o_ref, dl_ref = refs
        else:
            lam_ref, x_ref, o_ref = refs
        lr = jnp.broadcast_to(lam_ref[0], (ns, wb))
        li = jnp.broadcast_to(lam_ref[1], (ns, wb))
        if reverse:
            li = -li
        zero = jnp.zeros((ns, wb), F32)
        at = (lambda n: seg - 1 - n) if reverse else (lambda n: n)

        def local(n, c):
            r = at(n)
            mr, mi = _cmul(lr, li, c[0], c[1])
            nr = mr + x_ref[0, r]
            ni = mi + x_ref[1, r]
            o_ref[0, r] = nr
            o_ref[1, r] = ni
            return nr, ni

        er, ei = lax.fori_loop(0, seg, local, (zero, zero))
        pr, pi = lr, li
        for _ in range(nsq):
            pr, pi = _cmul(pr, pi, pr, pi)
        sub = lax.broadcasted_iota(jnp.int32, (ns, wb), 0)

        def shifted(a, sh):
            if reverse:
                return jnp.where(sub < ns - sh, pltpu.roll(a, ns - sh, 0), 0.0)
            return jnp.where(sub >= sh, pltpu.roll(a, sh, 0), 0.0)

        xr, xi = er, ei
        sh = 1
        while sh < ns:
            tr, ti = _cmul(pr, pi, shifted(xr, sh), shifted(xi, sh))
            xr, xi = xr + tr, xi + ti
            pr, pi = _cmul(pr, pi, pr, pi)
            sh *= 2
        cr, ci = shifted(xr, 1), shifted(xi, 1)

        def fix(r, q):
            tr, ti = _cmul(q[0], q[1], cr, ci)
            gr = o_ref[0, r] + tr
            gi = o_ref[1, r] + ti
            o_ref[0, r] = gr
            o_ref[1, r] = gi
            return gr, gi

        if not reverse:
            def fixup(n, q):
                fix(n, q)
                return _cmul(q[0], q[1], lr, li)

            lax.fori_loop(0, seg, fixup, (lr, li))
            return

        def fixup_acc(n, c):
            qr, qi, ar, ai = c
            r = seg - 1 - n
            gr, gi = fix(r, (qr, qi))
            hr = h_ref[0, r - 1]
            hi = h_ref[1, r - 1]
            qr, qi = _cmul(qr, qi, lr, li)
            return qr, qi, ar + gr * hr + gi * hi, ai + gi * hr - gr * hi

        qr, qi, ar, ai = lax.fori_loop(0, seg - 1, fixup_acc, (lr, li, zero, zero))
        gr, gi = fix(0, (qr, qi))
        hr = jnp.where(sub >= 1, pltpu.roll(h_ref[0, seg - 1], 1, 0), 0.0)
        hi = jnp.where(sub >= 1, pltpu.roll(h_ref[1, seg - 1], 1, 0), 0.0)
        dl_ref[0] = ar + gr * hr + gi * hi
        dl_ref[1] = ai + gi * hr - gr * hi

    big = pl.BlockSpec((2, seg, ns, wb), lambda j: (0, 0, 0, j))
    lam_spec = pl.BlockSpec((2, 1, wb), lambda j: (0, 0, j))
    sd = jax.ShapeDtypeStruct
    if reverse:
        return pl.pallas_call(
            body, grid=(w // wb,), name=name, in_specs=[lam_spec, big, big],
            out_specs=[big, pl.BlockSpec((2, ns, wb), lambda j: (0, 0, j))],
            out_shape=[sd(xin.shape, F32), sd((2, ns, w), F32)], compiler_params=_cp(1),
        )(lam, xin, hs)
    return pl.pallas_call(
        body, grid=(w // wb,), name=name, in_specs=[lam_spec, big], out_specs=big,
        out_shape=sd(xin.shape, F32), compiler_params=_cp(1),
    )(lam, xin)


def _place():
    x, y, c = lax.axis_index("x"), lax.axis_index("y"), lax.axis_index("c")
    chips = [(1 - x, y), (x, 1 - y), (1 - x, 1 - y)]
    return x, y, c, chips


def _comm_params():
    return pltpu.CompilerParams(vmem_limit_bytes=VMEM_LIMIT)


def _cast_place(name, w):
    nl, r, c = w.shape
    tr = _tile(r, max(16, (1 << 20) // (4 * c) // 16 * 16), 16)

    def body(w_ref, o_ref):
        o_ref[...] = w_ref[...].astype(BF16)

    return pl.pallas_call(
        body, name=name, grid=(nl, r // tr),
        in_specs=[pl.BlockSpec((None, tr, c), lambda l, i: (l, i, 0))],
        out_specs=pl.BlockSpec((None, None, tr, c), lambda l, i: (l, _my_shard(), i, 0)),
        out_shape=jax.ShapeDtypeStruct((nl, N_CHIPS, r, c), BF16),
        compiler_params=_cp(2),
    )(w)


def _gather_shards(name, bufs):
    n = len(bufs)

    def body(*refs):
        outs = refs[n:2 * n]
        send_sems, recv_sems = refs[2 * n:]
        x, y, c, chips = _place()
        my = 2 * x + y
        sibling = (x, y, 1 - c)

        def part(t, shard, half):
            h = bufs[t].shape[2] // 2
            return outs[t].at[:, shard, pl.ds(half * h, h)]

        def copy(t, k, ref, to):
            return pltpu.make_async_remote_copy(src_ref=ref, dst_ref=ref, send_sem=send_sems.at[t, k],
                                                recv_sem=recv_sems.at[t, k], device_id=to, device_id_type=MESH)

        sent = []
        for t in range(n):
            for k, chip in enumerate(chips):
                sent.append(copy(t, k, part(t, my, c), (*chip, c)))
                sent[-1].start()
        for k, chip in enumerate(chips):
            shard = 2 * chip[0] + chip[1]
            for t in range(n):
                copy(t, k, part(t, shard, c), (*chip, c)).wait_recv()
                sent.append(copy(t, 3 + k, part(t, shard, c), sibling))
                sent[-1].start()
        for k, chip in enumerate(chips):
            shard = 2 * chip[0] + chip[1]
            for t in range(n):
                copy(t, 3 + k, part(t, shard, 1 - c), sibling).wait_recv()
        for cp in sent:
            cp.wait_send()

    return pl.pallas_call(
        body, name=name, in_specs=[ANY] * n, out_specs=[ANY] * n,
        out_shape=[jax.ShapeDtypeStruct(b.shape, b.dtype) for b in bufs],
        input_output_aliases={t: t for t in range(n)},
        scratch_shapes=[pltpu.SemaphoreType.DMA((n, 6)), pltpu.SemaphoreType.DMA((n, 6))],
        compiler_params=_comm_params(),
    )(*bufs)


HBM_SPEC = pl.BlockSpec(memory_space=pltpu.HBM)
SEM_SPEC = pl.BlockSpec(memory_space=pltpu.SEMAPHORE)


def _split_params():
    return pltpu.CompilerParams(has_side_effects=pltpu.SideEffectType.DATAFLOW_SIDE_EFFECTING)


def _gather_start(name, bufs, groups):
    n, ng = len(bufs), len(groups)

    def body(*refs):
        sems = refs[n:n + 2 * ng]
        outs = refs[n + 2 * ng:]
        x, y, c, chips = _place()
        my = 2 * x + y
        for gi, group in enumerate(groups):
            for idx, (t, layer) in enumerate(group):
                block = outs[t].at[layer, my]
                for k, chip in enumerate(chips):
                    pltpu.make_async_remote_copy(
                        src_ref=block, dst_ref=block, send_sem=sems[2 * gi].at[3 * idx + k],
                        recv_sem=sems[2 * gi + 1].at[3 * idx + k], device_id=(*chip, c), device_id_type=MESH).start()

    sem_shapes = []
    for group in groups:
        sem_shapes += [pltpu.SemaphoreType.DMA((3 * len(group),))] * 2
    res = pl.pallas_call(
        body, name=name, in_specs=[HBM_SPEC] * n,
        out_specs=[SEM_SPEC] * (2 * ng) + [HBM_SPEC] * n,
        out_shape=sem_shapes + [pltpu.HBM(b.shape, b.dtype) for b in bufs],
        input_output_aliases={t: 2 * ng + t for t in range(n)},
        compiler_params=_split_params(),
    )(*[pltpu.with_memory_space_constraint(b, pltpu.HBM) for b in bufs])
    sems = [(res[2 * gi], res[2 * gi + 1]) for gi in range(ng)]
    return sems, list(res[2 * ng:])


def _gather_wait(name, bufs, send_sems, recv_sems, after, group):
    n = len(bufs)

    def body(*refs):
        ss, rs = refs[n], refs[n + 1]
        outs = refs[n + 3:]
        x, y, c, chips = _place()
        my = 2 * x + y
        for idx, (t, layer) in enumerate(group):
            for k, chip in enumerate(chips):
                cp = pltpu.make_async_remote_copy(
                    src_ref=outs[t].at[layer, my], dst_ref=outs[t].at[layer, 2 * chip[0] + chip[1]],
                    send_sem=ss.at[3 * idx + k], recv_sem=rs.at[3 * idx + k], device_id=(*chip, c), device_id_type=MESH)
                cp.wait_send()
                cp.wait_recv()

    return list(pl.pallas_call(
        body, name=name, in_specs=[HBM_SPEC] * n + [SEM_SPEC, SEM_SPEC, ANY],
        out_specs=[HBM_SPEC] * n,
        out_shape=[pltpu.HBM(b.shape, b.dtype) for b in bufs],
        input_output_aliases={t: t for t in range(n)},
        compiler_params=_split_params(),
    )(*bufs, send_sems, recv_sems, after))


N_PARTS = 7


def _scatter_items(send, rx, items, c, chips, x, y):
    my = 2 * x + y
    out = []
    for i, (k, layer) in enumerate(items):
        h = send[k].shape[2] // 2
        for kk, chip in enumerate(chips):
            shard = 2 * chip[0] + chip[1]
            for hf in (0, 1):
                out.append((send[k].at[layer, shard, pl.ds(hf * h, h)], rx[k].at[2 * kk + c, layer],
                            N_PARTS * i + 2 * kk + hf, N_PARTS * i + 2 * kk + c, (*chip, hf)))
        out.append((send[k].at[layer, my, pl.ds((1 - c) * h, h)], rx[k].at[N_PARTS - 1, layer],
                    N_PARTS * i + N_PARTS - 1, N_PARTS * i + N_PARTS - 1, (x, y, 1 - c)))
    return out


def _scatter_start(name, send, rx, items):
    n = len(send)
    m = N_PARTS * len(items)

    def body(*refs):
        ssem, rsem = refs[2 * n], refs[2 * n + 1]
        s_out, r_out = refs[2 * n + 2:3 * n + 2], refs[3 * n + 2:]
        x, y, c, chips = _place()
        for src, dst, si, ri, to in _scatter_items(s_out, r_out, items, c, chips, x, y):
            pltpu.make_async_remote_copy(src_ref=src, dst_ref=dst, send_sem=ssem.at[si], recv_sem=rsem.at[ri],
                                         device_id=to, device_id_type=MESH).start()

    res = pl.pallas_call(
        body, name=name, in_specs=[HBM_SPEC] * (2 * n),
        out_specs=[SEM_SPEC, SEM_SPEC] + [HBM_SPEC] * (2 * n),
        out_shape=[pltpu.SemaphoreType.DMA((m,)), pltpu.SemaphoreType.DMA((m,))]
        + [pltpu.HBM(b.shape, b.dtype) for b in list(send) + list(rx)],
        input_output_aliases={t: 2 + t for t in range(2 * n)},
        compiler_params=_split_params(),
    )(*[pltpu.with_memory_space_constraint(b, pltpu.HBM) for b in list(send) + list(rx)])
    return (res[0], res[1]), list(res[2:2 + n]), list(res[2 + n:])


def _scatter_wait(name, send, rx, ssem, rsem, items):
    n = len(send)

    def body(*refs):
        ss, rs = refs[2 * n], refs[2 * n + 1]
        s_out, r_out = refs[2 * n + 2:3 * n + 2], refs[3 * n + 2:]
        x, y, c, chips = _place()
        for i, (src, dst, si, ri, to) in enumerate(_scatter_items(s_out, r_out, items, c, chips, x, y)):
            arrival = i % N_PARTS
            landed = r_out[items[i // N_PARTS][0]].at[arrival, items[i // N_PARTS][1]]
            cp = pltpu.make_async_remote_copy(src_ref=src, dst_ref=landed, send_sem=ss.at[si],
                                              recv_sem=rs.at[N_PARTS * (i // N_PARTS) + arrival],
                                              device_id=to, device_id_type=MESH)
            cp.wait_send()
            cp.wait_recv()

    res = pl.pallas_call(
        body, name=name, in_specs=[HBM_SPEC] * (2 * n) + [SEM_SPEC, SEM_SPEC],
        out_specs=[HBM_SPEC] * (2 * n),
        out_shape=[pltpu.HBM(b.shape, b.dtype) for b in list(send) + list(rx)],
        input_output_aliases={t: t for t in range(2 * n)},
        compiler_params=_split_params(),
    )(*send, *rx, ssem, rsem)
    return list(res[:n]), list(res[n:])


def _chip_sum(name, g, rx):
    nl, _, r, c = g.shape
    h = r // 2
    tr = _tile(h, max(V7X_SUBLANES * 2, (1 << 19) // (2 * c) // 16 * 16), 16)
    nb = h // tr

    def body(g_ref, r_ref, o_ref):
        acc = g_ref[...].astype(F32)
        for k in range(N_PARTS):
            acc = acc + r_ref[k].astype(F32)
        o_ref[...] = acc

    return pl.pallas_call(
        body, name=name, grid=(nl, nb),
        in_specs=[pl.BlockSpec((None, None, tr, c), lambda l, i: (l, _my_shard(), _my_core() * nb + i, 0)),
                  pl.BlockSpec((N_PARTS, None, tr, c), lambda l, i: (0, l, i, 0))],
        out_specs=pl.BlockSpec((None, tr, c), lambda l, i: (l, i, 0)),
        out_shape=jax.ShapeDtypeStruct((nl, h, c), F32),
        compiler_params=_cp(2),
    )(g, rx)


def _send_half(name, fs):
    n = len(fs)

    def body(*refs):
        ins, outs = refs[:n], refs[n:2 * n]
        send_sems, recv_sems = refs[2 * n:]
        x, y, c, _ = _place()
        cps = []
        for t in range(n):
            cps.append(pltpu.make_async_remote_copy(
                src_ref=ins[t], dst_ref=outs[t], send_sem=send_sems.at[t], recv_sem=recv_sems.at[t],
                device_id=(x, y, 1 - c), device_id_type=MESH))
            cps[-1].start()
        for cp in cps:
            cp.wait()

    return pl.pallas_call(
        body, name=name, in_specs=[ANY] * n, out_specs=[ANY] * n,
        out_shape=[jax.ShapeDtypeStruct(f.shape, f.dtype) for f in fs],
        scratch_shapes=[pltpu.SemaphoreType.DMA((n,)), pltpu.SemaphoreType.DMA((n,))],
        compiler_params=_comm_params(),
    )(*fs)


def _gather_all(name, v):
    m_per = v.shape[0]

    def body(x_ref, out_ref, send_sems, recv_sems, local_sem):
        x, y, c, chips = _place()
        me, sibling = (x, y, c), (x, y, 1 - c)

        def rows(px, py, pc):
            return out_ref.at[pl.ds((4 * px + 2 * py + pc) * m_per, m_per), :]

        def copy(k, block, to, src=None):
            return pltpu.make_async_remote_copy(
                src_ref=rows(*block) if src is None else src, dst_ref=rows(*block), send_sem=send_sems.at[k],
                recv_sem=recv_sems.at[k], device_id=to, device_id_type=MESH)

        mine = pltpu.make_async_copy(x_ref, rows(*me), local_sem)
        mine.start()
        first = [copy(0, me, sibling, src=x_ref)]
        first += [copy(1 + j, me, (*chip, c), src=x_ref) for j, chip in enumerate(chips)]
        for cp in first:
            cp.start()
        passed = [copy(4 + j, (*chip, c), sibling) for j, chip in enumerate(chips)]
        for j, chip in enumerate(chips):
            copy(1 + j, (*chip, c), me).wait_recv()
            passed[j].start()
        copy(0, sibling, me).wait_recv()
        for j, chip in enumerate(chips):
            copy(4 + j, (*chip, 1 - c), me).wait_recv()
        for cp in first + passed:
            cp.wait_send()
        mine.wait()

    return pl.pallas_call(
        body, name=name, in_specs=[ANY], out_specs=ANY,
        out_shape=jax.ShapeDtypeStruct((N_DEV * m_per, v.shape[1]), v.dtype),
        scratch_shapes=[pltpu.SemaphoreType.DMA((7,)), pltpu.SemaphoreType.DMA((7,)), pltpu.SemaphoreType.DMA],
        compiler_params=_comm_params(),
    )(v)


def _rows_view(wall):
    nl, s, r, c = wall.shape
    return wall.reshape(nl, s * r, c)


def _ffn_fwd(tag, alpha, x, w_in, w_out3, layer, g, b):
    h, a = _ffn_in(f"{tag}_in", x, w_in, layer)
    y, xhat, rstd = _mm_ln(f"{tag}_out", alpha, 0.5, a, w_out3, layer, x, g, b)
    return y, (x, h, a, xhat, rstd)


def _ffn_bwd(tag, alpha, terms, saved, w_in, w_out3, layer, g, g_win, g_wout3):
    x, h, a, xhat, rstd = saved
    t = x.shape[0]
    _, s, k, n = w_in.shape
    tm = _tile(t, 512)
    dz, dg, db = _ln_bwd(f"{tag}_ln_bwd", terms, xhat, rstd, g)
    g_wout3 = _mm_tn(f"{tag}_dwout", a, dz, tm=n, scale=0.5, layer=layer, into=g_wout3)
    dh = _ffn_da(f"{tag}_da", dz, w_out3, layer, h)
    g_win = _mm(f"{tag}_dwin", x, dh, mode="tn", grid=(s, t // tm), kaxis=1,
                a_blk=(tm, k), a_map=lambda j, kk: (kk, 0),
                b_blk=(None, tm, n), b_map=lambda j, kk: (j // 2, kk, j % 2),
                o_shape=w_in.shape, o_blk=(None, None, k, n), o_map=lambda j, kk: (layer, j, 0, 0),
                o_dtype=g_win.dtype, into=g_win)
    dx = _mm(f"{tag}_dx", dh, w_in, mode="nt", grid=(t // tm, s), kaxis=1,
             a_blk=(None, tm, n), a_map=lambda i, kk: (kk // 2, i, kk % 2),
             b_blk=(None, None, k, n), b_map=lambda i, kk: (layer, kk, 0, 0),
             o_shape=(t, k), o_blk=(tm, k), o_map=lambda i, kk: (i, 0))
    return [(dz, alpha), (dx, 1.0)], g_win, g_wout3, dg, db


def _fox_fwd(tag, alpha, x, w_pad, bf, w_o3, layer, g, b):
    t, d = x.shape
    nh = bf.shape[1]
    tb = _tile(t, 512)
    nb = t // tb
    qkv = _mm_nn(f"{tag}_qkv", x, w_pad[:, :3 * d], o_dtype=BF16, tn=d).reshape(nb, tb, 3 * d)
    fl = _mm_nn(f"{tag}_gate", x, w_pad[:, 3 * d:])[:, :nh]
    cum = _fox_cumsum(f"{tag}_cum", fl, bf)
    ccol = cum.reshape(nb, tb, nh // 2, 2).transpose(2, 0, 1, 3)
    crow = cum.reshape(nb, tb, nh // 2, 2).transpose(2, 0, 3, 1)
    o, lse = _attn_fwd(f"{tag}_attn", qkv, ccol, crow, nh)
    o2 = o.reshape(t, d)
    y, xhat, rstd = _mm_ln(f"{tag}_oproj", alpha, 1.0, o2, w_o3, layer, x, g, b)
    return y, xhat, rstd, (x, qkv, ccol, crow, o, lse, fl)


def _fox_bwd(tag, dm, saved, w_pad, bf, w_o3, layer, g_wo3):
    x, qkv, ccol, crow, o, lse, fl = saved
    t, d = x.shape
    nh = bf.shape[1]
    nb, tb, _ = qkv.shape
    g_wo3 = _mm_tn(f"{tag}_dwo", o.reshape(t, d), dm, layer=layer, into=g_wo3)
    do = _mm_nt(f"{tag}_do", dm, w_o3, layer=layer).reshape(nb, tb, d)
    dq, dk, dv, drow, dcol = _attn_bwd(f"{tag}_attn_bwd", qkv, ccol, crow, o, lse, do, nh)
    dcum = drow.transpose(1, 2, 0, 3).reshape(t, nh) - dcol.transpose(1, 3, 0, 2).reshape(t, nh)
    dfl, dbf = _fox_cumsum_bwd(f"{tag}_cum_bwd", dcum, fl, bf)
    pad = w_pad.shape[1] - 3 * d - nh
    dproj = jnp.concatenate([dq.reshape(t, d), dk.reshape(t, d), dv.reshape(t, d),
                             dfl.astype(BF16), jnp.zeros((t, pad), BF16)], axis=1)
    d_wpad = _mm_tn(f"{tag}_dwin", x, dproj, tn=_tile(w_pad.shape[1], 640, V7X_LANES))
    dx = _mm_nt(f"{tag}_dx", dproj, w_pad)
    return dx, d_wpad, dbf, g_wo3


def _to_segments(a):
    t, d = a.shape
    return a.reshape(SCAN_SEGMENTS, t // SCAN_SEGMENTS, d).transpose(1, 0, 2).reshape(t, d)


def _from_segments(a):
    t, d = a.shape
    return a.reshape(t // SCAN_SEGMENTS, SCAN_SEGMENTS, d).transpose(1, 0, 2).reshape(t, d)


def _s5_discretise(a_re, a_im, log_dt, b_re, b_im):
    dt = jnp.exp(log_dt)[:, None]
    mag = jnp.exp(a_re * dt)
    ang = a_im * dt
    lb_re = mag * jnp.cos(ang)
    lb_im = mag * jnp.sin(ang)
    den = a_re * a_re + a_im * a_im
    nr = lb_re - 1.0
    ni = lb_im
    z_re = (nr * a_re + ni * a_im) / den
    z_im = (ni * a_re - nr * a_im) / den
    bb_re = z_re[..., None] * b_re - z_im[..., None] * b_im
    bb_im = z_re[..., None] * b_im + z_im[..., None] * b_re
    return lb_re, lb_im, bb_re, bb_im


S5_BLOCK_GROUPS = 8


def _blockdiag_in(bb):
    g, p, h = bb.shape
    e = jnp.eye(S5_BLOCK_GROUPS, dtype=bb.dtype)
    b4 = bb.reshape(g // S5_BLOCK_GROUPS, S5_BLOCK_GROUPS, p, h)
    return jnp.einsum("jgph,gf->jghfp", b4, e).reshape(g // S5_BLOCK_GROUPS, S5_BLOCK_GROUPS * h, S5_BLOCK_GROUPS * p)


def _blockdiag_in_grad(d):
    nj, gh, gp = d.shape
    h, p = gh // S5_BLOCK_GROUPS, gp // S5_BLOCK_GROUPS
    e = jnp.eye(S5_BLOCK_GROUPS, dtype=d.dtype)
    d6 = d.reshape(nj, S5_BLOCK_GROUPS, h, S5_BLOCK_GROUPS, p)
    return jnp.einsum("jghfp,gf->jgph", d6, e).reshape(nj * S5_BLOCK_GROUPS, p, h)


def _blockdiag_out(cc):
    g, h, p = cc.shape
    e = jnp.eye(S5_BLOCK_GROUPS, dtype=cc.dtype)
    c4 = cc.reshape(g // S5_BLOCK_GROUPS, S5_BLOCK_GROUPS, h, p)
    return jnp.einsum("jghp,gf->jfpgh", c4, e).reshape(g // S5_BLOCK_GROUPS, S5_BLOCK_GROUPS * p, S5_BLOCK_GROUPS * h)


def _blockdiag_out_grad(d):
    nj, gp, gh = d.shape
    h, p = gh // S5_BLOCK_GROUPS, gp // S5_BLOCK_GROUPS
    e = jnp.eye(S5_BLOCK_GROUPS, dtype=d.dtype)
    d6 = d.reshape(nj, S5_BLOCK_GROUPS, p, S5_BLOCK_GROUPS, h)
    return jnp.einsum("jfpgh,gf->jghp", d6, e).reshape(nj * S5_BLOCK_GROUPS, h, p)


def _s5_fwd(tag, x, prm, w_out, layer):
    a_re, a_im, log_dt, b_re, b_im, c_re, c_im, d_skip = prm
    t, d = x.shape
    g, p = a_re.shape
    w = g * p
    nj = g // S5_BLOCK_GROUPS
    cw, sw = S5_BLOCK_GROUPS * S5_GROUP, S5_BLOCK_GROUPS * p
    seg = t // SCAN_SEGMENTS
    tm = _tile(t, 4096)
    lb_re, lb_im, bb_re, bb_im = _s5_discretise(a_re, a_im, log_dt, b_re, b_im)
    lam = jnp.stack([lb_re.reshape(1, w), lb_im.reshape(1, w)])
    bs = jnp.stack([_blockdiag_in(bb_re), _blockdiag_in(bb_im)]).astype(BF16)
    cs = jnp.stack([_blockdiag_out(c_re), -_blockdiag_out(c_im)]).astype(BF16)
    dvec = d_skip.reshape(1, d)
    u = _to_segments(x)
    bu = _mm(f"{tag}_bu", u, bs, mode="nn", grid=(2, nj, t // tm), kaxis=None,
             a_blk=(tm, cw), a_map=lambda r, j, i: (i, j),
             b_blk=(None, None, cw, sw), b_map=lambda r, j, i: (r, j, 0, 0),
             o_shape=(2, t, w), o_blk=(None, tm, sw), o_map=lambda r, j, i: (r, i, j))
    hs = _s5_scan(f"{tag}_scan", lam, bu.reshape(2, seg, SCAN_SEGMENTS, w)).reshape(2, t, w)
    ych = _mm(f"{tag}_ch", hs, cs, mode="nn", grid=(nj, t // tm, 2), kaxis=2,
              a_blk=(None, tm, sw), a_map=lambda j, i, r: (r, i, j),
              b_blk=(None, None, sw, cw), b_map=lambda j, i, r: (r, j, 0, 0),
              o_shape=(t, d), o_blk=(tm, cw), o_map=lambda j, i, r: (i, j))
    ypre, act = _s5_act_fwd(f"{tag}_act", ych, u, dvec)
    vg = _mm_shards_nn(f"{tag}_wout", act, w_out, layer, F32)
    m = _from_segments(_glu_fwd(f"{tag}_glu", vg))
    return m, (u, lam, bs, cs, dvec, hs, ypre, act, vg)


def _s5_bwd(tag, dm, saved, prm, w_out, layer, g_wout):
    a_re, a_im, log_dt, b_re, b_im, c_re, c_im, d_skip = prm
    u, lam, bs, cs, dvec, hs, ypre, act, vg = saved
    t, d = u.shape
    g, p = a_re.shape
    w = g * p
    nj = g // S5_BLOCK_GROUPS
    cw, sw = S5_BLOCK_GROUPS * S5_GROUP, S5_BLOCK_GROUPS * p
    seg = t // SCAN_SEGMENTS
    tm = _tile(t, 4096)
    dvg = _glu_bwd(f"{tag}_glu_bwd", _to_segments(dm), vg)
    g_wout = _mm_shards_tn(f"{tag}_dwout", act, dvg, layer, g_wout)
    dact = _mm_shards_nt(f"{tag}_dact", dvg, w_out, layer)
    dypre, duskip, dd = _s5_act_bwd(f"{tag}_act_bwd", dact, ypre, u, dvec)
    dh = _mm(f"{tag}_dh", dypre, cs, mode="nt", grid=(2, nj, t // tm), kaxis=None,
             a_blk=(tm, cw), a_map=lambda r, j, i: (i, j),
             b_blk=(None, None, sw, cw), b_map=lambda r, j, i: (r, j, 0, 0),
             o_shape=(2, t, w), o_blk=(None, tm, sw), o_map=lambda r, j, i: (r, i, j))
    dcs = _mm(f"{tag}_dc", hs, dypre, mode="tn", grid=(2, nj, t // tm), kaxis=2,
              a_blk=(None, tm, sw), a_map=lambda r, j, i: (r, i, j),
              b_blk=(tm, cw), b_map=lambda r, j, i: (i, j),
              o_shape=(2, nj, sw, cw), o_blk=(None, None, sw, cw), o_map=lambda r, j, i: (r, j, 0, 0))
    gs, dlam8 = _s5_scan(f"{tag}_scan_bwd", lam, dh.reshape(2, seg, SCAN_SEGMENTS, w),
                         hs.reshape(2, seg, SCAN_SEGMENTS, w))
    gs = gs.reshape(2, t, w)
    du = _mm(f"{tag}_du", gs, bs, mode="nt", grid=(nj, t // tm, 2), kaxis=2,
             a_blk=(None, tm, sw), a_map=lambda j, i, r: (r, i, j),
             b_blk=(None, None, cw, sw), b_map=lambda j, i, r: (r, j, 0, 0),
             o_shape=(t, d), o_blk=(tm, cw), o_map=lambda j, i, r: (i, j))
    dbs = _mm(f"{tag}_db", u, gs, mode="tn", grid=(2, nj, t // tm), kaxis=2,
              a_blk=(tm, cw), a_map=lambda r, j, i: (i, j),
              b_blk=(None, tm, sw), b_map=lambda r, j, i: (r, i, j),
              o_shape=(2, nj, cw, sw), o_blk=(None, None, cw, sw), o_map=lambda r, j, i: (r, j, 0, 0))
    dx = _from_segments(du + duskip)
    dlam = jnp.sum(dlam8, axis=1).reshape(2, g, p)
    small = dict(dlb_re=dlam[0], dlb_im=dlam[1],
                 dbb_re=_blockdiag_in_grad(dbs[0]), dbb_im=_blockdiag_in_grad(dbs[1]),
                 dc_re=_blockdiag_out_grad(dcs[0]), dc_im=-_blockdiag_out_grad(dcs[1]),
                 dd=dd.reshape(g, S5_GROUP))
    return dx, g_wout, small


def _pack(pieces):
    rows = []
    for p in pieces:
        flat = p.reshape(-1).astype(F32)
        n = flat.shape[0]
        rows.append(jnp.pad(flat, (0, -n % V7X_LANES)).reshape(-1, V7X_LANES))
    buf = jnp.concatenate(rows, axis=0)
    return jnp.pad(buf, ((0, -buf.shape[0] % V7X_SUBLANES), (0, 0)))


def _unpack(buf, shapes):
    out, row = [], 0
    for s in shapes:
        n = math.prod(s)
        nr = -(-n // V7X_LANES)
        out.append(buf[row:row + nr].reshape(-1)[:n].reshape(s))
        row += nr
    return out


def kernel(x, ffn1_w_in, ffn1_w_out, ln1_g, ln1_b, lnm_g, lnm_b, ffn2_w_in, ffn2_w_out, ln2_g, ln2_b, fox_w_in, fox_b_f, fox_w_o, s5_a_re, s5_a_im, s5_log_dt, s5_b_re, s5_b_im, s5_c_re, s5_c_im, s5_d, s5_w_out, loss_target, m_ffn1_w_in, m_ffn1_w_out, m_ln1_g, m_ln1_b, m_lnm_g, m_lnm_b, m_ffn2_w_in, m_ffn2_w_out, m_ln2_g, m_ln2_b, m_fox_w_in, m_fox_b_f, m_fox_w_o, m_s5_a_re, m_s5_a_im, m_s5_log_dt, m_s5_b_re, m_s5_b_im, m_s5_c_re, m_s5_c_im, m_s5_d, m_s5_w_out, v_ffn1_w_in, v_ffn1_w_out, v_ln1_g, v_ln1_b, v_lnm_g, v_lnm_b, v_ffn2_w_in, v_ffn2_w_out, v_ln2_g, v_ln2_b, v_fox_w_in, v_fox_b_f, v_fox_w_o, v_s5_a_re, v_s5_a_im, v_s5_log_dt, v_s5_b_re, v_s5_b_im, v_s5_c_re, v_s5_c_im, v_s5_d, v_s5_w_out):
    big_names = ["ffn1_w_in", "ffn1_w_out", "ffn2_w_in", "ffn2_w_out", "fox_w_in", "fox_w_o", "s5_w_out"]
    small_names = ["ln1_g", "ln1_b", "lnm_g", "lnm_b", "ln2_g", "ln2_b", "fox_b_f", "s5_a_re", "s5_a_im", "s5_log_dt",
                   "s5_b_re", "s5_b_im", "s5_c_re", "s5_c_im", "s5_d"]
    out_order = ["ffn1_w_in", "ffn1_w_out", "ln1_g", "ln1_b", "lnm_g", "lnm_b", "ffn2_w_in", "ffn2_w_out", "ln2_g",
                 "ln2_b", "fox_w_in", "fox_b_f", "fox_w_o", "s5_a_re", "s5_a_im", "s5_log_dt", "s5_b_re", "s5_b_im",
                 "s5_c_re", "s5_c_im", "s5_d", "s5_w_out"]
    env = dict(locals())
    w = {n: env[n] for n in out_order}
    mom = {n: env["m_" + n] for n in out_order}
    vel = {n: env["v_" + n] for n in out_order}

    depth, d = ln1_g.shape
    t = x.shape[1]
    alpha = (2.0 * depth) ** 0.25
    x0 = x.reshape(t, d)
    tgt = loss_target.reshape(t, d)

    tix = {n: k for k, n in enumerate(big_names)}
    groups = []
    for i in range(depth):
        j = i // 2
        groups.append([(tix["ffn1_w_in"], i), (tix["ffn1_w_out"], i)])
        mixer = [(tix["fox_w_in"], j), (tix["fox_w_o"], j)] if i % 2 == 0 else [(tix["s5_w_out"], j)]
        groups.append(mixer + [(tix["ffn2_w_in"], i), (tix["ffn2_w_out"], i)])
    sems, bufs = _gather_start("gather_start", [_cast_place(f"cast_{n}", w[n]) for n in big_names], groups)
    full, rows3 = {}, {}

    def arrive(gi, after):
        nonlocal bufs
        bufs = _gather_wait(f"gather_wait_{gi}", bufs, sems[gi][0], sems[gi][1], after, groups[gi])
        full.update(zip(big_names, bufs))
        rows3.update({n: _rows_view(full[n]) for n in ("ffn1_w_out", "ffn2_w_out", "fox_w_o")})

    nh = fox_b_f.shape[1]
    fox_cols = 3 * d + nh
    fox_pad = -(-fox_cols // (5 * V7X_LANES)) * (5 * V7X_LANES)

    def fox_wpad(j):
        wf = full["fox_w_in"][j].transpose(1, 0, 2).reshape(d, fox_cols)
        return jnp.pad(wf, ((0, 0), (0, fox_pad - fox_cols)))

    def s5_params(j):
        return (s5_a_re[j], s5_a_im[j], s5_log_dt[j], s5_b_re[j], s5_b_im[j], s5_c_re[j], s5_c_im[j], s5_d[j])

    saved = []
    h = x0
    for i in range(depth):
        j = i // 2
        arrive(2 * i, h)
        h, s1 = _ffn_fwd(f"l{i}_ffn1", alpha, h, full["ffn1_w_in"], rows3["ffn1_w_out"], i,
                         ln1_g[i:i + 1], ln1_b[i:i + 1])
        arrive(2 * i + 1, h)
        if i % 2 == 0:
            h, xhat_m, rstd_m, sm = _fox_fwd(f"l{i}_fox", alpha, h, fox_wpad(j), fox_b_f[j:j + 1], rows3["fox_w_o"], j,
                                             lnm_g[i:i + 1], lnm_b[i:i + 1])
        else:
            m, sm = _s5_fwd(f"l{i}_s5", h, s5_params(j), full["s5_w_out"], j)
            h, xhat_m, rstd_m = _ln_fwd(f"l{i}_lnm", alpha, h, m, 1.0, lnm_g[i:i + 1], lnm_b[i:i + 1])
        h, s2 = _ffn_fwd(f"l{i}_ffn2", alpha, h, full["ffn2_w_in"], rows3["ffn2_w_out"], i,
                         ln2_g[i:i + 1], ln2_b[i:i + 1])
        saved.append((s1, sm, (xhat_m, rstd_m), s2))
    loss_part = _loss_sum("loss", h, tgt) * (0.5 / d)

    fox_in_names = [f"fox_w_in_l{j}" for j in range(fox_w_in.shape[0])]
    gshape = {n: full[n].shape for n in big_names if n != "fox_w_in"}
    gshape.update({n: (1,) + full["fox_w_in"].shape[1:] for n in fox_in_names})
    gbuf = {n: lax.empty(s, BF16) for n, s in gshape.items()}
    rxbuf = {n: lax.empty((N_PARTS, s[0], s[2] // 2, s[3]), BF16) for n, s in gshape.items()}
    pending = []

    def scatter(tag, pairs):
        names = list(dict.fromkeys(n for n, _ in pairs))
        items = [(names.index(n), layer) for n, layer in pairs]
        sem, send, rx = _scatter_start(f"scatter_start_{tag}", [gbuf[n] for n in names], [rxbuf[n] for n in names], items)
        gbuf.update(zip(names, send))
        rxbuf.update(zip(names, rx))
        pending.append((tag, names, items, sem))

    gsmall = {n: [None] * w[n].shape[0] for n in small_names}
    s5_cot = [None] * s5_a_re.shape[0]
    terms = [(h, 1.0 / d), (tgt, -1.0 / d)]
    for i in reversed(range(depth)):
        j = i // 2
        s1, sm, (xhat_m, rstd_m), s2 = saved[i]
        terms, gbuf["ffn2_w_in"], g3, dg, db = _ffn_bwd(
            f"l{i}_ffn2", alpha, terms, s2, full["ffn2_w_in"], rows3["ffn2_w_out"], i, ln2_g[i:i + 1],
            gbuf["ffn2_w_in"], _rows_view(gbuf["ffn2_w_out"]))
        gbuf["ffn2_w_out"] = g3.reshape(gshape["ffn2_w_out"])
        gsmall["ln2_g"][i], gsmall["ln2_b"][i] = dg, db
        scatter(f"l{i}_ffn2", [("ffn2_w_in", i), ("ffn2_w_out", i)])
        dz, dg, db = _ln_bwd(f"l{i}_lnm_bwd", terms, xhat_m, rstd_m, lnm_g[i:i + 1])
        gsmall["lnm_g"][i], gsmall["lnm_b"][i] = dg, db
        if i % 2 == 0:
            dx, d_wpad, dbf, g3 = _fox_bwd(f"l{i}_fox", dz, sm, fox_wpad(j), fox_b_f[j:j + 1],
                                           rows3["fox_w_o"], j, _rows_view(gbuf["fox_w_o"]))
            gbuf["fox_w_o"] = g3.reshape(gshape["fox_w_o"])
            gbuf[fox_in_names[j]] = d_wpad[:, :fox_cols].reshape(d, N_CHIPS, -1).transpose(1, 0, 2)[None].astype(BF16)
            gsmall["fox_b_f"][j] = dbf
            scatter(f"l{i}_fox", [("fox_w_o", j), (fox_in_names[j], 0)])
        else:
            dx, gbuf["s5_w_out"], s5_cot[j] = _s5_bwd(f"l{i}_s5", dz, sm, s5_params(j), full["s5_w_out"], j,
                                                      gbuf["s5_w_out"])
            scatter(f"l{i}_s5", [("s5_w_out", j)])
        terms = [(dz, alpha), (dx, 1.0)]
        terms, gbuf["ffn1_w_in"], g3, dg, db = _ffn_bwd(
            f"l{i}_ffn1", alpha, terms, s1, full["ffn1_w_in"], rows3["ffn1_w_out"], i, ln1_g[i:i + 1],
            gbuf["ffn1_w_in"], _rows_view(gbuf["ffn1_w_out"]))
        gbuf["ffn1_w_out"] = g3.reshape(gshape["ffn1_w_out"])
        gsmall["ln1_g"][i], gsmall["ln1_b"][i] = dg, db
        scatter(f"l{i}_ffn1", [("ffn1_w_in", i), ("ffn1_w_out", i)])
    grad_x = _lincomb("grad_x", terms).reshape(x.shape)

    cot_names = ["dlb_re", "dlb_im", "dbb_re", "dbb_im", "dc_re", "dc_im", "dd"]
    ln_names = ["ln1_g", "ln1_b", "lnm_g", "lnm_b", "ln2_g", "ln2_b"]
    pieces = [loss_part] + [jnp.concatenate(gsmall[n], axis=0) for n in ln_names + ["fox_b_f"]]
    pieces += [jnp.stack([s5_cot[j][n] for j in range(len(s5_cot))]) for n in cot_names]
    shapes = [p.shape for p in pieces]
    mine = _pack(pieces)
    everyone = _gather_all("small_gather", mine).reshape(N_DEV, *mine.shape)
    summed = _unpack(_sum_leading("small_sum", everyone), shapes)
    loss = summed[0].reshape(())
    gs_final = dict(zip(ln_names + ["fox_b_f"], summed[1:8]))
    cot = dict(zip(cot_names, summed[8:]))
    prm_names = ["s5_a_re", "s5_a_im", "s5_log_dt", "s5_b_re", "s5_b_im"]
    _, disc_vjp = jax.vjp(jax.vmap(_s5_discretise), *[w[n] for n in prm_names])
    for n, gval in zip(prm_names, disc_vjp((cot["dlb_re"], cot["dlb_im"], cot["dbb_re"], cot["dbb_im"]))):
        gs_final[n] = gval
    gs_final["s5_c_re"], gs_final["s5_c_im"], gs_final["s5_d"] = cot["dc_re"], cot["dc_im"], cot["dd"]

    grads, deltas, new_m, new_v = {}, {}, {}, {}
    small_shapes = [w[n].shape for n in small_names]
    for n in small_names:
        grads[n] = gs_final[n].reshape(w[n].shape)
    packed = [_pack([src[n] for n in small_names]) for src in (w, grads, mom, vel)]
    for dst, buf in zip((deltas, new_m, new_v), _adamw("adamw_small", *packed)):
        for n, val in zip(small_names, _unpack(buf, small_shapes)):
            dst[n] = val

    for tag, names, items, sem in pending:
        send, rx = _scatter_wait(f"scatter_wait_{tag}", [gbuf[n] for n in names], [rxbuf[n] for n in names],
                                 sem[0], sem[1], items)
        gbuf.update(zip(names, send))
        rxbuf.update(zip(names, rx))
    half = {n: _chip_sum(f"grad_chip_sum_{n}", gbuf[n], rxbuf[n]) for n in gshape}
    half["fox_w_in"] = jnp.concatenate([half[n] for n in fox_in_names], axis=0)
    halves = [half[n] for n in big_names]
    theirs = _send_half("grad_send_half", halves)
    for n, mine_h, their_h in zip(big_names, halves, theirs):
        grads[n], deltas[n], new_m[n], new_v[n] = _adamw_join(f"adamw_{n}", w[n], mine_h, their_h, mom[n], vel[n])
    return (loss, grad_x, *[grads[n] for n in out_order], *[deltas[n] for n in out_order],
            *[new_m[n] for n in out_order], *[new_v[n] for n in out_order])
```

```python
import functools
import math

import jax
import jax.numpy as jnp
from jax import lax
from jax.experimental import pallas as pl
from jax.experimental.pallas import tpu as pltpu

F32 = jnp.float32
BF16 = jnp.bfloat16
LN_EPS = 1e-5
NEG_INF = -1e30
ADAM_LR = 0.001
ADAM_B1 = 0.9
ADAM_B2 = 0.999
ADAM_EPS = 1e-08
ADAM_WD = 0.01
ADAM_STEP = 10
S5_GROUP = 16
SCAN_SEGMENTS = 8
V7X_SUBLANES = 8
V7X_LANES = 128
VMEM_LIMIT = 56 * 1024 * 1024
N_CHIPS = 4
N_DEV = 8
MESH = pl.DeviceIdType.MESH
ANY = pl.BlockSpec(memory_space=pl.ANY)


def _cp(n_grid, kaxis=None):
    sem = tuple("arbitrary" if (kaxis is None or i == kaxis) else "parallel" for i in range(n_grid))
    return pltpu.CompilerParams(dimension_semantics=sem, vmem_limit_bytes=VMEM_LIMIT)


def _tile(n, pref, mult=V7X_SUBLANES):
    if n <= pref:
        return n
    for t in range(pref, 0, -1):
        if n % t == 0 and t % mult == 0:
            return t
    return n


_CONTRACT = {"nn": ((1,), (0,)), "nt": ((1,), (1,)), "tn": ((0,), (0,))}


def _mm(name, a, b, *, mode, grid, kaxis, a_blk, a_map, b_blk, b_map, o_shape, o_blk, o_map, o_dtype=F32, scale=None,
        into=None):
    nk = 1 if kaxis is None else grid[kaxis]
    assert kaxis is None or kaxis == len(grid) - 1
    dims = (_CONTRACT[mode], ((), ()))
    use_acc = nk > 1 and o_dtype != F32
    acc_shape = tuple(d for d in o_blk if d is not None)

    def body(a_ref, b_ref, *rest):
        o_ref, scratch = (rest[1], rest[2:]) if into is not None else (rest[0], rest[1:])
        p = lax.dot_general(a_ref[...].astype(BF16), b_ref[...].astype(BF16), dims, preferred_element_type=F32)
        if nk == 1:
            if scale is not None:
                p = p * scale
            o_ref[...] = p.astype(o_dtype)
            return
        acc = scratch[0] if use_acc else o_ref
        k = pl.program_id(kaxis)

        @pl.when(k == 0)
        def _():
            acc[...] = p

        @pl.when(k > 0)
        def _():
            acc[...] += p

        if use_acc or scale is not None:
            @pl.when(k == nk - 1)
            def _():
                r = acc[...]
                if scale is not None:
                    r = r * scale
                o_ref[...] = r.astype(o_dtype)

    in_specs = [pl.BlockSpec(a_blk, a_map), pl.BlockSpec(b_blk, b_map)]
    args = [a, b]
    if into is not None:
        assert into.shape == tuple(o_shape) and into.dtype == o_dtype
        in_specs.append(ANY)
        args.append(into)
    return pl.pallas_call(
        body, grid=grid, name=name, in_specs=in_specs,
        out_specs=pl.BlockSpec(o_blk, o_map),
        out_shape=jax.ShapeDtypeStruct(o_shape, o_dtype),
        input_output_aliases={2: 0} if into is not None else {},
        scratch_shapes=[pltpu.VMEM(acc_shape, F32)] if use_acc else [],
        compiler_params=_cp(len(grid), kaxis),
    )(*args)


def _mm_shards_nn(name, a, wall, layer, o_dtype):
    t, k = a.shape
    _, s, _, n = wall.shape
    tm = _tile(t, 512)
    return _mm(name, a, wall, mode="nn", grid=(s, t // tm), kaxis=None,
               a_blk=(tm, k), a_map=lambda j, i: (i, 0),
               b_blk=(None, None, k, n), b_map=lambda j, i: (layer, j, 0, 0),
               o_shape=(t, s * n), o_blk=(tm, n), o_map=lambda j, i: (i, j), o_dtype=o_dtype)


def _mm_shards_nt(name, g, wall, layer):
    t = g.shape[0]
    _, s, k, n = wall.shape
    tm = _tile(t, 512)
    return _mm(name, g, wall, mode="nt", grid=(t // tm, s), kaxis=1,
               a_blk=(tm, n), a_map=lambda i, kk: (i, kk),
               b_blk=(None, None, k, n), b_map=lambda i, kk: (layer, kk, 0, 0),
               o_shape=(t, k), o_blk=(tm, k), o_map=lambda i, kk: (i, 0))


def _mm_shards_tn(name, a, g, layer, into):
    t, k = a.shape
    _, s, _, n = into.shape
    tk = _tile(t, 512)
    return _mm(name, a, g, mode="tn", grid=(s, t // tk), kaxis=1,
               a_blk=(tk, k), a_map=lambda j, kk: (kk, 0),
               b_blk=(tk, n), b_map=lambda j, kk: (kk, j),
               o_shape=into.shape, o_blk=(None, None, k, n), o_map=lambda j, kk: (layer, j, 0, 0),
               o_dtype=into.dtype, into=into)


def _mm_nn(name, a, w, o_dtype=F32, tn=None):
    t, k = a.shape
    n = w.shape[1]
    tm = _tile(t, 512)
    tn = n if tn is None else tn
    return _mm(name, a, w, mode="nn", grid=(n // tn, t // tm), kaxis=None,
               a_blk=(tm, k), a_map=lambda j, i: (i, 0),
               b_blk=(k, tn), b_map=lambda j, i: (0, j),
               o_shape=(t, n), o_blk=(tm, tn), o_map=lambda j, i: (i, j), o_dtype=o_dtype)


def _mm_nt(name, g, w, layer=None, o_dtype=F32):
    t, k = g.shape
    n = w.shape[-2]
    tm = _tile(t, 512)
    b_blk, b_map = ((n, k), lambda i: (0, 0)) if layer is None else ((None, n, k), lambda i: (layer, 0, 0))
    return _mm(name, g, w, mode="nt", grid=(t // tm,), kaxis=None,
               a_blk=(tm, k), a_map=lambda i: (i, 0), b_blk=b_blk, b_map=b_map,
               o_shape=(t, n), o_blk=(tm, n), o_map=lambda i: (i, 0), o_dtype=o_dtype)


def _mm_tn(name, a, g, tm=None, tn=None, scale=None, layer=None, into=None):
    t, m = a.shape
    n = g.shape[1]
    tk = _tile(t, 512)
    tm = m if tm is None else tm
    tn = n if tn is None else tn
    if layer is None:
        o_shape, o_blk, o_map, o_dtype = (m, n), (tm, tn), lambda i, j, kk: (i, j), F32
    else:
        o_shape, o_blk, o_map, o_dtype = into.shape, (None, tm, tn), lambda i, j, kk: (layer, i, j), into.dtype
    return _mm(name, a, g, mode="tn", grid=(m // tm, n // tn, t // tk), kaxis=2,
               a_blk=(tk, tm), a_map=lambda i, j, kk: (kk, i),
               b_blk=(tk, tn), b_map=lambda i, j, kk: (kk, j),
               o_shape=o_shape, o_blk=o_blk, o_map=o_map, o_dtype=o_dtype, scale=scale, into=into)


def _sigmoid(x):
    return 1.0 / (1.0 + jnp.exp(-x))


def _rows_call(name, body, t, tm, ins, in_cols, outs, acc_outs=()):
    in_specs = []
    for x, c in zip(ins, in_cols):
        if x.shape[0] == 1:
            in_specs.append(pl.BlockSpec((1, c), lambda i: (0, 0)))
        else:
            in_specs.append(pl.BlockSpec((tm, c), lambda i: (i, 0)))
    out_specs = [pl.BlockSpec((tm, s.shape[1]), lambda i: (i, 0)) for s in outs]
    out_specs += [pl.BlockSpec((1, s.shape[1]), lambda i: (0, 0)) for s in acc_outs]
    return pl.pallas_call(
        body, grid=(t // tm,), name=name, in_specs=in_specs, out_specs=out_specs,
        out_shape=list(outs) + list(acc_outs), compiler_params=_cp(1),
    )(*ins)


def _ln_fwd(name, alpha, x, r, coef, g, b):
    t, d = x.shape
    tm = _tile(t, 256)

    def body(x_ref, r_ref, g_ref, b_ref, y_ref, xh_ref, rs_ref):
        z = alpha * x_ref[...] + coef * r_ref[...]
        mu = jnp.mean(z, axis=-1, keepdims=True)
        zc = z - mu
        var = jnp.mean(zc * zc, axis=-1, keepdims=True)
        rstd = lax.rsqrt(var + LN_EPS)
        xh = zc * rstd
        y_ref[...] = xh * g_ref[...] + b_ref[...]
        xh_ref[...] = xh
        rs_ref[...] = rstd

    sd = jax.ShapeDtypeStruct
    return _rows_call(name, body, t, tm, [x, r, g, b], [d, d, d, d],
                      [sd((t, d), F32), sd((t, d), F32), sd((t, 1), F32)])


def _ln_bwd(name, terms, xhat, rstd, g):
    t, d = xhat.shape
    tm = _tile(t, 256)
    n = len(terms)
    coefs = [c for _, c in terms]

    def body(*refs):
        t_refs = refs[:n]
        xh_ref, rs_ref, g_ref, dz_ref, dg_ref, db_ref = refs[n:]
        dy = coefs[0] * t_refs[0][...]
        for c, r in zip(coefs[1:], t_refs[1:]):
            dy = dy + c * r[...]
        xh = xh_ref[...]
        dxh = dy * g_ref[...]
        m1 = jnp.mean(dxh, axis=-1, keepdims=True)
        m2 = jnp.mean(dxh * xh, axis=-1, keepdims=True)
        dz_ref[...] = rs_ref[...] * (dxh - m1 - xh * m2)
        pg = jnp.sum(dy * xh, axis=0, keepdims=True)
        pb = jnp.sum(dy, axis=0, keepdims=True)
        i = pl.program_id(0)

        @pl.when(i == 0)
        def _():
            dg_ref[...] = pg
            db_ref[...] = pb

        @pl.when(i > 0)
        def _():
            dg_ref[...] += pg
            db_ref[...] += pb

    sd = jax.ShapeDtypeStruct
    arrs = [a for a, _ in terms] + [xhat, rstd, g]
    cols = [d] * n + [d, 1, d]
    return _rows_call(name, body, t, tm, arrs, cols, [sd((t, d), F32)], [sd((1, d), F32), sd((1, d), F32)])


def _lincomb(name, terms):
    t, d = terms[0][0].shape
    tm = _tile(t, 256)
    coefs = [c for _, c in terms]
    n = len(terms)

    def body(*refs):
        acc = coefs[0] * refs[0][...]
        for c, r in zip(coefs[1:], refs[1:n]):
            acc = acc + c * r[...]
        refs[n][...] = acc

    return _rows_call(name, body, t, tm, [a for a, _ in terms], [d] * n, [jax.ShapeDtypeStruct((t, d), F32)])[0]


def _loss_sum(name, y, tgt):
    t, d = y.shape
    tm = _tile(t, 256)

    def body(y_ref, t_ref, o_ref):
        e = y_ref[...] - t_ref[...]
        s = jnp.sum(jnp.sum(e * e, axis=1, keepdims=True), axis=0, keepdims=True)
        i = pl.program_id(0)

        @pl.when(i == 0)
        def _():
            o_ref[...] = s

        @pl.when(i > 0)
        def _():
            o_ref[...] += s

    return _rows_call(name, body, t, tm, [y, tgt], [d, d], [], [jax.ShapeDtypeStruct((1, 1), F32)])[0]


def _ffn_in(name, x, wall, layer):
    t, k = x.shape
    n = wall.shape[3]
    tm = _tile(t, 512)

    def body(x_ref, wg_ref, wu_ref, h_ref, a_ref):
        xb = x_ref[...].astype(BF16)
        g = lax.dot_general(xb, wg_ref[...], _NN, preferred_element_type=F32)
        u = lax.dot_general(xb, wu_ref[...], _NN, preferred_element_type=F32)
        h_ref[0] = g.astype(BF16)
        h_ref[1] = u.astype(BF16)
        a_ref[...] = (g * _sigmoid(g) * u).astype(BF16)

    return pl.pallas_call(
        body, grid=(2, t // tm), name=name,
        in_specs=[pl.BlockSpec((tm, k), lambda j, i: (i, 0)),
                  pl.BlockSpec((None, None, k, n), lambda j, i: (layer, j, 0, 0)),
                  pl.BlockSpec((None, None, k, n), lambda j, i: (layer, 2 + j, 0, 0))],
        out_specs=[pl.BlockSpec((2, tm, n), lambda j, i: (0, i, j)), pl.BlockSpec((tm, n), lambda j, i: (i, j))],
        out_shape=[jax.ShapeDtypeStruct((2, t, 2 * n), BF16), jax.ShapeDtypeStruct((t, 2 * n), BF16)],
        compiler_params=_cp(2),
    )(x, wall, wall)


def _ffn_da(name, dz, w3, layer, h):
    t, k = dz.shape
    f = w3.shape[1]
    n = f // 2
    tm = _tile(t, 512)

    def body(dz_ref, w_ref, g_ref, u_ref, dh_ref):
        d = 0.5 * lax.dot_general(dz_ref[...].astype(BF16), w_ref[...], _NT, preferred_element_type=F32)
        g = g_ref[...].astype(F32)
        u = u_ref[...].astype(F32)
        sg = _sigmoid(g)
        dh_ref[0] = (d * u * sg * (1.0 + g * (1.0 - sg))).astype(BF16)
        dh_ref[1] = (d * g * sg).astype(BF16)

    return pl.pallas_call(
        body, grid=(2, t // tm), name=name,
        in_specs=[pl.BlockSpec((tm, k), lambda j, i: (i, 0)),
                  pl.BlockSpec((None, n, k), lambda j, i: (layer, j, 0)),
                  pl.BlockSpec((None, tm, n), lambda j, i: (0, i, j)),
                  pl.BlockSpec((None, tm, n), lambda j, i: (1, i, j))],
        out_specs=pl.BlockSpec((2, tm, n), lambda j, i: (0, i, j)),
        out_shape=jax.ShapeDtypeStruct((2, t, f), BF16),
        compiler_params=_cp(2),
    )(dz, w3, h, h)


def _mm_ln(name, alpha, coef, a, w3, layer, x, g, b):
    t, k = a.shape
    d = w3.shape[2]
    tm = _tile(t, 512)

    def body(a_ref, w_ref, x_ref, g_ref, b_ref, y_ref, xh_ref, rs_ref):
        f = lax.dot_general(a_ref[...].astype(BF16), w_ref[...], _NN, preferred_element_type=F32)
        z = alpha * x_ref[...] + coef * f
        mu = jnp.mean(z, axis=-1, keepdims=True)
        zc = z - mu
        var = jnp.mean(zc * zc, axis=-1, keepdims=True)
        rstd = lax.rsqrt(var + LN_EPS)
        xh = zc * rstd
        y_ref[...] = xh * g_ref[...] + b_ref[...]
        xh_ref[...] = xh
        rs_ref[...] = rstd

    row = lambda c: pl.BlockSpec((tm, c), lambda i: (i, 0))
    vec = pl.BlockSpec((1, d), lambda i: (0, 0))
    sd = jax.ShapeDtypeStruct
    return pl.pallas_call(
        body, grid=(t // tm,), name=name,
        in_specs=[row(k), pl.BlockSpec((None, k, d), lambda i: (layer, 0, 0)), row(d), vec, vec],
        out_specs=[row(d), row(d), row(1)],
        out_shape=[sd((t, d), F32), sd((t, d), F32), sd((t, 1), F32)],
        compiler_params=_cp(1),
    )(a, w3, x, g, b)


_GELU_C = math.sqrt(2.0 / math.pi)


def _s5_act_fwd(name, ych, u, dvec):
    t, d = u.shape
    tm = _tile(t, 256)

    def body(y_ref, u_ref, d_ref, p_ref, a_ref):
        y = y_ref[...] + d_ref[...] * u_ref[...]
        p_ref[...] = y
        a_ref[...] = (0.5 * y * (1.0 + jnp.tanh(_GELU_C * (y + 0.044715 * y * y * y)))).astype(BF16)

    sd = jax.ShapeDtypeStruct
    return _rows_call(name, body, t, tm, [ych, u, dvec], [d, d, d], [sd((t, d), F32), sd((t, d), BF16)])


def _s5_act_bwd(name, dact, ypre, u, dvec):
    t, d = u.shape
    tm = _tile(t, 256)

    def body(da_ref, y_ref, u_ref, d_ref, dy_ref, ds_ref, dd_ref):
        y = y_ref[...]
        th = jnp.tanh(_GELU_C * (y + 0.044715 * y * y * y))
        dg = 0.5 * (1.0 + th) + 0.5 * y * (1.0 - th * th) * _GELU_C * (1.0 + 3.0 * 0.044715 * y * y)
        dy = da_ref[...] * dg
        dy_ref[...] = dy
        ds_ref[...] = dy * d_ref[...]
        pd = jnp.sum(dy * u_ref[...], axis=0, keepdims=True)
        i = pl.program_id(0)

        @pl.when(i == 0)
        def _():
            dd_ref[...] = pd

        @pl.when(i > 0)
        def _():
            dd_ref[...] += pd

    sd = jax.ShapeDtypeStruct
    return _rows_call(name, body, t, tm, [dact, ypre, u, dvec], [d, d, d, d],
                      [sd((t, d), F32), sd((t, d), F32)], [sd((1, d), F32)])


def _glu_fwd(name, vg):
    t, d2 = vg.shape
    d = d2 // 2
    tm = _tile(t, 256)

    def body(vg_ref, m_ref):
        m_ref[...] = vg_ref[:, :d] * _sigmoid(vg_ref[:, d:])

    return _rows_call(name, body, t, tm, [vg], [d2], [jax.ShapeDtypeStruct((t, d), F32)])[0]


def _glu_bwd(name, dm, vg):
    t, d2 = vg.shape
    d = d2 // 2
    tm = _tile(t, 256)

    def body(dm_ref, vg_ref, o_ref):
        sg = _sigmoid(vg_ref[:, d:])
        g = dm_ref[...]
        o_ref[:, :d] = (g * sg).astype(BF16)
        o_ref[:, d:] = (g * vg_ref[:, :d] * sg * (1.0 - sg)).astype(BF16)

    return _rows_call(name, body, t, tm, [dm, vg], [d, d2], [jax.ShapeDtypeStruct((t, d2), BF16)])[0]


def _adamw(name, w, g, m, v):
    r, c = w.shape
    tr = _tile(r, max(V7X_SUBLANES, (1 << 20) // (4 * c) // V7X_SUBLANES * V7X_SUBLANES))

    def body(w_ref, g_ref, m_ref, v_ref, d_ref, nm_ref, nv_ref):
        gg = g_ref[...]
        nm = ADAM_B1 * m_ref[...] + (1.0 - ADAM_B1) * gg
        nv = ADAM_B2 * v_ref[...] + (1.0 - ADAM_B2) * (gg * gg)
        m_hat = nm / (1.0 - ADAM_B1 ** ADAM_STEP)
        v_hat = nv / (1.0 - ADAM_B2 ** ADAM_STEP)
        d_ref[...] = -ADAM_LR * (m_hat / (jnp.sqrt(v_hat) + ADAM_EPS) + ADAM_WD * w_ref[...])
        nm_ref[...] = nm
        nv_ref[...] = nv

    sd = jax.ShapeDtypeStruct((r, c), F32)
    return _rows_call(name, body, r, tr, [w, g, m, v], [c] * 4, [sd, sd, sd])


def _my_shard():
    return 2 * lax.axis_index("x") + lax.axis_index("y")


def _my_core():
    return lax.axis_index("c")


def _adamw_join(name, w, mine, theirs, m, v):
    nl, r, c = w.shape
    h = r // 2
    tr = _tile(h, max(V7X_SUBLANES, (1 << 19) // (4 * c) // V7X_SUBLANES * V7X_SUBLANES))
    nb = h // tr

    def body(w_ref, a_ref, b_ref, m_ref, v_ref, g_ref, d_ref, nm_ref, nv_ref):
        gg = jnp.where(pl.program_id(1) == _my_core(), a_ref[...], b_ref[...])
        nm = ADAM_B1 * m_ref[...] + (1.0 - ADAM_B1) * gg
        nv = ADAM_B2 * v_ref[...] + (1.0 - ADAM_B2) * (gg * gg)
        m_hat = nm / (1.0 - ADAM_B1 ** ADAM_STEP)
        v_hat = nv / (1.0 - ADAM_B2 ** ADAM_STEP)
        g_ref[...] = gg
        d_ref[...] = -ADAM_LR * (m_hat / (jnp.sqrt(v_hat) + ADAM_EPS) + ADAM_WD * w_ref[...])
        nm_ref[...] = nm
        nv_ref[...] = nv

    full = pl.BlockSpec((None, tr, c), lambda l, hf, i: (l, hf * nb + i, 0))
    sd = jax.ShapeDtypeStruct((nl, r, c), F32)
    return pl.pallas_call(
        body, name=name, grid=(nl, 2, nb),
        in_specs=[full,
                  pl.BlockSpec((None, tr, c), lambda l, hf, i: (l, jnp.where(hf == _my_core(), i, 0), 0)),
                  pl.BlockSpec((None, tr, c), lambda l, hf, i: (l, jnp.where(hf == _my_core(), 0, i), 0)),
                  full, full],
        out_specs=[full, full, full, full],
        out_shape=[sd, sd, sd, sd],
        compiler_params=_cp(3),
    )(w, mine, theirs, m, v)


def _sum_leading(name, a):
    n, r, c = a.shape
    tr = _tile(r, 512)

    def body(a_ref, o_ref):
        acc = a_ref[0]
        for k in range(1, n):
            acc = acc + a_ref[k]
        o_ref[...] = acc

    return pl.pallas_call(
        body, grid=(r // tr,), name=name,
        in_specs=[pl.BlockSpec((n, tr, c), lambda i: (0, i, 0))],
        out_specs=pl.BlockSpec((tr, c), lambda i: (i, 0)),
        out_shape=jax.ShapeDtypeStruct((r, c), F32), compiler_params=_cp(1),
    )(a)


def _split3(x):
    hi = x.astype(BF16)
    r1 = x - hi.astype(F32)
    mid = r1.astype(BF16)
    lo = (r1 - mid.astype(F32)).astype(BF16)
    return hi, mid, lo


def _tri_sum(tri, x):
    dims = (((1,), (0,)), ((), ()))
    hi, mid, lo = _split3(x)
    out = lax.dot_general(tri, lo, dims, preferred_element_type=F32)
    out = out + lax.dot_general(tri, mid, dims, preferred_element_type=F32)
    return out + lax.dot_general(tri, hi, dims, preferred_element_type=F32)


def _fox_cumsum(name, fl, bf):
    t, h = fl.shape
    tb = _tile(t, 512)

    def body(fl_ref, bf_ref, c_ref, carry):
        i = pl.program_id(0)

        @pl.when(i == 0)
        def _():
            carry[...] = jnp.zeros_like(carry)

        x = fl_ref[...] + bf_ref[...]
        lf = jnp.minimum(x, 0.0) - jnp.log(1.0 + jnp.exp(-jnp.abs(x)))
        row = lax.broadcasted_iota(jnp.int32, (tb, tb), 0)
        col = lax.broadcasted_iota(jnp.int32, (tb, tb), 1)
        tri = jnp.where(row >= col, 1.0, 0.0).astype(BF16)
        c_ref[...] = _tri_sum(tri, lf) + carry[...]
        carry[...] += jnp.sum(lf, axis=0, keepdims=True)

    return pl.pallas_call(
        body, grid=(t // tb,), name=name,
        in_specs=[pl.BlockSpec((tb, h), lambda i: (i, 0)), pl.BlockSpec((1, h), lambda i: (0, 0))],
        out_specs=pl.BlockSpec((tb, h), lambda i: (i, 0)),
        out_shape=jax.ShapeDtypeStruct((t, h), F32),
        scratch_shapes=[pltpu.VMEM((1, h), F32)], compiler_params=_cp(1),
    )(fl, bf)


def _fox_cumsum_bwd(name, dcum, fl, bf):
    t, h = fl.shape
    tb = _tile(t, 512)
    nb = t // tb

    def body(dc_ref, fl_ref, bf_ref, df_ref, db_ref, carry):
        i = pl.program_id(0)

        @pl.when(i == 0)
        def _():
            carry[...] = jnp.zeros_like(carry)

        dc = dc_ref[...]
        row = lax.broadcasted_iota(jnp.int32, (tb, tb), 0)
        col = lax.broadcasted_iota(jnp.int32, (tb, tb), 1)
        tri = jnp.where(row <= col, 1.0, 0.0).astype(BF16)
        dlf = _tri_sum(tri, dc) + carry[...]
        carry[...] += jnp.sum(dc, axis=0, keepdims=True)
        x = fl_ref[...] + bf_ref[...]
        df = dlf / (1.0 + jnp.exp(x))
        df_ref[...] = df
        pb = jnp.sum(df, axis=0, keepdims=True)

        @pl.when(i == 0)
        def _():
            db_ref[...] = pb

        @pl.when(i > 0)
        def _():
            db_ref[...] += pb

    rev = lambda i: (nb - 1 - i, 0)
    return pl.pallas_call(
        body, grid=(nb,), name=name,
        in_specs=[pl.BlockSpec((tb, h), rev), pl.BlockSpec((tb, h), rev), pl.BlockSpec((1, h), lambda i: (0, 0))],
        out_specs=[pl.BlockSpec((tb, h), rev), pl.BlockSpec((1, h), lambda i: (0, 0))],
        out_shape=[jax.ShapeDtypeStruct((t, h), F32), jax.ShapeDtypeStruct((1, h), F32)],
        scratch_shapes=[pltpu.VMEM((1, h), F32)], compiler_params=_cp(1),
    )(dcum, fl, bf)


_NT = (((1,), (1,)), ((), ()))
_TN = (((0,), (0,)), ((), ()))
_NN = (((1,), (0,)), ((), ()))


def _causal_mask(s, tb):
    row = lax.broadcasted_iota(jnp.int32, (tb, tb), 0)
    col = lax.broadcasted_iota(jnp.int32, (tb, tb), 1)
    return jnp.where(col <= row, s, NEG_INF)


def _first_head_lanes(hd):
    return lax.broadcasted_iota(jnp.int32, (1, 2 * hd), 1) < hd


def _attn_fwd(name, qkv, ccol, crow, nh):
    nb, tb, d3 = qkv.shape
    d = d3 // 3
    hd = d // nh
    lanes = 2 * hd
    assert lanes == V7X_LANES
    scale = 1.0 / math.sqrt(hd)

    def body(q_ref, k_ref, v_ref, cc_ref, cr_ref, o_ref, lse_ref):
        i = pl.program_id(1)
        first = _first_head_lanes(hd)
        q = q_ref[...] * scale
        res = []
        for hh in (0, 1):
            qh = jnp.where(first if hh == 0 else jnp.logical_not(first), q, jnp.zeros_like(q))
            cc = cc_ref[:, hh:hh + 1]

            def step(j, carry, diagonal=False, qh=qh, cc=cc, hh=hh):
                m, l, acc = carry
                s = lax.dot_general(qh, k_ref[j], _NT, preferred_element_type=F32) + cc - cr_ref[j][hh:hh + 1, :]
                if diagonal:
                    s = _causal_mask(s, tb)
                m_new = jnp.maximum(m, jnp.max(s, axis=1, keepdims=True))
                p = jnp.exp(s - m_new)
                a = jnp.exp(m - m_new)
                l = a * l + jnp.sum(p, axis=1, keepdims=True)
                acc = a * acc + lax.dot_general(p.astype(BF16), v_ref[j], _NN, preferred_element_type=F32)
                return m_new, l, acc

            init = (jnp.full((tb, 1), NEG_INF, F32), jnp.zeros((tb, 1), F32), jnp.zeros((tb, lanes), F32))
            m, l, acc = step(i, lax.fori_loop(0, i, step, init), diagonal=True)
            res.append((acc / l, m + jnp.log(l)))
        o_ref[...] = jnp.where(first, res[0][0], res[1][0])
        lse_ref[:, 0:1] = res[0][1]
        lse_ref[:, 1:2] = res[1][1]

    kb, vb = d // lanes, 2 * d // lanes
    return pl.pallas_call(
        body, grid=(nh // 2, nb), name=name,
        in_specs=[pl.BlockSpec((None, tb, lanes), lambda h, i: (i, 0, h)),
                  pl.BlockSpec((nb, tb, lanes), lambda h, i: (0, 0, kb + h)),
                  pl.BlockSpec((nb, tb, lanes), lambda h, i: (0, 0, vb + h)),
                  pl.BlockSpec((None, None, tb, 2), lambda h, i: (h, i, 0, 0)),
                  pl.BlockSpec((None, nb, 2, tb), lambda h, i: (h, 0, 0, 0))],
        out_specs=[pl.BlockSpec((None, tb, lanes), lambda h, i: (i, 0, h)),
                   pl.BlockSpec((None, None, tb, 2), lambda h, i: (h, i, 0, 0))],
        out_shape=[jax.ShapeDtypeStruct((nb, tb, d), F32), jax.ShapeDtypeStruct((nh // 2, nb, tb, 2), F32)],
        compiler_params=_cp(2),
    )(qkv, qkv, qkv, ccol, crow)


def _attn_bwd(name, qkv, ccol, crow, o, lse, do, nh):
    nb, tb, d3 = qkv.shape
    d = d3 // 3
    hd = d // nh
    lanes = 2 * hd
    scale = 1.0 / math.sqrt(hd)

    def body(q_ref, k_ref, v_ref, cc_ref, cr_ref, o_ref, lse_ref, do_ref, dq_ref, dk_ref, dv_ref, dr_ref, dc_ref, dq_acc):
        j = pl.program_id(1)

        @pl.when(j == 0)
        def _():
            dq_acc[...] = jnp.zeros_like(dq_acc)
            dr_ref[...] = jnp.zeros_like(dr_ref)

        first = _first_head_lanes(hd)
        kj = k_ref[...]
        vj = v_ref[...]
        dk = jnp.zeros((tb, lanes), F32)
        dv = jnp.zeros((tb, lanes), F32)
        for hh in (0, 1):
            mine = first if hh == 0 else jnp.logical_not(first)
            cr = cr_ref[hh:hh + 1, :]

            def step(i, carry, diagonal=False, mine=mine, cr=cr, hh=hh):
                dk, dv, dc = carry
                qi = q_ref[i] * scale
                qh = jnp.where(mine, qi, jnp.zeros_like(qi))
                doh = jnp.where(mine, do_ref[i], 0.0)
                dob = doh.astype(BF16)
                di = jnp.sum(doh * o_ref[i], axis=1, keepdims=True)
                s = lax.dot_general(qh, kj, _NT, preferred_element_type=F32) + cc_ref[i][:, hh:hh + 1] - cr
                if diagonal:
                    s = _causal_mask(s, tb)
                p = jnp.exp(s - lse_ref[i][:, hh:hh + 1])
                dv = dv + lax.dot_general(p.astype(BF16), dob, _TN, preferred_element_type=F32)
                dp = lax.dot_general(dob, vj, _NT, preferred_element_type=F32)
                ds = p * (dp - di)
                dsb = ds.astype(BF16)
                dk = dk + lax.dot_general(dsb, qh, _TN, preferred_element_type=F32)
                dq = lax.dot_general(dsb, kj, _NN, preferred_element_type=F32) * scale
                dq_acc[i] += jnp.where(mine, dq, 0.0)
                dr_ref[i, :, hh:hh + 1] += jnp.sum(ds, axis=1, keepdims=True)
                dc = dc + jnp.sum(ds, axis=0, keepdims=True)
                return dk, dv, dc

            dk, dv, dc = lax.fori_loop(j + 1, nb, step, step(j, (dk, dv, jnp.zeros((1, tb), F32)), diagonal=True))
            dc_ref[hh:hh + 1, :] = dc
        dk_ref[...] = dk.astype(BF16)
        dv_ref[...] = dv.astype(BF16)

        @pl.when(j == nb - 1)
        def _():
            dq_ref[...] = dq_acc[...].astype(BF16)

    kb, vb = d // lanes, 2 * d // lanes
    whole = lambda c: pl.BlockSpec((nb, tb, lanes), lambda h, j: (0, 0, c + h))
    block = lambda c: pl.BlockSpec((None, tb, lanes), lambda h, j: (j, 0, c + h))
    cols = pl.BlockSpec((None, nb, tb, 2), lambda h, j: (h, 0, 0, 0))
    rows = pl.BlockSpec((None, None, 2, tb), lambda h, j: (h, j, 0, 0))
    sd = jax.ShapeDtypeStruct
    return pl.pallas_call(
        body, grid=(nh // 2, nb), name=name,
        in_specs=[whole(0), block(kb), block(vb), cols, rows, whole(0), cols, whole(0)],
        out_specs=[whole(0), block(0), block(0), cols, rows],
        out_shape=[sd((nb, tb, d), BF16), sd((nb, tb, d), BF16), sd((nb, tb, d), BF16),
                   sd((nh // 2, nb, tb, 2), F32), sd((nh // 2, nb, 2, tb), F32)],
        scratch_shapes=[pltpu.VMEM((nb, tb, lanes), F32)],
        compiler_params=_cp(2),
    )(qkv, qkv, qkv, ccol, crow, o, lse, do)


def _cmul(ar, ai, br, bi):
    return ar * br - ai * bi, ar * bi + ai * br


def _s5_scan(name, lam, xin, hs=None):
    reverse = hs is not None
    _, seg, ns, w = xin.shape
    assert ns == SCAN_SEGMENTS
    wb = min(w, 2 * V7X_LANES)
    nsq = seg.bit_length() - 1
    assert (1 << nsq) == seg

    def body(*refs):
        if reverse:
            lam_ref, x_ref, h_ref, o_ref, dl_ref = refs
        else:
            lam_ref, x_ref, o_ref = refs
        lr = jnp.broadcast_to(lam_ref[0], (ns, wb))
        li = jnp.broadcast_to(lam_ref[1], (ns, wb))
        if reverse:
            li = -li
        zero = jnp.zeros((ns, wb), F32)
        at = (lambda n: seg - 1 - n) if reverse else (lambda n: n)

        def local(n, c):
            r = at(n)
            mr, mi = _cmul(lr, li, c[0], c[1])
            nr = mr + x_ref[0, r]
            ni = mi + x_ref[1, r]
            o_ref[0, r] = nr
            o_ref[1, r] = ni
            return nr, ni

        er, ei = lax.fori_loop(0, seg, local, (zero, zero))
        pr, pi = lr, li
        for _ in range(nsq):
            pr, pi = _cmul(pr, pi, pr, pi)
        sub = lax.broadcasted_iota(jnp.int32, (ns, wb), 0)

        def shifted(a, sh):
            if reverse:
                return jnp.where(sub < ns - sh, pltpu.roll(a, ns - sh, 0), 0.0)
            return jnp.where(sub >= sh, pltpu.roll(a, sh, 0), 0.0)

        xr, xi = er, ei
        sh = 1
        while sh < ns:
            tr, ti = _cmul(pr, pi, shifted(xr, sh), shifted(xi, sh))
            xr, xi = xr + tr, xi + ti
            pr, pi = _cmul(pr, pi, pr, pi)
            sh *= 2
        cr, ci = shifted(xr, 1), shifted(xi, 1)

        def fix(r, q):
            tr, ti = _cmul(q[0], q[1], cr, ci)
            gr = o_ref[0, r] + tr
            gi = o_ref[1, r] + ti
            o_ref[0, r] = gr
            o_ref[1, r] = gi
            return gr, gi

        if not reverse:
            def fixup(n, q):
                fix(n, q)
                return _cmul(q[0], q[1], lr, li)

            lax.fori_loop(0, seg, fixup, (lr, li))
            return

        def fixup_acc(n, c):
            qr, qi, ar, ai = c
            r = seg - 1 - n
            gr, gi = fix(r, (qr, qi))
            hr = h_ref[0, r - 1]
            hi = h_ref[1, r - 1]
            qr, qi = _cmul(qr, qi, lr, li)
            return qr, qi, ar + gr * hr + gi * hi, ai + gi * hr - gr * hi

        qr, qi, ar, ai = lax.fori_loop(0, seg - 1, fixup_acc, (lr, li, zero, zero))
        gr, gi = fix(0, (qr, qi))
        hr = jnp.where(sub >= 1, pltpu.roll(h_ref[0, seg - 1], 1, 0), 0.0)
        hi = jnp.where(sub >= 1, pltpu.roll(h_ref[1, seg - 1], 1, 0), 0.0)
        dl_ref[0] = ar + gr * hr + gi * hi
        dl_ref[1] = ai + gi * hr - gr * hi

    big = pl.BlockSpec((2, seg, ns, wb), lambda j: (0, 0, 0, j))
    lam_spec = pl.BlockSpec((2, 1, wb), lambda j: (0, 0, j))
    sd = jax.ShapeDtypeStruct
    if reverse:
        return pl.pallas_call(
            body, grid=(w // wb,), name=name, in_specs=[lam_spec, big, big],
            out_specs=[big, pl.BlockSpec((2, ns, wb), lambda j: (0, 0, j))],
            out_shape=[sd(xin.shape, F32), sd((2, ns, w), F32)], compiler_params=_cp(1),
        )(lam, xin, hs)
    return pl.pallas_call(
        body, grid=(w // wb,), name=name, in_specs=[lam_spec, big], out_specs=big,
        out_shape=sd(xin.shape, F32), compiler_params=_cp(1),
    )(lam, xin)


def _place():
    x, y, c = lax.axis_index("x"), lax.axis_index("y"), lax.axis_index("c")
    chips = [(1 - x, y), (x, 1 - y), (1 - x, 1 - y)]
    return x, y, c, chips


def _comm_params():
    return pltpu.CompilerParams(vmem_limit_bytes=VMEM_LIMIT)


def _cast_place(name, w):
    nl, r, c = w.shape
    tr = _tile(r, max(16, (1 << 20) // (4 * c) // 16 * 16), 16)

    def body(w_ref, o_ref):
        o_ref[...] = w_ref[...].astype(BF16)

    return pl.pallas_call(
        body, name=name, grid=(nl, r // tr),
        in_specs=[pl.BlockSpec((None, tr, c), lambda l, i: (l, i, 0))],
        out_specs=pl.BlockSpec((None, None, tr, c), lambda l, i: (l, _my_shard(), i, 0)),
        out_shape=jax.ShapeDtypeStruct((nl, N_CHIPS, r, c), BF16),
        compiler_params=_cp(2),
    )(w)


def _gather_shards(name, bufs):
    n = len(bufs)

    def body(*refs):
        outs = refs[n:2 * n]
        send_sems, recv_sems = refs[2 * n:]
        x, y, c, chips = _place()
        my = 2 * x + y
        sibling = (x, y, 1 - c)

        def part(t, shard, half):
            h = bufs[t].shape[2] // 2
            return outs[t].at[:, shard, pl.ds(half * h, h)]

        def copy(t, k, ref, to):
            return pltpu.make_async_remote_copy(src_ref=ref, dst_ref=ref, send_sem=send_sems.at[t, k],
                                                recv_sem=recv_sems.at[t, k], device_id=to, device_id_type=MESH)

        sent = []
        for t in range(n):
            for k, chip in enumerate(chips):
                sent.append(copy(t, k, part(t, my, c), (*chip, c)))
                sent[-1].start()
        for k, chip in enumerate(chips):
            shard = 2 * chip[0] + chip[1]
            for t in range(n):
                copy(t, k, part(t, shard, c), (*chip, c)).wait_recv()
                sent.append(copy(t, 3 + k, part(t, shard, c), sibling))
                sent[-1].start()
        for k, chip in enumerate(chips):
            shard = 2 * chip[0] + chip[1]
            for t in range(n):
                copy(t, 3 + k, part(t, shard, 1 - c), sibling).wait_recv()
        for cp in sent:
            cp.wait_send()

    return pl.pallas_call(
        body, name=name, in_specs=[ANY] * n, out_specs=[ANY] * n,
        out_shape=[jax.ShapeDtypeStruct(b.shape, b.dtype) for b in bufs],
        input_output_aliases={t: t for t in range(n)},
        scratch_shapes=[pltpu.SemaphoreType.DMA((n, 6)), pltpu.SemaphoreType.DMA((n, 6))],
        compiler_params=_comm_params(),
    )(*bufs)


HBM_SPEC = pl.BlockSpec(memory_space=pltpu.HBM)
SEM_SPEC = pl.BlockSpec(memory_space=pltpu.SEMAPHORE)


def _split_params():
    return pltpu.CompilerParams(has_side_effects=pltpu.SideEffectType.DATAFLOW_SIDE_EFFECTING)


def _gather_start(name, bufs, groups):
    n, ng = len(bufs), len(groups)

    def body(*refs):
        sems = refs[n:n + 2 * ng]
        outs = refs[n + 2 * ng:]
        x, y, c, chips = _place()
        my = 2 * x + y
        for gi, group in enumerate(groups):
            for idx, (t, layer) in enumerate(group):
                block = outs[t].at[layer, my]
                for k, chip in enumerate(chips):
                    pltpu.make_async_remote_copy(
                        src_ref=block, dst_ref=block, send_sem=sems[2 * gi].at[3 * idx + k],
                        recv_sem=sems[2 * gi + 1].at[3 * idx + k], device_id=(*chip, c), device_id_type=MESH).start()

    sem_shapes = []
    for group in groups:
        sem_shapes += [pltpu.SemaphoreType.DMA((3 * len(group),))] * 2
    res = pl.pallas_call(
        body, name=name, in_specs=[HBM_SPEC] * n,
        out_specs=[SEM_SPEC] * (2 * ng) + [HBM_SPEC] * n,
        out_shape=sem_shapes + [pltpu.HBM(b.shape, b.dtype) for b in bufs],
        input_output_aliases={t: 2 * ng + t for t in range(n)},
        compiler_params=_split_params(),
    )(*[pltpu.with_memory_space_constraint(b, pltpu.HBM) for b in bufs])
    sems = [(res[2 * gi], res[2 * gi + 1]) for gi in range(ng)]
    return sems, list(res[2 * ng:])


def _gather_wait(name, bufs, send_sems, recv_sems, after, group):
    n = len(bufs)

    def body(*refs):
        ss, rs = refs[n], refs[n + 1]
        outs = refs[n + 3:]
        x, y, c, chips = _place()
        my = 2 * x + y
        for idx, (t, layer) in enumerate(group):
            for k, chip in enumerate(chips):
                cp = pltpu.make_async_remote_copy(
                    src_ref=outs[t].at[layer, my], dst_ref=outs[t].at[layer, 2 * chip[0] + chip[1]],
                    send_sem=ss.at[3 * idx + k], recv_sem=rs.at[3 * idx + k], device_id=(*chip, c), device_id_type=MESH)
                cp.wait_send()
                cp.wait_recv()

    return list(pl.pallas_call(
        body, name=name, in_specs=[HBM_SPEC] * n + [SEM_SPEC, SEM_SPEC, ANY],
        out_specs=[HBM_SPEC] * n,
        out_shape=[pltpu.HBM(b.shape, b.dtype) for b in bufs],
        input_output_aliases={t: t for t in range(n)},
        compiler_params=_split_params(),
    )(*bufs, send_sems, recv_sems, after))


N_PARTS = 7


def _scatter_items(send, rx, items, c, chips, x, y):
    my = 2 * x + y
    out = []
    for i, (k, layer) in enumerate(items):
        h = send[k].shape[2] // 2
        for kk, chip in enumerate(chips):
            shard = 2 * chip[0] + chip[1]
            for hf in (0, 1):
                out.append((send[k].at[layer, shard, pl.ds(hf * h, h)], rx[k].at[2 * kk + c, layer],
                            N_PARTS * i + 2 * kk + hf, N_PARTS * i + 2 * kk + c, (*chip, hf)))
        out.append((send[k].at[layer, my, pl.ds((1 - c) * h, h)], rx[k].at[N_PARTS - 1, layer],
                    N_PARTS * i + N_PARTS - 1, N_PARTS * i + N_PARTS - 1, (x, y, 1 - c)))
    return out


def _scatter_start(name, send, rx, items):
    n = len(send)
    m = N_PARTS * len(items)

    def body(*refs):
        ssem, rsem = refs[2 * n], refs[2 * n + 1]
        s_out, r_out = refs[2 * n + 2:3 * n + 2], refs[3 * n + 2:4 * n + 2]
        x, y, c, chips = _place()
        for src, dst, si, ri, to in _scatter_items(s_out, r_out, items, c, chips, x, y):
            pltpu.make_async_remote_copy(src_ref=src, dst_ref=dst, send_sem=ssem.at[si], recv_sem=rsem.at[ri],
                                         device_id=to, device_id_type=MESH).start()
        refs[4 * n + 2][...] = jnp.zeros((V7X_SUBLANES, V7X_LANES), F32)

    res = pl.pallas_call(
        body, name=name, in_specs=[HBM_SPEC] * (2 * n),
        out_specs=[SEM_SPEC, SEM_SPEC] + [HBM_SPEC] * (2 * n) + [pl.BlockSpec(memory_space=pltpu.VMEM)],
        out_shape=[pltpu.SemaphoreType.DMA((m,)), pltpu.SemaphoreType.DMA((m,))]
        + [pltpu.HBM(b.shape, b.dtype) for b in list(send) + list(rx)]
        + [jax.ShapeDtypeStruct((V7X_SUBLANES, V7X_LANES), F32)],
        input_output_aliases={t: 2 + t for t in range(2 * n)},
        compiler_params=_split_params(),
    )(*[pltpu.with_memory_space_constraint(b, pltpu.HBM) for b in list(send) + list(rx)])
    return (res[0], res[1]), list(res[2:2 + n]), list(res[2 + n:2 + 2 * n]), res[2 + 2 * n][0, 0]


def _scatter_wait(name, send, rx, ssem, rsem, after, items):
    n = len(send)

    def body(*refs):
        ss, rs = refs[2 * n], refs[2 * n + 1]
        s_out, r_out = refs[2 * n + 3:3 * n + 3], refs[3 * n + 3:]
        x, y, c, chips = _place()
        for i, (src, dst, si, ri, to) in enumerate(_scatter_items(s_out, r_out, items, c, chips, x, y)):
            arrival = i % N_PARTS
            landed = r_out[items[i // N_PARTS][0]].at[arrival, items[i // N_PARTS][1]]
            cp = pltpu.make_async_remote_copy(src_ref=src, dst_ref=landed, send_sem=ss.at[si],
                                              recv_sem=rs.at[N_PARTS * (i // N_PARTS) + arrival],
                                              device_id=to, device_id_type=MESH)
            cp.wait_send()
            cp.wait_recv()

    res = pl.pallas_call(
        body, name=name, in_specs=[HBM_SPEC] * (2 * n) + [SEM_SPEC, SEM_SPEC, ANY],
        out_specs=[HBM_SPEC] * (2 * n),
        out_shape=[pltpu.HBM(b.shape, b.dtype) for b in list(send) + list(rx)],
        input_output_aliases={t: t for t in range(2 * n)},
        compiler_params=_split_params(),
    )(*send, *rx, ssem, rsem, after)
    return list(res[:n]), list(res[n:])


def _chip_sum(name, g, rx):
    nl, _, r, c = g.shape
    h = r // 2
    tr = _tile(h, max(V7X_SUBLANES * 2, (1 << 19) // (2 * c) // 16 * 16), 16)
    nb = h // tr

    def body(g_ref, r_ref, o_ref):
        acc = g_ref[...].astype(F32)
        for k in range(N_PARTS):
            acc = acc + r_ref[k].astype(F32)
        o_ref[...] = acc

    return pl.pallas_call(
        body, name=name, grid=(nl, nb),
        in_specs=[pl.BlockSpec((None, None, tr, c), lambda l, i: (l, _my_shard(), _my_core() * nb + i, 0)),
                  pl.BlockSpec((N_PARTS, None, tr, c), lambda l, i: (0, l, i, 0))],
        out_specs=pl.BlockSpec((None, tr, c), lambda l, i: (l, i, 0)),
        out_shape=jax.ShapeDtypeStruct((nl, h, c), F32),
        compiler_params=_cp(2),
    )(g, rx)


def _send_half(name, fs):
    n = len(fs)

    def body(*refs):
        ins, outs = refs[:n], refs[n:2 * n]
        send_sems, recv_sems = refs[2 * n:]
        x, y, c, _ = _place()
        cps = []
        for t in range(n):
            cps.append(pltpu.make_async_remote_copy(
                src_ref=ins[t], dst_ref=outs[t], send_sem=send_sems.at[t], recv_sem=recv_sems.at[t],
                device_id=(x, y, 1 - c), device_id_type=MESH))
            cps[-1].start()
        for cp in cps:
            cp.wait()

    return pl.pallas_call(
        body, name=name, in_specs=[ANY] * n, out_specs=[ANY] * n,
        out_shape=[jax.ShapeDtypeStruct(f.shape, f.dtype) for f in fs],
        scratch_shapes=[pltpu.SemaphoreType.DMA((n,)), pltpu.SemaphoreType.DMA((n,))],
        compiler_params=_comm_params(),
    )(*fs)


def _gather_all(name, v):
    m_per = v.shape[0]

    def body(x_ref, out_ref, send_sems, recv_sems, local_sem):
        x, y, c, chips = _place()
        me, sibling = (x, y, c), (x, y, 1 - c)

        def rows(px, py, pc):
            return out_ref.at[pl.ds((4 * px + 2 * py + pc) * m_per, m_per), :]

        def copy(k, block, to, src=None):
            return pltpu.make_async_remote_copy(
                src_ref=rows(*block) if src is None else src, dst_ref=rows(*block), send_sem=send_sems.at[k],
                recv_sem=recv_sems.at[k], device_id=to, device_id_type=MESH)

        mine = pltpu.make_async_copy(x_ref, rows(*me), local_sem)
        mine.start()
        first = [copy(0, me, sibling, src=x_ref)]
        first += [copy(1 + j, me, (*chip, c), src=x_ref) for j, chip in enumerate(chips)]
        for cp in first:
            cp.start()
        passed = [copy(4 + j, (*chip, c), sibling) for j, chip in enumerate(chips)]
        for j, chip in enumerate(chips):
            copy(1 + j, (*chip, c), me).wait_recv()
            passed[j].start()
        copy(0, sibling, me).wait_recv()
        for j, chip in enumerate(chips):
            copy(4 + j, (*chip, 1 - c), me).wait_recv()
        for cp in first + passed:
            cp.wait_send()
        mine.wait()

    return pl.pallas_call(
        body, name=name, in_specs=[ANY], out_specs=ANY,
        out_shape=jax.ShapeDtypeStruct((N_DEV * m_per, v.shape[1]), v.dtype),
        scratch_shapes=[pltpu.SemaphoreType.DMA((7,)), pltpu.SemaphoreType.DMA((7,)), pltpu.SemaphoreType.DMA],
        compiler_params=_comm_params(),
    )(v)


def _rows_view(wall):
    nl, s, r, c = wall.shape
    return wall.reshape(nl, s * r, c)


def _ffn_fwd(tag, alpha, x, w_in, w_out3, layer, g, b):
    h, a = _ffn_in(f"{tag}_in", x, w_in, layer)
    y, xhat, rstd = _mm_ln(f"{tag}_out", alpha, 0.5, a, w_out3, layer, x, g, b)
    return y, (x, h, a, xhat, rstd)


def _ffn_bwd(tag, alpha, terms, saved, w_in, w_out3, layer, g, g_win, g_wout3):
    x, h, a, xhat, rstd = saved
    t = x.shape[0]
    _, s, k, n = w_in.shape
    tm = _tile(t, 512)
    dz, dg, db = _ln_bwd(f"{tag}_ln_bwd", terms, xhat, rstd, g)
    g_wout3 = _mm_tn(f"{tag}_dwout", a, dz, tm=n, scale=0.5, layer=layer, into=g_wout3)
    dh = _ffn_da(f"{tag}_da", dz, w_out3, layer, h)
    g_win = _mm(f"{tag}_dwin", x, dh, mode="tn", grid=(s, t // tm), kaxis=1,
                a_blk=(tm, k), a_map=lambda j, kk: (kk, 0),
                b_blk=(None, tm, n), b_map=lambda j, kk: (j // 2, kk, j % 2),
                o_shape=w_in.shape, o_blk=(None, None, k, n), o_map=lambda j, kk: (layer, j, 0, 0),
                o_dtype=g_win.dtype, into=g_win)
    dx = _mm(f"{tag}_dx", dh, w_in, mode="nt", grid=(t // tm, s), kaxis=1,
             a_blk=(None, tm, n), a_map=lambda i, kk: (kk // 2, i, kk % 2),
             b_blk=(None, None, k, n), b_map=lambda i, kk: (layer, kk, 0, 0),
             o_shape=(t, k), o_blk=(tm, k), o_map=lambda i, kk: (i, 0))
    return [(dz, alpha), (dx, 1.0)], g_win, g_wout3, dg, db


def _fox_fwd(tag, alpha, x, w_pad, bf, w_o3, layer, g, b):
    t, d = x.shape
    nh = bf.shape[1]
    tb = _tile(t, 512)
    nb = t // tb
    qkv = _mm_nn(f"{tag}_qkv", x, w_pad[:, :3 * d], o_dtype=BF16, tn=d).reshape(nb, tb, 3 * d)
    fl = _mm_nn(f"{tag}_gate", x, w_pad[:, 3 * d:])[:, :nh]
    cum = _fox_cumsum(f"{tag}_cum", fl, bf)
    ccol = cum.reshape(nb, tb, nh // 2, 2).transpose(2, 0, 1, 3)
    crow = cum.reshape(nb, tb, nh // 2, 2).transpose(2, 0, 3, 1)
    o, lse = _attn_fwd(f"{tag}_attn", qkv, ccol, crow, nh)
    o2 = o.reshape(t, d)
    y, xhat, rstd = _mm_ln(f"{tag}_oproj", alpha, 1.0, o2, w_o3, layer, x, g, b)
    return y, xhat, rstd, (x, qkv, ccol, crow, o, lse, fl)


def _fox_bwd(tag, dm, saved, w_pad, bf, w_o3, layer, g_wo3):
    x, qkv, ccol, crow, o, lse, fl = saved
    t, d = x.shape
    nh = bf.shape[1]
    nb, tb, _ = qkv.shape
    g_wo3 = _mm_tn(f"{tag}_dwo", o.reshape(t, d), dm, layer=layer, into=g_wo3)
    do = _mm_nt(f"{tag}_do", dm, w_o3, layer=layer).reshape(nb, tb, d)
    dq, dk, dv, drow, dcol = _attn_bwd(f"{tag}_attn_bwd", qkv, ccol, crow, o, lse, do, nh)
    dcum = drow.transpose(1, 2, 0, 3).reshape(t, nh) - dcol.transpose(1, 3, 0, 2).reshape(t, nh)
    dfl, dbf = _fox_cumsum_bwd(f"{tag}_cum_bwd", dcum, fl, bf)
    pad = w_pad.shape[1] - 3 * d - nh
    dproj = jnp.concatenate([dq.reshape(t, d), dk.reshape(t, d), dv.reshape(t, d),
                             dfl.astype(BF16), jnp.zeros((t, pad), BF16)], axis=1)
    d_wpad = _mm_tn(f"{tag}_dwin", x, dproj, tn=_tile(w_pad.shape[1], 640, V7X_LANES))
    dx = _mm_nt(f"{tag}_dx", dproj, w_pad)
    return dx, d_wpad, dbf, g_wo3


def _to_segments(a):
    t, d = a.shape
    return a.reshape(SCAN_SEGMENTS, t // SCAN_SEGMENTS, d).transpose(1, 0, 2).reshape(t, d)


def _from_segments(a):
    t, d = a.shape
    return a.reshape(t // SCAN_SEGMENTS, SCAN_SEGMENTS, d).transpose(1, 0, 2).reshape(t, d)


def _s5_discretise(a_re, a_im, log_dt, b_re, b_im):
    dt = jnp.exp(log_dt)[:, None]
    mag = jnp.exp(a_re * dt)
    ang = a_im * dt
    lb_re = mag * jnp.cos(ang)
    lb_im = mag * jnp.sin(ang)
    den = a_re * a_re + a_im * a_im
    nr = lb_re - 1.0
    ni = lb_im
    z_re = (nr * a_re + ni * a_im) / den
    z_im = (ni * a_re - nr * a_im) / den
    bb_re = z_re[..., None] * b_re - z_im[..., None] * b_im
    bb_im = z_re[..., None] * b_im + z_im[..., None] * b_re
    return lb_re, lb_im, bb_re, bb_im


S5_BLOCK_GROUPS = 8


def _blockdiag_in(bb):
    g, p, h = bb.shape
    e = jnp.eye(S5_BLOCK_GROUPS, dtype=bb.dtype)
    b4 = bb.reshape(g // S5_BLOCK_GROUPS, S5_BLOCK_GROUPS, p, h)
    return jnp.einsum("jgph,gf->jghfp", b4, e).reshape(g // S5_BLOCK_GROUPS, S5_BLOCK_GROUPS * h, S5_BLOCK_GROUPS * p)


def _blockdiag_in_grad(d):
    nj, gh, gp = d.shape
    h, p = gh // S5_BLOCK_GROUPS, gp // S5_BLOCK_GROUPS
    e = jnp.eye(S5_BLOCK_GROUPS, dtype=d.dtype)
    d6 = d.reshape(nj, S5_BLOCK_GROUPS, h, S5_BLOCK_GROUPS, p)
    return jnp.einsum("jghfp,gf->jgph", d6, e).reshape(nj * S5_BLOCK_GROUPS, p, h)


def _blockdiag_out(cc):
    g, h, p = cc.shape
    e = jnp.eye(S5_BLOCK_GROUPS, dtype=cc.dtype)
    c4 = cc.reshape(g // S5_BLOCK_GROUPS, S5_BLOCK_GROUPS, h, p)
    return jnp.einsum("jghp,gf->jfpgh", c4, e).reshape(g // S5_BLOCK_GROUPS, S5_BLOCK_GROUPS * p, S5_BLOCK_GROUPS * h)


def _blockdiag_out_grad(d):
    nj, gp, gh = d.shape
    h, p = gh // S5_BLOCK_GROUPS, gp // S5_BLOCK_GROUPS
    e = jnp.eye(S5_BLOCK_GROUPS, dtype=d.dtype)
    d6 = d.reshape(nj, S5_BLOCK_GROUPS, p, S5_BLOCK_GROUPS, h)
    return jnp.einsum("jfpgh,gf->jghp", d6, e).reshape(nj * S5_BLOCK_GROUPS, h, p)


def _s5_fwd(tag, x, prm, w_out, layer):
    a_re, a_im, log_dt, b_re, b_im, c_re, c_im, d_skip = prm
    t, d = x.shape
    g, p = a_re.shape
    w = g * p
    nj = g // S5_BLOCK_GROUPS
    cw, sw = S5_BLOCK_GROUPS * S5_GROUP, S5_BLOCK_GROUPS * p
    seg = t // SCAN_SEGMENTS
    tm = _tile(t, 4096)
    lb_re, lb_im, bb_re, bb_im = _s5_discretise(a_re, a_im, log_dt, b_re, b_im)
    lam = jnp.stack([lb_re.reshape(1, w), lb_im.reshape(1, w)])
    bs = jnp.stack([_blockdiag_in(bb_re), _blockdiag_in(bb_im)]).astype(BF16)
    cs = jnp.stack([_blockdiag_out(c_re), -_blockdiag_out(c_im)]).astype(BF16)
    dvec = d_skip.reshape(1, d)
    u = _to_segments(x)
    bu = _mm(f"{tag}_bu", u, bs, mode="nn", grid=(2, nj, t // tm), kaxis=None,
             a_blk=(tm, cw), a_map=lambda r, j, i: (i, j),
             b_blk=(None, None, cw, sw), b_map=lambda r, j, i: (r, j, 0, 0),
             o_shape=(2, t, w), o_blk=(None, tm, sw), o_map=lambda r, j, i: (r, i, j))
    hs = _s5_scan(f"{tag}_scan", lam, bu.reshape(2, seg, SCAN_SEGMENTS, w)).reshape(2, t, w)
    ych = _mm(f"{tag}_ch", hs, cs, mode="nn", grid=(nj, t // tm, 2), kaxis=2,
              a_blk=(None, tm, sw), a_map=lambda j, i, r: (r, i, j),
              b_blk=(None, None, sw, cw), b_map=lambda j, i, r: (r, j, 0, 0),
              o_shape=(t, d), o_blk=(tm, cw), o_map=lambda j, i, r: (i, j))
    ypre, act = _s5_act_fwd(f"{tag}_act", ych, u, dvec)
    vg = _mm_shards_nn(f"{tag}_wout", act, w_out, layer, F32)
    m = _from_segments(_glu_fwd(f"{tag}_glu", vg))
    return m, (u, lam, bs, cs, dvec, hs, ypre, act, vg)


def _s5_bwd(tag, dm, saved, prm, w_out, layer, g_wout):
    a_re, a_im, log_dt, b_re, b_im, c_re, c_im, d_skip = prm
    u, lam, bs, cs, dvec, hs, ypre, act, vg = saved
    t, d = u.shape
    g, p = a_re.shape
    w = g * p
    nj = g // S5_BLOCK_GROUPS
    cw, sw = S5_BLOCK_GROUPS * S5_GROUP, S5_BLOCK_GROUPS * p
    seg = t // SCAN_SEGMENTS
    tm = _tile(t, 4096)
    dvg = _glu_bwd(f"{tag}_glu_bwd", _to_segments(dm), vg)
    g_wout = _mm_shards_tn(f"{tag}_dwout", act, dvg, layer, g_wout)
    dact = _mm_shards_nt(f"{tag}_dact", dvg, w_out, layer)
    dypre, duskip, dd = _s5_act_bwd(f"{tag}_act_bwd", dact, ypre, u, dvec)
    dh = _mm(f"{tag}_dh", dypre, cs, mode="nt", grid=(2, nj, t // tm), kaxis=None,
             a_blk=(tm, cw), a_map=lambda r, j, i: (i, j),
             b_blk=(None, None, sw, cw), b_map=lambda r, j, i: (r, j, 0, 0),
             o_shape=(2, t, w), o_blk=(None, tm, sw), o_map=lambda r, j, i: (r, i, j))
    dcs = _mm(f"{tag}_dc", hs, dypre, mode="tn", grid=(2, nj, t // tm), kaxis=2,
              a_blk=(None, tm, sw), a_map=lambda r, j, i: (r, i, j),
              b_blk=(tm, cw), b_map=lambda r, j, i: (i, j),
              o_shape=(2, nj, sw, cw), o_blk=(None, None, sw, cw), o_map=lambda r, j, i: (r, j, 0, 0))
    gs, dlam8 = _s5_scan(f"{tag}_scan_bwd", lam, dh.reshape(2, seg, SCAN_SEGMENTS, w),
                         hs.reshape(2, seg, SCAN_SEGMENTS, w))
    gs = gs.reshape(2, t, w)
    du = _mm(f"{tag}_du", gs, bs, mode="nt", grid=(nj, t // tm, 2), kaxis=2,
             a_blk=(None, tm, sw), a_map=lambda j, i, r: (r, i, j),
             b_blk=(None, None, cw, sw), b_map=lambda j, i, r: (r, j, 0, 0),
             o_shape=(t, d), o_blk=(tm, cw), o_map=lambda j, i, r: (i, j))
    dbs = _mm(f"{tag}_db", u, gs, mode="tn", grid=(2, nj, t // tm), kaxis=2,
              a_blk=(tm, cw), a_map=lambda r, j, i: (i, j),
              b_blk=(None, tm, sw), b_map=lambda r, j, i: (r, i, j),
              o_shape=(2, nj, cw, sw), o_blk=(None, None, cw, sw), o_map=lambda r, j, i: (r, j, 0, 0))
    dx = _from_segments(du + duskip)
    dlam = jnp.sum(dlam8, axis=1).reshape(2, g, p)
    small = dict(dlb_re=dlam[0], dlb_im=dlam[1],
                 dbb_re=_blockdiag_in_grad(dbs[0]), dbb_im=_blockdiag_in_grad(dbs[1]),
                 dc_re=_blockdiag_out_grad(dcs[0]), dc_im=-_blockdiag_out_grad(dcs[1]),
                 dd=dd.reshape(g, S5_GROUP))
    return dx, g_wout, small


def _pack(pieces):
    rows = []
    for p in pieces:
        flat = p.reshape(-1).astype(F32)
        n = flat.shape[0]
        rows.append(jnp.pad(flat, (0, -n % V7X_LANES)).reshape(-1, V7X_LANES))
    buf = jnp.concatenate(rows, axis=0)
    return jnp.pad(buf, ((0, -buf.shape[0] % V7X_SUBLANES), (0, 0)))


def _unpack(buf, shapes):
    out, row = [], 0
    for s in shapes:
        n = math.prod(s)
        nr = -(-n // V7X_LANES)
        out.append(buf[row:row + nr].reshape(-1)[:n].reshape(s))
        row += nr
    return out


def kernel(x, ffn1_w_in, ffn1_w_out, ln1_g, ln1_b, lnm_g, lnm_b, ffn2_w_in, ffn2_w_out, ln2_g, ln2_b, fox_w_in, fox_b_f, fox_w_o, s5_a_re, s5_a_im, s5_log_dt, s5_b_re, s5_b_im, s5_c_re, s5_c_im, s5_d, s5_w_out, loss_target, m_ffn1_w_in, m_ffn1_w_out, m_ln1_g, m_ln1_b, m_lnm_g, m_lnm_b, m_ffn2_w_in, m_ffn2_w_out, m_ln2_g, m_ln2_b, m_fox_w_in, m_fox_b_f, m_fox_w_o, m_s5_a_re, m_s5_a_im, m_s5_log_dt, m_s5_b_re, m_s5_b_im, m_s5_c_re, m_s5_c_im, m_s5_d, m_s5_w_out, v_ffn1_w_in, v_ffn1_w_out, v_ln1_g, v_ln1_b, v_lnm_g, v_lnm_b, v_ffn2_w_in, v_ffn2_w_out, v_ln2_g, v_ln2_b, v_fox_w_in, v_fox_b_f, v_fox_w_o, v_s5_a_re, v_s5_a_im, v_s5_log_dt, v_s5_b_re, v_s5_b_im, v_s5_c_re, v_s5_c_im, v_s5_d, v_s5_w_out):
    big_names = ["ffn1_w_in", "ffn1_w_out", "ffn2_w_in", "ffn2_w_out", "fox_w_in", "fox_w_o", "s5_w_out"]
    small_names = ["ln1_g", "ln1_b", "lnm_g", "lnm_b", "ln2_g", "ln2_b", "fox_b_f", "s5_a_re", "s5_a_im", "s5_log_dt",
                   "s5_b_re", "s5_b_im", "s5_c_re", "s5_c_im", "s5_d"]
    out_order = ["ffn1_w_in", "ffn1_w_out", "ln1_g", "ln1_b", "lnm_g", "lnm_b", "ffn2_w_in", "ffn2_w_out", "ln2_g",
                 "ln2_b", "fox_w_in", "fox_b_f", "fox_w_o", "s5_a_re", "s5_a_im", "s5_log_dt", "s5_b_re", "s5_b_im",
                 "s5_c_re", "s5_c_im", "s5_d", "s5_w_out"]
    env = dict(locals())
    w = {n: env[n] for n in out_order}
    mom = {n: env["m_" + n] for n in out_order}
    vel = {n: env["v_" + n] for n in out_order}

    depth, d = ln1_g.shape
    t = x.shape[1]
    alpha = (2.0 * depth) ** 0.25
    x0 = x.reshape(t, d)
    tgt = loss_target.reshape(t, d)

    tix = {n: k for k, n in enumerate(big_names)}
    groups = []
    for i in range(depth):
        j = i // 2
        groups.append([(tix["ffn1_w_in"], i), (tix["ffn1_w_out"], i)])
        mixer = [(tix["fox_w_in"], j), (tix["fox_w_o"], j)] if i % 2 == 0 else [(tix["s5_w_out"], j)]
        groups.append(mixer + [(tix["ffn2_w_in"], i), (tix["ffn2_w_out"], i)])
    sems, bufs = _gather_start("gather_start", [_cast_place(f"cast_{n}", w[n]) for n in big_names], groups)
    full, rows3 = {}, {}

    def arrive(gi, after):
        nonlocal bufs
        bufs = _gather_wait(f"gather_wait_{gi}", bufs, sems[gi][0], sems[gi][1], after, groups[gi])
        full.update(zip(big_names, bufs))
        rows3.update({n: _rows_view(full[n]) for n in ("ffn1_w_out", "ffn2_w_out", "fox_w_o")})

    nh = fox_b_f.shape[1]
    fox_cols = 3 * d + nh
    fox_pad = -(-fox_cols // (5 * V7X_LANES)) * (5 * V7X_LANES)

    def fox_wpad(j):
        wf = full["fox_w_in"][j].transpose(1, 0, 2).reshape(d, fox_cols)
        return jnp.pad(wf, ((0, 0), (0, fox_pad - fox_cols)))

    def s5_params(j):
        return (s5_a_re[j], s5_a_im[j], s5_log_dt[j], s5_b_re[j], s5_b_im[j], s5_c_re[j], s5_c_im[j], s5_d[j])

    saved = []
    h = x0
    for i in range(depth):
        j = i // 2
        arrive(2 * i, h)
        h, s1 = _ffn_fwd(f"l{i}_ffn1", alpha, h, full["ffn1_w_in"], rows3["ffn1_w_out"], i,
                         ln1_g[i:i + 1], ln1_b[i:i + 1])
        arrive(2 * i + 1, h)
        if i % 2 == 0:
            h, xhat_m, rstd_m, sm = _fox_fwd(f"l{i}_fox", alpha, h, fox_wpad(j), fox_b_f[j:j + 1], rows3["fox_w_o"], j,
                                             lnm_g[i:i + 1], lnm_b[i:i + 1])
        else:
            m, sm = _s5_fwd(f"l{i}_s5", h, s5_params(j), full["s5_w_out"], j)
            h, xhat_m, rstd_m = _ln_fwd(f"l{i}_lnm", alpha, h, m, 1.0, lnm_g[i:i + 1], lnm_b[i:i + 1])
        h, s2 = _ffn_fwd(f"l{i}_ffn2", alpha, h, full["ffn2_w_in"], rows3["ffn2_w_out"], i,
                         ln2_g[i:i + 1], ln2_b[i:i + 1])
        saved.append((s1, sm, (xhat_m, rstd_m), s2))
    loss_part = _loss_sum("loss", h, tgt) * (0.5 / d)

    fox_in_names = [f"fox_w_in_l{j}" for j in range(fox_w_in.shape[0])]
    gshape = {n: full[n].shape for n in big_names if n != "fox_w_in"}
    gshape.update({n: (1,) + full["fox_w_in"].shape[1:] for n in fox_in_names})
    gbuf = {n: lax.empty(s, BF16) for n, s in gshape.items()}
    rxbuf = {n: lax.empty((N_PARTS, s[0], s[2] // 2, s[3]), BF16) for n, s in gshape.items()}
    pending = []

    zero = jnp.zeros((), F32)

    def scatter(tag, pairs):
        nonlocal zero
        names = list(dict.fromkeys(n for n, _ in pairs))
        items = [(names.index(n), layer) for n, layer in pairs]
        sem, send, rx, zero = _scatter_start(f"scatter_start_{tag}", [gbuf[n] for n in names],
                                             [rxbuf[n] for n in names], items)
        gbuf.update(zip(names, send))
        rxbuf.update(zip(names, rx))
        pending.append((tag, names, items, sem))

    gsmall = {n: [None] * w[n].shape[0] for n in small_names}
    s5_cot = [None] * s5_a_re.shape[0]
    terms = [(h, 1.0 / d), (tgt, -1.0 / d)]
    for i in reversed(range(depth)):
        j = i // 2
        s1, sm, (xhat_m, rstd_m), s2 = saved[i]
        terms, gbuf["ffn2_w_in"], g3, dg, db = _ffn_bwd(
            f"l{i}_ffn2", alpha, terms, s2, full["ffn2_w_in"], rows3["ffn2_w_out"], i, ln2_g[i:i + 1] + zero,
            gbuf["ffn2_w_in"], _rows_view(gbuf["ffn2_w_out"]))
        gbuf["ffn2_w_out"] = g3.reshape(gshape["ffn2_w_out"])
        gsmall["ln2_g"][i], gsmall["ln2_b"][i] = dg, db
        scatter(f"l{i}_ffn2", [("ffn2_w_in", i), ("ffn2_w_out", i)])
        dz, dg, db = _ln_bwd(f"l{i}_lnm_bwd", terms, xhat_m, rstd_m, lnm_g[i:i + 1] + zero)
        gsmall["lnm_g"][i], gsmall["lnm_b"][i] = dg, db
        if i % 2 == 0:
            dx, d_wpad, dbf, g3 = _fox_bwd(f"l{i}_fox", dz, sm, fox_wpad(j), fox_b_f[j:j + 1],
                                           rows3["fox_w_o"], j, _rows_view(gbuf["fox_w_o"]))
            gbuf["fox_w_o"] = g3.reshape(gshape["fox_w_o"])
            gbuf[fox_in_names[j]] = d_wpad[:, :fox_cols].reshape(d, N_CHIPS, -1).transpose(1, 0, 2)[None].astype(BF16)
            gsmall["fox_b_f"][j] = dbf
            scatter(f"l{i}_fox", [("fox_w_o", j), (fox_in_names[j], 0)])
        else:
            dx, gbuf["s5_w_out"], s5_cot[j] = _s5_bwd(f"l{i}_s5", dz, sm, s5_params(j), full["s5_w_out"], j,
                                                      gbuf["s5_w_out"])
            scatter(f"l{i}_s5", [("s5_w_out", j)])
        terms = [(dz, alpha), (dx, 1.0)]
        terms, gbuf["ffn1_w_in"], g3, dg, db = _ffn_bwd(
            f"l{i}_ffn1", alpha, terms, s1, full["ffn1_w_in"], rows3["ffn1_w_out"], i, ln1_g[i:i + 1] + zero,
            gbuf["ffn1_w_in"], _rows_view(gbuf["ffn1_w_out"]))
        gbuf["ffn1_w_out"] = g3.reshape(gshape["ffn1_w_out"])
        gsmall["ln1_g"][i], gsmall["ln1_b"][i] = dg, db
        scatter(f"l{i}_ffn1", [("ffn1_w_in", i), ("ffn1_w_out", i)])
    grad_x = _lincomb("grad_x", terms).reshape(x.shape)

    cot_names = ["dlb_re", "dlb_im", "dbb_re", "dbb_im", "dc_re", "dc_im", "dd"]
    ln_names = ["ln1_g", "ln1_b", "lnm_g", "lnm_b", "ln2_g", "ln2_b"]
    pieces = [loss_part + zero] + [jnp.concatenate(gsmall[n], axis=0) for n in ln_names + ["fox_b_f"]]
    pieces += [jnp.stack([s5_cot[j][n] for j in range(len(s5_cot))]) for n in cot_names]
    shapes = [p.shape for p in pieces]
    mine = _pack(pieces)
    everyone = _gather_all("small_gather", mine).reshape(N_DEV, *mine.shape)
    summed = _unpack(_sum_leading("small_sum", everyone), shapes)
    loss = summed[0].reshape(())
    gs_final = dict(zip(ln_names + ["fox_b_f"], summed[1:8]))
    cot = dict(zip(cot_names, summed[8:]))
    prm_names = ["s5_a_re", "s5_a_im", "s5_log_dt", "s5_b_re", "s5_b_im"]
    _, disc_vjp = jax.vjp(jax.vmap(_s5_discretise), *[w[n] for n in prm_names])
    for n, gval in zip(prm_names, disc_vjp((cot["dlb_re"], cot["dlb_im"], cot["dbb_re"], cot["dbb_im"]))):
        gs_final[n] = gval
    gs_final["s5_c_re"], gs_final["s5_c_im"], gs_final["s5_d"] = cot["dc_re"], cot["dc_im"], cot["dd"]

    grads, deltas, new_m, new_v = {}, {}, {}, {}
    small_shapes = [w[n].shape for n in small_names]
    for n in small_names:
        grads[n] = gs_final[n].reshape(w[n].shape)
    packed = [_pack([src[n] for n in small_names]) for src in (w, grads, mom, vel)]
    small_out = _adamw("adamw_small", *packed)
    for dst, buf in zip((deltas, new_m, new_v), small_out):
        for n, val in zip(small_names, _unpack(buf, small_shapes)):
            dst[n] = val

    for tag, names, items, sem in pending:
        send, rx = _scatter_wait(f"scatter_wait_{tag}", [gbuf[n] for n in names], [rxbuf[n] for n in names],
                                 sem[0], sem[1], small_out[0], items)
        gbuf.update(zip(names, send))
        rxbuf.update(zip(names, rx))
    half = {n: _chip_sum(f"grad_chip_sum_{n}", gbuf[n], rxbuf[n]) for n in gshape}
    half["fox_w_in"] = jnp.concatenate([half[n] for n in fox_in_names], axis=0)
    halves = [half[n] for n in big_names]
    theirs = _send_half("grad_send_half", halves)
    for n, mine_h, their_h in zip(big_names, halves, theirs):
        grads[n], deltas[n], new_m[n], new_v[n] = _adamw_join(f"adamw_{n}", w[n], mine_h, their_h, mom[n], vel[n])
    return (loss, grad_x, *[grads[n] for n in out_order], *[deltas[n] for n in out_order],
            *[new_m[n] for n in out_order], *[new_v[n] for n in out_order])
```

```python
import functools
import math

import jax
import jax.numpy as jnp
from jax import lax
from jax.experimental import pallas as pl
from jax.experimental.pallas import tpu as pltpu

F32 = jnp.float32
BF16 = jnp.bfloat16
LN_EPS = 1e-5
NEG_INF = -1e30
ADAM_LR = 0.001
ADAM_B1 = 0.9
ADAM_B2 = 0.999
ADAM_EPS = 1e-08
ADAM_WD = 0.01
ADAM_STEP = 10
S5_GROUP = 16
SCAN_SEGMENTS = 32
V7X_SUBLANES = 8
V7X_LANES = 128
VMEM_LIMIT = 56 * 1024 * 1024
N_CHIPS = 4
N_DEV = 8
MESH = pl.DeviceIdType.MESH
ANY = pl.BlockSpec(memory_space=pl.ANY)


def _cp(n_grid, kaxis=None):
    sem = tuple("arbitrary" if (kaxis is None or i == kaxis) else "parallel" for i in range(n_grid))
    return pltpu.CompilerParams(dimension_semantics=sem, vmem_limit_bytes=VMEM_LIMIT)


def _tile(n, pref, mult=V7X_SUBLANES):
    if n <= pref:
        return n
    for t in range(pref, 0, -1):
        if n % t == 0 and t % mult == 0:
            return t
    return n


_CONTRACT = {"nn": ((1,), (0,)), "nt": ((1,), (1,)), "tn": ((0,), (0,))}


def _mm(name, a, b, *, mode, grid, kaxis, a_blk, a_map, b_blk, b_map, o_shape, o_blk, o_map, o_dtype=F32, scale=None,
        into=None):
    nk = 1 if kaxis is None else grid[kaxis]
    assert kaxis is None or kaxis == len(grid) - 1
    dims = (_CONTRACT[mode], ((), ()))
    use_acc = nk > 1 and o_dtype != F32
    acc_shape = tuple(d for d in o_blk if d is not None)

    def body(a_ref, b_ref, *rest):
        o_ref, scratch = (rest[1], rest[2:]) if into is not None else (rest[0], rest[1:])
        p = lax.dot_general(a_ref[...].astype(BF16), b_ref[...].astype(BF16), dims, preferred_element_type=F32)
        if nk == 1:
            if scale is not None:
                p = p * scale
            o_ref[...] = p.astype(o_dtype)
            return
        acc = scratch[0] if use_acc else o_ref
        k = pl.program_id(kaxis)

        @pl.when(k == 0)
        def _():
            acc[...] = p

        @pl.when(k > 0)
        def _():
            acc[...] += p

        if use_acc or scale is not None:
            @pl.when(k == nk - 1)
            def _():
                r = acc[...]
                if scale is not None:
                    r = r * scale
                o_ref[...] = r.astype(o_dtype)

    in_specs = [pl.BlockSpec(a_blk, a_map), pl.BlockSpec(b_blk, b_map)]
    args = [a, b]
    if into is not None:
        assert into.shape == tuple(o_shape) and into.dtype == o_dtype
        in_specs.append(ANY)
        args.append(into)
    return pl.pallas_call(
        body, grid=grid, name=name, in_specs=in_specs,
        out_specs=pl.BlockSpec(o_blk, o_map),
        out_shape=jax.ShapeDtypeStruct(o_shape, o_dtype),
        input_output_aliases={2: 0} if into is not None else {},
        scratch_shapes=[pltpu.VMEM(acc_shape, F32)] if use_acc else [],
        compiler_params=_cp(len(grid), kaxis),
    )(*args)


def _mm_shards_nn(name, a, wall, layer, o_dtype):
    t, k = a.shape
    _, s, _, n = wall.shape
    tm = _tile(t, 512)
    return _mm(name, a, wall, mode="nn", grid=(s, t // tm), kaxis=None,
               a_blk=(tm, k), a_map=lambda j, i: (i, 0),
               b_blk=(None, None, k, n), b_map=lambda j, i: (layer, j, 0, 0),
               o_shape=(t, s * n), o_blk=(tm, n), o_map=lambda j, i: (i, j), o_dtype=o_dtype)


def _mm_shards_nt(name, g, wall, layer):
    t = g.shape[0]
    _, s, k, n = wall.shape
    tm = _tile(t, 512)
    return _mm(name, g, wall, mode="nt", grid=(t // tm, s), kaxis=1,
               a_blk=(tm, n), a_map=lambda i, kk: (i, kk),
               b_blk=(None, None, k, n), b_map=lambda i, kk: (layer, kk, 0, 0),
               o_shape=(t, k), o_blk=(tm, k), o_map=lambda i, kk: (i, 0))


def _mm_shards_tn(name, a, g, layer, into):
    t, k = a.shape
    _, s, _, n = into.shape
    tk = _tile(t, 512)
    return _mm(name, a, g, mode="tn", grid=(s, t // tk), kaxis=1,
               a_blk=(tk, k), a_map=lambda j, kk: (kk, 0),
               b_blk=(tk, n), b_map=lambda j, kk: (kk, j),
               o_shape=into.shape, o_blk=(None, None, k, n), o_map=lambda j, kk: (layer, j, 0, 0),
               o_dtype=into.dtype, into=into)


def _mm_nn(name, a, w, o_dtype=F32, tn=None):
    t, k = a.shape
    n = w.shape[1]
    tm = _tile(t, 512)
    tn = n if tn is None else tn
    return _mm(name, a, w, mode="nn", grid=(n // tn, t // tm), kaxis=None,
               a_blk=(tm, k), a_map=lambda j, i: (i, 0),
               b_blk=(k, tn), b_map=lambda j, i: (0, j),
               o_shape=(t, n), o_blk=(tm, tn), o_map=lambda j, i: (i, j), o_dtype=o_dtype)


def _mm_nt(name, g, w, layer=None, o_dtype=F32):
    t, k = g.shape
    n = w.shape[-2]
    tm = _tile(t, 512)
    b_blk, b_map = ((n, k), lambda i: (0, 0)) if layer is None else ((None, n, k), lambda i: (layer, 0, 0))
    return _mm(name, g, w, mode="nt", grid=(t // tm,), kaxis=None,
               a_blk=(tm, k), a_map=lambda i: (i, 0), b_blk=b_blk, b_map=b_map,
               o_shape=(t, n), o_blk=(tm, n), o_map=lambda i: (i, 0), o_dtype=o_dtype)


def _mm_tn(name, a, g, tm=None, tn=None, scale=None, layer=None, into=None):
    t, m = a.shape
    n = g.shape[1]
    tk = _tile(t, 512)
    tm = m if tm is None else tm
    tn = n if tn is None else tn
    if layer is None:
        o_shape, o_blk, o_map, o_dtype = (m, n), (tm, tn), lambda i, j, kk: (i, j), F32
    else:
        o_shape, o_blk, o_map, o_dtype = into.shape, (None, tm, tn), lambda i, j, kk: (layer, i, j), into.dtype
    return _mm(name, a, g, mode="tn", grid=(m // tm, n // tn, t // tk), kaxis=2,
               a_blk=(tk, tm), a_map=lambda i, j, kk: (kk, i),
               b_blk=(tk, tn), b_map=lambda i, j, kk: (kk, j),
               o_shape=o_shape, o_blk=o_blk, o_map=o_map, o_dtype=o_dtype, scale=scale, into=into)


def _sigmoid(x):
    return 1.0 / (1.0 + jnp.exp(-x))


def _rows_call(name, body, t, tm, ins, in_cols, outs, acc_outs=()):
    in_specs = []
    for x, c in zip(ins, in_cols):
        if x.shape[0] == 1:
            in_specs.append(pl.BlockSpec((1, c), lambda i: (0, 0)))
        else:
            in_specs.append(pl.BlockSpec((tm, c), lambda i: (i, 0)))
    out_specs = [pl.BlockSpec((tm, s.shape[1]), lambda i: (i, 0)) for s in outs]
    out_specs += [pl.BlockSpec((1, s.shape[1]), lambda i: (0, 0)) for s in acc_outs]
    return pl.pallas_call(
        body, grid=(t // tm,), name=name, in_specs=in_specs, out_specs=out_specs,
        out_shape=list(outs) + list(acc_outs), compiler_params=_cp(1),
    )(*ins)


def _ln_fwd(name, alpha, x, r, coef, g, b):
    t, d = x.shape
    tm = _tile(t, 256)

    def body(x_ref, r_ref, g_ref, b_ref, y_ref, xh_ref, rs_ref):
        z = alpha * x_ref[...] + coef * r_ref[...]
        mu = jnp.mean(z, axis=-1, keepdims=True)
        zc = z - mu
        var = jnp.mean(zc * zc, axis=-1, keepdims=True)
        rstd = lax.rsqrt(var + LN_EPS)
        xh = zc * rstd
        y_ref[...] = xh * g_ref[...] + b_ref[...]
        xh_ref[...] = xh
        rs_ref[...] = rstd

    sd = jax.ShapeDtypeStruct
    return _rows_call(name, body, t, tm, [x, r, g, b], [d, d, d, d],
                      [sd((t, d), F32), sd((t, d), F32), sd((t, 1), F32)])


def _ln_bwd(name, terms, xhat, rstd, g):
    t, d = xhat.shape
    tm = _tile(t, 256)
    n = len(terms)
    coefs = [c for _, c in terms]

    def body(*refs):
        t_refs = refs[:n]
        xh_ref, rs_ref, g_ref, dz_ref, dg_ref, db_ref = refs[n:]
        dy = coefs[0] * t_refs[0][...]
        for c, r in zip(coefs[1:], t_refs[1:]):
            dy = dy + c * r[...]
        xh = xh_ref[...]
        dxh = dy * g_ref[...]
        m1 = jnp.mean(dxh, axis=-1, keepdims=True)
        m2 = jnp.mean(dxh * xh, axis=-1, keepdims=True)
        dz_ref[...] = rs_ref[...] * (dxh - m1 - xh * m2)
        pg = jnp.sum(dy * xh, axis=0, keepdims=True)
        pb = jnp.sum(dy, axis=0, keepdims=True)
        i = pl.program_id(0)

        @pl.when(i == 0)
        def _():
            dg_ref[...] = pg
            db_ref[...] = pb

        @pl.when(i > 0)
        def _():
            dg_ref[...] += pg
            db_ref[...] += pb

    sd = jax.ShapeDtypeStruct
    arrs = [a for a, _ in terms] + [xhat, rstd, g]
    cols = [d] * n + [d, 1, d]
    return _rows_call(name, body, t, tm, arrs, cols, [sd((t, d), F32)], [sd((1, d), F32), sd((1, d), F32)])


def _lincomb(name, terms):
    t, d = terms[0][0].shape
    tm = _tile(t, 256)
    coefs = [c for _, c in terms]
    n = len(terms)

    def body(*refs):
        acc = coefs[0] * refs[0][...]
        for c, r in zip(coefs[1:], refs[1:n]):
            acc = acc + c * r[...]
        refs[n][...] = acc

    return _rows_call(name, body, t, tm, [a for a, _ in terms], [d] * n, [jax.ShapeDtypeStruct((t, d), F32)])[0]


def _loss_sum(name, y, tgt):
    t, d = y.shape
    tm = _tile(t, 256)

    def body(y_ref, t_ref, o_ref):
        e = y_ref[...] - t_ref[...]
        s = jnp.sum(jnp.sum(e * e, axis=1, keepdims=True), axis=0, keepdims=True)
        i = pl.program_id(0)

        @pl.when(i == 0)
        def _():
            o_ref[...] = s

        @pl.when(i > 0)
        def _():
            o_ref[...] += s

    return _rows_call(name, body, t, tm, [y, tgt], [d, d], [], [jax.ShapeDtypeStruct((1, 1), F32)])[0]


def _ffn_in(name, x, wall, layer):
    t, k = x.shape
    n = wall.shape[3]
    tm = _tile(t, 512)

    def body(x_ref, wg_ref, wu_ref, h_ref, a_ref):
        xb = x_ref[...].astype(BF16)
        g = lax.dot_general(xb, wg_ref[...], _NN, preferred_element_type=F32)
        u = lax.dot_general(xb, wu_ref[...], _NN, preferred_element_type=F32)
        h_ref[0] = g.astype(BF16)
        h_ref[1] = u.astype(BF16)
        a_ref[...] = (g * _sigmoid(g) * u).astype(BF16)

    return pl.pallas_call(
        body, grid=(2, t // tm), name=name,
        in_specs=[pl.BlockSpec((tm, k), lambda j, i: (i, 0)),
                  pl.BlockSpec((None, None, k, n), lambda j, i: (layer, j, 0, 0)),
                  pl.BlockSpec((None, None, k, n), lambda j, i: (layer, 2 + j, 0, 0))],
        out_specs=[pl.BlockSpec((2, tm, n), lambda j, i: (0, i, j)), pl.BlockSpec((tm, n), lambda j, i: (i, j))],
        out_shape=[jax.ShapeDtypeStruct((2, t, 2 * n), BF16), jax.ShapeDtypeStruct((t, 2 * n), BF16)],
        compiler_params=_cp(2),
    )(x, wall, wall)


def _ffn_da(name, dz, w3, layer, h):
    t, k = dz.shape
    f = w3.shape[1]
    n = f // 2
    tm = _tile(t, 512)

    def body(dz_ref, w_ref, g_ref, u_ref, dh_ref):
        d = 0.5 * lax.dot_general(dz_ref[...].astype(BF16), w_ref[...], _NT, preferred_element_type=F32)
        g = g_ref[...].astype(F32)
        u = u_ref[...].astype(F32)
        sg = _sigmoid(g)
        dh_ref[0] = (d * u * sg * (1.0 + g * (1.0 - sg))).astype(BF16)
        dh_ref[1] = (d * g * sg).astype(BF16)

    return pl.pallas_call(
        body, grid=(2, t // tm), name=name,
        in_specs=[pl.BlockSpec((tm, k), lambda j, i: (i, 0)),
                  pl.BlockSpec((None, n, k), lambda j, i: (layer, j, 0)),
                  pl.BlockSpec((None, tm, n), lambda j, i: (0, i, j)),
                  pl.BlockSpec((None, tm, n), lambda j, i: (1, i, j))],
        out_specs=pl.BlockSpec((2, tm, n), lambda j, i: (0, i, j)),
        out_shape=jax.ShapeDtypeStruct((2, t, f), BF16),
        compiler_params=_cp(2),
    )(dz, w3, h, h)


def _mm_ln(name, alpha, coef, a, w3, layer, x, g, b):
    t, k = a.shape
    d = w3.shape[2]
    tm = _tile(t, 512)

    def body(a_ref, w_ref, x_ref, g_ref, b_ref, y_ref, xh_ref, rs_ref):
        f = lax.dot_general(a_ref[...].astype(BF16), w_ref[...], _NN, preferred_element_type=F32)
        z = alpha * x_ref[...] + coef * f
        mu = jnp.mean(z, axis=-1, keepdims=True)
        zc = z - mu
        var = jnp.mean(zc * zc, axis=-1, keepdims=True)
        rstd = lax.rsqrt(var + LN_EPS)
        xh = zc * rstd
        y_ref[...] = xh * g_ref[...] + b_ref[...]
        xh_ref[...] = xh
        rs_ref[...] = rstd

    row = lambda c: pl.BlockSpec((tm, c), lambda i: (i, 0))
    vec = pl.BlockSpec((1, d), lambda i: (0, 0))
    sd = jax.ShapeDtypeStruct
    return pl.pallas_call(
        body, grid=(t // tm,), name=name,
        in_specs=[row(k), pl.BlockSpec((None, k, d), lambda i: (layer, 0, 0)), row(d), vec, vec],
        out_specs=[row(d), row(d), row(1)],
        out_shape=[sd((t, d), F32), sd((t, d), F32), sd((t, 1), F32)],
        compiler_params=_cp(1),
    )(a, w3, x, g, b)


_GELU_C = math.sqrt(2.0 / math.pi)


def _s5_act_fwd(name, ych, u, dvec):
    t, d = u.shape
    tm = _tile(t, 256)

    def body(y_ref, u_ref, d_ref, p_ref, a_ref):
        y = y_ref[...] + d_ref[...] * u_ref[...]
        p_ref[...] = y
        a_ref[...] = (0.5 * y * (1.0 + jnp.tanh(_GELU_C * (y + 0.044715 * y * y * y)))).astype(BF16)

    sd = jax.ShapeDtypeStruct
    return _rows_call(name, body, t, tm, [ych, u, dvec], [d, d, d], [sd((t, d), F32), sd((t, d), BF16)])


def _s5_act_bwd(name, dact, ypre, u, dvec):
    t, d = u.shape
    tm = _tile(t, 256)

    def body(da_ref, y_ref, u_ref, d_ref, dy_ref, ds_ref, dd_ref):
        y = y_ref[...]
        th = jnp.tanh(_GELU_C * (y + 0.044715 * y * y * y))
        dg = 0.5 * (1.0 + th) + 0.5 * y * (1.0 - th * th) * _GELU_C * (1.0 + 3.0 * 0.044715 * y * y)
        dy = da_ref[...] * dg
        dy_ref[...] = dy
        ds_ref[...] = dy * d_ref[...]
        pd = jnp.sum(dy * u_ref[...], axis=0, keepdims=True)
        i = pl.program_id(0)

        @pl.when(i == 0)
        def _():
            dd_ref[...] = pd

        @pl.when(i > 0)
        def _():
            dd_ref[...] += pd

    sd = jax.ShapeDtypeStruct
    return _rows_call(name, body, t, tm, [dact, ypre, u, dvec], [d, d, d, d],
                      [sd((t, d), F32), sd((t, d), F32)], [sd((1, d), F32)])


def _glu_fwd(name, vg):
    t, d2 = vg.shape
    d = d2 // 2
    tm = _tile(t, 256)

    def body(vg_ref, m_ref):
        m_ref[...] = vg_ref[:, :d] * _sigmoid(vg_ref[:, d:])

    return _rows_call(name, body, t, tm, [vg], [d2], [jax.ShapeDtypeStruct((t, d), F32)])[0]


def _glu_bwd(name, dm, vg):
    t, d2 = vg.shape
    d = d2 // 2
    tm = _tile(t, 256)

    def body(dm_ref, vg_ref, o_ref):
        sg = _sigmoid(vg_ref[:, d:])
        g = dm_ref[...]
        o_ref[:, :d] = (g * sg).astype(BF16)
        o_ref[:, d:] = (g * vg_ref[:, :d] * sg * (1.0 - sg)).astype(BF16)

    return _rows_call(name, body, t, tm, [dm, vg], [d, d2], [jax.ShapeDtypeStruct((t, d2), BF16)])[0]


def _adamw(name, w, g, m, v):
    r, c = w.shape
    tr = _tile(r, max(V7X_SUBLANES, (1 << 20) // (4 * c) // V7X_SUBLANES * V7X_SUBLANES))

    def body(w_ref, g_ref, m_ref, v_ref, d_ref, nm_ref, nv_ref):
        gg = g_ref[...]
        nm = ADAM_B1 * m_ref[...] + (1.0 - ADAM_B1) * gg
        nv = ADAM_B2 * v_ref[...] + (1.0 - ADAM_B2) * (gg * gg)
        m_hat = nm / (1.0 - ADAM_B1 ** ADAM_STEP)
        v_hat = nv / (1.0 - ADAM_B2 ** ADAM_STEP)
        d_ref[...] = -ADAM_LR * (m_hat / (jnp.sqrt(v_hat) + ADAM_EPS) + ADAM_WD * w_ref[...])
        nm_ref[...] = nm
        nv_ref[...] = nv

    sd = jax.ShapeDtypeStruct((r, c), F32)
    return _rows_call(name, body, r, tr, [w, g, m, v], [c] * 4, [sd, sd, sd])


def _my_shard():
    return 2 * lax.axis_index("x") + lax.axis_index("y")


def _my_core():
    return lax.axis_index("c")


def _adamw_join(name, w, mine, theirs, m, v):
    nl, r, c = w.shape
    h = r // 2
    tr = _tile(h, max(V7X_SUBLANES, (1 << 19) // (4 * c) // V7X_SUBLANES * V7X_SUBLANES))
    nb = h // tr

    def body(w_ref, a_ref, b_ref, m_ref, v_ref, g_ref, d_ref, nm_ref, nv_ref):
        gg = jnp.where(pl.program_id(1) == _my_core(), a_ref[...], b_ref[...])
        nm = ADAM_B1 * m_ref[...] + (1.0 - ADAM_B1) * gg
        nv = ADAM_B2 * v_ref[...] + (1.0 - ADAM_B2) * (gg * gg)
        m_hat = nm / (1.0 - ADAM_B1 ** ADAM_STEP)
        v_hat = nv / (1.0 - ADAM_B2 ** ADAM_STEP)
        g_ref[...] = gg
        d_ref[...] = -ADAM_LR * (m_hat / (jnp.sqrt(v_hat) + ADAM_EPS) + ADAM_WD * w_ref[...])
        nm_ref[...] = nm
        nv_ref[...] = nv

    full = pl.BlockSpec((None, tr, c), lambda l, hf, i: (l, hf * nb + i, 0))
    sd = jax.ShapeDtypeStruct((nl, r, c), F32)
    return pl.pallas_call(
        body, name=name, grid=(nl, 2, nb),
        in_specs=[full,
                  pl.BlockSpec((None, tr, c), lambda l, hf, i: (l, jnp.where(hf == _my_core(), i, 0), 0)),
                  pl.BlockSpec((None, tr, c), lambda l, hf, i: (l, jnp.where(hf == _my_core(), 0, i), 0)),
                  full, full],
        out_specs=[full, full, full, full],
        out_shape=[sd, sd, sd, sd],
        compiler_params=_cp(3),
    )(w, mine, theirs, m, v)


def _sum_leading(name, a):
    n, r, c = a.shape
    tr = _tile(r, 512)

    def body(a_ref, o_ref):
        acc = a_ref[0]
        for k in range(1, n):
            acc = acc + a_ref[k]
        o_ref[...] = acc

    return pl.pallas_call(
        body, grid=(r // tr,), name=name,
        in_specs=[pl.BlockSpec((n, tr, c), lambda i: (0, i, 0))],
        out_specs=pl.BlockSpec((tr, c), lambda i: (i, 0)),
        out_shape=jax.ShapeDtypeStruct((r, c), F32), compiler_params=_cp(1),
    )(a)


def _split3(x):
    hi = x.astype(BF16)
    r1 = x - hi.astype(F32)
    mid = r1.astype(BF16)
    lo = (r1 - mid.astype(F32)).astype(BF16)
    return hi, mid, lo


def _tri_sum(tri, x):
    dims = (((1,), (0,)), ((), ()))
    hi, mid, lo = _split3(x)
    out = lax.dot_general(tri, lo, dims, preferred_element_type=F32)
    out = out + lax.dot_general(tri, mid, dims, preferred_element_type=F32)
    return out + lax.dot_general(tri, hi, dims, preferred_element_type=F32)


def _fox_cumsum(name, fl, bf):
    t, h = fl.shape
    tb = _tile(t, 512)

    def body(fl_ref, bf_ref, c_ref, carry):
        i = pl.program_id(0)

        @pl.when(i == 0)
        def _():
            carry[...] = jnp.zeros_like(carry)

        x = fl_ref[...] + bf_ref[...]
        lf = jnp.minimum(x, 0.0) - jnp.log(1.0 + jnp.exp(-jnp.abs(x)))
        row = lax.broadcasted_iota(jnp.int32, (tb, tb), 0)
        col = lax.broadcasted_iota(jnp.int32, (tb, tb), 1)
        tri = jnp.where(row >= col, 1.0, 0.0).astype(BF16)
        c_ref[...] = _tri_sum(tri, lf) + carry[...]
        carry[...] += jnp.sum(lf, axis=0, keepdims=True)

    return pl.pallas_call(
        body, grid=(t // tb,), name=name,
        in_specs=[pl.BlockSpec((tb, h), lambda i: (i, 0)), pl.BlockSpec((1, h), lambda i: (0, 0))],
        out_specs=pl.BlockSpec((tb, h), lambda i: (i, 0)),
        out_shape=jax.ShapeDtypeStruct((t, h), F32),
        scratch_shapes=[pltpu.VMEM((1, h), F32)], compiler_params=_cp(1),
    )(fl, bf)


def _fox_cumsum_bwd(name, dcum, fl, bf):
    t, h = fl.shape
    tb = _tile(t, 512)
    nb = t // tb

    def body(dc_ref, fl_ref, bf_ref, df_ref, db_ref, carry):
        i = pl.program_id(0)

        @pl.when(i == 0)
        def _():
            carry[...] = jnp.zeros_like(carry)

        dc = dc_ref[...]
        row = lax.broadcasted_iota(jnp.int32, (tb, tb), 0)
        col = lax.broadcasted_iota(jnp.int32, (tb, tb), 1)
        tri = jnp.where(row <= col, 1.0, 0.0).astype(BF16)
        dlf = _tri_sum(tri, dc) + carry[...]
        carry[...] += jnp.sum(dc, axis=0, keepdims=True)
        x = fl_ref[...] + bf_ref[...]
        df = dlf / (1.0 + jnp.exp(x))
        df_ref[...] = df
        pb = jnp.sum(df, axis=0, keepdims=True)

        @pl.when(i == 0)
        def _():
            db_ref[...] = pb

        @pl.when(i > 0)
        def _():
            db_ref[...] += pb

    rev = lambda i: (nb - 1 - i, 0)
    return pl.pallas_call(
        body, grid=(nb,), name=name,
        in_specs=[pl.BlockSpec((tb, h), rev), pl.BlockSpec((tb, h), rev), pl.BlockSpec((1, h), lambda i: (0, 0))],
        out_specs=[pl.BlockSpec((tb, h), rev), pl.BlockSpec((1, h), lambda i: (0, 0))],
        out_shape=[jax.ShapeDtypeStruct((t, h), F32), jax.ShapeDtypeStruct((1, h), F32)],
        scratch_shapes=[pltpu.VMEM((1, h), F32)], compiler_params=_cp(1),
    )(dcum, fl, bf)


_NT = (((1,), (1,)), ((), ()))
_TN = (((0,), (0,)), ((), ()))
_NN = (((1,), (0,)), ((), ()))


def _causal_mask(s, tb):
    row = lax.broadcasted_iota(jnp.int32, (tb, tb), 0)
    col = lax.broadcasted_iota(jnp.int32, (tb, tb), 1)
    return jnp.where(col <= row, s, NEG_INF)


def _first_head_lanes(hd):
    return lax.broadcasted_iota(jnp.int32, (1, 2 * hd), 1) < hd


def _attn_fwd(name, qkv, ccol, crow, nh):
    nb, tb, d3 = qkv.shape
    d = d3 // 3
    hd = d // nh
    lanes = 2 * hd
    assert lanes == V7X_LANES
    scale = 1.0 / math.sqrt(hd)

    def body(q_ref, k_ref, v_ref, cc_ref, cr_ref, o_ref, lse_ref):
        i = pl.program_id(1)
        first = _first_head_lanes(hd)
        q = q_ref[...] * scale
        res = []
        for hh in (0, 1):
            qh = jnp.where(first if hh == 0 else jnp.logical_not(first), q, jnp.zeros_like(q))
            cc = cc_ref[:, hh:hh + 1]

            def step(j, carry, diagonal=False, qh=qh, cc=cc, hh=hh):
                m, l, acc = carry
                s = lax.dot_general(qh, k_ref[j], _NT, preferred_element_type=F32) + cc - cr_ref[j][hh:hh + 1, :]
                if diagonal:
                    s = _causal_mask(s, tb)
                m_new = jnp.maximum(m, jnp.max(s, axis=1, keepdims=True))
                p = jnp.exp(s - m_new)
                a = jnp.exp(m - m_new)
                l = a * l + jnp.sum(p, axis=1, keepdims=True)
                acc = a * acc + lax.dot_general(p.astype(BF16), v_ref[j], _NN, preferred_element_type=F32)
                return m_new, l, acc

            init = (jnp.full((tb, 1), NEG_INF, F32), jnp.zeros((tb, 1), F32), jnp.zeros((tb, lanes), F32))
            m, l, acc = step(i, lax.fori_loop(0, i, step, init), diagonal=True)
            res.append((acc / l, m + jnp.log(l)))
        o_ref[...] = jnp.where(first, res[0][0], res[1][0])
        lse_ref[:, 0:1] = res[0][1]
        lse_ref[:, 1:2] = res[1][1]

    kb, vb = d // lanes, 2 * d // lanes
    return pl.pallas_call(
        body, grid=(nh // 2, nb), name=name,
        in_specs=[pl.BlockSpec((None, tb, lanes), lambda h, i: (i, 0, h)),
                  pl.BlockSpec((nb, tb, lanes), lambda h, i: (0, 0, kb + h)),
                  pl.BlockSpec((nb, tb, lanes), lambda h, i: (0, 0, vb + h)),
                  pl.BlockSpec((None, None, tb, 2), lambda h, i: (h, i, 0, 0)),
                  pl.BlockSpec((None, nb, 2, tb), lambda h, i: (h, 0, 0, 0))],
        out_specs=[pl.BlockSpec((None, tb, lanes), lambda h, i: (i, 0, h)),
                   pl.BlockSpec((None, None, tb, 2), lambda h, i: (h, i, 0, 0))],
        out_shape=[jax.ShapeDtypeStruct((nb, tb, d), F32), jax.ShapeDtypeStruct((nh // 2, nb, tb, 2), F32)],
        compiler_params=_cp(2),
    )(qkv, qkv, qkv, ccol, crow)


def _attn_bwd(name, qkv, ccol, crow, o, lse, do, nh):
    nb, tb, d3 = qkv.shape
    d = d3 // 3
    hd = d // nh
    lanes = 2 * hd
    scale = 1.0 / math.sqrt(hd)

    def body(q_ref, k_ref, v_ref, cc_ref, cr_ref, o_ref, lse_ref, do_ref, dq_ref, dk_ref, dv_ref, dr_ref, dc_ref, dq_acc):
        j = pl.program_id(1)

        @pl.when(j == 0)
        def _():
            dq_acc[...] = jnp.zeros_like(dq_acc)
            dr_ref[...] = jnp.zeros_like(dr_ref)

        first = _first_head_lanes(hd)
        kj = k_ref[...]
        vj = v_ref[...]
        dk = jnp.zeros((tb, lanes), F32)
        dv = jnp.zeros((tb, lanes), F32)
        for hh in (0, 1):
            mine = first if hh == 0 else jnp.logical_not(first)
            cr = cr_ref[hh:hh + 1, :]

            def step(i, carry, diagonal=False, mine=mine, cr=cr, hh=hh):
                dk, dv, dc = carry
                qi = q_ref[i] * scale
                qh = jnp.where(mine, qi, jnp.zeros_like(qi))
                doh = jnp.where(mine, do_ref[i], 0.0)
                dob = doh.astype(BF16)
                di = jnp.sum(doh * o_ref[i], axis=1, keepdims=True)
                s = lax.dot_general(qh, kj, _NT, preferred_element_type=F32) + cc_ref[i][:, hh:hh + 1] - cr
                if diagonal:
                    s = _causal_mask(s, tb)
                p = jnp.exp(s - lse_ref[i][:, hh:hh + 1])
                dv = dv + lax.dot_general(p.astype(BF16), dob, _TN, preferred_element_type=F32)
                dp = lax.dot_general(dob, vj, _NT, preferred_element_type=F32)
                ds = p * (dp - di)
                dsb = ds.astype(BF16)
                dk = dk + lax.dot_general(dsb, qh, _TN, preferred_element_type=F32)
                dq = lax.dot_general(dsb, kj, _NN, preferred_element_type=F32) * scale
                dq_acc[i] += jnp.where(mine, dq, 0.0)
                dr_ref[i, :, hh:hh + 1] += jnp.sum(ds, axis=1, keepdims=True)
                dc = dc + jnp.sum(ds, axis=0, keepdims=True)
                return dk, dv, dc

            dk, dv, dc = lax.fori_loop(j + 1, nb, step, step(j, (dk, dv, jnp.zeros((1, tb), F32)), diagonal=True))
            dc_ref[hh:hh + 1, :] = dc
        dk_ref[...] = dk.astype(BF16)
        dv_ref[...] = dv.astype(BF16)

        @pl.when(j == nb - 1)
        def _():
            dq_ref[...] = dq_acc[...].astype(BF16)

    kb, vb = d // lanes, 2 * d // lanes
    whole = lambda c: pl.BlockSpec((nb, tb, lanes), lambda h, j: (0, 0, c + h))
    block = lambda c: pl.BlockSpec((None, tb, lanes), lambda h, j: (j, 0, c + h))
    cols = pl.BlockSpec((None, nb, tb, 2), lambda h, j: (h, 0, 0, 0))
    rows = pl.BlockSpec((None, None, 2, tb), lambda h, j: (h, j, 0, 0))
    sd = jax.ShapeDtypeStruct
    return pl.pallas_call(
        body, grid=(nh // 2, nb), name=name,
        in_specs=[whole(0), block(kb), block(vb), cols, rows, whole(0), cols, whole(0)],
        out_specs=[whole(0), block(0), block(0), cols, rows],
        out_shape=[sd((nb, tb, d), BF16), sd((nb, tb, d), BF16), sd((nb, tb, d), BF16),
                   sd((nh // 2, nb, tb, 2), F32), sd((nh // 2, nb, 2, tb), F32)],
        scratch_shapes=[pltpu.VMEM((nb, tb, lanes), F32)],
        compiler_params=_cp(2),
    )(qkv, qkv, qkv, ccol, crow, o, lse, do)


def _cmul(ar, ai, br, bi):
    return ar * br - ai * bi, ar * bi + ai * br


def _s5_scan(name, lam, xin, hs=None):
    reverse = hs is not None
    _, seg, ns, w = xin.shape
    assert ns == SCAN_SEGMENTS
    wb = min(w, 2 * V7X_LANES)
    nsq = seg.bit_length() - 1
    assert (1 << nsq) == seg

    def body(*refs):
        if reverse:
            lam_ref, x_ref, h_ref, o_ref, dl_ref = refs
        else:
            lam_ref, x_ref, o_ref = refs
        lr = jnp.broadcast_to(lam_ref[0], (ns, wb))
        li = jnp.broadcast_to(lam_ref[1], (ns, wb))
        if reverse:
            li = -li
        zero = jnp.zeros((ns, wb), F32)
        at = (lambda n: seg - 1 - n) if reverse else (lambda n: n)

        def local(n, c):
            r = at(n)
            mr, mi = _cmul(lr, li, c[0], c[1])
            nr = mr + x_ref[0, r]
            ni = mi + x_ref[1, r]
            o_ref[0, r] = nr
            o_ref[1, r] = ni
            return nr, ni

        er, ei = lax.fori_loop(0, seg, local, (zero, zero))
        pr, pi = lr, li
        for _ in range(nsq):
            pr, pi = _cmul(pr, pi, pr, pi)
        sub = lax.broadcasted_iota(jnp.int32, (ns, wb), 0)

        def shifted(a, sh):
            if reverse:
                return jnp.where(sub < ns - sh, pltpu.roll(a, ns - sh, 0), 0.0)
            return jnp.where(sub >= sh, pltpu.roll(a, sh, 0), 0.0)

        xr, xi = er, ei
        sh = 1
        while sh < ns:
            tr, ti = _cmul(pr, pi, shifted(xr, sh), shifted(xi, sh))
            xr, xi = xr + tr, xi + ti
            pr, pi = _cmul(pr, pi, pr, pi)
            sh *= 2
        cr, ci = shifted(xr, 1), shifted(xi, 1)

        def fix(r, q):
            tr, ti = _cmul(q[0], q[1], cr, ci)
            gr = o_ref[0, r] + tr
            gi = o_ref[1, r] + ti
            o_ref[0, r] = gr
            o_ref[1, r] = gi
            return gr, gi

        if not reverse:
            def fixup(n, q):
                fix(n, q)
                return _cmul(q[0], q[1], lr, li)

            lax.fori_loop(0, seg, fixup, (lr, li))
            return

        def fixup_acc(n, c):
            qr, qi, ar, ai = c
            r = seg - 1 - n
            gr, gi = fix(r, (qr, qi))
            hr = h_ref[0, r - 1]
            hi = h_ref[1, r - 1]
            qr, qi = _cmul(qr, qi, lr, li)
            return qr, qi, ar + gr * hr + gi * hi, ai + gi * hr - gr * hi

        qr, qi, ar, ai = lax.fori_loop(0, seg - 1, fixup_acc, (lr, li, zero, zero))
        gr, gi = fix(0, (qr, qi))
        hr = jnp.where(sub >= 1, pltpu.roll(h_ref[0, seg - 1], 1, 0), 0.0)
        hi = jnp.where(sub >= 1, pltpu.roll(h_ref[1, seg - 1], 1, 0), 0.0)
        dl_ref[0] = ar + gr * hr + gi * hi
        dl_ref[1] = ai + gi * hr - gr * hi

    big = pl.BlockSpec((2, seg, ns, wb), lambda j: (0, 0, 0, j))
    lam_spec = pl.BlockSpec((2, 1, wb), lambda j: (0, 0, j))
    sd = jax.ShapeDtypeStruct
    if reverse:
        return pl.pallas_call(
            body, grid=(w // wb,), name=name, in_specs=[lam_spec, big, big],
            out_specs=[big, pl.BlockSpec((2, ns, wb), lambda j: (0, 0, j))],
            out_shape=[sd(xin.shape, F32), sd((2, ns, w), F32)], compiler_params=_cp(1),
        )(lam, xin, hs)
    return pl.pallas_call(
        body, grid=(w // wb,), name=name, in_specs=[lam_spec, big], out_specs=big,
        out_shape=sd(xin.shape, F32), compiler_params=_cp(1),
    )(lam, xin)


def _place():
    x, y, c = lax.axis_index("x"), lax.axis_index("y"), lax.axis_index("c")
    chips = [(1 - x, y), (x, 1 - y), (1 - x, 1 - y)]
    return x, y, c, chips


def _comm_params():
    return pltpu.CompilerParams(vmem_limit_bytes=VMEM_LIMIT)


def _cast_place(name, w):
    nl, r, c = w.shape
    tr = _tile(r, max(16, (1 << 20) // (4 * c) // 16 * 16), 16)

    def body(w_ref, o_ref):
        o_ref[...] = w_ref[...].astype(BF16)

    return pl.pallas_call(
        body, name=name, grid=(nl, r // tr),
        in_specs=[pl.BlockSpec((None, tr, c), lambda l, i: (l, i, 0))],
        out_specs=pl.BlockSpec((None, None, tr, c), lambda l, i: (l, _my_shard(), i, 0)),
        out_shape=jax.ShapeDtypeStruct((nl, N_CHIPS, r, c), BF16),
        compiler_params=_cp(2),
    )(w)


def _gather_shards(name, bufs):
    n = len(bufs)

    def body(*refs):
        outs = refs[n:2 * n]
        send_sems, recv_sems = refs[2 * n:]
        x, y, c, chips = _place()
        my = 2 * x + y
        sibling = (x, y, 1 - c)

        def part(t, shard, half):
            h = bufs[t].shape[2] // 2
            return outs[t].at[:, shard, pl.ds(half * h, h)]

        def copy(t, k, ref, to):
            return pltpu.make_async_remote_copy(src_ref=ref, dst_ref=ref, send_sem=send_sems.at[t, k],
                                                recv_sem=recv_sems.at[t, k], device_id=to, device_id_type=MESH)

        sent = []
        for t in range(n):
            for k, chip in enumerate(chips):
                sent.append(copy(t, k, part(t, my, c), (*chip, c)))
                sent[-1].start()
        for k, chip in enumerate(chips):
            shard = 2 * chip[0] + chip[1]
            for t in range(n):
                copy(t, k, part(t, shard, c), (*chip, c)).wait_recv()
                sent.append(copy(t, 3 + k, part(t, shard, c), sibling))
                sent[-1].start()
        for k, chip in enumerate(chips):
            shard = 2 * chip[0] + chip[1]
            for t in range(n):
                copy(t, 3 + k, part(t, shard, 1 - c), sibling).wait_recv()
        for cp in sent:
            cp.wait_send()

    return pl.pallas_call(
        body, name=name, in_specs=[ANY] * n, out_specs=[ANY] * n,
        out_shape=[jax.ShapeDtypeStruct(b.shape, b.dtype) for b in bufs],
        input_output_aliases={t: t for t in range(n)},
        scratch_shapes=[pltpu.SemaphoreType.DMA((n, 6)), pltpu.SemaphoreType.DMA((n, 6))],
        compiler_params=_comm_params(),
    )(*bufs)


HBM_SPEC = pl.BlockSpec(memory_space=pltpu.HBM)
SEM_SPEC = pl.BlockSpec(memory_space=pltpu.SEMAPHORE)


def _split_params():
    return pltpu.CompilerParams(has_side_effects=pltpu.SideEffectType.DATAFLOW_SIDE_EFFECTING)


def _gather_start(name, bufs, groups):
    n, ng = len(bufs), len(groups)

    def body(*refs):
        sems = refs[n:n + 2 * ng]
        outs = refs[n + 2 * ng:]
        x, y, c, chips = _place()
        my = 2 * x + y
        for gi, group in enumerate(groups):
            for idx, (t, layer) in enumerate(group):
                block = outs[t].at[layer, my]
                for k, chip in enumerate(chips):
                    pltpu.make_async_remote_copy(
                        src_ref=block, dst_ref=block, send_sem=sems[2 * gi].at[3 * idx + k],
                        recv_sem=sems[2 * gi + 1].at[3 * idx + k], device_id=(*chip, c), device_id_type=MESH).start()

    sem_shapes = []
    for group in groups:
        sem_shapes += [pltpu.SemaphoreType.DMA((3 * len(group),))] * 2
    res = pl.pallas_call(
        body, name=name, in_specs=[HBM_SPEC] * n,
        out_specs=[SEM_SPEC] * (2 * ng) + [HBM_SPEC] * n,
        out_shape=sem_shapes + [pltpu.HBM(b.shape, b.dtype) for b in bufs],
        input_output_aliases={t: 2 * ng + t for t in range(n)},
        compiler_params=_split_params(),
    )(*[pltpu.with_memory_space_constraint(b, pltpu.HBM) for b in bufs])
    sems = [(res[2 * gi], res[2 * gi + 1]) for gi in range(ng)]
    return sems, list(res[2 * ng:])


def _gather_wait(name, bufs, send_sems, recv_sems, after, group):
    n = len(bufs)

    def body(*refs):
        ss, rs = refs[n], refs[n + 1]
        outs = refs[n + 3:]
        x, y, c, chips = _place()
        my = 2 * x + y
        for idx, (t, layer) in enumerate(group):
            for k, chip in enumerate(chips):
                cp = pltpu.make_async_remote_copy(
                    src_ref=outs[t].at[layer, my], dst_ref=outs[t].at[layer, 2 * chip[0] + chip[1]],
                    send_sem=ss.at[3 * idx + k], recv_sem=rs.at[3 * idx + k], device_id=(*chip, c), device_id_type=MESH)
                cp.wait_send()
                cp.wait_recv()

    return list(pl.pallas_call(
        body, name=name, in_specs=[HBM_SPEC] * n + [SEM_SPEC, SEM_SPEC, ANY],
        out_specs=[HBM_SPEC] * n,
        out_shape=[pltpu.HBM(b.shape, b.dtype) for b in bufs],
        input_output_aliases={t: t for t in range(n)},
        compiler_params=_split_params(),
    )(*bufs, send_sems, recv_sems, after))


N_PARTS = 7


def _scatter_items(send, rx, items, c, chips, x, y):
    my = 2 * x + y
    out = []
    for i, (k, layer) in enumerate(items):
        h = send[k].shape[2] // 2
        for kk, chip in enumerate(chips):
            shard = 2 * chip[0] + chip[1]
            for hf in (0, 1):
                out.append((send[k].at[layer, shard, pl.ds(hf * h, h)], rx[k].at[2 * kk + c, layer],
                            N_PARTS * i + 2 * kk + hf, N_PARTS * i + 2 * kk + c, (*chip, hf)))
        out.append((send[k].at[layer, my, pl.ds((1 - c) * h, h)], rx[k].at[N_PARTS - 1, layer],
                    N_PARTS * i + N_PARTS - 1, N_PARTS * i + N_PARTS - 1, (x, y, 1 - c)))
    return out


def _scatter_start(name, send, rx, items):
    n = len(send)
    m = N_PARTS * len(items)

    def body(*refs):
        ssem, rsem = refs[2 * n], refs[2 * n + 1]
        s_out, r_out = refs[2 * n + 2:3 * n + 2], refs[3 * n + 2:4 * n + 2]
        x, y, c, chips = _place()
        for src, dst, si, ri, to in _scatter_items(s_out, r_out, items, c, chips, x, y):
            pltpu.make_async_remote_copy(src_ref=src, dst_ref=dst, send_sem=ssem.at[si], recv_sem=rsem.at[ri],
                                         device_id=to, device_id_type=MESH).start()
        refs[4 * n + 2][...] = jnp.zeros((V7X_SUBLANES, V7X_LANES), F32)

    res = pl.pallas_call(
        body, name=name, in_specs=[HBM_SPEC] * (2 * n),
        out_specs=[SEM_SPEC, SEM_SPEC] + [HBM_SPEC] * (2 * n) + [pl.BlockSpec(memory_space=pltpu.VMEM)],
        out_shape=[pltpu.SemaphoreType.DMA((m,)), pltpu.SemaphoreType.DMA((m,))]
        + [pltpu.HBM(b.shape, b.dtype) for b in list(send) + list(rx)]
        + [jax.ShapeDtypeStruct((V7X_SUBLANES, V7X_LANES), F32)],
        input_output_aliases={t: 2 + t for t in range(2 * n)},
        compiler_params=_split_params(),
    )(*[pltpu.with_memory_space_constraint(b, pltpu.HBM) for b in list(send) + list(rx)])
    return (res[0], res[1]), list(res[2:2 + n]), list(res[2 + n:2 + 2 * n]), res[2 + 2 * n][0, 0]


def _scatter_wait(name, send, rx, ssem, rsem, after, items):
    n = len(send)

    def body(*refs):
        ss, rs = refs[2 * n], refs[2 * n + 1]
        s_out, r_out = refs[2 * n + 3:3 * n + 3], refs[3 * n + 3:]
        x, y, c, chips = _place()
        for i, (src, dst, si, ri, to) in enumerate(_scatter_items(s_out, r_out, items, c, chips, x, y)):
            arrival = i % N_PARTS
            landed = r_out[items[i // N_PARTS][0]].at[arrival, items[i // N_PARTS][1]]
            cp = pltpu.make_async_remote_copy(src_ref=src, dst_ref=landed, send_sem=ss.at[si],
                                              recv_sem=rs.at[N_PARTS * (i // N_PARTS) + arrival],
                                              device_id=to, device_id_type=MESH)
            cp.wait_send()
            cp.wait_recv()

    res = pl.pallas_call(
        body, name=name, in_specs=[HBM_SPEC] * (2 * n) + [SEM_SPEC, SEM_SPEC, ANY],
        out_specs=[HBM_SPEC] * (2 * n),
        out_shape=[pltpu.HBM(b.shape, b.dtype) for b in list(send) + list(rx)],
        input_output_aliases={t: t for t in range(2 * n)},
        compiler_params=_split_params(),
    )(*send, *rx, ssem, rsem, after)
    return list(res[:n]), list(res[n:])


def _chip_sum(name, g, rx):
    nl, _, r, c = g.shape
    h = r // 2
    tr = _tile(h, max(V7X_SUBLANES * 2, (1 << 19) // (2 * c) // 16 * 16), 16)
    nb = h // tr

    def body(g_ref, r_ref, o_ref):
        acc = g_ref[...].astype(F32)
        for k in range(N_PARTS):
            acc = acc + r_ref[k].astype(F32)
        o_ref[...] = acc

    return pl.pallas_call(
        body, name=name, grid=(nl, nb),
        in_specs=[pl.BlockSpec((None, None, tr, c), lambda l, i: (l, _my_shard(), _my_core() * nb + i, 0)),
                  pl.BlockSpec((N_PARTS, None, tr, c), lambda l, i: (0, l, i, 0))],
        out_specs=pl.BlockSpec((None, tr, c), lambda l, i: (l, i, 0)),
        out_shape=jax.ShapeDtypeStruct((nl, h, c), F32),
        compiler_params=_cp(2),
    )(g, rx)


def _send_half(name, fs):
    n = len(fs)

    def body(*refs):
        ins, outs = refs[:n], refs[n:2 * n]
        send_sems, recv_sems = refs[2 * n:]
        x, y, c, _ = _place()
        cps = []
        for t in range(n):
            cps.append(pltpu.make_async_remote_copy(
                src_ref=ins[t], dst_ref=outs[t], send_sem=send_sems.at[t], recv_sem=recv_sems.at[t],
                device_id=(x, y, 1 - c), device_id_type=MESH))
            cps[-1].start()
        for cp in cps:
            cp.wait()

    return pl.pallas_call(
        body, name=name, in_specs=[ANY] * n, out_specs=[ANY] * n,
        out_shape=[jax.ShapeDtypeStruct(f.shape, f.dtype) for f in fs],
        scratch_shapes=[pltpu.SemaphoreType.DMA((n,)), pltpu.SemaphoreType.DMA((n,))],
        compiler_params=_comm_params(),
    )(*fs)


def _gather_all(name, v):
    m_per = v.shape[0]

    def body(x_ref, out_ref, send_sems, recv_sems, local_sem):
        x, y, c, chips = _place()
        me, sibling = (x, y, c), (x, y, 1 - c)

        def rows(px, py, pc):
            return out_ref.at[pl.ds((4 * px + 2 * py + pc) * m_per, m_per), :]

        def copy(k, block, to, src=None):
            return pltpu.make_async_remote_copy(
                src_ref=rows(*block) if src is None else src, dst_ref=rows(*block), send_sem=send_sems.at[k],
                recv_sem=recv_sems.at[k], device_id=to, device_id_type=MESH)

        mine = pltpu.make_async_copy(x_ref, rows(*me), local_sem)
        mine.start()
        first = [copy(0, me, sibling, src=x_ref)]
        first += [copy(1 + j, me, (*chip, c), src=x_ref) for j, chip in enumerate(chips)]
        for cp in first:
            cp.start()
        passed = [copy(4 + j, (*chip, c), sibling) for j, chip in enumerate(chips)]
        for j, chip in enumerate(chips):
            copy(1 + j, (*chip, c), me).wait_recv()
            passed[j].start()
        copy(0, sibling, me).wait_recv()
        for j, chip in enumerate(chips):
            copy(4 + j, (*chip, 1 - c), me).wait_recv()
        for cp in first + passed:
            cp.wait_send()
        mine.wait()

    return pl.pallas_call(
        body, name=name, in_specs=[ANY], out_specs=ANY,
        out_shape=jax.ShapeDtypeStruct((N_DEV * m_per, v.shape[1]), v.dtype),
        scratch_shapes=[pltpu.SemaphoreType.DMA((7,)), pltpu.SemaphoreType.DMA((7,)), pltpu.SemaphoreType.DMA],
        compiler_params=_comm_params(),
    )(v)


def _rows_view(wall):
    nl, s, r, c = wall.shape
    return wall.reshape(nl, s * r, c)


def _ffn_fwd(tag, alpha, x, w_in, w_out3, layer, g, b):
    h, a = _ffn_in(f"{tag}_in", x, w_in, layer)
    y, xhat, rstd = _mm_ln(f"{tag}_out", alpha, 0.5, a, w_out3, layer, x, g, b)
    return y, (x, h, a, xhat, rstd)


def _ffn_bwd(tag, alpha, terms, saved, w_in, w_out3, layer, g, g_win, g_wout3):
    x, h, a, xhat, rstd = saved
    t = x.shape[0]
    _, s, k, n = w_in.shape
    tm = _tile(t, 512)
    dz, dg, db = _ln_bwd(f"{tag}_ln_bwd", terms, xhat, rstd, g)
    g_wout3 = _mm_tn(f"{tag}_dwout", a, dz, tm=n, scale=0.5, layer=layer, into=g_wout3)
    dh = _ffn_da(f"{tag}_da", dz, w_out3, layer, h)
    g_win = _mm(f"{tag}_dwin", x, dh, mode="tn", grid=(s, t // tm), kaxis=1,
                a_blk=(tm, k), a_map=lambda j, kk: (kk, 0),
                b_blk=(None, tm, n), b_map=lambda j, kk: (j // 2, kk, j % 2),
                o_shape=w_in.shape, o_blk=(None, None, k, n), o_map=lambda j, kk: (layer, j, 0, 0),
                o_dtype=g_win.dtype, into=g_win)
    dx = _mm(f"{tag}_dx", dh, w_in, mode="nt", grid=(t // tm, s), kaxis=1,
             a_blk=(None, tm, n), a_map=lambda i, kk: (kk // 2, i, kk % 2),
             b_blk=(None, None, k, n), b_map=lambda i, kk: (layer, kk, 0, 0),
             o_shape=(t, k), o_blk=(tm, k), o_map=lambda i, kk: (i, 0))
    return [(dz, alpha), (dx, 1.0)], g_win, g_wout3, dg, db


def _fox_fwd(tag, alpha, x, w_pad, bf, w_o3, layer, g, b):
    t, d = x.shape
    nh = bf.shape[1]
    tb = _tile(t, 512)
    nb = t // tb
    qkv = _mm_nn(f"{tag}_qkv", x, w_pad[:, :3 * d], o_dtype=BF16, tn=d).reshape(nb, tb, 3 * d)
    fl = _mm_nn(f"{tag}_gate", x, w_pad[:, 3 * d:])[:, :nh]
    cum = _fox_cumsum(f"{tag}_cum", fl, bf)
    ccol = cum.reshape(nb, tb, nh // 2, 2).transpose(2, 0, 1, 3)
    crow = cum.reshape(nb, tb, nh // 2, 2).transpose(2, 0, 3, 1)
    o, lse = _attn_fwd(f"{tag}_attn", qkv, ccol, crow, nh)
    o2 = o.reshape(t, d)
    y, xhat, rstd = _mm_ln(f"{tag}_oproj", alpha, 1.0, o2, w_o3, layer, x, g, b)
    return y, xhat, rstd, (x, qkv, ccol, crow, o, lse, fl)


def _fox_bwd(tag, dm, saved, w_pad, bf, w_o3, layer, g_wo3):
    x, qkv, ccol, crow, o, lse, fl = saved
    t, d = x.shape
    nh = bf.shape[1]
    nb, tb, _ = qkv.shape
    g_wo3 = _mm_tn(f"{tag}_dwo", o.reshape(t, d), dm, layer=layer, into=g_wo3)
    do = _mm_nt(f"{tag}_do", dm, w_o3, layer=layer).reshape(nb, tb, d)
    dq, dk, dv, drow, dcol = _attn_bwd(f"{tag}_attn_bwd", qkv, ccol, crow, o, lse, do, nh)
    dcum = drow.transpose(1, 2, 0, 3).reshape(t, nh) - dcol.transpose(1, 3, 0, 2).reshape(t, nh)
    dfl, dbf = _fox_cumsum_bwd(f"{tag}_cum_bwd", dcum, fl, bf)
    pad = w_pad.shape[1] - 3 * d - nh
    dproj = jnp.concatenate([dq.reshape(t, d), dk.reshape(t, d), dv.reshape(t, d),
                             dfl.astype(BF16), jnp.zeros((t, pad), BF16)], axis=1)
    d_wpad = _mm_tn(f"{tag}_dwin", x, dproj, tn=_tile(w_pad.shape[1], 640, V7X_LANES))
    dx = _mm_nt(f"{tag}_dx", dproj, w_pad)
    return dx, d_wpad, dbf, g_wo3


def _to_segments(a):
    t, d = a.shape
    return a.reshape(SCAN_SEGMENTS, t // SCAN_SEGMENTS, d).transpose(1, 0, 2).reshape(t, d)


def _from_segments(a):
    t, d = a.shape
    return a.reshape(t // SCAN_SEGMENTS, SCAN_SEGMENTS, d).transpose(1, 0, 2).reshape(t, d)


def _s5_discretise(a_re, a_im, log_dt, b_re, b_im):
    dt = jnp.exp(log_dt)[:, None]
    mag = jnp.exp(a_re * dt)
    ang = a_im * dt
    lb_re = mag * jnp.cos(ang)
    lb_im = mag * jnp.sin(ang)
    den = a_re * a_re + a_im * a_im
    nr = lb_re - 1.0
    ni = lb_im
    z_re = (nr * a_re + ni * a_im) / den
    z_im = (ni * a_re - nr * a_im) / den
    bb_re = z_re[..., None] * b_re - z_im[..., None] * b_im
    bb_im = z_re[..., None] * b_im + z_im[..., None] * b_re
    return lb_re, lb_im, bb_re, bb_im


S5_BLOCK_GROUPS = 8


def _blockdiag_in(bb):
    g, p, h = bb.shape
    e = jnp.eye(S5_BLOCK_GROUPS, dtype=bb.dtype)
    b4 = bb.reshape(g // S5_BLOCK_GROUPS, S5_BLOCK_GROUPS, p, h)
    return jnp.einsum("jgph,gf->jghfp", b4, e).reshape(g // S5_BLOCK_GROUPS, S5_BLOCK_GROUPS * h, S5_BLOCK_GROUPS * p)


def _blockdiag_in_grad(d):
    nj, gh, gp = d.shape
    h, p = gh // S5_BLOCK_GROUPS, gp // S5_BLOCK_GROUPS
    e = jnp.eye(S5_BLOCK_GROUPS, dtype=d.dtype)
    d6 = d.reshape(nj, S5_BLOCK_GROUPS, h, S5_BLOCK_GROUPS, p)
    return jnp.einsum("jghfp,gf->jgph", d6, e).reshape(nj * S5_BLOCK_GROUPS, p, h)


def _blockdiag_out(cc):
    g, h, p = cc.shape
    e = jnp.eye(S5_BLOCK_GROUPS, dtype=cc.dtype)
    c4 = cc.reshape(g // S5_BLOCK_GROUPS, S5_BLOCK_GROUPS, h, p)
    return jnp.einsum("jghp,gf->jfpgh", c4, e).reshape(g // S5_BLOCK_GROUPS, S5_BLOCK_GROUPS * p, S5_BLOCK_GROUPS * h)


def _blockdiag_out_grad(d):
    nj, gp, gh = d.shape
    h, p = gh // S5_BLOCK_GROUPS, gp // S5_BLOCK_GROUPS
    e = jnp.eye(S5_BLOCK_GROUPS, dtype=d.dtype)
    d6 = d.reshape(nj, S5_BLOCK_GROUPS, p, S5_BLOCK_GROUPS, h)
    return jnp.einsum("jfpgh,gf->jghp", d6, e).reshape(nj * S5_BLOCK_GROUPS, h, p)


def _s5_fwd(tag, x, prm, w_out, layer):
    a_re, a_im, log_dt, b_re, b_im, c_re, c_im, d_skip = prm
    t, d = x.shape
    g, p = a_re.shape
    w = g * p
    nj = g // S5_BLOCK_GROUPS
    cw, sw = S5_BLOCK_GROUPS * S5_GROUP, S5_BLOCK_GROUPS * p
    seg = t // SCAN_SEGMENTS
    tm = _tile(t, 4096)
    lb_re, lb_im, bb_re, bb_im = _s5_discretise(a_re, a_im, log_dt, b_re, b_im)
    lam = jnp.stack([lb_re.reshape(1, w), lb_im.reshape(1, w)])
    bs = jnp.stack([_blockdiag_in(bb_re), _blockdiag_in(bb_im)]).astype(BF16)
    cs = jnp.stack([_blockdiag_out(c_re), -_blockdiag_out(c_im)]).astype(BF16)
    dvec = d_skip.reshape(1, d)
    u = _to_segments(x)
    bu = _mm(f"{tag}_bu", u, bs, mode="nn", grid=(2, nj, t // tm), kaxis=None,
             a_blk=(tm, cw), a_map=lambda r, j, i: (i, j),
             b_blk=(None, None, cw, sw), b_map=lambda r, j, i: (r, j, 0, 0),
             o_shape=(2, t, w), o_blk=(None, tm, sw), o_map=lambda r, j, i: (r, i, j))
    hs = _s5_scan(f"{tag}_scan", lam, bu.reshape(2, seg, SCAN_SEGMENTS, w)).reshape(2, t, w)
    ych = _mm(f"{tag}_ch", hs, cs, mode="nn", grid=(nj, t // tm, 2), kaxis=2,
              a_blk=(None, tm, sw), a_map=lambda j, i, r: (r, i, j),
              b_blk=(None, None, sw, cw), b_map=lambda j, i, r: (r, j, 0, 0),
              o_shape=(t, d), o_blk=(tm, cw), o_map=lambda j, i, r: (i, j))
    ypre, act = _s5_act_fwd(f"{tag}_act", ych, u, dvec)
    vg = _mm_shards_nn(f"{tag}_wout", act, w_out, layer, F32)
    m = _from_segments(_glu_fwd(f"{tag}_glu", vg))
    return m, (u, lam, bs, cs, dvec, hs, ypre, act, vg)


def _s5_bwd(tag, dm, saved, prm, w_out, layer, g_wout):
    a_re, a_im, log_dt, b_re, b_im, c_re, c_im, d_skip = prm
    u, lam, bs, cs, dvec, hs, ypre, act, vg = saved
    t, d = u.shape
    g, p = a_re.shape
    w = g * p
    nj = g // S5_BLOCK_GROUPS
    cw, sw = S5_BLOCK_GROUPS * S5_GROUP, S5_BLOCK_GROUPS * p
    seg = t // SCAN_SEGMENTS
    tm = _tile(t, 4096)
    dvg = _glu_bwd(f"{tag}_glu_bwd", _to_segments(dm), vg)
    g_wout = _mm_shards_tn(f"{tag}_dwout", act, dvg, layer, g_wout)
    dact = _mm_shards_nt(f"{tag}_dact", dvg, w_out, layer)
    dypre, duskip, dd = _s5_act_bwd(f"{tag}_act_bwd", dact, ypre, u, dvec)
    dh = _mm(f"{tag}_dh", dypre, cs, mode="nt", grid=(2, nj, t // tm), kaxis=None,
             a_blk=(tm, cw), a_map=lambda r, j, i: (i, j),
             b_blk=(None, None, sw, cw), b_map=lambda r, j, i: (r, j, 0, 0),
             o_shape=(2, t, w), o_blk=(None, tm, sw), o_map=lambda r, j, i: (r, i, j))
    dcs = _mm(f"{tag}_dc", hs, dypre, mode="tn", grid=(2, nj, t // tm), kaxis=2,
              a_blk=(None, tm, sw), a_map=lambda r, j, i: (r, i, j),
              b_blk=(tm, cw), b_map=lambda r, j, i: (i, j),
              o_shape=(2, nj, sw, cw), o_blk=(None, None, sw, cw), o_map=lambda r, j, i: (r, j, 0, 0))
    gs, dlam8 = _s5_scan(f"{tag}_scan_bwd", lam, dh.reshape(2, seg, SCAN_SEGMENTS, w),
                         hs.reshape(2, seg, SCAN_SEGMENTS, w))
    gs = gs.reshape(2, t, w)
    du = _mm(f"{tag}_du", gs, bs, mode="nt", grid=(nj, t // tm, 2), kaxis=2,
             a_blk=(None, tm, sw), a_map=lambda j, i, r: (r, i, j),
             b_blk=(None, None, cw, sw), b_map=lambda j, i, r: (r, j, 0, 0),
             o_shape=(t, d), o_blk=(tm, cw), o_map=lambda j, i, r: (i, j))
    dbs = _mm(f"{tag}_db", u, gs, mode="tn", grid=(2, nj, t // tm), kaxis=2,
              a_blk=(tm, cw), a_map=lambda r, j, i: (i, j),
              b_blk=(None, tm, sw), b_map=lambda r, j, i: (r, i, j),
              o_shape=(2, nj, cw, sw), o_blk=(None, None, cw, sw), o_map=lambda r, j, i: (r, j, 0, 0))
    dx = _from_segments(du + duskip)
    dlam = jnp.sum(dlam8, axis=1).reshape(2, g, p)
    small = dict(dlb_re=dlam[0], dlb_im=dlam[1],
                 dbb_re=_blockdiag_in_grad(dbs[0]), dbb_im=_blockdiag_in_grad(dbs[1]),
                 dc_re=_blockdiag_out_grad(dcs[0]), dc_im=-_blockdiag_out_grad(dcs[1]),
                 dd=dd.reshape(g, S5_GROUP))
    return dx, g_wout, small


def _pack(pieces):
    rows = []
    for p in pieces:
        flat = p.reshape(-1).astype(F32)
        n = flat.shape[0]
        rows.append(jnp.pad(flat, (0, -n % V7X_LANES)).reshape(-1, V7X_LANES))
    buf = jnp.concatenate(rows, axis=0)
    return jnp.pad(buf, ((0, -buf.shape[0] % V7X_SUBLANES), (0, 0)))


def _unpack(buf, shapes):
    out, row = [], 0
    for s in shapes:
        n = math.prod(s)
        nr = -(-n // V7X_LANES)
        out.append(buf[row:row + nr].reshape(-1)[:n].reshape(s))
        row += nr
    return out


def kernel(x, ffn1_w_in, ffn1_w_out, ln1_g, ln1_b, lnm_g, lnm_b, ffn2_w_in, ffn2_w_out, ln2_g, ln2_b, fox_w_in, fox_b_f, fox_w_o, s5_a_re, s5_a_im, s5_log_dt, s5_b_re, s5_b_im, s5_c_re, s5_c_im, s5_d, s5_w_out, loss_target, m_ffn1_w_in, m_ffn1_w_out, m_ln1_g, m_ln1_b, m_lnm_g, m_lnm_b, m_ffn2_w_in, m_ffn2_w_out, m_ln2_g, m_ln2_b, m_fox_w_in, m_fox_b_f, m_fox_w_o, m_s5_a_re, m_s5_a_im, m_s5_log_dt, m_s5_b_re, m_s5_b_im, m_s5_c_re, m_s5_c_im, m_s5_d, m_s5_w_out, v_ffn1_w_in, v_ffn1_w_out, v_ln1_g, v_ln1_b, v_lnm_g, v_lnm_b, v_ffn2_w_in, v_ffn2_w_out, v_ln2_g, v_ln2_b, v_fox_w_in, v_fox_b_f, v_fox_w_o, v_s5_a_re, v_s5_a_im, v_s5_log_dt, v_s5_b_re, v_s5_b_im, v_s5_c_re, v_s5_c_im, v_s5_d, v_s5_w_out):
    big_names = ["ffn1_w_in", "ffn1_w_out", "ffn2_w_in", "ffn2_w_out", "fox_w_in", "fox_w_o", "s5_w_out"]
    small_names = ["ln1_g", "ln1_b", "lnm_g", "lnm_b", "ln2_g", "ln2_b", "fox_b_f", "s5_a_re", "s5_a_im", "s5_log_dt",
                   "s5_b_re", "s5_b_im", "s5_c_re", "s5_c_im", "s5_d"]
    out_order = ["ffn1_w_in", "ffn1_w_out", "ln1_g", "ln1_b", "lnm_g", "lnm_b", "ffn2_w_in", "ffn2_w_out", "ln2_g",
                 "ln2_b", "fox_w_in", "fox_b_f", "fox_w_o", "s5_a_re", "s5_a_im", "s5_log_dt", "s5_b_re", "s5_b_im",
                 "s5_c_re", "s5_c_im", "s5_d", "s5_w_out"]
    env = dict(locals())
    w = {n: env[n] for n in out_order}
    mom = {n: env["m_" + n] for n in out_order}
    vel = {n: env["v_" + n] for n in out_order}

    depth, d = ln1_g.shape
    t = x.shape[1]
    alpha = (2.0 * depth) ** 0.25
    x0 = x.reshape(t, d)
    tgt = loss_target.reshape(t, d)

    tix = {n: k for k, n in enumerate(big_names)}
    groups = []
    for i in range(depth):
        j = i // 2
        groups.append([(tix["ffn1_w_in"], i), (tix["ffn1_w_out"], i)])
        mixer = [(tix["fox_w_in"], j), (tix["fox_w_o"], j)] if i % 2 == 0 else [(tix["s5_w_out"], j)]
        groups.append(mixer + [(tix["ffn2_w_in"], i), (tix["ffn2_w_out"], i)])
    sems, bufs = _gather_start("gather_start", [_cast_place(f"cast_{n}", w[n]) for n in big_names], groups)
    full, rows3 = {}, {}

    def arrive(gi, after):
        nonlocal bufs
        bufs = _gather_wait(f"gather_wait_{gi}", bufs, sems[gi][0], sems[gi][1], after, groups[gi])
        full.update(zip(big_names, bufs))
        rows3.update({n: _rows_view(full[n]) for n in ("ffn1_w_out", "ffn2_w_out", "fox_w_o")})

    nh = fox_b_f.shape[1]
    fox_cols = 3 * d + nh
    fox_pad = -(-fox_cols // (5 * V7X_LANES)) * (5 * V7X_LANES)

    def fox_wpad(j):
        wf = full["fox_w_in"][j].transpose(1, 0, 2).reshape(d, fox_cols)
        return jnp.pad(wf, ((0, 0), (0, fox_pad - fox_cols)))

    def s5_params(j):
        return (s5_a_re[j], s5_a_im[j], s5_log_dt[j], s5_b_re[j], s5_b_im[j], s5_c_re[j], s5_c_im[j], s5_d[j])

    saved = []
    h = x0
    for i in range(depth):
        j = i // 2
        arrive(2 * i, h)
        h, s1 = _ffn_fwd(f"l{i}_ffn1", alpha, h, full["ffn1_w_in"], rows3["ffn1_w_out"], i,
                         ln1_g[i:i + 1], ln1_b[i:i + 1])
        arrive(2 * i + 1, h)
        if i % 2 == 0:
            h, xhat_m, rstd_m, sm = _fox_fwd(f"l{i}_fox", alpha, h, fox_wpad(j), fox_b_f[j:j + 1], rows3["fox_w_o"], j,
                                             lnm_g[i:i + 1], lnm_b[i:i + 1])
        else:
            m, sm = _s5_fwd(f"l{i}_s5", h, s5_params(j), full["s5_w_out"], j)
            h, xhat_m, rstd_m = _ln_fwd(f"l{i}_lnm", alpha, h, m, 1.0, lnm_g[i:i + 1], lnm_b[i:i + 1])
        h, s2 = _ffn_fwd(f"l{i}_ffn2", alpha, h, full["ffn2_w_in"], rows3["ffn2_w_out"], i,
                         ln2_g[i:i + 1], ln2_b[i:i + 1])
        saved.append((s1, sm, (xhat_m, rstd_m), s2))
    loss_part = _loss_sum("loss", h, tgt) * (0.5 / d)

    fox_in_names = [f"fox_w_in_l{j}" for j in range(fox_w_in.shape[0])]
    gshape = {n: full[n].shape for n in big_names if n != "fox_w_in"}
    gshape.update({n: (1,) + full["fox_w_in"].shape[1:] for n in fox_in_names})
    gbuf = {n: lax.empty(s, BF16) for n, s in gshape.items()}
    rxbuf = {n: lax.empty((N_PARTS, s[0], s[2] // 2, s[3]), BF16) for n, s in gshape.items()}
    pending = []

    zero = jnp.zeros((), F32)

    def scatter(tag, pairs):
        nonlocal zero
        names = list(dict.fromkeys(n for n, _ in pairs))
        items = [(names.index(n), layer) for n, layer in pairs]
        sem, send, rx, zero = _scatter_start(f"scatter_start_{tag}", [gbuf[n] for n in names],
                                             [rxbuf[n] for n in names], items)
        gbuf.update(zip(names, send))
        rxbuf.update(zip(names, rx))
        pending.append((tag, names, items, sem))

    gsmall = {n: [None] * w[n].shape[0] for n in small_names}
    s5_cot = [None] * s5_a_re.shape[0]
    terms = [(h, 1.0 / d), (tgt, -1.0 / d)]
    for i in reversed(range(depth)):
        j = i // 2
        s1, sm, (xhat_m, rstd_m), s2 = saved[i]
        terms, gbuf["ffn2_w_in"], g3, dg, db = _ffn_bwd(
            f"l{i}_ffn2", alpha, terms, s2, full["ffn2_w_in"], rows3["ffn2_w_out"], i, ln2_g[i:i + 1] + zero,
            gbuf["ffn2_w_in"], _rows_view(gbuf["ffn2_w_out"]))
        gbuf["ffn2_w_out"] = g3.reshape(gshape["ffn2_w_out"])
        gsmall["ln2_g"][i], gsmall["ln2_b"][i] = dg, db
        scatter(f"l{i}_ffn2", [("ffn2_w_in", i), ("ffn2_w_out", i)])
        dz, dg, db = _ln_bwd(f"l{i}_lnm_bwd", terms, xhat_m, rstd_m, lnm_g[i:i + 1] + zero)
        gsmall["lnm_g"][i], gsmall["lnm_b"][i] = dg, db
        if i % 2 == 0:
            dx, d_wpad, dbf, g3 = _fox_bwd(f"l{i}_fox", dz, sm, fox_wpad(j), fox_b_f[j:j + 1],
                                           rows3["fox_w_o"], j, _rows_view(gbuf["fox_w_o"]))
            gbuf["fox_w_o"] = g3.reshape(gshape["fox_w_o"])
            gbuf[fox_in_names[j]] = d_wpad[:, :fox_cols].reshape(d, N_CHIPS, -1).transpose(1, 0, 2)[None].astype(BF16)
            gsmall["fox_b_f"][j] = dbf
            scatter(f"l{i}_fox", [("fox_w_o", j), (fox_in_names[j], 0)])
        else:
            dx, gbuf["s5_w_out"], s5_cot[j] = _s5_bwd(f"l{i}_s5", dz, sm, s5_params(j), full["s5_w_out"], j,
                                                      gbuf["s5_w_out"])
            scatter(f"l{i}_s5", [("s5_w_out", j)])
        terms = [(dz, alpha), (dx, 1.0)]
        terms, gbuf["ffn1_w_in"], g3, dg, db = _ffn_bwd(
            f"l{i}_ffn1", alpha, terms, s1, full["ffn1_w_in"], rows3["ffn1_w_out"], i, ln1_g[i:i + 1] + zero,
            gbuf["ffn1_w_in"], _rows_view(gbuf["ffn1_w_out"]))
        gbuf["ffn1_w_out"] = g3.reshape(gshape["ffn1_w_out"])
        gsmall["ln1_g"][i], gsmall["ln1_b"][i] = dg, db
        scatter(f"l{i}_ffn1", [("ffn1_w_in", i), ("ffn1_w_out", i)])
    grad_x = _lincomb("grad_x", terms).reshape(x.shape)

    cot_names = ["dlb_re", "dlb_im", "dbb_re", "dbb_im", "dc_re", "dc_im", "dd"]
    ln_names = ["ln1_g", "ln1_b", "lnm_g", "lnm_b", "ln2_g", "ln2_b"]
    pieces = [loss_part + zero] + [jnp.concatenate(gsmall[n], axis=0) for n in ln_names + ["fox_b_f"]]
    pieces += [jnp.stack([s5_cot[j][n] for j in range(len(s5_cot))]) for n in cot_names]
    shapes = [p.shape for p in pieces]
    mine = _pack(pieces)
    everyone = _gather_all("small_gather", mine).reshape(N_DEV, *mine.shape)
    summed = _unpack(_sum_leading("small_sum", everyone), shapes)
    loss = summed[0].reshape(())
    gs_final = dict(zip(ln_names + ["fox_b_f"], summed[1:8]))
    cot = dict(zip(cot_names, summed[8:]))
    prm_names = ["s5_a_re", "s5_a_im", "s5_log_dt", "s5_b_re", "s5_b_im"]
    _, disc_vjp = jax.vjp(jax.vmap(_s5_discretise), *[w[n] for n in prm_names])
    for n, gval in zip(prm_names, disc_vjp((cot["dlb_re"], cot["dlb_im"], cot["dbb_re"], cot["dbb_im"]))):
        gs_final[n] = gval
    gs_final["s5_c_re"], gs_final["s5_c_im"], gs_final["s5_d"] = cot["dc_re"], cot["dc_im"], cot["dd"]

    grads, deltas, new_m, new_v = {}, {}, {}, {}
    small_shapes = [w[n].shape for n in small_names]
    for n in small_names:
        grads[n] = gs_final[n].reshape(w[n].shape)
    packed = [_pack([src[n] for n in small_names]) for src in (w, grads, mom, vel)]
    small_out = _adamw("adamw_small", *packed)
    for dst, buf in zip((deltas, new_m, new_v), small_out):
        for n, val in zip(small_names, _unpack(buf, small_shapes)):
            dst[n] = val

    for tag, names, items, sem in pending:
        send, rx = _scatter_wait(f"scatter_wait_{tag}", [gbuf[n] for n in names], [rxbuf[n] for n in names],
                                 sem[0], sem[1], small_out[0], items)
        gbuf.update(zip(names, send))
        rxbuf.update(zip(names, rx))
    half = {n: _chip_sum(f"grad_chip_sum_{n}", gbuf[n], rxbuf[n]) for n in gshape}
    half["fox_w_in"] = jnp.concatenate([half[n] for n in fox_in_names], axis=0)
    halves = [half[n] for n in big_names]
    theirs = _send_half("grad_send_half", halves)
    for n, mine_h, their_h in zip(big_names, halves, theirs):
        grads[n], deltas[n], new_m[n], new_v[n] = _adamw_join(f"adamw_{n}", w[n], mine_h, their_h, mom[n], vel[n])
    return (loss, grad_x, *[grads[n] for n in out_order], *[deltas[n] for n in out_order],
            *[new_m[n] for n in out_order], *[new_v[n] for n in out_order])
```

```python
import functools
import math

import jax
import jax.numpy as jnp
from jax import lax
from jax.experimental import pallas as pl
from jax.experimental.pallas import tpu as pltpu

F32 = jnp.float32
BF16 = jnp.bfloat16
LN_EPS = 1e-5
NEG_INF = -1e30
ADAM_LR = 0.001
ADAM_B1 = 0.9
ADAM_B2 = 0.999
ADAM_EPS = 1e-08
ADAM_WD = 0.01
ADAM_STEP = 10
S5_GROUP = 16
SCAN_SEGMENTS = 32
V7X_SUBLANES = 8
V7X_LANES = 128
VMEM_LIMIT = 56 * 1024 * 1024
N_CHIPS = 4
N_DEV = 8
MESH = pl.DeviceIdType.MESH
ANY = pl.BlockSpec(memory_space=pl.ANY)


def _cp(n_grid, kaxis=None):
    sem = tuple("arbitrary" if (kaxis is None or i == kaxis) else "parallel" for i in range(n_grid))
    return pltpu.CompilerParams(dimension_semantics=sem, vmem_limit_bytes=VMEM_LIMIT)


def _tile(n, pref, mult=V7X_SUBLANES):
    if n <= pref:
        return n
    for t in range(pref, 0, -1):
        if n % t == 0 and t % mult == 0:
            return t
    return n


_CONTRACT = {"nn": ((1,), (0,)), "nt": ((1,), (1,)), "tn": ((0,), (0,))}


def _mm(name, a, b, *, mode, grid, kaxis, a_blk, a_map, b_blk, b_map, o_shape, o_blk, o_map, o_dtype=F32, scale=None,
        into=None):
    nk = 1 if kaxis is None else grid[kaxis]
    assert kaxis is None or kaxis == len(grid) - 1
    dims = (_CONTRACT[mode], ((), ()))
    use_acc = nk > 1 and o_dtype != F32
    acc_shape = tuple(d for d in o_blk if d is not None)

    def body(a_ref, b_ref, *rest):
        o_ref, scratch = (rest[1], rest[2:]) if into is not None else (rest[0], rest[1:])
        p = lax.dot_general(a_ref[...].astype(BF16), b_ref[...].astype(BF16), dims, preferred_element_type=F32)
        if nk == 1:
            if scale is not None:
                p = p * scale
            o_ref[...] = p.astype(o_dtype)
            return
        acc = scratch[0] if use_acc else o_ref
        k = pl.program_id(kaxis)

        @pl.when(k == 0)
        def _():
            acc[...] = p

        @pl.when(k > 0)
        def _():
            acc[...] += p

        if use_acc or scale is not None:
            @pl.when(k == nk - 1)
            def _():
                r = acc[...]
                if scale is not None:
                    r = r * scale
                o_ref[...] = r.astype(o_dtype)

    in_specs = [pl.BlockSpec(a_blk, a_map), pl.BlockSpec(b_blk, b_map)]
    args = [a, b]
    if into is not None:
        assert into.shape == tuple(o_shape) and into.dtype == o_dtype
        in_specs.append(ANY)
        args.append(into)
    return pl.pallas_call(
        body, grid=grid, name=name, in_specs=in_specs,
        out_specs=pl.BlockSpec(o_blk, o_map),
        out_shape=jax.ShapeDtypeStruct(o_shape, o_dtype),
        input_output_aliases={2: 0} if into is not None else {},
        scratch_shapes=[pltpu.VMEM(acc_shape, F32)] if use_acc else [],
        compiler_params=_cp(len(grid), kaxis),
    )(*args)


def _mm_shards_nn(name, a, wall, layer, o_dtype):
    t, k = a.shape
    _, s, _, n = wall.shape
    tm = _tile(t, 512)
    return _mm(name, a, wall, mode="nn", grid=(s, t // tm), kaxis=None,
               a_blk=(tm, k), a_map=lambda j, i: (i, 0),
               b_blk=(None, None, k, n), b_map=lambda j, i: (layer, j, 0, 0),
               o_shape=(t, s * n), o_blk=(tm, n), o_map=lambda j, i: (i, j), o_dtype=o_dtype)


def _mm_shards_nt(name, g, wall, layer):
    t = g.shape[0]
    _, s, k, n = wall.shape
    tm = _tile(t, 512)
    return _mm(name, g, wall, mode="nt", grid=(t // tm, s), kaxis=1,
               a_blk=(tm, n), a_map=lambda i, kk: (i, kk),
               b_blk=(None, None, k, n), b_map=lambda i, kk: (layer, kk, 0, 0),
               o_shape=(t, k), o_blk=(tm, k), o_map=lambda i, kk: (i, 0))


def _mm_shards_tn(name, a, g, layer, into):
    t, k = a.shape
    _, s, _, n = into.shape
    tk = _tile(t, 512)
    return _mm(name, a, g, mode="tn", grid=(s, t // tk), kaxis=1,
               a_blk=(tk, k), a_map=lambda j, kk: (kk, 0),
               b_blk=(tk, n), b_map=lambda j, kk: (kk, j),
               o_shape=into.shape, o_blk=(None, None, k, n), o_map=lambda j, kk: (layer, j, 0, 0),
               o_dtype=into.dtype, into=into)


def _mm_nn(name, a, w, o_dtype=F32, tn=None):
    t, k = a.shape
    n = w.shape[1]
    tm = _tile(t, 512)
    tn = n if tn is None else tn
    return _mm(name, a, w, mode="nn", grid=(n // tn, t // tm), kaxis=None,
               a_blk=(tm, k), a_map=lambda j, i: (i, 0),
               b_blk=(k, tn), b_map=lambda j, i: (0, j),
               o_shape=(t, n), o_blk=(tm, tn), o_map=lambda j, i: (i, j), o_dtype=o_dtype)


def _mm_nt(name, g, w, layer=None, o_dtype=F32):
    t, k = g.shape
    n = w.shape[-2]
    tm = _tile(t, 512)
    b_blk, b_map = ((n, k), lambda i: (0, 0)) if layer is None else ((None, n, k), lambda i: (layer, 0, 0))
    return _mm(name, g, w, mode="nt", grid=(t // tm,), kaxis=None,
               a_blk=(tm, k), a_map=lambda i: (i, 0), b_blk=b_blk, b_map=b_map,
               o_shape=(t, n), o_blk=(tm, n), o_map=lambda i: (i, 0), o_dtype=o_dtype)


def _mm_tn(name, a, g, tm=None, tn=None, scale=None, layer=None, into=None):
    t, m = a.shape
    n = g.shape[1]
    tk = _tile(t, 512)
    tm = m if tm is None else tm
    tn = n if tn is None else tn
    if layer is None:
        o_shape, o_blk, o_map, o_dtype = (m, n), (tm, tn), lambda i, j, kk: (i, j), F32
    else:
        o_shape, o_blk, o_map, o_dtype = into.shape, (None, tm, tn), lambda i, j, kk: (layer, i, j), into.dtype
    return _mm(name, a, g, mode="tn", grid=(m // tm, n // tn, t // tk), kaxis=2,
               a_blk=(tk, tm), a_map=lambda i, j, kk: (kk, i),
               b_blk=(tk, tn), b_map=lambda i, j, kk: (kk, j),
               o_shape=o_shape, o_blk=o_blk, o_map=o_map, o_dtype=o_dtype, scale=scale, into=into)


def _sigmoid(x):
    return 1.0 / (1.0 + jnp.exp(-x))


def _rows_call(name, body, t, tm, ins, in_cols, outs, acc_outs=()):
    in_specs = []
    for x, c in zip(ins, in_cols):
        if x.shape[0] == 1:
            in_specs.append(pl.BlockSpec((1, c), lambda i: (0, 0)))
        else:
            in_specs.append(pl.BlockSpec((tm, c), lambda i: (i, 0)))
    out_specs = [pl.BlockSpec((tm, s.shape[1]), lambda i: (i, 0)) for s in outs]
    out_specs += [pl.BlockSpec((1, s.shape[1]), lambda i: (0, 0)) for s in acc_outs]
    return pl.pallas_call(
        body, grid=(t // tm,), name=name, in_specs=in_specs, out_specs=out_specs,
        out_shape=list(outs) + list(acc_outs), compiler_params=_cp(1),
    )(*ins)


def _ln_fwd(name, alpha, x, r, coef, g, b):
    t, d = x.shape
    tm = _tile(t, 256)

    def body(x_ref, r_ref, g_ref, b_ref, y_ref, xh_ref, rs_ref):
        z = alpha * x_ref[...] + coef * r_ref[...]
        mu = jnp.mean(z, axis=-1, keepdims=True)
        zc = z - mu
        var = jnp.mean(zc * zc, axis=-1, keepdims=True)
        rstd = lax.rsqrt(var + LN_EPS)
        xh = zc * rstd
        y_ref[...] = xh * g_ref[...] + b_ref[...]
        xh_ref[...] = xh
        rs_ref[...] = rstd

    sd = jax.ShapeDtypeStruct
    return _rows_call(name, body, t, tm, [x, r, g, b], [d, d, d, d],
                      [sd((t, d), F32), sd((t, d), F32), sd((t, 1), F32)])


def _ln_bwd(name, terms, xhat, rstd, g):
    t, d = xhat.shape
    tm = _tile(t, 256)
    n = len(terms)
    coefs = [c for _, c in terms]

    def body(*refs):
        t_refs = refs[:n]
        xh_ref, rs_ref, g_ref, dz_ref, dg_ref, db_ref = refs[n:]
        dy = coefs[0] * t_refs[0][...]
        for c, r in zip(coefs[1:], t_refs[1:]):
            dy = dy + c * r[...]
        xh = xh_ref[...]
        dxh = dy * g_ref[...]
        m1 = jnp.mean(dxh, axis=-1, keepdims=True)
        m2 = jnp.mean(dxh * xh, axis=-1, keepdims=True)
        dz_ref[...] = rs_ref[...] * (dxh - m1 - xh * m2)
        pg = jnp.sum(dy * xh, axis=0, keepdims=True)
        pb = jnp.sum(dy, axis=0, keepdims=True)
        i = pl.program_id(0)

        @pl.when(i == 0)
        def _():
            dg_ref[...] = pg
            db_ref[...] = pb

        @pl.when(i > 0)
        def _():
            dg_ref[...] += pg
            db_ref[...] += pb

    sd = jax.ShapeDtypeStruct
    arrs = [a for a, _ in terms] + [xhat, rstd, g]
    cols = [d] * n + [d, 1, d]
    return _rows_call(name, body, t, tm, arrs, cols, [sd((t, d), F32)], [sd((1, d), F32), sd((1, d), F32)])


def _lincomb(name, terms):
    t, d = terms[0][0].shape
    tm = _tile(t, 256)
    coefs = [c for _, c in terms]
    n = len(terms)

    def body(*refs):
        acc = coefs[0] * refs[0][...]
        for c, r in zip(coefs[1:], refs[1:n]):
            acc = acc + c * r[...]
        refs[n][...] = acc

    return _rows_call(name, body, t, tm, [a for a, _ in terms], [d] * n, [jax.ShapeDtypeStruct((t, d), F32)])[0]


def _loss_sum(name, y, tgt):
    t, d = y.shape
    tm = _tile(t, 256)

    def body(y_ref, t_ref, o_ref):
        e = y_ref[...] - t_ref[...]
        s = jnp.sum(jnp.sum(e * e, axis=1, keepdims=True), axis=0, keepdims=True)
        i = pl.program_id(0)

        @pl.when(i == 0)
        def _():
            o_ref[...] = s

        @pl.when(i > 0)
        def _():
            o_ref[...] += s

    return _rows_call(name, body, t, tm, [y, tgt], [d, d], [], [jax.ShapeDtypeStruct((1, 1), F32)])[0]


def _ffn_in(name, x, wall, layer):
    t, k = x.shape
    n = wall.shape[3]
    tm = _tile(t, 512)

    def body(x_ref, wg_ref, wu_ref, h_ref, a_ref):
        xb = x_ref[...].astype(BF16)
        g = lax.dot_general(xb, wg_ref[...], _NN, preferred_element_type=F32)
        u = lax.dot_general(xb, wu_ref[...], _NN, preferred_element_type=F32)
        h_ref[0] = g.astype(BF16)
        h_ref[1] = u.astype(BF16)
        a_ref[...] = (g * _sigmoid(g) * u).astype(BF16)

    return pl.pallas_call(
        body, grid=(2, t // tm), name=name,
        in_specs=[pl.BlockSpec((tm, k), lambda j, i: (i, 0)),
                  pl.BlockSpec((None, None, k, n), lambda j, i: (layer, j, 0, 0)),
                  pl.BlockSpec((None, None, k, n), lambda j, i: (layer, 2 + j, 0, 0))],
        out_specs=[pl.BlockSpec((2, tm, n), lambda j, i: (0, i, j)), pl.BlockSpec((tm, n), lambda j, i: (i, j))],
        out_shape=[jax.ShapeDtypeStruct((2, t, 2 * n), BF16), jax.ShapeDtypeStruct((t, 2 * n), BF16)],
        compiler_params=_cp(2),
    )(x, wall, wall)


def _ffn_da(name, dz, w3, layer, h):
    t, k = dz.shape
    f = w3.shape[1]
    n = f // 2
    tm = _tile(t, 512)

    def body(dz_ref, w_ref, g_ref, u_ref, dh_ref):
        d = 0.5 * lax.dot_general(dz_ref[...].astype(BF16), w_ref[...], _NT, preferred_element_type=F32)
        g = g_ref[...].astype(F32)
        u = u_ref[...].astype(F32)
        sg = _sigmoid(g)
        dh_ref[0] = (d * u * sg * (1.0 + g * (1.0 - sg))).astype(BF16)
        dh_ref[1] = (d * g * sg).astype(BF16)

    return pl.pallas_call(
        body, grid=(2, t // tm), name=name,
        in_specs=[pl.BlockSpec((tm, k), lambda j, i: (i, 0)),
                  pl.BlockSpec((None, n, k), lambda j, i: (layer, j, 0)),
                  pl.BlockSpec((None, tm, n), lambda j, i: (0, i, j)),
                  pl.BlockSpec((None, tm, n), lambda j, i: (1, i, j))],
        out_specs=pl.BlockSpec((2, tm, n), lambda j, i: (0, i, j)),
        out_shape=jax.ShapeDtypeStruct((2, t, f), BF16),
        compiler_params=_cp(2),
    )(dz, w3, h, h)


def _mm_ln(name, alpha, coef, a, w3, layer, x, g, b):
    t, k = a.shape
    d = w3.shape[2]
    tm = _tile(t, 512)

    def body(a_ref, w_ref, x_ref, g_ref, b_ref, y_ref, xh_ref, rs_ref):
        f = lax.dot_general(a_ref[...].astype(BF16), w_ref[...], _NN, preferred_element_type=F32)
        z = alpha * x_ref[...] + coef * f
        mu = jnp.mean(z, axis=-1, keepdims=True)
        zc = z - mu
        var = jnp.mean(zc * zc, axis=-1, keepdims=True)
        rstd = lax.rsqrt(var + LN_EPS)
        xh = zc * rstd
        y_ref[...] = xh * g_ref[...] + b_ref[...]
        xh_ref[...] = xh
        rs_ref[...] = rstd

    row = lambda c: pl.BlockSpec((tm, c), lambda i: (i, 0))
    vec = pl.BlockSpec((1, d), lambda i: (0, 0))
    sd = jax.ShapeDtypeStruct
    return pl.pallas_call(
        body, grid=(t // tm,), name=name,
        in_specs=[row(k), pl.BlockSpec((None, k, d), lambda i: (layer, 0, 0)), row(d), vec, vec],
        out_specs=[row(d), row(d), row(1)],
        out_shape=[sd((t, d), F32), sd((t, d), F32), sd((t, 1), F32)],
        compiler_params=_cp(1),
    )(a, w3, x, g, b)


_GELU_C = math.sqrt(2.0 / math.pi)


def _s5_act_fwd(name, ych, u, dvec):
    t, d = u.shape
    tm = _tile(t, 256)

    def body(y_ref, u_ref, d_ref, p_ref, a_ref):
        y = y_ref[...] + d_ref[...] * u_ref[...]
        p_ref[...] = y
        a_ref[...] = (0.5 * y * (1.0 + jnp.tanh(_GELU_C * (y + 0.044715 * y * y * y)))).astype(BF16)

    sd = jax.ShapeDtypeStruct
    return _rows_call(name, body, t, tm, [ych, u, dvec], [d, d, d], [sd((t, d), F32), sd((t, d), BF16)])


def _s5_act_bwd(name, dact, ypre, u, dvec):
    t, d = u.shape
    tm = _tile(t, 256)

    def body(da_ref, y_ref, u_ref, d_ref, dy_ref, ds_ref, dd_ref):
        y = y_ref[...]
        th = jnp.tanh(_GELU_C * (y + 0.044715 * y * y * y))
        dg = 0.5 * (1.0 + th) + 0.5 * y * (1.0 - th * th) * _GELU_C * (1.0 + 3.0 * 0.044715 * y * y)
        dy = da_ref[...] * dg
        dy_ref[...] = dy
        ds_ref[...] = dy * d_ref[...]
        pd = jnp.sum(dy * u_ref[...], axis=0, keepdims=True)
        i = pl.program_id(0)

        @pl.when(i == 0)
        def _():
            dd_ref[...] = pd

        @pl.when(i > 0)
        def _():
            dd_ref[...] += pd

    sd = jax.ShapeDtypeStruct
    return _rows_call(name, body, t, tm, [dact, ypre, u, dvec], [d, d, d, d],
                      [sd((t, d), F32), sd((t, d), F32)], [sd((1, d), F32)])


def _glu_fwd(name, vg):
    t, d2 = vg.shape
    d = d2 // 2
    tm = _tile(t, 256)

    def body(vg_ref, m_ref):
        m_ref[...] = vg_ref[:, :d] * _sigmoid(vg_ref[:, d:])

    return _rows_call(name, body, t, tm, [vg], [d2], [jax.ShapeDtypeStruct((t, d), F32)])[0]


def _glu_bwd(name, dm, vg):
    t, d2 = vg.shape
    d = d2 // 2
    tm = _tile(t, 256)

    def body(dm_ref, vg_ref, o_ref):
        sg = _sigmoid(vg_ref[:, d:])
        g = dm_ref[...]
        o_ref[:, :d] = (g * sg).astype(BF16)
        o_ref[:, d:] = (g * vg_ref[:, :d] * sg * (1.0 - sg)).astype(BF16)

    return _rows_call(name, body, t, tm, [dm, vg], [d, d2], [jax.ShapeDtypeStruct((t, d2), BF16)])[0]


def _adamw(name, w, g, m, v):
    r, c = w.shape
    tr = _tile(r, max(V7X_SUBLANES, (1 << 20) // (4 * c) // V7X_SUBLANES * V7X_SUBLANES))

    def body(w_ref, g_ref, m_ref, v_ref, d_ref, nm_ref, nv_ref):
        gg = g_ref[...]
        nm = ADAM_B1 * m_ref[...] + (1.0 - ADAM_B1) * gg
        nv = ADAM_B2 * v_ref[...] + (1.0 - ADAM_B2) * (gg * gg)
        m_hat = nm / (1.0 - ADAM_B1 ** ADAM_STEP)
        v_hat = nv / (1.0 - ADAM_B2 ** ADAM_STEP)
        d_ref[...] = -ADAM_LR * (m_hat / (jnp.sqrt(v_hat) + ADAM_EPS) + ADAM_WD * w_ref[...])
        nm_ref[...] = nm
        nv_ref[...] = nv

    sd = jax.ShapeDtypeStruct((r, c), F32)
    return _rows_call(name, body, r, tr, [w, g, m, v], [c] * 4, [sd, sd, sd])


def _my_shard():
    return 2 * lax.axis_index("x") + lax.axis_index("y")


def _my_core():
    return lax.axis_index("c")


def _adamw_join(name, w, mine, theirs, m, v):
    nl, r, c = w.shape
    h = r // 2
    tr = _tile(h, max(V7X_SUBLANES, (1 << 19) // (4 * c) // V7X_SUBLANES * V7X_SUBLANES))
    nb = h // tr

    def body(w_ref, a_ref, b_ref, m_ref, v_ref, g_ref, d_ref, nm_ref, nv_ref):
        gg = jnp.where(pl.program_id(1) == _my_core(), a_ref[...], b_ref[...])
        nm = ADAM_B1 * m_ref[...] + (1.0 - ADAM_B1) * gg
        nv = ADAM_B2 * v_ref[...] + (1.0 - ADAM_B2) * (gg * gg)
        m_hat = nm / (1.0 - ADAM_B1 ** ADAM_STEP)
        v_hat = nv / (1.0 - ADAM_B2 ** ADAM_STEP)
        g_ref[...] = gg
        d_ref[...] = -ADAM_LR * (m_hat / (jnp.sqrt(v_hat) + ADAM_EPS) + ADAM_WD * w_ref[...])
        nm_ref[...] = nm
        nv_ref[...] = nv

    full = pl.BlockSpec((None, tr, c), lambda l, hf, i: (l, hf * nb + i, 0))
    sd = jax.ShapeDtypeStruct((nl, r, c), F32)
    return pl.pallas_call(
        body, name=name, grid=(nl, 2, nb),
        in_specs=[full,
                  pl.BlockSpec((None, tr, c), lambda l, hf, i: (l, jnp.where(hf == _my_core(), i, 0), 0)),
                  pl.BlockSpec((None, tr, c), lambda l, hf, i: (l, jnp.where(hf == _my_core(), 0, i), 0)),
                  full, full],
        out_specs=[full, full, full, full],
        out_shape=[sd, sd, sd, sd],
        compiler_params=_cp(3),
    )(w, mine, theirs, m, v)


def _sum_leading(name, a):
    n, r, c = a.shape
    tr = _tile(r, 512)

    def body(a_ref, o_ref):
        acc = a_ref[0]
        for k in range(1, n):
            acc = acc + a_ref[k]
        o_ref[...] = acc

    return pl.pallas_call(
        body, grid=(r // tr,), name=name,
        in_specs=[pl.BlockSpec((n, tr, c), lambda i: (0, i, 0))],
        out_specs=pl.BlockSpec((tr, c), lambda i: (i, 0)),
        out_shape=jax.ShapeDtypeStruct((r, c), F32), compiler_params=_cp(1),
    )(a)


def _split3(x):
    hi = x.astype(BF16)
    r1 = x - hi.astype(F32)
    mid = r1.astype(BF16)
    lo = (r1 - mid.astype(F32)).astype(BF16)
    return hi, mid, lo


def _tri_sum(tri, x):
    dims = (((1,), (0,)), ((), ()))
    hi, mid, lo = _split3(x)
    out = lax.dot_general(tri, lo, dims, preferred_element_type=F32)
    out = out + lax.dot_general(tri, mid, dims, preferred_element_type=F32)
    return out + lax.dot_general(tri, hi, dims, preferred_element_type=F32)


def _fox_cumsum(name, fl, bf):
    t, h = fl.shape
    tb = _tile(t, 512)

    def body(fl_ref, bf_ref, c_ref, carry):
        i = pl.program_id(0)

        @pl.when(i == 0)
        def _():
            carry[...] = jnp.zeros_like(carry)

        x = fl_ref[...] + bf_ref[...]
        lf = jnp.minimum(x, 0.0) - jnp.log(1.0 + jnp.exp(-jnp.abs(x)))
        row = lax.broadcasted_iota(jnp.int32, (tb, tb), 0)
        col = lax.broadcasted_iota(jnp.int32, (tb, tb), 1)
        tri = jnp.where(row >= col, 1.0, 0.0).astype(BF16)
        c_ref[...] = _tri_sum(tri, lf) + carry[...]
        carry[...] += jnp.sum(lf, axis=0, keepdims=True)

    return pl.pallas_call(
        body, grid=(t // tb,), name=name,
        in_specs=[pl.BlockSpec((tb, h), lambda i: (i, 0)), pl.BlockSpec((1, h), lambda i: (0, 0))],
        out_specs=pl.BlockSpec((tb, h), lambda i: (i, 0)),
        out_shape=jax.ShapeDtypeStruct((t, h), F32),
        scratch_shapes=[pltpu.VMEM((1, h), F32)], compiler_params=_cp(1),
    )(fl, bf)


def _fox_cumsum_bwd(name, dcum, fl, bf):
    t, h = fl.shape
    tb = _tile(t, 512)
    nb = t // tb

    def body(dc_ref, fl_ref, bf_ref, df_ref, db_ref, carry):
        i = pl.program_id(0)

        @pl.when(i == 0)
        def _():
            carry[...] = jnp.zeros_like(carry)

        dc = dc_ref[...]
        row = lax.broadcasted_iota(jnp.int32, (tb, tb), 0)
        col = lax.broadcasted_iota(jnp.int32, (tb, tb), 1)
        tri = jnp.where(row <= col, 1.0, 0.0).astype(BF16)
        dlf = _tri_sum(tri, dc) + carry[...]
        carry[...] += jnp.sum(dc, axis=0, keepdims=True)
        x = fl_ref[...] + bf_ref[...]
        df = dlf / (1.0 + jnp.exp(x))
        df_ref[...] = df
        pb = jnp.sum(df, axis=0, keepdims=True)

        @pl.when(i == 0)
        def _():
            db_ref[...] = pb

        @pl.when(i > 0)
        def _():
            db_ref[...] += pb

    rev = lambda i: (nb - 1 - i, 0)
    return pl.pallas_call(
        body, grid=(nb,), name=name,
        in_specs=[pl.BlockSpec((tb, h), rev), pl.BlockSpec((tb, h), rev), pl.BlockSpec((1, h), lambda i: (0, 0))],
        out_specs=[pl.BlockSpec((tb, h), rev), pl.BlockSpec((1, h), lambda i: (0, 0))],
        out_shape=[jax.ShapeDtypeStruct((t, h), F32), jax.ShapeDtypeStruct((1, h), F32)],
        scratch_shapes=[pltpu.VMEM((1, h), F32)], compiler_params=_cp(1),
    )(dcum, fl, bf)


_NT = (((1,), (1,)), ((), ()))
_TN = (((0,), (0,)), ((), ()))
_NN = (((1,), (0,)), ((), ()))


def _causal_mask(s, tb):
    row = lax.broadcasted_iota(jnp.int32, (tb, tb), 0)
    col = lax.broadcasted_iota(jnp.int32, (tb, tb), 1)
    return jnp.where(col <= row, s, NEG_INF)


def _first_head_lanes(hd):
    return lax.broadcasted_iota(jnp.int32, (1, 2 * hd), 1) < hd


def _attn_fwd(name, qkv, ccol, crow, nh):
    nb, tb, d3 = qkv.shape
    d = d3 // 3
    hd = d // nh
    lanes = 2 * hd
    assert lanes == V7X_LANES
    scale = 1.0 / math.sqrt(hd)

    def body(q_ref, k_ref, v_ref, cc_ref, cr_ref, o_ref, lse_ref):
        i = pl.program_id(1)
        first = _first_head_lanes(hd)
        q = q_ref[...] * scale
        res = []
        for hh in (0, 1):
            qh = jnp.where(first if hh == 0 else jnp.logical_not(first), q, jnp.zeros_like(q))
            cc = cc_ref[:, hh:hh + 1]

            def step(j, carry, diagonal=False, qh=qh, cc=cc, hh=hh):
                m, l, acc = carry
                s = lax.dot_general(qh, k_ref[j], _NT, preferred_element_type=F32) + cc - cr_ref[j][hh:hh + 1, :]
                if diagonal:
                    s = _causal_mask(s, tb)
                m_new = jnp.maximum(m, jnp.max(s, axis=1, keepdims=True))
                p = jnp.exp(s - m_new)
                a = jnp.exp(m - m_new)
                l = a * l + jnp.sum(p, axis=1, keepdims=True)
                acc = a * acc + lax.dot_general(p.astype(BF16), v_ref[j], _NN, preferred_element_type=F32)
                return m_new, l, acc

            init = (jnp.full((tb, 1), NEG_INF, F32), jnp.zeros((tb, 1), F32), jnp.zeros((tb, lanes), F32))
            m, l, acc = step(i, lax.fori_loop(0, i, step, init), diagonal=True)
            res.append((acc / l, m + jnp.log(l)))
        o_ref[...] = jnp.where(first, res[0][0], res[1][0])
        lse_ref[:, 0:1] = res[0][1]
        lse_ref[:, 1:2] = res[1][1]

    kb, vb = d // lanes, 2 * d // lanes
    return pl.pallas_call(
        body, grid=(nh // 2, nb), name=name,
        in_specs=[pl.BlockSpec((None, tb, lanes), lambda h, i: (i, 0, h)),
                  pl.BlockSpec((nb, tb, lanes), lambda h, i: (0, 0, kb + h)),
                  pl.BlockSpec((nb, tb, lanes), lambda h, i: (0, 0, vb + h)),
                  pl.BlockSpec((None, None, tb, 2), lambda h, i: (h, i, 0, 0)),
                  pl.BlockSpec((None, nb, 2, tb), lambda h, i: (h, 0, 0, 0))],
        out_specs=[pl.BlockSpec((None, tb, lanes), lambda h, i: (i, 0, h)),
                   pl.BlockSpec((None, None, tb, 2), lambda h, i: (h, i, 0, 0))],
        out_shape=[jax.ShapeDtypeStruct((nb, tb, d), F32), jax.ShapeDtypeStruct((nh // 2, nb, tb, 2), F32)],
        compiler_params=_cp(2),
    )(qkv, qkv, qkv, ccol, crow)


def _attn_bwd(name, qkv, ccol, crow, o, lse, do, nh):
    nb, tb, d3 = qkv.shape
    d = d3 // 3
    hd = d // nh
    lanes = 2 * hd
    scale = 1.0 / math.sqrt(hd)

    def body(q_ref, k_ref, v_ref, cc_ref, cr_ref, o_ref, lse_ref, do_ref, dq_ref, dk_ref, dv_ref, dr_ref, dc_ref, dq_acc):
        j = pl.program_id(1)

        @pl.when(j == 0)
        def _():
            dq_acc[...] = jnp.zeros_like(dq_acc)
            dr_ref[...] = jnp.zeros_like(dr_ref)

        first = _first_head_lanes(hd)
        kj = k_ref[...]
        vj = v_ref[...]
        dk = jnp.zeros((tb, lanes), F32)
        dv = jnp.zeros((tb, lanes), F32)
        for hh in (0, 1):
            mine = first if hh == 0 else jnp.logical_not(first)
            cr = cr_ref[hh:hh + 1, :]

            def step(i, carry, diagonal=False, mine=mine, cr=cr, hh=hh):
                dk, dv, dc = carry
                qi = q_ref[i] * scale
                qh = jnp.where(mine, qi, jnp.zeros_like(qi))
                doh = jnp.where(mine, do_ref[i], 0.0)
                dob = doh.astype(BF16)
                di = jnp.sum(doh * o_ref[i], axis=1, keepdims=True)
                s = lax.dot_general(qh, kj, _NT, preferred_element_type=F32) + cc_ref[i][:, hh:hh + 1] - cr
                if diagonal:
                    s = _causal_mask(s, tb)
                p = jnp.exp(s - lse_ref[i][:, hh:hh + 1])
                dv = dv + lax.dot_general(p.astype(BF16), dob, _TN, preferred_element_type=F32)
                dp = lax.dot_general(dob, vj, _NT, preferred_element_type=F32)
                ds = p * (dp - di)
                dsb = ds.astype(BF16)
                dk = dk + lax.dot_general(dsb, qh, _TN, preferred_element_type=F32)
                dq = lax.dot_general(dsb, kj, _NN, preferred_element_type=F32) * scale
                dq_acc[i] += jnp.where(mine, dq, 0.0)
                dr_ref[i, :, hh:hh + 1] += jnp.sum(ds, axis=1, keepdims=True)
                dc = dc + jnp.sum(ds, axis=0, keepdims=True)
                return dk, dv, dc

            dk, dv, dc = lax.fori_loop(j + 1, nb, step, step(j, (dk, dv, jnp.zeros((1, tb), F32)), diagonal=True))
            dc_ref[hh:hh + 1, :] = dc
        dk_ref[...] = dk.astype(BF16)
        dv_ref[...] = dv.astype(BF16)

        @pl.when(j == nb - 1)
        def _():
            dq_ref[...] = dq_acc[...].astype(BF16)

    kb, vb = d // lanes, 2 * d // lanes
    whole = lambda c: pl.BlockSpec((nb, tb, lanes), lambda h, j: (0, 0, c + h))
    block = lambda c: pl.BlockSpec((None, tb, lanes), lambda h, j: (j, 0, c + h))
    cols = pl.BlockSpec((None, nb, tb, 2), lambda h, j: (h, 0, 0, 0))
    rows = pl.BlockSpec((None, None, 2, tb), lambda h, j: (h, j, 0, 0))
    sd = jax.ShapeDtypeStruct
    return pl.pallas_call(
        body, grid=(nh // 2, nb), name=name,
        in_specs=[whole(0), block(kb), block(vb), cols, rows, whole(0), cols, whole(0)],
        out_specs=[whole(0), block(0), block(0), cols, rows],
        out_shape=[sd((nb, tb, d), BF16), sd((nb, tb, d), BF16), sd((nb, tb, d), BF16),
                   sd((nh // 2, nb, tb, 2), F32), sd((nh // 2, nb, 2, tb), F32)],
        scratch_shapes=[pltpu.VMEM((nb, tb, lanes), F32)],
        compiler_params=_cp(2),
    )(qkv, qkv, qkv, ccol, crow, o, lse, do)


def _cmul(ar, ai, br, bi):
    return ar * br - ai * bi, ar * bi + ai * br


def _s5_scan(name, lam, xin, hs=None):
    reverse = hs is not None
    _, seg, ns, w = xin.shape
    assert ns == SCAN_SEGMENTS
    wb = min(w, 2 * V7X_LANES)
    nsq = seg.bit_length() - 1
    assert (1 << nsq) == seg

    def body(*refs):
        if reverse:
            lam_ref, x_ref, h_ref, o_ref, dl_ref = refs
        else:
            lam_ref, x_ref, o_ref = refs
        lr = jnp.broadcast_to(lam_ref[0], (ns, wb))
        li = jnp.broadcast_to(lam_ref[1], (ns, wb))
        if reverse:
            li = -li
        zero = jnp.zeros((ns, wb), F32)
        at = (lambda n: seg - 1 - n) if reverse else (lambda n: n)

        def local(n, c):
            r = at(n)
            mr, mi = _cmul(lr, li, c[0], c[1])
            nr = mr + x_ref[0, r]
            ni = mi + x_ref[1, r]
            o_ref[0, r] = nr
            o_ref[1, r] = ni
            return nr, ni

        er, ei = lax.fori_loop(0, seg, local, (zero, zero))
        pr, pi = lr, li
        for _ in range(nsq):
            pr, pi = _cmul(pr, pi, pr, pi)
        sub = lax.broadcasted_iota(jnp.int32, (ns, wb), 0)

        def shifted(a, sh):
            if reverse:
                return jnp.where(sub < ns - sh, pltpu.roll(a, ns - sh, 0), 0.0)
            return jnp.where(sub >= sh, pltpu.roll(a, sh, 0), 0.0)

        xr, xi = er, ei
        sh = 1
        while sh < ns:
            tr, ti = _cmul(pr, pi, shifted(xr, sh), shifted(xi, sh))
            xr, xi = xr + tr, xi + ti
            pr, pi = _cmul(pr, pi, pr, pi)
            sh *= 2
        cr, ci = shifted(xr, 1), shifted(xi, 1)

        def fix(r, q):
            tr, ti = _cmul(q[0], q[1], cr, ci)
            gr = o_ref[0, r] + tr
            gi = o_ref[1, r] + ti
            o_ref[0, r] = gr
            o_ref[1, r] = gi
            return gr, gi

        if not reverse:
            def fixup(n, q):
                fix(n, q)
                return _cmul(q[0], q[1], lr, li)

            lax.fori_loop(0, seg, fixup, (lr, li))
            return

        def fixup_acc(n, c):
            qr, qi, ar, ai = c
            r = seg - 1 - n
            gr, gi = fix(r, (qr, qi))
            hr = h_ref[0, r - 1]
            hi = h_ref[1, r - 1]
            qr, qi = _cmul(qr, qi, lr, li)
            return qr, qi, ar + gr * hr + gi * hi, ai + gi * hr - gr * hi

        qr, qi, ar, ai = lax.fori_loop(0, seg - 1, fixup_acc, (lr, li, zero, zero))
        gr, gi = fix(0, (qr, qi))
        hr = jnp.where(sub >= 1, pltpu.roll(h_ref[0, seg - 1], 1, 0), 0.0)
        hi = jnp.where(sub >= 1, pltpu.roll(h_ref[1, seg - 1], 1, 0), 0.0)
        dl_ref[0] = ar + gr * hr + gi * hi
        dl_ref[1] = ai + gi * hr - gr * hi

    big = pl.BlockSpec((2, seg, ns, wb), lambda j: (0, 0, 0, j))
    lam_spec = pl.BlockSpec((2, 1, wb), lambda j: (0, 0, j))
    sd = jax.ShapeDtypeStruct
    if reverse:
        return pl.pallas_call(
            body, grid=(w // wb,), name=name, in_specs=[lam_spec, big, big],
            out_specs=[big, pl.BlockSpec((2, ns, wb), lambda j: (0, 0, j))],
            out_shape=[sd(xin.shape, F32), sd((2, ns, w), F32)], compiler_params=_cp(1),
        )(lam, xin, hs)
    return pl.pallas_call(
        body, grid=(w // wb,), name=name, in_specs=[lam_spec, big], out_specs=big,
        out_shape=sd(xin.shape, F32), compiler_params=_cp(1),
    )(lam, xin)


def _place():
    x, y, c = lax.axis_index("x"), lax.axis_index("y"), lax.axis_index("c")
    chips = [(1 - x, y), (x, 1 - y), (1 - x, 1 - y)]
    return x, y, c, chips


def _comm_params():
    return pltpu.CompilerParams(vmem_limit_bytes=VMEM_LIMIT)


def _cast_place(name, w):
    nl, r, c = w.shape
    tr = _tile(r, max(16, (1 << 20) // (4 * c) // 16 * 16), 16)

    def body(w_ref, o_ref):
        o_ref[...] = w_ref[...].astype(BF16)

    return pl.pallas_call(
        body, name=name, grid=(nl, r // tr),
        in_specs=[pl.BlockSpec((None, tr, c), lambda l, i: (l, i, 0))],
        out_specs=pl.BlockSpec((None, None, tr, c), lambda l, i: (l, _my_shard(), i, 0)),
        out_shape=jax.ShapeDtypeStruct((nl, N_CHIPS, r, c), BF16),
        compiler_params=_cp(2),
    )(w)


def _gather_shards(name, bufs):
    n = len(bufs)

    def body(*refs):
        outs = refs[n:2 * n]
        send_sems, recv_sems = refs[2 * n:]
        x, y, c, chips = _place()
        my = 2 * x + y
        sibling = (x, y, 1 - c)

        def part(t, shard, half):
            h = bufs[t].shape[2] // 2
            return outs[t].at[:, shard, pl.ds(half * h, h)]

        def copy(t, k, ref, to):
            return pltpu.make_async_remote_copy(src_ref=ref, dst_ref=ref, send_sem=send_sems.at[t, k],
                                                recv_sem=recv_sems.at[t, k], device_id=to, device_id_type=MESH)

        sent = []
        for t in range(n):
            for k, chip in enumerate(chips):
                sent.append(copy(t, k, part(t, my, c), (*chip, c)))
                sent[-1].start()
        for k, chip in enumerate(chips):
            shard = 2 * chip[0] + chip[1]
            for t in range(n):
                copy(t, k, part(t, shard, c), (*chip, c)).wait_recv()
                sent.append(copy(t, 3 + k, part(t, shard, c), sibling))
                sent[-1].start()
        for k, chip in enumerate(chips):
            shard = 2 * chip[0] + chip[1]
            for t in range(n):
                copy(t, 3 + k, part(t, shard, 1 - c), sibling).wait_recv()
        for cp in sent:
            cp.wait_send()

    return pl.pallas_call(
        body, name=name, in_specs=[ANY] * n, out_specs=[ANY] * n,
        out_shape=[jax.ShapeDtypeStruct(b.shape, b.dtype) for b in bufs],
        input_output_aliases={t: t for t in range(n)},
        scratch_shapes=[pltpu.SemaphoreType.DMA((n, 6)), pltpu.SemaphoreType.DMA((n, 6))],
        compiler_params=_comm_params(),
    )(*bufs)


HBM_SPEC = pl.BlockSpec(memory_space=pltpu.HBM)
SEM_SPEC = pl.BlockSpec(memory_space=pltpu.SEMAPHORE)


def _split_params():
    return pltpu.CompilerParams(has_side_effects=pltpu.SideEffectType.DATAFLOW_SIDE_EFFECTING)


def _gather_start(name, bufs, groups):
    n, ng = len(bufs), len(groups)

    def body(*refs):
        sems = refs[n:n + 2 * ng]
        outs = refs[n + 2 * ng:]
        x, y, c, chips = _place()
        my = 2 * x + y
        for gi, group in enumerate(groups):
            for idx, (t, layer) in enumerate(group):
                block = outs[t].at[layer, my]
                for k, chip in enumerate(chips):
                    pltpu.make_async_remote_copy(
                        src_ref=block, dst_ref=block, send_sem=sems[2 * gi].at[3 * idx + k],
                        recv_sem=sems[2 * gi + 1].at[3 * idx + k], device_id=(*chip, c), device_id_type=MESH).start()

    sem_shapes = []
    for group in groups:
        sem_shapes += [pltpu.SemaphoreType.DMA((3 * len(group),))] * 2
    res = pl.pallas_call(
        body, name=name, in_specs=[HBM_SPEC] * n,
        out_specs=[SEM_SPEC] * (2 * ng) + [HBM_SPEC] * n,
        out_shape=sem_shapes + [pltpu.HBM(b.shape, b.dtype) for b in bufs],
        input_output_aliases={t: 2 * ng + t for t in range(n)},
        compiler_params=_split_params(),
    )(*[pltpu.with_memory_space_constraint(b, pltpu.HBM) for b in bufs])
    sems = [(res[2 * gi], res[2 * gi + 1]) for gi in range(ng)]
    return sems, list(res[2 * ng:])


def _gather_wait(name, bufs, send_sems, recv_sems, after, group):
    n = len(bufs)

    def body(*refs):
        ss, rs = refs[n], refs[n + 1]
        outs = refs[n + 3:]
        x, y, c, chips = _place()
        my = 2 * x + y
        for idx, (t, layer) in enumerate(group):
            for k, chip in enumerate(chips):
                cp = pltpu.make_async_remote_copy(
                    src_ref=outs[t].at[layer, my], dst_ref=outs[t].at[layer, 2 * chip[0] + chip[1]],
                    send_sem=ss.at[3 * idx + k], recv_sem=rs.at[3 * idx + k], device_id=(*chip, c), device_id_type=MESH)
                cp.wait_send()
                cp.wait_recv()

    return list(pl.pallas_call(
        body, name=name, in_specs=[HBM_SPEC] * n + [SEM_SPEC, SEM_SPEC, ANY],
        out_specs=[HBM_SPEC] * n,
        out_shape=[pltpu.HBM(b.shape, b.dtype) for b in bufs],
        input_output_aliases={t: t for t in range(n)},
        compiler_params=_split_params(),
    )(*bufs, send_sems, recv_sems, after))


N_PARTS = 7


def _scatter_items(send, rx, items, c, chips, x, y):
    my = 2 * x + y
    out = []
    for i, (k, layer) in enumerate(items):
        h = send[k].shape[2] // 2
        for kk, chip in enumerate(chips):
            shard = 2 * chip[0] + chip[1]
            for hf in (0, 1):
                out.append((send[k].at[layer, shard, pl.ds(hf * h, h)], rx[k].at[2 * kk + c, layer],
                            N_PARTS * i + 2 * kk + hf, N_PARTS * i + 2 * kk + c, (*chip, hf)))
        out.append((send[k].at[layer, my, pl.ds((1 - c) * h, h)], rx[k].at[N_PARTS - 1, layer],
                    N_PARTS * i + N_PARTS - 1, N_PARTS * i + N_PARTS - 1, (x, y, 1 - c)))
    return out


def _scatter_start(name, send, rx, items):
    n = len(send)
    m = N_PARTS * len(items)

    def body(*refs):
        ssem, rsem = refs[2 * n], refs[2 * n + 1]
        s_out, r_out = refs[2 * n + 2:3 * n + 2], refs[3 * n + 2:4 * n + 2]
        x, y, c, chips = _place()
        for src, dst, si, ri, to in _scatter_items(s_out, r_out, items, c, chips, x, y):
            pltpu.make_async_remote_copy(src_ref=src, dst_ref=dst, send_sem=ssem.at[si], recv_sem=rsem.at[ri],
                                         device_id=to, device_id_type=MESH).start()
        refs[4 * n + 2][...] = jnp.zeros((V7X_SUBLANES, V7X_LANES), F32)

    res = pl.pallas_call(
        body, name=name, in_specs=[HBM_SPEC] * (2 * n),
        out_specs=[SEM_SPEC, SEM_SPEC] + [HBM_SPEC] * (2 * n) + [pl.BlockSpec(memory_space=pltpu.VMEM)],
        out_shape=[pltpu.SemaphoreType.DMA((m,)), pltpu.SemaphoreType.DMA((m,))]
        + [pltpu.HBM(b.shape, b.dtype) for b in list(send) + list(rx)]
        + [jax.ShapeDtypeStruct((V7X_SUBLANES, V7X_LANES), F32)],
        input_output_aliases={t: 2 + t for t in range(2 * n)},
        compiler_params=_split_params(),
    )(*[pltpu.with_memory_space_constraint(b, pltpu.HBM) for b in list(send) + list(rx)])
    return (res[0], res[1]), list(res[2:2 + n]), list(res[2 + n:2 + 2 * n]), res[2 + 2 * n][0, 0]


def _scatter_wait(name, send, rx, ssem, rsem, after, items):
    n = len(send)

    def body(*refs):
        ss, rs = refs[2 * n], refs[2 * n + 1]
        s_out, r_out = refs[2 * n + 3:3 * n + 3], refs[3 * n + 3:]
        x, y, c, chips = _place()
        for i, (src, dst, si, ri, to) in enumerate(_scatter_items(s_out, r_out, items, c, chips, x, y)):
            arrival = i % N_PARTS
            landed = r_out[items[i // N_PARTS][0]].at[arrival, items[i // N_PARTS][1]]
            cp = pltpu.make_async_remote_copy(src_ref=src, dst_ref=landed, send_sem=ss.at[si],
                                              recv_sem=rs.at[N_PARTS * (i // N_PARTS) + arrival],
                                              device_id=to, device_id_type=MESH)
            cp.wait_send()
            cp.wait_recv()

    res = pl.pallas_call(
        body, name=name, in_specs=[HBM_SPEC] * (2 * n) + [SEM_SPEC, SEM_SPEC, ANY],
        out_specs=[HBM_SPEC] * (2 * n),
        out_shape=[pltpu.HBM(b.shape, b.dtype) for b in list(send) + list(rx)],
        input_output_aliases={t: t for t in range(2 * n)},
        compiler_params=_split_params(),
    )(*send, *rx, ssem, rsem, after)
    return list(res[:n]), list(res[n:])


def _chip_sum(name, g, rx):
    nl, _, r, c = g.shape
    h = r // 2
    tr = _tile(h, max(V7X_SUBLANES * 2, (1 << 19) // (2 * c) // 16 * 16), 16)
    nb = h // tr

    def body(g_ref, r_ref, o_ref):
        acc = g_ref[...].astype(F32)
        for k in range(N_PARTS):
            acc = acc + r_ref[k].astype(F32)
        o_ref[...] = acc

    return pl.pallas_call(
        body, name=name, grid=(nl, nb),
        in_specs=[pl.BlockSpec((None, None, tr, c), lambda l, i: (l, _my_shard(), _my_core() * nb + i, 0)),
                  pl.BlockSpec((N_PARTS, None, tr, c), lambda l, i: (0, l, i, 0))],
        out_specs=pl.BlockSpec((None, tr, c), lambda l, i: (l, i, 0)),
        out_shape=jax.ShapeDtypeStruct((nl, h, c), F32),
        compiler_params=_cp(2),
    )(g, rx)


def _send_half(name, fs):
    n = len(fs)

    def body(*refs):
        ins, outs = refs[:n], refs[n:2 * n]
        send_sems, recv_sems = refs[2 * n:]
        x, y, c, _ = _place()
        cps = []
        for t in range(n):
            cps.append(pltpu.make_async_remote_copy(
                src_ref=ins[t], dst_ref=outs[t], send_sem=send_sems.at[t], recv_sem=recv_sems.at[t],
                device_id=(x, y, 1 - c), device_id_type=MESH))
            cps[-1].start()
        for cp in cps:
            cp.wait()

    return pl.pallas_call(
        body, name=name, in_specs=[ANY] * n, out_specs=[ANY] * n,
        out_shape=[jax.ShapeDtypeStruct(f.shape, f.dtype) for f in fs],
        scratch_shapes=[pltpu.SemaphoreType.DMA((n,)), pltpu.SemaphoreType.DMA((n,))],
        compiler_params=_comm_params(),
    )(*fs)


def _gather_all(name, v):
    m_per = v.shape[0]

    def body(x_ref, out_ref, send_sems, recv_sems, local_sem):
        x, y, c, chips = _place()
        me, sibling = (x, y, c), (x, y, 1 - c)

        def rows(px, py, pc):
            return out_ref.at[pl.ds((4 * px + 2 * py + pc) * m_per, m_per), :]

        def copy(k, block, to, src=None):
            return pltpu.make_async_remote_copy(
                src_ref=rows(*block) if src is None else src, dst_ref=rows(*block), send_sem=send_sems.at[k],
                recv_sem=recv_sems.at[k], device_id=to, device_id_type=MESH)

        mine = pltpu.make_async_copy(x_ref, rows(*me), local_sem)
        mine.start()
        first = [copy(0, me, sibling, src=x_ref)]
        first += [copy(1 + j, me, (*chip, c), src=x_ref) for j, chip in enumerate(chips)]
        for cp in first:
            cp.start()
        passed = [copy(4 + j, (*chip, c), sibling) for j, chip in enumerate(chips)]
        for j, chip in enumerate(chips):
            copy(1 + j, (*chip, c), me).wait_recv()
            passed[j].start()
        copy(0, sibling, me).wait_recv()
        for j, chip in enumerate(chips):
            copy(4 + j, (*chip, 1 - c), me).wait_recv()
        for cp in first + passed:
            cp.wait_send()
        mine.wait()

    return pl.pallas_call(
        body, name=name, in_specs=[ANY], out_specs=ANY,
        out_shape=jax.ShapeDtypeStruct((N_DEV * m_per, v.shape[1]), v.dtype),
        scratch_shapes=[pltpu.SemaphoreType.DMA((7,)), pltpu.SemaphoreType.DMA((7,)), pltpu.SemaphoreType.DMA],
        compiler_params=_comm_params(),
    )(v)


def _rows_view(wall):
    nl, s, r, c = wall.shape
    return wall.reshape(nl, s * r, c)


def _ffn_fwd(tag, alpha, x, w_in, w_out3, layer, g, b):
    h, a = _ffn_in(f"{tag}_in", x, w_in, layer)
    y, xhat, rstd = _mm_ln(f"{tag}_out", alpha, 0.5, a, w_out3, layer, x, g, b)
    return y, (x, h, a, xhat, rstd)


def _dx_ln(name, a, b, *, nk, a_blk, a_map, b_blk, b_map, alpha, dz, nxt):
    t, d = dz.shape
    tm = a_blk[-2]
    xhat, rstd, g = nxt
    through = xhat is not None

    def body(*refs):
        a_ref, b_ref, dz_ref = refs[:3]
        if through:
            xh_ref, rs_ref, g_ref, o_ref, dg_ref, db_ref, acc = refs[3:]
        else:
            g_ref, o_ref, acc = refs[3:]
        i, kk = pl.program_id(0), pl.program_id(1)
        p = lax.dot_general(a_ref[...].astype(BF16), b_ref[...].astype(BF16), _NT, preferred_element_type=F32)

        @pl.when(kk == 0)
        def _():
            acc[...] = p

        @pl.when(kk > 0)
        def _():
            acc[...] += p

        @pl.when(kk == nk - 1)
        def _():
            dy = alpha * dz_ref[...] + acc[...]
            if not through:
                o_ref[...] = dy + g_ref[...]
                return
            xh = xh_ref[...]
            dxh = dy * g_ref[...]
            m1 = jnp.mean(dxh, axis=-1, keepdims=True)
            m2 = jnp.mean(dxh * xh, axis=-1, keepdims=True)
            o_ref[...] = rs_ref[...] * (dxh - m1 - xh * m2)
            pg = jnp.sum(dy * xh, axis=0, keepdims=True)
            pb = jnp.sum(dy, axis=0, keepdims=True)

            @pl.when(i == 0)
            def _():
                dg_ref[...] = pg
                db_ref[...] = pb

            @pl.when(i > 0)
            def _():
                dg_ref[...] += pg
                db_ref[...] += pb

    row = lambda c: pl.BlockSpec((tm, c), lambda i, kk: (i, 0))
    vec = pl.BlockSpec((1, d), lambda i, kk: (0, 0))
    sd = jax.ShapeDtypeStruct
    in_specs = [pl.BlockSpec(a_blk, a_map), pl.BlockSpec(b_blk, b_map), row(d)]
    args = [a, b, dz]
    if through:
        in_specs += [row(d), row(1), vec]
        args += [xhat, rstd, g]
        out_specs, out_shape = [row(d), vec, vec], [sd((t, d), F32), sd((1, d), F32), sd((1, d), F32)]
    else:
        in_specs += [vec]
        args += [g]
        out_specs, out_shape = row(d), sd((t, d), F32)
    return pl.pallas_call(
        body, grid=(t // tm, nk), name=name, in_specs=in_specs, out_specs=out_specs, out_shape=out_shape,
        scratch_shapes=[pltpu.VMEM((tm, d), F32)], compiler_params=_cp(2),
    )(*args)


def _ffn_bwd(tag, alpha, dz, saved, w_in, w_out3, layer, g_win, g_wout3, grads_done, nxt):
    x, h, a, _, _ = saved
    t = x.shape[0]
    _, s, k, n = w_in.shape
    tm = _tile(t, 512)
    g_wout3 = _mm_tn(f"{tag}_dwout", a, dz, tm=n, scale=0.5, layer=layer, into=g_wout3)
    dh = _ffn_da(f"{tag}_da", dz, w_out3, layer, h)
    g_win = _mm(f"{tag}_dwin", x, dh, mode="tn", grid=(s, t // tm), kaxis=1,
                a_blk=(tm, k), a_map=lambda j, kk: (kk, 0),
                b_blk=(None, tm, n), b_map=lambda j, kk: (j // 2, kk, j % 2),
                o_shape=w_in.shape, o_blk=(None, None, k, n), o_map=lambda j, kk: (layer, j, 0, 0),
                o_dtype=g_win.dtype, into=g_win)
    zero = grads_done(g_win, g_wout3)
    return _dx_ln(f"{tag}_dx", dh, w_in, nk=s, a_blk=(None, tm, n), a_map=lambda i, kk: (kk // 2, i, kk % 2),
                  b_blk=(None, None, k, n), b_map=lambda i, kk: (layer, kk, 0, 0),
                  alpha=alpha, dz=dz, nxt=(nxt[0], nxt[1], nxt[2] + zero))


def _fox_fwd(tag, alpha, x, w_pad, bf, w_o3, layer, g, b):
    t, d = x.shape
    nh = bf.shape[1]
    tb = _tile(t, 512)
    nb = t // tb
    qkv = _mm_nn(f"{tag}_qkv", x, w_pad[:, :3 * d], o_dtype=BF16, tn=d).reshape(nb, tb, 3 * d)
    fl = _mm_nn(f"{tag}_gate", x, w_pad[:, 3 * d:])[:, :nh]
    cum = _fox_cumsum(f"{tag}_cum", fl, bf)
    ccol = cum.reshape(nb, tb, nh // 2, 2).transpose(2, 0, 1, 3)
    crow = cum.reshape(nb, tb, nh // 2, 2).transpose(2, 0, 3, 1)
    o, lse = _attn_fwd(f"{tag}_attn", qkv, ccol, crow, nh)
    o2 = o.reshape(t, d)
    y, xhat, rstd = _mm_ln(f"{tag}_oproj", alpha, 1.0, o2, w_o3, layer, x, g, b)
    return y, xhat, rstd, (x, qkv, ccol, crow, o, lse, fl)


def _fox_bwd(tag, alpha, dm, saved, w_pad, bf, w_o3, layer, g_wo3, grads_done, nxt):
    x, qkv, ccol, crow, o, lse, fl = saved
    t, d = x.shape
    nh = bf.shape[1]
    nb, tb, _ = qkv.shape
    tm = _tile(t, 512)
    g_wo3 = _mm_tn(f"{tag}_dwo", o.reshape(t, d), dm, layer=layer, into=g_wo3)
    do = _mm_nt(f"{tag}_do", dm, w_o3, layer=layer).reshape(nb, tb, d)
    dq, dk, dv, drow, dcol = _attn_bwd(f"{tag}_attn_bwd", qkv, ccol, crow, o, lse, do, nh)
    dcum = drow.transpose(1, 2, 0, 3).reshape(t, nh) - dcol.transpose(1, 3, 0, 2).reshape(t, nh)
    dfl, dbf = _fox_cumsum_bwd(f"{tag}_cum_bwd", dcum, fl, bf)
    pad = w_pad.shape[1] - 3 * d - nh
    dproj = jnp.concatenate([dq.reshape(t, d), dk.reshape(t, d), dv.reshape(t, d),
                             dfl.astype(BF16), jnp.zeros((t, pad), BF16)], axis=1)
    d_wpad = _mm_tn(f"{tag}_dwin", x, dproj, tn=_tile(w_pad.shape[1], 640, V7X_LANES))
    zero = grads_done(g_wo3, d_wpad)
    cols = w_pad.shape[1]
    out = _dx_ln(f"{tag}_dx", dproj, w_pad, nk=1, a_blk=(tm, cols), a_map=lambda i, kk: (i, 0),
                 b_blk=(d, cols), b_map=lambda i, kk: (0, 0), alpha=alpha, dz=dm, nxt=(nxt[0], nxt[1], nxt[2] + zero))
    return out, dbf


def _to_segments(a):
    t, d = a.shape
    return a.reshape(SCAN_SEGMENTS, t // SCAN_SEGMENTS, d).transpose(1, 0, 2).reshape(t, d)


def _from_segments(a):
    t, d = a.shape
    return a.reshape(t // SCAN_SEGMENTS, SCAN_SEGMENTS, d).transpose(1, 0, 2).reshape(t, d)


def _s5_discretise(a_re, a_im, log_dt, b_re, b_im):
    dt = jnp.exp(log_dt)[:, None]
    mag = jnp.exp(a_re * dt)
    ang = a_im * dt
    lb_re = mag * jnp.cos(ang)
    lb_im = mag * jnp.sin(ang)
    den = a_re * a_re + a_im * a_im
    nr = lb_re - 1.0
    ni = lb_im
    z_re = (nr * a_re + ni * a_im) / den
    z_im = (ni * a_re - nr * a_im) / den
    bb_re = z_re[..., None] * b_re - z_im[..., None] * b_im
    bb_im = z_re[..., None] * b_im + z_im[..., None] * b_re
    return lb_re, lb_im, bb_re, bb_im


S5_BLOCK_GROUPS = 8


def _blockdiag_in(bb):
    g, p, h = bb.shape
    e = jnp.eye(S5_BLOCK_GROUPS, dtype=bb.dtype)
    b4 = bb.reshape(g // S5_BLOCK_GROUPS, S5_BLOCK_GROUPS, p, h)
    return jnp.einsum("jgph,gf->jghfp", b4, e).reshape(g // S5_BLOCK_GROUPS, S5_BLOCK_GROUPS * h, S5_BLOCK_GROUPS * p)


def _blockdiag_in_grad(d):
    nj, gh, gp = d.shape
    h, p = gh // S5_BLOCK_GROUPS, gp // S5_BLOCK_GROUPS
    e = jnp.eye(S5_BLOCK_GROUPS, dtype=d.dtype)
    d6 = d.reshape(nj, S5_BLOCK_GROUPS, h, S5_BLOCK_GROUPS, p)
    return jnp.einsum("jghfp,gf->jgph", d6, e).reshape(nj * S5_BLOCK_GROUPS, p, h)


def _blockdiag_out(cc):
    g, h, p = cc.shape
    e = jnp.eye(S5_BLOCK_GROUPS, dtype=cc.dtype)
    c4 = cc.reshape(g // S5_BLOCK_GROUPS, S5_BLOCK_GROUPS, h, p)
    return jnp.einsum("jghp,gf->jfpgh", c4, e).reshape(g // S5_BLOCK_GROUPS, S5_BLOCK_GROUPS * p, S5_BLOCK_GROUPS * h)


def _blockdiag_out_grad(d):
    nj, gp, gh = d.shape
    h, p = gh // S5_BLOCK_GROUPS, gp // S5_BLOCK_GROUPS
    e = jnp.eye(S5_BLOCK_GROUPS, dtype=d.dtype)
    d6 = d.reshape(nj, S5_BLOCK_GROUPS, p, S5_BLOCK_GROUPS, h)
    return jnp.einsum("jfpgh,gf->jghp", d6, e).reshape(nj * S5_BLOCK_GROUPS, h, p)


def _s5_fwd(tag, x, prm, w_out, layer):
    a_re, a_im, log_dt, b_re, b_im, c_re, c_im, d_skip = prm
    t, d = x.shape
    g, p = a_re.shape
    w = g * p
    nj = g // S5_BLOCK_GROUPS
    cw, sw = S5_BLOCK_GROUPS * S5_GROUP, S5_BLOCK_GROUPS * p
    seg = t // SCAN_SEGMENTS
    tm = _tile(t, 4096)
    lb_re, lb_im, bb_re, bb_im = _s5_discretise(a_re, a_im, log_dt, b_re, b_im)
    lam = jnp.stack([lb_re.reshape(1, w), lb_im.reshape(1, w)])
    bs = jnp.stack([_blockdiag_in(bb_re), _blockdiag_in(bb_im)]).astype(BF16)
    cs = jnp.stack([_blockdiag_out(c_re), -_blockdiag_out(c_im)]).astype(BF16)
    dvec = d_skip.reshape(1, d)
    u = _to_segments(x)
    bu = _mm(f"{tag}_bu", u, bs, mode="nn", grid=(2, nj, t // tm), kaxis=None,
             a_blk=(tm, cw), a_map=lambda r, j, i: (i, j),
             b_blk=(None, None, cw, sw), b_map=lambda r, j, i: (r, j, 0, 0),
             o_shape=(2, t, w), o_blk=(None, tm, sw), o_map=lambda r, j, i: (r, i, j))
    hs = _s5_scan(f"{tag}_scan", lam, bu.reshape(2, seg, SCAN_SEGMENTS, w)).reshape(2, t, w)
    ych = _mm(f"{tag}_ch", hs, cs, mode="nn", grid=(nj, t // tm, 2), kaxis=2,
              a_blk=(None, tm, sw), a_map=lambda j, i, r: (r, i, j),
              b_blk=(None, None, sw, cw), b_map=lambda j, i, r: (r, j, 0, 0),
              o_shape=(t, d), o_blk=(tm, cw), o_map=lambda j, i, r: (i, j))
    ypre, act = _s5_act_fwd(f"{tag}_act", ych, u, dvec)
    vg = _mm_shards_nn(f"{tag}_wout", act, w_out, layer, F32)
    m = _from_segments(_glu_fwd(f"{tag}_glu", vg))
    return m, (u, lam, bs, cs, dvec, hs, ypre, act, vg)


def _s5_bwd(tag, dm, saved, prm, w_out, layer, g_wout):
    a_re, a_im, log_dt, b_re, b_im, c_re, c_im, d_skip = prm
    u, lam, bs, cs, dvec, hs, ypre, act, vg = saved
    t, d = u.shape
    g, p = a_re.shape
    w = g * p
    nj = g // S5_BLOCK_GROUPS
    cw, sw = S5_BLOCK_GROUPS * S5_GROUP, S5_BLOCK_GROUPS * p
    seg = t // SCAN_SEGMENTS
    tm = _tile(t, 4096)
    dvg = _glu_bwd(f"{tag}_glu_bwd", _to_segments(dm), vg)
    g_wout = _mm_shards_tn(f"{tag}_dwout", act, dvg, layer, g_wout)
    dact = _mm_shards_nt(f"{tag}_dact", dvg, w_out, layer)
    dypre, duskip, dd = _s5_act_bwd(f"{tag}_act_bwd", dact, ypre, u, dvec)
    dh = _mm(f"{tag}_dh", dypre, cs, mode="nt", grid=(2, nj, t // tm), kaxis=None,
             a_blk=(tm, cw), a_map=lambda r, j, i: (i, j),
             b_blk=(None, None, sw, cw), b_map=lambda r, j, i: (r, j, 0, 0),
             o_shape=(2, t, w), o_blk=(None, tm, sw), o_map=lambda r, j, i: (r, i, j))
    dcs = _mm(f"{tag}_dc", hs, dypre, mode="tn", grid=(2, nj, t // tm), kaxis=2,
              a_blk=(None, tm, sw), a_map=lambda r, j, i: (r, i, j),
              b_blk=(tm, cw), b_map=lambda r, j, i: (i, j),
              o_shape=(2, nj, sw, cw), o_blk=(None, None, sw, cw), o_map=lambda r, j, i: (r, j, 0, 0))
    gs, dlam8 = _s5_scan(f"{tag}_scan_bwd", lam, dh.reshape(2, seg, SCAN_SEGMENTS, w),
                         hs.reshape(2, seg, SCAN_SEGMENTS, w))
    gs = gs.reshape(2, t, w)
    du = _mm(f"{tag}_du", gs, bs, mode="nt", grid=(nj, t // tm, 2), kaxis=2,
             a_blk=(None, tm, sw), a_map=lambda j, i, r: (r, i, j),
             b_blk=(None, None, cw, sw), b_map=lambda j, i, r: (r, j, 0, 0),
             o_shape=(t, d), o_blk=(tm, cw), o_map=lambda j, i, r: (i, j))
    dbs = _mm(f"{tag}_db", u, gs, mode="tn", grid=(2, nj, t // tm), kaxis=2,
              a_blk=(tm, cw), a_map=lambda r, j, i: (i, j),
              b_blk=(None, tm, sw), b_map=lambda r, j, i: (r, i, j),
              o_shape=(2, nj, cw, sw), o_blk=(None, None, cw, sw), o_map=lambda r, j, i: (r, j, 0, 0))
    dx = _from_segments(du + duskip)
    dlam = jnp.sum(dlam8, axis=1).reshape(2, g, p)
    small = dict(dlb_re=dlam[0], dlb_im=dlam[1],
                 dbb_re=_blockdiag_in_grad(dbs[0]), dbb_im=_blockdiag_in_grad(dbs[1]),
                 dc_re=_blockdiag_out_grad(dcs[0]), dc_im=-_blockdiag_out_grad(dcs[1]),
                 dd=dd.reshape(g, S5_GROUP))
    return dx, g_wout, small


def _pack(pieces):
    rows = []
    for p in pieces:
        flat = p.reshape(-1).astype(F32)
        n = flat.shape[0]
        rows.append(jnp.pad(flat, (0, -n % V7X_LANES)).reshape(-1, V7X_LANES))
    buf = jnp.concatenate(rows, axis=0)
    return jnp.pad(buf, ((0, -buf.shape[0] % V7X_SUBLANES), (0, 0)))


def _unpack(buf, shapes):
    out, row = [], 0
    for s in shapes:
        n = math.prod(s)
        nr = -(-n // V7X_LANES)
        out.append(buf[row:row + nr].reshape(-1)[:n].reshape(s))
        row += nr
    return out


def kernel(x, ffn1_w_in, ffn1_w_out, ln1_g, ln1_b, lnm_g, lnm_b, ffn2_w_in, ffn2_w_out, ln2_g, ln2_b, fox_w_in, fox_b_f, fox_w_o, s5_a_re, s5_a_im, s5_log_dt, s5_b_re, s5_b_im, s5_c_re, s5_c_im, s5_d, s5_w_out, loss_target, m_ffn1_w_in, m_ffn1_w_out, m_ln1_g, m_ln1_b, m_lnm_g, m_lnm_b, m_ffn2_w_in, m_ffn2_w_out, m_ln2_g, m_ln2_b, m_fox_w_in, m_fox_b_f, m_fox_w_o, m_s5_a_re, m_s5_a_im, m_s5_log_dt, m_s5_b_re, m_s5_b_im, m_s5_c_re, m_s5_c_im, m_s5_d, m_s5_w_out, v_ffn1_w_in, v_ffn1_w_out, v_ln1_g, v_ln1_b, v_lnm_g, v_lnm_b, v_ffn2_w_in, v_ffn2_w_out, v_ln2_g, v_ln2_b, v_fox_w_in, v_fox_b_f, v_fox_w_o, v_s5_a_re, v_s5_a_im, v_s5_log_dt, v_s5_b_re, v_s5_b_im, v_s5_c_re, v_s5_c_im, v_s5_d, v_s5_w_out):
    big_names = ["ffn1_w_in", "ffn1_w_out", "ffn2_w_in", "ffn2_w_out", "fox_w_in", "fox_w_o", "s5_w_out"]
    small_names = ["ln1_g", "ln1_b", "lnm_g", "lnm_b", "ln2_g", "ln2_b", "fox_b_f", "s5_a_re", "s5_a_im", "s5_log_dt",
                   "s5_b_re", "s5_b_im", "s5_c_re", "s5_c_im", "s5_d"]
    out_order = ["ffn1_w_in", "ffn1_w_out", "ln1_g", "ln1_b", "lnm_g", "lnm_b", "ffn2_w_in", "ffn2_w_out", "ln2_g",
                 "ln2_b", "fox_w_in", "fox_b_f", "fox_w_o", "s5_a_re", "s5_a_im", "s5_log_dt", "s5_b_re", "s5_b_im",
                 "s5_c_re", "s5_c_im", "s5_d", "s5_w_out"]
    env = dict(locals())
    w = {n: env[n] for n in out_order}
    mom = {n: env["m_" + n] for n in out_order}
    vel = {n: env["v_" + n] for n in out_order}

    depth, d = ln1_g.shape
    t = x.shape[1]
    alpha = (2.0 * depth) ** 0.25
    x0 = x.reshape(t, d)
    tgt = loss_target.reshape(t, d)

    tix = {n: k for k, n in enumerate(big_names)}
    groups = []
    for i in range(depth):
        j = i // 2
        groups.append([(tix["ffn1_w_in"], i), (tix["ffn1_w_out"], i)])
        mixer = [(tix["fox_w_in"], j), (tix["fox_w_o"], j)] if i % 2 == 0 else [(tix["s5_w_out"], j)]
        groups.append(mixer + [(tix["ffn2_w_in"], i), (tix["ffn2_w_out"], i)])
    sems, bufs = _gather_start("gather_start", [_cast_place(f"cast_{n}", w[n]) for n in big_names], groups)
    full, rows3 = {}, {}

    def arrive(gi, after):
        nonlocal bufs
        bufs = _gather_wait(f"gather_wait_{gi}", bufs, sems[gi][0], sems[gi][1], after, groups[gi])
        full.update(zip(big_names, bufs))
        rows3.update({n: _rows_view(full[n]) for n in ("ffn1_w_out", "ffn2_w_out", "fox_w_o")})

    nh = fox_b_f.shape[1]
    fox_cols = 3 * d + nh
    fox_pad = -(-fox_cols // (5 * V7X_LANES)) * (5 * V7X_LANES)

    def fox_wpad(j):
        wf = full["fox_w_in"][j].transpose(1, 0, 2).reshape(d, fox_cols)
        return jnp.pad(wf, ((0, 0), (0, fox_pad - fox_cols)))

    def s5_params(j):
        return (s5_a_re[j], s5_a_im[j], s5_log_dt[j], s5_b_re[j], s5_b_im[j], s5_c_re[j], s5_c_im[j], s5_d[j])

    saved = []
    h = x0
    for i in range(depth):
        j = i // 2
        arrive(2 * i, h)
        h, s1 = _ffn_fwd(f"l{i}_ffn1", alpha, h, full["ffn1_w_in"], rows3["ffn1_w_out"], i,
                         ln1_g[i:i + 1], ln1_b[i:i + 1])
        arrive(2 * i + 1, h)
        if i % 2 == 0:
            h, xhat_m, rstd_m, sm = _fox_fwd(f"l{i}_fox", alpha, h, fox_wpad(j), fox_b_f[j:j + 1], rows3["fox_w_o"], j,
                                             lnm_g[i:i + 1], lnm_b[i:i + 1])
        else:
            m, sm = _s5_fwd(f"l{i}_s5", h, s5_params(j), full["s5_w_out"], j)
            h, xhat_m, rstd_m = _ln_fwd(f"l{i}_lnm", alpha, h, m, 1.0, lnm_g[i:i + 1], lnm_b[i:i + 1])
        h, s2 = _ffn_fwd(f"l{i}_ffn2", alpha, h, full["ffn2_w_in"], rows3["ffn2_w_out"], i,
                         ln2_g[i:i + 1], ln2_b[i:i + 1])
        saved.append((s1, sm, (xhat_m, rstd_m), s2))
    loss_part = _loss_sum("loss", h, tgt) * (0.5 / d)

    fox_in_names = [f"fox_w_in_l{j}" for j in range(fox_w_in.shape[0])]
    gshape = {n: full[n].shape for n in big_names if n != "fox_w_in"}
    gshape.update({n: (1,) + full["fox_w_in"].shape[1:] for n in fox_in_names})
    gbuf = {n: lax.empty(s, BF16) for n, s in gshape.items()}
    rxbuf = {n: lax.empty((N_PARTS, s[0], s[2] // 2, s[3]), BF16) for n, s in gshape.items()}
    pending = []

    zero = jnp.zeros((), F32)

    def scatter(tag, pairs):
        nonlocal zero
        names = list(dict.fromkeys(n for n, _ in pairs))
        items = [(names.index(n), layer) for n, layer in pairs]
        sem, send, rx, zero = _scatter_start(f"scatter_start_{tag}", [gbuf[n] for n in names],
                                             [rxbuf[n] for n in names], items)
        gbuf.update(zip(names, send))
        rxbuf.update(zip(names, rx))
        pending.append((tag, names, items, sem))

    gsmall = {n: [None] * w[n].shape[0] for n in small_names}
    s5_cot = [None] * s5_a_re.shape[0]
    def ffn_done(which, i):
        def done(g_win, g_wout3):
            gbuf[f"{which}_w_in"], gbuf[f"{which}_w_out"] = g_win, g_wout3.reshape(gshape[f"{which}_w_out"])
            scatter(f"l{i}_{which}", [(f"{which}_w_in", i), (f"{which}_w_out", i)])
            return zero
        return done

    _, _, _, (_, _, _, xhat_top, rstd_top) = saved[depth - 1]
    dz, dg, db = _ln_bwd("top_ln_bwd", [(h, 1.0 / d), (tgt, -1.0 / d)], xhat_top, rstd_top, ln2_g[depth - 1:depth])
    gsmall["ln2_g"][depth - 1], gsmall["ln2_b"][depth - 1] = dg, db
    grad_x = None
    for i in reversed(range(depth)):
        j = i // 2
        s1, sm, (xhat_m, rstd_m), s2 = saved[i]
        dz, dg, db = _ffn_bwd(f"l{i}_ffn2", alpha, dz, s2, full["ffn2_w_in"], rows3["ffn2_w_out"], i,
                              gbuf["ffn2_w_in"], _rows_view(gbuf["ffn2_w_out"]), ffn_done("ffn2", i),
                              (xhat_m, rstd_m, lnm_g[i:i + 1]))
        gsmall["lnm_g"][i], gsmall["lnm_b"][i] = dg, db
        ln1 = (s1[3], s1[4], ln1_g[i:i + 1])
        if i % 2 == 0:
            def fox_done(g_wo3, d_wpad, j=j, i=i):
                gbuf["fox_w_o"] = g_wo3.reshape(gshape["fox_w_o"])
                gbuf[fox_in_names[j]] = d_wpad[:, :fox_cols].reshape(d, N_CHIPS, -1).transpose(1, 0, 2)[None].astype(BF16)
                scatter(f"l{i}_fox", [("fox_w_o", j), (fox_in_names[j], 0)])
                return zero

            (dz, dg, db), gsmall["fox_b_f"][j] = _fox_bwd(
                f"l{i}_fox", alpha, dz, sm, fox_wpad(j), fox_b_f[j:j + 1], rows3["fox_w_o"], j,
                _rows_view(gbuf["fox_w_o"]), fox_done, ln1)
        else:
            dx, gbuf["s5_w_out"], s5_cot[j] = _s5_bwd(f"l{i}_s5", dz, sm, s5_params(j), full["s5_w_out"], j,
                                                      gbuf["s5_w_out"])
            scatter(f"l{i}_s5", [("s5_w_out", j)])
            dz, dg, db = _ln_bwd(f"l{i}_ln1_bwd", [(dz, alpha), (dx, 1.0)], ln1[0], ln1[1], ln1[2] + zero)
        gsmall["ln1_g"][i], gsmall["ln1_b"][i] = dg, db
        if i > 0:
            below = saved[i - 1][3]
            dz, dg, db = _ffn_bwd(f"l{i}_ffn1", alpha, dz, s1, full["ffn1_w_in"], rows3["ffn1_w_out"], i,
                                  gbuf["ffn1_w_in"], _rows_view(gbuf["ffn1_w_out"]), ffn_done("ffn1", i),
                                  (below[3], below[4], ln2_g[i - 1:i]))
            gsmall["ln2_g"][i - 1], gsmall["ln2_b"][i - 1] = dg, db
        else:
            grad_x = _ffn_bwd(f"l{i}_ffn1", alpha, dz, s1, full["ffn1_w_in"], rows3["ffn1_w_out"], i,
                              gbuf["ffn1_w_in"], _rows_view(gbuf["ffn1_w_out"]), ffn_done("ffn1", i),
                              (None, None, jnp.zeros((1, d), F32))).reshape(x.shape)

    cot_names = ["dlb_re", "dlb_im", "dbb_re", "dbb_im", "dc_re", "dc_im", "dd"]
    ln_names = ["ln1_g", "ln1_b", "lnm_g", "lnm_b", "ln2_g", "ln2_b"]
    pieces = [loss_part + zero] + [jnp.concatenate(gsmall[n], axis=0) for n in ln_names + ["fox_b_f"]]
    pieces += [jnp.stack([s5_cot[j][n] for j in range(len(s5_cot))]) for n in cot_names]
    shapes = [p.shape for p in pieces]
    mine = _pack(pieces)
    everyone = _gather_all("small_gather", mine).reshape(N_DEV, *mine.shape)
    summed = _unpack(_sum_leading("small_sum", everyone), shapes)
    loss = summed[0].reshape(())
    gs_final = dict(zip(ln_names + ["fox_b_f"], summed[1:8]))
    cot = dict(zip(cot_names, summed[8:]))
    prm_names = ["s5_a_re", "s5_a_im", "s5_log_dt", "s5_b_re", "s5_b_im"]
    _, disc_vjp = jax.vjp(jax.vmap(_s5_discretise), *[w[n] for n in prm_names])
    for n, gval in zip(prm_names, disc_vjp((cot["dlb_re"], cot["dlb_im"], cot["dbb_re"], cot["dbb_im"]))):
        gs_final[n] = gval
    gs_final["s5_c_re"], gs_final["s5_c_im"], gs_final["s5_d"] = cot["dc_re"], cot["dc_im"], cot["dd"]

    grads, deltas, new_m, new_v = {}, {}, {}, {}
    small_shapes = [w[n].shape for n in small_names]
    for n in small_names:
        grads[n] = gs_final[n].reshape(w[n].shape)
    packed = [_pack([src[n] for n in small_names]) for src in (w, grads, mom, vel)]
    small_out = _adamw("adamw_small", *packed)
    for dst, buf in zip((deltas, new_m, new_v), small_out):
        for n, val in zip(small_names, _unpack(buf, small_shapes)):
            dst[n] = val

    for tag, names, items, sem in pending:
        send, rx = _scatter_wait(f"scatter_wait_{tag}", [gbuf[n] for n in names], [rxbuf[n] for n in names],
                                 sem[0], sem[1], small_out[0], items)
        gbuf.update(zip(names, send))
        rxbuf.update(zip(names, rx))
    half = {n: _chip_sum(f"grad_chip_sum_{n}", gbuf[n], rxbuf[n]) for n in gshape}
    half["fox_w_in"] = jnp.concatenate([half[n] for n in fox_in_names], axis=0)
    halves = [half[n] for n in big_names]
    theirs = _send_half("grad_send_half", halves)
    for n, mine_h, their_h in zip(big_names, halves, theirs):
        grads[n], deltas[n], new_m[n], new_v[n] = _adamw_join(f"adamw_{n}", w[n], mine_h, their_h, mom[n], vel[n])
    return (loss, grad_x, *[grads[n] for n in out_order], *[deltas[n] for n in out_order],
            *[new_m[n] for n in out_order], *[new_v[n] for n in out_order])
```

```python
import functools
import math

import jax
import jax.numpy as jnp
from jax import lax
from jax.experimental import pallas as pl
from jax.experimental.pallas import tpu as pltpu

F32 = jnp.float32
BF16 = jnp.bfloat16
LN_EPS = 1e-5
NEG_INF = -1e30
ADAM_LR = 0.001
ADAM_B1 = 0.9
ADAM_B2 = 0.999
ADAM_EPS = 1e-08
ADAM_WD = 0.01
ADAM_STEP = 10
S5_GROUP = 16
SCAN_SEGMENTS = 32
ATTN_BLOCK = 256
V7X_SUBLANES = 8
V7X_LANES = 128
VMEM_LIMIT = 56 * 1024 * 1024
N_CHIPS = 4
N_DEV = 8
MESH = pl.DeviceIdType.MESH
ANY = pl.BlockSpec(memory_space=pl.ANY)


def _cp(n_grid, kaxis=None):
    sem = tuple("arbitrary" if (kaxis is None or i == kaxis) else "parallel" for i in range(n_grid))
    return pltpu.CompilerParams(dimension_semantics=sem, vmem_limit_bytes=VMEM_LIMIT)


def _tile(n, pref, mult=V7X_SUBLANES):
    if n <= pref:
        return n
    for t in range(pref, 0, -1):
        if n % t == 0 and t % mult == 0:
            return t
    return n


_CONTRACT = {"nn": ((1,), (0,)), "nt": ((1,), (1,)), "tn": ((0,), (0,))}


def _mm(name, a, b, *, mode, grid, kaxis, a_blk, a_map, b_blk, b_map, o_shape, o_blk, o_map, o_dtype=F32, scale=None,
        into=None):
    nk = 1 if kaxis is None else grid[kaxis]
    assert kaxis is None or kaxis == len(grid) - 1
    dims = (_CONTRACT[mode], ((), ()))
    use_acc = nk > 1 and o_dtype != F32
    acc_shape = tuple(d for d in o_blk if d is not None)

    def body(a_ref, b_ref, *rest):
        o_ref, scratch = (rest[1], rest[2:]) if into is not None else (rest[0], rest[1:])
        p = lax.dot_general(a_ref[...].astype(BF16), b_ref[...].astype(BF16), dims, preferred_element_type=F32)
        if nk == 1:
            if scale is not None:
                p = p * scale
            o_ref[...] = p.astype(o_dtype)
            return
        acc = scratch[0] if use_acc else o_ref
        k = pl.program_id(kaxis)

        @pl.when(k == 0)
        def _():
            acc[...] = p

        @pl.when(k > 0)
        def _():
            acc[...] += p

        if use_acc or scale is not None:
            @pl.when(k == nk - 1)
            def _():
                r = acc[...]
                if scale is not None:
                    r = r * scale
                o_ref[...] = r.astype(o_dtype)

    in_specs = [pl.BlockSpec(a_blk, a_map), pl.BlockSpec(b_blk, b_map)]
    args = [a, b]
    if into is not None:
        assert into.shape == tuple(o_shape) and into.dtype == o_dtype
        in_specs.append(ANY)
        args.append(into)
    return pl.pallas_call(
        body, grid=grid, name=name, in_specs=in_specs,
        out_specs=pl.BlockSpec(o_blk, o_map),
        out_shape=jax.ShapeDtypeStruct(o_shape, o_dtype),
        input_output_aliases={2: 0} if into is not None else {},
        scratch_shapes=[pltpu.VMEM(acc_shape, F32)] if use_acc else [],
        compiler_params=_cp(len(grid), kaxis),
    )(*args)


def _mm_shards_nn(name, a, wall, layer, o_dtype):
    t, k = a.shape
    _, s, _, n = wall.shape
    tm = _tile(t, 512)
    return _mm(name, a, wall, mode="nn", grid=(s, t // tm), kaxis=None,
               a_blk=(tm, k), a_map=lambda j, i: (i, 0),
               b_blk=(None, None, k, n), b_map=lambda j, i: (layer, j, 0, 0),
               o_shape=(t, s * n), o_blk=(tm, n), o_map=lambda j, i: (i, j), o_dtype=o_dtype)


def _mm_shards_nt(name, g, wall, layer):
    t = g.shape[0]
    _, s, k, n = wall.shape
    tm = _tile(t, 512)
    return _mm(name, g, wall, mode="nt", grid=(t // tm, s), kaxis=1,
               a_blk=(tm, n), a_map=lambda i, kk: (i, kk),
               b_blk=(None, None, k, n), b_map=lambda i, kk: (layer, kk, 0, 0),
               o_shape=(t, k), o_blk=(tm, k), o_map=lambda i, kk: (i, 0))


def _mm_shards_tn(name, a, g, layer, into):
    t, k = a.shape
    _, s, _, n = into.shape
    tk = _tile(t, 512)
    return _mm(name, a, g, mode="tn", grid=(s, t // tk), kaxis=1,
               a_blk=(tk, k), a_map=lambda j, kk: (kk, 0),
               b_blk=(tk, n), b_map=lambda j, kk: (kk, j),
               o_shape=into.shape, o_blk=(None, None, k, n), o_map=lambda j, kk: (layer, j, 0, 0),
               o_dtype=into.dtype, into=into)


def _mm_nn(name, a, w, o_dtype=F32, tn=None):
    t, k = a.shape
    n = w.shape[1]
    tm = _tile(t, 512)
    tn = n if tn is None else tn
    return _mm(name, a, w, mode="nn", grid=(n // tn, t // tm), kaxis=None,
               a_blk=(tm, k), a_map=lambda j, i: (i, 0),
               b_blk=(k, tn), b_map=lambda j, i: (0, j),
               o_shape=(t, n), o_blk=(tm, tn), o_map=lambda j, i: (i, j), o_dtype=o_dtype)


def _mm_nt(name, g, w, layer=None, o_dtype=F32):
    t, k = g.shape
    n = w.shape[-2]
    tm = _tile(t, 512)
    b_blk, b_map = ((n, k), lambda i: (0, 0)) if layer is None else ((None, n, k), lambda i: (layer, 0, 0))
    return _mm(name, g, w, mode="nt", grid=(t // tm,), kaxis=None,
               a_blk=(tm, k), a_map=lambda i: (i, 0), b_blk=b_blk, b_map=b_map,
               o_shape=(t, n), o_blk=(tm, n), o_map=lambda i: (i, 0), o_dtype=o_dtype)


def _mm_tn(name, a, g, tm=None, tn=None, scale=None, layer=None, into=None):
    t, m = a.shape
    n = g.shape[1]
    tk = _tile(t, 512)
    tm = m if tm is None else tm
    tn = n if tn is None else tn
    if layer is None:
        o_shape, o_blk, o_map, o_dtype = (m, n), (tm, tn), lambda i, j, kk: (i, j), F32
    else:
        o_shape, o_blk, o_map, o_dtype = into.shape, (None, tm, tn), lambda i, j, kk: (layer, i, j), into.dtype
    return _mm(name, a, g, mode="tn", grid=(m // tm, n // tn, t // tk), kaxis=2,
               a_blk=(tk, tm), a_map=lambda i, j, kk: (kk, i),
               b_blk=(tk, tn), b_map=lambda i, j, kk: (kk, j),
               o_shape=o_shape, o_blk=o_blk, o_map=o_map, o_dtype=o_dtype, scale=scale, into=into)


def _sigmoid(x):
    return 1.0 / (1.0 + jnp.exp(-x))


def _rows_call(name, body, t, tm, ins, in_cols, outs, acc_outs=()):
    in_specs = []
    for x, c in zip(ins, in_cols):
        if x.shape[0] == 1:
            in_specs.append(pl.BlockSpec((1, c), lambda i: (0, 0)))
        else:
            in_specs.append(pl.BlockSpec((tm, c), lambda i: (i, 0)))
    out_specs = [pl.BlockSpec((tm, s.shape[1]), lambda i: (i, 0)) for s in outs]
    out_specs += [pl.BlockSpec((1, s.shape[1]), lambda i: (0, 0)) for s in acc_outs]
    return pl.pallas_call(
        body, grid=(t // tm,), name=name, in_specs=in_specs, out_specs=out_specs,
        out_shape=list(outs) + list(acc_outs), compiler_params=_cp(1),
    )(*ins)


def _ln_fwd(name, alpha, x, r, coef, g, b):
    t, d = x.shape
    tm = _tile(t, 256)

    def body(x_ref, r_ref, g_ref, b_ref, y_ref, xh_ref, rs_ref):
        z = alpha * x_ref[...] + coef * r_ref[...]
        mu = jnp.mean(z, axis=-1, keepdims=True)
        zc = z - mu
        var = jnp.mean(zc * zc, axis=-1, keepdims=True)
        rstd = lax.rsqrt(var + LN_EPS)
        xh = zc * rstd
        y_ref[...] = xh * g_ref[...] + b_ref[...]
        xh_ref[...] = xh
        rs_ref[...] = rstd

    sd = jax.ShapeDtypeStruct
    return _rows_call(name, body, t, tm, [x, r, g, b], [d, d, d, d],
                      [sd((t, d), F32), sd((t, d), F32), sd((t, 1), F32)])


def _ln_bwd(name, terms, xhat, rstd, g):
    t, d = xhat.shape
    tm = _tile(t, 256)
    n = len(terms)
    coefs = [c for _, c in terms]

    def body(*refs):
        t_refs = refs[:n]
        xh_ref, rs_ref, g_ref, dz_ref, dg_ref, db_ref = refs[n:]
        dy = coefs[0] * t_refs[0][...]
        for c, r in zip(coefs[1:], t_refs[1:]):
            dy = dy + c * r[...]
        xh = xh_ref[...]
        dxh = dy * g_ref[...]
        m1 = jnp.mean(dxh, axis=-1, keepdims=True)
        m2 = jnp.mean(dxh * xh, axis=-1, keepdims=True)
        dz_ref[...] = rs_ref[...] * (dxh - m1 - xh * m2)
        pg = jnp.sum(dy * xh, axis=0, keepdims=True)
        pb = jnp.sum(dy, axis=0, keepdims=True)
        i = pl.program_id(0)

        @pl.when(i == 0)
        def _():
            dg_ref[...] = pg
            db_ref[...] = pb

        @pl.when(i > 0)
        def _():
            dg_ref[...] += pg
            db_ref[...] += pb

    sd = jax.ShapeDtypeStruct
    arrs = [a for a, _ in terms] + [xhat, rstd, g]
    cols = [d] * n + [d, 1, d]
    return _rows_call(name, body, t, tm, arrs, cols, [sd((t, d), F32)], [sd((1, d), F32), sd((1, d), F32)])


def _loss_sum(name, y, tgt):
    t, d = y.shape
    tm = _tile(t, 256)

    def body(y_ref, t_ref, o_ref):
        e = y_ref[...] - t_ref[...]
        s = jnp.sum(jnp.sum(e * e, axis=1, keepdims=True), axis=0, keepdims=True)
        i = pl.program_id(0)

        @pl.when(i == 0)
        def _():
            o_ref[...] = s

        @pl.when(i > 0)
        def _():
            o_ref[...] += s

    return _rows_call(name, body, t, tm, [y, tgt], [d, d], [], [jax.ShapeDtypeStruct((1, 1), F32)])[0]


def _ffn_in(name, x, wall, layer):
    t, k = x.shape
    n = wall.shape[3]
    tm = _tile(t, 512)

    def body(x_ref, wg_ref, wu_ref, h_ref, a_ref):
        xb = x_ref[...].astype(BF16)
        g = lax.dot_general(xb, wg_ref[...], _NN, preferred_element_type=F32)
        u = lax.dot_general(xb, wu_ref[...], _NN, preferred_element_type=F32)
        h_ref[0] = g.astype(BF16)
        h_ref[1] = u.astype(BF16)
        a_ref[...] = (g * _sigmoid(g) * u).astype(BF16)

    return pl.pallas_call(
        body, grid=(2, t // tm), name=name,
        in_specs=[pl.BlockSpec((tm, k), lambda j, i: (i, 0)),
                  pl.BlockSpec((None, None, k, n), lambda j, i: (layer, j, 0, 0)),
                  pl.BlockSpec((None, None, k, n), lambda j, i: (layer, 2 + j, 0, 0))],
        out_specs=[pl.BlockSpec((2, tm, n), lambda j, i: (0, i, j)), pl.BlockSpec((tm, n), lambda j, i: (i, j))],
        out_shape=[jax.ShapeDtypeStruct((2, t, 2 * n), BF16), jax.ShapeDtypeStruct((t, 2 * n), BF16)],
        compiler_params=_cp(2),
    )(x, wall, wall)


def _ffn_da(name, dz, w3, layer, h, zrow):
    t, k = dz.shape
    f = w3.shape[1]
    n = f // 2
    tm = _tile(t, 512)

    def body(dz_ref, z_ref, w_ref, g_ref, u_ref, dh_ref):
        d = 0.5 * lax.dot_general((dz_ref[...] + z_ref[...]).astype(BF16), w_ref[...], _NT, preferred_element_type=F32)
        g = g_ref[...].astype(F32)
        u = u_ref[...].astype(F32)
        sg = _sigmoid(g)
        dh_ref[0] = (d * u * sg * (1.0 + g * (1.0 - sg))).astype(BF16)
        dh_ref[1] = (d * g * sg).astype(BF16)

    return pl.pallas_call(
        body, grid=(2, t // tm), name=name,
        in_specs=[pl.BlockSpec((tm, k), lambda j, i: (i, 0)),
                  pl.BlockSpec((1, k), lambda j, i: (0, 0)),
                  pl.BlockSpec((None, n, k), lambda j, i: (layer, j, 0)),
                  pl.BlockSpec((None, tm, n), lambda j, i: (0, i, j)),
                  pl.BlockSpec((None, tm, n), lambda j, i: (1, i, j))],
        out_specs=pl.BlockSpec((2, tm, n), lambda j, i: (0, i, j)),
        out_shape=jax.ShapeDtypeStruct((2, t, f), BF16),
        compiler_params=_cp(2),
    )(dz, zrow, w3, h, h)


def _mm_ln(name, alpha, coef, a, w3, layer, x, g, b):
    t, k = a.shape
    d = w3.shape[2]
    tm = _tile(t, 512)

    def body(a_ref, w_ref, x_ref, g_ref, b_ref, y_ref, xh_ref, rs_ref):
        f = lax.dot_general(a_ref[...].astype(BF16), w_ref[...], _NN, preferred_element_type=F32)
        z = alpha * x_ref[...] + coef * f
        mu = jnp.mean(z, axis=-1, keepdims=True)
        zc = z - mu
        var = jnp.mean(zc * zc, axis=-1, keepdims=True)
        rstd = lax.rsqrt(var + LN_EPS)
        xh = zc * rstd
        y_ref[...] = xh * g_ref[...] + b_ref[...]
        xh_ref[...] = xh
        rs_ref[...] = rstd

    row = lambda c: pl.BlockSpec((tm, c), lambda i: (i, 0))
    vec = pl.BlockSpec((1, d), lambda i: (0, 0))
    sd = jax.ShapeDtypeStruct
    return pl.pallas_call(
        body, grid=(t // tm,), name=name,
        in_specs=[row(k), pl.BlockSpec((None, k, d), lambda i: (layer, 0, 0)), row(d), vec, vec],
        out_specs=[row(d), row(d), row(1)],
        out_shape=[sd((t, d), F32), sd((t, d), F32), sd((t, 1), F32)],
        compiler_params=_cp(1),
    )(a, w3, x, g, b)


_GELU_C = math.sqrt(2.0 / math.pi)


def _s5_act_fwd(name, ych, u, dvec):
    t, d = u.shape
    tm = _tile(t, 256)

    def body(y_ref, u_ref, d_ref, p_ref, a_ref):
        y = y_ref[...] + d_ref[...] * u_ref[...]
        p_ref[...] = y
        a_ref[...] = (0.5 * y * (1.0 + jnp.tanh(_GELU_C * (y + 0.044715 * y * y * y)))).astype(BF16)

    sd = jax.ShapeDtypeStruct
    return _rows_call(name, body, t, tm, [ych, u, dvec], [d, d, d], [sd((t, d), F32), sd((t, d), BF16)])


def _s5_act_bwd(name, dact, ypre, u, dvec):
    t, d = u.shape
    tm = _tile(t, 256)

    def body(da_ref, y_ref, u_ref, d_ref, dy_ref, ds_ref, dd_ref):
        y = y_ref[...]
        th = jnp.tanh(_GELU_C * (y + 0.044715 * y * y * y))
        dg = 0.5 * (1.0 + th) + 0.5 * y * (1.0 - th * th) * _GELU_C * (1.0 + 3.0 * 0.044715 * y * y)
        dy = da_ref[...] * dg
        dy_ref[...] = dy
        ds_ref[...] = dy * d_ref[...]
        pd = jnp.sum(dy * u_ref[...], axis=0, keepdims=True)
        i = pl.program_id(0)

        @pl.when(i == 0)
        def _():
            dd_ref[...] = pd

        @pl.when(i > 0)
        def _():
            dd_ref[...] += pd

    sd = jax.ShapeDtypeStruct
    return _rows_call(name, body, t, tm, [dact, ypre, u, dvec], [d, d, d, d],
                      [sd((t, d), F32), sd((t, d), F32)], [sd((1, d), F32)])


def _glu_fwd(name, vg):
    t, d2 = vg.shape
    d = d2 // 2
    tm = _tile(t, 256)

    def body(vg_ref, m_ref):
        m_ref[...] = vg_ref[:, :d] * _sigmoid(vg_ref[:, d:])

    return _rows_call(name, body, t, tm, [vg], [d2], [jax.ShapeDtypeStruct((t, d), F32)])[0]


def _glu_bwd(name, dm, vg):
    t, d2 = vg.shape
    d = d2 // 2
    tm = _tile(t, 256)

    def body(dm_ref, vg_ref, o_ref):
        sg = _sigmoid(vg_ref[:, d:])
        g = dm_ref[...]
        o_ref[:, :d] = (g * sg).astype(BF16)
        o_ref[:, d:] = (g * vg_ref[:, :d] * sg * (1.0 - sg)).astype(BF16)

    return _rows_call(name, body, t, tm, [dm, vg], [d, d2], [jax.ShapeDtypeStruct((t, d2), BF16)])[0]


def _adamw(name, w, g, m, v):
    r, c = w.shape
    tr = _tile(r, max(V7X_SUBLANES, (1 << 20) // (4 * c) // V7X_SUBLANES * V7X_SUBLANES))

    def body(w_ref, g_ref, m_ref, v_ref, d_ref, nm_ref, nv_ref):
        gg = g_ref[...]
        nm = ADAM_B1 * m_ref[...] + (1.0 - ADAM_B1) * gg
        nv = ADAM_B2 * v_ref[...] + (1.0 - ADAM_B2) * (gg * gg)
        m_hat = nm / (1.0 - ADAM_B1 ** ADAM_STEP)
        v_hat = nv / (1.0 - ADAM_B2 ** ADAM_STEP)
        d_ref[...] = -ADAM_LR * (m_hat / (jnp.sqrt(v_hat) + ADAM_EPS) + ADAM_WD * w_ref[...])
        nm_ref[...] = nm
        nv_ref[...] = nv

    sd = jax.ShapeDtypeStruct((r, c), F32)
    return _rows_call(name, body, r, tr, [w, g, m, v], [c] * 4, [sd, sd, sd])


def _my_shard():
    return 2 * lax.axis_index("x") + lax.axis_index("y")


def _my_core():
    return lax.axis_index("c")


def _adamw_join(name, w, mine, theirs, m, v):
    nl, r, c = w.shape
    h = r // 2
    tr = _tile(h, max(V7X_SUBLANES, (1 << 19) // (4 * c) // V7X_SUBLANES * V7X_SUBLANES))
    nb = h // tr

    def body(w_ref, a_ref, b_ref, m_ref, v_ref, g_ref, d_ref, nm_ref, nv_ref):
        gg = jnp.where(pl.program_id(1) == _my_core(), a_ref[...], b_ref[...])
        nm = ADAM_B1 * m_ref[...] + (1.0 - ADAM_B1) * gg
        nv = ADAM_B2 * v_ref[...] + (1.0 - ADAM_B2) * (gg * gg)
        m_hat = nm / (1.0 - ADAM_B1 ** ADAM_STEP)
        v_hat = nv / (1.0 - ADAM_B2 ** ADAM_STEP)
        g_ref[...] = gg
        d_ref[...] = -ADAM_LR * (m_hat / (jnp.sqrt(v_hat) + ADAM_EPS) + ADAM_WD * w_ref[...])
        nm_ref[...] = nm
        nv_ref[...] = nv

    full = pl.BlockSpec((None, tr, c), lambda l, hf, i: (l, hf * nb + i, 0))
    sd = jax.ShapeDtypeStruct((nl, r, c), F32)
    return pl.pallas_call(
        body, name=name, grid=(nl, 2, nb),
        in_specs=[full,
                  pl.BlockSpec((None, tr, c), lambda l, hf, i: (l, jnp.where(hf == _my_core(), i, 0), 0)),
                  pl.BlockSpec((None, tr, c), lambda l, hf, i: (l, jnp.where(hf == _my_core(), 0, i), 0)),
                  full, full],
        out_specs=[full, full, full, full],
        out_shape=[sd, sd, sd, sd],
        compiler_params=_cp(3),
    )(w, mine, theirs, m, v)


def _split3(x):
    hi = x.astype(BF16)
    r1 = x - hi.astype(F32)
    mid = r1.astype(BF16)
    lo = (r1 - mid.astype(F32)).astype(BF16)
    return hi, mid, lo


def _tri_sum(tri, x):
    dims = (((1,), (0,)), ((), ()))
    hi, mid, lo = _split3(x)
    out = lax.dot_general(tri, lo, dims, preferred_element_type=F32)
    out = out + lax.dot_general(tri, mid, dims, preferred_element_type=F32)
    return out + lax.dot_general(tri, hi, dims, preferred_element_type=F32)


def _fox_cumsum(name, fl, bf):
    t, h = fl.shape
    tb = _tile(t, 512)

    def body(fl_ref, bf_ref, c_ref, carry):
        i = pl.program_id(0)

        @pl.when(i == 0)
        def _():
            carry[...] = jnp.zeros_like(carry)

        x = fl_ref[...] + bf_ref[...]
        lf = jnp.minimum(x, 0.0) - jnp.log(1.0 + jnp.exp(-jnp.abs(x)))
        row = lax.broadcasted_iota(jnp.int32, (tb, tb), 0)
        col = lax.broadcasted_iota(jnp.int32, (tb, tb), 1)
        tri = jnp.where(row >= col, 1.0, 0.0).astype(BF16)
        c_ref[...] = _tri_sum(tri, lf) + carry[...]
        carry[...] += jnp.sum(lf, axis=0, keepdims=True)

    return pl.pallas_call(
        body, grid=(t // tb,), name=name,
        in_specs=[pl.BlockSpec((tb, h), lambda i: (i, 0)), pl.BlockSpec((1, h), lambda i: (0, 0))],
        out_specs=pl.BlockSpec((tb, h), lambda i: (i, 0)),
        out_shape=jax.ShapeDtypeStruct((t, h), F32),
        scratch_shapes=[pltpu.VMEM((1, h), F32)], compiler_params=_cp(1),
    )(fl, bf)


def _fox_cumsum_bwd(name, dcum, fl, bf):
    t, h = fl.shape
    tb = _tile(t, 512)
    nb = t // tb

    def body(dc_ref, fl_ref, bf_ref, df_ref, db_ref, carry):
        i = pl.program_id(0)

        @pl.when(i == 0)
        def _():
            carry[...] = jnp.zeros_like(carry)

        dc = dc_ref[...]
        row = lax.broadcasted_iota(jnp.int32, (tb, tb), 0)
        col = lax.broadcasted_iota(jnp.int32, (tb, tb), 1)
        tri = jnp.where(row <= col, 1.0, 0.0).astype(BF16)
        dlf = _tri_sum(tri, dc) + carry[...]
        carry[...] += jnp.sum(dc, axis=0, keepdims=True)
        x = fl_ref[...] + bf_ref[...]
        df = dlf / (1.0 + jnp.exp(x))
        df_ref[...] = df
        pb = jnp.sum(df, axis=0, keepdims=True)

        @pl.when(i == 0)
        def _():
            db_ref[...] = pb

        @pl.when(i > 0)
        def _():
            db_ref[...] += pb

    rev = lambda i: (nb - 1 - i, 0)
    return pl.pallas_call(
        body, grid=(nb,), name=name,
        in_specs=[pl.BlockSpec((tb, h), rev), pl.BlockSpec((tb, h), rev), pl.BlockSpec((1, h), lambda i: (0, 0))],
        out_specs=[pl.BlockSpec((tb, h), rev), pl.BlockSpec((1, h), lambda i: (0, 0))],
        out_shape=[jax.ShapeDtypeStruct((t, h), F32), jax.ShapeDtypeStruct((1, h), F32)],
        scratch_shapes=[pltpu.VMEM((1, h), F32)], compiler_params=_cp(1),
    )(dcum, fl, bf)


_NT = (((1,), (1,)), ((), ()))
_TN = (((0,), (0,)), ((), ()))
_NN = (((1,), (0,)), ((), ()))


def _causal_mask(s, tb):
    row = lax.broadcasted_iota(jnp.int32, (tb, tb), 0)
    col = lax.broadcasted_iota(jnp.int32, (tb, tb), 1)
    return jnp.where(col <= row, s, NEG_INF)


def _first_head_lanes(hd):
    return lax.broadcasted_iota(jnp.int32, (1, 2 * hd), 1) < hd


def _attn_fwd(name, qkv, ccol, crow, nh):
    nb, tb, d3 = qkv.shape
    d = d3 // 3
    hd = d // nh
    lanes = 2 * hd
    assert lanes == V7X_LANES
    scale = 1.0 / math.sqrt(hd)

    def body(q_ref, k_ref, v_ref, cc_ref, cr_ref, o_ref, lse_ref):
        i = pl.program_id(1)
        first = _first_head_lanes(hd)
        q = q_ref[...] * scale
        res = []
        for hh in (0, 1):
            qh = jnp.where(first if hh == 0 else jnp.logical_not(first), q, jnp.zeros_like(q))
            cc = cc_ref[:, hh:hh + 1]

            def step(j, carry, diagonal=False, qh=qh, cc=cc, hh=hh):
                m, l, acc = carry
                s = lax.dot_general(qh, k_ref[j], _NT, preferred_element_type=F32) + cc - cr_ref[j][hh:hh + 1, :]
                if diagonal:
                    s = _causal_mask(s, tb)
                m_new = jnp.maximum(m, jnp.max(s, axis=1, keepdims=True))
                p = jnp.exp(s - m_new)
                a = jnp.exp(m - m_new)
                l = a * l + jnp.sum(p, axis=1, keepdims=True)
                acc = a * acc + lax.dot_general(p.astype(BF16), v_ref[j], _NN, preferred_element_type=F32)
                return m_new, l, acc

            init = (jnp.full((tb, 1), NEG_INF, F32), jnp.zeros((tb, 1), F32), jnp.zeros((tb, lanes), F32))
            m, l, acc = step(i, lax.fori_loop(0, i, step, init), diagonal=True)
            res.append((acc / l, m + jnp.log(l)))
        o_ref[...] = jnp.where(first, res[0][0], res[1][0])
        lse_ref[:, 0:1] = res[0][1]
        lse_ref[:, 1:2] = res[1][1]

    kb, vb = d // lanes, 2 * d // lanes
    return pl.pallas_call(
        body, grid=(nh // 2, nb), name=name,
        in_specs=[pl.BlockSpec((None, tb, lanes), lambda h, i: (i, 0, h)),
                  pl.BlockSpec((nb, tb, lanes), lambda h, i: (0, 0, kb + h)),
                  pl.BlockSpec((nb, tb, lanes), lambda h, i: (0, 0, vb + h)),
                  pl.BlockSpec((None, None, tb, 2), lambda h, i: (h, i, 0, 0)),
                  pl.BlockSpec((None, nb, 2, tb), lambda h, i: (h, 0, 0, 0))],
        out_specs=[pl.BlockSpec((None, tb, lanes), lambda h, i: (i, 0, h)),
                   pl.BlockSpec((None, None, tb, 2), lambda h, i: (h, i, 0, 0))],
        out_shape=[jax.ShapeDtypeStruct((nb, tb, d), F32), jax.ShapeDtypeStruct((nh // 2, nb, tb, 2), F32)],
        compiler_params=_cp(2),
    )(qkv, qkv, qkv, ccol, crow)


def _attn_bwd(name, qkv, ccol, crow, o, lse, do, nh):
    nb, tb, d3 = qkv.shape
    d = d3 // 3
    hd = d // nh
    lanes = 2 * hd
    scale = 1.0 / math.sqrt(hd)

    def body(q_ref, k_ref, v_ref, cc_ref, cr_ref, o_ref, lse_ref, do_ref, dq_ref, dk_ref, dv_ref, dr_ref, dc_ref, dq_acc):
        j = pl.program_id(1)

        @pl.when(j == 0)
        def _():
            dq_acc[...] = jnp.zeros_like(dq_acc)
            dr_ref[...] = jnp.zeros_like(dr_ref)

        first = _first_head_lanes(hd)
        kj = k_ref[...]
        vj = v_ref[...]
        dk = jnp.zeros((tb, lanes), F32)
        dv = jnp.zeros((tb, lanes), F32)
        for hh in (0, 1):
            mine = first if hh == 0 else jnp.logical_not(first)
            cr = cr_ref[hh:hh + 1, :]

            def step(i, carry, diagonal=False, mine=mine, cr=cr, hh=hh):
                dk, dv, dc = carry
                qi = q_ref[i] * scale
                qh = jnp.where(mine, qi, jnp.zeros_like(qi))
                doh = jnp.where(mine, do_ref[i], 0.0)
                dob = doh.astype(BF16)
                di = jnp.sum(doh * o_ref[i], axis=1, keepdims=True)
                s = lax.dot_general(qh, kj, _NT, preferred_element_type=F32) + cc_ref[i][:, hh:hh + 1] - cr
                if diagonal:
                    s = _causal_mask(s, tb)
                p = jnp.exp(s - lse_ref[i][:, hh:hh + 1])
                dv = dv + lax.dot_general(p.astype(BF16), dob, _TN, preferred_element_type=F32)
                dp = lax.dot_general(dob, vj, _NT, preferred_element_type=F32)
                ds = p * (dp - di)
                dsb = ds.astype(BF16)
                dk = dk + lax.dot_general(dsb, qh, _TN, preferred_element_type=F32)
                dq = lax.dot_general(dsb, kj, _NN, preferred_element_type=F32) * scale
                dq_acc[i] += jnp.where(mine, dq, 0.0)
                dr_ref[i, :, hh:hh + 1] += jnp.sum(ds, axis=1, keepdims=True)
                dc = dc + jnp.sum(ds, axis=0, keepdims=True)
                return dk, dv, dc

            dk, dv, dc = lax.fori_loop(j + 1, nb, step, step(j, (dk, dv, jnp.zeros((1, tb), F32)), diagonal=True))
            dc_ref[hh:hh + 1, :] = dc
        dk_ref[...] = dk.astype(BF16)
        dv_ref[...] = dv.astype(BF16)

        @pl.when(j == nb - 1)
        def _():
            dq_ref[...] = dq_acc[...].astype(BF16)

    kb, vb = d // lanes, 2 * d // lanes
    whole = lambda c: pl.BlockSpec((nb, tb, lanes), lambda h, j: (0, 0, c + h))
    block = lambda c: pl.BlockSpec((None, tb, lanes), lambda h, j: (j, 0, c + h))
    cols = pl.BlockSpec((None, nb, tb, 2), lambda h, j: (h, 0, 0, 0))
    rows = pl.BlockSpec((None, None, 2, tb), lambda h, j: (h, j, 0, 0))
    sd = jax.ShapeDtypeStruct
    return pl.pallas_call(
        body, grid=(nh // 2, nb), name=name,
        in_specs=[whole(0), block(kb), block(vb), cols, rows, whole(0), cols, whole(0)],
        out_specs=[whole(0), block(0), block(0), cols, rows],
        out_shape=[sd((nb, tb, d), BF16), sd((nb, tb, d), BF16), sd((nb, tb, d), BF16),
                   sd((nh // 2, nb, tb, 2), F32), sd((nh // 2, nb, 2, tb), F32)],
        scratch_shapes=[pltpu.VMEM((nb, tb, lanes), F32)],
        compiler_params=_cp(2),
    )(qkv, qkv, qkv, ccol, crow, o, lse, do)


def _cmul(ar, ai, br, bi):
    return ar * br - ai * bi, ar * bi + ai * br


def _s5_scan(name, lam, xin, hs=None):
    reverse = hs is not None
    _, seg, ns, w = xin.shape
    assert ns == SCAN_SEGMENTS
    wb = min(w, 2 * V7X_LANES)
    nsq = seg.bit_length() - 1
    assert (1 << nsq) == seg

    def body(*refs):
        if reverse:
            lam_ref, x_ref, h_ref, o_ref, dl_ref = refs
        else:
            lam_ref, x_ref, o_ref = refs
        lr = jnp.broadcast_to(lam_ref[0], (ns, wb))
        li = jnp.broadcast_to(lam_ref[1], (ns, wb))
        if reverse:
            li = -li
        zero = jnp.zeros((ns, wb), F32)
        at = (lambda n: seg - 1 - n) if reverse else (lambda n: n)

        def local(n, c):
            r = at(n)
            mr, mi = _cmul(lr, li, c[0], c[1])
            nr = mr + x_ref[0, r]
            ni = mi + x_ref[1, r]
            o_ref[0, r] = nr
            o_ref[1, r] = ni
            return nr, ni

        er, ei = lax.fori_loop(0, seg, local, (zero, zero))
        pr, pi = lr, li
        for _ in range(nsq):
            pr, pi = _cmul(pr, pi, pr, pi)
        sub = lax.broadcasted_iota(jnp.int32, (ns, wb), 0)

        def shifted(a, sh):
            if reverse:
                return jnp.where(sub < ns - sh, pltpu.roll(a, ns - sh, 0), 0.0)
            return jnp.where(sub >= sh, pltpu.roll(a, sh, 0), 0.0)

        xr, xi = er, ei
        sh = 1
        while sh < ns:
            tr, ti = _cmul(pr, pi, shifted(xr, sh), shifted(xi, sh))
            xr, xi = xr + tr, xi + ti
            pr, pi = _cmul(pr, pi, pr, pi)
            sh *= 2
        cr, ci = shifted(xr, 1), shifted(xi, 1)

        def fix(r, q):
            tr, ti = _cmul(q[0], q[1], cr, ci)
            gr = o_ref[0, r] + tr
            gi = o_ref[1, r] + ti
            o_ref[0, r] = gr
            o_ref[1, r] = gi
            return gr, gi

        if not reverse:
            def fixup(n, q):
                fix(n, q)
                return _cmul(q[0], q[1], lr, li)

            lax.fori_loop(0, seg, fixup, (lr, li))
            return

        def fixup_acc(n, c):
            qr, qi, ar, ai = c
            r = seg - 1 - n
            gr, gi = fix(r, (qr, qi))
            hr = h_ref[0, r - 1]
            hi = h_ref[1, r - 1]
            qr, qi = _cmul(qr, qi, lr, li)
            return qr, qi, ar + gr * hr + gi * hi, ai + gi * hr - gr * hi

        qr, qi, ar, ai = lax.fori_loop(0, seg - 1, fixup_acc, (lr, li, zero, zero))
        gr, gi = fix(0, (qr, qi))
        hr = jnp.where(sub >= 1, pltpu.roll(h_ref[0, seg - 1], 1, 0), 0.0)
        hi = jnp.where(sub >= 1, pltpu.roll(h_ref[1, seg - 1], 1, 0), 0.0)
        dl_ref[0] = ar + gr * hr + gi * hi
        dl_ref[1] = ai + gi * hr - gr * hi

    big = pl.BlockSpec((2, seg, ns, wb), lambda j: (0, 0, 0, j))
    lam_spec = pl.BlockSpec((2, 1, wb), lambda j: (0, 0, j))
    sd = jax.ShapeDtypeStruct
    if reverse:
        return pl.pallas_call(
            body, grid=(w // wb,), name=name, in_specs=[lam_spec, big, big],
            out_specs=[big, pl.BlockSpec((2, ns, wb), lambda j: (0, 0, j))],
            out_shape=[sd(xin.shape, F32), sd((2, ns, w), F32)], compiler_params=_cp(1),
        )(lam, xin, hs)
    return pl.pallas_call(
        body, grid=(w // wb,), name=name, in_specs=[lam_spec, big], out_specs=big,
        out_shape=sd(xin.shape, F32), compiler_params=_cp(1),
    )(lam, xin)


def _place():
    x, y, c = lax.axis_index("x"), lax.axis_index("y"), lax.axis_index("c")
    chips = [(1 - x, y), (x, 1 - y), (1 - x, 1 - y)]
    return x, y, c, chips


def _comm_params():
    return pltpu.CompilerParams(vmem_limit_bytes=VMEM_LIMIT)


def _cast_place(name, w):
    nl, r, c = w.shape
    tr = _tile(r, max(16, (1 << 20) // (4 * c) // 16 * 16), 16)

    def body(w_ref, o_ref):
        o_ref[...] = w_ref[...].astype(BF16)

    return pl.pallas_call(
        body, name=name, grid=(nl, r // tr),
        in_specs=[pl.BlockSpec((None, tr, c), lambda l, i: (l, i, 0))],
        out_specs=pl.BlockSpec((None, None, tr, c), lambda l, i: (l, _my_shard(), i, 0)),
        out_shape=jax.ShapeDtypeStruct((nl, N_CHIPS, r, c), BF16),
        compiler_params=_cp(2),
    )(w)


def _gather_shards(name, bufs):
    n = len(bufs)

    def body(*refs):
        outs = refs[n:2 * n]
        send_sems, recv_sems = refs[2 * n:]
        x, y, c, chips = _place()
        my = 2 * x + y
        sibling = (x, y, 1 - c)

        def part(t, shard, half):
            h = bufs[t].shape[2] // 2
            return outs[t].at[:, shard, pl.ds(half * h, h)]

        def copy(t, k, ref, to):
            return pltpu.make_async_remote_copy(src_ref=ref, dst_ref=ref, send_sem=send_sems.at[t, k],
                                                recv_sem=recv_sems.at[t, k], device_id=to, device_id_type=MESH)

        sent = []
        for t in range(n):
            for k, chip in enumerate(chips):
                sent.append(copy(t, k, part(t, my, c), (*chip, c)))
                sent[-1].start()
        for k, chip in enumerate(chips):
            shard = 2 * chip[0] + chip[1]
            for t in range(n):
                copy(t, k, part(t, shard, c), (*chip, c)).wait_recv()
                sent.append(copy(t, 3 + k, part(t, shard, c), sibling))
                sent[-1].start()
        for k, chip in enumerate(chips):
            shard = 2 * chip[0] + chip[1]
            for t in range(n):
                copy(t, 3 + k, part(t, shard, 1 - c), sibling).wait_recv()
        for cp in sent:
            cp.wait_send()

    return pl.pallas_call(
        body, name=name, in_specs=[ANY] * n, out_specs=[ANY] * n,
        out_shape=[jax.ShapeDtypeStruct(b.shape, b.dtype) for b in bufs],
        input_output_aliases={t: t for t in range(n)},
        scratch_shapes=[pltpu.SemaphoreType.DMA((n, 6)), pltpu.SemaphoreType.DMA((n, 6))],
        compiler_params=_comm_params(),
    )(*bufs)


HBM_SPEC = pl.BlockSpec(memory_space=pltpu.HBM)
SEM_SPEC = pl.BlockSpec(memory_space=pltpu.SEMAPHORE)


def _split_params():
    return pltpu.CompilerParams(has_side_effects=pltpu.SideEffectType.DATAFLOW_SIDE_EFFECTING)


def _gather_start(name, bufs, groups):
    n, ng = len(bufs), len(groups)

    def body(*refs):
        sems = refs[n:n + 2 * ng]
        outs = refs[n + 2 * ng:]
        x, y, c, chips = _place()
        my = 2 * x + y
        for gi, group in enumerate(groups):
            for idx, (t, layer) in enumerate(group):
                block = outs[t].at[layer, my]
                for k, chip in enumerate(chips):
                    pltpu.make_async_remote_copy(
                        src_ref=block, dst_ref=block, send_sem=sems[2 * gi].at[3 * idx + k],
                        recv_sem=sems[2 * gi + 1].at[3 * idx + k], device_id=(*chip, c), device_id_type=MESH).start()

    sem_shapes = []
    for group in groups:
        sem_shapes += [pltpu.SemaphoreType.DMA((3 * len(group),))] * 2
    res = pl.pallas_call(
        body, name=name, in_specs=[HBM_SPEC] * n,
        out_specs=[SEM_SPEC] * (2 * ng) + [HBM_SPEC] * n,
        out_shape=sem_shapes + [pltpu.HBM(b.shape, b.dtype) for b in bufs],
        input_output_aliases={t: 2 * ng + t for t in range(n)},
        compiler_params=_split_params(),
    )(*[pltpu.with_memory_space_constraint(b, pltpu.HBM) for b in bufs])
    sems = [(res[2 * gi], res[2 * gi + 1]) for gi in range(ng)]
    return sems, list(res[2 * ng:])


def _gather_wait(name, bufs, send_sems, recv_sems, after, group):
    n = len(bufs)

    def body(*refs):
        ss, rs = refs[n], refs[n + 1]
        outs = refs[n + 3:]
        x, y, c, chips = _place()
        my = 2 * x + y
        for idx, (t, layer) in enumerate(group):
            for k, chip in enumerate(chips):
                cp = pltpu.make_async_remote_copy(
                    src_ref=outs[t].at[layer, my], dst_ref=outs[t].at[layer, 2 * chip[0] + chip[1]],
                    send_sem=ss.at[3 * idx + k], recv_sem=rs.at[3 * idx + k], device_id=(*chip, c), device_id_type=MESH)
                cp.wait_send()
                cp.wait_recv()

    return list(pl.pallas_call(
        body, name=name, in_specs=[HBM_SPEC] * n + [SEM_SPEC, SEM_SPEC, ANY],
        out_specs=[HBM_SPEC] * n,
        out_shape=[pltpu.HBM(b.shape, b.dtype) for b in bufs],
        input_output_aliases={t: t for t in range(n)},
        compiler_params=_split_params(),
    )(*bufs, send_sems, recv_sems, after))


N_PARTS = 7


def _scatter_items(send, rx, items, c, chips, x, y):
    my = 2 * x + y
    out = []
    for i, (k, layer) in enumerate(items):
        h = send[k].shape[2] // 2
        for kk, chip in enumerate(chips):
            shard = 2 * chip[0] + chip[1]
            for hf in (0, 1):
                out.append((send[k].at[layer, shard, pl.ds(hf * h, h)], rx[k].at[2 * kk + c, layer],
                            N_PARTS * i + 2 * kk + hf, N_PARTS * i + 2 * kk + c, (*chip, hf)))
        out.append((send[k].at[layer, my, pl.ds((1 - c) * h, h)], rx[k].at[N_PARTS - 1, layer],
                    N_PARTS * i + N_PARTS - 1, N_PARTS * i + N_PARTS - 1, (x, y, 1 - c)))
    return out


def _scatter_start(name, send, rx, items):
    n = len(send)
    m = N_PARTS * len(items)

    def body(*refs):
        ssem, rsem = refs[2 * n], refs[2 * n + 1]
        s_out, r_out = refs[2 * n + 2:3 * n + 2], refs[3 * n + 2:4 * n + 2]
        x, y, c, chips = _place()
        for src, dst, si, ri, to in _scatter_items(s_out, r_out, items, c, chips, x, y):
            pltpu.make_async_remote_copy(src_ref=src, dst_ref=dst, send_sem=ssem.at[si], recv_sem=rsem.at[ri],
                                         device_id=to, device_id_type=MESH).start()
        refs[4 * n + 2][...] = jnp.zeros((V7X_SUBLANES, V7X_LANES), F32)

    res = pl.pallas_call(
        body, name=name, in_specs=[HBM_SPEC] * (2 * n),
        out_specs=[SEM_SPEC, SEM_SPEC] + [HBM_SPEC] * (2 * n) + [pl.BlockSpec(memory_space=pltpu.VMEM)],
        out_shape=[pltpu.SemaphoreType.DMA((m,)), pltpu.SemaphoreType.DMA((m,))]
        + [pltpu.HBM(b.shape, b.dtype) for b in list(send) + list(rx)]
        + [jax.ShapeDtypeStruct((V7X_SUBLANES, V7X_LANES), F32)],
        input_output_aliases={t: 2 + t for t in range(2 * n)},
        compiler_params=_split_params(),
    )(*[pltpu.with_memory_space_constraint(b, pltpu.HBM) for b in list(send) + list(rx)])
    return (res[0], res[1]), list(res[2:2 + n]), list(res[2 + n:2 + 2 * n]), res[2 + 2 * n][0, 0]


def _scatter_wait(name, send, rx, ssem, rsem, after, items):
    n = len(send)

    def body(*refs):
        ss, rs = refs[2 * n], refs[2 * n + 1]
        s_out, r_out = refs[2 * n + 3:3 * n + 3], refs[3 * n + 3:]
        x, y, c, chips = _place()
        for i, (src, dst, si, ri, to) in enumerate(_scatter_items(s_out, r_out, items, c, chips, x, y)):
            arrival = i % N_PARTS
            landed = r_out[items[i // N_PARTS][0]].at[arrival, items[i // N_PARTS][1]]
            cp = pltpu.make_async_remote_copy(src_ref=src, dst_ref=landed, send_sem=ss.at[si],
                                              recv_sem=rs.at[N_PARTS * (i // N_PARTS) + arrival],
                                              device_id=to, device_id_type=MESH)
            cp.wait_send()
            cp.wait_recv()

    res = pl.pallas_call(
        body, name=name, in_specs=[HBM_SPEC] * (2 * n) + [SEM_SPEC, SEM_SPEC, ANY],
        out_specs=[HBM_SPEC] * (2 * n),
        out_shape=[pltpu.HBM(b.shape, b.dtype) for b in list(send) + list(rx)],
        input_output_aliases={t: t for t in range(2 * n)},
        compiler_params=_split_params(),
    )(*send, *rx, ssem, rsem, after)
    return list(res[:n]), list(res[n:])


def _chip_sum(name, g, rx):
    nl, _, r, c = g.shape
    h = r // 2
    tr = _tile(h, max(V7X_SUBLANES * 2, (1 << 19) // (2 * c) // 16 * 16), 16)
    nb = h // tr

    def body(g_ref, r_ref, o_ref):
        acc = g_ref[...].astype(F32)
        for k in range(N_PARTS):
            acc = acc + r_ref[k].astype(F32)
        o_ref[...] = acc

    return pl.pallas_call(
        body, name=name, grid=(nl, nb),
        in_specs=[pl.BlockSpec((None, None, tr, c), lambda l, i: (l, _my_shard(), _my_core() * nb + i, 0)),
                  pl.BlockSpec((N_PARTS, None, tr, c), lambda l, i: (0, l, i, 0))],
        out_specs=pl.BlockSpec((None, tr, c), lambda l, i: (l, i, 0)),
        out_shape=jax.ShapeDtypeStruct((nl, h, c), F32),
        compiler_params=_cp(2),
    )(g, rx)


def _send_half(name, fs):
    n = len(fs)

    def body(*refs):
        ins, outs = refs[:n], refs[n:2 * n]
        send_sems, recv_sems = refs[2 * n:]
        x, y, c, _ = _place()
        cps = []
        for t in range(n):
            cps.append(pltpu.make_async_remote_copy(
                src_ref=ins[t], dst_ref=outs[t], send_sem=send_sems.at[t], recv_sem=recv_sems.at[t],
                device_id=(x, y, 1 - c), device_id_type=MESH))
            cps[-1].start()
        for cp in cps:
            cp.wait()

    return pl.pallas_call(
        body, name=name, in_specs=[ANY] * n, out_specs=[ANY] * n,
        out_shape=[jax.ShapeDtypeStruct(f.shape, f.dtype) for f in fs],
        scratch_shapes=[pltpu.SemaphoreType.DMA((n,)), pltpu.SemaphoreType.DMA((n,))],
        compiler_params=_comm_params(),
    )(*fs)


def _peers(x, y, c):
    rel = [(dx, dy, dc) for dx in (0, 1) for dy in (0, 1) for dc in (0, 1) if (dx, dy, dc) != (0, 0, 0)]
    return [(1 - x if dx else x, 1 - y if dy else y, 1 - c if dc else c) for dx, dy, dc in rel]


def _share_start(name, v, land):
    def body(v_ref, land_ref, ssem, rsem, v_out, land_out, token):
        x, y, c, _ = _place()
        me = 4 * x + 2 * y + c
        for k, peer in enumerate(_peers(x, y, c)):
            pltpu.make_async_remote_copy(src_ref=v_out, dst_ref=land_out.at[me], send_sem=ssem.at[k],
                                         recv_sem=rsem.at[k], device_id=peer, device_id_type=MESH).start()
        token[...] = jnp.zeros((V7X_SUBLANES, V7X_LANES), F32)

    res = pl.pallas_call(
        body, name=name, in_specs=[HBM_SPEC, HBM_SPEC],
        out_specs=[SEM_SPEC, SEM_SPEC, HBM_SPEC, HBM_SPEC, pl.BlockSpec(memory_space=pltpu.VMEM)],
        out_shape=[pltpu.SemaphoreType.DMA((N_DEV - 1,)), pltpu.SemaphoreType.DMA((N_DEV - 1,)),
                   pltpu.HBM(v.shape, v.dtype), pltpu.HBM(land.shape, land.dtype),
                   jax.ShapeDtypeStruct((V7X_SUBLANES, V7X_LANES), F32)],
        input_output_aliases={0: 2, 1: 3},
        compiler_params=_split_params(),
    )(pltpu.with_memory_space_constraint(v, pltpu.HBM), pltpu.with_memory_space_constraint(land, pltpu.HBM))
    return (res[0], res[1]), res[2], res[3], res[4][0, 0]


def _share_wait(name, v, land, ssem, rsem, after):
    def body(v_ref, land_ref, ss, rs, after_ref, v_out, land_out):
        x, y, c, _ = _place()
        for k, (px, py, pc) in enumerate(_peers(x, y, c)):
            cp = pltpu.make_async_remote_copy(src_ref=v_out, dst_ref=land_out.at[4 * px + 2 * py + pc],
                                              send_sem=ss.at[k], recv_sem=rs.at[k], device_id=(px, py, pc),
                                              device_id_type=MESH)
            cp.wait_send()
            cp.wait_recv()

    res = pl.pallas_call(
        body, name=name, in_specs=[HBM_SPEC, HBM_SPEC, SEM_SPEC, SEM_SPEC, ANY],
        out_specs=[HBM_SPEC, HBM_SPEC],
        out_shape=[pltpu.HBM(v.shape, v.dtype), pltpu.HBM(land.shape, land.dtype)],
        input_output_aliases={0: 0, 1: 1},
        compiler_params=_split_params(),
    )(v, land, ssem, rsem, after)
    return res[0], res[1]


def _sum_devices(name, v, land):
    r, c = v.shape
    tr = _tile(r, 512)

    def body(v_ref, land_ref, o_ref):
        x, y, cc, _ = _place()
        me = 4 * x + 2 * y + cc
        own = v_ref[...]
        acc = jnp.where(me == 0, own, land_ref[0])
        for k in range(1, N_DEV):
            acc = acc + jnp.where(me == k, own, land_ref[k])
        o_ref[...] = acc

    return pl.pallas_call(
        body, grid=(r // tr,), name=name,
        in_specs=[pl.BlockSpec((tr, c), lambda i: (i, 0)), pl.BlockSpec((N_DEV, tr, c), lambda i: (0, i, 0))],
        out_specs=pl.BlockSpec((tr, c), lambda i: (i, 0)),
        out_shape=jax.ShapeDtypeStruct((r, c), F32), compiler_params=_cp(1),
    )(v, land)


def _rows_view(wall):
    nl, s, r, c = wall.shape
    return wall.reshape(nl, s * r, c)


def _ffn_fwd(tag, alpha, x, w_in, w_out3, layer, g, b):
    h, a = _ffn_in(f"{tag}_in", x, w_in, layer)
    y, xhat, rstd = _mm_ln(f"{tag}_out", alpha, 0.5, a, w_out3, layer, x, g, b)
    return y, (x, h, a, xhat, rstd)


def _dx_ln(name, a, b, *, nk, a_blk, a_map, b_blk, b_map, alpha, dz, nxt):
    t, d = dz.shape
    tm = a_blk[-2]
    xhat, rstd, g = nxt
    through = xhat is not None

    def body(*refs):
        a_ref, b_ref, dz_ref = refs[:3]
        if through:
            xh_ref, rs_ref, g_ref, o_ref, dg_ref, db_ref, acc = refs[3:]
        else:
            g_ref, o_ref, acc = refs[3:]
        i, kk = pl.program_id(0), pl.program_id(1)
        p = lax.dot_general(a_ref[...].astype(BF16), b_ref[...].astype(BF16), _NT, preferred_element_type=F32)

        @pl.when(kk == 0)
        def _():
            acc[...] = p

        @pl.when(kk > 0)
        def _():
            acc[...] += p

        @pl.when(kk == nk - 1)
        def _():
            dy = alpha * dz_ref[...] + acc[...]
            if not through:
                o_ref[...] = dy + g_ref[...]
                return
            xh = xh_ref[...]
            dxh = dy * g_ref[...]
            m1 = jnp.mean(dxh, axis=-1, keepdims=True)
            m2 = jnp.mean(dxh * xh, axis=-1, keepdims=True)
            o_ref[...] = rs_ref[...] * (dxh - m1 - xh * m2)
            pg = jnp.sum(dy * xh, axis=0, keepdims=True)
            pb = jnp.sum(dy, axis=0, keepdims=True)

            @pl.when(i == 0)
            def _():
                dg_ref[...] = pg
                db_ref[...] = pb

            @pl.when(i > 0)
            def _():
                dg_ref[...] += pg
                db_ref[...] += pb

    row = lambda c: pl.BlockSpec((tm, c), lambda i, kk: (i, 0))
    vec = pl.BlockSpec((1, d), lambda i, kk: (0, 0))
    sd = jax.ShapeDtypeStruct
    in_specs = [pl.BlockSpec(a_blk, a_map), pl.BlockSpec(b_blk, b_map), row(d)]
    args = [a, b, dz]
    if through:
        in_specs += [row(d), row(1), vec]
        args += [xhat, rstd, g]
        out_specs, out_shape = [row(d), vec, vec], [sd((t, d), F32), sd((1, d), F32), sd((1, d), F32)]
    else:
        in_specs += [vec]
        args += [g]
        out_specs, out_shape = row(d), sd((t, d), F32)
    return pl.pallas_call(
        body, grid=(t // tm, nk), name=name, in_specs=in_specs, out_specs=out_specs, out_shape=out_shape,
        scratch_shapes=[pltpu.VMEM((tm, d), F32)], compiler_params=_cp(2),
    )(*args)


def _ffn_bwd(tag, alpha, dz, saved, w_in, w_out3, layer, g_win, g_wout3, grads_done, nxt, zrow):
    x, h, a, _, _ = saved
    t = x.shape[0]
    _, s, k, n = w_in.shape
    tm = _tile(t, 512)
    g_wout3 = _mm_tn(f"{tag}_dwout", a, dz, tm=n, scale=0.5, layer=layer, into=g_wout3)
    dh = _ffn_da(f"{tag}_da", dz, w_out3, layer, h, zrow)
    g_win = _mm(f"{tag}_dwin", x, dh, mode="tn", grid=(s, t // tm), kaxis=1,
                a_blk=(tm, k), a_map=lambda j, kk: (kk, 0),
                b_blk=(None, tm, n), b_map=lambda j, kk: (j // 2, kk, j % 2),
                o_shape=w_in.shape, o_blk=(None, None, k, n), o_map=lambda j, kk: (layer, j, 0, 0),
                o_dtype=g_win.dtype, into=g_win)
    zero = grads_done(g_win, g_wout3)
    return _dx_ln(f"{tag}_dx", dh, w_in, nk=s, a_blk=(None, tm, n), a_map=lambda i, kk: (kk // 2, i, kk % 2),
                  b_blk=(None, None, k, n), b_map=lambda i, kk: (layer, kk, 0, 0),
                  alpha=alpha, dz=dz, nxt=(nxt[0], nxt[1], nxt[2] + zero))


def _fox_fwd(tag, alpha, x, w_pad, bf, w_o3, layer, g, b):
    t, d = x.shape
    nh = bf.shape[1]
    tb = _tile(t, ATTN_BLOCK)
    nb = t // tb
    qkv = _mm_nn(f"{tag}_qkv", x, w_pad[:, :3 * d], o_dtype=BF16, tn=d).reshape(nb, tb, 3 * d)
    fl = _mm_nn(f"{tag}_gate", x, w_pad[:, 3 * d:])[:, :nh]
    cum = _fox_cumsum(f"{tag}_cum", fl, bf)
    ccol = cum.reshape(nb, tb, nh // 2, 2).transpose(2, 0, 1, 3)
    crow = cum.reshape(nb, tb, nh // 2, 2).transpose(2, 0, 3, 1)
    o, lse = _attn_fwd(f"{tag}_attn", qkv, ccol, crow, nh)
    o2 = o.reshape(t, d)
    y, xhat, rstd = _mm_ln(f"{tag}_oproj", alpha, 1.0, o2, w_o3, layer, x, g, b)
    return y, xhat, rstd, (x, qkv, ccol, crow, o, lse, fl)


def _fox_bwd(tag, alpha, dm, saved, w_pad, bf, w_o3, layer, g_wo3, grads_done, nxt):
    x, qkv, ccol, crow, o, lse, fl = saved
    t, d = x.shape
    nh = bf.shape[1]
    nb, tb, _ = qkv.shape
    tm = _tile(t, 512)
    g_wo3 = _mm_tn(f"{tag}_dwo", o.reshape(t, d), dm, layer=layer, into=g_wo3)
    do = _mm_nt(f"{tag}_do", dm, w_o3, layer=layer).reshape(nb, tb, d)
    dq, dk, dv, drow, dcol = _attn_bwd(f"{tag}_attn_bwd", qkv, ccol, crow, o, lse, do, nh)
    dcum = drow.transpose(1, 2, 0, 3).reshape(t, nh) - dcol.transpose(1, 3, 0, 2).reshape(t, nh)
    dfl, dbf = _fox_cumsum_bwd(f"{tag}_cum_bwd", dcum, fl, bf)
    pad = w_pad.shape[1] - 3 * d - nh
    dproj = jnp.concatenate([dq.reshape(t, d), dk.reshape(t, d), dv.reshape(t, d),
                             dfl.astype(BF16), jnp.zeros((t, pad), BF16)], axis=1)
    d_wpad = _mm_tn(f"{tag}_dwin", x, dproj, tn=_tile(w_pad.shape[1], 640, V7X_LANES))
    zero = grads_done(g_wo3, d_wpad)
    cols = w_pad.shape[1]
    out = _dx_ln(f"{tag}_dx", dproj, w_pad, nk=1, a_blk=(tm, cols), a_map=lambda i, kk: (i, 0),
                 b_blk=(d, cols), b_map=lambda i, kk: (0, 0), alpha=alpha, dz=dm, nxt=(nxt[0], nxt[1], nxt[2] + zero))
    return out, dbf


def _to_segments(a):
    t, d = a.shape
    return a.reshape(SCAN_SEGMENTS, t // SCAN_SEGMENTS, d).transpose(1, 0, 2).reshape(t, d)


def _from_segments(a):
    t, d = a.shape
    return a.reshape(t // SCAN_SEGMENTS, SCAN_SEGMENTS, d).transpose(1, 0, 2).reshape(t, d)


def _s5_discretise(a_re, a_im, log_dt, b_re, b_im):
    dt = jnp.exp(log_dt)[:, None]
    mag = jnp.exp(a_re * dt)
    ang = a_im * dt
    lb_re = mag * jnp.cos(ang)
    lb_im = mag * jnp.sin(ang)
    den = a_re * a_re + a_im * a_im
    nr = lb_re - 1.0
    ni = lb_im
    z_re = (nr * a_re + ni * a_im) / den
    z_im = (ni * a_re - nr * a_im) / den
    bb_re = z_re[..., None] * b_re - z_im[..., None] * b_im
    bb_im = z_re[..., None] * b_im + z_im[..., None] * b_re
    return lb_re, lb_im, bb_re, bb_im


S5_BLOCK_GROUPS = 8


def _blockdiag_in(bb):
    g, p, h = bb.shape
    e = jnp.eye(S5_BLOCK_GROUPS, dtype=bb.dtype)
    b4 = bb.reshape(g // S5_BLOCK_GROUPS, S5_BLOCK_GROUPS, p, h)
    return jnp.einsum("jgph,gf->jghfp", b4, e).reshape(g // S5_BLOCK_GROUPS, S5_BLOCK_GROUPS * h, S5_BLOCK_GROUPS * p)


def _blockdiag_in_grad(d):
    nj, gh, gp = d.shape
    h, p = gh // S5_BLOCK_GROUPS, gp // S5_BLOCK_GROUPS
    e = jnp.eye(S5_BLOCK_GROUPS, dtype=d.dtype)
    d6 = d.reshape(nj, S5_BLOCK_GROUPS, h, S5_BLOCK_GROUPS, p)
    return jnp.einsum("jghfp,gf->jgph", d6, e).reshape(nj * S5_BLOCK_GROUPS, p, h)


def _blockdiag_out(cc):
    g, h, p = cc.shape
    e = jnp.eye(S5_BLOCK_GROUPS, dtype=cc.dtype)
    c4 = cc.reshape(g // S5_BLOCK_GROUPS, S5_BLOCK_GROUPS, h, p)
    return jnp.einsum("jghp,gf->jfpgh", c4, e).reshape(g // S5_BLOCK_GROUPS, S5_BLOCK_GROUPS * p, S5_BLOCK_GROUPS * h)


def _blockdiag_out_grad(d):
    nj, gp, gh = d.shape
    h, p = gh // S5_BLOCK_GROUPS, gp // S5_BLOCK_GROUPS
    e = jnp.eye(S5_BLOCK_GROUPS, dtype=d.dtype)
    d6 = d.reshape(nj, S5_BLOCK_GROUPS, p, S5_BLOCK_GROUPS, h)
    return jnp.einsum("jfpgh,gf->jghp", d6, e).reshape(nj * S5_BLOCK_GROUPS, h, p)


def _s5_fwd(tag, x, prm, w_out, layer):
    a_re, a_im, log_dt, b_re, b_im, c_re, c_im, d_skip = prm
    t, d = x.shape
    g, p = a_re.shape
    w = g * p
    nj = g // S5_BLOCK_GROUPS
    cw, sw = S5_BLOCK_GROUPS * S5_GROUP, S5_BLOCK_GROUPS * p
    seg = t // SCAN_SEGMENTS
    tm = _tile(t, 4096)
    lb_re, lb_im, bb_re, bb_im = _s5_discretise(a_re, a_im, log_dt, b_re, b_im)
    lam = jnp.stack([lb_re.reshape(1, w), lb_im.reshape(1, w)])
    bs = jnp.stack([_blockdiag_in(bb_re), _blockdiag_in(bb_im)]).astype(BF16)
    cs = jnp.stack([_blockdiag_out(c_re), -_blockdiag_out(c_im)]).astype(BF16)
    dvec = d_skip.reshape(1, d)
    u = _to_segments(x)
    bu = _mm(f"{tag}_bu", u, bs, mode="nn", grid=(2, nj, t // tm), kaxis=None,
             a_blk=(tm, cw), a_map=lambda r, j, i: (i, j),
             b_blk=(None, None, cw, sw), b_map=lambda r, j, i: (r, j, 0, 0),
             o_shape=(2, t, w), o_blk=(None, tm, sw), o_map=lambda r, j, i: (r, i, j))
    hs = _s5_scan(f"{tag}_scan", lam, bu.reshape(2, seg, SCAN_SEGMENTS, w)).reshape(2, t, w)
    ych = _mm(f"{tag}_ch", hs, cs, mode="nn", grid=(nj, t // tm, 2), kaxis=2,
              a_blk=(None, tm, sw), a_map=lambda j, i, r: (r, i, j),
              b_blk=(None, None, sw, cw), b_map=lambda j, i, r: (r, j, 0, 0),
              o_shape=(t, d), o_blk=(tm, cw), o_map=lambda j, i, r: (i, j))
    ypre, act = _s5_act_fwd(f"{tag}_act", ych, u, dvec)
    vg = _mm_shards_nn(f"{tag}_wout", act, w_out, layer, F32)
    m = _from_segments(_glu_fwd(f"{tag}_glu", vg))
    return m, (u, lam, bs, cs, dvec, hs, ypre, act, vg)


def _s5_bwd(tag, dm, saved, prm, w_out, layer, g_wout):
    a_re, a_im, log_dt, b_re, b_im, c_re, c_im, d_skip = prm
    u, lam, bs, cs, dvec, hs, ypre, act, vg = saved
    t, d = u.shape
    g, p = a_re.shape
    w = g * p
    nj = g // S5_BLOCK_GROUPS
    cw, sw = S5_BLOCK_GROUPS * S5_GROUP, S5_BLOCK_GROUPS * p
    seg = t // SCAN_SEGMENTS
    tm = _tile(t, 4096)
    dvg = _glu_bwd(f"{tag}_glu_bwd", _to_segments(dm), vg)
    g_wout = _mm_shards_tn(f"{tag}_dwout", act, dvg, layer, g_wout)
    dact = _mm_shards_nt(f"{tag}_dact", dvg, w_out, layer)
    dypre, duskip, dd = _s5_act_bwd(f"{tag}_act_bwd", dact, ypre, u, dvec)
    dh = _mm(f"{tag}_dh", dypre, cs, mode="nt", grid=(2, nj, t // tm), kaxis=None,
             a_blk=(tm, cw), a_map=lambda r, j, i: (i, j),
             b_blk=(None, None, sw, cw), b_map=lambda r, j, i: (r, j, 0, 0),
             o_shape=(2, t, w), o_blk=(None, tm, sw), o_map=lambda r, j, i: (r, i, j))
    dcs = _mm(f"{tag}_dc", hs, dypre, mode="tn", grid=(2, nj, t // tm), kaxis=2,
              a_blk=(None, tm, sw), a_map=lambda r, j, i: (r, i, j),
              b_blk=(tm, cw), b_map=lambda r, j, i: (i, j),
              o_shape=(2, nj, sw, cw), o_blk=(None, None, sw, cw), o_map=lambda r, j, i: (r, j, 0, 0))
    gs, dlam8 = _s5_scan(f"{tag}_scan_bwd", lam, dh.reshape(2, seg, SCAN_SEGMENTS, w),
                         hs.reshape(2, seg, SCAN_SEGMENTS, w))
    gs = gs.reshape(2, t, w)
    du = _mm(f"{tag}_du", gs, bs, mode="nt", grid=(nj, t // tm, 2), kaxis=2,
             a_blk=(None, tm, sw), a_map=lambda j, i, r: (r, i, j),
             b_blk=(None, None, cw, sw), b_map=lambda j, i, r: (r, j, 0, 0),
             o_shape=(t, d), o_blk=(tm, cw), o_map=lambda j, i, r: (i, j))
    dbs = _mm(f"{tag}_db", u, gs, mode="tn", grid=(2, nj, t // tm), kaxis=2,
              a_blk=(tm, cw), a_map=lambda r, j, i: (i, j),
              b_blk=(None, tm, sw), b_map=lambda r, j, i: (r, i, j),
              o_shape=(2, nj, cw, sw), o_blk=(None, None, cw, sw), o_map=lambda r, j, i: (r, j, 0, 0))
    dx = _from_segments(du + duskip)
    dlam = jnp.sum(dlam8, axis=1).reshape(2, g, p)
    small = dict(dlb_re=dlam[0], dlb_im=dlam[1],
                 dbb_re=_blockdiag_in_grad(dbs[0]), dbb_im=_blockdiag_in_grad(dbs[1]),
                 dc_re=_blockdiag_out_grad(dcs[0]), dc_im=-_blockdiag_out_grad(dcs[1]),
                 dd=dd.reshape(g, S5_GROUP))
    return dx, g_wout, small


def _pack(pieces):
    rows = []
    for p in pieces:
        flat = p.reshape(-1).astype(F32)
        n = flat.shape[0]
        rows.append(jnp.pad(flat, (0, -n % V7X_LANES)).reshape(-1, V7X_LANES))
    buf = jnp.concatenate(rows, axis=0)
    return jnp.pad(buf, ((0, -buf.shape[0] % V7X_SUBLANES), (0, 0)))


def _unpack(buf, shapes):
    out, row = [], 0
    for s in shapes:
        n = math.prod(s)
        nr = -(-n // V7X_LANES)
        out.append(buf[row:row + nr].reshape(-1)[:n].reshape(s))
        row += nr
    return out


def kernel(x, ffn1_w_in, ffn1_w_out, ln1_g, ln1_b, lnm_g, lnm_b, ffn2_w_in, ffn2_w_out, ln2_g, ln2_b, fox_w_in, fox_b_f, fox_w_o, s5_a_re, s5_a_im, s5_log_dt, s5_b_re, s5_b_im, s5_c_re, s5_c_im, s5_d, s5_w_out, loss_target, m_ffn1_w_in, m_ffn1_w_out, m_ln1_g, m_ln1_b, m_lnm_g, m_lnm_b, m_ffn2_w_in, m_ffn2_w_out, m_ln2_g, m_ln2_b, m_fox_w_in, m_fox_b_f, m_fox_w_o, m_s5_a_re, m_s5_a_im, m_s5_log_dt, m_s5_b_re, m_s5_b_im, m_s5_c_re, m_s5_c_im, m_s5_d, m_s5_w_out, v_ffn1_w_in, v_ffn1_w_out, v_ln1_g, v_ln1_b, v_lnm_g, v_lnm_b, v_ffn2_w_in, v_ffn2_w_out, v_ln2_g, v_ln2_b, v_fox_w_in, v_fox_b_f, v_fox_w_o, v_s5_a_re, v_s5_a_im, v_s5_log_dt, v_s5_b_re, v_s5_b_im, v_s5_c_re, v_s5_c_im, v_s5_d, v_s5_w_out):
    big_names = ["ffn1_w_in", "ffn1_w_out", "ffn2_w_in", "ffn2_w_out", "fox_w_in", "fox_w_o", "s5_w_out"]
    small_names = ["ln1_g", "ln1_b", "lnm_g", "lnm_b", "ln2_g", "ln2_b", "fox_b_f", "s5_a_re", "s5_a_im", "s5_log_dt",
                   "s5_b_re", "s5_b_im", "s5_c_re", "s5_c_im", "s5_d"]
    out_order = ["ffn1_w_in", "ffn1_w_out", "ln1_g", "ln1_b", "lnm_g", "lnm_b", "ffn2_w_in", "ffn2_w_out", "ln2_g",
                 "ln2_b", "fox_w_in", "fox_b_f", "fox_w_o", "s5_a_re", "s5_a_im", "s5_log_dt", "s5_b_re", "s5_b_im",
                 "s5_c_re", "s5_c_im", "s5_d", "s5_w_out"]
    env = dict(locals())
    w = {n: env[n] for n in out_order}
    mom = {n: env["m_" + n] for n in out_order}
    vel = {n: env["v_" + n] for n in out_order}

    depth, d = ln1_g.shape
    t = x.shape[1]
    alpha = (2.0 * depth) ** 0.25
    x0 = x.reshape(t, d)
    tgt = loss_target.reshape(t, d)

    tix = {n: k for k, n in enumerate(big_names)}
    groups = []
    for i in range(depth):
        j = i // 2
        groups.append([(tix["ffn1_w_in"], i), (tix["ffn1_w_out"], i)])
        mixer = [(tix["fox_w_in"], j), (tix["fox_w_o"], j)] if i % 2 == 0 else [(tix["s5_w_out"], j)]
        groups.append(mixer + [(tix["ffn2_w_in"], i), (tix["ffn2_w_out"], i)])
    sems, bufs = _gather_start("gather_start", [_cast_place(f"cast_{n}", w[n]) for n in big_names], groups)
    full, rows3 = {}, {}

    def arrive(gi, after):
        nonlocal bufs
        bufs = _gather_wait(f"gather_wait_{gi}", bufs, sems[gi][0], sems[gi][1], after, groups[gi])
        full.update(zip(big_names, bufs))
        rows3.update({n: _rows_view(full[n]) for n in ("ffn1_w_out", "ffn2_w_out", "fox_w_o")})

    nh = fox_b_f.shape[1]
    fox_cols = 3 * d + nh
    fox_pad = -(-fox_cols // (5 * V7X_LANES)) * (5 * V7X_LANES)

    def fox_wpad(j):
        wf = full["fox_w_in"][j].transpose(1, 0, 2).reshape(d, fox_cols)
        return jnp.pad(wf, ((0, 0), (0, fox_pad - fox_cols)))

    def s5_params(j):
        return (s5_a_re[j], s5_a_im[j], s5_log_dt[j], s5_b_re[j], s5_b_im[j], s5_c_re[j], s5_c_im[j], s5_d[j])

    saved = []
    h = x0
    for i in range(depth):
        j = i // 2
        arrive(2 * i, h)
        h, s1 = _ffn_fwd(f"l{i}_ffn1", alpha, h, full["ffn1_w_in"], rows3["ffn1_w_out"], i,
                         ln1_g[i:i + 1], ln1_b[i:i + 1])
        arrive(2 * i + 1, h)
        if i % 2 == 0:
            h, xhat_m, rstd_m, sm = _fox_fwd(f"l{i}_fox", alpha, h, fox_wpad(j), fox_b_f[j:j + 1], rows3["fox_w_o"], j,
                                             lnm_g[i:i + 1], lnm_b[i:i + 1])
        else:
            m, sm = _s5_fwd(f"l{i}_s5", h, s5_params(j), full["s5_w_out"], j)
            h, xhat_m, rstd_m = _ln_fwd(f"l{i}_lnm", alpha, h, m, 1.0, lnm_g[i:i + 1], lnm_b[i:i + 1])
        h, s2 = _ffn_fwd(f"l{i}_ffn2", alpha, h, full["ffn2_w_in"], rows3["ffn2_w_out"], i,
                         ln2_g[i:i + 1], ln2_b[i:i + 1])
        saved.append((s1, sm, (xhat_m, rstd_m), s2))
    loss_part = _loss_sum("loss", h, tgt) * (0.5 / d)

    fox_in_names = [f"fox_w_in_l{j}" for j in range(fox_w_in.shape[0])]
    gshape = {n: full[n].shape for n in big_names if n != "fox_w_in"}
    gshape.update({n: (1,) + full["fox_w_in"].shape[1:] for n in fox_in_names})
    gbuf = {n: lax.empty(s, BF16) for n, s in gshape.items()}
    rxbuf = {n: lax.empty((N_PARTS, s[0], s[2] // 2, s[3]), BF16) for n, s in gshape.items()}
    pending = []

    zero = jnp.zeros((), F32)

    def scatter(tag, pairs):
        nonlocal zero
        names = list(dict.fromkeys(n for n, _ in pairs))
        items = [(names.index(n), layer) for n, layer in pairs]
        sem, send, rx, zero = _scatter_start(f"scatter_start_{tag}", [gbuf[n] for n in names],
                                             [rxbuf[n] for n in names], items)
        gbuf.update(zip(names, send))
        rxbuf.update(zip(names, rx))
        pending.append((tag, names, items, sem))

    gsmall = {n: [None] * w[n].shape[0] for n in small_names}
    s5_cot = [None] * s5_a_re.shape[0]
    cot_names = ["dlb_re", "dlb_im", "dbb_re", "dbb_im", "dc_re", "dc_im", "dd"]
    ln_names = ["ln1_g", "ln1_b", "lnm_g", "lnm_b", "ln2_g", "ln2_b"]

    def zrow():
        return jnp.zeros((1, d), F32) + zero

    def ffn_done(which, i):
        def done(g_win, g_wout3):
            gbuf[f"{which}_w_in"], gbuf[f"{which}_w_out"] = g_win, g_wout3.reshape(gshape[f"{which}_w_out"])
            scatter(f"l{i}_{which}", [(f"{which}_w_in", i), (f"{which}_w_out", i)])
            return zero
        return done

    _, _, _, (_, _, _, xhat_top, rstd_top) = saved[depth - 1]
    dz, dg, db = _ln_bwd("top_ln_bwd", [(h, 1.0 / d), (tgt, -1.0 / d)], xhat_top, rstd_top, ln2_g[depth - 1:depth])
    gsmall["ln2_g"][depth - 1], gsmall["ln2_b"][depth - 1] = dg, db
    grad_x = None
    for i in reversed(range(depth)):
        j = i // 2
        s1, sm, (xhat_m, rstd_m), s2 = saved[i]
        dz, dg, db = _ffn_bwd(f"l{i}_ffn2", alpha, dz, s2, full["ffn2_w_in"], rows3["ffn2_w_out"], i,
                              gbuf["ffn2_w_in"], _rows_view(gbuf["ffn2_w_out"]), ffn_done("ffn2", i),
                              (xhat_m, rstd_m, lnm_g[i:i + 1]), zrow())
        gsmall["lnm_g"][i], gsmall["lnm_b"][i] = dg, db
        ln1 = (s1[3], s1[4], ln1_g[i:i + 1])
        if i % 2 == 0:
            def fox_done(g_wo3, d_wpad, j=j, i=i):
                gbuf["fox_w_o"] = g_wo3.reshape(gshape["fox_w_o"])
                gbuf[fox_in_names[j]] = d_wpad[:, :fox_cols].reshape(d, N_CHIPS, -1).transpose(1, 0, 2)[None].astype(BF16)
                scatter(f"l{i}_fox", [("fox_w_o", j), (fox_in_names[j], 0)])
                return zero

            (dz, dg, db), gsmall["fox_b_f"][j] = _fox_bwd(
                f"l{i}_fox", alpha, dz, sm, fox_wpad(j), fox_b_f[j:j + 1], rows3["fox_w_o"], j,
                _rows_view(gbuf["fox_w_o"]), fox_done, ln1)
        else:
            dx, gbuf["s5_w_out"], s5_cot[j] = _s5_bwd(f"l{i}_s5", dz, sm, s5_params(j), full["s5_w_out"], j,
                                                      gbuf["s5_w_out"])
            scatter(f"l{i}_s5", [("s5_w_out", j)])
            dz, dg, db = _ln_bwd(f"l{i}_ln1_bwd", [(dz, alpha), (dx, 1.0)], ln1[0], ln1[1], ln1[2] + zero)
        gsmall["ln1_g"][i], gsmall["ln1_b"][i] = dg, db
        if i > 0:
            below = saved[i - 1][3]
            dz, dg, db = _ffn_bwd(f"l{i}_ffn1", alpha, dz, s1, full["ffn1_w_in"], rows3["ffn1_w_out"], i,
                                  gbuf["ffn1_w_in"], _rows_view(gbuf["ffn1_w_out"]), ffn_done("ffn1", i),
                                  (below[3], below[4], ln2_g[i - 1:i]), zrow())
            gsmall["ln2_g"][i - 1], gsmall["ln2_b"][i - 1] = dg, db
        else:
            pieces = [loss_part + zero] + [jnp.concatenate(gsmall[n], axis=0) for n in ln_names + ["fox_b_f"]]
            pieces += [jnp.stack([s5_cot[k][n] for k in range(len(s5_cot))]) for n in cot_names]
            mine = _pack(pieces)
            share_sem, mine, land, zero = _share_start("small_share_start", mine, lax.empty((N_DEV,) + mine.shape, F32))
            grad_x = _ffn_bwd(f"l{i}_ffn1", alpha, dz, s1, full["ffn1_w_in"], rows3["ffn1_w_out"], i,
                              gbuf["ffn1_w_in"], _rows_view(gbuf["ffn1_w_out"]), ffn_done("ffn1", i),
                              (None, None, jnp.zeros((1, d), F32)), zrow()).reshape(x.shape)

    shapes = [p.shape for p in pieces]
    mine, land = _share_wait("small_share_wait", mine, land, share_sem[0], share_sem[1], grad_x)
    summed = _unpack(_sum_devices("small_sum", mine, land), shapes)
    loss = summed[0].reshape(())
    gs_final = dict(zip(ln_names + ["fox_b_f"], summed[1:8]))
    cot = dict(zip(cot_names, summed[8:]))
    prm_names = ["s5_a_re", "s5_a_im", "s5_log_dt", "s5_b_re", "s5_b_im"]
    _, disc_vjp = jax.vjp(jax.vmap(_s5_discretise), *[w[n] for n in prm_names])
    for n, gval in zip(prm_names, disc_vjp((cot["dlb_re"], cot["dlb_im"], cot["dbb_re"], cot["dbb_im"]))):
        gs_final[n] = gval
    gs_final["s5_c_re"], gs_final["s5_c_im"], gs_final["s5_d"] = cot["dc_re"], cot["dc_im"], cot["dd"]

    grads, deltas, new_m, new_v = {}, {}, {}, {}
    small_shapes = [w[n].shape for n in small_names]
    for n in small_names:
        grads[n] = gs_final[n].reshape(w[n].shape)
    packed = [_pack([src[n] for n in small_names]) for src in (w, grads, mom, vel)]
    small_out = _adamw("adamw_small", *packed)
    for dst, buf in zip((deltas, new_m, new_v), small_out):
        for n, val in zip(small_names, _unpack(buf, small_shapes)):
            dst[n] = val

    for tag, names, items, sem in pending:
        send, rx = _scatter_wait(f"scatter_wait_{tag}", [gbuf[n] for n in names], [rxbuf[n] for n in names],
                                 sem[0], sem[1], small_out[0], items)
        gbuf.update(zip(names, send))
        rxbuf.update(zip(names, rx))
    half = {n: _chip_sum(f"grad_chip_sum_{n}", gbuf[n], rxbuf[n]) for n in gshape}
    half["fox_w_in"] = jnp.concatenate([half[n] for n in fox_in_names], axis=0)
    halves = [half[n] for n in big_names]
    theirs = _send_half("grad_send_half", halves)
    for n, mine_h, their_h in zip(big_names, halves, theirs):
        grads[n], deltas[n], new_m[n], new_v[n] = _adamw_join(f"adamw_{n}", w[n], mine_h, their_h, mom[n], vel[n])
    return (loss, grad_x, *[grads[n] for n in out_order], *[deltas[n] for n in out_order],
            *[new_m[n] for n in out_order], *[new_v[n] for n in out_order])
```

```python
import functools
import math

import jax
import jax.numpy as jnp
from jax import lax
from jax.experimental import pallas as pl
from jax.experimental.pallas import tpu as pltpu

F32 = jnp.float32
BF16 = jnp.bfloat16
LN_EPS = 1e-5
NEG_INF = -1e30
ADAM_LR = 0.001
ADAM_B1 = 0.9
ADAM_B2 = 0.999
ADAM_EPS = 1e-08
ADAM_WD = 0.01
ADAM_STEP = 10
S5_GROUP = 16
SCAN_SEGMENTS = 32
ATTN_BLOCK = 1024
V7X_SUBLANES = 8
V7X_LANES = 128
VMEM_LIMIT = 56 * 1024 * 1024
N_CHIPS = 4
N_DEV = 8
MESH = pl.DeviceIdType.MESH
ANY = pl.BlockSpec(memory_space=pl.ANY)


def _cp(n_grid, kaxis=None):
    sem = tuple("arbitrary" if (kaxis is None or i == kaxis) else "parallel" for i in range(n_grid))
    return pltpu.CompilerParams(dimension_semantics=sem, vmem_limit_bytes=VMEM_LIMIT)


def _tile(n, pref, mult=V7X_SUBLANES):
    if n <= pref:
        return n
    for t in range(pref, 0, -1):
        if n % t == 0 and t % mult == 0:
            return t
    return n


_CONTRACT = {"nn": ((1,), (0,)), "nt": ((1,), (1,)), "tn": ((0,), (0,))}


def _mm(name, a, b, *, mode, grid, kaxis, a_blk, a_map, b_blk, b_map, o_shape, o_blk, o_map, o_dtype=F32, scale=None,
        into=None):
    nk = 1 if kaxis is None else grid[kaxis]
    assert kaxis is None or kaxis == len(grid) - 1
    dims = (_CONTRACT[mode], ((), ()))
    use_acc = nk > 1 and o_dtype != F32
    acc_shape = tuple(d for d in o_blk if d is not None)

    def body(a_ref, b_ref, *rest):
        o_ref, scratch = (rest[1], rest[2:]) if into is not None else (rest[0], rest[1:])
        p = lax.dot_general(a_ref[...].astype(BF16), b_ref[...].astype(BF16), dims, preferred_element_type=F32)
        if nk == 1:
            if scale is not None:
                p = p * scale
            o_ref[...] = p.astype(o_dtype)
            return
        acc = scratch[0] if use_acc else o_ref
        k = pl.program_id(kaxis)

        @pl.when(k == 0)
        def _():
            acc[...] = p

        @pl.when(k > 0)
        def _():
            acc[...] += p

        if use_acc or scale is not None:
            @pl.when(k == nk - 1)
            def _():
                r = acc[...]
                if scale is not None:
                    r = r * scale
                o_ref[...] = r.astype(o_dtype)

    in_specs = [pl.BlockSpec(a_blk, a_map), pl.BlockSpec(b_blk, b_map)]
    args = [a, b]
    if into is not None:
        assert into.shape == tuple(o_shape) and into.dtype == o_dtype
        in_specs.append(ANY)
        args.append(into)
    return pl.pallas_call(
        body, grid=grid, name=name, in_specs=in_specs,
        out_specs=pl.BlockSpec(o_blk, o_map),
        out_shape=jax.ShapeDtypeStruct(o_shape, o_dtype),
        input_output_aliases={2: 0} if into is not None else {},
        scratch_shapes=[pltpu.VMEM(acc_shape, F32)] if use_acc else [],
        compiler_params=_cp(len(grid), kaxis),
    )(*args)


def _mm_shards_nn(name, a, wall, layer, o_dtype):
    t, k = a.shape
    _, s, _, n = wall.shape
    tm = _tile(t, 512)
    return _mm(name, a, wall, mode="nn", grid=(s, t // tm), kaxis=None,
               a_blk=(tm, k), a_map=lambda j, i: (i, 0),
               b_blk=(None, None, k, n), b_map=lambda j, i: (layer, j, 0, 0),
               o_shape=(t, s * n), o_blk=(tm, n), o_map=lambda j, i: (i, j), o_dtype=o_dtype)


def _mm_shards_nt(name, g, wall, layer):
    t = g.shape[0]
    _, s, k, n = wall.shape
    tm = _tile(t, 512)
    return _mm(name, g, wall, mode="nt", grid=(t // tm, s), kaxis=1,
               a_blk=(tm, n), a_map=lambda i, kk: (i, kk),
               b_blk=(None, None, k, n), b_map=lambda i, kk: (layer, kk, 0, 0),
               o_shape=(t, k), o_blk=(tm, k), o_map=lambda i, kk: (i, 0))


def _mm_shards_tn(name, a, g, layer, into):
    t, k = a.shape
    _, s, _, n = into.shape
    tk = _tile(t, 512)
    return _mm(name, a, g, mode="tn", grid=(s, t // tk), kaxis=1,
               a_blk=(tk, k), a_map=lambda j, kk: (kk, 0),
               b_blk=(tk, n), b_map=lambda j, kk: (kk, j),
               o_shape=into.shape, o_blk=(None, None, k, n), o_map=lambda j, kk: (layer, j, 0, 0),
               o_dtype=into.dtype, into=into)


def _mm_nn(name, a, w, o_dtype=F32, tn=None):
    t, k = a.shape
    n = w.shape[1]
    tm = _tile(t, 512)
    tn = n if tn is None else tn
    return _mm(name, a, w, mode="nn", grid=(n // tn, t // tm), kaxis=None,
               a_blk=(tm, k), a_map=lambda j, i: (i, 0),
               b_blk=(k, tn), b_map=lambda j, i: (0, j),
               o_shape=(t, n), o_blk=(tm, tn), o_map=lambda j, i: (i, j), o_dtype=o_dtype)


def _mm_nt(name, g, w, layer=None, o_dtype=F32):
    t, k = g.shape
    n = w.shape[-2]
    tm = _tile(t, 512)
    b_blk, b_map = ((n, k), lambda i: (0, 0)) if layer is None else ((None, n, k), lambda i: (layer, 0, 0))
    return _mm(name, g, w, mode="nt", grid=(t // tm,), kaxis=None,
               a_blk=(tm, k), a_map=lambda i: (i, 0), b_blk=b_blk, b_map=b_map,
               o_shape=(t, n), o_blk=(tm, n), o_map=lambda i: (i, 0), o_dtype=o_dtype)


def _mm_tn(name, a, g, tm=None, tn=None, scale=None, layer=None, into=None):
    t, m = a.shape
    n = g.shape[1]
    tk = _tile(t, 512)
    tm = m if tm is None else tm
    tn = n if tn is None else tn
    if layer is None:
        o_shape, o_blk, o_map, o_dtype = (m, n), (tm, tn), lambda i, j, kk: (i, j), F32
    else:
        o_shape, o_blk, o_map, o_dtype = into.shape, (None, tm, tn), lambda i, j, kk: (layer, i, j), into.dtype
    return _mm(name, a, g, mode="tn", grid=(m // tm, n // tn, t // tk), kaxis=2,
               a_blk=(tk, tm), a_map=lambda i, j, kk: (kk, i),
               b_blk=(tk, tn), b_map=lambda i, j, kk: (kk, j),
               o_shape=o_shape, o_blk=o_blk, o_map=o_map, o_dtype=o_dtype, scale=scale, into=into)


def _sigmoid(x):
    return 1.0 / (1.0 + jnp.exp(-x))


def _rows_call(name, body, t, tm, ins, in_cols, outs, acc_outs=()):
    in_specs = []
    for x, c in zip(ins, in_cols):
        if x.shape[0] == 1:
            in_specs.append(pl.BlockSpec((1, c), lambda i: (0, 0)))
        else:
            in_specs.append(pl.BlockSpec((tm, c), lambda i: (i, 0)))
    out_specs = [pl.BlockSpec((tm, s.shape[1]), lambda i: (i, 0)) for s in outs]
    out_specs += [pl.BlockSpec((1, s.shape[1]), lambda i: (0, 0)) for s in acc_outs]
    return pl.pallas_call(
        body, grid=(t // tm,), name=name, in_specs=in_specs, out_specs=out_specs,
        out_shape=list(outs) + list(acc_outs), compiler_params=_cp(1),
    )(*ins)


def _ln_fwd(name, alpha, x, r, coef, g, b):
    t, d = x.shape
    tm = _tile(t, 256)

    def body(x_ref, r_ref, g_ref, b_ref, y_ref, xh_ref, rs_ref):
        z = alpha * x_ref[...] + coef * r_ref[...]
        mu = jnp.mean(z, axis=-1, keepdims=True)
        zc = z - mu
        var = jnp.mean(zc * zc, axis=-1, keepdims=True)
        rstd = lax.rsqrt(var + LN_EPS)
        xh = zc * rstd
        y_ref[...] = xh * g_ref[...] + b_ref[...]
        xh_ref[...] = xh
        rs_ref[...] = rstd

    sd = jax.ShapeDtypeStruct
    return _rows_call(name, body, t, tm, [x, r, g, b], [d, d, d, d],
                      [sd((t, d), F32), sd((t, d), F32), sd((t, 1), F32)])


def _ln_bwd(name, terms, xhat, rstd, g):
    t, d = xhat.shape
    tm = _tile(t, 256)
    n = len(terms)
    coefs = [c for _, c in terms]

    def body(*refs):
        t_refs = refs[:n]
        xh_ref, rs_ref, g_ref, dz_ref, dg_ref, db_ref = refs[n:]
        dy = coefs[0] * t_refs[0][...]
        for c, r in zip(coefs[1:], t_refs[1:]):
            dy = dy + c * r[...]
        xh = xh_ref[...]
        dxh = dy * g_ref[...]
        m1 = jnp.mean(dxh, axis=-1, keepdims=True)
        m2 = jnp.mean(dxh * xh, axis=-1, keepdims=True)
        dz_ref[...] = rs_ref[...] * (dxh - m1 - xh * m2)
        pg = jnp.sum(dy * xh, axis=0, keepdims=True)
        pb = jnp.sum(dy, axis=0, keepdims=True)
        i = pl.program_id(0)

        @pl.when(i == 0)
        def _():
            dg_ref[...] = pg
            db_ref[...] = pb

        @pl.when(i > 0)
        def _():
            dg_ref[...] += pg
            db_ref[...] += pb

    sd = jax.ShapeDtypeStruct
    arrs = [a for a, _ in terms] + [xhat, rstd, g]
    cols = [d] * n + [d, 1, d]
    return _rows_call(name, body, t, tm, arrs, cols, [sd((t, d), F32)], [sd((1, d), F32), sd((1, d), F32)])


def _loss_sum(name, y, tgt):
    t, d = y.shape
    tm = _tile(t, 256)

    def body(y_ref, t_ref, o_ref):
        e = y_ref[...] - t_ref[...]
        s = jnp.sum(jnp.sum(e * e, axis=1, keepdims=True), axis=0, keepdims=True)
        i = pl.program_id(0)

        @pl.when(i == 0)
        def _():
            o_ref[...] = s

        @pl.when(i > 0)
        def _():
            o_ref[...] += s

    return _rows_call(name, body, t, tm, [y, tgt], [d, d], [], [jax.ShapeDtypeStruct((1, 1), F32)])[0]


def _ffn_in(name, x, wall, layer):
    t, k = x.shape
    n = wall.shape[3]
    tm = _tile(t, 512)

    def body(x_ref, wg_ref, wu_ref, h_ref, a_ref):
        xb = x_ref[...].astype(BF16)
        g = lax.dot_general(xb, wg_ref[...], _NN, preferred_element_type=F32)
        u = lax.dot_general(xb, wu_ref[...], _NN, preferred_element_type=F32)
        h_ref[0] = g.astype(BF16)
        h_ref[1] = u.astype(BF16)
        a_ref[...] = (g * _sigmoid(g) * u).astype(BF16)

    return pl.pallas_call(
        body, grid=(2, t // tm), name=name,
        in_specs=[pl.BlockSpec((tm, k), lambda j, i: (i, 0)),
                  pl.BlockSpec((None, None, k, n), lambda j, i: (layer, j, 0, 0)),
                  pl.BlockSpec((None, None, k, n), lambda j, i: (layer, 2 + j, 0, 0))],
        out_specs=[pl.BlockSpec((2, tm, n), lambda j, i: (0, i, j)), pl.BlockSpec((tm, n), lambda j, i: (i, j))],
        out_shape=[jax.ShapeDtypeStruct((2, t, 2 * n), BF16), jax.ShapeDtypeStruct((t, 2 * n), BF16)],
        compiler_params=_cp(2),
    )(x, wall, wall)


def _ffn_da(name, dz, w3, layer, h, zrow):
    t, k = dz.shape
    f = w3.shape[1]
    n = f // 2
    tm = _tile(t, 512)

    def body(dz_ref, z_ref, w_ref, g_ref, u_ref, dh_ref):
        d = 0.5 * lax.dot_general((dz_ref[...] + z_ref[...]).astype(BF16), w_ref[...], _NT, preferred_element_type=F32)
        g = g_ref[...].astype(F32)
        u = u_ref[...].astype(F32)
        sg = _sigmoid(g)
        dh_ref[0] = (d * u * sg * (1.0 + g * (1.0 - sg))).astype(BF16)
        dh_ref[1] = (d * g * sg).astype(BF16)

    return pl.pallas_call(
        body, grid=(2, t // tm), name=name,
        in_specs=[pl.BlockSpec((tm, k), lambda j, i: (i, 0)),
                  pl.BlockSpec((1, k), lambda j, i: (0, 0)),
                  pl.BlockSpec((None, n, k), lambda j, i: (layer, j, 0)),
                  pl.BlockSpec((None, tm, n), lambda j, i: (0, i, j)),
                  pl.BlockSpec((None, tm, n), lambda j, i: (1, i, j))],
        out_specs=pl.BlockSpec((2, tm, n), lambda j, i: (0, i, j)),
        out_shape=jax.ShapeDtypeStruct((2, t, f), BF16),
        compiler_params=_cp(2),
    )(dz, zrow, w3, h, h)


def _mm_ln(name, alpha, coef, a, w3, layer, x, g, b):
    t, k = a.shape
    d = w3.shape[2]
    tm = _tile(t, 512)

    def body(a_ref, w_ref, x_ref, g_ref, b_ref, y_ref, xh_ref, rs_ref):
        f = lax.dot_general(a_ref[...].astype(BF16), w_ref[...], _NN, preferred_element_type=F32)
        z = alpha * x_ref[...] + coef * f
        mu = jnp.mean(z, axis=-1, keepdims=True)
        zc = z - mu
        var = jnp.mean(zc * zc, axis=-1, keepdims=True)
        rstd = lax.rsqrt(var + LN_EPS)
        xh = zc * rstd
        y_ref[...] = xh * g_ref[...] + b_ref[...]
        xh_ref[...] = xh
        rs_ref[...] = rstd

    row = lambda c: pl.BlockSpec((tm, c), lambda i: (i, 0))
    vec = pl.BlockSpec((1, d), lambda i: (0, 0))
    sd = jax.ShapeDtypeStruct
    return pl.pallas_call(
        body, grid=(t // tm,), name=name,
        in_specs=[row(k), pl.BlockSpec((None, k, d), lambda i: (layer, 0, 0)), row(d), vec, vec],
        out_specs=[row(d), row(d), row(1)],
        out_shape=[sd((t, d), F32), sd((t, d), F32), sd((t, 1), F32)],
        compiler_params=_cp(1),
    )(a, w3, x, g, b)


_GELU_C = math.sqrt(2.0 / math.pi)


def _s5_act_fwd(name, ych, u, dvec):
    t, d = u.shape
    tm = _tile(t, 256)

    def body(y_ref, u_ref, d_ref, p_ref, a_ref):
        y = y_ref[...] + d_ref[...] * u_ref[...]
        p_ref[...] = y
        a_ref[...] = (0.5 * y * (1.0 + jnp.tanh(_GELU_C * (y + 0.044715 * y * y * y)))).astype(BF16)

    sd = jax.ShapeDtypeStruct
    return _rows_call(name, body, t, tm, [ych, u, dvec], [d, d, d], [sd((t, d), F32), sd((t, d), BF16)])


def _s5_act_bwd(name, dact, ypre, u, dvec):
    t, d = u.shape
    tm = _tile(t, 256)

    def body(da_ref, y_ref, u_ref, d_ref, dy_ref, ds_ref, dd_ref):
        y = y_ref[...]
        th = jnp.tanh(_GELU_C * (y + 0.044715 * y * y * y))
        dg = 0.5 * (1.0 + th) + 0.5 * y * (1.0 - th * th) * _GELU_C * (1.0 + 3.0 * 0.044715 * y * y)
        dy = da_ref[...] * dg
        dy_ref[...] = dy
        ds_ref[...] = dy * d_ref[...]
        pd = jnp.sum(dy * u_ref[...], axis=0, keepdims=True)
        i = pl.program_id(0)

        @pl.when(i == 0)
        def _():
            dd_ref[...] = pd

        @pl.when(i > 0)
        def _():
            dd_ref[...] += pd

    sd = jax.ShapeDtypeStruct
    return _rows_call(name, body, t, tm, [dact, ypre, u, dvec], [d, d, d, d],
                      [sd((t, d), F32), sd((t, d), F32)], [sd((1, d), F32)])


def _glu_fwd(name, vg):
    t, d2 = vg.shape
    d = d2 // 2
    tm = _tile(t, 256)

    def body(vg_ref, m_ref):
        m_ref[...] = vg_ref[:, :d] * _sigmoid(vg_ref[:, d:])

    return _rows_call(name, body, t, tm, [vg], [d2], [jax.ShapeDtypeStruct((t, d), F32)])[0]


def _glu_bwd(name, dm, vg):
    t, d2 = vg.shape
    d = d2 // 2
    tm = _tile(t, 256)

    def body(dm_ref, vg_ref, o_ref):
        sg = _sigmoid(vg_ref[:, d:])
        g = dm_ref[...]
        o_ref[:, :d] = (g * sg).astype(BF16)
        o_ref[:, d:] = (g * vg_ref[:, :d] * sg * (1.0 - sg)).astype(BF16)

    return _rows_call(name, body, t, tm, [dm, vg], [d, d2], [jax.ShapeDtypeStruct((t, d2), BF16)])[0]


def _adamw(name, w, g, m, v):
    r, c = w.shape
    tr = _tile(r, max(V7X_SUBLANES, (1 << 20) // (4 * c) // V7X_SUBLANES * V7X_SUBLANES))

    def body(w_ref, g_ref, m_ref, v_ref, d_ref, nm_ref, nv_ref):
        gg = g_ref[...]
        nm = ADAM_B1 * m_ref[...] + (1.0 - ADAM_B1) * gg
        nv = ADAM_B2 * v_ref[...] + (1.0 - ADAM_B2) * (gg * gg)
        m_hat = nm / (1.0 - ADAM_B1 ** ADAM_STEP)
        v_hat = nv / (1.0 - ADAM_B2 ** ADAM_STEP)
        d_ref[...] = -ADAM_LR * (m_hat / (jnp.sqrt(v_hat) + ADAM_EPS) + ADAM_WD * w_ref[...])
        nm_ref[...] = nm
        nv_ref[...] = nv

    sd = jax.ShapeDtypeStruct((r, c), F32)
    return _rows_call(name, body, r, tr, [w, g, m, v], [c] * 4, [sd, sd, sd])


def _my_shard():
    return 2 * lax.axis_index("x") + lax.axis_index("y")


def _my_core():
    return lax.axis_index("c")


def _adamw_join(name, w, mine, theirs, m, v):
    nl, r, c = w.shape
    h = r // 2
    tr = _tile(h, max(V7X_SUBLANES, (1 << 19) // (4 * c) // V7X_SUBLANES * V7X_SUBLANES))
    nb = h // tr

    def body(w_ref, a_ref, b_ref, m_ref, v_ref, g_ref, d_ref, nm_ref, nv_ref):
        gg = jnp.where(pl.program_id(1) == _my_core(), a_ref[...], b_ref[...])
        nm = ADAM_B1 * m_ref[...] + (1.0 - ADAM_B1) * gg
        nv = ADAM_B2 * v_ref[...] + (1.0 - ADAM_B2) * (gg * gg)
        m_hat = nm / (1.0 - ADAM_B1 ** ADAM_STEP)
        v_hat = nv / (1.0 - ADAM_B2 ** ADAM_STEP)
        g_ref[...] = gg
        d_ref[...] = -ADAM_LR * (m_hat / (jnp.sqrt(v_hat) + ADAM_EPS) + ADAM_WD * w_ref[...])
        nm_ref[...] = nm
        nv_ref[...] = nv

    full = pl.BlockSpec((None, tr, c), lambda l, hf, i: (l, hf * nb + i, 0))
    sd = jax.ShapeDtypeStruct((nl, r, c), F32)
    return pl.pallas_call(
        body, name=name, grid=(nl, 2, nb),
        in_specs=[full,
                  pl.BlockSpec((None, tr, c), lambda l, hf, i: (l, jnp.where(hf == _my_core(), i, 0), 0)),
                  pl.BlockSpec((None, tr, c), lambda l, hf, i: (l, jnp.where(hf == _my_core(), 0, i), 0)),
                  full, full],
        out_specs=[full, full, full, full],
        out_shape=[sd, sd, sd, sd],
        compiler_params=_cp(3),
    )(w, mine, theirs, m, v)


def _split3(x):
    hi = x.astype(BF16)
    r1 = x - hi.astype(F32)
    mid = r1.astype(BF16)
    lo = (r1 - mid.astype(F32)).astype(BF16)
    return hi, mid, lo


def _tri_sum(tri, x):
    dims = (((1,), (0,)), ((), ()))
    hi, mid, lo = _split3(x)
    out = lax.dot_general(tri, lo, dims, preferred_element_type=F32)
    out = out + lax.dot_general(tri, mid, dims, preferred_element_type=F32)
    return out + lax.dot_general(tri, hi, dims, preferred_element_type=F32)


def _fox_cumsum(name, fl, bf):
    t, h = fl.shape
    tb = _tile(t, 512)

    def body(fl_ref, bf_ref, c_ref, carry):
        i = pl.program_id(0)

        @pl.when(i == 0)
        def _():
            carry[...] = jnp.zeros_like(carry)

        x = fl_ref[...] + bf_ref[...]
        lf = jnp.minimum(x, 0.0) - jnp.log(1.0 + jnp.exp(-jnp.abs(x)))
        row = lax.broadcasted_iota(jnp.int32, (tb, tb), 0)
        col = lax.broadcasted_iota(jnp.int32, (tb, tb), 1)
        tri = jnp.where(row >= col, 1.0, 0.0).astype(BF16)
        c_ref[...] = _tri_sum(tri, lf) + carry[...]
        carry[...] += jnp.sum(lf, axis=0, keepdims=True)

    return pl.pallas_call(
        body, grid=(t // tb,), name=name,
        in_specs=[pl.BlockSpec((tb, h), lambda i: (i, 0)), pl.BlockSpec((1, h), lambda i: (0, 0))],
        out_specs=pl.BlockSpec((tb, h), lambda i: (i, 0)),
        out_shape=jax.ShapeDtypeStruct((t, h), F32),
        scratch_shapes=[pltpu.VMEM((1, h), F32)], compiler_params=_cp(1),
    )(fl, bf)


def _fox_cumsum_bwd(name, dcum, fl, bf):
    t, h = fl.shape
    tb = _tile(t, 512)
    nb = t // tb

    def body(dc_ref, fl_ref, bf_ref, df_ref, db_ref, carry):
        i = pl.program_id(0)

        @pl.when(i == 0)
        def _():
            carry[...] = jnp.zeros_like(carry)

        dc = dc_ref[...]
        row = lax.broadcasted_iota(jnp.int32, (tb, tb), 0)
        col = lax.broadcasted_iota(jnp.int32, (tb, tb), 1)
        tri = jnp.where(row <= col, 1.0, 0.0).astype(BF16)
        dlf = _tri_sum(tri, dc) + carry[...]
        carry[...] += jnp.sum(dc, axis=0, keepdims=True)
        x = fl_ref[...] + bf_ref[...]
        df = dlf / (1.0 + jnp.exp(x))
        df_ref[...] = df
        pb = jnp.sum(df, axis=0, keepdims=True)

        @pl.when(i == 0)
        def _():
            db_ref[...] = pb

        @pl.when(i > 0)
        def _():
            db_ref[...] += pb

    rev = lambda i: (nb - 1 - i, 0)
    return pl.pallas_call(
        body, grid=(nb,), name=name,
        in_specs=[pl.BlockSpec((tb, h), rev), pl.BlockSpec((tb, h), rev), pl.BlockSpec((1, h), lambda i: (0, 0))],
        out_specs=[pl.BlockSpec((tb, h), rev), pl.BlockSpec((1, h), lambda i: (0, 0))],
        out_shape=[jax.ShapeDtypeStruct((t, h), F32), jax.ShapeDtypeStruct((1, h), F32)],
        scratch_shapes=[pltpu.VMEM((1, h), F32)], compiler_params=_cp(1),
    )(dcum, fl, bf)


_NT = (((1,), (1,)), ((), ()))
_TN = (((0,), (0,)), ((), ()))
_NN = (((1,), (0,)), ((), ()))


def _causal_mask(s, tb):
    row = lax.broadcasted_iota(jnp.int32, (tb, tb), 0)
    col = lax.broadcasted_iota(jnp.int32, (tb, tb), 1)
    return jnp.where(col <= row, s, NEG_INF)


def _first_head_lanes(hd):
    return lax.broadcasted_iota(jnp.int32, (1, 2 * hd), 1) < hd


def _attn_fwd(name, qkv, ccol, crow, nh):
    nb, tb, d3 = qkv.shape
    d = d3 // 3
    hd = d // nh
    lanes = 2 * hd
    assert lanes == V7X_LANES
    scale = 1.0 / math.sqrt(hd)

    def body(q_ref, k_ref, v_ref, cc_ref, cr_ref, o_ref, lse_ref):
        i = pl.program_id(1)
        first = _first_head_lanes(hd)
        q = q_ref[...] * scale
        res = []
        for hh in (0, 1):
            qh = jnp.where(first if hh == 0 else jnp.logical_not(first), q, jnp.zeros_like(q))
            cc = cc_ref[:, hh:hh + 1]

            def step(j, carry, diagonal=False, qh=qh, cc=cc, hh=hh):
                m, l, acc = carry
                s = lax.dot_general(qh, k_ref[j], _NT, preferred_element_type=F32) + cc - cr_ref[j][hh:hh + 1, :]
                if diagonal:
                    s = _causal_mask(s, tb)
                m_new = jnp.maximum(m, jnp.max(s, axis=1, keepdims=True))
                p = jnp.exp(s - m_new)
                a = jnp.exp(m - m_new)
                l = a * l + jnp.sum(p, axis=1, keepdims=True)
                acc = a * acc + lax.dot_general(p.astype(BF16), v_ref[j], _NN, preferred_element_type=F32)
                return m_new, l, acc

            init = (jnp.full((tb, 1), NEG_INF, F32), jnp.zeros((tb, 1), F32), jnp.zeros((tb, lanes), F32))
            m, l, acc = step(i, lax.fori_loop(0, i, step, init), diagonal=True)
            res.append((acc / l, m + jnp.log(l)))
        o_ref[...] = jnp.where(first, res[0][0], res[1][0])
        lse_ref[:, 0:1] = res[0][1]
        lse_ref[:, 1:2] = res[1][1]

    kb, vb = d // lanes, 2 * d // lanes
    return pl.pallas_call(
        body, grid=(nh // 2, nb), name=name,
        in_specs=[pl.BlockSpec((None, tb, lanes), lambda h, i: (i, 0, h)),
                  pl.BlockSpec((nb, tb, lanes), lambda h, i: (0, 0, kb + h)),
                  pl.BlockSpec((nb, tb, lanes), lambda h, i: (0, 0, vb + h)),
                  pl.BlockSpec((None, None, tb, 2), lambda h, i: (h, i, 0, 0)),
                  pl.BlockSpec((None, nb, 2, tb), lambda h, i: (h, 0, 0, 0))],
        out_specs=[pl.BlockSpec((None, tb, lanes), lambda h, i: (i, 0, h)),
                   pl.BlockSpec((None, None, tb, 2), lambda h, i: (h, i, 0, 0))],
        out_shape=[jax.ShapeDtypeStruct((nb, tb, d), F32), jax.ShapeDtypeStruct((nh // 2, nb, tb, 2), F32)],
        compiler_params=_cp(2),
    )(qkv, qkv, qkv, ccol, crow)


def _attn_bwd(name, qkv, ccol, crow, o, lse, do, nh):
    nb, tb, d3 = qkv.shape
    d = d3 // 3
    hd = d // nh
    lanes = 2 * hd
    scale = 1.0 / math.sqrt(hd)

    def body(q_ref, k_ref, v_ref, cc_ref, cr_ref, o_ref, lse_ref, do_ref, dq_ref, dk_ref, dv_ref, dr_ref, dc_ref, dq_acc):
        j = pl.program_id(1)

        @pl.when(j == 0)
        def _():
            dq_acc[...] = jnp.zeros_like(dq_acc)
            dr_ref[...] = jnp.zeros_like(dr_ref)

        first = _first_head_lanes(hd)
        kj = k_ref[...]
        vj = v_ref[...]
        dk = jnp.zeros((tb, lanes), F32)
        dv = jnp.zeros((tb, lanes), F32)
        for hh in (0, 1):
            mine = first if hh == 0 else jnp.logical_not(first)
            cr = cr_ref[hh:hh + 1, :]

            def step(i, carry, diagonal=False, mine=mine, cr=cr, hh=hh):
                dk, dv, dc = carry
                qi = q_ref[i] * scale
                qh = jnp.where(mine, qi, jnp.zeros_like(qi))
                doh = jnp.where(mine, do_ref[i], 0.0)
                dob = doh.astype(BF16)
                di = jnp.sum(doh * o_ref[i], axis=1, keepdims=True)
                s = lax.dot_general(qh, kj, _NT, preferred_element_type=F32) + cc_ref[i][:, hh:hh + 1] - cr
                if diagonal:
                    s = _causal_mask(s, tb)
                p = jnp.exp(s - lse_ref[i][:, hh:hh + 1])
                dv = dv + lax.dot_general(p.astype(BF16), dob, _TN, preferred_element_type=F32)
                dp = lax.dot_general(dob, vj, _NT, preferred_element_type=F32)
                ds = p * (dp - di)
                dsb = ds.astype(BF16)
                dk = dk + lax.dot_general(dsb, qh, _TN, preferred_element_type=F32)
                dq = lax.dot_general(dsb, kj, _NN, preferred_element_type=F32) * scale
                dq_acc[i] += jnp.where(mine, dq, 0.0)
                dr_ref[i, :, hh:hh + 1] += jnp.sum(ds, axis=1, keepdims=True)
                dc = dc + jnp.sum(ds, axis=0, keepdims=True)
                return dk, dv, dc

            dk, dv, dc = lax.fori_loop(j + 1, nb, step, step(j, (dk, dv, jnp.zeros((1, tb), F32)), diagonal=True))
            dc_ref[hh:hh + 1, :] = dc
        dk_ref[...] = dk.astype(BF16)
        dv_ref[...] = dv.astype(BF16)

        @pl.when(j == nb - 1)
        def _():
            dq_ref[...] = dq_acc[...].astype(BF16)

    kb, vb = d // lanes, 2 * d // lanes
    whole = lambda c: pl.BlockSpec((nb, tb, lanes), lambda h, j: (0, 0, c + h))
    block = lambda c: pl.BlockSpec((None, tb, lanes), lambda h, j: (j, 0, c + h))
    cols = pl.BlockSpec((None, nb, tb, 2), lambda h, j: (h, 0, 0, 0))
    rows = pl.BlockSpec((None, None, 2, tb), lambda h, j: (h, j, 0, 0))
    sd = jax.ShapeDtypeStruct
    return pl.pallas_call(
        body, grid=(nh // 2, nb), name=name,
        in_specs=[whole(0), block(kb), block(vb), cols, rows, whole(0), cols, whole(0)],
        out_specs=[whole(0), block(0), block(0), cols, rows],
        out_shape=[sd((nb, tb, d), BF16), sd((nb, tb, d), BF16), sd((nb, tb, d), BF16),
                   sd((nh // 2, nb, tb, 2), F32), sd((nh // 2, nb, 2, tb), F32)],
        scratch_shapes=[pltpu.VMEM((nb, tb, lanes), F32)],
        compiler_params=_cp(2),
    )(qkv, qkv, qkv, ccol, crow, o, lse, do)


def _cmul(ar, ai, br, bi):
    return ar * br - ai * bi, ar * bi + ai * br


def _s5_scan(name, lam, xin, hs=None):
    reverse = hs is not None
    _, seg, ns, w = xin.shape
    assert ns == SCAN_SEGMENTS
    wb = min(w, 2 * V7X_LANES)
    nsq = seg.bit_length() - 1
    assert (1 << nsq) == seg

    def body(*refs):
        if reverse:
            lam_ref, x_ref, h_ref, o_ref, dl_ref = refs
        else:
            lam_ref, x_ref, o_ref = refs
        lr = jnp.broadcast_to(lam_ref[0], (ns, wb))
        li = jnp.broadcast_to(lam_ref[1], (ns, wb))
        if reverse:
            li = -li
        zero = jnp.zeros((ns, wb), F32)
        at = (lambda n: seg - 1 - n) if reverse else (lambda n: n)

        def local(n, c):
            r = at(n)
            mr, mi = _cmul(lr, li, c[0], c[1])
            nr = mr + x_ref[0, r]
            ni = mi + x_ref[1, r]
            o_ref[0, r] = nr
            o_ref[1, r] = ni
            return nr, ni

        er, ei = lax.fori_loop(0, seg, local, (zero, zero))
        pr, pi = lr, li
        for _ in range(nsq):
            pr, pi = _cmul(pr, pi, pr, pi)
        sub = lax.broadcasted_iota(jnp.int32, (ns, wb), 0)

        def shifted(a, sh):
            if reverse:
                return jnp.where(sub < ns - sh, pltpu.roll(a, ns - sh, 0), 0.0)
            return jnp.where(sub >= sh, pltpu.roll(a, sh, 0), 0.0)

        xr, xi = er, ei
        sh = 1
        while sh < ns:
            tr, ti = _cmul(pr, pi, shifted(xr, sh), shifted(xi, sh))
            xr, xi = xr + tr, xi + ti
            pr, pi = _cmul(pr, pi, pr, pi)
            sh *= 2
        cr, ci = shifted(xr, 1), shifted(xi, 1)

        def fix(r, q):
            tr, ti = _cmul(q[0], q[1], cr, ci)
            gr = o_ref[0, r] + tr
            gi = o_ref[1, r] + ti
            o_ref[0, r] = gr
            o_ref[1, r] = gi
            return gr, gi

        if not reverse:
            def fixup(n, q):
                fix(n, q)
                return _cmul(q[0], q[1], lr, li)

            lax.fori_loop(0, seg, fixup, (lr, li))
            return

        def fixup_acc(n, c):
            qr, qi, ar, ai = c
            r = seg - 1 - n
            gr, gi = fix(r, (qr, qi))
            hr = h_ref[0, r - 1]
            hi = h_ref[1, r - 1]
            qr, qi = _cmul(qr, qi, lr, li)
            return qr, qi, ar + gr * hr + gi * hi, ai + gi * hr - gr * hi

        qr, qi, ar, ai = lax.fori_loop(0, seg - 1, fixup_acc, (lr, li, zero, zero))
        gr, gi = fix(0, (qr, qi))
        hr = jnp.where(sub >= 1, pltpu.roll(h_ref[0, seg - 1], 1, 0), 0.0)
        hi = jnp.where(sub >= 1, pltpu.roll(h_ref[1, seg - 1], 1, 0), 0.0)
        dl_ref[0] = ar + gr * hr + gi * hi
        dl_ref[1] = ai + gi * hr - gr * hi

    big = pl.BlockSpec((2, seg, ns, wb), lambda j: (0, 0, 0, j))
    lam_spec = pl.BlockSpec((2, 1, wb), lambda j: (0, 0, j))
    sd = jax.ShapeDtypeStruct
    if reverse:
        return pl.pallas_call(
            body, grid=(w // wb,), name=name, in_specs=[lam_spec, big, big],
            out_specs=[big, pl.BlockSpec((2, ns, wb), lambda j: (0, 0, j))],
            out_shape=[sd(xin.shape, F32), sd((2, ns, w), F32)], compiler_params=_cp(1),
        )(lam, xin, hs)
    return pl.pallas_call(
        body, grid=(w // wb,), name=name, in_specs=[lam_spec, big], out_specs=big,
        out_shape=sd(xin.shape, F32), compiler_params=_cp(1),
    )(lam, xin)


def _place():
    x, y, c = lax.axis_index("x"), lax.axis_index("y"), lax.axis_index("c")
    chips = [(1 - x, y), (x, 1 - y), (1 - x, 1 - y)]
    return x, y, c, chips


def _comm_params():
    return pltpu.CompilerParams(vmem_limit_bytes=VMEM_LIMIT)


def _cast_place(name, w):
    nl, r, c = w.shape
    tr = _tile(r, max(16, (1 << 20) // (4 * c) // 16 * 16), 16)

    def body(w_ref, o_ref):
        o_ref[...] = w_ref[...].astype(BF16)

    return pl.pallas_call(
        body, name=name, grid=(nl, r // tr),
        in_specs=[pl.BlockSpec((None, tr, c), lambda l, i: (l, i, 0))],
        out_specs=pl.BlockSpec((None, None, tr, c), lambda l, i: (l, _my_shard(), i, 0)),
        out_shape=jax.ShapeDtypeStruct((nl, N_CHIPS, r, c), BF16),
        compiler_params=_cp(2),
    )(w)


def _gather_shards(name, bufs):
    n = len(bufs)

    def body(*refs):
        outs = refs[n:2 * n]
        send_sems, recv_sems = refs[2 * n:]
        x, y, c, chips = _place()
        my = 2 * x + y
        sibling = (x, y, 1 - c)

        def part(t, shard, half):
            h = bufs[t].shape[2] // 2
            return outs[t].at[:, shard, pl.ds(half * h, h)]

        def copy(t, k, ref, to):
            return pltpu.make_async_remote_copy(src_ref=ref, dst_ref=ref, send_sem=send_sems.at[t, k],
                                                recv_sem=recv_sems.at[t, k], device_id=to, device_id_type=MESH)

        sent = []
        for t in range(n):
            for k, chip in enumerate(chips):
                sent.append(copy(t, k, part(t, my, c), (*chip, c)))
                sent[-1].start()
        for k, chip in enumerate(chips):
            shard = 2 * chip[0] + chip[1]
            for t in range(n):
                copy(t, k, part(t, shard, c), (*chip, c)).wait_recv()
                sent.append(copy(t, 3 + k, part(t, shard, c), sibling))
                sent[-1].start()
        for k, chip in enumerate(chips):
            shard = 2 * chip[0] + chip[1]
            for t in range(n):
                copy(t, 3 + k, part(t, shard, 1 - c), sibling).wait_recv()
        for cp in sent:
            cp.wait_send()

    return pl.pallas_call(
        body, name=name, in_specs=[ANY] * n, out_specs=[ANY] * n,
        out_shape=[jax.ShapeDtypeStruct(b.shape, b.dtype) for b in bufs],
        input_output_aliases={t: t for t in range(n)},
        scratch_shapes=[pltpu.SemaphoreType.DMA((n, 6)), pltpu.SemaphoreType.DMA((n, 6))],
        compiler_params=_comm_params(),
    )(*bufs)


HBM_SPEC = pl.BlockSpec(memory_space=pltpu.HBM)
SEM_SPEC = pl.BlockSpec(memory_space=pltpu.SEMAPHORE)


def _split_params():
    return pltpu.CompilerParams(has_side_effects=pltpu.SideEffectType.DATAFLOW_SIDE_EFFECTING)


def _gather_start(name, bufs, groups):
    n, ng = len(bufs), len(groups)

    def body(*refs):
        sems = refs[n:n + 2 * ng]
        outs = refs[n + 2 * ng:]
        x, y, c, chips = _place()
        my = 2 * x + y
        for gi, group in enumerate(groups):
            for idx, (t, layer) in enumerate(group):
                block = outs[t].at[layer, my]
                for k, chip in enumerate(chips):
                    pltpu.make_async_remote_copy(
                        src_ref=block, dst_ref=block, send_sem=sems[2 * gi].at[3 * idx + k],
                        recv_sem=sems[2 * gi + 1].at[3 * idx + k], device_id=(*chip, c), device_id_type=MESH).start()

    sem_shapes = []
    for group in groups:
        sem_shapes += [pltpu.SemaphoreType.DMA((3 * len(group),))] * 2
    res = pl.pallas_call(
        body, name=name, in_specs=[HBM_SPEC] * n,
        out_specs=[SEM_SPEC] * (2 * ng) + [HBM_SPEC] * n,
        out_shape=sem_shapes + [pltpu.HBM(b.shape, b.dtype) for b in bufs],
        input_output_aliases={t: 2 * ng + t for t in range(n)},
        compiler_params=_split_params(),
    )(*[pltpu.with_memory_space_constraint(b, pltpu.HBM) for b in bufs])
    sems = [(res[2 * gi], res[2 * gi + 1]) for gi in range(ng)]
    return sems, list(res[2 * ng:])


def _gather_wait(name, bufs, send_sems, recv_sems, after, group):
    n = len(bufs)

    def body(*refs):
        ss, rs = refs[n], refs[n + 1]
        outs = refs[n + 3:]
        x, y, c, chips = _place()
        my = 2 * x + y
        for idx, (t, layer) in enumerate(group):
            for k, chip in enumerate(chips):
                cp = pltpu.make_async_remote_copy(
                    src_ref=outs[t].at[layer, my], dst_ref=outs[t].at[layer, 2 * chip[0] + chip[1]],
                    send_sem=ss.at[3 * idx + k], recv_sem=rs.at[3 * idx + k], device_id=(*chip, c), device_id_type=MESH)
                cp.wait_send()
                cp.wait_recv()

    return list(pl.pallas_call(
        body, name=name, in_specs=[HBM_SPEC] * n + [SEM_SPEC, SEM_SPEC, ANY],
        out_specs=[HBM_SPEC] * n,
        out_shape=[pltpu.HBM(b.shape, b.dtype) for b in bufs],
        input_output_aliases={t: t for t in range(n)},
        compiler_params=_split_params(),
    )(*bufs, send_sems, recv_sems, after))


N_PARTS = 7


def _scatter_items(send, rx, items, c, chips, x, y):
    my = 2 * x + y
    out = []
    for i, (k, layer) in enumerate(items):
        h = send[k].shape[2] // 2
        for kk, chip in enumerate(chips):
            shard = 2 * chip[0] + chip[1]
            for hf in (0, 1):
                out.append((send[k].at[layer, shard, pl.ds(hf * h, h)], rx[k].at[2 * kk + c, layer],
                            N_PARTS * i + 2 * kk + hf, N_PARTS * i + 2 * kk + c, (*chip, hf)))
        out.append((send[k].at[layer, my, pl.ds((1 - c) * h, h)], rx[k].at[N_PARTS - 1, layer],
                    N_PARTS * i + N_PARTS - 1, N_PARTS * i + N_PARTS - 1, (x, y, 1 - c)))
    return out


def _scatter_start(name, send, rx, items):
    n = len(send)
    m = N_PARTS * len(items)

    def body(*refs):
        ssem, rsem = refs[2 * n], refs[2 * n + 1]
        s_out, r_out = refs[2 * n + 2:3 * n + 2], refs[3 * n + 2:4 * n + 2]
        x, y, c, chips = _place()
        for src, dst, si, ri, to in _scatter_items(s_out, r_out, items, c, chips, x, y):
            pltpu.make_async_remote_copy(src_ref=src, dst_ref=dst, send_sem=ssem.at[si], recv_sem=rsem.at[ri],
                                         device_id=to, device_id_type=MESH).start()
        refs[4 * n + 2][...] = jnp.zeros((V7X_SUBLANES, V7X_LANES), F32)

    res = pl.pallas_call(
        body, name=name, in_specs=[HBM_SPEC] * (2 * n),
        out_specs=[SEM_SPEC, SEM_SPEC] + [HBM_SPEC] * (2 * n) + [pl.BlockSpec(memory_space=pltpu.VMEM)],
        out_shape=[pltpu.SemaphoreType.DMA((m,)), pltpu.SemaphoreType.DMA((m,))]
        + [pltpu.HBM(b.shape, b.dtype) for b in list(send) + list(rx)]
        + [jax.ShapeDtypeStruct((V7X_SUBLANES, V7X_LANES), F32)],
        input_output_aliases={t: 2 + t for t in range(2 * n)},
        compiler_params=_split_params(),
    )(*[pltpu.with_memory_space_constraint(b, pltpu.HBM) for b in list(send) + list(rx)])
    return (res[0], res[1]), list(res[2:2 + n]), list(res[2 + n:2 + 2 * n]), res[2 + 2 * n][0, 0]


def _scatter_wait(name, send, rx, ssem, rsem, after, items):
    n = len(send)

    def body(*refs):
        ss, rs = refs[2 * n], refs[2 * n + 1]
        s_out, r_out = refs[2 * n + 3:3 * n + 3], refs[3 * n + 3:]
        x, y, c, chips = _place()
        for i, (src, dst, si, ri, to) in enumerate(_scatter_items(s_out, r_out, items, c, chips, x, y)):
            arrival = i % N_PARTS
            landed = r_out[items[i // N_PARTS][0]].at[arrival, items[i // N_PARTS][1]]
            cp = pltpu.make_async_remote_copy(src_ref=src, dst_ref=landed, send_sem=ss.at[si],
                                              recv_sem=rs.at[N_PARTS * (i // N_PARTS) + arrival],
                                              device_id=to, device_id_type=MESH)
            cp.wait_send()
            cp.wait_recv()

    res = pl.pallas_call(
        body, name=name, in_specs=[HBM_SPEC] * (2 * n) + [SEM_SPEC, SEM_SPEC, ANY],
        out_specs=[HBM_SPEC] * (2 * n),
        out_shape=[pltpu.HBM(b.shape, b.dtype) for b in list(send) + list(rx)],
        input_output_aliases={t: t for t in range(2 * n)},
        compiler_params=_split_params(),
    )(*send, *rx, ssem, rsem, after)
    return list(res[:n]), list(res[n:])


def _chip_sum(name, g, rx):
    nl, _, r, c = g.shape
    h = r // 2
    tr = _tile(h, max(V7X_SUBLANES * 2, (1 << 19) // (2 * c) // 16 * 16), 16)
    nb = h // tr

    def body(g_ref, r_ref, o_ref):
        acc = g_ref[...].astype(F32)
        for k in range(N_PARTS):
            acc = acc + r_ref[k].astype(F32)
        o_ref[...] = acc

    return pl.pallas_call(
        body, name=name, grid=(nl, nb),
        in_specs=[pl.BlockSpec((None, None, tr, c), lambda l, i: (l, _my_shard(), _my_core() * nb + i, 0)),
                  pl.BlockSpec((N_PARTS, None, tr, c), lambda l, i: (0, l, i, 0))],
        out_specs=pl.BlockSpec((None, tr, c), lambda l, i: (l, i, 0)),
        out_shape=jax.ShapeDtypeStruct((nl, h, c), F32),
        compiler_params=_cp(2),
    )(g, rx)


def _send_half(name, fs):
    n = len(fs)

    def body(*refs):
        ins, outs = refs[:n], refs[n:2 * n]
        send_sems, recv_sems = refs[2 * n:]
        x, y, c, _ = _place()
        cps = []
        for t in range(n):
            cps.append(pltpu.make_async_remote_copy(
                src_ref=ins[t], dst_ref=outs[t], send_sem=send_sems.at[t], recv_sem=recv_sems.at[t],
                device_id=(x, y, 1 - c), device_id_type=MESH))
            cps[-1].start()
        for cp in cps:
            cp.wait()

    return pl.pallas_call(
        body, name=name, in_specs=[ANY] * n, out_specs=[ANY] * n,
        out_shape=[jax.ShapeDtypeStruct(f.shape, f.dtype) for f in fs],
        scratch_shapes=[pltpu.SemaphoreType.DMA((n,)), pltpu.SemaphoreType.DMA((n,))],
        compiler_params=_comm_params(),
    )(*fs)


def _peers(x, y, c):
    rel = [(dx, dy, dc) for dx in (0, 1) for dy in (0, 1) for dc in (0, 1) if (dx, dy, dc) != (0, 0, 0)]
    return [(1 - x if dx else x, 1 - y if dy else y, 1 - c if dc else c) for dx, dy, dc in rel]


def _share_start(name, v, land):
    def body(v_ref, land_ref, ssem, rsem, v_out, land_out, token):
        x, y, c, _ = _place()
        me = 4 * x + 2 * y + c
        for k, peer in enumerate(_peers(x, y, c)):
            pltpu.make_async_remote_copy(src_ref=v_out, dst_ref=land_out.at[me], send_sem=ssem.at[k],
                                         recv_sem=rsem.at[k], device_id=peer, device_id_type=MESH).start()
        token[...] = jnp.zeros((V7X_SUBLANES, V7X_LANES), F32)

    res = pl.pallas_call(
        body, name=name, in_specs=[HBM_SPEC, HBM_SPEC],
        out_specs=[SEM_SPEC, SEM_SPEC, HBM_SPEC, HBM_SPEC, pl.BlockSpec(memory_space=pltpu.VMEM)],
        out_shape=[pltpu.SemaphoreType.DMA((N_DEV - 1,)), pltpu.SemaphoreType.DMA((N_DEV - 1,)),
                   pltpu.HBM(v.shape, v.dtype), pltpu.HBM(land.shape, land.dtype),
                   jax.ShapeDtypeStruct((V7X_SUBLANES, V7X_LANES), F32)],
        input_output_aliases={0: 2, 1: 3},
        compiler_params=_split_params(),
    )(pltpu.with_memory_space_constraint(v, pltpu.HBM), pltpu.with_memory_space_constraint(land, pltpu.HBM))
    return (res[0], res[1]), res[2], res[3], res[4][0, 0]


def _share_wait(name, v, land, ssem, rsem, after):
    def body(v_ref, land_ref, ss, rs, after_ref, v_out, land_out):
        x, y, c, _ = _place()
        for k, (px, py, pc) in enumerate(_peers(x, y, c)):
            cp = pltpu.make_async_remote_copy(src_ref=v_out, dst_ref=land_out.at[4 * px + 2 * py + pc],
                                              send_sem=ss.at[k], recv_sem=rs.at[k], device_id=(px, py, pc),
                                              device_id_type=MESH)
            cp.wait_send()
            cp.wait_recv()

    res = pl.pallas_call(
        body, name=name, in_specs=[HBM_SPEC, HBM_SPEC, SEM_SPEC, SEM_SPEC, ANY],
        out_specs=[HBM_SPEC, HBM_SPEC],
        out_shape=[pltpu.HBM(v.shape, v.dtype), pltpu.HBM(land.shape, land.dtype)],
        input_output_aliases={0: 0, 1: 1},
        compiler_params=_split_params(),
    )(v, land, ssem, rsem, after)
    return res[0], res[1]


def _sum_devices(name, v, land):
    r, c = v.shape
    tr = _tile(r, 512)

    def body(v_ref, land_ref, o_ref):
        x, y, cc, _ = _place()
        me = 4 * x + 2 * y + cc
        own = v_ref[...]
        acc = jnp.where(me == 0, own, land_ref[0])
        for k in range(1, N_DEV):
            acc = acc + jnp.where(me == k, own, land_ref[k])
        o_ref[...] = acc

    return pl.pallas_call(
        body, grid=(r // tr,), name=name,
        in_specs=[pl.BlockSpec((tr, c), lambda i: (i, 0)), pl.BlockSpec((N_DEV, tr, c), lambda i: (0, i, 0))],
        out_specs=pl.BlockSpec((tr, c), lambda i: (i, 0)),
        out_shape=jax.ShapeDtypeStruct((r, c), F32), compiler_params=_cp(1),
    )(v, land)


def _rows_view(wall):
    nl, s, r, c = wall.shape
    return wall.reshape(nl, s * r, c)


def _ffn_fwd(tag, alpha, x, w_in, w_out3, layer, g, b):
    h, a = _ffn_in(f"{tag}_in", x, w_in, layer)
    y, xhat, rstd = _mm_ln(f"{tag}_out", alpha, 0.5, a, w_out3, layer, x, g, b)
    return y, (x, h, a, xhat, rstd)


def _dx_ln(name, a, b, *, nk, a_blk, a_map, b_blk, b_map, alpha, dz, nxt):
    t, d = dz.shape
    tm = a_blk[-2]
    xhat, rstd, g = nxt
    through = xhat is not None

    def body(*refs):
        a_ref, b_ref, dz_ref = refs[:3]
        if through:
            xh_ref, rs_ref, g_ref, o_ref, dg_ref, db_ref, acc = refs[3:]
        else:
            g_ref, o_ref, acc = refs[3:]
        i, kk = pl.program_id(0), pl.program_id(1)
        p = lax.dot_general(a_ref[...].astype(BF16), b_ref[...].astype(BF16), _NT, preferred_element_type=F32)

        @pl.when(kk == 0)
        def _():
            acc[...] = p

        @pl.when(kk > 0)
        def _():
            acc[...] += p

        @pl.when(kk == nk - 1)
        def _():
            dy = alpha * dz_ref[...] + acc[...]
            if not through:
                o_ref[...] = dy + g_ref[...]
                return
            xh = xh_ref[...]
            dxh = dy * g_ref[...]
            m1 = jnp.mean(dxh, axis=-1, keepdims=True)
            m2 = jnp.mean(dxh * xh, axis=-1, keepdims=True)
            o_ref[...] = rs_ref[...] * (dxh - m1 - xh * m2)
            pg = jnp.sum(dy * xh, axis=0, keepdims=True)
            pb = jnp.sum(dy, axis=0, keepdims=True)

            @pl.when(i == 0)
            def _():
                dg_ref[...] = pg
                db_ref[...] = pb

            @pl.when(i > 0)
            def _():
                dg_ref[...] += pg
                db_ref[...] += pb

    row = lambda c: pl.BlockSpec((tm, c), lambda i, kk: (i, 0))
    vec = pl.BlockSpec((1, d), lambda i, kk: (0, 0))
    sd = jax.ShapeDtypeStruct
    in_specs = [pl.BlockSpec(a_blk, a_map), pl.BlockSpec(b_blk, b_map), row(d)]
    args = [a, b, dz]
    if through:
        in_specs += [row(d), row(1), vec]
        args += [xhat, rstd, g]
        out_specs, out_shape = [row(d), vec, vec], [sd((t, d), F32), sd((1, d), F32), sd((1, d), F32)]
    else:
        in_specs += [vec]
        args += [g]
        out_specs, out_shape = row(d), sd((t, d), F32)
    return pl.pallas_call(
        body, grid=(t // tm, nk), name=name, in_specs=in_specs, out_specs=out_specs, out_shape=out_shape,
        scratch_shapes=[pltpu.VMEM((tm, d), F32)], compiler_params=_cp(2),
    )(*args)


def _ffn_bwd(tag, alpha, dz, saved, w_in, w_out3, layer, g_win, g_wout3, grads_done, nxt, zrow):
    x, h, a, _, _ = saved
    t = x.shape[0]
    _, s, k, n = w_in.shape
    tm = _tile(t, 512)
    g_wout3 = _mm_tn(f"{tag}_dwout", a, dz, tm=n, scale=0.5, layer=layer, into=g_wout3)
    dh = _ffn_da(f"{tag}_da", dz, w_out3, layer, h, zrow)
    g_win = _mm(f"{tag}_dwin", x, dh, mode="tn", grid=(s, t // tm), kaxis=1,
                a_blk=(tm, k), a_map=lambda j, kk: (kk, 0),
                b_blk=(None, tm, n), b_map=lambda j, kk: (j // 2, kk, j % 2),
                o_shape=w_in.shape, o_blk=(None, None, k, n), o_map=lambda j, kk: (layer, j, 0, 0),
                o_dtype=g_win.dtype, into=g_win)
    zero = grads_done(g_win, g_wout3)
    return _dx_ln(f"{tag}_dx", dh, w_in, nk=s, a_blk=(None, tm, n), a_map=lambda i, kk: (kk // 2, i, kk % 2),
                  b_blk=(None, None, k, n), b_map=lambda i, kk: (layer, kk, 0, 0),
                  alpha=alpha, dz=dz, nxt=(nxt[0], nxt[1], nxt[2] + zero))


def _fox_fwd(tag, alpha, x, w_pad, bf, w_o3, layer, g, b):
    t, d = x.shape
    nh = bf.shape[1]
    tb = _tile(t, ATTN_BLOCK)
    nb = t // tb
    qkv = _mm_nn(f"{tag}_qkv", x, w_pad[:, :3 * d], o_dtype=BF16, tn=d).reshape(nb, tb, 3 * d)
    fl = _mm_nn(f"{tag}_gate", x, w_pad[:, 3 * d:])[:, :nh]
    cum = _fox_cumsum(f"{tag}_cum", fl, bf)
    ccol = cum.reshape(nb, tb, nh // 2, 2).transpose(2, 0, 1, 3)
    crow = cum.reshape(nb, tb, nh // 2, 2).transpose(2, 0, 3, 1)
    o, lse = _attn_fwd(f"{tag}_attn", qkv, ccol, crow, nh)
    o2 = o.reshape(t, d)
    y, xhat, rstd = _mm_ln(f"{tag}_oproj", alpha, 1.0, o2, w_o3, layer, x, g, b)
    return y, xhat, rstd, (x, qkv, ccol, crow, o, lse, fl)


def _fox_bwd(tag, alpha, dm, saved, w_pad, bf, w_o3, layer, g_wo3, grads_done, nxt):
    x, qkv, ccol, crow, o, lse, fl = saved
    t, d = x.shape
    nh = bf.shape[1]
    nb, tb, _ = qkv.shape
    tm = _tile(t, 512)
    g_wo3 = _mm_tn(f"{tag}_dwo", o.reshape(t, d), dm, layer=layer, into=g_wo3)
    do = _mm_nt(f"{tag}_do", dm, w_o3, layer=layer).reshape(nb, tb, d)
    dq, dk, dv, drow, dcol = _attn_bwd(f"{tag}_attn_bwd", qkv, ccol, crow, o, lse, do, nh)
    dcum = drow.transpose(1, 2, 0, 3).reshape(t, nh) - dcol.transpose(1, 3, 0, 2).reshape(t, nh)
    dfl, dbf = _fox_cumsum_bwd(f"{tag}_cum_bwd", dcum, fl, bf)
    pad = w_pad.shape[1] - 3 * d - nh
    dproj = jnp.concatenate([dq.reshape(t, d), dk.reshape(t, d), dv.reshape(t, d),
                             dfl.astype(BF16), jnp.zeros((t, pad), BF16)], axis=1)
    d_wpad = _mm_tn(f"{tag}_dwin", x, dproj, tn=_tile(w_pad.shape[1], 640, V7X_LANES))
    zero = grads_done(g_wo3, d_wpad)
    cols = w_pad.shape[1]
    out = _dx_ln(f"{tag}_dx", dproj, w_pad, nk=1, a_blk=(tm, cols), a_map=lambda i, kk: (i, 0),
                 b_blk=(d, cols), b_map=lambda i, kk: (0, 0), alpha=alpha, dz=dm, nxt=(nxt[0], nxt[1], nxt[2] + zero))
    return out, dbf


def _to_segments(a):
    t, d = a.shape
    return a.reshape(SCAN_SEGMENTS, t // SCAN_SEGMENTS, d).transpose(1, 0, 2).reshape(t, d)


def _from_segments(a):
    t, d = a.shape
    return a.reshape(t // SCAN_SEGMENTS, SCAN_SEGMENTS, d).transpose(1, 0, 2).reshape(t, d)


def _s5_discretise(a_re, a_im, log_dt, b_re, b_im):
    dt = jnp.exp(log_dt)[:, None]
    mag = jnp.exp(a_re * dt)
    ang = a_im * dt
    lb_re = mag * jnp.cos(ang)
    lb_im = mag * jnp.sin(ang)
    den = a_re * a_re + a_im * a_im
    nr = lb_re - 1.0
    ni = lb_im
    z_re = (nr * a_re + ni * a_im) / den
    z_im = (ni * a_re - nr * a_im) / den
    bb_re = z_re[..., None] * b_re - z_im[..., None] * b_im
    bb_im = z_re[..., None] * b_im + z_im[..., None] * b_re
    return lb_re, lb_im, bb_re, bb_im


S5_BLOCK_GROUPS = 8


def _blockdiag_in(bb):
    g, p, h = bb.shape
    e = jnp.eye(S5_BLOCK_GROUPS, dtype=bb.dtype)
    b4 = bb.reshape(g // S5_BLOCK_GROUPS, S5_BLOCK_GROUPS, p, h)
    return jnp.einsum("jgph,gf->jghfp", b4, e).reshape(g // S5_BLOCK_GROUPS, S5_BLOCK_GROUPS * h, S5_BLOCK_GROUPS * p)


def _blockdiag_in_grad(d):
    nj, gh, gp = d.shape
    h, p = gh // S5_BLOCK_GROUPS, gp // S5_BLOCK_GROUPS
    e = jnp.eye(S5_BLOCK_GROUPS, dtype=d.dtype)
    d6 = d.reshape(nj, S5_BLOCK_GROUPS, h, S5_BLOCK_GROUPS, p)
    return jnp.einsum("jghfp,gf->jgph", d6, e).reshape(nj * S5_BLOCK_GROUPS, p, h)


def _blockdiag_out(cc):
    g, h, p = cc.shape
    e = jnp.eye(S5_BLOCK_GROUPS, dtype=cc.dtype)
    c4 = cc.reshape(g // S5_BLOCK_GROUPS, S5_BLOCK_GROUPS, h, p)
    return jnp.einsum("jghp,gf->jfpgh", c4, e).reshape(g // S5_BLOCK_GROUPS, S5_BLOCK_GROUPS * p, S5_BLOCK_GROUPS * h)


def _blockdiag_out_grad(d):
    nj, gp, gh = d.shape
    h, p = gh // S5_BLOCK_GROUPS, gp // S5_BLOCK_GROUPS
    e = jnp.eye(S5_BLOCK_GROUPS, dtype=d.dtype)
    d6 = d.reshape(nj, S5_BLOCK_GROUPS, p, S5_BLOCK_GROUPS, h)
    return jnp.einsum("jfpgh,gf->jghp", d6, e).reshape(nj * S5_BLOCK_GROUPS, h, p)


def _s5_fwd(tag, x, prm, w_out, layer):
    a_re, a_im, log_dt, b_re, b_im, c_re, c_im, d_skip = prm
    t, d = x.shape
    g, p = a_re.shape
    w = g * p
    nj = g // S5_BLOCK_GROUPS
    cw, sw = S5_BLOCK_GROUPS * S5_GROUP, S5_BLOCK_GROUPS * p
    seg = t // SCAN_SEGMENTS
    tm = _tile(t, 4096)
    lb_re, lb_im, bb_re, bb_im = _s5_discretise(a_re, a_im, log_dt, b_re, b_im)
    lam = jnp.stack([lb_re.reshape(1, w), lb_im.reshape(1, w)])
    bs = jnp.stack([_blockdiag_in(bb_re), _blockdiag_in(bb_im)]).astype(BF16)
    cs = jnp.stack([_blockdiag_out(c_re), -_blockdiag_out(c_im)]).astype(BF16)
    dvec = d_skip.reshape(1, d)
    u = _to_segments(x)
    bu = _mm(f"{tag}_bu", u, bs, mode="nn", grid=(2, nj, t // tm), kaxis=None,
             a_blk=(tm, cw), a_map=lambda r, j, i: (i, j),
             b_blk=(None, None, cw, sw), b_map=lambda r, j, i: (r, j, 0, 0),
             o_shape=(2, t, w), o_blk=(None, tm, sw), o_map=lambda r, j, i: (r, i, j))
    hs = _s5_scan(f"{tag}_scan", lam, bu.reshape(2, seg, SCAN_SEGMENTS, w)).reshape(2, t, w)
    ych = _mm(f"{tag}_ch", hs, cs, mode="nn", grid=(nj, t // tm, 2), kaxis=2,
              a_blk=(None, tm, sw), a_map=lambda j, i, r: (r, i, j),
              b_blk=(None, None, sw, cw), b_map=lambda j, i, r: (r, j, 0, 0),
              o_shape=(t, d), o_blk=(tm, cw), o_map=lambda j, i, r: (i, j))
    ypre, act = _s5_act_fwd(f"{tag}_act", ych, u, dvec)
    vg = _mm_shards_nn(f"{tag}_wout", act, w_out, layer, F32)
    m = _from_segments(_glu_fwd(f"{tag}_glu", vg))
    return m, (u, lam, bs, cs, dvec, hs, ypre, act, vg)


def _s5_bwd(tag, dm, saved, prm, w_out, layer, g_wout):
    a_re, a_im, log_dt, b_re, b_im, c_re, c_im, d_skip = prm
    u, lam, bs, cs, dvec, hs, ypre, act, vg = saved
    t, d = u.shape
    g, p = a_re.shape
    w = g * p
    nj = g // S5_BLOCK_GROUPS
    cw, sw = S5_BLOCK_GROUPS * S5_GROUP, S5_BLOCK_GROUPS * p
    seg = t // SCAN_SEGMENTS
    tm = _tile(t, 4096)
    dvg = _glu_bwd(f"{tag}_glu_bwd", _to_segments(dm), vg)
    g_wout = _mm_shards_tn(f"{tag}_dwout", act, dvg, layer, g_wout)
    dact = _mm_shards_nt(f"{tag}_dact", dvg, w_out, layer)
    dypre, duskip, dd = _s5_act_bwd(f"{tag}_act_bwd", dact, ypre, u, dvec)
    dh = _mm(f"{tag}_dh", dypre, cs, mode="nt", grid=(2, nj, t // tm), kaxis=None,
             a_blk=(tm, cw), a_map=lambda r, j, i: (i, j),
             b_blk=(None, None, sw, cw), b_map=lambda r, j, i: (r, j, 0, 0),
             o_shape=(2, t, w), o_blk=(None, tm, sw), o_map=lambda r, j, i: (r, i, j))
    dcs = _mm(f"{tag}_dc", hs, dypre, mode="tn", grid=(2, nj, t // tm), kaxis=2,
              a_blk=(None, tm, sw), a_map=lambda r, j, i: (r, i, j),
              b_blk=(tm, cw), b_map=lambda r, j, i: (i, j),
              o_shape=(2, nj, sw, cw), o_blk=(None, None, sw, cw), o_map=lambda r, j, i: (r, j, 0, 0))
    gs, dlam8 = _s5_scan(f"{tag}_scan_bwd", lam, dh.reshape(2, seg, SCAN_SEGMENTS, w),
                         hs.reshape(2, seg, SCAN_SEGMENTS, w))
    gs = gs.reshape(2, t, w)
    du = _mm(f"{tag}_du", gs, bs, mode="nt", grid=(nj, t // tm, 2), kaxis=2,
             a_blk=(None, tm, sw), a_map=lambda j, i, r: (r, i, j),
             b_blk=(None, None, cw, sw), b_map=lambda j, i, r: (r, j, 0, 0),
             o_shape=(t, d), o_blk=(tm, cw), o_map=lambda j, i, r: (i, j))
    dbs = _mm(f"{tag}_db", u, gs, mode="tn", grid=(2, nj, t // tm), kaxis=2,
              a_blk=(tm, cw), a_map=lambda r, j, i: (i, j),
              b_blk=(None, tm, sw), b_map=lambda r, j, i: (r, i, j),
              o_shape=(2, nj, cw, sw), o_blk=(None, None, cw, sw), o_map=lambda r, j, i: (r, j, 0, 0))
    dx = _from_segments(du + duskip)
    dlam = jnp.sum(dlam8, axis=1).reshape(2, g, p)
    small = dict(dlb_re=dlam[0], dlb_im=dlam[1],
                 dbb_re=_blockdiag_in_grad(dbs[0]), dbb_im=_blockdiag_in_grad(dbs[1]),
                 dc_re=_blockdiag_out_grad(dcs[0]), dc_im=-_blockdiag_out_grad(dcs[1]),
                 dd=dd.reshape(g, S5_GROUP))
    return dx, g_wout, small


def _pack(pieces):
    rows = []
    for p in pieces:
        flat = p.reshape(-1).astype(F32)
        n = flat.shape[0]
        rows.append(jnp.pad(flat, (0, -n % V7X_LANES)).reshape(-1, V7X_LANES))
    buf = jnp.concatenate(rows, axis=0)
    return jnp.pad(buf, ((0, -buf.shape[0] % V7X_SUBLANES), (0, 0)))


def _unpack(buf, shapes):
    out, row = [], 0
    for s in shapes:
        n = math.prod(s)
        nr = -(-n // V7X_LANES)
        out.append(buf[row:row + nr].reshape(-1)[:n].reshape(s))
        row += nr
    return out


def kernel(x, ffn1_w_in, ffn1_w_out, ln1_g, ln1_b, lnm_g, lnm_b, ffn2_w_in, ffn2_w_out, ln2_g, ln2_b, fox_w_in, fox_b_f, fox_w_o, s5_a_re, s5_a_im, s5_log_dt, s5_b_re, s5_b_im, s5_c_re, s5_c_im, s5_d, s5_w_out, loss_target, m_ffn1_w_in, m_ffn1_w_out, m_ln1_g, m_ln1_b, m_lnm_g, m_lnm_b, m_ffn2_w_in, m_ffn2_w_out, m_ln2_g, m_ln2_b, m_fox_w_in, m_fox_b_f, m_fox_w_o, m_s5_a_re, m_s5_a_im, m_s5_log_dt, m_s5_b_re, m_s5_b_im, m_s5_c_re, m_s5_c_im, m_s5_d, m_s5_w_out, v_ffn1_w_in, v_ffn1_w_out, v_ln1_g, v_ln1_b, v_lnm_g, v_lnm_b, v_ffn2_w_in, v_ffn2_w_out, v_ln2_g, v_ln2_b, v_fox_w_in, v_fox_b_f, v_fox_w_o, v_s5_a_re, v_s5_a_im, v_s5_log_dt, v_s5_b_re, v_s5_b_im, v_s5_c_re, v_s5_c_im, v_s5_d, v_s5_w_out):
    big_names = ["ffn1_w_in", "ffn1_w_out", "ffn2_w_in", "ffn2_w_out", "fox_w_in", "fox_w_o", "s5_w_out"]
    small_names = ["ln1_g", "ln1_b", "lnm_g", "lnm_b", "ln2_g", "ln2_b", "fox_b_f", "s5_a_re", "s5_a_im", "s5_log_dt",
                   "s5_b_re", "s5_b_im", "s5_c_re", "s5_c_im", "s5_d"]
    out_order = ["ffn1_w_in", "ffn1_w_out", "ln1_g", "ln1_b", "lnm_g", "lnm_b", "ffn2_w_in", "ffn2_w_out", "ln2_g",
                 "ln2_b", "fox_w_in", "fox_b_f", "fox_w_o", "s5_a_re", "s5_a_im", "s5_log_dt", "s5_b_re", "s5_b_im",
                 "s5_c_re", "s5_c_im", "s5_d", "s5_w_out"]
    env = dict(locals())
    w = {n: env[n] for n in out_order}
    mom = {n: env["m_" + n] for n in out_order}
    vel = {n: env["v_" + n] for n in out_order}

    depth, d = ln1_g.shape
    t = x.shape[1]
    alpha = (2.0 * depth) ** 0.25
    x0 = x.reshape(t, d)
    tgt = loss_target.reshape(t, d)

    tix = {n: k for k, n in enumerate(big_names)}
    groups = []
    for i in range(depth):
        j = i // 2
        groups.append([(tix["ffn1_w_in"], i), (tix["ffn1_w_out"], i)])
        mixer = [(tix["fox_w_in"], j), (tix["fox_w_o"], j)] if i % 2 == 0 else [(tix["s5_w_out"], j)]
        groups.append(mixer + [(tix["ffn2_w_in"], i), (tix["ffn2_w_out"], i)])
    sems, bufs = _gather_start("gather_start", [_cast_place(f"cast_{n}", w[n]) for n in big_names], groups)
    full, rows3 = {}, {}

    def arrive(gi, after):
        nonlocal bufs
        bufs = _gather_wait(f"gather_wait_{gi}", bufs, sems[gi][0], sems[gi][1], after, groups[gi])
        full.update(zip(big_names, bufs))
        rows3.update({n: _rows_view(full[n]) for n in ("ffn1_w_out", "ffn2_w_out", "fox_w_o")})

    nh = fox_b_f.shape[1]
    fox_cols = 3 * d + nh
    fox_pad = -(-fox_cols // (5 * V7X_LANES)) * (5 * V7X_LANES)

    def fox_wpad(j):
        wf = full["fox_w_in"][j].transpose(1, 0, 2).reshape(d, fox_cols)
        return jnp.pad(wf, ((0, 0), (0, fox_pad - fox_cols)))

    def s5_params(j):
        return (s5_a_re[j], s5_a_im[j], s5_log_dt[j], s5_b_re[j], s5_b_im[j], s5_c_re[j], s5_c_im[j], s5_d[j])

    saved = []
    h = x0
    for i in range(depth):
        j = i // 2
        arrive(2 * i, h)
        h, s1 = _ffn_fwd(f"l{i}_ffn1", alpha, h, full["ffn1_w_in"], rows3["ffn1_w_out"], i,
                         ln1_g[i:i + 1], ln1_b[i:i + 1])
        arrive(2 * i + 1, h)
        if i % 2 == 0:
            h, xhat_m, rstd_m, sm = _fox_fwd(f"l{i}_fox", alpha, h, fox_wpad(j), fox_b_f[j:j + 1], rows3["fox_w_o"], j,
                                             lnm_g[i:i + 1], lnm_b[i:i + 1])
        else:
            m, sm = _s5_fwd(f"l{i}_s5", h, s5_params(j), full["s5_w_out"], j)
            h, xhat_m, rstd_m = _ln_fwd(f"l{i}_lnm", alpha, h, m, 1.0, lnm_g[i:i + 1], lnm_b[i:i + 1])
        h, s2 = _ffn_fwd(f"l{i}_ffn2", alpha, h, full["ffn2_w_in"], rows3["ffn2_w_out"], i,
                         ln2_g[i:i + 1], ln2_b[i:i + 1])
        saved.append((s1, sm, (xhat_m, rstd_m), s2))
    loss_part = _loss_sum("loss", h, tgt) * (0.5 / d)

    fox_in_names = [f"fox_w_in_l{j}" for j in range(fox_w_in.shape[0])]
    gshape = {n: full[n].shape for n in big_names if n != "fox_w_in"}
    gshape.update({n: (1,) + full["fox_w_in"].shape[1:] for n in fox_in_names})
    gbuf = {n: lax.empty(s, BF16) for n, s in gshape.items()}
    rxbuf = {n: lax.empty((N_PARTS, s[0], s[2] // 2, s[3]), BF16) for n, s in gshape.items()}
    pending = []

    zero = jnp.zeros((), F32)

    def scatter(tag, pairs):
        nonlocal zero
        names = list(dict.fromkeys(n for n, _ in pairs))
        items = [(names.index(n), layer) for n, layer in pairs]
        sem, send, rx, zero = _scatter_start(f"scatter_start_{tag}", [gbuf[n] for n in names],
                                             [rxbuf[n] for n in names], items)
        gbuf.update(zip(names, send))
        rxbuf.update(zip(names, rx))
        pending.append((tag, names, items, sem))

    gsmall = {n: [None] * w[n].shape[0] for n in small_names}
    s5_cot = [None] * s5_a_re.shape[0]
    cot_names = ["dlb_re", "dlb_im", "dbb_re", "dbb_im", "dc_re", "dc_im", "dd"]
    ln_names = ["ln1_g", "ln1_b", "lnm_g", "lnm_b", "ln2_g", "ln2_b"]

    def zrow():
        return jnp.zeros((1, d), F32) + zero

    def ffn_done(which, i):
        def done(g_win, g_wout3):
            gbuf[f"{which}_w_in"], gbuf[f"{which}_w_out"] = g_win, g_wout3.reshape(gshape[f"{which}_w_out"])
            scatter(f"l{i}_{which}", [(f"{which}_w_in", i), (f"{which}_w_out", i)])
            return zero
        return done

    _, _, _, (_, _, _, xhat_top, rstd_top) = saved[depth - 1]
    dz, dg, db = _ln_bwd("top_ln_bwd", [(h, 1.0 / d), (tgt, -1.0 / d)], xhat_top, rstd_top, ln2_g[depth - 1:depth])
    gsmall["ln2_g"][depth - 1], gsmall["ln2_b"][depth - 1] = dg, db
    grad_x = None
    for i in reversed(range(depth)):
        j = i // 2
        s1, sm, (xhat_m, rstd_m), s2 = saved[i]
        dz, dg, db = _ffn_bwd(f"l{i}_ffn2", alpha, dz, s2, full["ffn2_w_in"], rows3["ffn2_w_out"], i,
                              gbuf["ffn2_w_in"], _rows_view(gbuf["ffn2_w_out"]), ffn_done("ffn2", i),
                              (xhat_m, rstd_m, lnm_g[i:i + 1]), zrow())
        gsmall["lnm_g"][i], gsmall["lnm_b"][i] = dg, db
        ln1 = (s1[3], s1[4], ln1_g[i:i + 1])
        if i % 2 == 0:
            def fox_done(g_wo3, d_wpad, j=j, i=i):
                gbuf["fox_w_o"] = g_wo3.reshape(gshape["fox_w_o"])
                gbuf[fox_in_names[j]] = d_wpad[:, :fox_cols].reshape(d, N_CHIPS, -1).transpose(1, 0, 2)[None].astype(BF16)
                scatter(f"l{i}_fox", [("fox_w_o", j), (fox_in_names[j], 0)])
                return zero

            (dz, dg, db), gsmall["fox_b_f"][j] = _fox_bwd(
                f"l{i}_fox", alpha, dz, sm, fox_wpad(j), fox_b_f[j:j + 1], rows3["fox_w_o"], j,
                _rows_view(gbuf["fox_w_o"]), fox_done, ln1)
        else:
            dx, gbuf["s5_w_out"], s5_cot[j] = _s5_bwd(f"l{i}_s5", dz, sm, s5_params(j), full["s5_w_out"], j,
                                                      gbuf["s5_w_out"])
            scatter(f"l{i}_s5", [("s5_w_out", j)])
            dz, dg, db = _ln_bwd(f"l{i}_ln1_bwd", [(dz, alpha), (dx, 1.0)], ln1[0], ln1[1], ln1[2] + zero)
        gsmall["ln1_g"][i], gsmall["ln1_b"][i] = dg, db
        if i > 0:
            below = saved[i - 1][3]
            dz, dg, db = _ffn_bwd(f"l{i}_ffn1", alpha, dz, s1, full["ffn1_w_in"], rows3["ffn1_w_out"], i,
                                  gbuf["ffn1_w_in"], _rows_view(gbuf["ffn1_w_out"]), ffn_done("ffn1", i),
                                  (below[3], below[4], ln2_g[i - 1:i]), zrow())
            gsmall["ln2_g"][i - 1], gsmall["ln2_b"][i - 1] = dg, db
        else:
            pieces = [loss_part + zero] + [jnp.concatenate(gsmall[n], axis=0) for n in ln_names + ["fox_b_f"]]
            pieces += [jnp.stack([s5_cot[k][n] for k in range(len(s5_cot))]) for n in cot_names]
            mine = _pack(pieces)
            share_sem, mine, land, zero = _share_start("small_share_start", mine, lax.empty((N_DEV,) + mine.shape, F32))
            grad_x = _ffn_bwd(f"l{i}_ffn1", alpha, dz, s1, full["ffn1_w_in"], rows3["ffn1_w_out"], i,
                              gbuf["ffn1_w_in"], _rows_view(gbuf["ffn1_w_out"]), ffn_done("ffn1", i),
                              (None, None, jnp.zeros((1, d), F32)), zrow()).reshape(x.shape)

    shapes = [p.shape for p in pieces]
    mine, land = _share_wait("small_share_wait", mine, land, share_sem[0], share_sem[1], grad_x)
    summed = _unpack(_sum_devices("small_sum", mine, land), shapes)
    loss = summed[0].reshape(())
    gs_final = dict(zip(ln_names + ["fox_b_f"], summed[1:8]))
    cot = dict(zip(cot_names, summed[8:]))
    prm_names = ["s5_a_re", "s5_a_im", "s5_log_dt", "s5_b_re", "s5_b_im"]
    _, disc_vjp = jax.vjp(jax.vmap(_s5_discretise), *[w[n] for n in prm_names])
    for n, gval in zip(prm_names, disc_vjp((cot["dlb_re"], cot["dlb_im"], cot["dbb_re"], cot["dbb_im"]))):
        gs_final[n] = gval
    gs_final["s5_c_re"], gs_final["s5_c_im"], gs_final["s5_d"] = cot["dc_re"], cot["dc_im"], cot["dd"]

    grads, deltas, new_m, new_v = {}, {}, {}, {}
    small_shapes = [w[n].shape for n in small_names]
    for n in small_names:
        grads[n] = gs_final[n].reshape(w[n].shape)
    packed = [_pack([src[n] for n in small_names]) for src in (w, grads, mom, vel)]
    small_out = _adamw("adamw_small", *packed)
    for dst, buf in zip((deltas, new_m, new_v), small_out):
        for n, val in zip(small_names, _unpack(buf, small_shapes)):
            dst[n] = val

    for tag, names, items, sem in pending:
        send, rx = _scatter_wait(f"scatter_wait_{tag}", [gbuf[n] for n in names], [rxbuf[n] for n in names],
                                 sem[0], sem[1], small_out[0], items)
        gbuf.update(zip(names, send))
        rxbuf.update(zip(names, rx))
    half = {n: _chip_sum(f"grad_chip_sum_{n}", gbuf[n], rxbuf[n]) for n in gshape}
    half["fox_w_in"] = jnp.concatenate([half[n] for n in fox_in_names], axis=0)
    halves = [half[n] for n in big_names]
    theirs = _send_half("grad_send_half", halves)
    for n, mine_h, their_h in zip(big_names, halves, theirs):
        grads[n], deltas[n], new_m[n], new_v[n] = _adamw_join(f"adamw_{n}", w[n], mine_h, their_h, mom[n], vel[n])
    return (loss, grad_x, *[grads[n] for n in out_order], *[deltas[n] for n in out_order],
            *[new_m[n] for n in out_order], *[new_v[n] for n in out_order])
```

```python
import functools
import math

import jax
import jax.numpy as jnp
from jax import lax
from jax.experimental import pallas as pl
from jax.experimental.pallas import tpu as pltpu

F32 = jnp.float32
BF16 = jnp.bfloat16
LN_EPS = 1e-5
NEG_INF = -1e30
ADAM_LR = 0.001
ADAM_B1 = 0.9
ADAM_B2 = 0.999
ADAM_EPS = 1e-08
ADAM_WD = 0.01
ADAM_STEP = 10
S5_GROUP = 16
SCAN_SEGMENTS = 32
ATTN_BLOCK = 1024
V7X_SUBLANES = 8
V7X_LANES = 128
VMEM_LIMIT = 56 * 1024 * 1024
N_CHIPS = 4
N_DEV = 8
MESH = pl.DeviceIdType.MESH
ANY = pl.BlockSpec(memory_space=pl.ANY)


def _cp(n_grid, kaxis=None):
    sem = tuple("arbitrary" if (kaxis is None or i == kaxis) else "parallel" for i in range(n_grid))
    return pltpu.CompilerParams(dimension_semantics=sem, vmem_limit_bytes=VMEM_LIMIT)


def _tile(n, pref, mult=V7X_SUBLANES):
    if n <= pref:
        return n
    for t in range(pref, 0, -1):
        if n % t == 0 and t % mult == 0:
            return t
    return n


_CONTRACT = {"nn": ((1,), (0,)), "nt": ((1,), (1,)), "tn": ((0,), (0,))}


def _mm(name, a, b, *, mode, grid, kaxis, a_blk, a_map, b_blk, b_map, o_shape, o_blk, o_map, o_dtype=F32, scale=None,
        into=None):
    nk = 1 if kaxis is None else grid[kaxis]
    assert kaxis is None or kaxis == len(grid) - 1
    dims = (_CONTRACT[mode], ((), ()))
    use_acc = nk > 1 and o_dtype != F32
    acc_shape = tuple(d for d in o_blk if d is not None)

    def body(a_ref, b_ref, *rest):
        o_ref, scratch = (rest[1], rest[2:]) if into is not None else (rest[0], rest[1:])
        p = lax.dot_general(a_ref[...].astype(BF16), b_ref[...].astype(BF16), dims, preferred_element_type=F32)
        if nk == 1:
            if scale is not None:
                p = p * scale
            o_ref[...] = p.astype(o_dtype)
            return
        acc = scratch[0] if use_acc else o_ref
        k = pl.program_id(kaxis)

        @pl.when(k == 0)
        def _():
            acc[...] = p

        @pl.when(k > 0)
        def _():
            acc[...] += p

        if use_acc or scale is not None:
            @pl.when(k == nk - 1)
            def _():
                r = acc[...]
                if scale is not None:
                    r = r * scale
                o_ref[...] = r.astype(o_dtype)

    in_specs = [pl.BlockSpec(a_blk, a_map), pl.BlockSpec(b_blk, b_map)]
    args = [a, b]
    if into is not None:
        assert into.shape == tuple(o_shape) and into.dtype == o_dtype
        in_specs.append(ANY)
        args.append(into)
    return pl.pallas_call(
        body, grid=grid, name=name, in_specs=in_specs,
        out_specs=pl.BlockSpec(o_blk, o_map),
        out_shape=jax.ShapeDtypeStruct(o_shape, o_dtype),
        input_output_aliases={2: 0} if into is not None else {},
        scratch_shapes=[pltpu.VMEM(acc_shape, F32)] if use_acc else [],
        compiler_params=_cp(len(grid), kaxis),
    )(*args)


def _mm_shards_nn(name, a, wall, layer, o_dtype):
    t, k = a.shape
    _, s, _, n = wall.shape
    tm = _tile(t, 512)
    return _mm(name, a, wall, mode="nn", grid=(s, t // tm), kaxis=None,
               a_blk=(tm, k), a_map=lambda j, i: (i, 0),
               b_blk=(None, None, k, n), b_map=lambda j, i: (layer, j, 0, 0),
               o_shape=(t, s * n), o_blk=(tm, n), o_map=lambda j, i: (i, j), o_dtype=o_dtype)


def _mm_shards_nt(name, g, wall, layer):
    t = g.shape[0]
    _, s, k, n = wall.shape
    tm = _tile(t, 512)
    return _mm(name, g, wall, mode="nt", grid=(t // tm, s), kaxis=1,
               a_blk=(tm, n), a_map=lambda i, kk: (i, kk),
               b_blk=(None, None, k, n), b_map=lambda i, kk: (layer, kk, 0, 0),
               o_shape=(t, k), o_blk=(tm, k), o_map=lambda i, kk: (i, 0))


def _mm_shards_tn(name, a, g, layer, into):
    t, k = a.shape
    _, s, _, n = into.shape
    tk = _tile(t, 512)
    return _mm(name, a, g, mode="tn", grid=(s, t // tk), kaxis=1,
               a_blk=(tk, k), a_map=lambda j, kk: (kk, 0),
               b_blk=(tk, n), b_map=lambda j, kk: (kk, j),
               o_shape=into.shape, o_blk=(None, None, k, n), o_map=lambda j, kk: (layer, j, 0, 0),
               o_dtype=into.dtype, into=into)


def _mm_nn(name, a, w, o_dtype=F32, tn=None):
    t, k = a.shape
    n = w.shape[1]
    tm = _tile(t, 512)
    tn = n if tn is None else tn
    return _mm(name, a, w, mode="nn", grid=(n // tn, t // tm), kaxis=None,
               a_blk=(tm, k), a_map=lambda j, i: (i, 0),
               b_blk=(k, tn), b_map=lambda j, i: (0, j),
               o_shape=(t, n), o_blk=(tm, tn), o_map=lambda j, i: (i, j), o_dtype=o_dtype)


def _mm_nt(name, g, w, layer=None, o_dtype=F32):
    t, k = g.shape
    n = w.shape[-2]
    tm = _tile(t, 512)
    b_blk, b_map = ((n, k), lambda i: (0, 0)) if layer is None else ((None, n, k), lambda i: (layer, 0, 0))
    return _mm(name, g, w, mode="nt", grid=(t // tm,), kaxis=None,
               a_blk=(tm, k), a_map=lambda i: (i, 0), b_blk=b_blk, b_map=b_map,
               o_shape=(t, n), o_blk=(tm, n), o_map=lambda i: (i, 0), o_dtype=o_dtype)


def _mm_tn(name, a, g, tm=None, tn=None, scale=None, layer=None, into=None):
    t, m = a.shape
    n = g.shape[1]
    tk = _tile(t, 512)
    tm = m if tm is None else tm
    tn = n if tn is None else tn
    if layer is None:
        o_shape, o_blk, o_map, o_dtype = (m, n), (tm, tn), lambda i, j, kk: (i, j), F32
    else:
        o_shape, o_blk, o_map, o_dtype = into.shape, (None, tm, tn), lambda i, j, kk: (layer, i, j), into.dtype
    return _mm(name, a, g, mode="tn", grid=(m // tm, n // tn, t // tk), kaxis=2,
               a_blk=(tk, tm), a_map=lambda i, j, kk: (kk, i),
               b_blk=(tk, tn), b_map=lambda i, j, kk: (kk, j),
               o_shape=o_shape, o_blk=o_blk, o_map=o_map, o_dtype=o_dtype, scale=scale, into=into)


def _sigmoid(x):
    return 1.0 / (1.0 + jnp.exp(-x))


def _rows_call(name, body, t, tm, ins, in_cols, outs, acc_outs=()):
    in_specs = []
    for x, c in zip(ins, in_cols):
        if x.shape[0] == 1:
            in_specs.append(pl.BlockSpec((1, c), lambda i: (0, 0)))
        else:
            in_specs.append(pl.BlockSpec((tm, c), lambda i: (i, 0)))
    out_specs = [pl.BlockSpec((tm, s.shape[1]), lambda i: (i, 0)) for s in outs]
    out_specs += [pl.BlockSpec((1, s.shape[1]), lambda i: (0, 0)) for s in acc_outs]
    return pl.pallas_call(
        body, grid=(t // tm,), name=name, in_specs=in_specs, out_specs=out_specs,
        out_shape=list(outs) + list(acc_outs), compiler_params=_cp(1),
    )(*ins)


def _ln_fwd(name, alpha, x, r, coef, g, b):
    t, d = x.shape
    tm = _tile(t, 256)

    def body(x_ref, r_ref, g_ref, b_ref, y_ref, xh_ref, rs_ref):
        z = alpha * x_ref[...] + coef * r_ref[...]
        mu = jnp.mean(z, axis=-1, keepdims=True)
        zc = z - mu
        var = jnp.mean(zc * zc, axis=-1, keepdims=True)
        rstd = lax.rsqrt(var + LN_EPS)
        xh = zc * rstd
        y_ref[...] = xh * g_ref[...] + b_ref[...]
        xh_ref[...] = xh
        rs_ref[...] = rstd

    sd = jax.ShapeDtypeStruct
    return _rows_call(name, body, t, tm, [x, r, g, b], [d, d, d, d],
                      [sd((t, d), F32), sd((t, d), F32), sd((t, 1), F32)])


def _ln_bwd(name, terms, xhat, rstd, g):
    t, d = xhat.shape
    tm = _tile(t, 256)
    n = len(terms)
    coefs = [c for _, c in terms]

    def body(*refs):
        t_refs = refs[:n]
        xh_ref, rs_ref, g_ref, dz_ref, dg_ref, db_ref = refs[n:]
        dy = coefs[0] * t_refs[0][...]
        for c, r in zip(coefs[1:], t_refs[1:]):
            dy = dy + c * r[...]
        xh = xh_ref[...]
        dxh = dy * g_ref[...]
        m1 = jnp.mean(dxh, axis=-1, keepdims=True)
        m2 = jnp.mean(dxh * xh, axis=-1, keepdims=True)
        dz_ref[...] = rs_ref[...] * (dxh - m1 - xh * m2)
        pg = jnp.sum(dy * xh, axis=0, keepdims=True)
        pb = jnp.sum(dy, axis=0, keepdims=True)
        i = pl.program_id(0)

        @pl.when(i == 0)
        def _():
            dg_ref[...] = pg
            db_ref[...] = pb

        @pl.when(i > 0)
        def _():
            dg_ref[...] += pg
            db_ref[...] += pb

    sd = jax.ShapeDtypeStruct
    arrs = [a for a, _ in terms] + [xhat, rstd, g]
    cols = [d] * n + [d, 1, d]
    return _rows_call(name, body, t, tm, arrs, cols, [sd((t, d), F32)], [sd((1, d), F32), sd((1, d), F32)])


def _loss_sum(name, y, tgt):
    t, d = y.shape
    tm = _tile(t, 256)

    def body(y_ref, t_ref, o_ref):
        e = y_ref[...] - t_ref[...]
        s = jnp.sum(jnp.sum(e * e, axis=1, keepdims=True), axis=0, keepdims=True)
        i = pl.program_id(0)

        @pl.when(i == 0)
        def _():
            o_ref[...] = s

        @pl.when(i > 0)
        def _():
            o_ref[...] += s

    return _rows_call(name, body, t, tm, [y, tgt], [d, d], [], [jax.ShapeDtypeStruct((1, 1), F32)])[0]


def _ffn_in(name, x, wall, layer):
    t, k = x.shape
    n = wall.shape[3]
    tm = _tile(t, 512)

    def body(x_ref, wg_ref, wu_ref, h_ref, a_ref):
        xb = x_ref[...].astype(BF16)
        g = lax.dot_general(xb, wg_ref[...], _NN, preferred_element_type=F32)
        u = lax.dot_general(xb, wu_ref[...], _NN, preferred_element_type=F32)
        h_ref[0] = g.astype(BF16)
        h_ref[1] = u.astype(BF16)
        a_ref[...] = (g * _sigmoid(g) * u).astype(BF16)

    return pl.pallas_call(
        body, grid=(2, t // tm), name=name,
        in_specs=[pl.BlockSpec((tm, k), lambda j, i: (i, 0)),
                  pl.BlockSpec((None, None, k, n), lambda j, i: (layer, j, 0, 0)),
                  pl.BlockSpec((None, None, k, n), lambda j, i: (layer, 2 + j, 0, 0))],
        out_specs=[pl.BlockSpec((2, tm, n), lambda j, i: (0, i, j)), pl.BlockSpec((tm, n), lambda j, i: (i, j))],
        out_shape=[jax.ShapeDtypeStruct((2, t, 2 * n), BF16), jax.ShapeDtypeStruct((t, 2 * n), BF16)],
        compiler_params=_cp(2),
    )(x, wall, wall)


def _ffn_da(name, dz, w3, layer, h, zrow):
    t, k = dz.shape
    f = w3.shape[1]
    n = f // 2
    tm = _tile(t, 512)

    def body(dz_ref, z_ref, w_ref, g_ref, u_ref, dh_ref):
        d = 0.5 * lax.dot_general((dz_ref[...] + z_ref[...]).astype(BF16), w_ref[...], _NT, preferred_element_type=F32)
        g = g_ref[...].astype(F32)
        u = u_ref[...].astype(F32)
        sg = _sigmoid(g)
        dh_ref[0] = (d * u * sg * (1.0 + g * (1.0 - sg))).astype(BF16)
        dh_ref[1] = (d * g * sg).astype(BF16)

    return pl.pallas_call(
        body, grid=(2, t // tm), name=name,
        in_specs=[pl.BlockSpec((tm, k), lambda j, i: (i, 0)),
                  pl.BlockSpec((1, k), lambda j, i: (0, 0)),
                  pl.BlockSpec((None, n, k), lambda j, i: (layer, j, 0)),
                  pl.BlockSpec((None, tm, n), lambda j, i: (0, i, j)),
                  pl.BlockSpec((None, tm, n), lambda j, i: (1, i, j))],
        out_specs=pl.BlockSpec((2, tm, n), lambda j, i: (0, i, j)),
        out_shape=jax.ShapeDtypeStruct((2, t, f), BF16),
        compiler_params=_cp(2),
    )(dz, zrow, w3, h, h)


def _mm_ln(name, alpha, coef, a, w3, layer, x, g, b):
    t, k = a.shape
    d = w3.shape[2]
    tm = _tile(t, 512)

    def body(a_ref, w_ref, x_ref, g_ref, b_ref, y_ref, xh_ref, rs_ref):
        f = lax.dot_general(a_ref[...].astype(BF16), w_ref[...], _NN, preferred_element_type=F32)
        z = alpha * x_ref[...] + coef * f
        mu = jnp.mean(z, axis=-1, keepdims=True)
        zc = z - mu
        var = jnp.mean(zc * zc, axis=-1, keepdims=True)
        rstd = lax.rsqrt(var + LN_EPS)
        xh = zc * rstd
        y_ref[...] = xh * g_ref[...] + b_ref[...]
        xh_ref[...] = xh
        rs_ref[...] = rstd

    row = lambda c: pl.BlockSpec((tm, c), lambda i: (i, 0))
    vec = pl.BlockSpec((1, d), lambda i: (0, 0))
    sd = jax.ShapeDtypeStruct
    return pl.pallas_call(
        body, grid=(t // tm,), name=name,
        in_specs=[row(k), pl.BlockSpec((None, k, d), lambda i: (layer, 0, 0)), row(d), vec, vec],
        out_specs=[row(d), row(d), row(1)],
        out_shape=[sd((t, d), F32), sd((t, d), F32), sd((t, 1), F32)],
        compiler_params=_cp(1),
    )(a, w3, x, g, b)


_GELU_C = math.sqrt(2.0 / math.pi)


def _s5_act_fwd(name, ych, u, dvec):
    t, d = u.shape
    tm = _tile(t, 256)

    def body(y_ref, u_ref, d_ref, p_ref, a_ref):
        y = y_ref[...] + d_ref[...] * u_ref[...]
        p_ref[...] = y
        a_ref[...] = (0.5 * y * (1.0 + jnp.tanh(_GELU_C * (y + 0.044715 * y * y * y)))).astype(BF16)

    sd = jax.ShapeDtypeStruct
    return _rows_call(name, body, t, tm, [ych, u, dvec], [d, d, d], [sd((t, d), F32), sd((t, d), BF16)])


def _s5_act_bwd(name, dact, ypre, u, dvec):
    t, d = u.shape
    tm = _tile(t, 256)

    def body(da_ref, y_ref, u_ref, d_ref, dy_ref, ds_ref, dd_ref):
        y = y_ref[...]
        th = jnp.tanh(_GELU_C * (y + 0.044715 * y * y * y))
        dg = 0.5 * (1.0 + th) + 0.5 * y * (1.0 - th * th) * _GELU_C * (1.0 + 3.0 * 0.044715 * y * y)
        dy = da_ref[...] * dg
        dy_ref[...] = dy
        ds_ref[...] = dy * d_ref[...]
        pd = jnp.sum(dy * u_ref[...], axis=0, keepdims=True)
        i = pl.program_id(0)

        @pl.when(i == 0)
        def _():
            dd_ref[...] = pd

        @pl.when(i > 0)
        def _():
            dd_ref[...] += pd

    sd = jax.ShapeDtypeStruct
    return _rows_call(name, body, t, tm, [dact, ypre, u, dvec], [d, d, d, d],
                      [sd((t, d), F32), sd((t, d), F32)], [sd((1, d), F32)])


def _glu_fwd(name, vg):
    t, d2 = vg.shape
    d = d2 // 2
    tm = _tile(t, 256)

    def body(vg_ref, m_ref):
        m_ref[...] = vg_ref[:, :d] * _sigmoid(vg_ref[:, d:])

    return _rows_call(name, body, t, tm, [vg], [d2], [jax.ShapeDtypeStruct((t, d), F32)])[0]


def _glu_bwd(name, dm, vg):
    t, d2 = vg.shape
    d = d2 // 2
    tm = _tile(t, 256)

    def body(dm_ref, vg_ref, o_ref):
        sg = _sigmoid(vg_ref[:, d:])
        g = dm_ref[...]
        o_ref[:, :d] = (g * sg).astype(BF16)
        o_ref[:, d:] = (g * vg_ref[:, :d] * sg * (1.0 - sg)).astype(BF16)

    return _rows_call(name, body, t, tm, [dm, vg], [d, d2], [jax.ShapeDtypeStruct((t, d2), BF16)])[0]


def _adamw(name, w, g, m, v):
    r, c = w.shape
    tr = _tile(r, max(V7X_SUBLANES, (1 << 20) // (4 * c) // V7X_SUBLANES * V7X_SUBLANES))

    def body(w_ref, g_ref, m_ref, v_ref, d_ref, nm_ref, nv_ref):
        gg = g_ref[...]
        nm = ADAM_B1 * m_ref[...] + (1.0 - ADAM_B1) * gg
        nv = ADAM_B2 * v_ref[...] + (1.0 - ADAM_B2) * (gg * gg)
        m_hat = nm / (1.0 - ADAM_B1 ** ADAM_STEP)
        v_hat = nv / (1.0 - ADAM_B2 ** ADAM_STEP)
        d_ref[...] = -ADAM_LR * (m_hat / (jnp.sqrt(v_hat) + ADAM_EPS) + ADAM_WD * w_ref[...])
        nm_ref[...] = nm
        nv_ref[...] = nv

    sd = jax.ShapeDtypeStruct((r, c), F32)
    return _rows_call(name, body, r, tr, [w, g, m, v], [c] * 4, [sd, sd, sd])


def _my_shard():
    return 2 * lax.axis_index("x") + lax.axis_index("y")


def _my_core():
    return lax.axis_index("c")


def _adamw_join(name, w, mine, theirs, m, v):
    nl, r, c = w.shape
    h = r // 2
    tr = _tile(h, max(V7X_SUBLANES, (1 << 20) // (4 * c) // V7X_SUBLANES * V7X_SUBLANES))
    nb = h // tr

    def body(w_ref, a_ref, b_ref, m_ref, v_ref, g_ref, d_ref, nm_ref, nv_ref):
        gg = jnp.where(pl.program_id(1) == _my_core(), a_ref[...], b_ref[...])
        nm = ADAM_B1 * m_ref[...] + (1.0 - ADAM_B1) * gg
        nv = ADAM_B2 * v_ref[...] + (1.0 - ADAM_B2) * (gg * gg)
        m_hat = nm / (1.0 - ADAM_B1 ** ADAM_STEP)
        v_hat = nv / (1.0 - ADAM_B2 ** ADAM_STEP)
        g_ref[...] = gg
        d_ref[...] = -ADAM_LR * (m_hat / (jnp.sqrt(v_hat) + ADAM_EPS) + ADAM_WD * w_ref[...])
        nm_ref[...] = nm
        nv_ref[...] = nv

    full = pl.BlockSpec((None, tr, c), lambda l, hf, i: (l, hf * nb + i, 0))
    sd = jax.ShapeDtypeStruct((nl, r, c), F32)
    return pl.pallas_call(
        body, name=name, grid=(nl, 2, nb),
        in_specs=[full,
                  pl.BlockSpec((None, tr, c), lambda l, hf, i: (l, jnp.where(hf == _my_core(), i, 0), 0)),
                  pl.BlockSpec((None, tr, c), lambda l, hf, i: (l, jnp.where(hf == _my_core(), 0, i), 0)),
                  full, full],
        out_specs=[full, full, full, full],
        out_shape=[sd, sd, sd, sd],
        compiler_params=_cp(3),
    )(w, mine, theirs, m, v)


def _split3(x):
    hi = x.astype(BF16)
    r1 = x - hi.astype(F32)
    mid = r1.astype(BF16)
    lo = (r1 - mid.astype(F32)).astype(BF16)
    return hi, mid, lo


def _tri_sum(tri, x):
    dims = (((1,), (0,)), ((), ()))
    hi, mid, lo = _split3(x)
    out = lax.dot_general(tri, lo, dims, preferred_element_type=F32)
    out = out + lax.dot_general(tri, mid, dims, preferred_element_type=F32)
    return out + lax.dot_general(tri, hi, dims, preferred_element_type=F32)


def _fox_cumsum(name, fl, bf):
    t, h = fl.shape
    tb = _tile(t, 512)

    def body(fl_ref, bf_ref, c_ref, carry):
        i = pl.program_id(0)

        @pl.when(i == 0)
        def _():
            carry[...] = jnp.zeros_like(carry)

        x = fl_ref[...] + bf_ref[...]
        lf = jnp.minimum(x, 0.0) - jnp.log(1.0 + jnp.exp(-jnp.abs(x)))
        row = lax.broadcasted_iota(jnp.int32, (tb, tb), 0)
        col = lax.broadcasted_iota(jnp.int32, (tb, tb), 1)
        tri = jnp.where(row >= col, 1.0, 0.0).astype(BF16)
        c_ref[...] = _tri_sum(tri, lf) + carry[...]
        carry[...] += jnp.sum(lf, axis=0, keepdims=True)

    return pl.pallas_call(
        body, grid=(t // tb,), name=name,
        in_specs=[pl.BlockSpec((tb, h), lambda i: (i, 0)), pl.BlockSpec((1, h), lambda i: (0, 0))],
        out_specs=pl.BlockSpec((tb, h), lambda i: (i, 0)),
        out_shape=jax.ShapeDtypeStruct((t, h), F32),
        scratch_shapes=[pltpu.VMEM((1, h), F32)], compiler_params=_cp(1),
    )(fl, bf)


def _fox_cumsum_bwd(name, dcum, fl, bf):
    t, h = fl.shape
    tb = _tile(t, 512)
    nb = t // tb

    def body(dc_ref, fl_ref, bf_ref, df_ref, db_ref, carry):
        i = pl.program_id(0)

        @pl.when(i == 0)
        def _():
            carry[...] = jnp.zeros_like(carry)

        dc = dc_ref[...]
        row = lax.broadcasted_iota(jnp.int32, (tb, tb), 0)
        col = lax.broadcasted_iota(jnp.int32, (tb, tb), 1)
        tri = jnp.where(row <= col, 1.0, 0.0).astype(BF16)
        dlf = _tri_sum(tri, dc) + carry[...]
        carry[...] += jnp.sum(dc, axis=0, keepdims=True)
        x = fl_ref[...] + bf_ref[...]
        df = dlf / (1.0 + jnp.exp(x))
        df_ref[...] = df
        pb = jnp.sum(df, axis=0, keepdims=True)

        @pl.when(i == 0)
        def _():
            db_ref[...] = pb

        @pl.when(i > 0)
        def _():
            db_ref[...] += pb

    rev = lambda i: (nb - 1 - i, 0)
    return pl.pallas_call(
        body, grid=(nb,), name=name,
        in_specs=[pl.BlockSpec((tb, h), rev), pl.BlockSpec((tb, h), rev), pl.BlockSpec((1, h), lambda i: (0, 0))],
        out_specs=[pl.BlockSpec((tb, h), rev), pl.BlockSpec((1, h), lambda i: (0, 0))],
        out_shape=[jax.ShapeDtypeStruct((t, h), F32), jax.ShapeDtypeStruct((1, h), F32)],
        scratch_shapes=[pltpu.VMEM((1, h), F32)], compiler_params=_cp(1),
    )(dcum, fl, bf)


_NT = (((1,), (1,)), ((), ()))
_TN = (((0,), (0,)), ((), ()))
_NN = (((1,), (0,)), ((), ()))


def _causal_mask(s, r0):
    row = lax.broadcasted_iota(jnp.int32, s.shape, 0) + r0
    col = lax.broadcasted_iota(jnp.int32, s.shape, 1)
    return jnp.where(col <= row, s, NEG_INF)


def _first_head_lanes(hd):
    return lax.broadcasted_iota(jnp.int32, (1, 2 * hd), 1) < hd


def _attn_fwd(name, qkv, ccol, crow, nh):
    nb, tb, d3 = qkv.shape
    d = d3 // 3
    hd = d // nh
    lanes = 2 * hd
    assert lanes == V7X_LANES
    hb = tb // 2
    scale = 1.0 / math.sqrt(hd)

    def body(q_ref, k_ref, v_ref, cc_ref, cr_ref, o_ref, lse_ref):
        i = pl.program_id(1)
        first = _first_head_lanes(hd)
        q = q_ref[...] * scale
        res = []
        for hh in (0, 1):
            qh = jnp.where(first if hh == 0 else jnp.logical_not(first), q, jnp.zeros_like(q))
            cc = cc_ref[:, hh:hh + 1]

            def tile(j, carry, r0=0, nr=tb, ncol=tb, masked=False, qh=qh, cc=cc, hh=hh):
                m, l, acc = carry
                s = lax.dot_general(qh[r0:r0 + nr], k_ref[j, 0:ncol, :], _NT, preferred_element_type=F32)
                s = s + cc[r0:r0 + nr] - cr_ref[j][hh:hh + 1, 0:ncol]
                if masked:
                    s = _causal_mask(s, r0)
                m_new = jnp.maximum(m, jnp.max(s, axis=1, keepdims=True))
                p = jnp.exp(s - m_new)
                a = jnp.exp(m - m_new)
                l = a * l + jnp.sum(p, axis=1, keepdims=True)
                acc = a * acc + lax.dot_general(p.astype(BF16), v_ref[j, 0:ncol, :], _NN, preferred_element_type=F32)
                return m_new, l, acc

            init = (jnp.full((tb, 1), NEG_INF, F32), jnp.zeros((tb, 1), F32), jnp.zeros((tb, lanes), F32))
            before = lax.fori_loop(0, i, tile, init)
            top = tile(i, tuple(a[:hb] for a in before), 0, hb, hb, True)
            bottom = tile(i, tuple(a[hb:] for a in before), hb, hb, tb, True)
            m, l, acc = (jnp.concatenate([u, w], axis=0) for u, w in zip(top, bottom))
            res.append((acc / l, m + jnp.log(l)))
        o_ref[...] = jnp.where(first, res[0][0], res[1][0])
        lse_ref[:, 0:1] = res[0][1]
        lse_ref[:, 1:2] = res[1][1]

    kb, vb = d // lanes, 2 * d // lanes
    return pl.pallas_call(
        body, grid=(nh // 2, nb), name=name,
        in_specs=[pl.BlockSpec((None, tb, lanes), lambda h, i: (i, 0, h)),
                  pl.BlockSpec((nb, tb, lanes), lambda h, i: (0, 0, kb + h)),
                  pl.BlockSpec((nb, tb, lanes), lambda h, i: (0, 0, vb + h)),
                  pl.BlockSpec((None, None, tb, 2), lambda h, i: (h, i, 0, 0)),
                  pl.BlockSpec((None, nb, 2, tb), lambda h, i: (h, 0, 0, 0))],
        out_specs=[pl.BlockSpec((None, tb, lanes), lambda h, i: (i, 0, h)),
                   pl.BlockSpec((None, None, tb, 2), lambda h, i: (h, i, 0, 0))],
        out_shape=[jax.ShapeDtypeStruct((nb, tb, d), F32), jax.ShapeDtypeStruct((nh // 2, nb, tb, 2), F32)],
        compiler_params=_cp(2),
    )(qkv, qkv, qkv, ccol, crow)


def _attn_bwd(name, qkv, ccol, crow, o, lse, do, nh):
    nb, tb, d3 = qkv.shape
    d = d3 // 3
    hd = d // nh
    lanes = 2 * hd
    hb = tb // 2
    scale = 1.0 / math.sqrt(hd)

    def body(q_ref, k_ref, v_ref, cc_ref, cr_ref, o_ref, lse_ref, do_ref, dq_ref, dk_ref, dv_ref, dr_ref, dc_ref, dq_acc):
        j = pl.program_id(1)

        @pl.when(j == 0)
        def _():
            dq_acc[...] = jnp.zeros_like(dq_acc)
            dr_ref[...] = jnp.zeros_like(dr_ref)

        first = _first_head_lanes(hd)
        kj = k_ref[...]
        vj = v_ref[...]
        dk = jnp.zeros((tb, lanes), F32)
        dv = jnp.zeros((tb, lanes), F32)
        for hh in (0, 1):
            mine = first if hh == 0 else jnp.logical_not(first)
            cr = cr_ref[hh:hh + 1, :]

            def tile(i, carry, r0=0, nr=tb, ncol=tb, masked=False, mine=mine, cr=cr, hh=hh):
                dk, dv, dc = carry
                rows = pl.ds(r0, nr)
                qi = q_ref[i, rows, :] * scale
                qh = jnp.where(mine, qi, jnp.zeros_like(qi))
                doh = jnp.where(mine, do_ref[i, rows, :], 0.0)
                dob = doh.astype(BF16)
                di = jnp.sum(doh * o_ref[i, rows, :], axis=1, keepdims=True)
                s = lax.dot_general(qh, kj[:ncol], _NT, preferred_element_type=F32)
                s = s + cc_ref[i, rows, hh:hh + 1] - cr[:, :ncol]
                if masked:
                    s = _causal_mask(s, r0)
                p = jnp.exp(s - lse_ref[i, rows, hh:hh + 1])
                dp = lax.dot_general(dob, vj[:ncol], _NT, preferred_element_type=F32)
                ds = p * (dp - di)
                dsb = ds.astype(BF16)
                dvc = lax.dot_general(p.astype(BF16), dob, _TN, preferred_element_type=F32)
                dkc = lax.dot_general(dsb, qh, _TN, preferred_element_type=F32)
                dcc = jnp.sum(ds, axis=0, keepdims=True)
                if ncol < tb:
                    dvc = jnp.concatenate([dvc, jnp.zeros((tb - ncol, lanes), F32)], axis=0)
                    dkc = jnp.concatenate([dkc, jnp.zeros((tb - ncol, lanes), F32)], axis=0)
                    dcc = jnp.concatenate([dcc, jnp.zeros((1, tb - ncol), F32)], axis=1)
                dq = lax.dot_general(dsb, kj[:ncol], _NN, preferred_element_type=F32) * scale
                dq_acc[i, rows, :] += jnp.where(mine, dq, 0.0)
                dr_ref[i, rows, hh:hh + 1] += jnp.sum(ds, axis=1, keepdims=True)
                return dk + dkc, dv + dvc, dc + dcc

            carry = tile(j, (dk, dv, jnp.zeros((1, tb), F32)), 0, hb, hb, True)
            carry = tile(j, carry, hb, hb, tb, True)
            dk, dv, dc = lax.fori_loop(j + 1, nb, tile, carry)
            dc_ref[hh:hh + 1, :] = dc
        dk_ref[...] = dk.astype(BF16)
        dv_ref[...] = dv.astype(BF16)

        @pl.when(j == nb - 1)
        def _():
            dq_ref[...] = dq_acc[...].astype(BF16)

    kb, vb = d // lanes, 2 * d // lanes
    whole = lambda c: pl.BlockSpec((nb, tb, lanes), lambda h, j: (0, 0, c + h))
    block = lambda c: pl.BlockSpec((None, tb, lanes), lambda h, j: (j, 0, c + h))
    cols = pl.BlockSpec((None, nb, tb, 2), lambda h, j: (h, 0, 0, 0))
    rows = pl.BlockSpec((None, None, 2, tb), lambda h, j: (h, j, 0, 0))
    sd = jax.ShapeDtypeStruct
    return pl.pallas_call(
        body, grid=(nh // 2, nb), name=name,
        in_specs=[whole(0), block(kb), block(vb), cols, rows, whole(0), cols, whole(0)],
        out_specs=[whole(0), block(0), block(0), cols, rows],
        out_shape=[sd((nb, tb, d), BF16), sd((nb, tb, d), BF16), sd((nb, tb, d), BF16),
                   sd((nh // 2, nb, tb, 2), F32), sd((nh // 2, nb, 2, tb), F32)],
        scratch_shapes=[pltpu.VMEM((nb, tb, lanes), F32)],
        compiler_params=_cp(2),
    )(qkv, qkv, qkv, ccol, crow, o, lse, do)


def _cmul(ar, ai, br, bi):
    return ar * br - ai * bi, ar * bi + ai * br


def _s5_scan(name, lam, xin, hs=None):
    reverse = hs is not None
    _, seg, ns, w = xin.shape
    assert ns == SCAN_SEGMENTS
    wb = min(w, 2 * V7X_LANES)
    nsq = seg.bit_length() - 1
    assert (1 << nsq) == seg

    def body(*refs):
        if reverse:
            lam_ref, x_ref, h_ref, o_ref, dl_ref = refs
        else:
            lam_ref, x_ref, o_ref = refs
        lr = jnp.broadcast_to(lam_ref[0], (ns, wb))
        li = jnp.broadcast_to(lam_ref[1], (ns, wb))
        if reverse:
            li = -li
        zero = jnp.zeros((ns, wb), F32)
        at = (lambda n: seg - 1 - n) if reverse else (lambda n: n)

        def local(n, c):
            r = at(n)
            mr, mi = _cmul(lr, li, c[0], c[1])
            nr = mr + x_ref[0, r]
            ni = mi + x_ref[1, r]
            o_ref[0, r] = nr
            o_ref[1, r] = ni
            return nr, ni

        er, ei = lax.fori_loop(0, seg, local, (zero, zero))
        pr, pi = lr, li
        for _ in range(nsq):
            pr, pi = _cmul(pr, pi, pr, pi)
        sub = lax.broadcasted_iota(jnp.int32, (ns, wb), 0)

        def shifted(a, sh):
            if reverse:
                return jnp.where(sub < ns - sh, pltpu.roll(a, ns - sh, 0), 0.0)
            return jnp.where(sub >= sh, pltpu.roll(a, sh, 0), 0.0)

        xr, xi = er, ei
        sh = 1
        while sh < ns:
            tr, ti = _cmul(pr, pi, shifted(xr, sh), shifted(xi, sh))
            xr, xi = xr + tr, xi + ti
            pr, pi = _cmul(pr, pi, pr, pi)
            sh *= 2
        cr, ci = shifted(xr, 1), shifted(xi, 1)

        def fix(r, q):
            tr, ti = _cmul(q[0], q[1], cr, ci)
            gr = o_ref[0, r] + tr
            gi = o_ref[1, r] + ti
            o_ref[0, r] = gr
            o_ref[1, r] = gi
            return gr, gi

        if not reverse:
            def fixup(n, q):
                fix(n, q)
                return _cmul(q[0], q[1], lr, li)

            lax.fori_loop(0, seg, fixup, (lr, li))
            return

        def fixup_acc(n, c):
            qr, qi, ar, ai = c
            r = seg - 1 - n
            gr, gi = fix(r, (qr, qi))
            hr = h_ref[0, r - 1]
            hi = h_ref[1, r - 1]
            qr, qi = _cmul(qr, qi, lr, li)
            return qr, qi, ar + gr * hr + gi * hi, ai + gi * hr - gr * hi

        qr, qi, ar, ai = lax.fori_loop(0, seg - 1, fixup_acc, (lr, li, zero, zero))
        gr, gi = fix(0, (qr, qi))
        hr = jnp.where(sub >= 1, pltpu.roll(h_ref[0, seg - 1], 1, 0), 0.0)
        hi = jnp.where(sub >= 1, pltpu.roll(h_ref[1, seg - 1], 1, 0), 0.0)
        dl_ref[0] = ar + gr * hr + gi * hi
        dl_ref[1] = ai + gi * hr - gr * hi

    big = pl.BlockSpec((2, seg, ns, wb), lambda j: (0, 0, 0, j))
    lam_spec = pl.BlockSpec((2, 1, wb), lambda j: (0, 0, j))
    sd = jax.ShapeDtypeStruct
    if reverse:
        return pl.pallas_call(
            body, grid=(w // wb,), name=name, in_specs=[lam_spec, big, big],
            out_specs=[big, pl.BlockSpec((2, ns, wb), lambda j: (0, 0, j))],
            out_shape=[sd(xin.shape, F32), sd((2, ns, w), F32)], compiler_params=_cp(1),
        )(lam, xin, hs)
    return pl.pallas_call(
        body, grid=(w // wb,), name=name, in_specs=[lam_spec, big], out_specs=big,
        out_shape=sd(xin.shape, F32), compiler_params=_cp(1),
    )(lam, xin)


def _place():
    x, y, c = lax.axis_index("x"), lax.axis_index("y"), lax.axis_index("c")
    chips = [(1 - x, y), (x, 1 - y), (1 - x, 1 - y)]
    return x, y, c, chips


def _comm_params():
    return pltpu.CompilerParams(vmem_limit_bytes=VMEM_LIMIT)


def _cast_place(name, w):
    nl, r, c = w.shape
    tr = _tile(r, max(16, (1 << 20) // (4 * c) // 16 * 16), 16)

    def body(w_ref, o_ref):
        o_ref[...] = w_ref[...].astype(BF16)

    return pl.pallas_call(
        body, name=name, grid=(nl, r // tr),
        in_specs=[pl.BlockSpec((None, tr, c), lambda l, i: (l, i, 0))],
        out_specs=pl.BlockSpec((None, None, tr, c), lambda l, i: (l, _my_shard(), i, 0)),
        out_shape=jax.ShapeDtypeStruct((nl, N_CHIPS, r, c), BF16),
        compiler_params=_cp(2),
    )(w)


def _gather_shards(name, bufs):
    n = len(bufs)

    def body(*refs):
        outs = refs[n:2 * n]
        send_sems, recv_sems = refs[2 * n:]
        x, y, c, chips = _place()
        my = 2 * x + y
        sibling = (x, y, 1 - c)

        def part(t, shard, half):
            h = bufs[t].shape[2] // 2
            return outs[t].at[:, shard, pl.ds(half * h, h)]

        def copy(t, k, ref, to):
            return pltpu.make_async_remote_copy(src_ref=ref, dst_ref=ref, send_sem=send_sems.at[t, k],
                                                recv_sem=recv_sems.at[t, k], device_id=to, device_id_type=MESH)

        sent = []
        for t in range(n):
            for k, chip in enumerate(chips):
                sent.append(copy(t, k, part(t, my, c), (*chip, c)))
                sent[-1].start()
        for k, chip in enumerate(chips):
            shard = 2 * chip[0] + chip[1]
            for t in range(n):
                copy(t, k, part(t, shard, c), (*chip, c)).wait_recv()
                sent.append(copy(t, 3 + k, part(t, shard, c), sibling))
                sent[-1].start()
        for k, chip in enumerate(chips):
            shard = 2 * chip[0] + chip[1]
            for t in range(n):
                copy(t, 3 + k, part(t, shard, 1 - c), sibling).wait_recv()
        for cp in sent:
            cp.wait_send()

    return pl.pallas_call(
        body, name=name, in_specs=[ANY] * n, out_specs=[ANY] * n,
        out_shape=[jax.ShapeDtypeStruct(b.shape, b.dtype) for b in bufs],
        input_output_aliases={t: t for t in range(n)},
        scratch_shapes=[pltpu.SemaphoreType.DMA((n, 6)), pltpu.SemaphoreType.DMA((n, 6))],
        compiler_params=_comm_params(),
    )(*bufs)


HBM_SPEC = pl.BlockSpec(memory_space=pltpu.HBM)
SEM_SPEC = pl.BlockSpec(memory_space=pltpu.SEMAPHORE)


def _split_params():
    return pltpu.CompilerParams(has_side_effects=pltpu.SideEffectType.DATAFLOW_SIDE_EFFECTING)


def _gather_start(name, bufs, groups):
    n, ng = len(bufs), len(groups)

    def body(*refs):
        sems = refs[n:n + 2 * ng]
        outs = refs[n + 2 * ng:]
        x, y, c, chips = _place()
        my = 2 * x + y
        for gi, group in enumerate(groups):
            for idx, (t, layer) in enumerate(group):
                block = outs[t].at[layer, my]
                for k, chip in enumerate(chips):
                    pltpu.make_async_remote_copy(
                        src_ref=block, dst_ref=block, send_sem=sems[2 * gi].at[3 * idx + k],
                        recv_sem=sems[2 * gi + 1].at[3 * idx + k], device_id=(*chip, c), device_id_type=MESH).start()

    sem_shapes = []
    for group in groups:
        sem_shapes += [pltpu.SemaphoreType.DMA((3 * len(group),))] * 2
    res = pl.pallas_call(
        body, name=name, in_specs=[HBM_SPEC] * n,
        out_specs=[SEM_SPEC] * (2 * ng) + [HBM_SPEC] * n,
        out_shape=sem_shapes + [pltpu.HBM(b.shape, b.dtype) for b in bufs],
        input_output_aliases={t: 2 * ng + t for t in range(n)},
        compiler_params=_split_params(),
    )(*[pltpu.with_memory_space_constraint(b, pltpu.HBM) for b in bufs])
    sems = [(res[2 * gi], res[2 * gi + 1]) for gi in range(ng)]
    return sems, list(res[2 * ng:])


def _gather_wait(name, bufs, send_sems, recv_sems, after, group):
    n = len(bufs)

    def body(*refs):
        ss, rs = refs[n], refs[n + 1]
        outs = refs[n + 3:]
        x, y, c, chips = _place()
        my = 2 * x + y
        for idx, (t, layer) in enumerate(group):
            for k, chip in enumerate(chips):
                cp = pltpu.make_async_remote_copy(
                    src_ref=outs[t].at[layer, my], dst_ref=outs[t].at[layer, 2 * chip[0] + chip[1]],
                    send_sem=ss.at[3 * idx + k], recv_sem=rs.at[3 * idx + k], device_id=(*chip, c), device_id_type=MESH)
                cp.wait_send()
                cp.wait_recv()

    return list(pl.pallas_call(
        body, name=name, in_specs=[HBM_SPEC] * n + [SEM_SPEC, SEM_SPEC, ANY],
        out_specs=[HBM_SPEC] * n,
        out_shape=[pltpu.HBM(b.shape, b.dtype) for b in bufs],
        input_output_aliases={t: t for t in range(n)},
        compiler_params=_split_params(),
    )(*bufs, send_sems, recv_sems, after))


N_PARTS = 7


def _scatter_items(send, rx, items, c, chips, x, y):
    my = 2 * x + y
    out = []
    for i, (k, layer) in enumerate(items):
        h = send[k].shape[2] // 2
        for kk, chip in enumerate(chips):
            shard = 2 * chip[0] + chip[1]
            for hf in (0, 1):
                out.append((send[k].at[layer, shard, pl.ds(hf * h, h)], rx[k].at[2 * kk + c, layer],
                            N_PARTS * i + 2 * kk + hf, N_PARTS * i + 2 * kk + c, (*chip, hf)))
        out.append((send[k].at[layer, my, pl.ds((1 - c) * h, h)], rx[k].at[N_PARTS - 1, layer],
                    N_PARTS * i + N_PARTS - 1, N_PARTS * i + N_PARTS - 1, (x, y, 1 - c)))
    return out


def _scatter_start(name, send, rx, items):
    n = len(send)
    m = N_PARTS * len(items)

    def body(*refs):
        ssem, rsem = refs[2 * n], refs[2 * n + 1]
        s_out, r_out = refs[2 * n + 2:3 * n + 2], refs[3 * n + 2:4 * n + 2]
        x, y, c, chips = _place()
        for src, dst, si, ri, to in _scatter_items(s_out, r_out, items, c, chips, x, y):
            pltpu.make_async_remote_copy(src_ref=src, dst_ref=dst, send_sem=ssem.at[si], recv_sem=rsem.at[ri],
                                         device_id=to, device_id_type=MESH).start()
        refs[4 * n + 2][...] = jnp.zeros((V7X_SUBLANES, V7X_LANES), F32)

    res = pl.pallas_call(
        body, name=name, in_specs=[HBM_SPEC] * (2 * n),
        out_specs=[SEM_SPEC, SEM_SPEC] + [HBM_SPEC] * (2 * n) + [pl.BlockSpec(memory_space=pltpu.VMEM)],
        out_shape=[pltpu.SemaphoreType.DMA((m,)), pltpu.SemaphoreType.DMA((m,))]
        + [pltpu.HBM(b.shape, b.dtype) for b in list(send) + list(rx)]
        + [jax.ShapeDtypeStruct((V7X_SUBLANES, V7X_LANES), F32)],
        input_output_aliases={t: 2 + t for t in range(2 * n)},
        compiler_params=_split_params(),
    )(*[pltpu.with_memory_space_constraint(b, pltpu.HBM) for b in list(send) + list(rx)])
    return (res[0], res[1]), list(res[2:2 + n]), list(res[2 + n:2 + 2 * n]), res[2 + 2 * n][0, 0]


def _scatter_wait(name, send, rx, ssem, rsem, after, items):
    n = len(send)

    def body(*refs):
        ss, rs = refs[2 * n], refs[2 * n + 1]
        s_out, r_out = refs[2 * n + 3:3 * n + 3], refs[3 * n + 3:]
        x, y, c, chips = _place()
        for i, (src, dst, si, ri, to) in enumerate(_scatter_items(s_out, r_out, items, c, chips, x, y)):
            arrival = i % N_PARTS
            landed = r_out[items[i // N_PARTS][0]].at[arrival, items[i // N_PARTS][1]]
            cp = pltpu.make_async_remote_copy(src_ref=src, dst_ref=landed, send_sem=ss.at[si],
                                              recv_sem=rs.at[N_PARTS * (i // N_PARTS) + arrival],
                                              device_id=to, device_id_type=MESH)
            cp.wait_send()
            cp.wait_recv()

    res = pl.pallas_call(
        body, name=name, in_specs=[HBM_SPEC] * (2 * n) + [SEM_SPEC, SEM_SPEC, ANY],
        out_specs=[HBM_SPEC] * (2 * n),
        out_shape=[pltpu.HBM(b.shape, b.dtype) for b in list(send) + list(rx)],
        input_output_aliases={t: t for t in range(2 * n)},
        compiler_params=_split_params(),
    )(*send, *rx, ssem, rsem, after)
    return list(res[:n]), list(res[n:])


def _chip_sum(name, g, rx):
    nl, _, r, c = g.shape
    h = r // 2
    tr = _tile(h, max(V7X_SUBLANES * 2, (1 << 19) // (2 * c) // 16 * 16), 16)
    nb = h // tr

    def body(g_ref, r_ref, o_ref):
        acc = g_ref[...].astype(F32)
        for k in range(N_PARTS):
            acc = acc + r_ref[k].astype(F32)
        o_ref[...] = acc

    return pl.pallas_call(
        body, name=name, grid=(nl, nb),
        in_specs=[pl.BlockSpec((None, None, tr, c), lambda l, i: (l, _my_shard(), _my_core() * nb + i, 0)),
                  pl.BlockSpec((N_PARTS, None, tr, c), lambda l, i: (0, l, i, 0))],
        out_specs=pl.BlockSpec((None, tr, c), lambda l, i: (l, i, 0)),
        out_shape=jax.ShapeDtypeStruct((nl, h, c), F32),
        compiler_params=_cp(2),
    )(g, rx)


def _send_half(name, fs):
    n = len(fs)

    def body(*refs):
        ins, outs = refs[:n], refs[n:2 * n]
        send_sems, recv_sems = refs[2 * n:]
        x, y, c, _ = _place()
        cps = []
        for t in range(n):
            cps.append(pltpu.make_async_remote_copy(
                src_ref=ins[t], dst_ref=outs[t], send_sem=send_sems.at[t], recv_sem=recv_sems.at[t],
                device_id=(x, y, 1 - c), device_id_type=MESH))
            cps[-1].start()
        for cp in cps:
            cp.wait()

    return pl.pallas_call(
        body, name=name, in_specs=[ANY] * n, out_specs=[ANY] * n,
        out_shape=[jax.ShapeDtypeStruct(f.shape, f.dtype) for f in fs],
        scratch_shapes=[pltpu.SemaphoreType.DMA((n,)), pltpu.SemaphoreType.DMA((n,))],
        compiler_params=_comm_params(),
    )(*fs)


def _peers(x, y, c):
    rel = [(dx, dy, dc) for dx in (0, 1) for dy in (0, 1) for dc in (0, 1) if (dx, dy, dc) != (0, 0, 0)]
    return [(1 - x if dx else x, 1 - y if dy else y, 1 - c if dc else c) for dx, dy, dc in rel]


def _share_start(name, v, land):
    def body(v_ref, land_ref, ssem, rsem, v_out, land_out, token):
        x, y, c, _ = _place()
        me = 4 * x + 2 * y + c
        for k, peer in enumerate(_peers(x, y, c)):
            pltpu.make_async_remote_copy(src_ref=v_out, dst_ref=land_out.at[me], send_sem=ssem.at[k],
                                         recv_sem=rsem.at[k], device_id=peer, device_id_type=MESH).start()
        token[...] = jnp.zeros((V7X_SUBLANES, V7X_LANES), F32)

    res = pl.pallas_call(
        body, name=name, in_specs=[HBM_SPEC, HBM_SPEC],
        out_specs=[SEM_SPEC, SEM_SPEC, HBM_SPEC, HBM_SPEC, pl.BlockSpec(memory_space=pltpu.VMEM)],
        out_shape=[pltpu.SemaphoreType.DMA((N_DEV - 1,)), pltpu.SemaphoreType.DMA((N_DEV - 1,)),
                   pltpu.HBM(v.shape, v.dtype), pltpu.HBM(land.shape, land.dtype),
                   jax.ShapeDtypeStruct((V7X_SUBLANES, V7X_LANES), F32)],
        input_output_aliases={0: 2, 1: 3},
        compiler_params=_split_params(),
    )(pltpu.with_memory_space_constraint(v, pltpu.HBM), pltpu.with_memory_space_constraint(land, pltpu.HBM))
    return (res[0], res[1]), res[2], res[3], res[4][0, 0]


def _share_wait(name, v, land, ssem, rsem, after):
    def body(v_ref, land_ref, ss, rs, after_ref, v_out, land_out):
        x, y, c, _ = _place()
        for k, (px, py, pc) in enumerate(_peers(x, y, c)):
            cp = pltpu.make_async_remote_copy(src_ref=v_out, dst_ref=land_out.at[4 * px + 2 * py + pc],
                                              send_sem=ss.at[k], recv_sem=rs.at[k], device_id=(px, py, pc),
                                              device_id_type=MESH)
            cp.wait_send()
            cp.wait_recv()

    res = pl.pallas_call(
        body, name=name, in_specs=[HBM_SPEC, HBM_SPEC, SEM_SPEC, SEM_SPEC, ANY],
        out_specs=[HBM_SPEC, HBM_SPEC],
        out_shape=[pltpu.HBM(v.shape, v.dtype), pltpu.HBM(land.shape, land.dtype)],
        input_output_aliases={0: 0, 1: 1},
        compiler_params=_split_params(),
    )(v, land, ssem, rsem, after)
    return res[0], res[1]


def _sum_devices(name, v, land):
    r, c = v.shape
    tr = _tile(r, 512)

    def body(v_ref, land_ref, o_ref):
        x, y, cc, _ = _place()
        me = 4 * x + 2 * y + cc
        own = v_ref[...]
        acc = jnp.where(me == 0, own, land_ref[0])
        for k in range(1, N_DEV):
            acc = acc + jnp.where(me == k, own, land_ref[k])
        o_ref[...] = acc

    return pl.pallas_call(
        body, grid=(r // tr,), name=name,
        in_specs=[pl.BlockSpec((tr, c), lambda i: (i, 0)), pl.BlockSpec((N_DEV, tr, c), lambda i: (0, i, 0))],
        out_specs=pl.BlockSpec((tr, c), lambda i: (i, 0)),
        out_shape=jax.ShapeDtypeStruct((r, c), F32), compiler_params=_cp(1),
    )(v, land)


def _rows_view(wall):
    nl, s, r, c = wall.shape
    return wall.reshape(nl, s * r, c)


def _ffn_fwd(tag, alpha, x, w_in, w_out3, layer, g, b):
    h, a = _ffn_in(f"{tag}_in", x, w_in, layer)
    y, xhat, rstd = _mm_ln(f"{tag}_out", alpha, 0.5, a, w_out3, layer, x, g, b)
    return y, (x, h, a, xhat, rstd)


def _dx_ln(name, a, b, *, nk, a_blk, a_map, b_blk, b_map, alpha, dz, nxt):
    t, d = dz.shape
    tm = a_blk[-2]
    xhat, rstd, g = nxt
    through = xhat is not None

    def body(*refs):
        a_ref, b_ref, dz_ref = refs[:3]
        if through:
            xh_ref, rs_ref, g_ref, o_ref, dg_ref, db_ref, acc = refs[3:]
        else:
            g_ref, o_ref, acc = refs[3:]
        i, kk = pl.program_id(0), pl.program_id(1)
        p = lax.dot_general(a_ref[...].astype(BF16), b_ref[...].astype(BF16), _NT, preferred_element_type=F32)

        @pl.when(kk == 0)
        def _():
            acc[...] = p

        @pl.when(kk > 0)
        def _():
            acc[...] += p

        @pl.when(kk == nk - 1)
        def _():
            dy = alpha * dz_ref[...] + acc[...]
            if not through:
                o_ref[...] = dy + g_ref[...]
                return
            xh = xh_ref[...]
            dxh = dy * g_ref[...]
            m1 = jnp.mean(dxh, axis=-1, keepdims=True)
            m2 = jnp.mean(dxh * xh, axis=-1, keepdims=True)
            o_ref[...] = rs_ref[...] * (dxh - m1 - xh * m2)
            pg = jnp.sum(dy * xh, axis=0, keepdims=True)
            pb = jnp.sum(dy, axis=0, keepdims=True)

            @pl.when(i == 0)
            def _():
                dg_ref[...] = pg
                db_ref[...] = pb

            @pl.when(i > 0)
            def _():
                dg_ref[...] += pg
                db_ref[...] += pb

    row = lambda c: pl.BlockSpec((tm, c), lambda i, kk: (i, 0))
    vec = pl.BlockSpec((1, d), lambda i, kk: (0, 0))
    sd = jax.ShapeDtypeStruct
    in_specs = [pl.BlockSpec(a_blk, a_map), pl.BlockSpec(b_blk, b_map), row(d)]
    args = [a, b, dz]
    if through:
        in_specs += [row(d), row(1), vec]
        args += [xhat, rstd, g]
        out_specs, out_shape = [row(d), vec, vec], [sd((t, d), F32), sd((1, d), F32), sd((1, d), F32)]
    else:
        in_specs += [vec]
        args += [g]
        out_specs, out_shape = row(d), sd((t, d), F32)
    return pl.pallas_call(
        body, grid=(t // tm, nk), name=name, in_specs=in_specs, out_specs=out_specs, out_shape=out_shape,
        scratch_shapes=[pltpu.VMEM((tm, d), F32)], compiler_params=_cp(2),
    )(*args)


def _ffn_bwd(tag, alpha, dz, saved, w_in, w_out3, layer, g_win, g_wout3, grads_done, nxt, zrow):
    x, h, a, _, _ = saved
    t = x.shape[0]
    _, s, k, n = w_in.shape
    tm = _tile(t, 512)
    g_wout3 = _mm_tn(f"{tag}_dwout", a, dz, tm=n, scale=0.5, layer=layer, into=g_wout3)
    dh = _ffn_da(f"{tag}_da", dz, w_out3, layer, h, zrow)
    g_win = _mm(f"{tag}_dwin", x, dh, mode="tn", grid=(s, t // tm), kaxis=1,
                a_blk=(tm, k), a_map=lambda j, kk: (kk, 0),
                b_blk=(None, tm, n), b_map=lambda j, kk: (j // 2, kk, j % 2),
                o_shape=w_in.shape, o_blk=(None, None, k, n), o_map=lambda j, kk: (layer, j, 0, 0),
                o_dtype=g_win.dtype, into=g_win)
    zero = grads_done(g_win, g_wout3)
    return _dx_ln(f"{tag}_dx", dh, w_in, nk=s, a_blk=(None, tm, n), a_map=lambda i, kk: (kk // 2, i, kk % 2),
                  b_blk=(None, None, k, n), b_map=lambda i, kk: (layer, kk, 0, 0),
                  alpha=alpha, dz=dz, nxt=(nxt[0], nxt[1], nxt[2] + zero))


def _fox_fwd(tag, alpha, x, w_pad, bf, w_o3, layer, g, b):
    t, d = x.shape
    nh = bf.shape[1]
    tb = _tile(t, ATTN_BLOCK)
    nb = t // tb
    qkv = _mm_nn(f"{tag}_qkv", x, w_pad[:, :3 * d], o_dtype=BF16, tn=d).reshape(nb, tb, 3 * d)
    fl = _mm_nn(f"{tag}_gate", x, w_pad[:, 3 * d:])[:, :nh]
    cum = _fox_cumsum(f"{tag}_cum", fl, bf)
    ccol = cum.reshape(nb, tb, nh // 2, 2).transpose(2, 0, 1, 3)
    crow = cum.reshape(nb, tb, nh // 2, 2).transpose(2, 0, 3, 1)
    o, lse = _attn_fwd(f"{tag}_attn", qkv, ccol, crow, nh)
    o2 = o.reshape(t, d)
    y, xhat, rstd = _mm_ln(f"{tag}_oproj", alpha, 1.0, o2, w_o3, layer, x, g, b)
    return y, xhat, rstd, (x, qkv, ccol, crow, o, lse, fl)


def _fox_bwd(tag, alpha, dm, saved, w_pad, bf, w_o3, layer, g_wo3, grads_done, nxt):
    x, qkv, ccol, crow, o, lse, fl = saved
    t, d = x.shape
    nh = bf.shape[1]
    nb, tb, _ = qkv.shape
    tm = _tile(t, 512)
    g_wo3 = _mm_tn(f"{tag}_dwo", o.reshape(t, d), dm, layer=layer, into=g_wo3)
    do = _mm_nt(f"{tag}_do", dm, w_o3, layer=layer).reshape(nb, tb, d)
    dq, dk, dv, drow, dcol = _attn_bwd(f"{tag}_attn_bwd", qkv, ccol, crow, o, lse, do, nh)
    dcum = drow.transpose(1, 2, 0, 3).reshape(t, nh) - dcol.transpose(1, 3, 0, 2).reshape(t, nh)
    dfl, dbf = _fox_cumsum_bwd(f"{tag}_cum_bwd", dcum, fl, bf)
    pad = w_pad.shape[1] - 3 * d - nh
    dproj = jnp.concatenate([dq.reshape(t, d), dk.reshape(t, d), dv.reshape(t, d),
                             dfl.astype(BF16), jnp.zeros((t, pad), BF16)], axis=1)
    d_wpad = _mm_tn(f"{tag}_dwin", x, dproj, tn=_tile(w_pad.shape[1], 640, V7X_LANES))
    zero = grads_done(g_wo3, d_wpad)
    cols = w_pad.shape[1]
    out = _dx_ln(f"{tag}_dx", dproj, w_pad, nk=1, a_blk=(tm, cols), a_map=lambda i, kk: (i, 0),
                 b_blk=(d, cols), b_map=lambda i, kk: (0, 0), alpha=alpha, dz=dm, nxt=(nxt[0], nxt[1], nxt[2] + zero))
    return out, dbf


def _to_segments(a):
    t, d = a.shape
    return a.reshape(SCAN_SEGMENTS, t // SCAN_SEGMENTS, d).transpose(1, 0, 2).reshape(t, d)


def _from_segments(a):
    t, d = a.shape
    return a.reshape(t // SCAN_SEGMENTS, SCAN_SEGMENTS, d).transpose(1, 0, 2).reshape(t, d)


def _s5_discretise(a_re, a_im, log_dt, b_re, b_im):
    dt = jnp.exp(log_dt)[:, None]
    mag = jnp.exp(a_re * dt)
    ang = a_im * dt
    lb_re = mag * jnp.cos(ang)
    lb_im = mag * jnp.sin(ang)
    den = a_re * a_re + a_im * a_im
    nr = lb_re - 1.0
    ni = lb_im
    z_re = (nr * a_re + ni * a_im) / den
    z_im = (ni * a_re - nr * a_im) / den
    bb_re = z_re[..., None] * b_re - z_im[..., None] * b_im
    bb_im = z_re[..., None] * b_im + z_im[..., None] * b_re
    return lb_re, lb_im, bb_re, bb_im


S5_BLOCK_GROUPS = 8


def _blockdiag_in(bb):
    g, p, h = bb.shape
    e = jnp.eye(S5_BLOCK_GROUPS, dtype=bb.dtype)
    b4 = bb.reshape(g // S5_BLOCK_GROUPS, S5_BLOCK_GROUPS, p, h)
    return jnp.einsum("jgph,gf->jghfp", b4, e).reshape(g // S5_BLOCK_GROUPS, S5_BLOCK_GROUPS * h, S5_BLOCK_GROUPS * p)


def _blockdiag_in_grad(d):
    nj, gh, gp = d.shape
    h, p = gh // S5_BLOCK_GROUPS, gp // S5_BLOCK_GROUPS
    e = jnp.eye(S5_BLOCK_GROUPS, dtype=d.dtype)
    d6 = d.reshape(nj, S5_BLOCK_GROUPS, h, S5_BLOCK_GROUPS, p)
    return jnp.einsum("jghfp,gf->jgph", d6, e).reshape(nj * S5_BLOCK_GROUPS, p, h)


def _blockdiag_out(cc):
    g, h, p = cc.shape
    e = jnp.eye(S5_BLOCK_GROUPS, dtype=cc.dtype)
    c4 = cc.reshape(g // S5_BLOCK_GROUPS, S5_BLOCK_GROUPS, h, p)
    return jnp.einsum("jghp,gf->jfpgh", c4, e).reshape(g // S5_BLOCK_GROUPS, S5_BLOCK_GROUPS * p, S5_BLOCK_GROUPS * h)


def _blockdiag_out_grad(d):
    nj, gp, gh = d.shape
    h, p = gh // S5_BLOCK_GROUPS, gp // S5_BLOCK_GROUPS
    e = jnp.eye(S5_BLOCK_GROUPS, dtype=d.dtype)
    d6 = d.reshape(nj, S5_BLOCK_GROUPS, p, S5_BLOCK_GROUPS, h)
    return jnp.einsum("jfpgh,gf->jghp", d6, e).reshape(nj * S5_BLOCK_GROUPS, h, p)


def _s5_fwd(tag, x, prm, w_out, layer):
    a_re, a_im, log_dt, b_re, b_im, c_re, c_im, d_skip = prm
    t, d = x.shape
    g, p = a_re.shape
    w = g * p
    nj = g // S5_BLOCK_GROUPS
    cw, sw = S5_BLOCK_GROUPS * S5_GROUP, S5_BLOCK_GROUPS * p
    seg = t // SCAN_SEGMENTS
    tm = _tile(t, 4096)
    lb_re, lb_im, bb_re, bb_im = _s5_discretise(a_re, a_im, log_dt, b_re, b_im)
    lam = jnp.stack([lb_re.reshape(1, w), lb_im.reshape(1, w)])
    bs = jnp.stack([_blockdiag_in(bb_re), _blockdiag_in(bb_im)]).astype(BF16)
    cs = jnp.stack([_blockdiag_out(c_re), -_blockdiag_out(c_im)]).astype(BF16)
    dvec = d_skip.reshape(1, d)
    u = _to_segments(x)
    bu = _mm(f"{tag}_bu", u, bs, mode="nn", grid=(2, nj, t // tm), kaxis=None,
             a_blk=(tm, cw), a_map=lambda r, j, i: (i, j),
             b_blk=(None, None, cw, sw), b_map=lambda r, j, i: (r, j, 0, 0),
             o_shape=(2, t, w), o_blk=(None, tm, sw), o_map=lambda r, j, i: (r, i, j))
    hs = _s5_scan(f"{tag}_scan", lam, bu.reshape(2, seg, SCAN_SEGMENTS, w)).reshape(2, t, w)
    ych = _mm(f"{tag}_ch", hs, cs, mode="nn", grid=(nj, t // tm, 2), kaxis=2,
              a_blk=(None, tm, sw), a_map=lambda j, i, r: (r, i, j),
              b_blk=(None, None, sw, cw), b_map=lambda j, i, r: (r, j, 0, 0),
              o_shape=(t, d), o_blk=(tm, cw), o_map=lambda j, i, r: (i, j))
    ypre, act = _s5_act_fwd(f"{tag}_act", ych, u, dvec)
    vg = _mm_shards_nn(f"{tag}_wout", act, w_out, layer, F32)
    m = _from_segments(_glu_fwd(f"{tag}_glu", vg))
    return m, (u, lam, bs, cs, dvec, hs, ypre, act, vg)


def _s5_bwd(tag, dm, saved, prm, w_out, layer, g_wout):
    a_re, a_im, log_dt, b_re, b_im, c_re, c_im, d_skip = prm
    u, lam, bs, cs, dvec, hs, ypre, act, vg = saved
    t, d = u.shape
    g, p = a_re.shape
    w = g * p
    nj = g // S5_BLOCK_GROUPS
    cw, sw = S5_BLOCK_GROUPS * S5_GROUP, S5_BLOCK_GROUPS * p
    seg = t // SCAN_SEGMENTS
    tm = _tile(t, 4096)
    dvg = _glu_bwd(f"{tag}_glu_bwd", _to_segments(dm), vg)
    g_wout = _mm_shards_tn(f"{tag}_dwout", act, dvg, layer, g_wout)
    dact = _mm_shards_nt(f"{tag}_dact", dvg, w_out, layer)
    dypre, duskip, dd = _s5_act_bwd(f"{tag}_act_bwd", dact, ypre, u, dvec)
    dh = _mm(f"{tag}_dh", dypre, cs, mode="nt", grid=(2, nj, t // tm), kaxis=None,
             a_blk=(tm, cw), a_map=lambda r, j, i: (i, j),
             b_blk=(None, None, sw, cw), b_map=lambda r, j, i: (r, j, 0, 0),
             o_shape=(2, t, w), o_blk=(None, tm, sw), o_map=lambda r, j, i: (r, i, j))
    dcs = _mm(f"{tag}_dc", hs, dypre, mode="tn", grid=(2, nj, t // tm), kaxis=2,
              a_blk=(None, tm, sw), a_map=lambda r, j, i: (r, i, j),
              b_blk=(tm, cw), b_map=lambda r, j, i: (i, j),
              o_shape=(2, nj, sw, cw), o_blk=(None, None, sw, cw), o_map=lambda r, j, i: (r, j, 0, 0))
    gs, dlam8 = _s5_scan(f"{tag}_scan_bwd", lam, dh.reshape(2, seg, SCAN_SEGMENTS, w),
                         hs.reshape(2, seg, SCAN_SEGMENTS, w))
    gs = gs.reshape(2, t, w)
    du = _mm(f"{tag}_du", gs, bs, mode="nt", grid=(nj, t // tm, 2), kaxis=2,
             a_blk=(None, tm, sw), a_map=lambda j, i, r: (r, i, j),
             b_blk=(None, None, cw, sw), b_map=lambda j, i, r: (r, j, 0, 0),
             o_shape=(t, d), o_blk=(tm, cw), o_map=lambda j, i, r: (i, j))
    dbs = _mm(f"{tag}_db", u, gs, mode="tn", grid=(2, nj, t // tm), kaxis=2,
              a_blk=(tm, cw), a_map=lambda r, j, i: (i, j),
              b_blk=(None, tm, sw), b_map=lambda r, j, i: (r, i, j),
              o_shape=(2, nj, cw, sw), o_blk=(None, None, cw, sw), o_map=lambda r, j, i: (r, j, 0, 0))
    dx = _from_segments(du + duskip)
    dlam = jnp.sum(dlam8, axis=1).reshape(2, g, p)
    small = dict(dlb_re=dlam[0], dlb_im=dlam[1],
                 dbb_re=_blockdiag_in_grad(dbs[0]), dbb_im=_blockdiag_in_grad(dbs[1]),
                 dc_re=_blockdiag_out_grad(dcs[0]), dc_im=-_blockdiag_out_grad(dcs[1]),
                 dd=dd.reshape(g, S5_GROUP))
    return dx, g_wout, small


def _pack(pieces):
    rows = []
    for p in pieces:
        flat = p.reshape(-1).astype(F32)
        n = flat.shape[0]
        rows.append(jnp.pad(flat, (0, -n % V7X_LANES)).reshape(-1, V7X_LANES))
    buf = jnp.concatenate(rows, axis=0)
    return jnp.pad(buf, ((0, -buf.shape[0] % V7X_SUBLANES), (0, 0)))


def _unpack(buf, shapes):
    out, row = [], 0
    for s in shapes:
        n = math.prod(s)
        nr = -(-n // V7X_LANES)
        out.append(buf[row:row + nr].reshape(-1)[:n].reshape(s))
        row += nr
    return out


def kernel(x, ffn1_w_in, ffn1_w_out, ln1_g, ln1_b, lnm_g, lnm_b, ffn2_w_in, ffn2_w_out, ln2_g, ln2_b, fox_w_in, fox_b_f, fox_w_o, s5_a_re, s5_a_im, s5_log_dt, s5_b_re, s5_b_im, s5_c_re, s5_c_im, s5_d, s5_w_out, loss_target, m_ffn1_w_in, m_ffn1_w_out, m_ln1_g, m_ln1_b, m_lnm_g, m_lnm_b, m_ffn2_w_in, m_ffn2_w_out, m_ln2_g, m_ln2_b, m_fox_w_in, m_fox_b_f, m_fox_w_o, m_s5_a_re, m_s5_a_im, m_s5_log_dt, m_s5_b_re, m_s5_b_im, m_s5_c_re, m_s5_c_im, m_s5_d, m_s5_w_out, v_ffn1_w_in, v_ffn1_w_out, v_ln1_g, v_ln1_b, v_lnm_g, v_lnm_b, v_ffn2_w_in, v_ffn2_w_out, v_ln2_g, v_ln2_b, v_fox_w_in, v_fox_b_f, v_fox_w_o, v_s5_a_re, v_s5_a_im, v_s5_log_dt, v_s5_b_re, v_s5_b_im, v_s5_c_re, v_s5_c_im, v_s5_d, v_s5_w_out):
    big_names = ["ffn1_w_in", "ffn1_w_out", "ffn2_w_in", "ffn2_w_out", "fox_w_in", "fox_w_o", "s5_w_out"]
    small_names = ["ln1_g", "ln1_b", "lnm_g", "lnm_b", "ln2_g", "ln2_b", "fox_b_f", "s5_a_re", "s5_a_im", "s5_log_dt",
                   "s5_b_re", "s5_b_im", "s5_c_re", "s5_c_im", "s5_d"]
    out_order = ["ffn1_w_in", "ffn1_w_out", "ln1_g", "ln1_b", "lnm_g", "lnm_b", "ffn2_w_in", "ffn2_w_out", "ln2_g",
                 "ln2_b", "fox_w_in", "fox_b_f", "fox_w_o", "s5_a_re", "s5_a_im", "s5_log_dt", "s5_b_re", "s5_b_im",
                 "s5_c_re", "s5_c_im", "s5_d", "s5_w_out"]
    env = dict(locals())
    w = {n: env[n] for n in out_order}
    mom = {n: env["m_" + n] for n in out_order}
    vel = {n: env["v_" + n] for n in out_order}

    depth, d = ln1_g.shape
    t = x.shape[1]
    alpha = (2.0 * depth) ** 0.25
    x0 = x.reshape(t, d)
    tgt = loss_target.reshape(t, d)

    tix = {n: k for k, n in enumerate(big_names)}
    groups = []
    for i in range(depth):
        j = i // 2
        groups.append([(tix["ffn1_w_in"], i), (tix["ffn1_w_out"], i)])
        mixer = [(tix["fox_w_in"], j), (tix["fox_w_o"], j)] if i % 2 == 0 else [(tix["s5_w_out"], j)]
        groups.append(mixer + [(tix["ffn2_w_in"], i), (tix["ffn2_w_out"], i)])
    sems, bufs = _gather_start("gather_start", [_cast_place(f"cast_{n}", w[n]) for n in big_names], groups)
    full, rows3 = {}, {}

    def arrive(gi, after):
        nonlocal bufs
        bufs = _gather_wait(f"gather_wait_{gi}", bufs, sems[gi][0], sems[gi][1], after, groups[gi])
        full.update(zip(big_names, bufs))
        rows3.update({n: _rows_view(full[n]) for n in ("ffn1_w_out", "ffn2_w_out", "fox_w_o")})

    nh = fox_b_f.shape[1]
    fox_cols = 3 * d + nh
    fox_pad = -(-fox_cols // (5 * V7X_LANES)) * (5 * V7X_LANES)

    def fox_wpad(j):
        wf = full["fox_w_in"][j].transpose(1, 0, 2).reshape(d, fox_cols)
        return jnp.pad(wf, ((0, 0), (0, fox_pad - fox_cols)))

    def s5_params(j):
        return (s5_a_re[j], s5_a_im[j], s5_log_dt[j], s5_b_re[j], s5_b_im[j], s5_c_re[j], s5_c_im[j], s5_d[j])

    saved = []
    h = x0
    for i in range(depth):
        j = i // 2
        arrive(2 * i, h)
        h, s1 = _ffn_fwd(f"l{i}_ffn1", alpha, h, full["ffn1_w_in"], rows3["ffn1_w_out"], i,
                         ln1_g[i:i + 1], ln1_b[i:i + 1])
        arrive(2 * i + 1, h)
        if i % 2 == 0:
            h, xhat_m, rstd_m, sm = _fox_fwd(f"l{i}_fox", alpha, h, fox_wpad(j), fox_b_f[j:j + 1], rows3["fox_w_o"], j,
                                             lnm_g[i:i + 1], lnm_b[i:i + 1])
        else:
            m, sm = _s5_fwd(f"l{i}_s5", h, s5_params(j), full["s5_w_out"], j)
            h, xhat_m, rstd_m = _ln_fwd(f"l{i}_lnm", alpha, h, m, 1.0, lnm_g[i:i + 1], lnm_b[i:i + 1])
        h, s2 = _ffn_fwd(f"l{i}_ffn2", alpha, h, full["ffn2_w_in"], rows3["ffn2_w_out"], i,
                         ln2_g[i:i + 1], ln2_b[i:i + 1])
        saved.append((s1, sm, (xhat_m, rstd_m), s2))
    loss_part = _loss_sum("loss", h, tgt) * (0.5 / d)

    fox_in_names = [f"fox_w_in_l{j}" for j in range(fox_w_in.shape[0])]
    gshape = {n: full[n].shape for n in big_names if n != "fox_w_in"}
    gshape.update({n: (1,) + full["fox_w_in"].shape[1:] for n in fox_in_names})
    gbuf = {n: lax.empty(s, BF16) for n, s in gshape.items()}
    rxbuf = {n: lax.empty((N_PARTS, s[0], s[2] // 2, s[3]), BF16) for n, s in gshape.items()}
    pending = []

    zero = jnp.zeros((), F32)

    def scatter(tag, pairs):
        nonlocal zero
        names = list(dict.fromkeys(n for n, _ in pairs))
        items = [(names.index(n), layer) for n, layer in pairs]
        sem, send, rx, zero = _scatter_start(f"scatter_start_{tag}", [gbuf[n] for n in names],
                                             [rxbuf[n] for n in names], items)
        gbuf.update(zip(names, send))
        rxbuf.update(zip(names, rx))
        pending.append((tag, names, items, sem))

    gsmall = {n: [None] * w[n].shape[0] for n in small_names}
    s5_cot = [None] * s5_a_re.shape[0]
    cot_names = ["dlb_re", "dlb_im", "dbb_re", "dbb_im", "dc_re", "dc_im", "dd"]
    ln_names = ["ln1_g", "ln1_b", "lnm_g", "lnm_b", "ln2_g", "ln2_b"]

    def zrow():
        return jnp.zeros((1, d), F32) + zero

    def ffn_done(which, i):
        def done(g_win, g_wout3):
            gbuf[f"{which}_w_in"], gbuf[f"{which}_w_out"] = g_win, g_wout3.reshape(gshape[f"{which}_w_out"])
            scatter(f"l{i}_{which}", [(f"{which}_w_in", i), (f"{which}_w_out", i)])
            return zero
        return done

    _, _, _, (_, _, _, xhat_top, rstd_top) = saved[depth - 1]
    dz, dg, db = _ln_bwd("top_ln_bwd", [(h, 1.0 / d), (tgt, -1.0 / d)], xhat_top, rstd_top, ln2_g[depth - 1:depth])
    gsmall["ln2_g"][depth - 1], gsmall["ln2_b"][depth - 1] = dg, db
    grad_x = None
    for i in reversed(range(depth)):
        j = i // 2
        s1, sm, (xhat_m, rstd_m), s2 = saved[i]
        dz, dg, db = _ffn_bwd(f"l{i}_ffn2", alpha, dz, s2, full["ffn2_w_in"], rows3["ffn2_w_out"], i,
                              gbuf["ffn2_w_in"], _rows_view(gbuf["ffn2_w_out"]), ffn_done("ffn2", i),
                              (xhat_m, rstd_m, lnm_g[i:i + 1]), zrow())
        gsmall["lnm_g"][i], gsmall["lnm_b"][i] = dg, db
        ln1 = (s1[3], s1[4], ln1_g[i:i + 1])
        if i % 2 == 0:
            def fox_done(g_wo3, d_wpad, j=j, i=i):
                gbuf["fox_w_o"] = g_wo3.reshape(gshape["fox_w_o"])
                gbuf[fox_in_names[j]] = d_wpad[:, :fox_cols].reshape(d, N_CHIPS, -1).transpose(1, 0, 2)[None].astype(BF16)
                scatter(f"l{i}_fox", [("fox_w_o", j), (fox_in_names[j], 0)])
                return zero

            (dz, dg, db), gsmall["fox_b_f"][j] = _fox_bwd(
                f"l{i}_fox", alpha, dz, sm, fox_wpad(j), fox_b_f[j:j + 1], rows3["fox_w_o"], j,
                _rows_view(gbuf["fox_w_o"]), fox_done, ln1)
        else:
            dx, gbuf["s5_w_out"], s5_cot[j] = _s5_bwd(f"l{i}_s5", dz, sm, s5_params(j), full["s5_w_out"], j,
                                                      gbuf["s5_w_out"])
            scatter(f"l{i}_s5", [("s5_w_out", j)])
            dz, dg, db = _ln_bwd(f"l{i}_ln1_bwd", [(dz, alpha), (dx, 1.0)], ln1[0], ln1[1], ln1[2] + zero)
        gsmall["ln1_g"][i], gsmall["ln1_b"][i] = dg, db
        if i > 0:
            below = saved[i - 1][3]
            dz, dg, db = _ffn_bwd(f"l{i}_ffn1", alpha, dz, s1, full["ffn1_w_in"], rows3["ffn1_w_out"], i,
                                  gbuf["ffn1_w_in"], _rows_view(gbuf["ffn1_w_out"]), ffn_done("ffn1", i),
                                  (below[3], below[4], ln2_g[i - 1:i]), zrow())
            gsmall["ln2_g"][i - 1], gsmall["ln2_b"][i - 1] = dg, db
        else:
            pieces = [loss_part + zero] + [jnp.concatenate(gsmall[n], axis=0) for n in ln_names + ["fox_b_f"]]
            pieces += [jnp.stack([s5_cot[k][n] for k in range(len(s5_cot))]) for n in cot_names]
            mine = _pack(pieces)
            share_sem, mine, land, zero = _share_start("small_share_start", mine, lax.empty((N_DEV,) + mine.shape, F32))
            grad_x = _ffn_bwd(f"l{i}_ffn1", alpha, dz, s1, full["ffn1_w_in"], rows3["ffn1_w_out"], i,
                              gbuf["ffn1_w_in"], _rows_view(gbuf["ffn1_w_out"]), ffn_done("ffn1", i),
                              (None, None, jnp.zeros((1, d), F32)), zrow()).reshape(x.shape)

    shapes = [p.shape for p in pieces]
    mine, land = _share_wait("small_share_wait", mine, land, share_sem[0], share_sem[1], grad_x)
    summed = _unpack(_sum_devices("small_sum", mine, land), shapes)
    loss = summed[0].reshape(())
    gs_final = dict(zip(ln_names + ["fox_b_f"], summed[1:8]))
    cot = dict(zip(cot_names, summed[8:]))
    prm_names = ["s5_a_re", "s5_a_im", "s5_log_dt", "s5_b_re", "s5_b_im"]
    _, disc_vjp = jax.vjp(jax.vmap(_s5_discretise), *[w[n] for n in prm_names])
    for n, gval in zip(prm_names, disc_vjp((cot["dlb_re"], cot["dlb_im"], cot["dbb_re"], cot["dbb_im"]))):
        gs_final[n] = gval
    gs_final["s5_c_re"], gs_final["s5_c_im"], gs_final["s5_d"] = cot["dc_re"], cot["dc_im"], cot["dd"]

    grads, deltas, new_m, new_v = {}, {}, {}, {}
    small_shapes = [w[n].shape for n in small_names]
    for n in small_names:
        grads[n] = gs_final[n].reshape(w[n].shape)
    packed = [_pack([src[n] for n in small_names]) for src in (w, grads, mom, vel)]
    small_out = _adamw("adamw_small", *packed)
    for dst, buf in zip((deltas, new_m, new_v), small_out):
        for n, val in zip(small_names, _unpack(buf, small_shapes)):
            dst[n] = val

    for tag, names, items, sem in pending:
        send, rx = _scatter_wait(f"scatter_wait_{tag}", [gbuf[n] for n in names], [rxbuf[n] for n in names],
                                 sem[0], sem[1], small_out[0], items)
        gbuf.update(zip(names, send))
        rxbuf.update(zip(names, rx))
    half = {n: _chip_sum(f"grad_chip_sum_{n}", gbuf[n], rxbuf[n]) for n in gshape}
    half["fox_w_in"] = jnp.concatenate([half[n] for n in fox_in_names], axis=0)
    halves = [half[n] for n in big_names]
    theirs = _send_half("grad_send_half", halves)
    for n, mine_h, their_h in zip(big_names, halves, theirs):
        grads[n], deltas[n], new_m[n], new_v[n] = _adamw_join(f"adamw_{n}", w[n], mine_h, their_h, mom[n], vel[n])
    return (loss, grad_x, *[grads[n] for n in out_order], *[deltas[n] for n in out_order],
            *[new_m[n] for n in out_order], *[new_v[n] for n in out_order])
```

```python
import functools
import math

import jax
import jax.numpy as jnp
from jax import lax
from jax.experimental import pallas as pl
from jax.experimental.pallas import tpu as pltpu

F32 = jnp.float32
BF16 = jnp.bfloat16
LN_EPS = 1e-5
NEG_INF = -1e30
ADAM_LR = 0.001
ADAM_B1 = 0.9
ADAM_B2 = 0.999
ADAM_EPS = 1e-08
ADAM_WD = 0.01
ADAM_STEP = 10
S5_GROUP = 16
SCAN_SEGMENTS = 32
ATTN_BLOCK = 1024
V7X_SUBLANES = 8
V7X_LANES = 128
VMEM_LIMIT = 56 * 1024 * 1024
N_CHIPS = 4
N_DEV = 8
MESH = pl.DeviceIdType.MESH
ANY = pl.BlockSpec(memory_space=pl.ANY)


def _cp(n_grid, kaxis=None):
    sem = tuple("arbitrary" if (kaxis is None or i == kaxis) else "parallel" for i in range(n_grid))
    return pltpu.CompilerParams(dimension_semantics=sem, vmem_limit_bytes=VMEM_LIMIT)


def _tile(n, pref, mult=V7X_SUBLANES):
    if n <= pref:
        return n
    for t in range(pref, 0, -1):
        if n % t == 0 and t % mult == 0:
            return t
    return n


_CONTRACT = {"nn": ((1,), (0,)), "nt": ((1,), (1,)), "tn": ((0,), (0,))}


def _mm(name, a, b, *, mode, grid, kaxis, a_blk, a_map, b_blk, b_map, o_shape, o_blk, o_map, o_dtype=F32, scale=None,
        into=None):
    nk = 1 if kaxis is None else grid[kaxis]
    assert kaxis is None or kaxis == len(grid) - 1
    dims = (_CONTRACT[mode], ((), ()))
    use_acc = nk > 1 and o_dtype != F32
    acc_shape = tuple(d for d in o_blk if d is not None)

    def body(a_ref, b_ref, *rest):
        o_ref, scratch = (rest[1], rest[2:]) if into is not None else (rest[0], rest[1:])
        p = lax.dot_general(a_ref[...].astype(BF16), b_ref[...].astype(BF16), dims, preferred_element_type=F32)
        if nk == 1:
            if scale is not None:
                p = p * scale
            o_ref[...] = p.astype(o_dtype)
            return
        acc = scratch[0] if use_acc else o_ref
        k = pl.program_id(kaxis)

        @pl.when(k == 0)
        def _():
            acc[...] = p

        @pl.when(k > 0)
        def _():
            acc[...] += p

        if use_acc or scale is not None:
            @pl.when(k == nk - 1)
            def _():
                r = acc[...]
                if scale is not None:
                    r = r * scale
                o_ref[...] = r.astype(o_dtype)

    in_specs = [pl.BlockSpec(a_blk, a_map), pl.BlockSpec(b_blk, b_map)]
    args = [a, b]
    if into is not None:
        assert into.shape == tuple(o_shape) and into.dtype == o_dtype
        in_specs.append(ANY)
        args.append(into)
    return pl.pallas_call(
        body, grid=grid, name=name, in_specs=in_specs,
        out_specs=pl.BlockSpec(o_blk, o_map),
        out_shape=jax.ShapeDtypeStruct(o_shape, o_dtype),
        input_output_aliases={2: 0} if into is not None else {},
        scratch_shapes=[pltpu.VMEM(acc_shape, F32)] if use_acc else [],
        compiler_params=_cp(len(grid), kaxis),
    )(*args)


def _mm_shards_nn(name, a, wall, layer, o_dtype):
    t, k = a.shape
    _, s, _, n = wall.shape
    tm = _tile(t, 512)
    return _mm(name, a, wall, mode="nn", grid=(s, t // tm), kaxis=None,
               a_blk=(tm, k), a_map=lambda j, i: (i, 0),
               b_blk=(None, None, k, n), b_map=lambda j, i: (layer, j, 0, 0),
               o_shape=(t, s * n), o_blk=(tm, n), o_map=lambda j, i: (i, j), o_dtype=o_dtype)


def _mm_shards_nt(name, g, wall, layer):
    t = g.shape[0]
    _, s, k, n = wall.shape
    tm = _tile(t, 512)
    return _mm(name, g, wall, mode="nt", grid=(t // tm, s), kaxis=1,
               a_blk=(tm, n), a_map=lambda i, kk: (i, kk),
               b_blk=(None, None, k, n), b_map=lambda i, kk: (layer, kk, 0, 0),
               o_shape=(t, k), o_blk=(tm, k), o_map=lambda i, kk: (i, 0))


def _mm_shards_tn(name, a, g, layer, into):
    t, k = a.shape
    _, s, _, n = into.shape
    tk = _tile(t, 512)
    return _mm(name, a, g, mode="tn", grid=(s, t // tk), kaxis=1,
               a_blk=(tk, k), a_map=lambda j, kk: (kk, 0),
               b_blk=(tk, n), b_map=lambda j, kk: (kk, j),
               o_shape=into.shape, o_blk=(None, None, k, n), o_map=lambda j, kk: (layer, j, 0, 0),
               o_dtype=into.dtype, into=into)


def _mm_nn(name, a, w, o_dtype=F32, tn=None):
    t, k = a.shape
    n = w.shape[1]
    tm = _tile(t, 512)
    tn = n if tn is None else tn
    return _mm(name, a, w, mode="nn", grid=(n // tn, t // tm), kaxis=None,
               a_blk=(tm, k), a_map=lambda j, i: (i, 0),
               b_blk=(k, tn), b_map=lambda j, i: (0, j),
               o_shape=(t, n), o_blk=(tm, tn), o_map=lambda j, i: (i, j), o_dtype=o_dtype)


def _mm_nt(name, g, w, layer=None, o_dtype=F32):
    t, k = g.shape
    n = w.shape[-2]
    tm = _tile(t, 512)
    b_blk, b_map = ((n, k), lambda i: (0, 0)) if layer is None else ((None, n, k), lambda i: (layer, 0, 0))
    return _mm(name, g, w, mode="nt", grid=(t // tm,), kaxis=None,
               a_blk=(tm, k), a_map=lambda i: (i, 0), b_blk=b_blk, b_map=b_map,
               o_shape=(t, n), o_blk=(tm, n), o_map=lambda i: (i, 0), o_dtype=o_dtype)


def _mm_tn(name, a, g, tm=None, tn=None, scale=None, layer=None, into=None):
    t, m = a.shape
    n = g.shape[1]
    tk = _tile(t, 512)
    tm = m if tm is None else tm
    tn = n if tn is None else tn
    if layer is None:
        o_shape, o_blk, o_map, o_dtype = (m, n), (tm, tn), lambda i, j, kk: (i, j), F32
    else:
        o_shape, o_blk, o_map, o_dtype = into.shape, (None, tm, tn), lambda i, j, kk: (layer, i, j), into.dtype
    return _mm(name, a, g, mode="tn", grid=(m // tm, n // tn, t // tk), kaxis=2,
               a_blk=(tk, tm), a_map=lambda i, j, kk: (kk, i),
               b_blk=(tk, tn), b_map=lambda i, j, kk: (kk, j),
               o_shape=o_shape, o_blk=o_blk, o_map=o_map, o_dtype=o_dtype, scale=scale, into=into)


def _sigmoid(x):
    return 1.0 / (1.0 + jnp.exp(-x))


def _rows_call(name, body, t, tm, ins, in_cols, outs, acc_outs=()):
    in_specs = []
    for x, c in zip(ins, in_cols):
        if x.shape[0] == 1:
            in_specs.append(pl.BlockSpec((1, c), lambda i: (0, 0)))
        else:
            in_specs.append(pl.BlockSpec((tm, c), lambda i: (i, 0)))
    out_specs = [pl.BlockSpec((tm, s.shape[1]), lambda i: (i, 0)) for s in outs]
    out_specs += [pl.BlockSpec((1, s.shape[1]), lambda i: (0, 0)) for s in acc_outs]
    return pl.pallas_call(
        body, grid=(t // tm,), name=name, in_specs=in_specs, out_specs=out_specs,
        out_shape=list(outs) + list(acc_outs), compiler_params=_cp(1),
    )(*ins)


def _ln_fwd(name, alpha, x, r, coef, g, b):
    t, d = x.shape
    tm = _tile(t, 256)

    def body(x_ref, r_ref, g_ref, b_ref, y_ref, xh_ref, rs_ref):
        z = alpha * x_ref[...] + coef * r_ref[...]
        mu = jnp.mean(z, axis=-1, keepdims=True)
        zc = z - mu
        var = jnp.mean(zc * zc, axis=-1, keepdims=True)
        rstd = lax.rsqrt(var + LN_EPS)
        xh = zc * rstd
        y_ref[...] = xh * g_ref[...] + b_ref[...]
        xh_ref[...] = xh
        rs_ref[...] = rstd

    sd = jax.ShapeDtypeStruct
    return _rows_call(name, body, t, tm, [x, r, g, b], [d, d, d, d],
                      [sd((t, d), F32), sd((t, d), F32), sd((t, 1), F32)])


def _ln_bwd(name, terms, xhat, rstd, g):
    t, d = xhat.shape
    tm = _tile(t, 256)
    n = len(terms)
    coefs = [c for _, c in terms]

    def body(*refs):
        t_refs = refs[:n]
        xh_ref, rs_ref, g_ref, dz_ref, dg_ref, db_ref = refs[n:]
        dy = coefs[0] * t_refs[0][...]
        for c, r in zip(coefs[1:], t_refs[1:]):
            dy = dy + c * r[...]
        xh = xh_ref[...]
        dxh = dy * g_ref[...]
        m1 = jnp.mean(dxh, axis=-1, keepdims=True)
        m2 = jnp.mean(dxh * xh, axis=-1, keepdims=True)
        dz_ref[...] = rs_ref[...] * (dxh - m1 - xh * m2)
        pg = jnp.sum(dy * xh, axis=0, keepdims=True)
        pb = jnp.sum(dy, axis=0, keepdims=True)
        i = pl.program_id(0)

        @pl.when(i == 0)
        def _():
            dg_ref[...] = pg
            db_ref[...] = pb

        @pl.when(i > 0)
        def _():
            dg_ref[...] += pg
            db_ref[...] += pb

    sd = jax.ShapeDtypeStruct
    arrs = [a for a, _ in terms] + [xhat, rstd, g]
    cols = [d] * n + [d, 1, d]
    return _rows_call(name, body, t, tm, arrs, cols, [sd((t, d), F32)], [sd((1, d), F32), sd((1, d), F32)])


def _loss_sum(name, y, tgt):
    t, d = y.shape
    tm = _tile(t, 256)

    def body(y_ref, t_ref, o_ref):
        e = y_ref[...] - t_ref[...]
        s = jnp.sum(jnp.sum(e * e, axis=1, keepdims=True), axis=0, keepdims=True)
        i = pl.program_id(0)

        @pl.when(i == 0)
        def _():
            o_ref[...] = s

        @pl.when(i > 0)
        def _():
            o_ref[...] += s

    return _rows_call(name, body, t, tm, [y, tgt], [d, d], [], [jax.ShapeDtypeStruct((1, 1), F32)])[0]


def _ffn_in(name, x, wall, layer):
    t, k = x.shape
    n = wall.shape[3]
    tm = _tile(t, 512)

    def body(x_ref, wg_ref, wu_ref, h_ref, a_ref):
        xb = x_ref[...].astype(BF16)
        g = lax.dot_general(xb, wg_ref[...], _NN, preferred_element_type=F32)
        u = lax.dot_general(xb, wu_ref[...], _NN, preferred_element_type=F32)
        h_ref[0] = g.astype(BF16)
        h_ref[1] = u.astype(BF16)
        a_ref[...] = (g * _sigmoid(g) * u).astype(BF16)

    return pl.pallas_call(
        body, grid=(2, t // tm), name=name,
        in_specs=[pl.BlockSpec((tm, k), lambda j, i: (i, 0)),
                  pl.BlockSpec((None, None, k, n), lambda j, i: (layer, j, 0, 0)),
                  pl.BlockSpec((None, None, k, n), lambda j, i: (layer, 2 + j, 0, 0))],
        out_specs=[pl.BlockSpec((2, tm, n), lambda j, i: (0, i, j)), pl.BlockSpec((tm, n), lambda j, i: (i, j))],
        out_shape=[jax.ShapeDtypeStruct((2, t, 2 * n), BF16), jax.ShapeDtypeStruct((t, 2 * n), BF16)],
        compiler_params=_cp(2),
    )(x, wall, wall)


def _ffn_da(name, dz, w3, layer, h, zrow):
    t, k = dz.shape
    f = w3.shape[1]
    n = f // 2
    tm = _tile(t, 512)

    def body(dz_ref, z_ref, w_ref, g_ref, u_ref, dh_ref):
        d = 0.5 * lax.dot_general((dz_ref[...] + z_ref[...]).astype(BF16), w_ref[...], _NT, preferred_element_type=F32)
        g = g_ref[...].astype(F32)
        u = u_ref[...].astype(F32)
        sg = _sigmoid(g)
        dh_ref[0] = (d * u * sg * (1.0 + g * (1.0 - sg))).astype(BF16)
        dh_ref[1] = (d * g * sg).astype(BF16)

    return pl.pallas_call(
        body, grid=(2, t // tm), name=name,
        in_specs=[pl.BlockSpec((tm, k), lambda j, i: (i, 0)),
                  pl.BlockSpec((1, k), lambda j, i: (0, 0)),
                  pl.BlockSpec((None, n, k), lambda j, i: (layer, j, 0)),
                  pl.BlockSpec((None, tm, n), lambda j, i: (0, i, j)),
                  pl.BlockSpec((None, tm, n), lambda j, i: (1, i, j))],
        out_specs=pl.BlockSpec((2, tm, n), lambda j, i: (0, i, j)),
        out_shape=jax.ShapeDtypeStruct((2, t, f), BF16),
        compiler_params=_cp(2),
    )(dz, zrow, w3, h, h)


def _mm_ln(name, alpha, coef, a, w3, layer, x, g, b):
    t, k = a.shape
    d = w3.shape[2]
    tm = _tile(t, 512)

    def body(a_ref, w_ref, x_ref, g_ref, b_ref, y_ref, xh_ref, rs_ref):
        f = lax.dot_general(a_ref[...].astype(BF16), w_ref[...], _NN, preferred_element_type=F32)
        z = alpha * x_ref[...] + coef * f
        mu = jnp.mean(z, axis=-1, keepdims=True)
        zc = z - mu
        var = jnp.mean(zc * zc, axis=-1, keepdims=True)
        rstd = lax.rsqrt(var + LN_EPS)
        xh = zc * rstd
        y_ref[...] = xh * g_ref[...] + b_ref[...]
        xh_ref[...] = xh
        rs_ref[...] = rstd

    row = lambda c: pl.BlockSpec((tm, c), lambda i: (i, 0))
    vec = pl.BlockSpec((1, d), lambda i: (0, 0))
    sd = jax.ShapeDtypeStruct
    return pl.pallas_call(
        body, grid=(t // tm,), name=name,
        in_specs=[row(k), pl.BlockSpec((None, k, d), lambda i: (layer, 0, 0)), row(d), vec, vec],
        out_specs=[row(d), row(d), row(1)],
        out_shape=[sd((t, d), F32), sd((t, d), F32), sd((t, 1), F32)],
        compiler_params=_cp(1),
    )(a, w3, x, g, b)


_GELU_C = math.sqrt(2.0 / math.pi)


def _s5_act_fwd(name, ych, u, dvec):
    t, d = u.shape
    tm = _tile(t, 256)

    def body(y_ref, u_ref, d_ref, p_ref, a_ref):
        y = y_ref[...] + d_ref[...] * u_ref[...]
        p_ref[...] = y
        a_ref[...] = (0.5 * y * (1.0 + jnp.tanh(_GELU_C * (y + 0.044715 * y * y * y)))).astype(BF16)

    sd = jax.ShapeDtypeStruct
    return _rows_call(name, body, t, tm, [ych, u, dvec], [d, d, d], [sd((t, d), F32), sd((t, d), BF16)])


def _s5_act_bwd(name, dact, ypre, u, dvec):
    t, d = u.shape
    tm = _tile(t, 256)

    def body(da_ref, y_ref, u_ref, d_ref, dy_ref, ds_ref, dd_ref):
        y = y_ref[...]
        th = jnp.tanh(_GELU_C * (y + 0.044715 * y * y * y))
        dg = 0.5 * (1.0 + th) + 0.5 * y * (1.0 - th * th) * _GELU_C * (1.0 + 3.0 * 0.044715 * y * y)
        dy = da_ref[...] * dg
        dy_ref[...] = dy
        ds_ref[...] = dy * d_ref[...]
        pd = jnp.sum(dy * u_ref[...], axis=0, keepdims=True)
        i = pl.program_id(0)

        @pl.when(i == 0)
        def _():
            dd_ref[...] = pd

        @pl.when(i > 0)
        def _():
            dd_ref[...] += pd

    sd = jax.ShapeDtypeStruct
    return _rows_call(name, body, t, tm, [dact, ypre, u, dvec], [d, d, d, d],
                      [sd((t, d), F32), sd((t, d), F32)], [sd((1, d), F32)])


def _glu_fwd(name, vg):
    t, d2 = vg.shape
    d = d2 // 2
    tm = _tile(t, 256)

    def body(vg_ref, m_ref):
        m_ref[...] = vg_ref[:, :d] * _sigmoid(vg_ref[:, d:])

    return _rows_call(name, body, t, tm, [vg], [d2], [jax.ShapeDtypeStruct((t, d), F32)])[0]


def _glu_bwd(name, dm, vg):
    t, d2 = vg.shape
    d = d2 // 2
    tm = _tile(t, 256)

    def body(dm_ref, vg_ref, o_ref):
        sg = _sigmoid(vg_ref[:, d:])
        g = dm_ref[...]
        o_ref[:, :d] = (g * sg).astype(BF16)
        o_ref[:, d:] = (g * vg_ref[:, :d] * sg * (1.0 - sg)).astype(BF16)

    return _rows_call(name, body, t, tm, [dm, vg], [d, d2], [jax.ShapeDtypeStruct((t, d2), BF16)])[0]


def _adamw(name, w, g, m, v):
    r, c = w.shape
    tr = _tile(r, max(V7X_SUBLANES, (1 << 20) // (4 * c) // V7X_SUBLANES * V7X_SUBLANES))

    def body(w_ref, g_ref, m_ref, v_ref, d_ref, nm_ref, nv_ref):
        gg = g_ref[...]
        nm = ADAM_B1 * m_ref[...] + (1.0 - ADAM_B1) * gg
        nv = ADAM_B2 * v_ref[...] + (1.0 - ADAM_B2) * (gg * gg)
        m_hat = nm / (1.0 - ADAM_B1 ** ADAM_STEP)
        v_hat = nv / (1.0 - ADAM_B2 ** ADAM_STEP)
        d_ref[...] = -ADAM_LR * (m_hat / (jnp.sqrt(v_hat) + ADAM_EPS) + ADAM_WD * w_ref[...])
        nm_ref[...] = nm
        nv_ref[...] = nv

    sd = jax.ShapeDtypeStruct((r, c), F32)
    return _rows_call(name, body, r, tr, [w, g, m, v], [c] * 4, [sd, sd, sd])


def _my_shard():
    return 2 * lax.axis_index("x") + lax.axis_index("y")


def _my_core():
    return lax.axis_index("c")


def _adamw_join(name, w, mine, theirs, m, v):
    nl, r, c = w.shape
    h = r // 2
    tr = _tile(h, max(V7X_SUBLANES, (1 << 20) // (4 * c) // V7X_SUBLANES * V7X_SUBLANES))
    nb = h // tr

    def body(w_ref, a_ref, b_ref, m_ref, v_ref, g_ref, d_ref, nm_ref, nv_ref):
        gg = jnp.where(pl.program_id(1) == _my_core(), a_ref[...], b_ref[...])
        nm = ADAM_B1 * m_ref[...] + (1.0 - ADAM_B1) * gg
        nv = ADAM_B2 * v_ref[...] + (1.0 - ADAM_B2) * (gg * gg)
        m_hat = nm / (1.0 - ADAM_B1 ** ADAM_STEP)
        v_hat = nv / (1.0 - ADAM_B2 ** ADAM_STEP)
        g_ref[...] = gg
        d_ref[...] = -ADAM_LR * (m_hat / (jnp.sqrt(v_hat) + ADAM_EPS) + ADAM_WD * w_ref[...])
        nm_ref[...] = nm
        nv_ref[...] = nv

    full = pl.BlockSpec((None, tr, c), lambda l, hf, i: (l, hf * nb + i, 0))
    sd = jax.ShapeDtypeStruct((nl, r, c), F32)
    return pl.pallas_call(
        body, name=name, grid=(nl, 2, nb),
        in_specs=[full,
                  pl.BlockSpec((None, tr, c), lambda l, hf, i: (l, jnp.where(hf == _my_core(), i, 0), 0)),
                  pl.BlockSpec((None, tr, c), lambda l, hf, i: (l, jnp.where(hf == _my_core(), 0, i), 0)),
                  full, full],
        out_specs=[full, full, full, full],
        out_shape=[sd, sd, sd, sd],
        compiler_params=_cp(3),
    )(w, mine, theirs, m, v)


def _split3(x):
    hi = x.astype(BF16)
    r1 = x - hi.astype(F32)
    mid = r1.astype(BF16)
    lo = (r1 - mid.astype(F32)).astype(BF16)
    return hi, mid, lo


def _tri_sum(tri, x):
    dims = (((1,), (0,)), ((), ()))
    hi, mid, lo = _split3(x)
    out = lax.dot_general(tri, lo, dims, preferred_element_type=F32)
    out = out + lax.dot_general(tri, mid, dims, preferred_element_type=F32)
    return out + lax.dot_general(tri, hi, dims, preferred_element_type=F32)


def _fox_cumsum(name, fl, bf):
    t, h = fl.shape
    tb = _tile(t, 512)

    def body(fl_ref, bf_ref, c_ref, carry):
        i = pl.program_id(0)

        @pl.when(i == 0)
        def _():
            carry[...] = jnp.zeros_like(carry)

        x = fl_ref[...] + bf_ref[...]
        lf = jnp.minimum(x, 0.0) - jnp.log(1.0 + jnp.exp(-jnp.abs(x)))
        row = lax.broadcasted_iota(jnp.int32, (tb, tb), 0)
        col = lax.broadcasted_iota(jnp.int32, (tb, tb), 1)
        tri = jnp.where(row >= col, 1.0, 0.0).astype(BF16)
        c_ref[...] = _tri_sum(tri, lf) + carry[...]
        carry[...] += jnp.sum(lf, axis=0, keepdims=True)

    return pl.pallas_call(
        body, grid=(t // tb,), name=name,
        in_specs=[pl.BlockSpec((tb, h), lambda i: (i, 0)), pl.BlockSpec((1, h), lambda i: (0, 0))],
        out_specs=pl.BlockSpec((tb, h), lambda i: (i, 0)),
        out_shape=jax.ShapeDtypeStruct((t, h), F32),
        scratch_shapes=[pltpu.VMEM((1, h), F32)], compiler_params=_cp(1),
    )(fl, bf)


def _fox_cumsum_bwd(name, dcum, fl, bf):
    t, h = fl.shape
    tb = _tile(t, 512)
    nb = t // tb

    def body(dc_ref, fl_ref, bf_ref, df_ref, db_ref, carry):
        i = pl.program_id(0)

        @pl.when(i == 0)
        def _():
            carry[...] = jnp.zeros_like(carry)

        dc = dc_ref[...]
        row = lax.broadcasted_iota(jnp.int32, (tb, tb), 0)
        col = lax.broadcasted_iota(jnp.int32, (tb, tb), 1)
        tri = jnp.where(row <= col, 1.0, 0.0).astype(BF16)
        dlf = _tri_sum(tri, dc) + carry[...]
        carry[...] += jnp.sum(dc, axis=0, keepdims=True)
        x = fl_ref[...] + bf_ref[...]
        df = dlf / (1.0 + jnp.exp(x))
        df_ref[...] = df
        pb = jnp.sum(df, axis=0, keepdims=True)

        @pl.when(i == 0)
        def _():
            db_ref[...] = pb

        @pl.when(i > 0)
        def _():
            db_ref[...] += pb

    rev = lambda i: (nb - 1 - i, 0)
    return pl.pallas_call(
        body, grid=(nb,), name=name,
        in_specs=[pl.BlockSpec((tb, h), rev), pl.BlockSpec((tb, h), rev), pl.BlockSpec((1, h), lambda i: (0, 0))],
        out_specs=[pl.BlockSpec((tb, h), rev), pl.BlockSpec((1, h), lambda i: (0, 0))],
        out_shape=[jax.ShapeDtypeStruct((t, h), F32), jax.ShapeDtypeStruct((1, h), F32)],
        scratch_shapes=[pltpu.VMEM((1, h), F32)], compiler_params=_cp(1),
    )(dcum, fl, bf)


_NT = (((1,), (1,)), ((), ()))
_TN = (((0,), (0,)), ((), ()))
_NN = (((1,), (0,)), ((), ()))


def _causal_mask(s, r0):
    row = lax.broadcasted_iota(jnp.int32, s.shape, 0) + r0
    col = lax.broadcasted_iota(jnp.int32, s.shape, 1)
    return jnp.where(col <= row, s, NEG_INF)


def _first_head_lanes(hd):
    return lax.broadcasted_iota(jnp.int32, (1, 2 * hd), 1) < hd


def _attn_fwd(name, qkv, ccol, crow, nh):
    nb, tb, d3 = qkv.shape
    d = d3 // 3
    hd = d // nh
    lanes = 2 * hd
    assert lanes == V7X_LANES
    scale = 1.0 / math.sqrt(hd)

    def body(q_ref, k_ref, v_ref, cc_ref, cr_ref, o_ref, lse_ref):
        i = pl.program_id(1)
        first = _first_head_lanes(hd)
        q = q_ref[...] * scale
        res = []
        for hh in (0, 1):
            qh = jnp.where(first if hh == 0 else jnp.logical_not(first), q, jnp.zeros_like(q))
            cc = cc_ref[:, hh:hh + 1]

            def tile(j, carry, r0=0, nr=tb, ncol=tb, masked=False, qh=qh, cc=cc, hh=hh):
                m, l, acc = carry
                s = lax.dot_general(qh[r0:r0 + nr], k_ref[j, 0:ncol, :], _NT, preferred_element_type=F32)
                s = s + cc[r0:r0 + nr] - cr_ref[j][hh:hh + 1, 0:ncol]
                if masked:
                    s = _causal_mask(s, r0)
                m_new = jnp.maximum(m, jnp.max(s, axis=1, keepdims=True))
                p = jnp.exp(s - m_new)
                a = jnp.exp(m - m_new)
                l = a * l + jnp.sum(p, axis=1, keepdims=True)
                acc = a * acc + lax.dot_general(p.astype(BF16), v_ref[j, 0:ncol, :], _NN, preferred_element_type=F32)
                return m_new, l, acc

            init = (jnp.full((tb, 1), NEG_INF, F32), jnp.zeros((tb, 1), F32), jnp.zeros((tb, lanes), F32))
            m, l, acc = tile(i, lax.fori_loop(0, i, tile, init), masked=True)
            res.append((acc / l, m + jnp.log(l)))
        o_ref[...] = jnp.where(first, res[0][0], res[1][0])
        lse_ref[:, 0:1] = res[0][1]
        lse_ref[:, 1:2] = res[1][1]

    kb, vb = d // lanes, 2 * d // lanes
    return pl.pallas_call(
        body, grid=(nh // 2, nb), name=name,
        in_specs=[pl.BlockSpec((None, tb, lanes), lambda h, i: (i, 0, h)),
                  pl.BlockSpec((nb, tb, lanes), lambda h, i: (0, 0, kb + h)),
                  pl.BlockSpec((nb, tb, lanes), lambda h, i: (0, 0, vb + h)),
                  pl.BlockSpec((None, None, tb, 2), lambda h, i: (h, i, 0, 0)),
                  pl.BlockSpec((None, nb, 2, tb), lambda h, i: (h, 0, 0, 0))],
        out_specs=[pl.BlockSpec((None, tb, lanes), lambda h, i: (i, 0, h)),
                   pl.BlockSpec((None, None, tb, 2), lambda h, i: (h, i, 0, 0))],
        out_shape=[jax.ShapeDtypeStruct((nb, tb, d), F32), jax.ShapeDtypeStruct((nh // 2, nb, tb, 2), F32)],
        compiler_params=_cp(2),
    )(qkv, qkv, qkv, ccol, crow)


def _attn_bwd(name, qkv, ccol, crow, o, lse, do, nh):
    nb, tb, d3 = qkv.shape
    d = d3 // 3
    hd = d // nh
    lanes = 2 * hd
    hb = tb // 2
    scale = 1.0 / math.sqrt(hd)

    def body(q_ref, k_ref, v_ref, cc_ref, cr_ref, o_ref, lse_ref, do_ref, dq_ref, dk_ref, dv_ref, dr_ref, dc_ref, dq_acc):
        j = pl.program_id(1)

        @pl.when(j == 0)
        def _():
            dq_acc[...] = jnp.zeros_like(dq_acc)
            dr_ref[...] = jnp.zeros_like(dr_ref)

        first = _first_head_lanes(hd)
        kj = k_ref[...]
        vj = v_ref[...]
        dk = jnp.zeros((tb, lanes), F32)
        dv = jnp.zeros((tb, lanes), F32)
        for hh in (0, 1):
            mine = first if hh == 0 else jnp.logical_not(first)
            cr = cr_ref[hh:hh + 1, :]

            def tile(i, carry, r0=0, nr=tb, ncol=tb, masked=False, mine=mine, cr=cr, hh=hh):
                dk, dv, dc = carry
                rows = pl.ds(r0, nr)
                qi = q_ref[i, rows, :] * scale
                qh = jnp.where(mine, qi, jnp.zeros_like(qi))
                doh = jnp.where(mine, do_ref[i, rows, :], 0.0)
                dob = doh.astype(BF16)
                di = jnp.sum(doh * o_ref[i, rows, :], axis=1, keepdims=True)
                s = lax.dot_general(qh, kj[:ncol], _NT, preferred_element_type=F32)
                s = s + cc_ref[i, rows, hh:hh + 1] - cr[:, :ncol]
                if masked:
                    s = _causal_mask(s, r0)
                p = jnp.exp(s - lse_ref[i, rows, hh:hh + 1])
                dp = lax.dot_general(dob, vj[:ncol], _NT, preferred_element_type=F32)
                ds = p * (dp - di)
                dsb = ds.astype(BF16)
                dvc = lax.dot_general(p.astype(BF16), dob, _TN, preferred_element_type=F32)
                dkc = lax.dot_general(dsb, qh, _TN, preferred_element_type=F32)
                dcc = jnp.sum(ds, axis=0, keepdims=True)
                if ncol < tb:
                    dvc = jnp.concatenate([dvc, jnp.zeros((tb - ncol, lanes), F32)], axis=0)
                    dkc = jnp.concatenate([dkc, jnp.zeros((tb - ncol, lanes), F32)], axis=0)
                    dcc = jnp.concatenate([dcc, jnp.zeros((1, tb - ncol), F32)], axis=1)
                dq = lax.dot_general(dsb, kj[:ncol], _NN, preferred_element_type=F32) * scale
                dq_acc[i, rows, :] += jnp.where(mine, dq, 0.0)
                dr_ref[i, rows, hh:hh + 1] += jnp.sum(ds, axis=1, keepdims=True)
                return dk + dkc, dv + dvc, dc + dcc

            carry = tile(j, (dk, dv, jnp.zeros((1, tb), F32)), 0, hb, hb, True)
            carry = tile(j, carry, hb, hb, tb, True)
            dk, dv, dc = lax.fori_loop(j + 1, nb, tile, carry)
            dc_ref[hh:hh + 1, :] = dc
        dk_ref[...] = dk.astype(BF16)
        dv_ref[...] = dv.astype(BF16)

        @pl.when(j == nb - 1)
        def _():
            dq_ref[...] = dq_acc[...].astype(BF16)

    kb, vb = d // lanes, 2 * d // lanes
    whole = lambda c: pl.BlockSpec((nb, tb, lanes), lambda h, j: (0, 0, c + h))
    block = lambda c: pl.BlockSpec((None, tb, lanes), lambda h, j: (j, 0, c + h))
    cols = pl.BlockSpec((None, nb, tb, 2), lambda h, j: (h, 0, 0, 0))
    rows = pl.BlockSpec((None, None, 2, tb), lambda h, j: (h, j, 0, 0))
    sd = jax.ShapeDtypeStruct
    return pl.pallas_call(
        body, grid=(nh // 2, nb), name=name,
        in_specs=[whole(0), block(kb), block(vb), cols, rows, whole(0), cols, whole(0)],
        out_specs=[whole(0), block(0), block(0), cols, rows],
        out_shape=[sd((nb, tb, d), BF16), sd((nb, tb, d), BF16), sd((nb, tb, d), BF16),
                   sd((nh // 2, nb, tb, 2), F32), sd((nh // 2, nb, 2, tb), F32)],
        scratch_shapes=[pltpu.VMEM((nb, tb, lanes), F32)],
        compiler_params=_cp(2),
    )(qkv, qkv, qkv, ccol, crow, o, lse, do)


def _cmul(ar, ai, br, bi):
    return ar * br - ai * bi, ar * bi + ai * br


def _s5_scan(name, lam, xin, hs=None):
    reverse = hs is not None
    _, seg, ns, w = xin.shape
    assert ns == SCAN_SEGMENTS
    wb = min(w, 2 * V7X_LANES)
    nsq = seg.bit_length() - 1
    assert (1 << nsq) == seg

    def body(*refs):
        if reverse:
            lam_ref, x_ref, h_ref, o_ref, dl_ref = refs
        else:
            lam_ref, x_ref, o_ref = refs
        lr = jnp.broadcast_to(lam_ref[0], (ns, wb))
        li = jnp.broadcast_to(lam_ref[1], (ns, wb))
        if reverse:
            li = -li
        zero = jnp.zeros((ns, wb), F32)
        at = (lambda n: seg - 1 - n) if reverse else (lambda n: n)

        def local(n, c):
            r = at(n)
            mr, mi = _cmul(lr, li, c[0], c[1])
            nr = mr + x_ref[0, r]
            ni = mi + x_ref[1, r]
            o_ref[0, r] = nr
            o_ref[1, r] = ni
            return nr, ni

        er, ei = lax.fori_loop(0, seg, local, (zero, zero))
        pr, pi = lr, li
        for _ in range(nsq):
            pr, pi = _cmul(pr, pi, pr, pi)
        sub = lax.broadcasted_iota(jnp.int32, (ns, wb), 0)

        def shifted(a, sh):
            if reverse:
                return jnp.where(sub < ns - sh, pltpu.roll(a, ns - sh, 0), 0.0)
            return jnp.where(sub >= sh, pltpu.roll(a, sh, 0), 0.0)

        xr, xi = er, ei
        sh = 1
        while sh < ns:
            tr, ti = _cmul(pr, pi, shifted(xr, sh), shifted(xi, sh))
            xr, xi = xr + tr, xi + ti
            pr, pi = _cmul(pr, pi, pr, pi)
            sh *= 2
        cr, ci = shifted(xr, 1), shifted(xi, 1)

        def fix(r, q):
            tr, ti = _cmul(q[0], q[1], cr, ci)
            gr = o_ref[0, r] + tr
            gi = o_ref[1, r] + ti
            o_ref[0, r] = gr
            o_ref[1, r] = gi
            return gr, gi

        if not reverse:
            def fixup(n, q):
                fix(n, q)
                return _cmul(q[0], q[1], lr, li)

            lax.fori_loop(0, seg, fixup, (lr, li))
            return

        def fixup_acc(n, c):
            qr, qi, ar, ai = c
            r = seg - 1 - n
            gr, gi = fix(r, (qr, qi))
            hr = h_ref[0, r - 1]
            hi = h_ref[1, r - 1]
            qr, qi = _cmul(qr, qi, lr, li)
            return qr, qi, ar + gr * hr + gi * hi, ai + gi * hr - gr * hi

        qr, qi, ar, ai = lax.fori_loop(0, seg - 1, fixup_acc, (lr, li, zero, zero))
        gr, gi = fix(0, (qr, qi))
        hr = jnp.where(sub >= 1, pltpu.roll(h_ref[0, seg - 1], 1, 0), 0.0)
        hi = jnp.where(sub >= 1, pltpu.roll(h_ref[1, seg - 1], 1, 0), 0.0)
        dl_ref[0] = ar + gr * hr + gi * hi
        dl_ref[1] = ai + gi * hr - gr * hi

    big = pl.BlockSpec((2, seg, ns, wb), lambda j: (0, 0, 0, j))
    lam_spec = pl.BlockSpec((2, 1, wb), lambda j: (0, 0, j))
    sd = jax.ShapeDtypeStruct
    if reverse:
        return pl.pallas_call(
            body, grid=(w // wb,), name=name, in_specs=[lam_spec, big, big],
            out_specs=[big, pl.BlockSpec((2, ns, wb), lambda j: (0, 0, j))],
            out_shape=[sd(xin.shape, F32), sd((2, ns, w), F32)], compiler_params=_cp(1),
        )(lam, xin, hs)
    return pl.pallas_call(
        body, grid=(w // wb,), name=name, in_specs=[lam_spec, big], out_specs=big,
        out_shape=sd(xin.shape, F32), compiler_params=_cp(1),
    )(lam, xin)


def _place():
    x, y, c = lax.axis_index("x"), lax.axis_index("y"), lax.axis_index("c")
    chips = [(1 - x, y), (x, 1 - y), (1 - x, 1 - y)]
    return x, y, c, chips


def _comm_params():
    return pltpu.CompilerParams(vmem_limit_bytes=VMEM_LIMIT)


def _cast_place(name, w):
    nl, r, c = w.shape
    tr = _tile(r, max(16, (1 << 20) // (4 * c) // 16 * 16), 16)

    def body(w_ref, o_ref):
        o_ref[...] = w_ref[...].astype(BF16)

    return pl.pallas_call(
        body, name=name, grid=(nl, r // tr),
        in_specs=[pl.BlockSpec((None, tr, c), lambda l, i: (l, i, 0))],
        out_specs=pl.BlockSpec((None, None, tr, c), lambda l, i: (l, _my_shard(), i, 0)),
        out_shape=jax.ShapeDtypeStruct((nl, N_CHIPS, r, c), BF16),
        compiler_params=_cp(2),
    )(w)


def _gather_shards(name, bufs):
    n = len(bufs)

    def body(*refs):
        outs = refs[n:2 * n]
        send_sems, recv_sems = refs[2 * n:]
        x, y, c, chips = _place()
        my = 2 * x + y
        sibling = (x, y, 1 - c)

        def part(t, shard, half):
            h = bufs[t].shape[2] // 2
            return outs[t].at[:, shard, pl.ds(half * h, h)]

        def copy(t, k, ref, to):
            return pltpu.make_async_remote_copy(src_ref=ref, dst_ref=ref, send_sem=send_sems.at[t, k],
                                                recv_sem=recv_sems.at[t, k], device_id=to, device_id_type=MESH)

        sent = []
        for t in range(n):
            for k, chip in enumerate(chips):
                sent.append(copy(t, k, part(t, my, c), (*chip, c)))
                sent[-1].start()
        for k, chip in enumerate(chips):
            shard = 2 * chip[0] + chip[1]
            for t in range(n):
                copy(t, k, part(t, shard, c), (*chip, c)).wait_recv()
                sent.append(copy(t, 3 + k, part(t, shard, c), sibling))
                sent[-1].start()
        for k, chip in enumerate(chips):
            shard = 2 * chip[0] + chip[1]
            for t in range(n):
                copy(t, 3 + k, part(t, shard, 1 - c), sibling).wait_recv()
        for cp in sent:
            cp.wait_send()

    return pl.pallas_call(
        body, name=name, in_specs=[ANY] * n, out_specs=[ANY] * n,
        out_shape=[jax.ShapeDtypeStruct(b.shape, b.dtype) for b in bufs],
        input_output_aliases={t: t for t in range(n)},
        scratch_shapes=[pltpu.SemaphoreType.DMA((n, 6)), pltpu.SemaphoreType.DMA((n, 6))],
        compiler_params=_comm_params(),
    )(*bufs)


HBM_SPEC = pl.BlockSpec(memory_space=pltpu.HBM)
SEM_SPEC = pl.BlockSpec(memory_space=pltpu.SEMAPHORE)


def _split_params():
    return pltpu.CompilerParams(has_side_effects=pltpu.SideEffectType.DATAFLOW_SIDE_EFFECTING)


def _gather_start(name, bufs, groups):
    n, ng = len(bufs), len(groups)

    def body(*refs):
        sems = refs[n:n + 2 * ng]
        outs = refs[n + 2 * ng:]
        x, y, c, chips = _place()
        my = 2 * x + y
        for gi, group in enumerate(groups):
            for idx, (t, layer) in enumerate(group):
                block = outs[t].at[layer, my]
                for k, chip in enumerate(chips):
                    pltpu.make_async_remote_copy(
                        src_ref=block, dst_ref=block, send_sem=sems[2 * gi].at[3 * idx + k],
                        recv_sem=sems[2 * gi + 1].at[3 * idx + k], device_id=(*chip, c), device_id_type=MESH).start()

    sem_shapes = []
    for group in groups:
        sem_shapes += [pltpu.SemaphoreType.DMA((3 * len(group),))] * 2
    res = pl.pallas_call(
        body, name=name, in_specs=[HBM_SPEC] * n,
        out_specs=[SEM_SPEC] * (2 * ng) + [HBM_SPEC] * n,
        out_shape=sem_shapes + [pltpu.HBM(b.shape, b.dtype) for b in bufs],
        input_output_aliases={t: 2 * ng + t for t in range(n)},
        compiler_params=_split_params(),
    )(*[pltpu.with_memory_space_constraint(b, pltpu.HBM) for b in bufs])
    sems = [(res[2 * gi], res[2 * gi + 1]) for gi in range(ng)]
    return sems, list(res[2 * ng:])


def _gather_wait(name, bufs, send_sems, recv_sems, after, group):
    n = len(bufs)

    def body(*refs):
        ss, rs = refs[n], refs[n + 1]
        outs = refs[n + 3:]
        x, y, c, chips = _place()
        my = 2 * x + y
        for idx, (t, layer) in enumerate(group):
            for k, chip in enumerate(chips):
                cp = pltpu.make_async_remote_copy(
                    src_ref=outs[t].at[layer, my], dst_ref=outs[t].at[layer, 2 * chip[0] + chip[1]],
                    send_sem=ss.at[3 * idx + k], recv_sem=rs.at[3 * idx + k], device_id=(*chip, c), device_id_type=MESH)
                cp.wait_send()
                cp.wait_recv()

    return list(pl.pallas_call(
        body, name=name, in_specs=[HBM_SPEC] * n + [SEM_SPEC, SEM_SPEC, ANY],
        out_specs=[HBM_SPEC] * n,
        out_shape=[pltpu.HBM(b.shape, b.dtype) for b in bufs],
        input_output_aliases={t: t for t in range(n)},
        compiler_params=_split_params(),
    )(*bufs, send_sems, recv_sems, after))


N_PARTS = 7


def _scatter_items(send, rx, items, c, chips, x, y):
    my = 2 * x + y
    out = []
    for i, (k, layer) in enumerate(items):
        h = send[k].shape[2] // 2
        for kk, chip in enumerate(chips):
            shard = 2 * chip[0] + chip[1]
            for hf in (0, 1):
                out.append((send[k].at[layer, shard, pl.ds(hf * h, h)], rx[k].at[2 * kk + c, layer],
                            N_PARTS * i + 2 * kk + hf, N_PARTS * i + 2 * kk + c, (*chip, hf)))
        out.append((send[k].at[layer, my, pl.ds((1 - c) * h, h)], rx[k].at[N_PARTS - 1, layer],
                    N_PARTS * i + N_PARTS - 1, N_PARTS * i + N_PARTS - 1, (x, y, 1 - c)))
    return out


def _scatter_start(name, send, rx, items):
    n = len(send)
    m = N_PARTS * len(items)

    def body(*refs):
        ssem, rsem = refs[2 * n], refs[2 * n + 1]
        s_out, r_out = refs[2 * n + 2:3 * n + 2], refs[3 * n + 2:4 * n + 2]
        x, y, c, chips = _place()
        for src, dst, si, ri, to in _scatter_items(s_out, r_out, items, c, chips, x, y):
            pltpu.make_async_remote_copy(src_ref=src, dst_ref=dst, send_sem=ssem.at[si], recv_sem=rsem.at[ri],
                                         device_id=to, device_id_type=MESH).start()
        refs[4 * n + 2][...] = jnp.zeros((V7X_SUBLANES, V7X_LANES), F32)

    res = pl.pallas_call(
        body, name=name, in_specs=[HBM_SPEC] * (2 * n),
        out_specs=[SEM_SPEC, SEM_SPEC] + [HBM_SPEC] * (2 * n) + [pl.BlockSpec(memory_space=pltpu.VMEM)],
        out_shape=[pltpu.SemaphoreType.DMA((m,)), pltpu.SemaphoreType.DMA((m,))]
        + [pltpu.HBM(b.shape, b.dtype) for b in list(send) + list(rx)]
        + [jax.ShapeDtypeStruct((V7X_SUBLANES, V7X_LANES), F32)],
        input_output_aliases={t: 2 + t for t in range(2 * n)},
        compiler_params=_split_params(),
    )(*[pltpu.with_memory_space_constraint(b, pltpu.HBM) for b in list(send) + list(rx)])
    return (res[0], res[1]), list(res[2:2 + n]), list(res[2 + n:2 + 2 * n]), res[2 + 2 * n][0, 0]


def _scatter_wait(name, send, rx, ssem, rsem, after, items):
    n = len(send)

    def body(*refs):
        ss, rs = refs[2 * n], refs[2 * n + 1]
        s_out, r_out = refs[2 * n + 3:3 * n + 3], refs[3 * n + 3:]
        x, y, c, chips = _place()
        for i, (src, dst, si, ri, to) in enumerate(_scatter_items(s_out, r_out, items, c, chips, x, y)):
            arrival = i % N_PARTS
            landed = r_out[items[i // N_PARTS][0]].at[arrival, items[i // N_PARTS][1]]
            cp = pltpu.make_async_remote_copy(src_ref=src, dst_ref=landed, send_sem=ss.at[si],
                                              recv_sem=rs.at[N_PARTS * (i // N_PARTS) + arrival],
                                              device_id=to, device_id_type=MESH)
            cp.wait_send()
            cp.wait_recv()

    res = pl.pallas_call(
        body, name=name, in_specs=[HBM_SPEC] * (2 * n) + [SEM_SPEC, SEM_SPEC, ANY],
        out_specs=[HBM_SPEC] * (2 * n),
        out_shape=[pltpu.HBM(b.shape, b.dtype) for b in list(send) + list(rx)],
        input_output_aliases={t: t for t in range(2 * n)},
        compiler_params=_split_params(),
    )(*send, *rx, ssem, rsem, after)
    return list(res[:n]), list(res[n:])


def _chip_sum(name, g, rx):
    nl, _, r, c = g.shape
    h = r // 2
    tr = _tile(h, max(V7X_SUBLANES * 2, (1 << 19) // (2 * c) // 16 * 16), 16)
    nb = h // tr

    def body(g_ref, r_ref, o_ref):
        acc = g_ref[...].astype(F32)
        for k in range(N_PARTS):
            acc = acc + r_ref[k].astype(F32)
        o_ref[...] = acc

    return pl.pallas_call(
        body, name=name, grid=(nl, nb),
        in_specs=[pl.BlockSpec((None, None, tr, c), lambda l, i: (l, _my_shard(), _my_core() * nb + i, 0)),
                  pl.BlockSpec((N_PARTS, None, tr, c), lambda l, i: (0, l, i, 0))],
        out_specs=pl.BlockSpec((None, tr, c), lambda l, i: (l, i, 0)),
        out_shape=jax.ShapeDtypeStruct((nl, h, c), F32),
        compiler_params=_cp(2),
    )(g, rx)


def _send_half(name, fs):
    n = len(fs)

    def body(*refs):
        ins, outs = refs[:n], refs[n:2 * n]
        send_sems, recv_sems = refs[2 * n:]
        x, y, c, _ = _place()
        cps = []
        for t in range(n):
            cps.append(pltpu.make_async_remote_copy(
                src_ref=ins[t], dst_ref=outs[t], send_sem=send_sems.at[t], recv_sem=recv_sems.at[t],
                device_id=(x, y, 1 - c), device_id_type=MESH))
            cps[-1].start()
        for cp in cps:
            cp.wait()

    return pl.pallas_call(
        body, name=name, in_specs=[ANY] * n, out_specs=[ANY] * n,
        out_shape=[jax.ShapeDtypeStruct(f.shape, f.dtype) for f in fs],
        scratch_shapes=[pltpu.SemaphoreType.DMA((n,)), pltpu.SemaphoreType.DMA((n,))],
        compiler_params=_comm_params(),
    )(*fs)


def _peers(x, y, c):
    rel = [(dx, dy, dc) for dx in (0, 1) for dy in (0, 1) for dc in (0, 1) if (dx, dy, dc) != (0, 0, 0)]
    return [(1 - x if dx else x, 1 - y if dy else y, 1 - c if dc else c) for dx, dy, dc in rel]


def _share_start(name, v, land):
    def body(v_ref, land_ref, ssem, rsem, v_out, land_out, token):
        x, y, c, _ = _place()
        me = 4 * x + 2 * y + c
        for k, peer in enumerate(_peers(x, y, c)):
            pltpu.make_async_remote_copy(src_ref=v_out, dst_ref=land_out.at[me], send_sem=ssem.at[k],
                                         recv_sem=rsem.at[k], device_id=peer, device_id_type=MESH).start()
        token[...] = jnp.zeros((V7X_SUBLANES, V7X_LANES), F32)

    res = pl.pallas_call(
        body, name=name, in_specs=[HBM_SPEC, HBM_SPEC],
        out_specs=[SEM_SPEC, SEM_SPEC, HBM_SPEC, HBM_SPEC, pl.BlockSpec(memory_space=pltpu.VMEM)],
        out_shape=[pltpu.SemaphoreType.DMA((N_DEV - 1,)), pltpu.SemaphoreType.DMA((N_DEV - 1,)),
                   pltpu.HBM(v.shape, v.dtype), pltpu.HBM(land.shape, land.dtype),
                   jax.ShapeDtypeStruct((V7X_SUBLANES, V7X_LANES), F32)],
        input_output_aliases={0: 2, 1: 3},
        compiler_params=_split_params(),
    )(pltpu.with_memory_space_constraint(v, pltpu.HBM), pltpu.with_memory_space_constraint(land, pltpu.HBM))
    return (res[0], res[1]), res[2], res[3], res[4][0, 0]


def _share_wait(name, v, land, ssem, rsem, after):
    def body(v_ref, land_ref, ss, rs, after_ref, v_out, land_out):
        x, y, c, _ = _place()
        for k, (px, py, pc) in enumerate(_peers(x, y, c)):
            cp = pltpu.make_async_remote_copy(src_ref=v_out, dst_ref=land_out.at[4 * px + 2 * py + pc],
                                              send_sem=ss.at[k], recv_sem=rs.at[k], device_id=(px, py, pc),
                                              device_id_type=MESH)
            cp.wait_send()
            cp.wait_recv()

    res = pl.pallas_call(
        body, name=name, in_specs=[HBM_SPEC, HBM_SPEC, SEM_SPEC, SEM_SPEC, ANY],
        out_specs=[HBM_SPEC, HBM_SPEC],
        out_shape=[pltpu.HBM(v.shape, v.dtype), pltpu.HBM(land.shape, land.dtype)],
        input_output_aliases={0: 0, 1: 1},
        compiler_params=_split_params(),
    )(v, land, ssem, rsem, after)
    return res[0], res[1]


def _sum_devices(name, v, land):
    r, c = v.shape
    tr = _tile(r, 512)

    def body(v_ref, land_ref, o_ref):
        x, y, cc, _ = _place()
        me = 4 * x + 2 * y + cc
        own = v_ref[...]
        acc = jnp.where(me == 0, own, land_ref[0])
        for k in range(1, N_DEV):
            acc = acc + jnp.where(me == k, own, land_ref[k])
        o_ref[...] = acc

    return pl.pallas_call(
        body, grid=(r // tr,), name=name,
        in_specs=[pl.BlockSpec((tr, c), lambda i: (i, 0)), pl.BlockSpec((N_DEV, tr, c), lambda i: (0, i, 0))],
        out_specs=pl.BlockSpec((tr, c), lambda i: (i, 0)),
        out_shape=jax.ShapeDtypeStruct((r, c), F32), compiler_params=_cp(1),
    )(v, land)


def _rows_view(wall):
    nl, s, r, c = wall.shape
    return wall.reshape(nl, s * r, c)


def _ffn_fwd(tag, alpha, x, w_in, w_out3, layer, g, b):
    h, a = _ffn_in(f"{tag}_in", x, w_in, layer)
    y, xhat, rstd = _mm_ln(f"{tag}_out", alpha, 0.5, a, w_out3, layer, x, g, b)
    return y, (x, h, a, xhat, rstd)


def _dx_ln(name, a, b, *, nk, a_blk, a_map, b_blk, b_map, alpha, dz, nxt):
    t, d = dz.shape
    tm = a_blk[-2]
    xhat, rstd, g = nxt
    through = xhat is not None

    def body(*refs):
        a_ref, b_ref, dz_ref = refs[:3]
        if through:
            xh_ref, rs_ref, g_ref, o_ref, dg_ref, db_ref, acc = refs[3:]
        else:
            g_ref, o_ref, acc = refs[3:]
        i, kk = pl.program_id(0), pl.program_id(1)
        p = lax.dot_general(a_ref[...].astype(BF16), b_ref[...].astype(BF16), _NT, preferred_element_type=F32)

        @pl.when(kk == 0)
        def _():
            acc[...] = p

        @pl.when(kk > 0)
        def _():
            acc[...] += p

        @pl.when(kk == nk - 1)
        def _():
            dy = alpha * dz_ref[...] + acc[...]
            if not through:
                o_ref[...] = dy + g_ref[...]
                return
            xh = xh_ref[...]
            dxh = dy * g_ref[...]
            m1 = jnp.mean(dxh, axis=-1, keepdims=True)
            m2 = jnp.mean(dxh * xh, axis=-1, keepdims=True)
            o_ref[...] = rs_ref[...] * (dxh - m1 - xh * m2)
            pg = jnp.sum(dy * xh, axis=0, keepdims=True)
            pb = jnp.sum(dy, axis=0, keepdims=True)

            @pl.when(i == 0)
            def _():
                dg_ref[...] = pg
                db_ref[...] = pb

            @pl.when(i > 0)
            def _():
                dg_ref[...] += pg
                db_ref[...] += pb

    row = lambda c: pl.BlockSpec((tm, c), lambda i, kk: (i, 0))
    vec = pl.BlockSpec((1, d), lambda i, kk: (0, 0))
    sd = jax.ShapeDtypeStruct
    in_specs = [pl.BlockSpec(a_blk, a_map), pl.BlockSpec(b_blk, b_map), row(d)]
    args = [a, b, dz]
    if through:
        in_specs += [row(d), row(1), vec]
        args += [xhat, rstd, g]
        out_specs, out_shape = [row(d), vec, vec], [sd((t, d), F32), sd((1, d), F32), sd((1, d), F32)]
    else:
        in_specs += [vec]
        args += [g]
        out_specs, out_shape = row(d), sd((t, d), F32)
    return pl.pallas_call(
        body, grid=(t // tm, nk), name=name, in_specs=in_specs, out_specs=out_specs, out_shape=out_shape,
        scratch_shapes=[pltpu.VMEM((tm, d), F32)], compiler_params=_cp(2),
    )(*args)


def _ffn_bwd(tag, alpha, dz, saved, w_in, w_out3, layer, g_win, g_wout3, grads_done, nxt, zrow):
    x, h, a, _, _ = saved
    t = x.shape[0]
    _, s, k, n = w_in.shape
    tm = _tile(t, 512)
    g_wout3 = _mm_tn(f"{tag}_dwout", a, dz, tm=n, scale=0.5, layer=layer, into=g_wout3)
    dh = _ffn_da(f"{tag}_da", dz, w_out3, layer, h, zrow)
    g_win = _mm(f"{tag}_dwin", x, dh, mode="tn", grid=(s, t // tm), kaxis=1,
                a_blk=(tm, k), a_map=lambda j, kk: (kk, 0),
                b_blk=(None, tm, n), b_map=lambda j, kk: (j // 2, kk, j % 2),
                o_shape=w_in.shape, o_blk=(None, None, k, n), o_map=lambda j, kk: (layer, j, 0, 0),
                o_dtype=g_win.dtype, into=g_win)
    zero = grads_done(g_win, g_wout3)
    return _dx_ln(f"{tag}_dx", dh, w_in, nk=s, a_blk=(None, tm, n), a_map=lambda i, kk: (kk // 2, i, kk % 2),
                  b_blk=(None, None, k, n), b_map=lambda i, kk: (layer, kk, 0, 0),
                  alpha=alpha, dz=dz, nxt=(nxt[0], nxt[1], nxt[2] + zero))


def _fox_fwd(tag, alpha, x, w_pad, bf, w_o3, layer, g, b):
    t, d = x.shape
    nh = bf.shape[1]
    tb = _tile(t, ATTN_BLOCK)
    nb = t // tb
    qkv = _mm_nn(f"{tag}_qkv", x, w_pad[:, :3 * d], o_dtype=BF16, tn=d).reshape(nb, tb, 3 * d)
    fl = _mm_nn(f"{tag}_gate", x, w_pad[:, 3 * d:])[:, :nh]
    cum = _fox_cumsum(f"{tag}_cum", fl, bf)
    ccol = cum.reshape(nb, tb, nh // 2, 2).transpose(2, 0, 1, 3)
    crow = cum.reshape(nb, tb, nh // 2, 2).transpose(2, 0, 3, 1)
    o, lse = _attn_fwd(f"{tag}_attn", qkv, ccol, crow, nh)
    o2 = o.reshape(t, d)
    y, xhat, rstd = _mm_ln(f"{tag}_oproj", alpha, 1.0, o2, w_o3, layer, x, g, b)
    return y, xhat, rstd, (x, qkv, ccol, crow, o, lse, fl, w_pad)


def _fox_bwd(tag, alpha, dm, saved, bf, w_o3, layer, g_wo3, grads_done, nxt):
    x, qkv, ccol, crow, o, lse, fl, w_pad = saved
    t, d = x.shape
    nh = bf.shape[1]
    nb, tb, _ = qkv.shape
    tm = _tile(t, 512)
    g_wo3 = _mm_tn(f"{tag}_dwo", o.reshape(t, d), dm, layer=layer, into=g_wo3)
    do = _mm_nt(f"{tag}_do", dm, w_o3, layer=layer).reshape(nb, tb, d)
    dq, dk, dv, drow, dcol = _attn_bwd(f"{tag}_attn_bwd", qkv, ccol, crow, o, lse, do, nh)
    dcum = drow.transpose(1, 2, 0, 3).reshape(t, nh) - dcol.transpose(1, 3, 0, 2).reshape(t, nh)
    dfl, dbf = _fox_cumsum_bwd(f"{tag}_cum_bwd", dcum, fl, bf)
    pad = w_pad.shape[1] - 3 * d - nh
    dproj = jnp.concatenate([dq.reshape(t, d), dk.reshape(t, d), dv.reshape(t, d),
                             dfl.astype(BF16), jnp.zeros((t, pad), BF16)], axis=1)
    d_wpad = _mm_tn(f"{tag}_dwin", x, dproj, tn=_tile(w_pad.shape[1], 640, V7X_LANES))
    zero = grads_done(g_wo3, d_wpad)
    cols = w_pad.shape[1]
    out = _dx_ln(f"{tag}_dx", dproj, w_pad, nk=1, a_blk=(tm, cols), a_map=lambda i, kk: (i, 0),
                 b_blk=(d, cols), b_map=lambda i, kk: (0, 0), alpha=alpha, dz=dm, nxt=(nxt[0], nxt[1], nxt[2] + zero))
    return out, dbf


def _to_segments(a):
    t, d = a.shape
    return a.reshape(SCAN_SEGMENTS, t // SCAN_SEGMENTS, d).transpose(1, 0, 2).reshape(t, d)


def _from_segments(a):
    t, d = a.shape
    return a.reshape(t // SCAN_SEGMENTS, SCAN_SEGMENTS, d).transpose(1, 0, 2).reshape(t, d)


def _s5_discretise(a_re, a_im, log_dt, b_re, b_im):
    dt = jnp.exp(log_dt)[:, None]
    mag = jnp.exp(a_re * dt)
    ang = a_im * dt
    lb_re = mag * jnp.cos(ang)
    lb_im = mag * jnp.sin(ang)
    den = a_re * a_re + a_im * a_im
    nr = lb_re - 1.0
    ni = lb_im
    z_re = (nr * a_re + ni * a_im) / den
    z_im = (ni * a_re - nr * a_im) / den
    bb_re = z_re[..., None] * b_re - z_im[..., None] * b_im
    bb_im = z_re[..., None] * b_im + z_im[..., None] * b_re
    return lb_re, lb_im, bb_re, bb_im


S5_BLOCK_GROUPS = 8


def _blockdiag_in(bb):
    g, p, h = bb.shape
    e = jnp.eye(S5_BLOCK_GROUPS, dtype=bb.dtype)
    b4 = bb.reshape(g // S5_BLOCK_GROUPS, S5_BLOCK_GROUPS, p, h)
    return jnp.einsum("jgph,gf->jghfp", b4, e).reshape(g // S5_BLOCK_GROUPS, S5_BLOCK_GROUPS * h, S5_BLOCK_GROUPS * p)


def _blockdiag_in_grad(d):
    nj, gh, gp = d.shape
    h, p = gh // S5_BLOCK_GROUPS, gp // S5_BLOCK_GROUPS
    e = jnp.eye(S5_BLOCK_GROUPS, dtype=d.dtype)
    d6 = d.reshape(nj, S5_BLOCK_GROUPS, h, S5_BLOCK_GROUPS, p)
    return jnp.einsum("jghfp,gf->jgph", d6, e).reshape(nj * S5_BLOCK_GROUPS, p, h)


def _blockdiag_out(cc):
    g, h, p = cc.shape
    e = jnp.eye(S5_BLOCK_GROUPS, dtype=cc.dtype)
    c4 = cc.reshape(g // S5_BLOCK_GROUPS, S5_BLOCK_GROUPS, h, p)
    return jnp.einsum("jghp,gf->jfpgh", c4, e).reshape(g // S5_BLOCK_GROUPS, S5_BLOCK_GROUPS * p, S5_BLOCK_GROUPS * h)


def _blockdiag_out_grad(d):
    nj, gp, gh = d.shape
    h, p = gh // S5_BLOCK_GROUPS, gp // S5_BLOCK_GROUPS
    e = jnp.eye(S5_BLOCK_GROUPS, dtype=d.dtype)
    d6 = d.reshape(nj, S5_BLOCK_GROUPS, p, S5_BLOCK_GROUPS, h)
    return jnp.einsum("jfpgh,gf->jghp", d6, e).reshape(nj * S5_BLOCK_GROUPS, h, p)


def _s5_fwd(tag, x, prm, w_out, layer):
    a_re, a_im, log_dt, b_re, b_im, c_re, c_im, d_skip = prm
    t, d = x.shape
    g, p = a_re.shape
    w = g * p
    nj = g // S5_BLOCK_GROUPS
    cw, sw = S5_BLOCK_GROUPS * S5_GROUP, S5_BLOCK_GROUPS * p
    seg = t // SCAN_SEGMENTS
    tm = _tile(t, 4096)
    lb_re, lb_im, bb_re, bb_im = _s5_discretise(a_re, a_im, log_dt, b_re, b_im)
    lam = jnp.stack([lb_re.reshape(1, w), lb_im.reshape(1, w)])
    bs = jnp.stack([_blockdiag_in(bb_re), _blockdiag_in(bb_im)]).astype(BF16)
    cs = jnp.stack([_blockdiag_out(c_re), -_blockdiag_out(c_im)]).astype(BF16)
    dvec = d_skip.reshape(1, d)
    u = _to_segments(x)
    bu = _mm(f"{tag}_bu", u, bs, mode="nn", grid=(2, nj, t // tm), kaxis=None,
             a_blk=(tm, cw), a_map=lambda r, j, i: (i, j),
             b_blk=(None, None, cw, sw), b_map=lambda r, j, i: (r, j, 0, 0),
             o_shape=(2, t, w), o_blk=(None, tm, sw), o_map=lambda r, j, i: (r, i, j))
    hs = _s5_scan(f"{tag}_scan", lam, bu.reshape(2, seg, SCAN_SEGMENTS, w)).reshape(2, t, w)
    ych = _mm(f"{tag}_ch", hs, cs, mode="nn", grid=(nj, t // tm, 2), kaxis=2,
              a_blk=(None, tm, sw), a_map=lambda j, i, r: (r, i, j),
              b_blk=(None, None, sw, cw), b_map=lambda j, i, r: (r, j, 0, 0),
              o_shape=(t, d), o_blk=(tm, cw), o_map=lambda j, i, r: (i, j))
    ypre, act = _s5_act_fwd(f"{tag}_act", ych, u, dvec)
    vg = _mm_shards_nn(f"{tag}_wout", act, w_out, layer, F32)
    m = _from_segments(_glu_fwd(f"{tag}_glu", vg))
    return m, (u, lam, bs, cs, dvec, hs, ypre, act, vg)


def _s5_bwd(tag, dm, saved, prm, w_out, layer, g_wout):
    a_re, a_im, log_dt, b_re, b_im, c_re, c_im, d_skip = prm
    u, lam, bs, cs, dvec, hs, ypre, act, vg = saved
    t, d = u.shape
    g, p = a_re.shape
    w = g * p
    nj = g // S5_BLOCK_GROUPS
    cw, sw = S5_BLOCK_GROUPS * S5_GROUP, S5_BLOCK_GROUPS * p
    seg = t // SCAN_SEGMENTS
    tm = _tile(t, 4096)
    dvg = _glu_bwd(f"{tag}_glu_bwd", _to_segments(dm), vg)
    g_wout = _mm_shards_tn(f"{tag}_dwout", act, dvg, layer, g_wout)
    dact = _mm_shards_nt(f"{tag}_dact", dvg, w_out, layer)
    dypre, duskip, dd = _s5_act_bwd(f"{tag}_act_bwd", dact, ypre, u, dvec)
    dh = _mm(f"{tag}_dh", dypre, cs, mode="nt", grid=(2, nj, t // tm), kaxis=None,
             a_blk=(tm, cw), a_map=lambda r, j, i: (i, j),
             b_blk=(None, None, sw, cw), b_map=lambda r, j, i: (r, j, 0, 0),
             o_shape=(2, t, w), o_blk=(None, tm, sw), o_map=lambda r, j, i: (r, i, j))
    dcs = _mm(f"{tag}_dc", hs, dypre, mode="tn", grid=(2, nj, t // tm), kaxis=2,
              a_blk=(None, tm, sw), a_map=lambda r, j, i: (r, i, j),
              b_blk=(tm, cw), b_map=lambda r, j, i: (i, j),
              o_shape=(2, nj, sw, cw), o_blk=(None, None, sw, cw), o_map=lambda r, j, i: (r, j, 0, 0))
    gs, dlam8 = _s5_scan(f"{tag}_scan_bwd", lam, dh.reshape(2, seg, SCAN_SEGMENTS, w),
                         hs.reshape(2, seg, SCAN_SEGMENTS, w))
    gs = gs.reshape(2, t, w)
    du = _mm(f"{tag}_du", gs, bs, mode="nt", grid=(nj, t // tm, 2), kaxis=2,
             a_blk=(None, tm, sw), a_map=lambda j, i, r: (r, i, j),
             b_blk=(None, None, cw, sw), b_map=lambda j, i, r: (r, j, 0, 0),
             o_shape=(t, d), o_blk=(tm, cw), o_map=lambda j, i, r: (i, j))
    dbs = _mm(f"{tag}_db", u, gs, mode="tn", grid=(2, nj, t // tm), kaxis=2,
              a_blk=(tm, cw), a_map=lambda r, j, i: (i, j),
              b_blk=(None, tm, sw), b_map=lambda r, j, i: (r, i, j),
              o_shape=(2, nj, cw, sw), o_blk=(None, None, cw, sw), o_map=lambda r, j, i: (r, j, 0, 0))
    dx = _from_segments(du + duskip)
    dlam = jnp.sum(dlam8, axis=1).reshape(2, g, p)
    small = dict(dlb_re=dlam[0], dlb_im=dlam[1],
                 dbb_re=_blockdiag_in_grad(dbs[0]), dbb_im=_blockdiag_in_grad(dbs[1]),
                 dc_re=_blockdiag_out_grad(dcs[0]), dc_im=-_blockdiag_out_grad(dcs[1]),
                 dd=dd.reshape(g, S5_GROUP))
    return dx, g_wout, small


def _pack(pieces):
    rows = []
    for p in pieces:
        flat = p.reshape(-1).astype(F32)
        n = flat.shape[0]
        rows.append(jnp.pad(flat, (0, -n % V7X_LANES)).reshape(-1, V7X_LANES))
    buf = jnp.concatenate(rows, axis=0)
    return jnp.pad(buf, ((0, -buf.shape[0] % V7X_SUBLANES), (0, 0)))


def _unpack(buf, shapes):
    out, row = [], 0
    for s in shapes:
        n = math.prod(s)
        nr = -(-n // V7X_LANES)
        out.append(buf[row:row + nr].reshape(-1)[:n].reshape(s))
        row += nr
    return out


def kernel(x, ffn1_w_in, ffn1_w_out, ln1_g, ln1_b, lnm_g, lnm_b, ffn2_w_in, ffn2_w_out, ln2_g, ln2_b, fox_w_in, fox_b_f, fox_w_o, s5_a_re, s5_a_im, s5_log_dt, s5_b_re, s5_b_im, s5_c_re, s5_c_im, s5_d, s5_w_out, loss_target, m_ffn1_w_in, m_ffn1_w_out, m_ln1_g, m_ln1_b, m_lnm_g, m_lnm_b, m_ffn2_w_in, m_ffn2_w_out, m_ln2_g, m_ln2_b, m_fox_w_in, m_fox_b_f, m_fox_w_o, m_s5_a_re, m_s5_a_im, m_s5_log_dt, m_s5_b_re, m_s5_b_im, m_s5_c_re, m_s5_c_im, m_s5_d, m_s5_w_out, v_ffn1_w_in, v_ffn1_w_out, v_ln1_g, v_ln1_b, v_lnm_g, v_lnm_b, v_ffn2_w_in, v_ffn2_w_out, v_ln2_g, v_ln2_b, v_fox_w_in, v_fox_b_f, v_fox_w_o, v_s5_a_re, v_s5_a_im, v_s5_log_dt, v_s5_b_re, v_s5_b_im, v_s5_c_re, v_s5_c_im, v_s5_d, v_s5_w_out):
    big_names = ["ffn1_w_in", "ffn1_w_out", "ffn2_w_in", "ffn2_w_out", "fox_w_in", "fox_w_o", "s5_w_out"]
    small_names = ["ln1_g", "ln1_b", "lnm_g", "lnm_b", "ln2_g", "ln2_b", "fox_b_f", "s5_a_re", "s5_a_im", "s5_log_dt",
                   "s5_b_re", "s5_b_im", "s5_c_re", "s5_c_im", "s5_d"]
    out_order = ["ffn1_w_in", "ffn1_w_out", "ln1_g", "ln1_b", "lnm_g", "lnm_b", "ffn2_w_in", "ffn2_w_out", "ln2_g",
                 "ln2_b", "fox_w_in", "fox_b_f", "fox_w_o", "s5_a_re", "s5_a_im", "s5_log_dt", "s5_b_re", "s5_b_im",
                 "s5_c_re", "s5_c_im", "s5_d", "s5_w_out"]
    env = dict(locals())
    w = {n: env[n] for n in out_order}
    mom = {n: env["m_" + n] for n in out_order}
    vel = {n: env["v_" + n] for n in out_order}

    depth, d = ln1_g.shape
    t = x.shape[1]
    alpha = (2.0 * depth) ** 0.25
    x0 = x.reshape(t, d)
    tgt = loss_target.reshape(t, d)

    tix = {n: k for k, n in enumerate(big_names)}
    groups = []
    for i in range(depth):
        j = i // 2
        groups.append([(tix["ffn1_w_in"], i), (tix["ffn1_w_out"], i)])
        mixer = [(tix["fox_w_in"], j), (tix["fox_w_o"], j)] if i % 2 == 0 else [(tix["s5_w_out"], j)]
        groups.append(mixer + [(tix["ffn2_w_in"], i), (tix["ffn2_w_out"], i)])
    sems, bufs = _gather_start("gather_start", [_cast_place(f"cast_{n}", w[n]) for n in big_names], groups)
    full, rows3 = {}, {}

    def arrive(gi, after):
        nonlocal bufs
        bufs = _gather_wait(f"gather_wait_{gi}", bufs, sems[gi][0], sems[gi][1], after, groups[gi])
        full.update(zip(big_names, bufs))
        rows3.update({n: _rows_view(full[n]) for n in ("ffn1_w_out", "ffn2_w_out", "fox_w_o")})

    nh = fox_b_f.shape[1]
    fox_cols = 3 * d + nh
    fox_pad = -(-fox_cols // (5 * V7X_LANES)) * (5 * V7X_LANES)

    def fox_wpad(j):
        wf = full["fox_w_in"][j].transpose(1, 0, 2).reshape(d, fox_cols)
        return jnp.pad(wf, ((0, 0), (0, fox_pad - fox_cols)))

    def s5_params(j):
        return (s5_a_re[j], s5_a_im[j], s5_log_dt[j], s5_b_re[j], s5_b_im[j], s5_c_re[j], s5_c_im[j], s5_d[j])

    saved = []
    h = x0
    for i in range(depth):
        j = i // 2
        arrive(2 * i, h)
        h, s1 = _ffn_fwd(f"l{i}_ffn1", alpha, h, full["ffn1_w_in"], rows3["ffn1_w_out"], i,
                         ln1_g[i:i + 1], ln1_b[i:i + 1])
        arrive(2 * i + 1, h)
        if i % 2 == 0:
            h, xhat_m, rstd_m, sm = _fox_fwd(f"l{i}_fox", alpha, h, fox_wpad(j), fox_b_f[j:j + 1], rows3["fox_w_o"], j,
                                             lnm_g[i:i + 1], lnm_b[i:i + 1])
        else:
            m, sm = _s5_fwd(f"l{i}_s5", h, s5_params(j), full["s5_w_out"], j)
            h, xhat_m, rstd_m = _ln_fwd(f"l{i}_lnm", alpha, h, m, 1.0, lnm_g[i:i + 1], lnm_b[i:i + 1])
        h, s2 = _ffn_fwd(f"l{i}_ffn2", alpha, h, full["ffn2_w_in"], rows3["ffn2_w_out"], i,
                         ln2_g[i:i + 1], ln2_b[i:i + 1])
        saved.append((s1, sm, (xhat_m, rstd_m), s2))
    loss_part = _loss_sum("loss", h, tgt) * (0.5 / d)

    fox_in_names = [f"fox_w_in_l{j}" for j in range(fox_w_in.shape[0])]
    gshape = {n: full[n].shape for n in big_names if n != "fox_w_in"}
    gshape.update({n: (1,) + full["fox_w_in"].shape[1:] for n in fox_in_names})
    gbuf = {n: lax.empty(s, BF16) for n, s in gshape.items()}
    rxbuf = {n: lax.empty((N_PARTS, s[0], s[2] // 2, s[3]), BF16) for n, s in gshape.items()}
    pending = []

    zero = jnp.zeros((), F32)

    def scatter(tag, pairs):
        nonlocal zero
        names = list(dict.fromkeys(n for n, _ in pairs))
        items = [(names.index(n), layer) for n, layer in pairs]
        sem, send, rx, zero = _scatter_start(f"scatter_start_{tag}", [gbuf[n] for n in names],
                                             [rxbuf[n] for n in names], items)
        gbuf.update(zip(names, send))
        rxbuf.update(zip(names, rx))
        pending.append((tag, names, items, sem))

    gsmall = {n: [None] * w[n].shape[0] for n in small_names}
    s5_cot = [None] * s5_a_re.shape[0]
    cot_names = ["dlb_re", "dlb_im", "dbb_re", "dbb_im", "dc_re", "dc_im", "dd"]
    ln_names = ["ln1_g", "ln1_b", "lnm_g", "lnm_b", "ln2_g", "ln2_b"]

    def zrow():
        return jnp.zeros((1, d), F32) + zero

    def ffn_done(which, i):
        def done(g_win, g_wout3):
            gbuf[f"{which}_w_in"], gbuf[f"{which}_w_out"] = g_win, g_wout3.reshape(gshape[f"{which}_w_out"])
            scatter(f"l{i}_{which}", [(f"{which}_w_in", i), (f"{which}_w_out", i)])
            return zero
        return done

    _, _, _, (_, _, _, xhat_top, rstd_top) = saved[depth - 1]
    dz, dg, db = _ln_bwd("top_ln_bwd", [(h, 1.0 / d), (tgt, -1.0 / d)], xhat_top, rstd_top, ln2_g[depth - 1:depth])
    gsmall["ln2_g"][depth - 1], gsmall["ln2_b"][depth - 1] = dg, db
    grad_x = None
    for i in reversed(range(depth)):
        j = i // 2
        s1, sm, (xhat_m, rstd_m), s2 = saved[i]
        dz, dg, db = _ffn_bwd(f"l{i}_ffn2", alpha, dz, s2, full["ffn2_w_in"], rows3["ffn2_w_out"], i,
                              gbuf["ffn2_w_in"], _rows_view(gbuf["ffn2_w_out"]), ffn_done("ffn2", i),
                              (xhat_m, rstd_m, lnm_g[i:i + 1]), zrow())
        gsmall["lnm_g"][i], gsmall["lnm_b"][i] = dg, db
        ln1 = (s1[3], s1[4], ln1_g[i:i + 1])
        if i % 2 == 0:
            def fox_done(g_wo3, d_wpad, j=j, i=i):
                gbuf["fox_w_o"] = g_wo3.reshape(gshape["fox_w_o"])
                gbuf[fox_in_names[j]] = d_wpad[:, :fox_cols].reshape(d, N_CHIPS, -1).transpose(1, 0, 2)[None].astype(BF16)
                scatter(f"l{i}_fox", [("fox_w_o", j), (fox_in_names[j], 0)])
                return zero

            (dz, dg, db), gsmall["fox_b_f"][j] = _fox_bwd(
                f"l{i}_fox", alpha, dz, sm, fox_b_f[j:j + 1], rows3["fox_w_o"], j,
                _rows_view(gbuf["fox_w_o"]), fox_done, ln1)
        else:
            dx, gbuf["s5_w_out"], s5_cot[j] = _s5_bwd(f"l{i}_s5", dz, sm, s5_params(j), full["s5_w_out"], j,
                                                      gbuf["s5_w_out"])
            scatter(f"l{i}_s5", [("s5_w_out", j)])
            dz, dg, db = _ln_bwd(f"l{i}_ln1_bwd", [(dz, alpha), (dx, 1.0)], ln1[0], ln1[1], ln1[2] + zero)
        gsmall["ln1_g"][i], gsmall["ln1_b"][i] = dg, db
        if i > 0:
            below = saved[i - 1][3]
            dz, dg, db = _ffn_bwd(f"l{i}_ffn1", alpha, dz, s1, full["ffn1_w_in"], rows3["ffn1_w_out"], i,
                                  gbuf["ffn1_w_in"], _rows_view(gbuf["ffn1_w_out"]), ffn_done("ffn1", i),
                                  (below[3], below[4], ln2_g[i - 1:i]), zrow())
            gsmall["ln2_g"][i - 1], gsmall["ln2_b"][i - 1] = dg, db
        else:
            pieces = [loss_part + zero] + [jnp.concatenate(gsmall[n], axis=0) for n in ln_names + ["fox_b_f"]]
            pieces += [jnp.stack([s5_cot[k][n] for k in range(len(s5_cot))]) for n in cot_names]
            mine = _pack(pieces)
            share_sem, mine, land, zero = _share_start("small_share_start", mine, lax.empty((N_DEV,) + mine.shape, F32))
            grad_x = _ffn_bwd(f"l{i}_ffn1", alpha, dz, s1, full["ffn1_w_in"], rows3["ffn1_w_out"], i,
                              gbuf["ffn1_w_in"], _rows_view(gbuf["ffn1_w_out"]), ffn_done("ffn1", i),
                              (None, None, jnp.zeros((1, d), F32)), zrow()).reshape(x.shape)

    shapes = [p.shape for p in pieces]
    mine, land = _share_wait("small_share_wait", mine, land, share_sem[0], share_sem[1], grad_x)
    summed = _unpack(_sum_devices("small_sum", mine, land), shapes)
    loss = summed[0].reshape(())
    gs_final = dict(zip(ln_names + ["fox_b_f"], summed[1:8]))
    cot = dict(zip(cot_names, summed[8:]))
    prm_names = ["s5_a_re", "s5_a_im", "s5_log_dt", "s5_b_re", "s5_b_im"]
    _, disc_vjp = jax.vjp(jax.vmap(_s5_discretise), *[w[n] for n in prm_names])
    for n, gval in zip(prm_names, disc_vjp((cot["dlb_re"], cot["dlb_im"], cot["dbb_re"], cot["dbb_im"]))):
        gs_final[n] = gval
    gs_final["s5_c_re"], gs_final["s5_c_im"], gs_final["s5_d"] = cot["dc_re"], cot["dc_im"], cot["dd"]

    grads, deltas, new_m, new_v = {}, {}, {}, {}
    small_shapes = [w[n].shape for n in small_names]
    for n in small_names:
        grads[n] = gs_final[n].reshape(w[n].shape)
    packed = [_pack([src[n] for n in small_names]) for src in (w, grads, mom, vel)]
    small_out = _adamw("adamw_small", *packed)
    for dst, buf in zip((deltas, new_m, new_v), small_out):
        for n, val in zip(small_names, _unpack(buf, small_shapes)):
            dst[n] = val

    for tag, names, items, sem in pending:
        send, rx = _scatter_wait(f"scatter_wait_{tag}", [gbuf[n] for n in names], [rxbuf[n] for n in names],
                                 sem[0], sem[1], small_out[0], items)
        gbuf.update(zip(names, send))
        rxbuf.update(zip(names, rx))
    half = {n: _chip_sum(f"grad_chip_sum_{n}", gbuf[n], rxbuf[n]) for n in gshape}
    half["fox_w_in"] = jnp.concatenate([half[n] for n in fox_in_names], axis=0)
    halves = [half[n] for n in big_names]
    theirs = _send_half("grad_send_half", halves)
    for n, mine_h, their_h in zip(big_names, halves, theirs):
        grads[n], deltas[n], new_m[n], new_v[n] = _adamw_join(f"adamw_{n}", w[n], mine_h, their_h, mom[n], vel[n])
    return (loss, grad_x, *[grads[n] for n in out_order], *[deltas[n] for n in out_order],
            *[new_m[n] for n in out_order], *[new_v[n] for n in out_order])
```

```python
import functools
import math

import jax
import jax.numpy as jnp
from jax import lax
from jax.experimental import pallas as pl
from jax.experimental.pallas import tpu as pltpu

F32 = jnp.float32
BF16 = jnp.bfloat16
LN_EPS = 1e-5
NEG_INF = -1e30
ADAM_LR = 0.001
ADAM_B1 = 0.9
ADAM_B2 = 0.999
ADAM_EPS = 1e-08
ADAM_WD = 0.01
ADAM_STEP = 10
S5_GROUP = 16
SCAN_SEGMENTS = 32
ATTN_BLOCK = 1024
V7X_SUBLANES = 8
V7X_LANES = 128
VMEM_LIMIT = 56 * 1024 * 1024
N_CHIPS = 4
N_DEV = 8
MESH = pl.DeviceIdType.MESH
ANY = pl.BlockSpec(memory_space=pl.ANY)


def _cp(n_grid, kaxis=None):
    sem = tuple("arbitrary" if (kaxis is None or i == kaxis) else "parallel" for i in range(n_grid))
    return pltpu.CompilerParams(dimension_semantics=sem, vmem_limit_bytes=VMEM_LIMIT)


def _tile(n, pref, mult=V7X_SUBLANES):
    if n <= pref:
        return n
    for t in range(pref, 0, -1):
        if n % t == 0 and t % mult == 0:
            return t
    return n


_CONTRACT = {"nn": ((1,), (0,)), "nt": ((1,), (1,)), "tn": ((0,), (0,))}


def _mm(name, a, b, *, mode, grid, kaxis, a_blk, a_map, b_blk, b_map, o_shape, o_blk, o_map, o_dtype=F32, scale=None,
        into=None):
    nk = 1 if kaxis is None else grid[kaxis]
    assert kaxis is None or kaxis == len(grid) - 1
    dims = (_CONTRACT[mode], ((), ()))
    use_acc = nk > 1 and o_dtype != F32
    acc_shape = tuple(d for d in o_blk if d is not None)

    def body(a_ref, b_ref, *rest):
        o_ref, scratch = (rest[1], rest[2:]) if into is not None else (rest[0], rest[1:])
        p = lax.dot_general(a_ref[...].astype(BF16), b_ref[...].astype(BF16), dims, preferred_element_type=F32)
        if nk == 1:
            if scale is not None:
                p = p * scale
            o_ref[...] = p.astype(o_dtype)
            return
        acc = scratch[0] if use_acc else o_ref
        k = pl.program_id(kaxis)

        @pl.when(k == 0)
        def _():
            acc[...] = p

        @pl.when(k > 0)
        def _():
            acc[...] += p

        if use_acc or scale is not None:
            @pl.when(k == nk - 1)
            def _():
                r = acc[...]
                if scale is not None:
                    r = r * scale
                o_ref[...] = r.astype(o_dtype)

    in_specs = [pl.BlockSpec(a_blk, a_map), pl.BlockSpec(b_blk, b_map)]
    args = [a, b]
    if into is not None:
        assert into.shape == tuple(o_shape) and into.dtype == o_dtype
        in_specs.append(ANY)
        args.append(into)
    return pl.pallas_call(
        body, grid=grid, name=name, in_specs=in_specs,
        out_specs=pl.BlockSpec(o_blk, o_map),
        out_shape=jax.ShapeDtypeStruct(o_shape, o_dtype),
        input_output_aliases={2: 0} if into is not None else {},
        scratch_shapes=[pltpu.VMEM(acc_shape, F32)] if use_acc else [],
        compiler_params=_cp(len(grid), kaxis),
    )(*args)


def _mm_shards_nn(name, a, wall, layer, o_dtype):
    t, k = a.shape
    _, s, _, n = wall.shape
    tm = _tile(t, 512)
    return _mm(name, a, wall, mode="nn", grid=(s, t // tm), kaxis=None,
               a_blk=(tm, k), a_map=lambda j, i: (i, 0),
               b_blk=(None, None, k, n), b_map=lambda j, i: (layer, j, 0, 0),
               o_shape=(t, s * n), o_blk=(tm, n), o_map=lambda j, i: (i, j), o_dtype=o_dtype)


def _mm_shards_nt(name, g, wall, layer):
    t = g.shape[0]
    _, s, k, n = wall.shape
    tm = _tile(t, 512)
    return _mm(name, g, wall, mode="nt", grid=(t // tm, s), kaxis=1,
               a_blk=(tm, n), a_map=lambda i, kk: (i, kk),
               b_blk=(None, None, k, n), b_map=lambda i, kk: (layer, kk, 0, 0),
               o_shape=(t, k), o_blk=(tm, k), o_map=lambda i, kk: (i, 0))


def _mm_shards_tn(name, a, g, layer, into):
    t, k = a.shape
    _, s, _, n = into.shape
    tk = _tile(t, 512)
    return _mm(name, a, g, mode="tn", grid=(s, t // tk), kaxis=1,
               a_blk=(tk, k), a_map=lambda j, kk: (kk, 0),
               b_blk=(tk, n), b_map=lambda j, kk: (kk, j),
               o_shape=into.shape, o_blk=(None, None, k, n), o_map=lambda j, kk: (layer, j, 0, 0),
               o_dtype=into.dtype, into=into)


def _mm_nn(name, a, w, o_dtype=F32, tn=None):
    t, k = a.shape
    n = w.shape[1]
    tm = _tile(t, 512)
    tn = n if tn is None else tn
    return _mm(name, a, w, mode="nn", grid=(n // tn, t // tm), kaxis=None,
               a_blk=(tm, k), a_map=lambda j, i: (i, 0),
               b_blk=(k, tn), b_map=lambda j, i: (0, j),
               o_shape=(t, n), o_blk=(tm, tn), o_map=lambda j, i: (i, j), o_dtype=o_dtype)


def _mm_nt(name, g, w, layer=None, o_dtype=F32):
    t, k = g.shape
    n = w.shape[-2]
    tm = _tile(t, 512)
    b_blk, b_map = ((n, k), lambda i: (0, 0)) if layer is None else ((None, n, k), lambda i: (layer, 0, 0))
    return _mm(name, g, w, mode="nt", grid=(t // tm,), kaxis=None,
               a_blk=(tm, k), a_map=lambda i: (i, 0), b_blk=b_blk, b_map=b_map,
               o_shape=(t, n), o_blk=(tm, n), o_map=lambda i: (i, 0), o_dtype=o_dtype)


def _mm_tn(name, a, g, tm=None, tn=None, scale=None, layer=None, into=None):
    t, m = a.shape
    n = g.shape[1]
    tk = _tile(t, 512)
    tm = m if tm is None else tm
    tn = n if tn is None else tn
    if layer is None:
        o_shape, o_blk, o_map, o_dtype = (m, n), (tm, tn), lambda i, j, kk: (i, j), F32
    else:
        o_shape, o_blk, o_map, o_dtype = into.shape, (None, tm, tn), lambda i, j, kk: (layer, i, j), into.dtype
    return _mm(name, a, g, mode="tn", grid=(m // tm, n // tn, t // tk), kaxis=2,
               a_blk=(tk, tm), a_map=lambda i, j, kk: (kk, i),
               b_blk=(tk, tn), b_map=lambda i, j, kk: (kk, j),
               o_shape=o_shape, o_blk=o_blk, o_map=o_map, o_dtype=o_dtype, scale=scale, into=into)


def _sigmoid(x):
    return 1.0 / (1.0 + jnp.exp(-x))


def _rows_call(name, body, t, tm, ins, in_cols, outs, acc_outs=()):
    in_specs = []
    for x, c in zip(ins, in_cols):
        if x.shape[0] == 1:
            in_specs.append(pl.BlockSpec((1, c), lambda i: (0, 0)))
        else:
            in_specs.append(pl.BlockSpec((tm, c), lambda i: (i, 0)))
    out_specs = [pl.BlockSpec((tm, s.shape[1]), lambda i: (i, 0)) for s in outs]
    out_specs += [pl.BlockSpec((1, s.shape[1]), lambda i: (0, 0)) for s in acc_outs]
    return pl.pallas_call(
        body, grid=(t // tm,), name=name, in_specs=in_specs, out_specs=out_specs,
        out_shape=list(outs) + list(acc_outs), compiler_params=_cp(1),
    )(*ins)


def _ln_fwd(name, alpha, x, r, coef, g, b):
    t, d = x.shape
    tm = _tile(t, 256)

    def body(x_ref, r_ref, g_ref, b_ref, y_ref, xh_ref, rs_ref):
        z = alpha * x_ref[...] + coef * r_ref[...]
        mu = jnp.mean(z, axis=-1, keepdims=True)
        zc = z - mu
        var = jnp.mean(zc * zc, axis=-1, keepdims=True)
        rstd = lax.rsqrt(var + LN_EPS)
        xh = zc * rstd
        y_ref[...] = xh * g_ref[...] + b_ref[...]
        xh_ref[...] = xh
        rs_ref[...] = rstd

    sd = jax.ShapeDtypeStruct
    return _rows_call(name, body, t, tm, [x, r, g, b], [d, d, d, d],
                      [sd((t, d), F32), sd((t, d), F32), sd((t, 1), F32)])


def _ln_bwd(name, terms, xhat, rstd, g):
    t, d = xhat.shape
    tm = _tile(t, 256)
    n = len(terms)
    coefs = [c for _, c in terms]

    def body(*refs):
        t_refs = refs[:n]
        xh_ref, rs_ref, g_ref, dz_ref, dg_ref, db_ref = refs[n:]
        dy = coefs[0] * t_refs[0][...]
        for c, r in zip(coefs[1:], t_refs[1:]):
            dy = dy + c * r[...]
        xh = xh_ref[...]
        dxh = dy * g_ref[...]
        m1 = jnp.mean(dxh, axis=-1, keepdims=True)
        m2 = jnp.mean(dxh * xh, axis=-1, keepdims=True)
        dz_ref[...] = rs_ref[...] * (dxh - m1 - xh * m2)
        pg = jnp.sum(dy * xh, axis=0, keepdims=True)
        pb = jnp.sum(dy, axis=0, keepdims=True)
        i = pl.program_id(0)

        @pl.when(i == 0)
        def _():
            dg_ref[...] = pg
            db_ref[...] = pb

        @pl.when(i > 0)
        def _():
            dg_ref[...] += pg
            db_ref[...] += pb

    sd = jax.ShapeDtypeStruct
    arrs = [a for a, _ in terms] + [xhat, rstd, g]
    cols = [d] * n + [d, 1, d]
    return _rows_call(name, body, t, tm, arrs, cols, [sd((t, d), F32)], [sd((1, d), F32), sd((1, d), F32)])


def _loss_sum(name, y, tgt):
    t, d = y.shape
    tm = _tile(t, 256)

    def body(y_ref, t_ref, o_ref):
        e = y_ref[...] - t_ref[...]
        s = jnp.sum(jnp.sum(e * e, axis=1, keepdims=True), axis=0, keepdims=True)
        i = pl.program_id(0)

        @pl.when(i == 0)
        def _():
            o_ref[...] = s

        @pl.when(i > 0)
        def _():
            o_ref[...] += s

    return _rows_call(name, body, t, tm, [y, tgt], [d, d], [], [jax.ShapeDtypeStruct((1, 1), F32)])[0]


def _ffn_in(name, x, wall, layer):
    t, k = x.shape
    n = wall.shape[3]
    tm = _tile(t, 512)

    def body(x_ref, wg_ref, wu_ref, h_ref, a_ref):
        xb = x_ref[...].astype(BF16)
        g = lax.dot_general(xb, wg_ref[...], _NN, preferred_element_type=F32)
        u = lax.dot_general(xb, wu_ref[...], _NN, preferred_element_type=F32)
        h_ref[0] = g.astype(BF16)
        h_ref[1] = u.astype(BF16)
        a_ref[...] = (g * _sigmoid(g) * u).astype(BF16)

    return pl.pallas_call(
        body, grid=(2, t // tm), name=name,
        in_specs=[pl.BlockSpec((tm, k), lambda j, i: (i, 0)),
                  pl.BlockSpec((None, None, k, n), lambda j, i: (layer, j, 0, 0)),
                  pl.BlockSpec((None, None, k, n), lambda j, i: (layer, 2 + j, 0, 0))],
        out_specs=[pl.BlockSpec((2, tm, n), lambda j, i: (0, i, j)), pl.BlockSpec((tm, n), lambda j, i: (i, j))],
        out_shape=[jax.ShapeDtypeStruct((2, t, 2 * n), BF16), jax.ShapeDtypeStruct((t, 2 * n), BF16)],
        compiler_params=_cp(2),
    )(x, wall, wall)


def _ffn_da(name, dz, w3, layer, h, zrow):
    t, k = dz.shape
    f = w3.shape[1]
    n = f // 2
    tm = _tile(t, 512)

    def body(dz_ref, z_ref, w_ref, g_ref, u_ref, dh_ref):
        d = 0.5 * lax.dot_general((dz_ref[...] + z_ref[...]).astype(BF16), w_ref[...], _NT, preferred_element_type=F32)
        g = g_ref[...].astype(F32)
        u = u_ref[...].astype(F32)
        sg = _sigmoid(g)
        dh_ref[0] = (d * u * sg * (1.0 + g * (1.0 - sg))).astype(BF16)
        dh_ref[1] = (d * g * sg).astype(BF16)

    return pl.pallas_call(
        body, grid=(2, t // tm), name=name,
        in_specs=[pl.BlockSpec((tm, k), lambda j, i: (i, 0)),
                  pl.BlockSpec((1, k), lambda j, i: (0, 0)),
                  pl.BlockSpec((None, n, k), lambda j, i: (layer, j, 0)),
                  pl.BlockSpec((None, tm, n), lambda j, i: (0, i, j)),
                  pl.BlockSpec((None, tm, n), lambda j, i: (1, i, j))],
        out_specs=pl.BlockSpec((2, tm, n), lambda j, i: (0, i, j)),
        out_shape=jax.ShapeDtypeStruct((2, t, f), BF16),
        compiler_params=_cp(2),
    )(dz, zrow, w3, h, h)


def _mm_ln(name, alpha, coef, a, w3, layer, x, g, b):
    t, k = a.shape
    d = w3.shape[2]
    tm = _tile(t, 512)

    def body(a_ref, w_ref, x_ref, g_ref, b_ref, y_ref, xh_ref, rs_ref):
        f = lax.dot_general(a_ref[...].astype(BF16), w_ref[...], _NN, preferred_element_type=F32)
        z = alpha * x_ref[...] + coef * f
        mu = jnp.mean(z, axis=-1, keepdims=True)
        zc = z - mu
        var = jnp.mean(zc * zc, axis=-1, keepdims=True)
        rstd = lax.rsqrt(var + LN_EPS)
        xh = zc * rstd
        y_ref[...] = xh * g_ref[...] + b_ref[...]
        xh_ref[...] = xh
        rs_ref[...] = rstd

    row = lambda c: pl.BlockSpec((tm, c), lambda i: (i, 0))
    vec = pl.BlockSpec((1, d), lambda i: (0, 0))
    sd = jax.ShapeDtypeStruct
    return pl.pallas_call(
        body, grid=(t // tm,), name=name,
        in_specs=[row(k), pl.BlockSpec((None, k, d), lambda i: (layer, 0, 0)), row(d), vec, vec],
        out_specs=[row(d), row(d), row(1)],
        out_shape=[sd((t, d), F32), sd((t, d), F32), sd((t, 1), F32)],
        compiler_params=_cp(1),
    )(a, w3, x, g, b)


_GELU_C = math.sqrt(2.0 / math.pi)


def _s5_act_fwd(name, ych, u, dvec):
    t, d = u.shape
    tm = _tile(t, 256)

    def body(y_ref, u_ref, d_ref, p_ref, a_ref):
        y = y_ref[...] + d_ref[...] * u_ref[...]
        p_ref[...] = y
        a_ref[...] = (0.5 * y * (1.0 + jnp.tanh(_GELU_C * (y + 0.044715 * y * y * y)))).astype(BF16)

    sd = jax.ShapeDtypeStruct
    return _rows_call(name, body, t, tm, [ych, u, dvec], [d, d, d], [sd((t, d), F32), sd((t, d), BF16)])


def _s5_act_bwd(name, dact, ypre, u, dvec):
    t, d = u.shape
    tm = _tile(t, 256)

    def body(da_ref, y_ref, u_ref, d_ref, dy_ref, ds_ref, dd_ref):
        y = y_ref[...]
        th = jnp.tanh(_GELU_C * (y + 0.044715 * y * y * y))
        dg = 0.5 * (1.0 + th) + 0.5 * y * (1.0 - th * th) * _GELU_C * (1.0 + 3.0 * 0.044715 * y * y)
        dy = da_ref[...] * dg
        dy_ref[...] = dy
        ds_ref[...] = dy * d_ref[...]
        pd = jnp.sum(dy * u_ref[...], axis=0, keepdims=True)
        i = pl.program_id(0)

        @pl.when(i == 0)
        def _():
            dd_ref[...] = pd

        @pl.when(i > 0)
        def _():
            dd_ref[...] += pd

    sd = jax.ShapeDtypeStruct
    return _rows_call(name, body, t, tm, [dact, ypre, u, dvec], [d, d, d, d],
                      [sd((t, d), F32), sd((t, d), F32)], [sd((1, d), F32)])


def _glu_fwd(name, vg):
    t, d2 = vg.shape
    d = d2 // 2
    tm = _tile(t, 256)

    def body(vg_ref, m_ref):
        m_ref[...] = vg_ref[:, :d] * _sigmoid(vg_ref[:, d:])

    return _rows_call(name, body, t, tm, [vg], [d2], [jax.ShapeDtypeStruct((t, d), F32)])[0]


def _glu_bwd(name, dm, vg):
    t, d2 = vg.shape
    d = d2 // 2
    tm = _tile(t, 256)

    def body(dm_ref, vg_ref, o_ref):
        sg = _sigmoid(vg_ref[:, d:])
        g = dm_ref[...]
        o_ref[:, :d] = (g * sg).astype(BF16)
        o_ref[:, d:] = (g * vg_ref[:, :d] * sg * (1.0 - sg)).astype(BF16)

    return _rows_call(name, body, t, tm, [dm, vg], [d, d2], [jax.ShapeDtypeStruct((t, d2), BF16)])[0]


def _adamw(name, w, g, m, v):
    r, c = w.shape
    tr = _tile(r, max(V7X_SUBLANES, (1 << 20) // (4 * c) // V7X_SUBLANES * V7X_SUBLANES))

    def body(w_ref, g_ref, m_ref, v_ref, d_ref, nm_ref, nv_ref):
        gg = g_ref[...]
        nm = ADAM_B1 * m_ref[...] + (1.0 - ADAM_B1) * gg
        nv = ADAM_B2 * v_ref[...] + (1.0 - ADAM_B2) * (gg * gg)
        m_hat = nm / (1.0 - ADAM_B1 ** ADAM_STEP)
        v_hat = nv / (1.0 - ADAM_B2 ** ADAM_STEP)
        d_ref[...] = -ADAM_LR * (m_hat / (jnp.sqrt(v_hat) + ADAM_EPS) + ADAM_WD * w_ref[...])
        nm_ref[...] = nm
        nv_ref[...] = nv

    sd = jax.ShapeDtypeStruct((r, c), F32)
    return _rows_call(name, body, r, tr, [w, g, m, v], [c] * 4, [sd, sd, sd])


def _my_shard():
    return 2 * lax.axis_index("x") + lax.axis_index("y")


def _my_core():
    return lax.axis_index("c")


def _adamw_join(name, w, mine, theirs, m, v):
    nl, r, c = w.shape
    h = r // 2
    tr = _tile(h, max(V7X_SUBLANES, (1 << 20) // (4 * c) // V7X_SUBLANES * V7X_SUBLANES))
    nb = h // tr

    def body(w_ref, a_ref, b_ref, m_ref, v_ref, g_ref, d_ref, nm_ref, nv_ref):
        gg = jnp.where(pl.program_id(1) == _my_core(), a_ref[...], b_ref[...])
        nm = ADAM_B1 * m_ref[...] + (1.0 - ADAM_B1) * gg
        nv = ADAM_B2 * v_ref[...] + (1.0 - ADAM_B2) * (gg * gg)
        m_hat = nm / (1.0 - ADAM_B1 ** ADAM_STEP)
        v_hat = nv / (1.0 - ADAM_B2 ** ADAM_STEP)
        g_ref[...] = gg
        d_ref[...] = -ADAM_LR * (m_hat / (jnp.sqrt(v_hat) + ADAM_EPS) + ADAM_WD * w_ref[...])
        nm_ref[...] = nm
        nv_ref[...] = nv

    full = pl.BlockSpec((None, tr, c), lambda l, hf, i: (l, hf * nb + i, 0))
    sd = jax.ShapeDtypeStruct((nl, r, c), F32)
    return pl.pallas_call(
        body, name=name, grid=(nl, 2, nb),
        in_specs=[full,
                  pl.BlockSpec((None, tr, c), lambda l, hf, i: (l, jnp.where(hf == _my_core(), i, 0), 0)),
                  pl.BlockSpec((None, tr, c), lambda l, hf, i: (l, jnp.where(hf == _my_core(), 0, i), 0)),
                  full, full],
        out_specs=[full, full, full, full],
        out_shape=[sd, sd, sd, sd],
        compiler_params=_cp(3),
    )(w, mine, theirs, m, v)


def _split3(x):
    hi = x.astype(BF16)
    r1 = x - hi.astype(F32)
    mid = r1.astype(BF16)
    lo = (r1 - mid.astype(F32)).astype(BF16)
    return hi, mid, lo


def _tri_sum(tri, x):
    dims = (((1,), (0,)), ((), ()))
    hi, mid, lo = _split3(x)
    out = lax.dot_general(tri, lo, dims, preferred_element_type=F32)
    out = out + lax.dot_general(tri, mid, dims, preferred_element_type=F32)
    return out + lax.dot_general(tri, hi, dims, preferred_element_type=F32)


def _fox_cumsum(name, fl, bf):
    t, h = fl.shape
    tb = _tile(t, 512)

    def body(fl_ref, bf_ref, c_ref, carry):
        i = pl.program_id(0)

        @pl.when(i == 0)
        def _():
            carry[...] = jnp.zeros_like(carry)

        x = fl_ref[...] + bf_ref[...]
        lf = jnp.minimum(x, 0.0) - jnp.log(1.0 + jnp.exp(-jnp.abs(x)))
        row = lax.broadcasted_iota(jnp.int32, (tb, tb), 0)
        col = lax.broadcasted_iota(jnp.int32, (tb, tb), 1)
        tri = jnp.where(row >= col, 1.0, 0.0).astype(BF16)
        c_ref[...] = _tri_sum(tri, lf) + carry[...]
        carry[...] += jnp.sum(lf, axis=0, keepdims=True)

    return pl.pallas_call(
        body, grid=(t // tb,), name=name,
        in_specs=[pl.BlockSpec((tb, h), lambda i: (i, 0)), pl.BlockSpec((1, h), lambda i: (0, 0))],
        out_specs=pl.BlockSpec((tb, h), lambda i: (i, 0)),
        out_shape=jax.ShapeDtypeStruct((t, h), F32),
        scratch_shapes=[pltpu.VMEM((1, h), F32)], compiler_params=_cp(1),
    )(fl, bf)


def _fox_cumsum_bwd(name, dcum, fl, bf):
    t, h = fl.shape
    tb = _tile(t, 512)
    nb = t // tb

    def body(dc_ref, fl_ref, bf_ref, df_ref, db_ref, carry):
        i = pl.program_id(0)

        @pl.when(i == 0)
        def _():
            carry[...] = jnp.zeros_like(carry)

        dc = dc_ref[...]
        row = lax.broadcasted_iota(jnp.int32, (tb, tb), 0)
        col = lax.broadcasted_iota(jnp.int32, (tb, tb), 1)
        tri = jnp.where(row <= col, 1.0, 0.0).astype(BF16)
        dlf = _tri_sum(tri, dc) + carry[...]
        carry[...] += jnp.sum(dc, axis=0, keepdims=True)
        x = fl_ref[...] + bf_ref[...]
        df = dlf / (1.0 + jnp.exp(x))
        df_ref[...] = df
        pb = jnp.sum(df, axis=0, keepdims=True)

        @pl.when(i == 0)
        def _():
            db_ref[...] = pb

        @pl.when(i > 0)
        def _():
            db_ref[...] += pb

    rev = lambda i: (nb - 1 - i, 0)
    return pl.pallas_call(
        body, grid=(nb,), name=name,
        in_specs=[pl.BlockSpec((tb, h), rev), pl.BlockSpec((tb, h), rev), pl.BlockSpec((1, h), lambda i: (0, 0))],
        out_specs=[pl.BlockSpec((tb, h), rev), pl.BlockSpec((1, h), lambda i: (0, 0))],
        out_shape=[jax.ShapeDtypeStruct((t, h), F32), jax.ShapeDtypeStruct((1, h), F32)],
        scratch_shapes=[pltpu.VMEM((1, h), F32)], compiler_params=_cp(1),
    )(dcum, fl, bf)


_NT = (((1,), (1,)), ((), ()))
_TN = (((0,), (0,)), ((), ()))
_NN = (((1,), (0,)), ((), ()))


def _causal_mask(s, r0):
    row = lax.broadcasted_iota(jnp.int32, s.shape, 0) + r0
    col = lax.broadcasted_iota(jnp.int32, s.shape, 1)
    return jnp.where(col <= row, s, NEG_INF)


def _first_head_lanes(hd):
    return lax.broadcasted_iota(jnp.int32, (1, 2 * hd), 1) < hd


def _attn_fwd(name, qkv, ccol, crow, nh):
    nb, tb, d3 = qkv.shape
    d = d3 // 3
    hd = d // nh
    lanes = 2 * hd
    assert lanes == V7X_LANES
    scale = 1.0 / math.sqrt(hd)

    def body(q_ref, k_ref, v_ref, cc_ref, cr_ref, o_ref, lse_ref):
        i = pl.program_id(1)
        first = _first_head_lanes(hd)
        q = q_ref[...] * scale
        res = []
        for hh in (0, 1):
            qh = jnp.where(first if hh == 0 else jnp.logical_not(first), q, jnp.zeros_like(q))
            cc = cc_ref[:, hh:hh + 1]

            def tile(j, carry, r0=0, nr=tb, ncol=tb, masked=False, qh=qh, cc=cc, hh=hh):
                m, l, acc = carry
                s = lax.dot_general(qh[r0:r0 + nr], k_ref[j, 0:ncol, :], _NT, preferred_element_type=F32)
                s = s + cc[r0:r0 + nr] - cr_ref[j][hh:hh + 1, 0:ncol]
                if masked:
                    s = _causal_mask(s, r0)
                m_new = jnp.maximum(m, jnp.max(s, axis=1, keepdims=True))
                p = jnp.exp(s - m_new)
                a = jnp.exp(m - m_new)
                l = a * l + jnp.sum(p, axis=1, keepdims=True)
                acc = a * acc + lax.dot_general(p.astype(BF16), v_ref[j, 0:ncol, :], _NN, preferred_element_type=F32)
                return m_new, l, acc

            init = (jnp.full((tb, 1), NEG_INF, F32), jnp.zeros((tb, 1), F32), jnp.zeros((tb, lanes), F32))
            m, l, acc = tile(i, lax.fori_loop(0, i, tile, init), masked=True)
            res.append((acc / l, m + jnp.log(l)))
        o_ref[...] = jnp.where(first, res[0][0], res[1][0])
        lse_ref[:, 0:1] = res[0][1]
        lse_ref[:, 1:2] = res[1][1]

    kb, vb = d // lanes, 2 * d // lanes
    return pl.pallas_call(
        body, grid=(nh // 2, nb), name=name,
        in_specs=[pl.BlockSpec((None, tb, lanes), lambda h, i: (i, 0, h)),
                  pl.BlockSpec((nb, tb, lanes), lambda h, i: (0, 0, kb + h)),
                  pl.BlockSpec((nb, tb, lanes), lambda h, i: (0, 0, vb + h)),
                  pl.BlockSpec((None, None, tb, 2), lambda h, i: (h, i, 0, 0)),
                  pl.BlockSpec((None, nb, 2, tb), lambda h, i: (h, 0, 0, 0))],
        out_specs=[pl.BlockSpec((None, tb, lanes), lambda h, i: (i, 0, h)),
                   pl.BlockSpec((None, None, tb, 2), lambda h, i: (h, i, 0, 0))],
        out_shape=[jax.ShapeDtypeStruct((nb, tb, d), F32), jax.ShapeDtypeStruct((nh // 2, nb, tb, 2), F32)],
        compiler_params=_cp(2),
    )(qkv, qkv, qkv, ccol, crow)


def _attn_bwd(name, qkv, ccol, crow, o, lse, do, nh):
    nb, tb, d3 = qkv.shape
    d = d3 // 3
    hd = d // nh
    lanes = 2 * hd
    hb = tb // 2
    scale = 1.0 / math.sqrt(hd)

    def body(q_ref, k_ref, v_ref, cc_ref, cr_ref, o_ref, lse_ref, do_ref, dq_ref, dk_ref, dv_ref, dr_ref, dc_ref, dq_acc):
        j = pl.program_id(1)

        @pl.when(j == 0)
        def _():
            dq_acc[...] = jnp.zeros_like(dq_acc)
            dr_ref[...] = jnp.zeros_like(dr_ref)

        first = _first_head_lanes(hd)
        kj = k_ref[...]
        vj = v_ref[...]
        dk = jnp.zeros((tb, lanes), F32)
        dv = jnp.zeros((tb, lanes), F32)
        for hh in (0, 1):
            mine = first if hh == 0 else jnp.logical_not(first)
            cr = cr_ref[hh:hh + 1, :]

            def tile(i, carry, r0=0, nr=tb, ncol=tb, masked=False, mine=mine, cr=cr, hh=hh):
                dk, dv, dc = carry
                rows = pl.ds(r0, nr)
                qi = q_ref[i, rows, :] * scale
                qh = jnp.where(mine, qi, jnp.zeros_like(qi))
                doh = jnp.where(mine, do_ref[i, rows, :], 0.0)
                dob = doh.astype(BF16)
                di = jnp.sum(doh * o_ref[i, rows, :], axis=1, keepdims=True)
                s = lax.dot_general(qh, kj[:ncol], _NT, preferred_element_type=F32)
                s = s + cc_ref[i, rows, hh:hh + 1] - cr[:, :ncol]
                if masked:
                    s = _causal_mask(s, r0)
                p = jnp.exp(s - lse_ref[i, rows, hh:hh + 1])
                dp = lax.dot_general(dob, vj[:ncol], _NT, preferred_element_type=F32)
                ds = p * (dp - di)
                dsb = ds.astype(BF16)
                dvc = lax.dot_general(p.astype(BF16), dob, _TN, preferred_element_type=F32)
                dkc = lax.dot_general(dsb, qh, _TN, preferred_element_type=F32)
                dcc = jnp.sum(ds, axis=0, keepdims=True)
                if ncol < tb:
                    dvc = jnp.concatenate([dvc, jnp.zeros((tb - ncol, lanes), F32)], axis=0)
                    dkc = jnp.concatenate([dkc, jnp.zeros((tb - ncol, lanes), F32)], axis=0)
                    dcc = jnp.concatenate([dcc, jnp.zeros((1, tb - ncol), F32)], axis=1)
                dq = lax.dot_general(dsb, kj[:ncol], _NN, preferred_element_type=F32) * scale
                dq_acc[i, rows, :] += jnp.where(mine, dq, 0.0)
                dr_ref[i, rows, hh:hh + 1] += jnp.sum(ds, axis=1, keepdims=True)
                return dk + dkc, dv + dvc, dc + dcc

            carry = tile(j, (dk, dv, jnp.zeros((1, tb), F32)), 0, hb, hb, True)
            carry = tile(j, carry, hb, hb, tb, True)
            dk, dv, dc = lax.fori_loop(j + 1, nb, tile, carry)
            dc_ref[hh:hh + 1, :] = dc
        dk_ref[...] = dk.astype(BF16)
        dv_ref[...] = dv.astype(BF16)

        @pl.when(j == nb - 1)
        def _():
            dq_ref[...] = dq_acc[...].astype(BF16)

    kb, vb = d // lanes, 2 * d // lanes
    whole = lambda c: pl.BlockSpec((nb, tb, lanes), lambda h, j: (0, 0, c + h))
    block = lambda c: pl.BlockSpec((None, tb, lanes), lambda h, j: (j, 0, c + h))
    cols = pl.BlockSpec((None, nb, tb, 2), lambda h, j: (h, 0, 0, 0))
    rows = pl.BlockSpec((None, None, 2, tb), lambda h, j: (h, j, 0, 0))
    sd = jax.ShapeDtypeStruct
    return pl.pallas_call(
        body, grid=(nh // 2, nb), name=name,
        in_specs=[whole(0), block(kb), block(vb), cols, rows, whole(0), cols, whole(0)],
        out_specs=[whole(0), block(0), block(0), cols, rows],
        out_shape=[sd((nb, tb, d), BF16), sd((nb, tb, d), BF16), sd((nb, tb, d), BF16),
                   sd((nh // 2, nb, tb, 2), F32), sd((nh // 2, nb, 2, tb), F32)],
        scratch_shapes=[pltpu.VMEM((nb, tb, lanes), F32)],
        compiler_params=_cp(2),
    )(qkv, qkv, qkv, ccol, crow, o, lse, do)


def _cmul(ar, ai, br, bi):
    return ar * br - ai * bi, ar * bi + ai * br


def _s5_scan(name, lam, xin, hs=None):
    reverse = hs is not None
    _, seg, ns, w = xin.shape
    assert ns == SCAN_SEGMENTS
    wb = min(w, 2 * V7X_LANES)
    nsq = seg.bit_length() - 1
    assert (1 << nsq) == seg

    def body(*refs):
        if reverse:
            lam_ref, x_ref, h_ref, o_ref, dl_ref = refs
        else:
            lam_ref, x_ref, o_ref = refs
        lr = jnp.broadcast_to(lam_ref[0], (ns, wb))
        li = jnp.broadcast_to(lam_ref[1], (ns, wb))
        if reverse:
            li = -li
        zero = jnp.zeros((ns, wb), F32)
        at = (lambda n: seg - 1 - n) if reverse else (lambda n: n)

        def local(n, c):
            r = at(n)
            mr, mi = _cmul(lr, li, c[0], c[1])
            nr = mr + x_ref[0, r]
            ni = mi + x_ref[1, r]
            o_ref[0, r] = nr
            o_ref[1, r] = ni
            return nr, ni

        er, ei = lax.fori_loop(0, seg, local, (zero, zero))
        pr, pi = lr, li
        for _ in range(nsq):
            pr, pi = _cmul(pr, pi, pr, pi)
        sub = lax.broadcasted_iota(jnp.int32, (ns, wb), 0)

        def shifted(a, sh):
            if reverse:
                return jnp.where(sub < ns - sh, pltpu.roll(a, ns - sh, 0), 0.0)
            return jnp.where(sub >= sh, pltpu.roll(a, sh, 0), 0.0)

        xr, xi = er, ei
        sh = 1
        while sh < ns:
            tr, ti = _cmul(pr, pi, shifted(xr, sh), shifted(xi, sh))
            xr, xi = xr + tr, xi + ti
            pr, pi = _cmul(pr, pi, pr, pi)
            sh *= 2
        cr, ci = shifted(xr, 1), shifted(xi, 1)

        def fix(r, q):
            tr, ti = _cmul(q[0], q[1], cr, ci)
            gr = o_ref[0, r] + tr
            gi = o_ref[1, r] + ti
            o_ref[0, r] = gr
            o_ref[1, r] = gi
            return gr, gi

        if not reverse:
            def fixup(n, q):
                fix(n, q)
                return _cmul(q[0], q[1], lr, li)

            lax.fori_loop(0, seg, fixup, (lr, li))
            return

        def fixup_acc(n, c):
            qr, qi, ar, ai = c
            r = seg - 1 - n
            gr, gi = fix(r, (qr, qi))
            hr = h_ref[0, r - 1]
            hi = h_ref[1, r - 1]
            qr, qi = _cmul(qr, qi, lr, li)
            return qr, qi, ar + gr * hr + gi * hi, ai + gi * hr - gr * hi

        qr, qi, ar, ai = lax.fori_loop(0, seg - 1, fixup_acc, (lr, li, zero, zero))
        gr, gi = fix(0, (qr, qi))
        hr = jnp.where(sub >= 1, pltpu.roll(h_ref[0, seg - 1], 1, 0), 0.0)
        hi = jnp.where(sub >= 1, pltpu.roll(h_ref[1, seg - 1], 1, 0), 0.0)
        dl_ref[0] = ar + gr * hr + gi * hi
        dl_ref[1] = ai + gi * hr - gr * hi

    big = pl.BlockSpec((2, seg, ns, wb), lambda j: (0, 0, 0, j))
    lam_spec = pl.BlockSpec((2, 1, wb), lambda j: (0, 0, j))
    sd = jax.ShapeDtypeStruct
    if reverse:
        return pl.pallas_call(
            body, grid=(w // wb,), name=name, in_specs=[lam_spec, big, big],
            out_specs=[big, pl.BlockSpec((2, ns, wb), lambda j: (0, 0, j))],
            out_shape=[sd(xin.shape, F32), sd((2, ns, w), F32)], compiler_params=_cp(1),
        )(lam, xin, hs)
    return pl.pallas_call(
        body, grid=(w // wb,), name=name, in_specs=[lam_spec, big], out_specs=big,
        out_shape=sd(xin.shape, F32), compiler_params=_cp(1),
    )(lam, xin)


def _place():
    x, y, c = lax.axis_index("x"), lax.axis_index("y"), lax.axis_index("c")
    chips = [(1 - x, y), (x, 1 - y), (1 - x, 1 - y)]
    return x, y, c, chips


def _comm_params():
    return pltpu.CompilerParams(vmem_limit_bytes=VMEM_LIMIT)


def _cast_place(name, w, after=None):
    nl, r, c = w.shape
    tr = _tile(r, max(16, (1 << 20) // (4 * c) // 16 * 16), 16)

    def body(w_ref, *rest):
        rest[-1][...] = w_ref[...].astype(BF16)

    return pl.pallas_call(
        body, name=name, grid=(nl, r // tr),
        in_specs=[pl.BlockSpec((None, tr, c), lambda l, i: (l, i, 0))] + ([ANY] if after is not None else []),
        out_specs=pl.BlockSpec((None, None, tr, c), lambda l, i: (l, _my_shard(), i, 0)),
        out_shape=jax.ShapeDtypeStruct((nl, N_CHIPS, r, c), BF16),
        compiler_params=_cp(2),
    )(*([w] if after is None else [w, after]))


def _gather_shards(name, bufs):
    n = len(bufs)

    def body(*refs):
        outs = refs[n:2 * n]
        send_sems, recv_sems = refs[2 * n:]
        x, y, c, chips = _place()
        my = 2 * x + y
        sibling = (x, y, 1 - c)

        def part(t, shard, half):
            h = bufs[t].shape[2] // 2
            return outs[t].at[:, shard, pl.ds(half * h, h)]

        def copy(t, k, ref, to):
            return pltpu.make_async_remote_copy(src_ref=ref, dst_ref=ref, send_sem=send_sems.at[t, k],
                                                recv_sem=recv_sems.at[t, k], device_id=to, device_id_type=MESH)

        sent = []
        for t in range(n):
            for k, chip in enumerate(chips):
                sent.append(copy(t, k, part(t, my, c), (*chip, c)))
                sent[-1].start()
        for k, chip in enumerate(chips):
            shard = 2 * chip[0] + chip[1]
            for t in range(n):
                copy(t, k, part(t, shard, c), (*chip, c)).wait_recv()
                sent.append(copy(t, 3 + k, part(t, shard, c), sibling))
                sent[-1].start()
        for k, chip in enumerate(chips):
            shard = 2 * chip[0] + chip[1]
            for t in range(n):
                copy(t, 3 + k, part(t, shard, 1 - c), sibling).wait_recv()
        for cp in sent:
            cp.wait_send()

    return pl.pallas_call(
        body, name=name, in_specs=[ANY] * n, out_specs=[ANY] * n,
        out_shape=[jax.ShapeDtypeStruct(b.shape, b.dtype) for b in bufs],
        input_output_aliases={t: t for t in range(n)},
        scratch_shapes=[pltpu.SemaphoreType.DMA((n, 6)), pltpu.SemaphoreType.DMA((n, 6))],
        compiler_params=_comm_params(),
    )(*bufs)


HBM_SPEC = pl.BlockSpec(memory_space=pltpu.HBM)
SEM_SPEC = pl.BlockSpec(memory_space=pltpu.SEMAPHORE)


def _split_params():
    return pltpu.CompilerParams(has_side_effects=pltpu.SideEffectType.DATAFLOW_SIDE_EFFECTING)


def _gather_start(name, bufs, groups):
    n, ng = len(bufs), len(groups)

    def body(*refs):
        sems = refs[n:n + 2 * ng]
        outs = refs[n + 2 * ng:]
        x, y, c, chips = _place()
        my = 2 * x + y
        for gi, group in enumerate(groups):
            for idx, (t, layer) in enumerate(group):
                block = outs[t].at[layer, my]
                for k, chip in enumerate(chips):
                    pltpu.make_async_remote_copy(
                        src_ref=block, dst_ref=block, send_sem=sems[2 * gi].at[3 * idx + k],
                        recv_sem=sems[2 * gi + 1].at[3 * idx + k], device_id=(*chip, c), device_id_type=MESH).start()

    sem_shapes = []
    for group in groups:
        sem_shapes += [pltpu.SemaphoreType.DMA((3 * len(group),))] * 2
    res = pl.pallas_call(
        body, name=name, in_specs=[HBM_SPEC] * n,
        out_specs=[SEM_SPEC] * (2 * ng) + [HBM_SPEC] * n,
        out_shape=sem_shapes + [pltpu.HBM(b.shape, b.dtype) for b in bufs],
        input_output_aliases={t: 2 * ng + t for t in range(n)},
        compiler_params=_split_params(),
    )(*[pltpu.with_memory_space_constraint(b, pltpu.HBM) for b in bufs])
    sems = [(res[2 * gi], res[2 * gi + 1]) for gi in range(ng)]
    return sems, list(res[2 * ng:])


def _gather_wait(name, bufs, send_sems, recv_sems, after, group):
    n = len(bufs)

    def body(*refs):
        ss, rs = refs[n], refs[n + 1]
        outs = refs[n + 3:]
        x, y, c, chips = _place()
        my = 2 * x + y
        for idx, (t, layer) in enumerate(group):
            for k, chip in enumerate(chips):
                cp = pltpu.make_async_remote_copy(
                    src_ref=outs[t].at[layer, my], dst_ref=outs[t].at[layer, 2 * chip[0] + chip[1]],
                    send_sem=ss.at[3 * idx + k], recv_sem=rs.at[3 * idx + k], device_id=(*chip, c), device_id_type=MESH)
                cp.wait_send()
                cp.wait_recv()

    return list(pl.pallas_call(
        body, name=name, in_specs=[HBM_SPEC] * n + [SEM_SPEC, SEM_SPEC, ANY],
        out_specs=[HBM_SPEC] * n,
        out_shape=[pltpu.HBM(b.shape, b.dtype) for b in bufs],
        input_output_aliases={t: t for t in range(n)},
        compiler_params=_split_params(),
    )(*bufs, send_sems, recv_sems, after))


N_PARTS = 7


def _scatter_items(send, rx, items, c, chips, x, y):
    my = 2 * x + y
    out = []
    for i, (k, layer) in enumerate(items):
        h = send[k].shape[2] // 2
        for kk, chip in enumerate(chips):
            shard = 2 * chip[0] + chip[1]
            for hf in (0, 1):
                out.append((send[k].at[layer, shard, pl.ds(hf * h, h)], rx[k].at[2 * kk + c, layer],
                            N_PARTS * i + 2 * kk + hf, N_PARTS * i + 2 * kk + c, (*chip, hf)))
        out.append((send[k].at[layer, my, pl.ds((1 - c) * h, h)], rx[k].at[N_PARTS - 1, layer],
                    N_PARTS * i + N_PARTS - 1, N_PARTS * i + N_PARTS - 1, (x, y, 1 - c)))
    return out


def _scatter_start(name, send, rx, items):
    n = len(send)
    m = N_PARTS * len(items)

    def body(*refs):
        ssem, rsem = refs[2 * n], refs[2 * n + 1]
        s_out, r_out = refs[2 * n + 2:3 * n + 2], refs[3 * n + 2:4 * n + 2]
        x, y, c, chips = _place()
        for src, dst, si, ri, to in _scatter_items(s_out, r_out, items, c, chips, x, y):
            pltpu.make_async_remote_copy(src_ref=src, dst_ref=dst, send_sem=ssem.at[si], recv_sem=rsem.at[ri],
                                         device_id=to, device_id_type=MESH).start()
        refs[4 * n + 2][...] = jnp.zeros((V7X_SUBLANES, V7X_LANES), F32)

    res = pl.pallas_call(
        body, name=name, in_specs=[HBM_SPEC] * (2 * n),
        out_specs=[SEM_SPEC, SEM_SPEC] + [HBM_SPEC] * (2 * n) + [pl.BlockSpec(memory_space=pltpu.VMEM)],
        out_shape=[pltpu.SemaphoreType.DMA((m,)), pltpu.SemaphoreType.DMA((m,))]
        + [pltpu.HBM(b.shape, b.dtype) for b in list(send) + list(rx)]
        + [jax.ShapeDtypeStruct((V7X_SUBLANES, V7X_LANES), F32)],
        input_output_aliases={t: 2 + t for t in range(2 * n)},
        compiler_params=_split_params(),
    )(*[pltpu.with_memory_space_constraint(b, pltpu.HBM) for b in list(send) + list(rx)])
    return (res[0], res[1]), list(res[2:2 + n]), list(res[2 + n:2 + 2 * n]), res[2 + 2 * n][0, 0]


def _scatter_wait(name, send, rx, ssem, rsem, after, items):
    n = len(send)

    def body(*refs):
        ss, rs = refs[2 * n], refs[2 * n + 1]
        s_out, r_out = refs[2 * n + 3:3 * n + 3], refs[3 * n + 3:]
        x, y, c, chips = _place()
        for i, (src, dst, si, ri, to) in enumerate(_scatter_items(s_out, r_out, items, c, chips, x, y)):
            arrival = i % N_PARTS
            landed = r_out[items[i // N_PARTS][0]].at[arrival, items[i // N_PARTS][1]]
            cp = pltpu.make_async_remote_copy(src_ref=src, dst_ref=landed, send_sem=ss.at[si],
                                              recv_sem=rs.at[N_PARTS * (i // N_PARTS) + arrival],
                                              device_id=to, device_id_type=MESH)
            cp.wait_send()
            cp.wait_recv()

    res = pl.pallas_call(
        body, name=name, in_specs=[HBM_SPEC] * (2 * n) + [SEM_SPEC, SEM_SPEC, ANY],
        out_specs=[HBM_SPEC] * (2 * n),
        out_shape=[pltpu.HBM(b.shape, b.dtype) for b in list(send) + list(rx)],
        input_output_aliases={t: t for t in range(2 * n)},
        compiler_params=_split_params(),
    )(*send, *rx, ssem, rsem, after)
    return list(res[:n]), list(res[n:])


def _chip_sum(name, g, rx):
    nl, _, r, c = g.shape
    h = r // 2
    tr = _tile(h, max(V7X_SUBLANES * 2, (1 << 19) // (2 * c) // 16 * 16), 16)
    nb = h // tr

    def body(g_ref, r_ref, o_ref):
        acc = g_ref[...].astype(F32)
        for k in range(N_PARTS):
            acc = acc + r_ref[k].astype(F32)
        o_ref[...] = acc

    return pl.pallas_call(
        body, name=name, grid=(nl, nb),
        in_specs=[pl.BlockSpec((None, None, tr, c), lambda l, i: (l, _my_shard(), _my_core() * nb + i, 0)),
                  pl.BlockSpec((N_PARTS, None, tr, c), lambda l, i: (0, l, i, 0))],
        out_specs=pl.BlockSpec((None, tr, c), lambda l, i: (l, i, 0)),
        out_shape=jax.ShapeDtypeStruct((nl, h, c), F32),
        compiler_params=_cp(2),
    )(g, rx)


def _send_half(name, fs):
    n = len(fs)

    def body(*refs):
        ins, outs = refs[:n], refs[n:2 * n]
        send_sems, recv_sems = refs[2 * n:]
        x, y, c, _ = _place()
        cps = []
        for t in range(n):
            cps.append(pltpu.make_async_remote_copy(
                src_ref=ins[t], dst_ref=outs[t], send_sem=send_sems.at[t], recv_sem=recv_sems.at[t],
                device_id=(x, y, 1 - c), device_id_type=MESH))
            cps[-1].start()
        for cp in cps:
            cp.wait()

    return pl.pallas_call(
        body, name=name, in_specs=[ANY] * n, out_specs=[ANY] * n,
        out_shape=[jax.ShapeDtypeStruct(f.shape, f.dtype) for f in fs],
        scratch_shapes=[pltpu.SemaphoreType.DMA((n,)), pltpu.SemaphoreType.DMA((n,))],
        compiler_params=_comm_params(),
    )(*fs)


def _peers(x, y, c):
    rel = [(dx, dy, dc) for dx in (0, 1) for dy in (0, 1) for dc in (0, 1) if (dx, dy, dc) != (0, 0, 0)]
    return [(1 - x if dx else x, 1 - y if dy else y, 1 - c if dc else c) for dx, dy, dc in rel]


def _share_start(name, v, land):
    def body(v_ref, land_ref, ssem, rsem, v_out, land_out, token):
        x, y, c, _ = _place()
        me = 4 * x + 2 * y + c
        for k, peer in enumerate(_peers(x, y, c)):
            pltpu.make_async_remote_copy(src_ref=v_out, dst_ref=land_out.at[me], send_sem=ssem.at[k],
                                         recv_sem=rsem.at[k], device_id=peer, device_id_type=MESH).start()
        token[...] = jnp.zeros((V7X_SUBLANES, V7X_LANES), F32)

    res = pl.pallas_call(
        body, name=name, in_specs=[HBM_SPEC, HBM_SPEC],
        out_specs=[SEM_SPEC, SEM_SPEC, HBM_SPEC, HBM_SPEC, pl.BlockSpec(memory_space=pltpu.VMEM)],
        out_shape=[pltpu.SemaphoreType.DMA((N_DEV - 1,)), pltpu.SemaphoreType.DMA((N_DEV - 1,)),
                   pltpu.HBM(v.shape, v.dtype), pltpu.HBM(land.shape, land.dtype),
                   jax.ShapeDtypeStruct((V7X_SUBLANES, V7X_LANES), F32)],
        input_output_aliases={0: 2, 1: 3},
        compiler_params=_split_params(),
    )(pltpu.with_memory_space_constraint(v, pltpu.HBM), pltpu.with_memory_space_constraint(land, pltpu.HBM))
    return (res[0], res[1]), res[2], res[3], res[4][0, 0]


def _share_wait(name, v, land, ssem, rsem, after):
    def body(v_ref, land_ref, ss, rs, after_ref, v_out, land_out):
        x, y, c, _ = _place()
        for k, (px, py, pc) in enumerate(_peers(x, y, c)):
            cp = pltpu.make_async_remote_copy(src_ref=v_out, dst_ref=land_out.at[4 * px + 2 * py + pc],
                                              send_sem=ss.at[k], recv_sem=rs.at[k], device_id=(px, py, pc),
                                              device_id_type=MESH)
            cp.wait_send()
            cp.wait_recv()

    res = pl.pallas_call(
        body, name=name, in_specs=[HBM_SPEC, HBM_SPEC, SEM_SPEC, SEM_SPEC, ANY],
        out_specs=[HBM_SPEC, HBM_SPEC],
        out_shape=[pltpu.HBM(v.shape, v.dtype), pltpu.HBM(land.shape, land.dtype)],
        input_output_aliases={0: 0, 1: 1},
        compiler_params=_split_params(),
    )(v, land, ssem, rsem, after)
    return res[0], res[1]


def _sum_devices(name, v, land):
    r, c = v.shape
    tr = _tile(r, 512)

    def body(v_ref, land_ref, o_ref):
        x, y, cc, _ = _place()
        me = 4 * x + 2 * y + cc
        own = v_ref[...]
        acc = jnp.where(me == 0, own, land_ref[0])
        for k in range(1, N_DEV):
            acc = acc + jnp.where(me == k, own, land_ref[k])
        o_ref[...] = acc

    return pl.pallas_call(
        body, grid=(r // tr,), name=name,
        in_specs=[pl.BlockSpec((tr, c), lambda i: (i, 0)), pl.BlockSpec((N_DEV, tr, c), lambda i: (0, i, 0))],
        out_specs=pl.BlockSpec((tr, c), lambda i: (i, 0)),
        out_shape=jax.ShapeDtypeStruct((r, c), F32), compiler_params=_cp(1),
    )(v, land)


def _rows_view(wall):
    nl, s, r, c = wall.shape
    return wall.reshape(nl, s * r, c)


def _ffn_fwd(tag, alpha, x, w_in, w_out3, layer, g, b):
    h, a = _ffn_in(f"{tag}_in", x, w_in, layer)
    y, xhat, rstd = _mm_ln(f"{tag}_out", alpha, 0.5, a, w_out3, layer, x, g, b)
    return y, (x, h, a, xhat, rstd)


def _dx_ln(name, a, b, *, nk, a_blk, a_map, b_blk, b_map, alpha, dz, nxt):
    t, d = dz.shape
    tm = a_blk[-2]
    xhat, rstd, g = nxt
    through = xhat is not None

    def body(*refs):
        a_ref, b_ref, dz_ref = refs[:3]
        if through:
            xh_ref, rs_ref, g_ref, o_ref, dg_ref, db_ref, acc = refs[3:]
        else:
            g_ref, o_ref, acc = refs[3:]
        i, kk = pl.program_id(0), pl.program_id(1)
        p = lax.dot_general(a_ref[...].astype(BF16), b_ref[...].astype(BF16), _NT, preferred_element_type=F32)

        @pl.when(kk == 0)
        def _():
            acc[...] = p

        @pl.when(kk > 0)
        def _():
            acc[...] += p

        @pl.when(kk == nk - 1)
        def _():
            dy = alpha * dz_ref[...] + acc[...]
            if not through:
                o_ref[...] = dy + g_ref[...]
                return
            xh = xh_ref[...]
            dxh = dy * g_ref[...]
            m1 = jnp.mean(dxh, axis=-1, keepdims=True)
            m2 = jnp.mean(dxh * xh, axis=-1, keepdims=True)
            o_ref[...] = rs_ref[...] * (dxh - m1 - xh * m2)
            pg = jnp.sum(dy * xh, axis=0, keepdims=True)
            pb = jnp.sum(dy, axis=0, keepdims=True)

            @pl.when(i == 0)
            def _():
                dg_ref[...] = pg
                db_ref[...] = pb

            @pl.when(i > 0)
            def _():
                dg_ref[...] += pg
                db_ref[...] += pb

    row = lambda c: pl.BlockSpec((tm, c), lambda i, kk: (i, 0))
    vec = pl.BlockSpec((1, d), lambda i, kk: (0, 0))
    sd = jax.ShapeDtypeStruct
    in_specs = [pl.BlockSpec(a_blk, a_map), pl.BlockSpec(b_blk, b_map), row(d)]
    args = [a, b, dz]
    if through:
        in_specs += [row(d), row(1), vec]
        args += [xhat, rstd, g]
        out_specs, out_shape = [row(d), vec, vec], [sd((t, d), F32), sd((1, d), F32), sd((1, d), F32)]
    else:
        in_specs += [vec]
        args += [g]
        out_specs, out_shape = row(d), sd((t, d), F32)
    return pl.pallas_call(
        body, grid=(t // tm, nk), name=name, in_specs=in_specs, out_specs=out_specs, out_shape=out_shape,
        scratch_shapes=[pltpu.VMEM((tm, d), F32)], compiler_params=_cp(2),
    )(*args)


def _ffn_bwd(tag, alpha, dz, saved, w_in, w_out3, layer, g_win, g_wout3, grads_done, nxt, zrow):
    x, h, a, _, _ = saved
    t = x.shape[0]
    _, s, k, n = w_in.shape
    tm = _tile(t, 512)
    g_wout3 = _mm_tn(f"{tag}_dwout", a, dz, tm=n, scale=0.5, layer=layer, into=g_wout3)
    dh = _ffn_da(f"{tag}_da", dz, w_out3, layer, h, zrow)
    g_win = _mm(f"{tag}_dwin", x, dh, mode="tn", grid=(s, t // tm), kaxis=1,
                a_blk=(tm, k), a_map=lambda j, kk: (kk, 0),
                b_blk=(None, tm, n), b_map=lambda j, kk: (j // 2, kk, j % 2),
                o_shape=w_in.shape, o_blk=(None, None, k, n), o_map=lambda j, kk: (layer, j, 0, 0),
                o_dtype=g_win.dtype, into=g_win)
    zero = grads_done(g_win, g_wout3)
    return _dx_ln(f"{tag}_dx", dh, w_in, nk=s, a_blk=(None, tm, n), a_map=lambda i, kk: (kk // 2, i, kk % 2),
                  b_blk=(None, None, k, n), b_map=lambda i, kk: (layer, kk, 0, 0),
                  alpha=alpha, dz=dz, nxt=(nxt[0], nxt[1], nxt[2] + zero))


def _fox_fwd(tag, alpha, x, w_pad, bf, w_o3, layer, g, b):
    t, d = x.shape
    nh = bf.shape[1]
    tb = _tile(t, ATTN_BLOCK)
    nb = t // tb
    qkv = _mm_nn(f"{tag}_qkv", x, w_pad[:, :3 * d], o_dtype=BF16, tn=d).reshape(nb, tb, 3 * d)
    fl = _mm_nn(f"{tag}_gate", x, w_pad[:, 3 * d:])[:, :nh]
    cum = _fox_cumsum(f"{tag}_cum", fl, bf)
    ccol = cum.reshape(nb, tb, nh // 2, 2).transpose(2, 0, 1, 3)
    crow = cum.reshape(nb, tb, nh // 2, 2).transpose(2, 0, 3, 1)
    o, lse = _attn_fwd(f"{tag}_attn", qkv, ccol, crow, nh)
    o2 = o.reshape(t, d)
    y, xhat, rstd = _mm_ln(f"{tag}_oproj", alpha, 1.0, o2, w_o3, layer, x, g, b)
    return y, xhat, rstd, (x, qkv, ccol, crow, o, lse, fl, w_pad)


def _fox_bwd(tag, alpha, dm, saved, bf, w_o3, layer, g_wo3, grads_done, nxt):
    x, qkv, ccol, crow, o, lse, fl, w_pad = saved
    t, d = x.shape
    nh = bf.shape[1]
    nb, tb, _ = qkv.shape
    tm = _tile(t, 512)
    g_wo3 = _mm_tn(f"{tag}_dwo", o.reshape(t, d), dm, layer=layer, into=g_wo3)
    do = _mm_nt(f"{tag}_do", dm, w_o3, layer=layer).reshape(nb, tb, d)
    dq, dk, dv, drow, dcol = _attn_bwd(f"{tag}_attn_bwd", qkv, ccol, crow, o, lse, do, nh)
    dcum = drow.transpose(1, 2, 0, 3).reshape(t, nh) - dcol.transpose(1, 3, 0, 2).reshape(t, nh)
    dfl, dbf = _fox_cumsum_bwd(f"{tag}_cum_bwd", dcum, fl, bf)
    pad = w_pad.shape[1] - 3 * d - nh
    dproj = jnp.concatenate([dq.reshape(t, d), dk.reshape(t, d), dv.reshape(t, d),
                             dfl.astype(BF16), jnp.zeros((t, pad), BF16)], axis=1)
    d_wpad = _mm_tn(f"{tag}_dwin", x, dproj, tn=_tile(w_pad.shape[1], 640, V7X_LANES))
    zero = grads_done(g_wo3, d_wpad)
    cols = w_pad.shape[1]
    out = _dx_ln(f"{tag}_dx", dproj, w_pad, nk=1, a_blk=(tm, cols), a_map=lambda i, kk: (i, 0),
                 b_blk=(d, cols), b_map=lambda i, kk: (0, 0), alpha=alpha, dz=dm, nxt=(nxt[0], nxt[1], nxt[2] + zero))
    return out, dbf


def _to_segments(a):
    t, d = a.shape
    return a.reshape(SCAN_SEGMENTS, t // SCAN_SEGMENTS, d).transpose(1, 0, 2).reshape(t, d)


def _from_segments(a):
    t, d = a.shape
    return a.reshape(t // SCAN_SEGMENTS, SCAN_SEGMENTS, d).transpose(1, 0, 2).reshape(t, d)


def _s5_discretise(a_re, a_im, log_dt, b_re, b_im):
    dt = jnp.exp(log_dt)[:, None]
    mag = jnp.exp(a_re * dt)
    ang = a_im * dt
    lb_re = mag * jnp.cos(ang)
    lb_im = mag * jnp.sin(ang)
    den = a_re * a_re + a_im * a_im
    nr = lb_re - 1.0
    ni = lb_im
    z_re = (nr * a_re + ni * a_im) / den
    z_im = (ni * a_re - nr * a_im) / den
    bb_re = z_re[..., None] * b_re - z_im[..., None] * b_im
    bb_im = z_re[..., None] * b_im + z_im[..., None] * b_re
    return lb_re, lb_im, bb_re, bb_im


S5_BLOCK_GROUPS = 8


def _blockdiag_in(bb):
    g, p, h = bb.shape
    e = jnp.eye(S5_BLOCK_GROUPS, dtype=bb.dtype)
    b4 = bb.reshape(g // S5_BLOCK_GROUPS, S5_BLOCK_GROUPS, p, h)
    return jnp.einsum("jgph,gf->jghfp", b4, e).reshape(g // S5_BLOCK_GROUPS, S5_BLOCK_GROUPS * h, S5_BLOCK_GROUPS * p)


def _blockdiag_in_grad(d):
    nj, gh, gp = d.shape
    h, p = gh // S5_BLOCK_GROUPS, gp // S5_BLOCK_GROUPS
    e = jnp.eye(S5_BLOCK_GROUPS, dtype=d.dtype)
    d6 = d.reshape(nj, S5_BLOCK_GROUPS, h, S5_BLOCK_GROUPS, p)
    return jnp.einsum("jghfp,gf->jgph", d6, e).reshape(nj * S5_BLOCK_GROUPS, p, h)


def _blockdiag_out(cc):
    g, h, p = cc.shape
    e = jnp.eye(S5_BLOCK_GROUPS, dtype=cc.dtype)
    c4 = cc.reshape(g // S5_BLOCK_GROUPS, S5_BLOCK_GROUPS, h, p)
    return jnp.einsum("jghp,gf->jfpgh", c4, e).reshape(g // S5_BLOCK_GROUPS, S5_BLOCK_GROUPS * p, S5_BLOCK_GROUPS * h)


def _blockdiag_out_grad(d):
    nj, gp, gh = d.shape
    h, p = gh // S5_BLOCK_GROUPS, gp // S5_BLOCK_GROUPS
    e = jnp.eye(S5_BLOCK_GROUPS, dtype=d.dtype)
    d6 = d.reshape(nj, S5_BLOCK_GROUPS, p, S5_BLOCK_GROUPS, h)
    return jnp.einsum("jfpgh,gf->jghp", d6, e).reshape(nj * S5_BLOCK_GROUPS, h, p)


def _s5_fwd(tag, x, prm, w_out, layer):
    a_re, a_im, log_dt, b_re, b_im, c_re, c_im, d_skip = prm
    t, d = x.shape
    g, p = a_re.shape
    w = g * p
    nj = g // S5_BLOCK_GROUPS
    cw, sw = S5_BLOCK_GROUPS * S5_GROUP, S5_BLOCK_GROUPS * p
    seg = t // SCAN_SEGMENTS
    tm = _tile(t, 4096)
    lb_re, lb_im, bb_re, bb_im = _s5_discretise(a_re, a_im, log_dt, b_re, b_im)
    lam = jnp.stack([lb_re.reshape(1, w), lb_im.reshape(1, w)])
    bs = jnp.stack([_blockdiag_in(bb_re), _blockdiag_in(bb_im)]).astype(BF16)
    cs = jnp.stack([_blockdiag_out(c_re), -_blockdiag_out(c_im)]).astype(BF16)
    dvec = d_skip.reshape(1, d)
    u = _to_segments(x)
    bu = _mm(f"{tag}_bu", u, bs, mode="nn", grid=(2, nj, t // tm), kaxis=None,
             a_blk=(tm, cw), a_map=lambda r, j, i: (i, j),
             b_blk=(None, None, cw, sw), b_map=lambda r, j, i: (r, j, 0, 0),
             o_shape=(2, t, w), o_blk=(None, tm, sw), o_map=lambda r, j, i: (r, i, j))
    hs = _s5_scan(f"{tag}_scan", lam, bu.reshape(2, seg, SCAN_SEGMENTS, w)).reshape(2, t, w)
    ych = _mm(f"{tag}_ch", hs, cs, mode="nn", grid=(nj, t // tm, 2), kaxis=2,
              a_blk=(None, tm, sw), a_map=lambda j, i, r: (r, i, j),
              b_blk=(None, None, sw, cw), b_map=lambda j, i, r: (r, j, 0, 0),
              o_shape=(t, d), o_blk=(tm, cw), o_map=lambda j, i, r: (i, j))
    ypre, act = _s5_act_fwd(f"{tag}_act", ych, u, dvec)
    vg = _mm_shards_nn(f"{tag}_wout", act, w_out, layer, F32)
    m = _from_segments(_glu_fwd(f"{tag}_glu", vg))
    return m, (u, lam, bs, cs, dvec, hs, ypre, act, vg)


def _s5_bwd(tag, dm, saved, prm, w_out, layer, g_wout):
    a_re, a_im, log_dt, b_re, b_im, c_re, c_im, d_skip = prm
    u, lam, bs, cs, dvec, hs, ypre, act, vg = saved
    t, d = u.shape
    g, p = a_re.shape
    w = g * p
    nj = g // S5_BLOCK_GROUPS
    cw, sw = S5_BLOCK_GROUPS * S5_GROUP, S5_BLOCK_GROUPS * p
    seg = t // SCAN_SEGMENTS
    tm = _tile(t, 4096)
    dvg = _glu_bwd(f"{tag}_glu_bwd", _to_segments(dm), vg)
    g_wout = _mm_shards_tn(f"{tag}_dwout", act, dvg, layer, g_wout)
    dact = _mm_shards_nt(f"{tag}_dact", dvg, w_out, layer)
    dypre, duskip, dd = _s5_act_bwd(f"{tag}_act_bwd", dact, ypre, u, dvec)
    dh = _mm(f"{tag}_dh", dypre, cs, mode="nt", grid=(2, nj, t // tm), kaxis=None,
             a_blk=(tm, cw), a_map=lambda r, j, i: (i, j),
             b_blk=(None, None, sw, cw), b_map=lambda r, j, i: (r, j, 0, 0),
             o_shape=(2, t, w), o_blk=(None, tm, sw), o_map=lambda r, j, i: (r, i, j))
    dcs = _mm(f"{tag}_dc", hs, dypre, mode="tn", grid=(2, nj, t // tm), kaxis=2,
              a_blk=(None, tm, sw), a_map=lambda r, j, i: (r, i, j),
              b_blk=(tm, cw), b_map=lambda r, j, i: (i, j),
              o_shape=(2, nj, sw, cw), o_blk=(None, None, sw, cw), o_map=lambda r, j, i: (r, j, 0, 0))
    gs, dlam8 = _s5_scan(f"{tag}_scan_bwd", lam, dh.reshape(2, seg, SCAN_SEGMENTS, w),
                         hs.reshape(2, seg, SCAN_SEGMENTS, w))
    gs = gs.reshape(2, t, w)
    du = _mm(f"{tag}_du", gs, bs, mode="nt", grid=(nj, t // tm, 2), kaxis=2,
             a_blk=(None, tm, sw), a_map=lambda j, i, r: (r, i, j),
             b_blk=(None, None, cw, sw), b_map=lambda j, i, r: (r, j, 0, 0),
             o_shape=(t, d), o_blk=(tm, cw), o_map=lambda j, i, r: (i, j))
    dbs = _mm(f"{tag}_db", u, gs, mode="tn", grid=(2, nj, t // tm), kaxis=2,
              a_blk=(tm, cw), a_map=lambda r, j, i: (i, j),
              b_blk=(None, tm, sw), b_map=lambda r, j, i: (r, i, j),
              o_shape=(2, nj, cw, sw), o_blk=(None, None, cw, sw), o_map=lambda r, j, i: (r, j, 0, 0))
    dx = _from_segments(du + duskip)
    dlam = jnp.sum(dlam8, axis=1).reshape(2, g, p)
    small = dict(dlb_re=dlam[0], dlb_im=dlam[1],
                 dbb_re=_blockdiag_in_grad(dbs[0]), dbb_im=_blockdiag_in_grad(dbs[1]),
                 dc_re=_blockdiag_out_grad(dcs[0]), dc_im=-_blockdiag_out_grad(dcs[1]),
                 dd=dd.reshape(g, S5_GROUP))
    return dx, g_wout, small


def _pack(pieces):
    rows = []
    for p in pieces:
        flat = p.reshape(-1).astype(F32)
        n = flat.shape[0]
        rows.append(jnp.pad(flat, (0, -n % V7X_LANES)).reshape(-1, V7X_LANES))
    buf = jnp.concatenate(rows, axis=0)
    return jnp.pad(buf, ((0, -buf.shape[0] % V7X_SUBLANES), (0, 0)))


def _unpack(buf, shapes):
    out, row = [], 0
    for s in shapes:
        n = math.prod(s)
        nr = -(-n // V7X_LANES)
        out.append(buf[row:row + nr].reshape(-1)[:n].reshape(s))
        row += nr
    return out


def kernel(x, ffn1_w_in, ffn1_w_out, ln1_g, ln1_b, lnm_g, lnm_b, ffn2_w_in, ffn2_w_out, ln2_g, ln2_b, fox_w_in, fox_b_f, fox_w_o, s5_a_re, s5_a_im, s5_log_dt, s5_b_re, s5_b_im, s5_c_re, s5_c_im, s5_d, s5_w_out, loss_target, m_ffn1_w_in, m_ffn1_w_out, m_ln1_g, m_ln1_b, m_lnm_g, m_lnm_b, m_ffn2_w_in, m_ffn2_w_out, m_ln2_g, m_ln2_b, m_fox_w_in, m_fox_b_f, m_fox_w_o, m_s5_a_re, m_s5_a_im, m_s5_log_dt, m_s5_b_re, m_s5_b_im, m_s5_c_re, m_s5_c_im, m_s5_d, m_s5_w_out, v_ffn1_w_in, v_ffn1_w_out, v_ln1_g, v_ln1_b, v_lnm_g, v_lnm_b, v_ffn2_w_in, v_ffn2_w_out, v_ln2_g, v_ln2_b, v_fox_w_in, v_fox_b_f, v_fox_w_o, v_s5_a_re, v_s5_a_im, v_s5_log_dt, v_s5_b_re, v_s5_b_im, v_s5_c_re, v_s5_c_im, v_s5_d, v_s5_w_out):
    big_names = ["ffn1_w_in", "ffn1_w_out", "ffn2_w_in", "ffn2_w_out", "fox_w_in", "fox_w_o", "s5_w_out"]
    small_names = ["ln1_g", "ln1_b", "lnm_g", "lnm_b", "ln2_g", "ln2_b", "fox_b_f", "s5_a_re", "s5_a_im", "s5_log_dt",
                   "s5_b_re", "s5_b_im", "s5_c_re", "s5_c_im", "s5_d"]
    out_order = ["ffn1_w_in", "ffn1_w_out", "ln1_g", "ln1_b", "lnm_g", "lnm_b", "ffn2_w_in", "ffn2_w_out", "ln2_g",
                 "ln2_b", "fox_w_in", "fox_b_f", "fox_w_o", "s5_a_re", "s5_a_im", "s5_log_dt", "s5_b_re", "s5_b_im",
                 "s5_c_re", "s5_c_im", "s5_d", "s5_w_out"]
    env = dict(locals())
    w = {n: env[n] for n in out_order}
    mom = {n: env["m_" + n] for n in out_order}
    vel = {n: env["v_" + n] for n in out_order}

    depth, d = ln1_g.shape
    t = x.shape[1]
    alpha = (2.0 * depth) ** 0.25
    x0 = x.reshape(t, d)
    tgt = loss_target.reshape(t, d)

    tix = {n: k for k, n in enumerate(big_names)}
    groups = []
    for i in range(depth):
        j = i // 2
        groups.append([(tix["ffn1_w_in"], i), (tix["ffn1_w_out"], i)])
        mixer = [(tix["fox_w_in"], j), (tix["fox_w_o"], j)] if i % 2 == 0 else [(tix["s5_w_out"], j)]
        groups.append(mixer + [(tix["ffn2_w_in"], i), (tix["ffn2_w_out"], i)])
    first = groups[0]
    cast = {big_names[k]: _cast_place(f"cast_{big_names[k]}", w[big_names[k]]) for k, _ in first}
    sems, sent = _gather_start("gather_start_first", list(cast.values()), [[(p, layer) for p, (_, layer) in enumerate(first)]])
    cast = dict(zip(cast, sent))
    for n in big_names:
        if n not in cast:
            cast[n] = _cast_place(f"cast_{n}", w[n], after=sent[0])
    more, bufs = _gather_start("gather_start", [cast[n] for n in big_names], groups[1:])
    sems = sems + more
    full, rows3 = {}, {}

    def arrive(gi, after):
        nonlocal bufs
        bufs = _gather_wait(f"gather_wait_{gi}", bufs, sems[gi][0], sems[gi][1], after, groups[gi])
        full.update(zip(big_names, bufs))
        rows3.update({n: _rows_view(full[n]) for n in ("ffn1_w_out", "ffn2_w_out", "fox_w_o")})

    nh = fox_b_f.shape[1]
    fox_cols = 3 * d + nh
    fox_pad = -(-fox_cols // (5 * V7X_LANES)) * (5 * V7X_LANES)

    def fox_wpad(j):
        wf = full["fox_w_in"][j].transpose(1, 0, 2).reshape(d, fox_cols)
        return jnp.pad(wf, ((0, 0), (0, fox_pad - fox_cols)))

    def s5_params(j):
        return (s5_a_re[j], s5_a_im[j], s5_log_dt[j], s5_b_re[j], s5_b_im[j], s5_c_re[j], s5_c_im[j], s5_d[j])

    saved = []
    h = x0
    for i in range(depth):
        j = i // 2
        arrive(2 * i, h)
        h, s1 = _ffn_fwd(f"l{i}_ffn1", alpha, h, full["ffn1_w_in"], rows3["ffn1_w_out"], i,
                         ln1_g[i:i + 1], ln1_b[i:i + 1])
        arrive(2 * i + 1, h)
        if i % 2 == 0:
            h, xhat_m, rstd_m, sm = _fox_fwd(f"l{i}_fox", alpha, h, fox_wpad(j), fox_b_f[j:j + 1], rows3["fox_w_o"], j,
                                             lnm_g[i:i + 1], lnm_b[i:i + 1])
        else:
            m, sm = _s5_fwd(f"l{i}_s5", h, s5_params(j), full["s5_w_out"], j)
            h, xhat_m, rstd_m = _ln_fwd(f"l{i}_lnm", alpha, h, m, 1.0, lnm_g[i:i + 1], lnm_b[i:i + 1])
        h, s2 = _ffn_fwd(f"l{i}_ffn2", alpha, h, full["ffn2_w_in"], rows3["ffn2_w_out"], i,
                         ln2_g[i:i + 1], ln2_b[i:i + 1])
        saved.append((s1, sm, (xhat_m, rstd_m), s2))
    loss_part = _loss_sum("loss", h, tgt) * (0.5 / d)

    fox_in_names = [f"fox_w_in_l{j}" for j in range(fox_w_in.shape[0])]
    gshape = {n: full[n].shape for n in big_names if n != "fox_w_in"}
    gshape.update({n: (1,) + full["fox_w_in"].shape[1:] for n in fox_in_names})
    gbuf = {n: lax.empty(s, BF16) for n, s in gshape.items()}
    rxbuf = {n: lax.empty((N_PARTS, s[0], s[2] // 2, s[3]), BF16) for n, s in gshape.items()}
    pending = []

    zero = jnp.zeros((), F32)

    def scatter(tag, pairs):
        nonlocal zero
        names = list(dict.fromkeys(n for n, _ in pairs))
        items = [(names.index(n), layer) for n, layer in pairs]
        sem, send, rx, zero = _scatter_start(f"scatter_start_{tag}", [gbuf[n] for n in names],
                                             [rxbuf[n] for n in names], items)
        gbuf.update(zip(names, send))
        rxbuf.update(zip(names, rx))
        pending.append((tag, names, items, sem))

    gsmall = {n: [None] * w[n].shape[0] for n in small_names}
    s5_cot = [None] * s5_a_re.shape[0]
    cot_names = ["dlb_re", "dlb_im", "dbb_re", "dbb_im", "dc_re", "dc_im", "dd"]
    ln_names = ["ln1_g", "ln1_b", "lnm_g", "lnm_b", "ln2_g", "ln2_b"]

    def zrow():
        return jnp.zeros((1, d), F32) + zero

    def ffn_done(which, i):
        def done(g_win, g_wout3):
            gbuf[f"{which}_w_in"], gbuf[f"{which}_w_out"] = g_win, g_wout3.reshape(gshape[f"{which}_w_out"])
            scatter(f"l{i}_{which}", [(f"{which}_w_in", i), (f"{which}_w_out", i)])
            return zero
        return done

    _, _, _, (_, _, _, xhat_top, rstd_top) = saved[depth - 1]
    dz, dg, db = _ln_bwd("top_ln_bwd", [(h, 1.0 / d), (tgt, -1.0 / d)], xhat_top, rstd_top, ln2_g[depth - 1:depth])
    gsmall["ln2_g"][depth - 1], gsmall["ln2_b"][depth - 1] = dg, db
    grad_x = None
    for i in reversed(range(depth)):
        j = i // 2
        s1, sm, (xhat_m, rstd_m), s2 = saved[i]
        dz, dg, db = _ffn_bwd(f"l{i}_ffn2", alpha, dz, s2, full["ffn2_w_in"], rows3["ffn2_w_out"], i,
                              gbuf["ffn2_w_in"], _rows_view(gbuf["ffn2_w_out"]), ffn_done("ffn2", i),
                              (xhat_m, rstd_m, lnm_g[i:i + 1]), zrow())
        gsmall["lnm_g"][i], gsmall["lnm_b"][i] = dg, db
        ln1 = (s1[3], s1[4], ln1_g[i:i + 1])
        if i % 2 == 0:
            def fox_done(g_wo3, d_wpad, j=j, i=i):
                gbuf["fox_w_o"] = g_wo3.reshape(gshape["fox_w_o"])
                gbuf[fox_in_names[j]] = d_wpad[:, :fox_cols].reshape(d, N_CHIPS, -1).transpose(1, 0, 2)[None].astype(BF16)
                scatter(f"l{i}_fox", [("fox_w_o", j), (fox_in_names[j], 0)])
                return zero

            (dz, dg, db), gsmall["fox_b_f"][j] = _fox_bwd(
                f"l{i}_fox", alpha, dz, sm, fox_b_f[j:j + 1], rows3["fox_w_o"], j,
                _rows_view(gbuf["fox_w_o"]), fox_done, ln1)
        else:
            dx, gbuf["s5_w_out"], s5_cot[j] = _s5_bwd(f"l{i}_s5", dz, sm, s5_params(j), full["s5_w_out"], j,
                                                      gbuf["s5_w_out"])
            scatter(f"l{i}_s5", [("s5_w_out", j)])
            dz, dg, db = _ln_bwd(f"l{i}_ln1_bwd", [(dz, alpha), (dx, 1.0)], ln1[0], ln1[1], ln1[2] + zero)
        gsmall["ln1_g"][i], gsmall["ln1_b"][i] = dg, db
        if i > 0:
            below = saved[i - 1][3]
            dz, dg, db = _ffn_bwd(f"l{i}_ffn1", alpha, dz, s1, full["ffn1_w_in"], rows3["ffn1_w_out"], i,
                                  gbuf["ffn1_w_in"], _rows_view(gbuf["ffn1_w_out"]), ffn_done("ffn1", i),
                                  (below[3], below[4], ln2_g[i - 1:i]), zrow())
            gsmall["ln2_g"][i - 1], gsmall["ln2_b"][i - 1] = dg, db
        else:
            pieces = [loss_part + zero] + [jnp.concatenate(gsmall[n], axis=0) for n in ln_names + ["fox_b_f"]]
            pieces += [jnp.stack([s5_cot[k][n] for k in range(len(s5_cot))]) for n in cot_names]
            mine = _pack(pieces)
            share_sem, mine, land, zero = _share_start("small_share_start", mine, lax.empty((N_DEV,) + mine.shape, F32))
            grad_x = _ffn_bwd(f"l{i}_ffn1", alpha, dz, s1, full["ffn1_w_in"], rows3["ffn1_w_out"], i,
                              gbuf["ffn1_w_in"], _rows_view(gbuf["ffn1_w_out"]), ffn_done("ffn1", i),
                              (None, None, jnp.zeros((1, d), F32)), zrow()).reshape(x.shape)

    shapes = [p.shape for p in pieces]
    mine, land = _share_wait("small_share_wait", mine, land, share_sem[0], share_sem[1], grad_x)
    summed = _unpack(_sum_devices("small_sum", mine, land), shapes)
    loss = summed[0].reshape(())
    gs_final = dict(zip(ln_names + ["fox_b_f"], summed[1:8]))
    cot = dict(zip(cot_names, summed[8:]))
    prm_names = ["s5_a_re", "s5_a_im", "s5_log_dt", "s5_b_re", "s5_b_im"]
    _, disc_vjp = jax.vjp(jax.vmap(_s5_discretise), *[w[n] for n in prm_names])
    for n, gval in zip(prm_names, disc_vjp((cot["dlb_re"], cot["dlb_im"], cot["dbb_re"], cot["dbb_im"]))):
        gs_final[n] = gval
    gs_final["s5_c_re"], gs_final["s5_c_im"], gs_final["s5_d"] = cot["dc_re"], cot["dc_im"], cot["dd"]

    grads, deltas, new_m, new_v = {}, {}, {}, {}
    small_shapes = [w[n].shape for n in small_names]
    for n in small_names:
        grads[n] = gs_final[n].reshape(w[n].shape)
    packed = [_pack([src[n] for n in small_names]) for src in (w, grads, mom, vel)]
    small_out = _adamw("adamw_small", *packed)
    for dst, buf in zip((deltas, new_m, new_v), small_out):
        for n, val in zip(small_names, _unpack(buf, small_shapes)):
            dst[n] = val

    for tag, names, items, sem in pending:
        send, rx = _scatter_wait(f"scatter_wait_{tag}", [gbuf[n] for n in names], [rxbuf[n] for n in names],
                                 sem[0], sem[1], small_out[0], items)
        gbuf.update(zip(names, send))
        rxbuf.update(zip(names, rx))
    half = {n: _chip_sum(f"grad_chip_sum_{n}", gbuf[n], rxbuf[n]) for n in gshape}
    half["fox_w_in"] = jnp.concatenate([half[n] for n in fox_in_names], axis=0)
    halves = [half[n] for n in big_names]
    theirs = _send_half("grad_send_half", halves)
    for n, mine_h, their_h in zip(big_names, halves, theirs):
        grads[n], deltas[n], new_m[n], new_v[n] = _adamw_join(f"adamw_{n}", w[n], mine_h, their_h, mom[n], vel[n])
    return (loss, grad_x, *[grads[n] for n in out_order], *[deltas[n] for n in out_order],
            *[new_m[n] for n in out_order], *[new_v[n] for n in out_order])
```

```python
import functools
import math

import jax
import jax.numpy as jnp
from jax import lax
from jax.experimental import pallas as pl
from jax.experimental.pallas import tpu as pltpu

F32 = jnp.float32
BF16 = jnp.bfloat16
LN_EPS = 1e-5
NEG_INF = -1e30
ADAM_LR = 0.001
ADAM_B1 = 0.9
ADAM_B2 = 0.999
ADAM_EPS = 1e-08
ADAM_WD = 0.01
ADAM_STEP = 10
S5_GROUP = 16
SCAN_SEGMENTS = 32
ATTN_BLOCK = 1024
V7X_SUBLANES = 8
V7X_LANES = 128
VMEM_LIMIT = 56 * 1024 * 1024
N_CHIPS = 4
N_DEV = 8
MESH = pl.DeviceIdType.MESH
ANY = pl.BlockSpec(memory_space=pl.ANY)


def _cp(n_grid, kaxis=None):
    sem = tuple("arbitrary" if (kaxis is None or i == kaxis) else "parallel" for i in range(n_grid))
    return pltpu.CompilerParams(dimension_semantics=sem, vmem_limit_bytes=VMEM_LIMIT)


def _tile(n, pref, mult=V7X_SUBLANES):
    if n <= pref:
        return n
    for t in range(pref, 0, -1):
        if n % t == 0 and t % mult == 0:
            return t
    return n


_CONTRACT = {"nn": ((1,), (0,)), "nt": ((1,), (1,)), "tn": ((0,), (0,))}


def _mm(name, a, b, *, mode, grid, kaxis, a_blk, a_map, b_blk, b_map, o_shape, o_blk, o_map, o_dtype=F32, scale=None,
        into=None):
    nk = 1 if kaxis is None else grid[kaxis]
    assert kaxis is None or kaxis == len(grid) - 1
    dims = (_CONTRACT[mode], ((), ()))
    use_acc = nk > 1 and o_dtype != F32
    acc_shape = tuple(d for d in o_blk if d is not None)

    def body(a_ref, b_ref, *rest):
        o_ref, scratch = (rest[1], rest[2:]) if into is not None else (rest[0], rest[1:])
        p = lax.dot_general(a_ref[...].astype(BF16), b_ref[...].astype(BF16), dims, preferred_element_type=F32)
        if nk == 1:
            if scale is not None:
                p = p * scale
            o_ref[...] = p.astype(o_dtype)
            return
        acc = scratch[0] if use_acc else o_ref
        k = pl.program_id(kaxis)

        @pl.when(k == 0)
        def _():
            acc[...] = p

        @pl.when(k > 0)
        def _():
            acc[...] += p

        if use_acc or scale is not None:
            @pl.when(k == nk - 1)
            def _():
                r = acc[...]
                if scale is not None:
                    r = r * scale
                o_ref[...] = r.astype(o_dtype)

    in_specs = [pl.BlockSpec(a_blk, a_map), pl.BlockSpec(b_blk, b_map)]
    args = [a, b]
    if into is not None:
        assert into.shape == tuple(o_shape) and into.dtype == o_dtype
        in_specs.append(ANY)
        args.append(into)
    return pl.pallas_call(
        body, grid=grid, name=name, in_specs=in_specs,
        out_specs=pl.BlockSpec(o_blk, o_map),
        out_shape=jax.ShapeDtypeStruct(o_shape, o_dtype),
        input_output_aliases={2: 0} if into is not None else {},
        scratch_shapes=[pltpu.VMEM(acc_shape, F32)] if use_acc else [],
        compiler_params=_cp(len(grid), kaxis),
    )(*args)


def _mm_shards_nn(name, a, wall, layer, o_dtype):
    t, k = a.shape
    _, s, _, n = wall.shape
    tm = _tile(t, 512)
    return _mm(name, a, wall, mode="nn", grid=(s, t // tm), kaxis=None,
               a_blk=(tm, k), a_map=lambda j, i: (i, 0),
               b_blk=(None, None, k, n), b_map=lambda j, i: (layer, j, 0, 0),
               o_shape=(t, s * n), o_blk=(tm, n), o_map=lambda j, i: (i, j), o_dtype=o_dtype)


def _mm_shards_nt(name, g, wall, layer):
    t = g.shape[0]
    _, s, k, n = wall.shape
    tm = _tile(t, 512)
    return _mm(name, g, wall, mode="nt", grid=(t // tm, s), kaxis=1,
               a_blk=(tm, n), a_map=lambda i, kk: (i, kk),
               b_blk=(None, None, k, n), b_map=lambda i, kk: (layer, kk, 0, 0),
               o_shape=(t, k), o_blk=(tm, k), o_map=lambda i, kk: (i, 0))


def _mm_shards_tn(name, a, g, layer, into):
    t, k = a.shape
    _, s, _, n = into.shape
    tk = _tile(t, 512)
    return _mm(name, a, g, mode="tn", grid=(s, t // tk), kaxis=1,
               a_blk=(tk, k), a_map=lambda j, kk: (kk, 0),
               b_blk=(tk, n), b_map=lambda j, kk: (kk, j),
               o_shape=into.shape, o_blk=(None, None, k, n), o_map=lambda j, kk: (layer, j, 0, 0),
               o_dtype=into.dtype, into=into)


def _mm_nn(name, a, w, o_dtype=F32, tn=None):
    t, k = a.shape
    n = w.shape[1]
    tm = _tile(t, 512)
    tn = n if tn is None else tn
    return _mm(name, a, w, mode="nn", grid=(n // tn, t // tm), kaxis=None,
               a_blk=(tm, k), a_map=lambda j, i: (i, 0),
               b_blk=(k, tn), b_map=lambda j, i: (0, j),
               o_shape=(t, n), o_blk=(tm, tn), o_map=lambda j, i: (i, j), o_dtype=o_dtype)


def _mm_nt(name, g, w, layer=None, o_dtype=F32):
    t, k = g.shape
    n = w.shape[-2]
    tm = _tile(t, 512)
    b_blk, b_map = ((n, k), lambda i: (0, 0)) if layer is None else ((None, n, k), lambda i: (layer, 0, 0))
    return _mm(name, g, w, mode="nt", grid=(t // tm,), kaxis=None,
               a_blk=(tm, k), a_map=lambda i: (i, 0), b_blk=b_blk, b_map=b_map,
               o_shape=(t, n), o_blk=(tm, n), o_map=lambda i: (i, 0), o_dtype=o_dtype)


def _mm_tn(name, a, g, tm=None, tn=None, scale=None, layer=None, into=None):
    t, m = a.shape
    n = g.shape[1]
    tk = _tile(t, 512)
    tm = m if tm is None else tm
    tn = n if tn is None else tn
    if layer is None:
        o_shape, o_blk, o_map, o_dtype = (m, n), (tm, tn), lambda i, j, kk: (i, j), F32
    else:
        o_shape, o_blk, o_map, o_dtype = into.shape, (None, tm, tn), lambda i, j, kk: (layer, i, j), into.dtype
    return _mm(name, a, g, mode="tn", grid=(m // tm, n // tn, t // tk), kaxis=2,
               a_blk=(tk, tm), a_map=lambda i, j, kk: (kk, i),
               b_blk=(tk, tn), b_map=lambda i, j, kk: (kk, j),
               o_shape=o_shape, o_blk=o_blk, o_map=o_map, o_dtype=o_dtype, scale=scale, into=into)


def _sigmoid(x):
    return 1.0 / (1.0 + jnp.exp(-x))


def _rows_call(name, body, t, tm, ins, in_cols, outs, acc_outs=()):
    in_specs = []
    for x, c in zip(ins, in_cols):
        if x.shape[0] == 1:
            in_specs.append(pl.BlockSpec((1, c), lambda i: (0, 0)))
        else:
            in_specs.append(pl.BlockSpec((tm, c), lambda i: (i, 0)))
    out_specs = [pl.BlockSpec((tm, s.shape[1]), lambda i: (i, 0)) for s in outs]
    out_specs += [pl.BlockSpec((1, s.shape[1]), lambda i: (0, 0)) for s in acc_outs]
    return pl.pallas_call(
        body, grid=(t // tm,), name=name, in_specs=in_specs, out_specs=out_specs,
        out_shape=list(outs) + list(acc_outs), compiler_params=_cp(1),
    )(*ins)


def _ln_fwd(name, alpha, x, r, coef, g, b):
    t, d = x.shape
    tm = _tile(t, 256)

    def body(x_ref, r_ref, g_ref, b_ref, y_ref, xh_ref, rs_ref):
        z = alpha * x_ref[...] + coef * r_ref[...]
        mu = jnp.mean(z, axis=-1, keepdims=True)
        zc = z - mu
        var = jnp.mean(zc * zc, axis=-1, keepdims=True)
        rstd = lax.rsqrt(var + LN_EPS)
        xh = zc * rstd
        y_ref[...] = xh * g_ref[...] + b_ref[...]
        xh_ref[...] = xh
        rs_ref[...] = rstd

    sd = jax.ShapeDtypeStruct
    return _rows_call(name, body, t, tm, [x, r, g, b], [d, d, d, d],
                      [sd((t, d), F32), sd((t, d), F32), sd((t, 1), F32)])


def _ln_bwd(name, terms, xhat, rstd, g):
    t, d = xhat.shape
    tm = _tile(t, 256)
    n = len(terms)
    coefs = [c for _, c in terms]

    def body(*refs):
        t_refs = refs[:n]
        xh_ref, rs_ref, g_ref, dz_ref, dg_ref, db_ref = refs[n:]
        dy = coefs[0] * t_refs[0][...]
        for c, r in zip(coefs[1:], t_refs[1:]):
            dy = dy + c * r[...]
        xh = xh_ref[...]
        dxh = dy * g_ref[...]
        m1 = jnp.mean(dxh, axis=-1, keepdims=True)
        m2 = jnp.mean(dxh * xh, axis=-1, keepdims=True)
        dz_ref[...] = rs_ref[...] * (dxh - m1 - xh * m2)
        pg = jnp.sum(dy * xh, axis=0, keepdims=True)
        pb = jnp.sum(dy, axis=0, keepdims=True)
        i = pl.program_id(0)

        @pl.when(i == 0)
        def _():
            dg_ref[...] = pg
            db_ref[...] = pb

        @pl.when(i > 0)
        def _():
            dg_ref[...] += pg
            db_ref[...] += pb

    sd = jax.ShapeDtypeStruct
    arrs = [a for a, _ in terms] + [xhat, rstd, g]
    cols = [d] * n + [d, 1, d]
    return _rows_call(name, body, t, tm, arrs, cols, [sd((t, d), F32)], [sd((1, d), F32), sd((1, d), F32)])


def _loss_sum(name, y, tgt):
    t, d = y.shape
    tm = _tile(t, 256)

    def body(y_ref, t_ref, o_ref):
        e = y_ref[...] - t_ref[...]
        s = jnp.sum(jnp.sum(e * e, axis=1, keepdims=True), axis=0, keepdims=True)
        i = pl.program_id(0)

        @pl.when(i == 0)
        def _():
            o_ref[...] = s

        @pl.when(i > 0)
        def _():
            o_ref[...] += s

    return _rows_call(name, body, t, tm, [y, tgt], [d, d], [], [jax.ShapeDtypeStruct((1, 1), F32)])[0]


def _ffn_in(name, x, wall, layer):
    t, k = x.shape
    n = wall.shape[3]
    tm = _tile(t, 512)

    def body(x_ref, wg_ref, wu_ref, h_ref, a_ref):
        xb = x_ref[...].astype(BF16)
        g = lax.dot_general(xb, wg_ref[...], _NN, preferred_element_type=F32)
        u = lax.dot_general(xb, wu_ref[...], _NN, preferred_element_type=F32)
        h_ref[0] = g.astype(BF16)
        h_ref[1] = u.astype(BF16)
        a_ref[...] = (g * _sigmoid(g) * u).astype(BF16)

    return pl.pallas_call(
        body, grid=(2, t // tm), name=name,
        in_specs=[pl.BlockSpec((tm, k), lambda j, i: (i, 0)),
                  pl.BlockSpec((None, None, k, n), lambda j, i: (layer, j, 0, 0)),
                  pl.BlockSpec((None, None, k, n), lambda j, i: (layer, 2 + j, 0, 0))],
        out_specs=[pl.BlockSpec((2, tm, n), lambda j, i: (0, i, j)), pl.BlockSpec((tm, n), lambda j, i: (i, j))],
        out_shape=[jax.ShapeDtypeStruct((2, t, 2 * n), BF16), jax.ShapeDtypeStruct((t, 2 * n), BF16)],
        compiler_params=_cp(2),
    )(x, wall, wall)


def _ffn_da(name, dz, w3, layer, h, zrow):
    t, k = dz.shape
    f = w3.shape[1]
    n = f // 2
    tm = _tile(t, 512)

    def body(dz_ref, z_ref, w_ref, g_ref, u_ref, dh_ref):
        d = 0.5 * lax.dot_general((dz_ref[...] + z_ref[...]).astype(BF16), w_ref[...], _NT, preferred_element_type=F32)
        g = g_ref[...].astype(F32)
        u = u_ref[...].astype(F32)
        sg = _sigmoid(g)
        dh_ref[0] = (d * u * sg * (1.0 + g * (1.0 - sg))).astype(BF16)
        dh_ref[1] = (d * g * sg).astype(BF16)

    return pl.pallas_call(
        body, grid=(2, t // tm), name=name,
        in_specs=[pl.BlockSpec((tm, k), lambda j, i: (i, 0)),
                  pl.BlockSpec((1, k), lambda j, i: (0, 0)),
                  pl.BlockSpec((None, n, k), lambda j, i: (layer, j, 0)),
                  pl.BlockSpec((None, tm, n), lambda j, i: (0, i, j)),
                  pl.BlockSpec((None, tm, n), lambda j, i: (1, i, j))],
        out_specs=pl.BlockSpec((2, tm, n), lambda j, i: (0, i, j)),
        out_shape=jax.ShapeDtypeStruct((2, t, f), BF16),
        compiler_params=_cp(2),
    )(dz, zrow, w3, h, h)


def _mm_ln(name, alpha, coef, a, w3, layer, x, g, b):
    t, k = a.shape
    d = w3.shape[2]
    tm = _tile(t, 512)

    def body(a_ref, w_ref, x_ref, g_ref, b_ref, y_ref, xh_ref, rs_ref):
        f = lax.dot_general(a_ref[...].astype(BF16), w_ref[...], _NN, preferred_element_type=F32)
        z = alpha * x_ref[...] + coef * f
        mu = jnp.mean(z, axis=-1, keepdims=True)
        zc = z - mu
        var = jnp.mean(zc * zc, axis=-1, keepdims=True)
        rstd = lax.rsqrt(var + LN_EPS)
        xh = zc * rstd
        y_ref[...] = xh * g_ref[...] + b_ref[...]
        xh_ref[...] = xh
        rs_ref[...] = rstd

    row = lambda c: pl.BlockSpec((tm, c), lambda i: (i, 0))
    vec = pl.BlockSpec((1, d), lambda i: (0, 0))
    sd = jax.ShapeDtypeStruct
    return pl.pallas_call(
        body, grid=(t // tm,), name=name,
        in_specs=[row(k), pl.BlockSpec((None, k, d), lambda i: (layer, 0, 0)), row(d), vec, vec],
        out_specs=[row(d), row(d), row(1)],
        out_shape=[sd((t, d), F32), sd((t, d), F32), sd((t, 1), F32)],
        compiler_params=_cp(1),
    )(a, w3, x, g, b)


_GELU_C = math.sqrt(2.0 / math.pi)


def _s5_act_fwd(name, ych, u, dvec):
    t, d = u.shape
    tm = _tile(t, 256)

    def body(y_ref, u_ref, d_ref, p_ref, a_ref):
        y = y_ref[...] + d_ref[...] * u_ref[...]
        p_ref[...] = y
        a_ref[...] = (0.5 * y * (1.0 + jnp.tanh(_GELU_C * (y + 0.044715 * y * y * y)))).astype(BF16)

    sd = jax.ShapeDtypeStruct
    return _rows_call(name, body, t, tm, [ych, u, dvec], [d, d, d], [sd((t, d), F32), sd((t, d), BF16)])


def _s5_act_bwd(name, dact, ypre, u, dvec):
    t, d = u.shape
    tm = _tile(t, 256)

    def body(da_ref, y_ref, u_ref, d_ref, dy_ref, ds_ref, dd_ref):
        y = y_ref[...]
        th = jnp.tanh(_GELU_C * (y + 0.044715 * y * y * y))
        dg = 0.5 * (1.0 + th) + 0.5 * y * (1.0 - th * th) * _GELU_C * (1.0 + 3.0 * 0.044715 * y * y)
        dy = da_ref[...] * dg
        dy_ref[...] = dy
        ds_ref[...] = dy * d_ref[...]
        pd = jnp.sum(dy * u_ref[...], axis=0, keepdims=True)
        i = pl.program_id(0)

        @pl.when(i == 0)
        def _():
            dd_ref[...] = pd

        @pl.when(i > 0)
        def _():
            dd_ref[...] += pd

    sd = jax.ShapeDtypeStruct
    return _rows_call(name, body, t, tm, [dact, ypre, u, dvec], [d, d, d, d],
                      [sd((t, d), F32), sd((t, d), F32)], [sd((1, d), F32)])


def _glu_fwd(name, vg):
    t, d2 = vg.shape
    d = d2 // 2
    tm = _tile(t, 256)

    def body(vg_ref, m_ref):
        m_ref[...] = vg_ref[:, :d] * _sigmoid(vg_ref[:, d:])

    return _rows_call(name, body, t, tm, [vg], [d2], [jax.ShapeDtypeStruct((t, d), F32)])[0]


def _glu_bwd(name, dm, vg):
    t, d2 = vg.shape
    d = d2 // 2
    tm = _tile(t, 256)

    def body(dm_ref, vg_ref, o_ref):
        sg = _sigmoid(vg_ref[:, d:])
        g = dm_ref[...]
        o_ref[:, :d] = (g * sg).astype(BF16)
        o_ref[:, d:] = (g * vg_ref[:, :d] * sg * (1.0 - sg)).astype(BF16)

    return _rows_call(name, body, t, tm, [dm, vg], [d, d2], [jax.ShapeDtypeStruct((t, d2), BF16)])[0]


def _adamw(name, w, g, m, v):
    r, c = w.shape
    tr = _tile(r, max(V7X_SUBLANES, (1 << 20) // (4 * c) // V7X_SUBLANES * V7X_SUBLANES))

    def body(w_ref, g_ref, m_ref, v_ref, d_ref, nm_ref, nv_ref):
        gg = g_ref[...]
        nm = ADAM_B1 * m_ref[...] + (1.0 - ADAM_B1) * gg
        nv = ADAM_B2 * v_ref[...] + (1.0 - ADAM_B2) * (gg * gg)
        m_hat = nm / (1.0 - ADAM_B1 ** ADAM_STEP)
        v_hat = nv / (1.0 - ADAM_B2 ** ADAM_STEP)
        d_ref[...] = -ADAM_LR * (m_hat / (jnp.sqrt(v_hat) + ADAM_EPS) + ADAM_WD * w_ref[...])
        nm_ref[...] = nm
        nv_ref[...] = nv

    sd = jax.ShapeDtypeStruct((r, c), F32)
    return _rows_call(name, body, r, tr, [w, g, m, v], [c] * 4, [sd, sd, sd])


def _my_shard():
    return 2 * lax.axis_index("x") + lax.axis_index("y")


def _my_core():
    return lax.axis_index("c")


def _adamw_join(name, w, mine, theirs, m, v):
    nl, r, c = w.shape
    h = r // 2
    tr = _tile(h, max(V7X_SUBLANES, (1 << 20) // (4 * c) // V7X_SUBLANES * V7X_SUBLANES))
    nb = h // tr

    def body(w_ref, a_ref, b_ref, m_ref, v_ref, g_ref, d_ref, nm_ref, nv_ref):
        gg = jnp.where(pl.program_id(1) == _my_core(), a_ref[...], b_ref[...])
        nm = ADAM_B1 * m_ref[...] + (1.0 - ADAM_B1) * gg
        nv = ADAM_B2 * v_ref[...] + (1.0 - ADAM_B2) * (gg * gg)
        m_hat = nm / (1.0 - ADAM_B1 ** ADAM_STEP)
        v_hat = nv / (1.0 - ADAM_B2 ** ADAM_STEP)
        g_ref[...] = gg
        d_ref[...] = -ADAM_LR * (m_hat / (jnp.sqrt(v_hat) + ADAM_EPS) + ADAM_WD * w_ref[...])
        nm_ref[...] = nm
        nv_ref[...] = nv

    full = pl.BlockSpec((None, tr, c), lambda l, hf, i: (l, hf * nb + i, 0))
    sd = jax.ShapeDtypeStruct((nl, r, c), F32)
    return pl.pallas_call(
        body, name=name, grid=(nl, 2, nb),
        in_specs=[full,
                  pl.BlockSpec((None, tr, c), lambda l, hf, i: (l, jnp.where(hf == _my_core(), i, 0), 0)),
                  pl.BlockSpec((None, tr, c), lambda l, hf, i: (l, jnp.where(hf == _my_core(), 0, i), 0)),
                  full, full],
        out_specs=[full, full, full, full],
        out_shape=[sd, sd, sd, sd],
        compiler_params=_cp(3),
    )(w, mine, theirs, m, v)


def _split3(x):
    hi = x.astype(BF16)
    r1 = x - hi.astype(F32)
    mid = r1.astype(BF16)
    lo = (r1 - mid.astype(F32)).astype(BF16)
    return hi, mid, lo


def _tri_sum(tri, x):
    dims = (((1,), (0,)), ((), ()))
    hi, mid, lo = _split3(x)
    out = lax.dot_general(tri, lo, dims, preferred_element_type=F32)
    out = out + lax.dot_general(tri, mid, dims, preferred_element_type=F32)
    return out + lax.dot_general(tri, hi, dims, preferred_element_type=F32)


def _fox_cumsum(name, fl, bf):
    t, h = fl.shape
    tb = _tile(t, 512)

    def body(fl_ref, bf_ref, c_ref, carry):
        i = pl.program_id(0)

        @pl.when(i == 0)
        def _():
            carry[...] = jnp.zeros_like(carry)

        x = fl_ref[...] + bf_ref[...]
        lf = jnp.minimum(x, 0.0) - jnp.log(1.0 + jnp.exp(-jnp.abs(x)))
        row = lax.broadcasted_iota(jnp.int32, (tb, tb), 0)
        col = lax.broadcasted_iota(jnp.int32, (tb, tb), 1)
        tri = jnp.where(row >= col, 1.0, 0.0).astype(BF16)
        c_ref[...] = _tri_sum(tri, lf) + carry[...]
        carry[...] += jnp.sum(lf, axis=0, keepdims=True)

    return pl.pallas_call(
        body, grid=(t // tb,), name=name,
        in_specs=[pl.BlockSpec((tb, h), lambda i: (i, 0)), pl.BlockSpec((1, h), lambda i: (0, 0))],
        out_specs=pl.BlockSpec((tb, h), lambda i: (i, 0)),
        out_shape=jax.ShapeDtypeStruct((t, h), F32),
        scratch_shapes=[pltpu.VMEM((1, h), F32)], compiler_params=_cp(1),
    )(fl, bf)


def _fox_cumsum_bwd(name, dcum, fl, bf):
    t, h = fl.shape
    tb = _tile(t, 512)
    nb = t // tb

    def body(dc_ref, fl_ref, bf_ref, df_ref, db_ref, carry):
        i = pl.program_id(0)

        @pl.when(i == 0)
        def _():
            carry[...] = jnp.zeros_like(carry)

        dc = dc_ref[...]
        row = lax.broadcasted_iota(jnp.int32, (tb, tb), 0)
        col = lax.broadcasted_iota(jnp.int32, (tb, tb), 1)
        tri = jnp.where(row <= col, 1.0, 0.0).astype(BF16)
        dlf = _tri_sum(tri, dc) + carry[...]
        carry[...] += jnp.sum(dc, axis=0, keepdims=True)
        x = fl_ref[...] + bf_ref[...]
        df = dlf / (1.0 + jnp.exp(x))
        df_ref[...] = df
        pb = jnp.sum(df, axis=0, keepdims=True)

        @pl.when(i == 0)
        def _():
            db_ref[...] = pb

        @pl.when(i > 0)
        def _():
            db_ref[...] += pb

    rev = lambda i: (nb - 1 - i, 0)
    return pl.pallas_call(
        body, grid=(nb,), name=name,
        in_specs=[pl.BlockSpec((tb, h), rev), pl.BlockSpec((tb, h), rev), pl.BlockSpec((1, h), lambda i: (0, 0))],
        out_specs=[pl.BlockSpec((tb, h), rev), pl.BlockSpec((1, h), lambda i: (0, 0))],
        out_shape=[jax.ShapeDtypeStruct((t, h), F32), jax.ShapeDtypeStruct((1, h), F32)],
        scratch_shapes=[pltpu.VMEM((1, h), F32)], compiler_params=_cp(1),
    )(dcum, fl, bf)


_NT = (((1,), (1,)), ((), ()))
_TN = (((0,), (0,)), ((), ()))
_NN = (((1,), (0,)), ((), ()))


def _causal_mask(s, r0):
    row = lax.broadcasted_iota(jnp.int32, s.shape, 0) + r0
    col = lax.broadcasted_iota(jnp.int32, s.shape, 1)
    return jnp.where(col <= row, s, NEG_INF)


def _first_head_lanes(hd):
    return lax.broadcasted_iota(jnp.int32, (1, 2 * hd), 1) < hd


def _attn_fwd(name, qkv, ccol, crow, nh):
    nb, tb, d3 = qkv.shape
    d = d3 // 3
    hd = d // nh
    lanes = 2 * hd
    assert lanes == V7X_LANES
    scale = 1.0 / math.sqrt(hd)

    def body(q_ref, k_ref, v_ref, cc_ref, cr_ref, o_ref, lse_ref):
        i = pl.program_id(1)
        first = _first_head_lanes(hd)
        q = q_ref[...] * scale
        res = []
        for hh in (0, 1):
            qh = jnp.where(first if hh == 0 else jnp.logical_not(first), q, jnp.zeros_like(q))
            cc = cc_ref[:, hh:hh + 1]

            def tile(j, carry, r0=0, nr=tb, ncol=tb, masked=False, qh=qh, cc=cc, hh=hh):
                m, l, acc = carry
                s = lax.dot_general(qh[r0:r0 + nr], k_ref[j, 0:ncol, :], _NT, preferred_element_type=F32)
                s = s + cc[r0:r0 + nr] - cr_ref[j][hh:hh + 1, 0:ncol]
                if masked:
                    s = _causal_mask(s, r0)
                m_new = jnp.maximum(m, jnp.max(s, axis=1, keepdims=True))
                p = jnp.exp(s - m_new)
                a = jnp.exp(m - m_new)
                l = a * l + jnp.sum(p, axis=1, keepdims=True)
                acc = a * acc + lax.dot_general(p.astype(BF16), v_ref[j, 0:ncol, :], _NN, preferred_element_type=F32)
                return m_new, l, acc

            init = (jnp.full((tb, 1), NEG_INF, F32), jnp.zeros((tb, 1), F32), jnp.zeros((tb, lanes), F32))
            m, l, acc = tile(i, lax.fori_loop(0, i, tile, init), masked=True)
            res.append((acc / l, m + jnp.log(l)))
        o_ref[...] = jnp.where(first, res[0][0], res[1][0])
        lse_ref[:, 0:1] = res[0][1]
        lse_ref[:, 1:2] = res[1][1]

    kb, vb = d // lanes, 2 * d // lanes
    return pl.pallas_call(
        body, grid=(nh // 2, nb), name=name,
        in_specs=[pl.BlockSpec((None, tb, lanes), lambda h, i: (i, 0, h)),
                  pl.BlockSpec((nb, tb, lanes), lambda h, i: (0, 0, kb + h)),
                  pl.BlockSpec((nb, tb, lanes), lambda h, i: (0, 0, vb + h)),
                  pl.BlockSpec((None, None, tb, 2), lambda h, i: (h, i, 0, 0)),
                  pl.BlockSpec((None, nb, 2, tb), lambda h, i: (h, 0, 0, 0))],
        out_specs=[pl.BlockSpec((None, tb, lanes), lambda h, i: (i, 0, h)),
                   pl.BlockSpec((None, None, tb, 2), lambda h, i: (h, i, 0, 0))],
        out_shape=[jax.ShapeDtypeStruct((nb, tb, d), F32), jax.ShapeDtypeStruct((nh // 2, nb, tb, 2), F32)],
        compiler_params=_cp(2),
    )(qkv, qkv, qkv, ccol, crow)


def _attn_bwd(name, qkv, ccol, crow, o, lse, do, nh):
    nb, tb, d3 = qkv.shape
    d = d3 // 3
    hd = d // nh
    lanes = 2 * hd
    hb = tb // 2
    scale = 1.0 / math.sqrt(hd)

    def body(q_ref, k_ref, v_ref, cc_ref, cr_ref, o_ref, lse_ref, do_ref, dq_ref, dk_ref, dv_ref, dr_ref, dc_ref, dq_acc):
        j = pl.program_id(1)

        @pl.when(j == 0)
        def _():
            dq_acc[...] = jnp.zeros_like(dq_acc)
            dr_ref[...] = jnp.zeros_like(dr_ref)

        first = _first_head_lanes(hd)
        kj = k_ref[...]
        vj = v_ref[...]
        dk = jnp.zeros((tb, lanes), F32)
        dv = jnp.zeros((tb, lanes), F32)
        for hh in (0, 1):
            mine = first if hh == 0 else jnp.logical_not(first)
            cr = cr_ref[hh:hh + 1, :]

            def tile(i, carry, r0=0, nr=tb, ncol=tb, masked=False, mine=mine, cr=cr, hh=hh):
                dk, dv, dc = carry
                rows = pl.ds(r0, nr)
                qi = q_ref[i, rows, :] * scale
                qh = jnp.where(mine, qi, jnp.zeros_like(qi))
                doh = jnp.where(mine, do_ref[i, rows, :], 0.0)
                dob = doh.astype(BF16)
                di = jnp.sum(doh * o_ref[i, rows, :], axis=1, keepdims=True)
                s = lax.dot_general(qh, kj[:ncol], _NT, preferred_element_type=F32)
                s = s + cc_ref[i, rows, hh:hh + 1] - cr[:, :ncol]
                if masked:
                    s = _causal_mask(s, r0)
                p = jnp.exp(s - lse_ref[i, rows, hh:hh + 1])
                dp = lax.dot_general(dob, vj[:ncol], _NT, preferred_element_type=F32)
                ds = p * (dp - di)
                dsb = ds.astype(BF16)
                dvc = lax.dot_general(p.astype(BF16), dob, _TN, preferred_element_type=F32)
                dkc = lax.dot_general(dsb, qh, _TN, preferred_element_type=F32)
                dcc = jnp.sum(ds, axis=0, keepdims=True)
                if ncol < tb:
                    dvc = jnp.concatenate([dvc, jnp.zeros((tb - ncol, lanes), F32)], axis=0)
                    dkc = jnp.concatenate([dkc, jnp.zeros((tb - ncol, lanes), F32)], axis=0)
                    dcc = jnp.concatenate([dcc, jnp.zeros((1, tb - ncol), F32)], axis=1)
                dq = lax.dot_general(dsb, kj[:ncol], _NN, preferred_element_type=F32) * scale
                dq_acc[i, rows, :] += jnp.where(mine, dq, 0.0)
                dr_ref[i, rows, hh:hh + 1] += jnp.sum(ds, axis=1, keepdims=True)
                return dk + dkc, dv + dvc, dc + dcc

            carry = tile(j, (dk, dv, jnp.zeros((1, tb), F32)), 0, hb, hb, True)
            carry = tile(j, carry, hb, hb, tb, True)
            dk, dv, dc = lax.fori_loop(j + 1, nb, tile, carry)
            dc_ref[hh:hh + 1, :] = dc
        dk_ref[...] = dk.astype(BF16)
        dv_ref[...] = dv.astype(BF16)

        @pl.when(j == nb - 1)
        def _():
            dq_ref[...] = dq_acc[...].astype(BF16)

    kb, vb = d // lanes, 2 * d // lanes
    whole = lambda c: pl.BlockSpec((nb, tb, lanes), lambda h, j: (0, 0, c + h))
    block = lambda c: pl.BlockSpec((None, tb, lanes), lambda h, j: (j, 0, c + h))
    cols = pl.BlockSpec((None, nb, tb, 2), lambda h, j: (h, 0, 0, 0))
    rows = pl.BlockSpec((None, None, 2, tb), lambda h, j: (h, j, 0, 0))
    sd = jax.ShapeDtypeStruct
    return pl.pallas_call(
        body, grid=(nh // 2, nb), name=name,
        in_specs=[whole(0), block(kb), block(vb), cols, rows, whole(0), cols, whole(0)],
        out_specs=[whole(0), block(0), block(0), cols, rows],
        out_shape=[sd((nb, tb, d), BF16), sd((nb, tb, d), BF16), sd((nb, tb, d), BF16),
                   sd((nh // 2, nb, tb, 2), F32), sd((nh // 2, nb, 2, tb), F32)],
        scratch_shapes=[pltpu.VMEM((nb, tb, lanes), F32)],
        compiler_params=_cp(2),
    )(qkv, qkv, qkv, ccol, crow, o, lse, do)


def _cmul(ar, ai, br, bi):
    return ar * br - ai * bi, ar * bi + ai * br


def _s5_scan(name, lam, xin, hs=None):
    reverse = hs is not None
    _, seg, ns, w = xin.shape
    assert ns == SCAN_SEGMENTS
    wb = min(w, 2 * V7X_LANES)
    nsq = seg.bit_length() - 1
    assert (1 << nsq) == seg

    def body(*refs):
        if reverse:
            lam_ref, x_ref, h_ref, o_ref, dl_ref = refs
        else:
            lam_ref, x_ref, o_ref = refs
        lr = jnp.broadcast_to(lam_ref[0], (ns, wb))
        li = jnp.broadcast_to(lam_ref[1], (ns, wb))
        if reverse:
            li = -li
        zero = jnp.zeros((ns, wb), F32)
        at = (lambda n: seg - 1 - n) if reverse else (lambda n: n)

        def local(n, c):
            r = at(n)
            mr, mi = _cmul(lr, li, c[0], c[1])
            nr = mr + x_ref[0, r]
            ni = mi + x_ref[1, r]
            o_ref[0, r] = nr
            o_ref[1, r] = ni
            return nr, ni

        er, ei = lax.fori_loop(0, seg, local, (zero, zero))
        pr, pi = lr, li
        for _ in range(nsq):
            pr, pi = _cmul(pr, pi, pr, pi)
        sub = lax.broadcasted_iota(jnp.int32, (ns, wb), 0)

        def shifted(a, sh):
            if reverse:
                return jnp.where(sub < ns - sh, pltpu.roll(a, ns - sh, 0), 0.0)
            return jnp.where(sub >= sh, pltpu.roll(a, sh, 0), 0.0)

        xr, xi = er, ei
        sh = 1
        while sh < ns:
            tr, ti = _cmul(pr, pi, shifted(xr, sh), shifted(xi, sh))
            xr, xi = xr + tr, xi + ti
            pr, pi = _cmul(pr, pi, pr, pi)
            sh *= 2
        cr, ci = shifted(xr, 1), shifted(xi, 1)

        def fix(r, q):
            tr, ti = _cmul(q[0], q[1], cr, ci)
            gr = o_ref[0, r] + tr
            gi = o_ref[1, r] + ti
            o_ref[0, r] = gr
            o_ref[1, r] = gi
            return gr, gi

        if not reverse:
            def fixup(n, q):
                fix(n, q)
                return _cmul(q[0], q[1], lr, li)

            lax.fori_loop(0, seg, fixup, (lr, li))
            return

        def fixup_acc(n, c):
            qr, qi, ar, ai = c
            r = seg - 1 - n
            gr, gi = fix(r, (qr, qi))
            hr = h_ref[0, r - 1]
            hi = h_ref[1, r - 1]
            qr, qi = _cmul(qr, qi, lr, li)
            return qr, qi, ar + gr * hr + gi * hi, ai + gi * hr - gr * hi

        qr, qi, ar, ai = lax.fori_loop(0, seg - 1, fixup_acc, (lr, li, zero, zero))
        gr, gi = fix(0, (qr, qi))
        hr = jnp.where(sub >= 1, pltpu.roll(h_ref[0, seg - 1], 1, 0), 0.0)
        hi = jnp.where(sub >= 1, pltpu.roll(h_ref[1, seg - 1], 1, 0), 0.0)
        dl_ref[0] = ar + gr * hr + gi * hi
        dl_ref[1] = ai + gi * hr - gr * hi

    big = pl.BlockSpec((2, seg, ns, wb), lambda j: (0, 0, 0, j))
    lam_spec = pl.BlockSpec((2, 1, wb), lambda j: (0, 0, j))
    sd = jax.ShapeDtypeStruct
    if reverse:
        return pl.pallas_call(
            body, grid=(w // wb,), name=name, in_specs=[lam_spec, big, big],
            out_specs=[big, pl.BlockSpec((2, ns, wb), lambda j: (0, 0, j))],
            out_shape=[sd(xin.shape, F32), sd((2, ns, w), F32)], compiler_params=_cp(1),
        )(lam, xin, hs)
    return pl.pallas_call(
        body, grid=(w // wb,), name=name, in_specs=[lam_spec, big], out_specs=big,
        out_shape=sd(xin.shape, F32), compiler_params=_cp(1),
    )(lam, xin)


def _place():
    x, y, c = lax.axis_index("x"), lax.axis_index("y"), lax.axis_index("c")
    chips = [(1 - x, y), (x, 1 - y), (1 - x, 1 - y)]
    return x, y, c, chips


def _comm_params():
    return pltpu.CompilerParams(vmem_limit_bytes=VMEM_LIMIT)


def _cast_place(name, w, after=None):
    nl, r, c = w.shape
    tr = _tile(r, max(16, (1 << 20) // (4 * c) // 16 * 16), 16)

    def body(w_ref, *rest):
        rest[-1][...] = w_ref[...].astype(BF16)

    return pl.pallas_call(
        body, name=name, grid=(nl, r // tr),
        in_specs=[pl.BlockSpec((None, tr, c), lambda l, i: (l, i, 0))] + ([ANY] if after is not None else []),
        out_specs=pl.BlockSpec((None, None, tr, c), lambda l, i: (l, _my_shard(), i, 0)),
        out_shape=jax.ShapeDtypeStruct((nl, N_CHIPS, r, c), BF16),
        compiler_params=_cp(2),
    )(*([w] if after is None else [w, after]))


def _gather_shards(name, bufs):
    n = len(bufs)

    def body(*refs):
        outs = refs[n:2 * n]
        send_sems, recv_sems = refs[2 * n:]
        x, y, c, chips = _place()
        my = 2 * x + y
        sibling = (x, y, 1 - c)

        def part(t, shard, half):
            h = bufs[t].shape[2] // 2
            return outs[t].at[:, shard, pl.ds(half * h, h)]

        def copy(t, k, ref, to):
            return pltpu.make_async_remote_copy(src_ref=ref, dst_ref=ref, send_sem=send_sems.at[t, k],
                                                recv_sem=recv_sems.at[t, k], device_id=to, device_id_type=MESH)

        sent = []
        for t in range(n):
            for k, chip in enumerate(chips):
                sent.append(copy(t, k, part(t, my, c), (*chip, c)))
                sent[-1].start()
        for k, chip in enumerate(chips):
            shard = 2 * chip[0] + chip[1]
            for t in range(n):
                copy(t, k, part(t, shard, c), (*chip, c)).wait_recv()
                sent.append(copy(t, 3 + k, part(t, shard, c), sibling))
                sent[-1].start()
        for k, chip in enumerate(chips):
            shard = 2 * chip[0] + chip[1]
            for t in range(n):
                copy(t, 3 + k, part(t, shard, 1 - c), sibling).wait_recv()
        for cp in sent:
            cp.wait_send()

    return pl.pallas_call(
        body, name=name, in_specs=[ANY] * n, out_specs=[ANY] * n,
        out_shape=[jax.ShapeDtypeStruct(b.shape, b.dtype) for b in bufs],
        input_output_aliases={t: t for t in range(n)},
        scratch_shapes=[pltpu.SemaphoreType.DMA((n, 6)), pltpu.SemaphoreType.DMA((n, 6))],
        compiler_params=_comm_params(),
    )(*bufs)


HBM_SPEC = pl.BlockSpec(memory_space=pltpu.HBM)
SEM_SPEC = pl.BlockSpec(memory_space=pltpu.SEMAPHORE)


def _split_params():
    return pltpu.CompilerParams(has_side_effects=pltpu.SideEffectType.DATAFLOW_SIDE_EFFECTING)


def _gather_start(name, bufs, groups):
    n, ng = len(bufs), len(groups)

    def body(*refs):
        sems = refs[n:n + 2 * ng]
        outs = refs[n + 2 * ng:]
        x, y, c, chips = _place()
        my = 2 * x + y
        for gi, group in enumerate(groups):
            for idx, (t, layer) in enumerate(group):
                block = outs[t].at[layer, my]
                for k, chip in enumerate(chips):
                    pltpu.make_async_remote_copy(
                        src_ref=block, dst_ref=block, send_sem=sems[2 * gi].at[3 * idx + k],
                        recv_sem=sems[2 * gi + 1].at[3 * idx + k], device_id=(*chip, c), device_id_type=MESH).start()

    sem_shapes = []
    for group in groups:
        sem_shapes += [pltpu.SemaphoreType.DMA((3 * len(group),))] * 2
    res = pl.pallas_call(
        body, name=name, in_specs=[HBM_SPEC] * n,
        out_specs=[SEM_SPEC] * (2 * ng) + [HBM_SPEC] * n,
        out_shape=sem_shapes + [pltpu.HBM(b.shape, b.dtype) for b in bufs],
        input_output_aliases={t: 2 * ng + t for t in range(n)},
        compiler_params=_split_params(),
    )(*[pltpu.with_memory_space_constraint(b, pltpu.HBM) for b in bufs])
    sems = [(res[2 * gi], res[2 * gi + 1]) for gi in range(ng)]
    return sems, list(res[2 * ng:])


def _gather_wait(name, bufs, send_sems, recv_sems, after, group):
    n = len(bufs)

    def body(*refs):
        ss, rs = refs[n], refs[n + 1]
        outs = refs[n + 3:]
        x, y, c, chips = _place()
        my = 2 * x + y
        for idx, (t, layer) in enumerate(group):
            for k, chip in enumerate(chips):
                cp = pltpu.make_async_remote_copy(
                    src_ref=outs[t].at[layer, my], dst_ref=outs[t].at[layer, 2 * chip[0] + chip[1]],
                    send_sem=ss.at[3 * idx + k], recv_sem=rs.at[3 * idx + k], device_id=(*chip, c), device_id_type=MESH)
                cp.wait_send()
                cp.wait_recv()

    return list(pl.pallas_call(
        body, name=name, in_specs=[HBM_SPEC] * n + [SEM_SPEC, SEM_SPEC, ANY],
        out_specs=[HBM_SPEC] * n,
        out_shape=[pltpu.HBM(b.shape, b.dtype) for b in bufs],
        input_output_aliases={t: t for t in range(n)},
        compiler_params=_split_params(),
    )(*bufs, send_sems, recv_sems, after))


N_PARTS = 7


def _scatter_items(send, rx, items, c, chips, x, y):
    my = 2 * x + y
    out = []
    for i, (k, layer) in enumerate(items):
        h = send[k].shape[2] // 2
        for kk, chip in enumerate(chips):
            shard = 2 * chip[0] + chip[1]
            for hf in (0, 1):
                out.append((send[k].at[layer, shard, pl.ds(hf * h, h)], rx[k].at[2 * kk + c, layer],
                            N_PARTS * i + 2 * kk + hf, N_PARTS * i + 2 * kk + c, (*chip, hf)))
        out.append((send[k].at[layer, my, pl.ds((1 - c) * h, h)], rx[k].at[N_PARTS - 1, layer],
                    N_PARTS * i + N_PARTS - 1, N_PARTS * i + N_PARTS - 1, (x, y, 1 - c)))
    return out


def _scatter_start(name, send, rx, items):
    n = len(send)
    m = N_PARTS * len(items)

    def body(*refs):
        ssem, rsem = refs[2 * n], refs[2 * n + 1]
        s_out, r_out = refs[2 * n + 2:3 * n + 2], refs[3 * n + 2:4 * n + 2]
        x, y, c, chips = _place()
        for src, dst, si, ri, to in _scatter_items(s_out, r_out, items, c, chips, x, y):
            pltpu.make_async_remote_copy(src_ref=src, dst_ref=dst, send_sem=ssem.at[si], recv_sem=rsem.at[ri],
                                         device_id=to, device_id_type=MESH).start()
        refs[4 * n + 2][...] = jnp.zeros((V7X_SUBLANES, V7X_LANES), F32)

    res = pl.pallas_call(
        body, name=name, in_specs=[HBM_SPEC] * (2 * n),
        out_specs=[SEM_SPEC, SEM_SPEC] + [HBM_SPEC] * (2 * n) + [pl.BlockSpec(memory_space=pltpu.VMEM)],
        out_shape=[pltpu.SemaphoreType.DMA((m,)), pltpu.SemaphoreType.DMA((m,))]
        + [pltpu.HBM(b.shape, b.dtype) for b in list(send) + list(rx)]
        + [jax.ShapeDtypeStruct((V7X_SUBLANES, V7X_LANES), F32)],
        input_output_aliases={t: 2 + t for t in range(2 * n)},
        compiler_params=_split_params(),
    )(*[pltpu.with_memory_space_constraint(b, pltpu.HBM) for b in list(send) + list(rx)])
    return (res[0], res[1]), list(res[2:2 + n]), list(res[2 + n:2 + 2 * n]), res[2 + 2 * n][0, 0]


def _scatter_wait(name, send, rx, ssem, rsem, after, items):
    n = len(send)

    def body(*refs):
        ss, rs = refs[2 * n], refs[2 * n + 1]
        s_out, r_out = refs[2 * n + 3:3 * n + 3], refs[3 * n + 3:]
        x, y, c, chips = _place()
        for i, (src, dst, si, ri, to) in enumerate(_scatter_items(s_out, r_out, items, c, chips, x, y)):
            arrival = i % N_PARTS
            landed = r_out[items[i // N_PARTS][0]].at[arrival, items[i // N_PARTS][1]]
            cp = pltpu.make_async_remote_copy(src_ref=src, dst_ref=landed, send_sem=ss.at[si],
                                              recv_sem=rs.at[N_PARTS * (i // N_PARTS) + arrival],
                                              device_id=to, device_id_type=MESH)
            cp.wait_send()
            cp.wait_recv()

    res = pl.pallas_call(
        body, name=name, in_specs=[HBM_SPEC] * (2 * n) + [SEM_SPEC, SEM_SPEC, ANY],
        out_specs=[HBM_SPEC] * (2 * n),
        out_shape=[pltpu.HBM(b.shape, b.dtype) for b in list(send) + list(rx)],
        input_output_aliases={t: t for t in range(2 * n)},
        compiler_params=_split_params(),
    )(*send, *rx, ssem, rsem, after)
    return list(res[:n]), list(res[n:])


def _chip_sum(name, g, rx):
    nl, _, r, c = g.shape
    h = r // 2
    tr = _tile(h, max(V7X_SUBLANES * 2, (1 << 19) // (2 * c) // 16 * 16), 16)
    nb = h // tr

    def body(g_ref, r_ref, o_ref):
        acc = g_ref[...].astype(F32)
        for k in range(N_PARTS):
            acc = acc + r_ref[k].astype(F32)
        o_ref[...] = acc

    return pl.pallas_call(
        body, name=name, grid=(nl, nb),
        in_specs=[pl.BlockSpec((None, None, tr, c), lambda l, i: (l, _my_shard(), _my_core() * nb + i, 0)),
                  pl.BlockSpec((N_PARTS, None, tr, c), lambda l, i: (0, l, i, 0))],
        out_specs=pl.BlockSpec((None, tr, c), lambda l, i: (l, i, 0)),
        out_shape=jax.ShapeDtypeStruct((nl, h, c), F32),
        compiler_params=_cp(2),
    )(g, rx)


def _send_half(name, fs):
    n = len(fs)

    def body(*refs):
        ins, outs = refs[:n], refs[n:2 * n]
        send_sems, recv_sems = refs[2 * n:]
        x, y, c, _ = _place()
        cps = []
        for t in range(n):
            cps.append(pltpu.make_async_remote_copy(
                src_ref=ins[t], dst_ref=outs[t], send_sem=send_sems.at[t], recv_sem=recv_sems.at[t],
                device_id=(x, y, 1 - c), device_id_type=MESH))
            cps[-1].start()
        for cp in cps:
            cp.wait()

    return pl.pallas_call(
        body, name=name, in_specs=[ANY] * n, out_specs=[ANY] * n,
        out_shape=[jax.ShapeDtypeStruct(f.shape, f.dtype) for f in fs],
        scratch_shapes=[pltpu.SemaphoreType.DMA((n,)), pltpu.SemaphoreType.DMA((n,))],
        compiler_params=_comm_params(),
    )(*fs)


def _peers(x, y, c):
    rel = [(dx, dy, dc) for dx in (0, 1) for dy in (0, 1) for dc in (0, 1) if (dx, dy, dc) != (0, 0, 0)]
    return [(1 - x if dx else x, 1 - y if dy else y, 1 - c if dc else c) for dx, dy, dc in rel]


def _share_start(name, v, land):
    def body(v_ref, land_ref, ssem, rsem, v_out, land_out, token):
        x, y, c, _ = _place()
        me = 4 * x + 2 * y + c
        for k, peer in enumerate(_peers(x, y, c)):
            pltpu.make_async_remote_copy(src_ref=v_out, dst_ref=land_out.at[me], send_sem=ssem.at[k],
                                         recv_sem=rsem.at[k], device_id=peer, device_id_type=MESH).start()
        token[...] = jnp.zeros((V7X_SUBLANES, V7X_LANES), F32)

    res = pl.pallas_call(
        body, name=name, in_specs=[HBM_SPEC, HBM_SPEC],
        out_specs=[SEM_SPEC, SEM_SPEC, HBM_SPEC, HBM_SPEC, pl.BlockSpec(memory_space=pltpu.VMEM)],
        out_shape=[pltpu.SemaphoreType.DMA((N_DEV - 1,)), pltpu.SemaphoreType.DMA((N_DEV - 1,)),
                   pltpu.HBM(v.shape, v.dtype), pltpu.HBM(land.shape, land.dtype),
                   jax.ShapeDtypeStruct((V7X_SUBLANES, V7X_LANES), F32)],
        input_output_aliases={0: 2, 1: 3},
        compiler_params=_split_params(),
    )(pltpu.with_memory_space_constraint(v, pltpu.HBM), pltpu.with_memory_space_constraint(land, pltpu.HBM))
    return (res[0], res[1]), res[2], res[3], res[4][0, 0]


def _share_wait(name, v, land, ssem, rsem, after):
    def body(v_ref, land_ref, ss, rs, after_ref, v_out, land_out):
        x, y, c, _ = _place()
        for k, (px, py, pc) in enumerate(_peers(x, y, c)):
            cp = pltpu.make_async_remote_copy(src_ref=v_out, dst_ref=land_out.at[4 * px + 2 * py + pc],
                                              send_sem=ss.at[k], recv_sem=rs.at[k], device_id=(px, py, pc),
                                              device_id_type=MESH)
            cp.wait_send()
            cp.wait_recv()

    res = pl.pallas_call(
        body, name=name, in_specs=[HBM_SPEC, HBM_SPEC, SEM_SPEC, SEM_SPEC, ANY],
        out_specs=[HBM_SPEC, HBM_SPEC],
        out_shape=[pltpu.HBM(v.shape, v.dtype), pltpu.HBM(land.shape, land.dtype)],
        input_output_aliases={0: 0, 1: 1},
        compiler_params=_split_params(),
    )(v, land, ssem, rsem, after)
    return res[0], res[1]


def _sum_devices(name, v, land):
    r, c = v.shape
    tr = _tile(r, 512)

    def body(v_ref, land_ref, o_ref):
        x, y, cc, _ = _place()
        me = 4 * x + 2 * y + cc
        own = v_ref[...]
        acc = jnp.where(me == 0, own, land_ref[0])
        for k in range(1, N_DEV):
            acc = acc + jnp.where(me == k, own, land_ref[k])
        o_ref[...] = acc

    return pl.pallas_call(
        body, grid=(r // tr,), name=name,
        in_specs=[pl.BlockSpec((tr, c), lambda i: (i, 0)), pl.BlockSpec((N_DEV, tr, c), lambda i: (0, i, 0))],
        out_specs=pl.BlockSpec((tr, c), lambda i: (i, 0)),
        out_shape=jax.ShapeDtypeStruct((r, c), F32), compiler_params=_cp(1),
    )(v, land)


def _rows_view(wall):
    nl, s, r, c = wall.shape
    return wall.reshape(nl, s * r, c)


def _ffn_fwd(tag, alpha, x, w_in, w_out3, layer, g, b):
    h, a = _ffn_in(f"{tag}_in", x, w_in, layer)
    y, xhat, rstd = _mm_ln(f"{tag}_out", alpha, 0.5, a, w_out3, layer, x, g, b)
    return y, (x, h, a, xhat, rstd)


def _dx_ln(name, a, b, *, nk, a_blk, a_map, b_blk, b_map, alpha, dz, nxt):
    t, d = dz.shape
    tm = a_blk[-2]
    xhat, rstd, g = nxt
    through = xhat is not None

    def body(*refs):
        a_ref, b_ref, dz_ref = refs[:3]
        if through:
            xh_ref, rs_ref, g_ref, o_ref, dg_ref, db_ref, acc = refs[3:]
        else:
            g_ref, o_ref, acc = refs[3:]
        i, kk = pl.program_id(0), pl.program_id(1)
        p = lax.dot_general(a_ref[...].astype(BF16), b_ref[...].astype(BF16), _NT, preferred_element_type=F32)

        @pl.when(kk == 0)
        def _():
            acc[...] = p

        @pl.when(kk > 0)
        def _():
            acc[...] += p

        @pl.when(kk == nk - 1)
        def _():
            dy = alpha * dz_ref[...] + acc[...]
            if not through:
                o_ref[...] = dy + g_ref[...]
                return
            xh = xh_ref[...]
            dxh = dy * g_ref[...]
            m1 = jnp.mean(dxh, axis=-1, keepdims=True)
            m2 = jnp.mean(dxh * xh, axis=-1, keepdims=True)
            o_ref[...] = rs_ref[...] * (dxh - m1 - xh * m2)
            pg = jnp.sum(dy * xh, axis=0, keepdims=True)
            pb = jnp.sum(dy, axis=0, keepdims=True)

            @pl.when(i == 0)
            def _():
                dg_ref[...] = pg
                db_ref[...] = pb

            @pl.when(i > 0)
            def _():
                dg_ref[...] += pg
                db_ref[...] += pb

    row = lambda c: pl.BlockSpec((tm, c), lambda i, kk: (i, 0))
    vec = pl.BlockSpec((1, d), lambda i, kk: (0, 0))
    sd = jax.ShapeDtypeStruct
    in_specs = [pl.BlockSpec(a_blk, a_map), pl.BlockSpec(b_blk, b_map), row(d)]
    args = [a, b, dz]
    if through:
        in_specs += [row(d), row(1), vec]
        args += [xhat, rstd, g]
        out_specs, out_shape = [row(d), vec, vec], [sd((t, d), F32), sd((1, d), F32), sd((1, d), F32)]
    else:
        in_specs += [vec]
        args += [g]
        out_specs, out_shape = row(d), sd((t, d), F32)
    return pl.pallas_call(
        body, grid=(t // tm, nk), name=name, in_specs=in_specs, out_specs=out_specs, out_shape=out_shape,
        scratch_shapes=[pltpu.VMEM((tm, d), F32)], compiler_params=_cp(2),
    )(*args)


def _ffn_bwd(tag, alpha, dz, saved, w_in, w_out3, layer, g_win, g_wout3, grads_done, nxt, zrow):
    x, h, a, _, _ = saved
    t = x.shape[0]
    _, s, k, n = w_in.shape
    tm = _tile(t, 512)
    g_wout3 = _mm_tn(f"{tag}_dwout", a, dz, tm=n, scale=0.5, layer=layer, into=g_wout3)
    dh = _ffn_da(f"{tag}_da", dz, w_out3, layer, h, zrow)
    g_win = _mm(f"{tag}_dwin", x, dh, mode="tn", grid=(s, t // tm), kaxis=1,
                a_blk=(tm, k), a_map=lambda j, kk: (kk, 0),
                b_blk=(None, tm, n), b_map=lambda j, kk: (j // 2, kk, j % 2),
                o_shape=w_in.shape, o_blk=(None, None, k, n), o_map=lambda j, kk: (layer, j, 0, 0),
                o_dtype=g_win.dtype, into=g_win)
    zero = grads_done(g_win, g_wout3)
    return _dx_ln(f"{tag}_dx", dh, w_in, nk=s, a_blk=(None, tm, n), a_map=lambda i, kk: (kk // 2, i, kk % 2),
                  b_blk=(None, None, k, n), b_map=lambda i, kk: (layer, kk, 0, 0),
                  alpha=alpha, dz=dz, nxt=(nxt[0], nxt[1], nxt[2] + zero))


def _fox_fwd(tag, alpha, x, w_pad, bf, w_o3, layer, g, b):
    t, d = x.shape
    nh = bf.shape[1]
    tb = _tile(t, ATTN_BLOCK)
    nb = t // tb
    qkv = _mm_nn(f"{tag}_qkv", x, w_pad[:, :3 * d], o_dtype=BF16, tn=d).reshape(nb, tb, 3 * d)
    fl = _mm_nn(f"{tag}_gate", x, w_pad[:, 3 * d:])[:, :nh]
    cum = _fox_cumsum(f"{tag}_cum", fl, bf)
    ccol = cum.reshape(nb, tb, nh // 2, 2).transpose(2, 0, 1, 3)
    crow = cum.reshape(nb, tb, nh // 2, 2).transpose(2, 0, 3, 1)
    o, lse = _attn_fwd(f"{tag}_attn", qkv, ccol, crow, nh)
    o2 = o.reshape(t, d)
    y, xhat, rstd = _mm_ln(f"{tag}_oproj", alpha, 1.0, o2, w_o3, layer, x, g, b)
    return y, xhat, rstd, (x, qkv, ccol, crow, o, lse, fl, w_pad)


def _fox_bwd(tag, alpha, dm, saved, bf, w_o3, layer, g_wo3, grads_done, nxt):
    x, qkv, ccol, crow, o, lse, fl, w_pad = saved
    t, d = x.shape
    nh = bf.shape[1]
    nb, tb, _ = qkv.shape
    tm = _tile(t, 512)
    g_wo3 = _mm_tn(f"{tag}_dwo", o.reshape(t, d), dm, layer=layer, into=g_wo3)
    do = _mm_nt(f"{tag}_do", dm, w_o3, layer=layer).reshape(nb, tb, d)
    dq, dk, dv, drow, dcol = _attn_bwd(f"{tag}_attn_bwd", qkv, ccol, crow, o, lse, do, nh)
    dcum = drow.transpose(1, 2, 0, 3).reshape(t, nh) - dcol.transpose(1, 3, 0, 2).reshape(t, nh)
    dfl, dbf = _fox_cumsum_bwd(f"{tag}_cum_bwd", dcum, fl, bf)
    pad = w_pad.shape[1] - 3 * d - nh
    dproj = jnp.concatenate([dq.reshape(t, d), dk.reshape(t, d), dv.reshape(t, d),
                             dfl.astype(BF16), jnp.zeros((t, pad), BF16)], axis=1)
    d_wpad = _mm_tn(f"{tag}_dwin", x, dproj, tn=_tile(w_pad.shape[1], 640, V7X_LANES))
    zero = grads_done(g_wo3, d_wpad)
    cols = w_pad.shape[1]
    out = _dx_ln(f"{tag}_dx", dproj, w_pad, nk=1, a_blk=(tm, cols), a_map=lambda i, kk: (i, 0),
                 b_blk=(d, cols), b_map=lambda i, kk: (0, 0), alpha=alpha, dz=dm, nxt=(nxt[0], nxt[1], nxt[2] + zero))
    return out, dbf


def _to_segments(a):
    t, d = a.shape
    return a.reshape(SCAN_SEGMENTS, t // SCAN_SEGMENTS, d).transpose(1, 0, 2).reshape(t, d)


def _from_segments(a):
    t, d = a.shape
    return a.reshape(t // SCAN_SEGMENTS, SCAN_SEGMENTS, d).transpose(1, 0, 2).reshape(t, d)


def _s5_discretise(a_re, a_im, log_dt, b_re, b_im):
    dt = jnp.exp(log_dt)[:, None]
    mag = jnp.exp(a_re * dt)
    ang = a_im * dt
    lb_re = mag * jnp.cos(ang)
    lb_im = mag * jnp.sin(ang)
    den = a_re * a_re + a_im * a_im
    nr = lb_re - 1.0
    ni = lb_im
    z_re = (nr * a_re + ni * a_im) / den
    z_im = (ni * a_re - nr * a_im) / den
    bb_re = z_re[..., None] * b_re - z_im[..., None] * b_im
    bb_im = z_re[..., None] * b_im + z_im[..., None] * b_re
    return lb_re, lb_im, bb_re, bb_im


S5_BLOCK_GROUPS = 8


def _blockdiag_in(bb):
    g, p, h = bb.shape
    e = jnp.eye(S5_BLOCK_GROUPS, dtype=bb.dtype)
    b4 = bb.reshape(g // S5_BLOCK_GROUPS, S5_BLOCK_GROUPS, p, h)
    return jnp.einsum("jgph,gf->jghfp", b4, e).reshape(g // S5_BLOCK_GROUPS, S5_BLOCK_GROUPS * h, S5_BLOCK_GROUPS * p)


def _blockdiag_in_grad(d):
    nj, gh, gp = d.shape
    h, p = gh // S5_BLOCK_GROUPS, gp // S5_BLOCK_GROUPS
    e = jnp.eye(S5_BLOCK_GROUPS, dtype=d.dtype)
    d6 = d.reshape(nj, S5_BLOCK_GROUPS, h, S5_BLOCK_GROUPS, p)
    return jnp.einsum("jghfp,gf->jgph", d6, e).reshape(nj * S5_BLOCK_GROUPS, p, h)


def _blockdiag_out(cc):
    g, h, p = cc.shape
    e = jnp.eye(S5_BLOCK_GROUPS, dtype=cc.dtype)
    c4 = cc.reshape(g // S5_BLOCK_GROUPS, S5_BLOCK_GROUPS, h, p)
    return jnp.einsum("jghp,gf->jfpgh", c4, e).reshape(g // S5_BLOCK_GROUPS, S5_BLOCK_GROUPS * p, S5_BLOCK_GROUPS * h)


def _blockdiag_out_grad(d):
    nj, gp, gh = d.shape
    h, p = gh // S5_BLOCK_GROUPS, gp // S5_BLOCK_GROUPS
    e = jnp.eye(S5_BLOCK_GROUPS, dtype=d.dtype)
    d6 = d.reshape(nj, S5_BLOCK_GROUPS, p, S5_BLOCK_GROUPS, h)
    return jnp.einsum("jfpgh,gf->jghp", d6, e).reshape(nj * S5_BLOCK_GROUPS, h, p)


def _s5_fwd(tag, x, prm, w_out, layer):
    a_re, a_im, log_dt, b_re, b_im, c_re, c_im, d_skip = prm
    t, d = x.shape
    g, p = a_re.shape
    w = g * p
    nj = g // S5_BLOCK_GROUPS
    cw, sw = S5_BLOCK_GROUPS * S5_GROUP, S5_BLOCK_GROUPS * p
    seg = t // SCAN_SEGMENTS
    tm = _tile(t, 4096)
    lb_re, lb_im, bb_re, bb_im = _s5_discretise(a_re, a_im, log_dt, b_re, b_im)
    lam = jnp.stack([lb_re.reshape(1, w), lb_im.reshape(1, w)])
    bs = jnp.stack([_blockdiag_in(bb_re), _blockdiag_in(bb_im)]).astype(BF16)
    cs = jnp.stack([_blockdiag_out(c_re), -_blockdiag_out(c_im)]).astype(BF16)
    dvec = d_skip.reshape(1, d)
    u = _to_segments(x)
    bu = _mm(f"{tag}_bu", u, bs, mode="nn", grid=(2, nj, t // tm), kaxis=None,
             a_blk=(tm, cw), a_map=lambda r, j, i: (i, j),
             b_blk=(None, None, cw, sw), b_map=lambda r, j, i: (r, j, 0, 0),
             o_shape=(2, t, w), o_blk=(None, tm, sw), o_map=lambda r, j, i: (r, i, j))
    hs = _s5_scan(f"{tag}_scan", lam, bu.reshape(2, seg, SCAN_SEGMENTS, w)).reshape(2, t, w)
    ych = _mm(f"{tag}_ch", hs, cs, mode="nn", grid=(nj, t // tm, 2), kaxis=2,
              a_blk=(None, tm, sw), a_map=lambda j, i, r: (r, i, j),
              b_blk=(None, None, sw, cw), b_map=lambda j, i, r: (r, j, 0, 0),
              o_shape=(t, d), o_blk=(tm, cw), o_map=lambda j, i, r: (i, j))
    ypre, act = _s5_act_fwd(f"{tag}_act", ych, u, dvec)
    vg = _mm_shards_nn(f"{tag}_wout", act, w_out, layer, F32)
    m = _from_segments(_glu_fwd(f"{tag}_glu", vg))
    return m, (u, lam, bs, cs, dvec, hs, ypre, act, vg)


def _s5_bwd(tag, dm, saved, prm, w_out, layer, g_wout):
    a_re, a_im, log_dt, b_re, b_im, c_re, c_im, d_skip = prm
    u, lam, bs, cs, dvec, hs, ypre, act, vg = saved
    t, d = u.shape
    g, p = a_re.shape
    w = g * p
    nj = g // S5_BLOCK_GROUPS
    cw, sw = S5_BLOCK_GROUPS * S5_GROUP, S5_BLOCK_GROUPS * p
    seg = t // SCAN_SEGMENTS
    tm = _tile(t, 4096)
    dvg = _glu_bwd(f"{tag}_glu_bwd", _to_segments(dm), vg)
    g_wout = _mm_shards_tn(f"{tag}_dwout", act, dvg, layer, g_wout)
    dact = _mm_shards_nt(f"{tag}_dact", dvg, w_out, layer)
    dypre, duskip, dd = _s5_act_bwd(f"{tag}_act_bwd", dact, ypre, u, dvec)
    dh = _mm(f"{tag}_dh", dypre, cs, mode="nt", grid=(2, nj, t // tm), kaxis=None,
             a_blk=(tm, cw), a_map=lambda r, j, i: (i, j),
             b_blk=(None, None, sw, cw), b_map=lambda r, j, i: (r, j, 0, 0),
             o_shape=(2, t, w), o_blk=(None, tm, sw), o_map=lambda r, j, i: (r, i, j))
    dcs = _mm(f"{tag}_dc", hs, dypre, mode="tn", grid=(2, nj, t // tm), kaxis=2,
              a_blk=(None, tm, sw), a_map=lambda r, j, i: (r, i, j),
              b_blk=(tm, cw), b_map=lambda r, j, i: (i, j),
              o_shape=(2, nj, sw, cw), o_blk=(None, None, sw, cw), o_map=lambda r, j, i: (r, j, 0, 0))
    gs, dlam8 = _s5_scan(f"{tag}_scan_bwd", lam, dh.reshape(2, seg, SCAN_SEGMENTS, w),
                         hs.reshape(2, seg, SCAN_SEGMENTS, w))
    gs = gs.reshape(2, t, w)
    du = _mm(f"{tag}_du", gs, bs, mode="nt", grid=(nj, t // tm, 2), kaxis=2,
             a_blk=(None, tm, sw), a_map=lambda j, i, r: (r, i, j),
             b_blk=(None, None, cw, sw), b_map=lambda j, i, r: (r, j, 0, 0),
             o_shape=(t, d), o_blk=(tm, cw), o_map=lambda j, i, r: (i, j))
    dbs = _mm(f"{tag}_db", u, gs, mode="tn", grid=(2, nj, t // tm), kaxis=2,
              a_blk=(tm, cw), a_map=lambda r, j, i: (i, j),
              b_blk=(None, tm, sw), b_map=lambda r, j, i: (r, i, j),
              o_shape=(2, nj, cw, sw), o_blk=(None, None, cw, sw), o_map=lambda r, j, i: (r, j, 0, 0))
    dx = _from_segments(du + duskip)
    dlam = jnp.sum(dlam8, axis=1).reshape(2, g, p)
    small = dict(dlb_re=dlam[0], dlb_im=dlam[1],
                 dbb_re=_blockdiag_in_grad(dbs[0]), dbb_im=_blockdiag_in_grad(dbs[1]),
                 dc_re=_blockdiag_out_grad(dcs[0]), dc_im=-_blockdiag_out_grad(dcs[1]),
                 dd=dd.reshape(g, S5_GROUP))
    return dx, g_wout, small


def _pack(pieces):
    rows = []
    for p in pieces:
        flat = p.reshape(-1).astype(F32)
        n = flat.shape[0]
        rows.append(jnp.pad(flat, (0, -n % V7X_LANES)).reshape(-1, V7X_LANES))
    buf = jnp.concatenate(rows, axis=0)
    return jnp.pad(buf, ((0, -buf.shape[0] % V7X_SUBLANES), (0, 0)))


def _unpack(buf, shapes):
    out, row = [], 0
    for s in shapes:
        n = math.prod(s)
        nr = -(-n // V7X_LANES)
        out.append(buf[row:row + nr].reshape(-1)[:n].reshape(s))
        row += nr
    return out


def kernel(x, ffn1_w_in, ffn1_w_out, ln1_g, ln1_b, lnm_g, lnm_b, ffn2_w_in, ffn2_w_out, ln2_g, ln2_b, fox_w_in, fox_b_f, fox_w_o, s5_a_re, s5_a_im, s5_log_dt, s5_b_re, s5_b_im, s5_c_re, s5_c_im, s5_d, s5_w_out, loss_target, m_ffn1_w_in, m_ffn1_w_out, m_ln1_g, m_ln1_b, m_lnm_g, m_lnm_b, m_ffn2_w_in, m_ffn2_w_out, m_ln2_g, m_ln2_b, m_fox_w_in, m_fox_b_f, m_fox_w_o, m_s5_a_re, m_s5_a_im, m_s5_log_dt, m_s5_b_re, m_s5_b_im, m_s5_c_re, m_s5_c_im, m_s5_d, m_s5_w_out, v_ffn1_w_in, v_ffn1_w_out, v_ln1_g, v_ln1_b, v_lnm_g, v_lnm_b, v_ffn2_w_in, v_ffn2_w_out, v_ln2_g, v_ln2_b, v_fox_w_in, v_fox_b_f, v_fox_w_o, v_s5_a_re, v_s5_a_im, v_s5_log_dt, v_s5_b_re, v_s5_b_im, v_s5_c_re, v_s5_c_im, v_s5_d, v_s5_w_out):
    big_names = ["ffn1_w_in", "ffn1_w_out", "ffn2_w_in", "ffn2_w_out", "fox_w_in", "fox_w_o", "s5_w_out"]
    small_names = ["ln1_g", "ln1_b", "lnm_g", "lnm_b", "ln2_g", "ln2_b", "fox_b_f", "s5_a_re", "s5_a_im", "s5_log_dt",
                   "s5_b_re", "s5_b_im", "s5_c_re", "s5_c_im", "s5_d"]
    out_order = ["ffn1_w_in", "ffn1_w_out", "ln1_g", "ln1_b", "lnm_g", "lnm_b", "ffn2_w_in", "ffn2_w_out", "ln2_g",
                 "ln2_b", "fox_w_in", "fox_b_f", "fox_w_o", "s5_a_re", "s5_a_im", "s5_log_dt", "s5_b_re", "s5_b_im",
                 "s5_c_re", "s5_c_im", "s5_d", "s5_w_out"]
    env = dict(locals())
    w = {n: env[n] for n in out_order}
    mom = {n: env["m_" + n] for n in out_order}
    vel = {n: env["v_" + n] for n in out_order}

    depth, d = ln1_g.shape
    t = x.shape[1]
    alpha = (2.0 * depth) ** 0.25
    x0 = x.reshape(t, d)
    tgt = loss_target.reshape(t, d)

    tix = {n: k for k, n in enumerate(big_names)}
    groups = []
    for i in range(depth):
        j = i // 2
        groups.append([(tix["ffn1_w_in"], i), (tix["ffn1_w_out"], i)])
        mixer = [(tix["fox_w_in"], j), (tix["fox_w_o"], j)] if i % 2 == 0 else [(tix["s5_w_out"], j)]
        groups.append(mixer)
        groups.append([(tix["ffn2_w_in"], i), (tix["ffn2_w_out"], i)])
    first = groups[0]
    cast = {big_names[k]: _cast_place(f"cast_{big_names[k]}", w[big_names[k]]) for k, _ in first}
    sems, sent = _gather_start("gather_start_first", list(cast.values()), [[(p, layer) for p, (_, layer) in enumerate(first)]])
    cast = dict(zip(cast, sent))
    for n in big_names:
        if n not in cast:
            cast[n] = _cast_place(f"cast_{n}", w[n], after=sent[0])
    more, bufs = _gather_start("gather_start", [cast[n] for n in big_names], groups[1:])
    sems = sems + more
    full, rows3 = {}, {}

    def arrive(gi, after):
        nonlocal bufs
        bufs = _gather_wait(f"gather_wait_{gi}", bufs, sems[gi][0], sems[gi][1], after, groups[gi])
        full.update(zip(big_names, bufs))
        rows3.update({n: _rows_view(full[n]) for n in ("ffn1_w_out", "ffn2_w_out", "fox_w_o")})

    nh = fox_b_f.shape[1]
    fox_cols = 3 * d + nh
    fox_pad = -(-fox_cols // (5 * V7X_LANES)) * (5 * V7X_LANES)

    def fox_wpad(j):
        wf = full["fox_w_in"][j].transpose(1, 0, 2).reshape(d, fox_cols)
        return jnp.pad(wf, ((0, 0), (0, fox_pad - fox_cols)))

    def s5_params(j):
        return (s5_a_re[j], s5_a_im[j], s5_log_dt[j], s5_b_re[j], s5_b_im[j], s5_c_re[j], s5_c_im[j], s5_d[j])

    saved = []
    h = x0
    for i in range(depth):
        j = i // 2
        arrive(3 * i, h)
        h, s1 = _ffn_fwd(f"l{i}_ffn1", alpha, h, full["ffn1_w_in"], rows3["ffn1_w_out"], i,
                         ln1_g[i:i + 1], ln1_b[i:i + 1])
        arrive(3 * i + 1, h)
        if i % 2 == 0:
            h, xhat_m, rstd_m, sm = _fox_fwd(f"l{i}_fox", alpha, h, fox_wpad(j), fox_b_f[j:j + 1], rows3["fox_w_o"], j,
                                             lnm_g[i:i + 1], lnm_b[i:i + 1])
        else:
            m, sm = _s5_fwd(f"l{i}_s5", h, s5_params(j), full["s5_w_out"], j)
            h, xhat_m, rstd_m = _ln_fwd(f"l{i}_lnm", alpha, h, m, 1.0, lnm_g[i:i + 1], lnm_b[i:i + 1])
        arrive(3 * i + 2, h)
        h, s2 = _ffn_fwd(f"l{i}_ffn2", alpha, h, full["ffn2_w_in"], rows3["ffn2_w_out"], i,
                         ln2_g[i:i + 1], ln2_b[i:i + 1])
        saved.append((s1, sm, (xhat_m, rstd_m), s2))
    loss_part = _loss_sum("loss", h, tgt) * (0.5 / d)

    fox_in_names = [f"fox_w_in_l{j}" for j in range(fox_w_in.shape[0])]
    gshape = {n: full[n].shape for n in big_names if n != "fox_w_in"}
    gshape.update({n: (1,) + full["fox_w_in"].shape[1:] for n in fox_in_names})
    gbuf = {n: lax.empty(s, BF16) for n, s in gshape.items()}
    rxbuf = {n: lax.empty((N_PARTS, s[0], s[2] // 2, s[3]), BF16) for n, s in gshape.items()}
    pending = []

    zero = jnp.zeros((), F32)

    def scatter(tag, pairs):
        nonlocal zero
        names = list(dict.fromkeys(n for n, _ in pairs))
        items = [(names.index(n), layer) for n, layer in pairs]
        sem, send, rx, zero = _scatter_start(f"scatter_start_{tag}", [gbuf[n] for n in names],
                                             [rxbuf[n] for n in names], items)
        gbuf.update(zip(names, send))
        rxbuf.update(zip(names, rx))
        pending.append((tag, names, items, sem))

    gsmall = {n: [None] * w[n].shape[0] for n in small_names}
    s5_cot = [None] * s5_a_re.shape[0]
    cot_names = ["dlb_re", "dlb_im", "dbb_re", "dbb_im", "dc_re", "dc_im", "dd"]
    ln_names = ["ln1_g", "ln1_b", "lnm_g", "lnm_b", "ln2_g", "ln2_b"]

    def zrow():
        return jnp.zeros((1, d), F32) + zero

    def ffn_done(which, i):
        def done(g_win, g_wout3):
            gbuf[f"{which}_w_in"], gbuf[f"{which}_w_out"] = g_win, g_wout3.reshape(gshape[f"{which}_w_out"])
            scatter(f"l{i}_{which}", [(f"{which}_w_in", i), (f"{which}_w_out", i)])
            return zero
        return done

    _, _, _, (_, _, _, xhat_top, rstd_top) = saved[depth - 1]
    dz, dg, db = _ln_bwd("top_ln_bwd", [(h, 1.0 / d), (tgt, -1.0 / d)], xhat_top, rstd_top, ln2_g[depth - 1:depth])
    gsmall["ln2_g"][depth - 1], gsmall["ln2_b"][depth - 1] = dg, db
    grad_x = None
    for i in reversed(range(depth)):
        j = i // 2
        s1, sm, (xhat_m, rstd_m), s2 = saved[i]
        dz, dg, db = _ffn_bwd(f"l{i}_ffn2", alpha, dz, s2, full["ffn2_w_in"], rows3["ffn2_w_out"], i,
                              gbuf["ffn2_w_in"], _rows_view(gbuf["ffn2_w_out"]), ffn_done("ffn2", i),
                              (xhat_m, rstd_m, lnm_g[i:i + 1]), zrow())
        gsmall["lnm_g"][i], gsmall["lnm_b"][i] = dg, db
        ln1 = (s1[3], s1[4], ln1_g[i:i + 1])
        if i % 2 == 0:
            def fox_done(g_wo3, d_wpad, j=j, i=i):
                gbuf["fox_w_o"] = g_wo3.reshape(gshape["fox_w_o"])
                gbuf[fox_in_names[j]] = d_wpad[:, :fox_cols].reshape(d, N_CHIPS, -1).transpose(1, 0, 2)[None].astype(BF16)
                scatter(f"l{i}_fox", [("fox_w_o", j), (fox_in_names[j], 0)])
                return zero

            (dz, dg, db), gsmall["fox_b_f"][j] = _fox_bwd(
                f"l{i}_fox", alpha, dz, sm, fox_b_f[j:j + 1], rows3["fox_w_o"], j,
                _rows_view(gbuf["fox_w_o"]), fox_done, ln1)
        else:
            dx, gbuf["s5_w_out"], s5_cot[j] = _s5_bwd(f"l{i}_s5", dz, sm, s5_params(j), full["s5_w_out"], j,
                                                      gbuf["s5_w_out"])
            scatter(f"l{i}_s5", [("s5_w_out", j)])
            dz, dg, db = _ln_bwd(f"l{i}_ln1_bwd", [(dz, alpha), (dx, 1.0)], ln1[0], ln1[1], ln1[2] + zero)
        gsmall["ln1_g"][i], gsmall["ln1_b"][i] = dg, db
        if i > 0:
            below = saved[i - 1][3]
            dz, dg, db = _ffn_bwd(f"l{i}_ffn1", alpha, dz, s1, full["ffn1_w_in"], rows3["ffn1_w_out"], i,
                                  gbuf["ffn1_w_in"], _rows_view(gbuf["ffn1_w_out"]), ffn_done("ffn1", i),
                                  (below[3], below[4], ln2_g[i - 1:i]), zrow())
            gsmall["ln2_g"][i - 1], gsmall["ln2_b"][i - 1] = dg, db
        else:
            pieces = [loss_part + zero] + [jnp.concatenate(gsmall[n], axis=0) for n in ln_names + ["fox_b_f"]]
            pieces += [jnp.stack([s5_cot[k][n] for k in range(len(s5_cot))]) for n in cot_names]
            mine = _pack(pieces)
            share_sem, mine, land, zero = _share_start("small_share_start", mine, lax.empty((N_DEV,) + mine.shape, F32))
            grad_x = _ffn_bwd(f"l{i}_ffn1", alpha, dz, s1, full["ffn1_w_in"], rows3["ffn1_w_out"], i,
                              gbuf["ffn1_w_in"], _rows_view(gbuf["ffn1_w_out"]), ffn_done("ffn1", i),
                              (None, None, jnp.zeros((1, d), F32)), zrow()).reshape(x.shape)

    shapes = [p.shape for p in pieces]
    mine, land = _share_wait("small_share_wait", mine, land, share_sem[0], share_sem[1], grad_x)
    summed = _unpack(_sum_devices("small_sum", mine, land), shapes)
    loss = summed[0].reshape(())
    gs_final = dict(zip(ln_names + ["fox_b_f"], summed[1:8]))
    cot = dict(zip(cot_names, summed[8:]))
    prm_names = ["s5_a_re", "s5_a_im", "s5_log_dt", "s5_b_re", "s5_b_im"]
    _, disc_vjp = jax.vjp(jax.vmap(_s5_discretise), *[w[n] for n in prm_names])
    for n, gval in zip(prm_names, disc_vjp((cot["dlb_re"], cot["dlb_im"], cot["dbb_re"], cot["dbb_im"]))):
        gs_final[n] = gval
    gs_final["s5_c_re"], gs_final["s5_c_im"], gs_final["s5_d"] = cot["dc_re"], cot["dc_im"], cot["dd"]

    grads, deltas, new_m, new_v = {}, {}, {}, {}
    small_shapes = [w[n].shape for n in small_names]
    for n in small_names:
        grads[n] = gs_final[n].reshape(w[n].shape)
    packed = [_pack([src[n] for n in small_names]) for src in (w, grads, mom, vel)]
    small_out = _adamw("adamw_small", *packed)
    for dst, buf in zip((deltas, new_m, new_v), small_out):
        for n, val in zip(small_names, _unpack(buf, small_shapes)):
            dst[n] = val

    for tag, names, items, sem in pending:
        send, rx = _scatter_wait(f"scatter_wait_{tag}", [gbuf[n] for n in names], [rxbuf[n] for n in names],
                                 sem[0], sem[1], small_out[0], items)
        gbuf.update(zip(names, send))
        rxbuf.update(zip(names, rx))
    half = {n: _chip_sum(f"grad_chip_sum_{n}", gbuf[n], rxbuf[n]) for n in gshape}
    half["fox_w_in"] = jnp.concatenate([half[n] for n in fox_in_names], axis=0)
    halves = [half[n] for n in big_names]
    theirs = _send_half("grad_send_half", halves)
    for n, mine_h, their_h in zip(big_names, halves, theirs):
        grads[n], deltas[n], new_m[n], new_v[n] = _adamw_join(f"adamw_{n}", w[n], mine_h, their_h, mom[n], vel[n])
    return (loss, grad_x, *[grads[n] for n in out_order], *[deltas[n] for n in out_order],
            *[new_m[n] for n in out_order], *[new_v[n] for n in out_order])
```

```python
import functools
import math

import jax
import jax.numpy as jnp
from jax import lax
from jax.experimental import pallas as pl
from jax.experimental.pallas import tpu as pltpu

F32 = jnp.float32
BF16 = jnp.bfloat16
LN_EPS = 1e-5
NEG_INF = -1e30
ADAM_LR = 0.001
ADAM_B1 = 0.9
ADAM_B2 = 0.999
ADAM_EPS = 1e-08
ADAM_WD = 0.01
ADAM_STEP = 10
S5_GROUP = 16
SCAN_SEGMENTS = 32
ATTN_BLOCK = 1024
V7X_SUBLANES = 8
V7X_LANES = 128
VMEM_LIMIT = 56 * 1024 * 1024
N_CHIPS = 4
N_DEV = 8
MESH = pl.DeviceIdType.MESH
ANY = pl.BlockSpec(memory_space=pl.ANY)


def _cp(n_grid, kaxis=None):
    sem = tuple("arbitrary" if (kaxis is None or i == kaxis) else "parallel" for i in range(n_grid))
    return pltpu.CompilerParams(dimension_semantics=sem, vmem_limit_bytes=VMEM_LIMIT)


def _tile(n, pref, mult=V7X_SUBLANES):
    if n <= pref:
        return n
    for t in range(pref, 0, -1):
        if n % t == 0 and t % mult == 0:
            return t
    return n


_CONTRACT = {"nn": ((1,), (0,)), "nt": ((1,), (1,)), "tn": ((0,), (0,))}


def _mm(name, a, b, *, mode, grid, kaxis, a_blk, a_map, b_blk, b_map, o_shape, o_blk, o_map, o_dtype=F32, scale=None,
        into=None):
    nk = 1 if kaxis is None else grid[kaxis]
    assert kaxis is None or kaxis == len(grid) - 1
    dims = (_CONTRACT[mode], ((), ()))
    use_acc = nk > 1 and o_dtype != F32
    acc_shape = tuple(d for d in o_blk if d is not None)

    def body(a_ref, b_ref, *rest):
        o_ref, scratch = (rest[1], rest[2:]) if into is not None else (rest[0], rest[1:])
        p = lax.dot_general(a_ref[...].astype(BF16), b_ref[...].astype(BF16), dims, preferred_element_type=F32)
        if nk == 1:
            if scale is not None:
                p = p * scale
            o_ref[...] = p.astype(o_dtype)
            return
        acc = scratch[0] if use_acc else o_ref
        k = pl.program_id(kaxis)

        @pl.when(k == 0)
        def _():
            acc[...] = p

        @pl.when(k > 0)
        def _():
            acc[...] += p

        if use_acc or scale is not None:
            @pl.when(k == nk - 1)
            def _():
                r = acc[...]
                if scale is not None:
                    r = r * scale
                o_ref[...] = r.astype(o_dtype)

    in_specs = [pl.BlockSpec(a_blk, a_map), pl.BlockSpec(b_blk, b_map)]
    args = [a, b]
    if into is not None:
        assert into.shape == tuple(o_shape) and into.dtype == o_dtype
        in_specs.append(ANY)
        args.append(into)
    return pl.pallas_call(
        body, grid=grid, name=name, in_specs=in_specs,
        out_specs=pl.BlockSpec(o_blk, o_map),
        out_shape=jax.ShapeDtypeStruct(o_shape, o_dtype),
        input_output_aliases={2: 0} if into is not None else {},
        scratch_shapes=[pltpu.VMEM(acc_shape, F32)] if use_acc else [],
        compiler_params=_cp(len(grid), kaxis),
    )(*args)


def _mm_shards_nn(name, a, wall, layer, o_dtype):
    t, k = a.shape
    _, s, _, n = wall.shape
    tm = _tile(t, 512)
    return _mm(name, a, wall, mode="nn", grid=(s, t // tm), kaxis=None,
               a_blk=(tm, k), a_map=lambda j, i: (i, 0),
               b_blk=(None, None, k, n), b_map=lambda j, i: (layer, j, 0, 0),
               o_shape=(t, s * n), o_blk=(tm, n), o_map=lambda j, i: (i, j), o_dtype=o_dtype)


def _mm_shards_nt(name, g, wall, layer):
    t = g.shape[0]
    _, s, k, n = wall.shape
    tm = _tile(t, 512)
    return _mm(name, g, wall, mode="nt", grid=(t // tm, s), kaxis=1,
               a_blk=(tm, n), a_map=lambda i, kk: (i, kk),
               b_blk=(None, None, k, n), b_map=lambda i, kk: (layer, kk, 0, 0),
               o_shape=(t, k), o_blk=(tm, k), o_map=lambda i, kk: (i, 0))


def _mm_shards_tn(name, a, g, layer, into):
    t, k = a.shape
    _, s, _, n = into.shape
    tk = _tile(t, 512)
    return _mm(name, a, g, mode="tn", grid=(s, t // tk), kaxis=1,
               a_blk=(tk, k), a_map=lambda j, kk: (kk, 0),
               b_blk=(tk, n), b_map=lambda j, kk: (kk, j),
               o_shape=into.shape, o_blk=(None, None, k, n), o_map=lambda j, kk: (layer, j, 0, 0),
               o_dtype=into.dtype, into=into)


def _mm_nn(name, a, w, o_dtype=F32, tn=None):
    t, k = a.shape
    n = w.shape[1]
    tm = _tile(t, 512)
    tn = n if tn is None else tn
    return _mm(name, a, w, mode="nn", grid=(n // tn, t // tm), kaxis=None,
               a_blk=(tm, k), a_map=lambda j, i: (i, 0),
               b_blk=(k, tn), b_map=lambda j, i: (0, j),
               o_shape=(t, n), o_blk=(tm, tn), o_map=lambda j, i: (i, j), o_dtype=o_dtype)


def _mm_nt(name, g, w, layer=None, o_dtype=F32):
    t, k = g.shape
    n = w.shape[-2]
    tm = _tile(t, 512)
    b_blk, b_map = ((n, k), lambda i: (0, 0)) if layer is None else ((None, n, k), lambda i: (layer, 0, 0))
    return _mm(name, g, w, mode="nt", grid=(t // tm,), kaxis=None,
               a_blk=(tm, k), a_map=lambda i: (i, 0), b_blk=b_blk, b_map=b_map,
               o_shape=(t, n), o_blk=(tm, n), o_map=lambda i: (i, 0), o_dtype=o_dtype)


def _mm_tn(name, a, g, tm=None, tn=None, scale=None, layer=None, into=None):
    t, m = a.shape
    n = g.shape[1]
    tk = _tile(t, 512)
    tm = m if tm is None else tm
    tn = n if tn is None else tn
    if layer is None:
        o_shape, o_blk, o_map, o_dtype = (m, n), (tm, tn), lambda i, j, kk: (i, j), F32
    else:
        o_shape, o_blk, o_map, o_dtype = into.shape, (None, tm, tn), lambda i, j, kk: (layer, i, j), into.dtype
    return _mm(name, a, g, mode="tn", grid=(m // tm, n // tn, t // tk), kaxis=2,
               a_blk=(tk, tm), a_map=lambda i, j, kk: (kk, i),
               b_blk=(tk, tn), b_map=lambda i, j, kk: (kk, j),
               o_shape=o_shape, o_blk=o_blk, o_map=o_map, o_dtype=o_dtype, scale=scale, into=into)


def _sigmoid(x):
    return 1.0 / (1.0 + jnp.exp(-x))


def _rows_call(name, body, t, tm, ins, in_cols, outs, acc_outs=()):
    in_specs = []
    for x, c in zip(ins, in_cols):
        if x.shape[0] == 1:
            in_specs.append(pl.BlockSpec((1, c), lambda i: (0, 0)))
        else:
            in_specs.append(pl.BlockSpec((tm, c), lambda i: (i, 0)))
    out_specs = [pl.BlockSpec((tm, s.shape[1]), lambda i: (i, 0)) for s in outs]
    out_specs += [pl.BlockSpec((1, s.shape[1]), lambda i: (0, 0)) for s in acc_outs]
    return pl.pallas_call(
        body, grid=(t // tm,), name=name, in_specs=in_specs, out_specs=out_specs,
        out_shape=list(outs) + list(acc_outs), compiler_params=_cp(1),
    )(*ins)


def _ln_fwd(name, alpha, x, r, coef, g, b):
    t, d = x.shape
    tm = _tile(t, 256)

    def body(x_ref, r_ref, g_ref, b_ref, y_ref, xh_ref, rs_ref):
        z = alpha * x_ref[...] + coef * r_ref[...]
        mu = jnp.mean(z, axis=-1, keepdims=True)
        zc = z - mu
        var = jnp.mean(zc * zc, axis=-1, keepdims=True)
        rstd = lax.rsqrt(var + LN_EPS)
        xh = zc * rstd
        y_ref[...] = xh * g_ref[...] + b_ref[...]
        xh_ref[...] = xh
        rs_ref[...] = rstd

    sd = jax.ShapeDtypeStruct
    return _rows_call(name, body, t, tm, [x, r, g, b], [d, d, d, d],
                      [sd((t, d), F32), sd((t, d), F32), sd((t, 1), F32)])


def _ln_bwd(name, terms, xhat, rstd, g):
    t, d = xhat.shape
    tm = _tile(t, 256)
    n = len(terms)
    coefs = [c for _, c in terms]

    def body(*refs):
        t_refs = refs[:n]
        xh_ref, rs_ref, g_ref, dz_ref, dg_ref, db_ref = refs[n:]
        dy = coefs[0] * t_refs[0][...]
        for c, r in zip(coefs[1:], t_refs[1:]):
            dy = dy + c * r[...]
        xh = xh_ref[...]
        dxh = dy * g_ref[...]
        m1 = jnp.mean(dxh, axis=-1, keepdims=True)
        m2 = jnp.mean(dxh * xh, axis=-1, keepdims=True)
        dz_ref[...] = rs_ref[...] * (dxh - m1 - xh * m2)
        pg = jnp.sum(dy * xh, axis=0, keepdims=True)
        pb = jnp.sum(dy, axis=0, keepdims=True)
        i = pl.program_id(0)

        @pl.when(i == 0)
        def _():
            dg_ref[...] = pg
            db_ref[...] = pb

        @pl.when(i > 0)
        def _():
            dg_ref[...] += pg
            db_ref[...] += pb

    sd = jax.ShapeDtypeStruct
    arrs = [a for a, _ in terms] + [xhat, rstd, g]
    cols = [d] * n + [d, 1, d]
    return _rows_call(name, body, t, tm, arrs, cols, [sd((t, d), F32)], [sd((1, d), F32), sd((1, d), F32)])


def _loss_sum(name, y, tgt):
    t, d = y.shape
    tm = _tile(t, 256)

    def body(y_ref, t_ref, o_ref):
        e = y_ref[...] - t_ref[...]
        s = jnp.sum(jnp.sum(e * e, axis=1, keepdims=True), axis=0, keepdims=True)
        i = pl.program_id(0)

        @pl.when(i == 0)
        def _():
            o_ref[...] = s

        @pl.when(i > 0)
        def _():
            o_ref[...] += s

    return _rows_call(name, body, t, tm, [y, tgt], [d, d], [], [jax.ShapeDtypeStruct((1, 1), F32)])[0]


def _ffn_in(name, x, wall, layer):
    t, k = x.shape
    n = wall.shape[3]
    tm = _tile(t, 512)

    def body(x_ref, wg_ref, wu_ref, h_ref, a_ref):
        xb = x_ref[...].astype(BF16)
        g = lax.dot_general(xb, wg_ref[...], _NN, preferred_element_type=F32)
        u = lax.dot_general(xb, wu_ref[...], _NN, preferred_element_type=F32)
        h_ref[0] = g.astype(BF16)
        h_ref[1] = u.astype(BF16)
        a_ref[...] = (g * _sigmoid(g) * u).astype(BF16)

    return pl.pallas_call(
        body, grid=(2, t // tm), name=name,
        in_specs=[pl.BlockSpec((tm, k), lambda j, i: (i, 0)),
                  pl.BlockSpec((None, None, k, n), lambda j, i: (layer, j, 0, 0)),
                  pl.BlockSpec((None, None, k, n), lambda j, i: (layer, 2 + j, 0, 0))],
        out_specs=[pl.BlockSpec((2, tm, n), lambda j, i: (0, i, j)), pl.BlockSpec((tm, n), lambda j, i: (i, j))],
        out_shape=[jax.ShapeDtypeStruct((2, t, 2 * n), BF16), jax.ShapeDtypeStruct((t, 2 * n), BF16)],
        compiler_params=_cp(2),
    )(x, wall, wall)


def _ffn_da(name, dz, w3, layer, h, zrow):
    t, k = dz.shape
    f = w3.shape[1]
    n = f // 2
    tm = _tile(t, 512)

    def body(dz_ref, z_ref, w_ref, g_ref, u_ref, dh_ref):
        d = 0.5 * lax.dot_general((dz_ref[...] + z_ref[...]).astype(BF16), w_ref[...], _NT, preferred_element_type=F32)
        g = g_ref[...].astype(F32)
        u = u_ref[...].astype(F32)
        sg = _sigmoid(g)
        dh_ref[0] = (d * u * sg * (1.0 + g * (1.0 - sg))).astype(BF16)
        dh_ref[1] = (d * g * sg).astype(BF16)

    return pl.pallas_call(
        body, grid=(2, t // tm), name=name,
        in_specs=[pl.BlockSpec((tm, k), lambda j, i: (i, 0)),
                  pl.BlockSpec((1, k), lambda j, i: (0, 0)),
                  pl.BlockSpec((None, n, k), lambda j, i: (layer, j, 0)),
                  pl.BlockSpec((None, tm, n), lambda j, i: (0, i, j)),
                  pl.BlockSpec((None, tm, n), lambda j, i: (1, i, j))],
        out_specs=pl.BlockSpec((2, tm, n), lambda j, i: (0, i, j)),
        out_shape=jax.ShapeDtypeStruct((2, t, f), BF16),
        compiler_params=_cp(2),
    )(dz, zrow, w3, h, h)


def _mm_ln(name, alpha, coef, a, w3, layer, x, g, b):
    t, k = a.shape
    d = w3.shape[2]
    tm = _tile(t, 512)

    def body(a_ref, w_ref, x_ref, g_ref, b_ref, y_ref, xh_ref, rs_ref, yb_ref):
        f = lax.dot_general(a_ref[...].astype(BF16), w_ref[...], _NN, preferred_element_type=F32)
        z = alpha * x_ref[...] + coef * f
        mu = jnp.mean(z, axis=-1, keepdims=True)
        zc = z - mu
        var = jnp.mean(zc * zc, axis=-1, keepdims=True)
        rstd = lax.rsqrt(var + LN_EPS)
        xh = zc * rstd
        y = xh * g_ref[...] + b_ref[...]
        y_ref[...] = y
        xh_ref[...] = xh
        rs_ref[...] = rstd
        yb_ref[...] = y.astype(BF16)

    row = lambda c: pl.BlockSpec((tm, c), lambda i: (i, 0))
    vec = pl.BlockSpec((1, d), lambda i: (0, 0))
    sd = jax.ShapeDtypeStruct
    return pl.pallas_call(
        body, grid=(t // tm,), name=name,
        in_specs=[row(k), pl.BlockSpec((None, k, d), lambda i: (layer, 0, 0)), row(d), vec, vec],
        out_specs=[row(d), row(d), row(1), row(d)],
        out_shape=[sd((t, d), F32), sd((t, d), F32), sd((t, 1), F32), sd((t, d), BF16)],
        compiler_params=_cp(1),
    )(a, w3, x, g, b)


_GELU_C = math.sqrt(2.0 / math.pi)


def _s5_act_fwd(name, ych, u, dvec):
    t, d = u.shape
    tm = _tile(t, 256)

    def body(y_ref, u_ref, d_ref, p_ref, a_ref):
        y = y_ref[...] + d_ref[...] * u_ref[...]
        p_ref[...] = y
        a_ref[...] = (0.5 * y * (1.0 + jnp.tanh(_GELU_C * (y + 0.044715 * y * y * y)))).astype(BF16)

    sd = jax.ShapeDtypeStruct
    return _rows_call(name, body, t, tm, [ych, u, dvec], [d, d, d], [sd((t, d), F32), sd((t, d), BF16)])


def _s5_act_bwd(name, dact, ypre, u, dvec):
    t, d = u.shape
    tm = _tile(t, 256)

    def body(da_ref, y_ref, u_ref, d_ref, dy_ref, ds_ref, dd_ref):
        y = y_ref[...]
        th = jnp.tanh(_GELU_C * (y + 0.044715 * y * y * y))
        dg = 0.5 * (1.0 + th) + 0.5 * y * (1.0 - th * th) * _GELU_C * (1.0 + 3.0 * 0.044715 * y * y)
        dy = da_ref[...] * dg
        dy_ref[...] = dy
        ds_ref[...] = dy * d_ref[...]
        pd = jnp.sum(dy * u_ref[...], axis=0, keepdims=True)
        i = pl.program_id(0)

        @pl.when(i == 0)
        def _():
            dd_ref[...] = pd

        @pl.when(i > 0)
        def _():
            dd_ref[...] += pd

    sd = jax.ShapeDtypeStruct
    return _rows_call(name, body, t, tm, [dact, ypre, u, dvec], [d, d, d, d],
                      [sd((t, d), F32), sd((t, d), F32)], [sd((1, d), F32)])


def _glu_fwd(name, vg):
    t, d2 = vg.shape
    d = d2 // 2
    tm = _tile(t, 256)

    def body(vg_ref, m_ref):
        m_ref[...] = vg_ref[:, :d] * _sigmoid(vg_ref[:, d:])

    return _rows_call(name, body, t, tm, [vg], [d2], [jax.ShapeDtypeStruct((t, d), F32)])[0]


def _glu_bwd(name, dm, vg):
    t, d2 = vg.shape
    d = d2 // 2
    tm = _tile(t, 256)

    def body(dm_ref, vg_ref, o_ref):
        sg = _sigmoid(vg_ref[:, d:])
        g = dm_ref[...]
        o_ref[:, :d] = (g * sg).astype(BF16)
        o_ref[:, d:] = (g * vg_ref[:, :d] * sg * (1.0 - sg)).astype(BF16)

    return _rows_call(name, body, t, tm, [dm, vg], [d, d2], [jax.ShapeDtypeStruct((t, d2), BF16)])[0]


def _adamw(name, w, g, m, v):
    r, c = w.shape
    tr = _tile(r, max(V7X_SUBLANES, (1 << 20) // (4 * c) // V7X_SUBLANES * V7X_SUBLANES))

    def body(w_ref, g_ref, m_ref, v_ref, d_ref, nm_ref, nv_ref):
        gg = g_ref[...]
        nm = ADAM_B1 * m_ref[...] + (1.0 - ADAM_B1) * gg
        nv = ADAM_B2 * v_ref[...] + (1.0 - ADAM_B2) * (gg * gg)
        m_hat = nm / (1.0 - ADAM_B1 ** ADAM_STEP)
        v_hat = nv / (1.0 - ADAM_B2 ** ADAM_STEP)
        d_ref[...] = -ADAM_LR * (m_hat / (jnp.sqrt(v_hat) + ADAM_EPS) + ADAM_WD * w_ref[...])
        nm_ref[...] = nm
        nv_ref[...] = nv

    sd = jax.ShapeDtypeStruct((r, c), F32)
    return _rows_call(name, body, r, tr, [w, g, m, v], [c] * 4, [sd, sd, sd])


def _my_shard():
    return 2 * lax.axis_index("x") + lax.axis_index("y")


def _my_core():
    return lax.axis_index("c")


def _adamw_join(name, w, mine, theirs, m, v):
    nl, r, c = w.shape
    h = r // 2
    tr = _tile(h, max(V7X_SUBLANES, (1 << 20) // (4 * c) // V7X_SUBLANES * V7X_SUBLANES))
    nb = h // tr

    def body(w_ref, a_ref, b_ref, m_ref, v_ref, g_ref, d_ref, nm_ref, nv_ref):
        gg = jnp.where(pl.program_id(1) == _my_core(), a_ref[...], b_ref[...])
        nm = ADAM_B1 * m_ref[...] + (1.0 - ADAM_B1) * gg
        nv = ADAM_B2 * v_ref[...] + (1.0 - ADAM_B2) * (gg * gg)
        m_hat = nm / (1.0 - ADAM_B1 ** ADAM_STEP)
        v_hat = nv / (1.0 - ADAM_B2 ** ADAM_STEP)
        g_ref[...] = gg
        d_ref[...] = -ADAM_LR * (m_hat / (jnp.sqrt(v_hat) + ADAM_EPS) + ADAM_WD * w_ref[...])
        nm_ref[...] = nm
        nv_ref[...] = nv

    full = pl.BlockSpec((None, tr, c), lambda l, hf, i: (l, hf * nb + i, 0))
    sd = jax.ShapeDtypeStruct((nl, r, c), F32)
    return pl.pallas_call(
        body, name=name, grid=(nl, 2, nb),
        in_specs=[full,
                  pl.BlockSpec((None, tr, c), lambda l, hf, i: (l, jnp.where(hf == _my_core(), i, 0), 0)),
                  pl.BlockSpec((None, tr, c), lambda l, hf, i: (l, jnp.where(hf == _my_core(), 0, i), 0)),
                  full, full],
        out_specs=[full, full, full, full],
        out_shape=[sd, sd, sd, sd],
        compiler_params=_cp(3),
    )(w, mine, theirs, m, v)


def _split3(x):
    hi = x.astype(BF16)
    r1 = x - hi.astype(F32)
    mid = r1.astype(BF16)
    lo = (r1 - mid.astype(F32)).astype(BF16)
    return hi, mid, lo


def _tri_sum(tri, x):
    dims = (((1,), (0,)), ((), ()))
    hi, mid, lo = _split3(x)
    out = lax.dot_general(tri, lo, dims, preferred_element_type=F32)
    out = out + lax.dot_general(tri, mid, dims, preferred_element_type=F32)
    return out + lax.dot_general(tri, hi, dims, preferred_element_type=F32)


def _fox_cumsum(name, fl, bf):
    t, h = fl.shape
    tb = _tile(t, 512)

    def body(fl_ref, bf_ref, c_ref, carry):
        i = pl.program_id(0)

        @pl.when(i == 0)
        def _():
            carry[...] = jnp.zeros_like(carry)

        x = fl_ref[...] + bf_ref[...]
        lf = jnp.minimum(x, 0.0) - jnp.log(1.0 + jnp.exp(-jnp.abs(x)))
        row = lax.broadcasted_iota(jnp.int32, (tb, tb), 0)
        col = lax.broadcasted_iota(jnp.int32, (tb, tb), 1)
        tri = jnp.where(row >= col, 1.0, 0.0).astype(BF16)
        c_ref[...] = _tri_sum(tri, lf) + carry[...]
        carry[...] += jnp.sum(lf, axis=0, keepdims=True)

    return pl.pallas_call(
        body, grid=(t // tb,), name=name,
        in_specs=[pl.BlockSpec((tb, h), lambda i: (i, 0)), pl.BlockSpec((1, h), lambda i: (0, 0))],
        out_specs=pl.BlockSpec((tb, h), lambda i: (i, 0)),
        out_shape=jax.ShapeDtypeStruct((t, h), F32),
        scratch_shapes=[pltpu.VMEM((1, h), F32)], compiler_params=_cp(1),
    )(fl, bf)


def _fox_cumsum_bwd(name, dcum, fl, bf):
    t, h = fl.shape
    tb = _tile(t, 512)
    nb = t // tb

    def body(dc_ref, fl_ref, bf_ref, df_ref, db_ref, carry):
        i = pl.program_id(0)

        @pl.when(i == 0)
        def _():
            carry[...] = jnp.zeros_like(carry)

        dc = dc_ref[...]
        row = lax.broadcasted_iota(jnp.int32, (tb, tb), 0)
        col = lax.broadcasted_iota(jnp.int32, (tb, tb), 1)
        tri = jnp.where(row <= col, 1.0, 0.0).astype(BF16)
        dlf = _tri_sum(tri, dc) + carry[...]
        carry[...] += jnp.sum(dc, axis=0, keepdims=True)
        x = fl_ref[...] + bf_ref[...]
        df = dlf / (1.0 + jnp.exp(x))
        df_ref[...] = df
        pb = jnp.sum(df, axis=0, keepdims=True)

        @pl.when(i == 0)
        def _():
            db_ref[...] = pb

        @pl.when(i > 0)
        def _():
            db_ref[...] += pb

    rev = lambda i: (nb - 1 - i, 0)
    return pl.pallas_call(
        body, grid=(nb,), name=name,
        in_specs=[pl.BlockSpec((tb, h), rev), pl.BlockSpec((tb, h), rev), pl.BlockSpec((1, h), lambda i: (0, 0))],
        out_specs=[pl.BlockSpec((tb, h), rev), pl.BlockSpec((1, h), lambda i: (0, 0))],
        out_shape=[jax.ShapeDtypeStruct((t, h), F32), jax.ShapeDtypeStruct((1, h), F32)],
        scratch_shapes=[pltpu.VMEM((1, h), F32)], compiler_params=_cp(1),
    )(dcum, fl, bf)


_NT = (((1,), (1,)), ((), ()))
_TN = (((0,), (0,)), ((), ()))
_NN = (((1,), (0,)), ((), ()))


def _causal_mask(s, r0):
    row = lax.broadcasted_iota(jnp.int32, s.shape, 0) + r0
    col = lax.broadcasted_iota(jnp.int32, s.shape, 1)
    return jnp.where(col <= row, s, NEG_INF)


def _first_head_lanes(hd):
    return lax.broadcasted_iota(jnp.int32, (1, 2 * hd), 1) < hd


def _attn_fwd(name, qkv, ccol, crow, nh):
    nb, tb, d3 = qkv.shape
    d = d3 // 3
    hd = d // nh
    lanes = 2 * hd
    assert lanes == V7X_LANES
    scale = 1.0 / math.sqrt(hd)

    def body(q_ref, k_ref, v_ref, cc_ref, cr_ref, o_ref, lse_ref):
        i = pl.program_id(1)
        first = _first_head_lanes(hd)
        q = q_ref[...] * scale
        res = []
        for hh in (0, 1):
            qh = jnp.where(first if hh == 0 else jnp.logical_not(first), q, jnp.zeros_like(q))
            cc = cc_ref[:, hh:hh + 1]

            def tile(j, carry, r0=0, nr=tb, ncol=tb, masked=False, qh=qh, cc=cc, hh=hh):
                m, l, acc = carry
                s = lax.dot_general(qh[r0:r0 + nr], k_ref[j, 0:ncol, :], _NT, preferred_element_type=F32)
                s = s + cc[r0:r0 + nr] - cr_ref[j][hh:hh + 1, 0:ncol]
                if masked:
                    s = _causal_mask(s, r0)
                m_new = jnp.maximum(m, jnp.max(s, axis=1, keepdims=True))
                p = jnp.exp(s - m_new)
                a = jnp.exp(m - m_new)
                l = a * l + jnp.sum(p, axis=1, keepdims=True)
                acc = a * acc + lax.dot_general(p.astype(BF16), v_ref[j, 0:ncol, :], _NN, preferred_element_type=F32)
                return m_new, l, acc

            init = (jnp.full((tb, 1), NEG_INF, F32), jnp.zeros((tb, 1), F32), jnp.zeros((tb, lanes), F32))
            m, l, acc = tile(i, lax.fori_loop(0, i, tile, init), masked=True)
            res.append((acc / l, m + jnp.log(l)))
        o_ref[...] = jnp.where(first, res[0][0], res[1][0])
        lse_ref[:, 0:1] = res[0][1]
        lse_ref[:, 1:2] = res[1][1]

    kb, vb = d // lanes, 2 * d // lanes
    return pl.pallas_call(
        body, grid=(nh // 2, nb), name=name,
        in_specs=[pl.BlockSpec((None, tb, lanes), lambda h, i: (i, 0, h)),
                  pl.BlockSpec((nb, tb, lanes), lambda h, i: (0, 0, kb + h)),
                  pl.BlockSpec((nb, tb, lanes), lambda h, i: (0, 0, vb + h)),
                  pl.BlockSpec((None, None, tb, 2), lambda h, i: (h, i, 0, 0)),
                  pl.BlockSpec((None, nb, 2, tb), lambda h, i: (h, 0, 0, 0))],
        out_specs=[pl.BlockSpec((None, tb, lanes), lambda h, i: (i, 0, h)),
                   pl.BlockSpec((None, None, tb, 2), lambda h, i: (h, i, 0, 0))],
        out_shape=[jax.ShapeDtypeStruct((nb, tb, d), F32), jax.ShapeDtypeStruct((nh // 2, nb, tb, 2), F32)],
        compiler_params=_cp(2),
    )(qkv, qkv, qkv, ccol, crow)


def _attn_bwd(name, qkv, ccol, crow, o, lse, do, nh):
    nb, tb, d3 = qkv.shape
    d = d3 // 3
    hd = d // nh
    lanes = 2 * hd
    hb = tb // 2
    scale = 1.0 / math.sqrt(hd)

    def body(q_ref, k_ref, v_ref, cc_ref, cr_ref, o_ref, lse_ref, do_ref, dq_ref, dk_ref, dv_ref, dr_ref, dc_ref, dq_acc):
        j = pl.program_id(1)

        @pl.when(j == 0)
        def _():
            dq_acc[...] = jnp.zeros_like(dq_acc)
            dr_ref[...] = jnp.zeros_like(dr_ref)

        first = _first_head_lanes(hd)
        kj = k_ref[...]
        vj = v_ref[...]
        dk = jnp.zeros((tb, lanes), F32)
        dv = jnp.zeros((tb, lanes), F32)
        for hh in (0, 1):
            mine = first if hh == 0 else jnp.logical_not(first)
            cr = cr_ref[hh:hh + 1, :]

            def tile(i, carry, r0=0, nr=tb, ncol=tb, masked=False, mine=mine, cr=cr, hh=hh):
                dk, dv, dc = carry
                rows = pl.ds(r0, nr)
                qi = q_ref[i, rows, :] * scale
                qh = jnp.where(mine, qi, jnp.zeros_like(qi))
                doh = jnp.where(mine, do_ref[i, rows, :], 0.0)
                dob = doh.astype(BF16)
                di = jnp.sum(doh * o_ref[i, rows, :], axis=1, keepdims=True)
                s = lax.dot_general(qh, kj[:ncol], _NT, preferred_element_type=F32)
                s = s + cc_ref[i, rows, hh:hh + 1] - cr[:, :ncol]
                if masked:
                    s = _causal_mask(s, r0)
                p = jnp.exp(s - lse_ref[i, rows, hh:hh + 1])
                dp = lax.dot_general(dob, vj[:ncol], _NT, preferred_element_type=F32)
                ds = p * (dp - di)
                dsb = ds.astype(BF16)
                dvc = lax.dot_general(p.astype(BF16), dob, _TN, preferred_element_type=F32)
                dkc = lax.dot_general(dsb, qh, _TN, preferred_element_type=F32)
                dcc = jnp.sum(ds, axis=0, keepdims=True)
                if ncol < tb:
                    dvc = jnp.concatenate([dvc, jnp.zeros((tb - ncol, lanes), F32)], axis=0)
                    dkc = jnp.concatenate([dkc, jnp.zeros((tb - ncol, lanes), F32)], axis=0)
                    dcc = jnp.concatenate([dcc, jnp.zeros((1, tb - ncol), F32)], axis=1)
                dq = lax.dot_general(dsb, kj[:ncol], _NN, preferred_element_type=F32) * scale
                dq_acc[i, rows, :] += jnp.where(mine, dq, 0.0)
                dr_ref[i, rows, hh:hh + 1] += jnp.sum(ds, axis=1, keepdims=True)
                return dk + dkc, dv + dvc, dc + dcc

            carry = tile(j, (dk, dv, jnp.zeros((1, tb), F32)), 0, hb, hb, True)
            carry = tile(j, carry, hb, hb, tb, True)
            dk, dv, dc = lax.fori_loop(j + 1, nb, tile, carry)
            dc_ref[hh:hh + 1, :] = dc
        dk_ref[...] = dk.astype(BF16)
        dv_ref[...] = dv.astype(BF16)

        @pl.when(j == nb - 1)
        def _():
            dq_ref[...] = dq_acc[...].astype(BF16)

    kb, vb = d // lanes, 2 * d // lanes
    whole = lambda c: pl.BlockSpec((nb, tb, lanes), lambda h, j: (0, 0, c + h))
    block = lambda c: pl.BlockSpec((None, tb, lanes), lambda h, j: (j, 0, c + h))
    cols = pl.BlockSpec((None, nb, tb, 2), lambda h, j: (h, 0, 0, 0))
    rows = pl.BlockSpec((None, None, 2, tb), lambda h, j: (h, j, 0, 0))
    sd = jax.ShapeDtypeStruct
    return pl.pallas_call(
        body, grid=(nh // 2, nb), name=name,
        in_specs=[whole(0), block(kb), block(vb), cols, rows, whole(0), cols, whole(0)],
        out_specs=[whole(0), block(0), block(0), cols, rows],
        out_shape=[sd((nb, tb, d), BF16), sd((nb, tb, d), BF16), sd((nb, tb, d), BF16),
                   sd((nh // 2, nb, tb, 2), F32), sd((nh // 2, nb, 2, tb), F32)],
        scratch_shapes=[pltpu.VMEM((nb, tb, lanes), F32)],
        compiler_params=_cp(2),
    )(qkv, qkv, qkv, ccol, crow, o, lse, do)


def _cmul(ar, ai, br, bi):
    return ar * br - ai * bi, ar * bi + ai * br


def _s5_scan(name, lam, xin, hs=None):
    reverse = hs is not None
    _, seg, ns, w = xin.shape
    assert ns == SCAN_SEGMENTS
    wb = min(w, 2 * V7X_LANES)
    nsq = seg.bit_length() - 1
    assert (1 << nsq) == seg

    def body(*refs):
        if reverse:
            lam_ref, x_ref, h_ref, o_ref, dl_ref = refs
        else:
            lam_ref, x_ref, o_ref = refs
        lr = jnp.broadcast_to(lam_ref[0], (ns, wb))
        li = jnp.broadcast_to(lam_ref[1], (ns, wb))
        if reverse:
            li = -li
        zero = jnp.zeros((ns, wb), F32)
        at = (lambda n: seg - 1 - n) if reverse else (lambda n: n)

        def local(n, c):
            r = at(n)
            mr, mi = _cmul(lr, li, c[0], c[1])
            nr = mr + x_ref[0, r]
            ni = mi + x_ref[1, r]
            o_ref[0, r] = nr
            o_ref[1, r] = ni
            return nr, ni

        er, ei = lax.fori_loop(0, seg, local, (zero, zero))
        pr, pi = lr, li
        for _ in range(nsq):
            pr, pi = _cmul(pr, pi, pr, pi)
        sub = lax.broadcasted_iota(jnp.int32, (ns, wb), 0)

        def shifted(a, sh):
            if reverse:
                return jnp.where(sub < ns - sh, pltpu.roll(a, ns - sh, 0), 0.0)
            return jnp.where(sub >= sh, pltpu.roll(a, sh, 0), 0.0)

        xr, xi = er, ei
        sh = 1
        while sh < ns:
            tr, ti = _cmul(pr, pi, shifted(xr, sh), shifted(xi, sh))
            xr, xi = xr + tr, xi + ti
            pr, pi = _cmul(pr, pi, pr, pi)
            sh *= 2
        cr, ci = shifted(xr, 1), shifted(xi, 1)

        def fix(r, q):
            tr, ti = _cmul(q[0], q[1], cr, ci)
            gr = o_ref[0, r] + tr
            gi = o_ref[1, r] + ti
            o_ref[0, r] = gr
            o_ref[1, r] = gi
            return gr, gi

        if not reverse:
            def fixup(n, q):
                fix(n, q)
                return _cmul(q[0], q[1], lr, li)

            lax.fori_loop(0, seg, fixup, (lr, li))
            return

        def fixup_acc(n, c):
            qr, qi, ar, ai = c
            r = seg - 1 - n
            gr, gi = fix(r, (qr, qi))
            hr = h_ref[0, r - 1]
            hi = h_ref[1, r - 1]
            qr, qi = _cmul(qr, qi, lr, li)
            return qr, qi, ar + gr * hr + gi * hi, ai + gi * hr - gr * hi

        qr, qi, ar, ai = lax.fori_loop(0, seg - 1, fixup_acc, (lr, li, zero, zero))
        gr, gi = fix(0, (qr, qi))
        hr = jnp.where(sub >= 1, pltpu.roll(h_ref[0, seg - 1], 1, 0), 0.0)
        hi = jnp.where(sub >= 1, pltpu.roll(h_ref[1, seg - 1], 1, 0), 0.0)
        dl_ref[0] = ar + gr * hr + gi * hi
        dl_ref[1] = ai + gi * hr - gr * hi

    big = pl.BlockSpec((2, seg, ns, wb), lambda j: (0, 0, 0, j))
    lam_spec = pl.BlockSpec((2, 1, wb), lambda j: (0, 0, j))
    sd = jax.ShapeDtypeStruct
    if reverse:
        return pl.pallas_call(
            body, grid=(w // wb,), name=name, in_specs=[lam_spec, big, big],
            out_specs=[big, pl.BlockSpec((2, ns, wb), lambda j: (0, 0, j))],
            out_shape=[sd(xin.shape, F32), sd((2, ns, w), F32)], compiler_params=_cp(1),
        )(lam, xin, hs)
    return pl.pallas_call(
        body, grid=(w // wb,), name=name, in_specs=[lam_spec, big], out_specs=big,
        out_shape=sd(xin.shape, F32), compiler_params=_cp(1),
    )(lam, xin)


def _place():
    x, y, c = lax.axis_index("x"), lax.axis_index("y"), lax.axis_index("c")
    chips = [(1 - x, y), (x, 1 - y), (1 - x, 1 - y)]
    return x, y, c, chips


def _comm_params():
    return pltpu.CompilerParams(vmem_limit_bytes=VMEM_LIMIT)


def _cast_place(name, w, after=None):
    nl, r, c = w.shape
    tr = _tile(r, max(16, (1 << 20) // (4 * c) // 16 * 16), 16)

    def body(w_ref, *rest):
        rest[-1][...] = w_ref[...].astype(BF16)

    return pl.pallas_call(
        body, name=name, grid=(nl, r // tr),
        in_specs=[pl.BlockSpec((None, tr, c), lambda l, i: (l, i, 0))] + ([ANY] if after is not None else []),
        out_specs=pl.BlockSpec((None, None, tr, c), lambda l, i: (l, _my_shard(), i, 0)),
        out_shape=jax.ShapeDtypeStruct((nl, N_CHIPS, r, c), BF16),
        compiler_params=_cp(2),
    )(*([w] if after is None else [w, after]))


def _gather_shards(name, bufs):
    n = len(bufs)

    def body(*refs):
        outs = refs[n:2 * n]
        send_sems, recv_sems = refs[2 * n:]
        x, y, c, chips = _place()
        my = 2 * x + y
        sibling = (x, y, 1 - c)

        def part(t, shard, half):
            h = bufs[t].shape[2] // 2
            return outs[t].at[:, shard, pl.ds(half * h, h)]

        def copy(t, k, ref, to):
            return pltpu.make_async_remote_copy(src_ref=ref, dst_ref=ref, send_sem=send_sems.at[t, k],
                                                recv_sem=recv_sems.at[t, k], device_id=to, device_id_type=MESH)

        sent = []
        for t in range(n):
            for k, chip in enumerate(chips):
                sent.append(copy(t, k, part(t, my, c), (*chip, c)))
                sent[-1].start()
        for k, chip in enumerate(chips):
            shard = 2 * chip[0] + chip[1]
            for t in range(n):
                copy(t, k, part(t, shard, c), (*chip, c)).wait_recv()
                sent.append(copy(t, 3 + k, part(t, shard, c), sibling))
                sent[-1].start()
        for k, chip in enumerate(chips):
            shard = 2 * chip[0] + chip[1]
            for t in range(n):
                copy(t, 3 + k, part(t, shard, 1 - c), sibling).wait_recv()
        for cp in sent:
            cp.wait_send()

    return pl.pallas_call(
        body, name=name, in_specs=[ANY] * n, out_specs=[ANY] * n,
        out_shape=[jax.ShapeDtypeStruct(b.shape, b.dtype) for b in bufs],
        input_output_aliases={t: t for t in range(n)},
        scratch_shapes=[pltpu.SemaphoreType.DMA((n, 6)), pltpu.SemaphoreType.DMA((n, 6))],
        compiler_params=_comm_params(),
    )(*bufs)


HBM_SPEC = pl.BlockSpec(memory_space=pltpu.HBM)
SEM_SPEC = pl.BlockSpec(memory_space=pltpu.SEMAPHORE)


def _split_params():
    return pltpu.CompilerParams(has_side_effects=pltpu.SideEffectType.DATAFLOW_SIDE_EFFECTING)


def _gather_start(name, bufs, groups):
    n, ng = len(bufs), len(groups)

    def body(*refs):
        sems = refs[n:n + 2 * ng]
        outs = refs[n + 2 * ng:]
        x, y, c, chips = _place()
        my = 2 * x + y
        for gi, group in enumerate(groups):
            for idx, (t, layer) in enumerate(group):
                block = outs[t].at[layer, my]
                for k, chip in enumerate(chips):
                    pltpu.make_async_remote_copy(
                        src_ref=block, dst_ref=block, send_sem=sems[2 * gi].at[3 * idx + k],
                        recv_sem=sems[2 * gi + 1].at[3 * idx + k], device_id=(*chip, c), device_id_type=MESH).start()

    sem_shapes = []
    for group in groups:
        sem_shapes += [pltpu.SemaphoreType.DMA((3 * len(group),))] * 2
    res = pl.pallas_call(
        body, name=name, in_specs=[HBM_SPEC] * n,
        out_specs=[SEM_SPEC] * (2 * ng) + [HBM_SPEC] * n,
        out_shape=sem_shapes + [pltpu.HBM(b.shape, b.dtype) for b in bufs],
        input_output_aliases={t: 2 * ng + t for t in range(n)},
        compiler_params=_split_params(),
    )(*[pltpu.with_memory_space_constraint(b, pltpu.HBM) for b in bufs])
    sems = [(res[2 * gi], res[2 * gi + 1]) for gi in range(ng)]
    return sems, list(res[2 * ng:])


def _gather_wait(name, bufs, send_sems, recv_sems, after, group):
    n = len(bufs)

    def body(*refs):
        ss, rs = refs[n], refs[n + 1]
        outs = refs[n + 3:]
        x, y, c, chips = _place()
        my = 2 * x + y
        for idx, (t, layer) in enumerate(group):
            for k, chip in enumerate(chips):
                cp = pltpu.make_async_remote_copy(
                    src_ref=outs[t].at[layer, my], dst_ref=outs[t].at[layer, 2 * chip[0] + chip[1]],
                    send_sem=ss.at[3 * idx + k], recv_sem=rs.at[3 * idx + k], device_id=(*chip, c), device_id_type=MESH)
                cp.wait_send()
                cp.wait_recv()

    return list(pl.pallas_call(
        body, name=name, in_specs=[HBM_SPEC] * n + [SEM_SPEC, SEM_SPEC, ANY],
        out_specs=[HBM_SPEC] * n,
        out_shape=[pltpu.HBM(b.shape, b.dtype) for b in bufs],
        input_output_aliases={t: t for t in range(n)},
        compiler_params=_split_params(),
    )(*bufs, send_sems, recv_sems, after))


N_PARTS = 7


def _scatter_items(send, rx, items, c, chips, x, y):
    my = 2 * x + y
    out = []
    for i, (k, layer) in enumerate(items):
        h = send[k].shape[2] // 2
        for kk, chip in enumerate(chips):
            shard = 2 * chip[0] + chip[1]
            for hf in (0, 1):
                out.append((send[k].at[layer, shard, pl.ds(hf * h, h)], rx[k].at[2 * kk + c, layer],
                            N_PARTS * i + 2 * kk + hf, N_PARTS * i + 2 * kk + c, (*chip, hf)))
        out.append((send[k].at[layer, my, pl.ds((1 - c) * h, h)], rx[k].at[N_PARTS - 1, layer],
                    N_PARTS * i + N_PARTS - 1, N_PARTS * i + N_PARTS - 1, (x, y, 1 - c)))
    return out


def _scatter_start(name, send, rx, items):
    n = len(send)
    m = N_PARTS * len(items)

    def body(*refs):
        ssem, rsem = refs[2 * n], refs[2 * n + 1]
        s_out, r_out = refs[2 * n + 2:3 * n + 2], refs[3 * n + 2:4 * n + 2]
        x, y, c, chips = _place()
        for src, dst, si, ri, to in _scatter_items(s_out, r_out, items, c, chips, x, y):
            pltpu.make_async_remote_copy(src_ref=src, dst_ref=dst, send_sem=ssem.at[si], recv_sem=rsem.at[ri],
                                         device_id=to, device_id_type=MESH).start()
        refs[4 * n + 2][...] = jnp.zeros((V7X_SUBLANES, V7X_LANES), F32)

    res = pl.pallas_call(
        body, name=name, in_specs=[HBM_SPEC] * (2 * n),
        out_specs=[SEM_SPEC, SEM_SPEC] + [HBM_SPEC] * (2 * n) + [pl.BlockSpec(memory_space=pltpu.VMEM)],
        out_shape=[pltpu.SemaphoreType.DMA((m,)), pltpu.SemaphoreType.DMA((m,))]
        + [pltpu.HBM(b.shape, b.dtype) for b in list(send) + list(rx)]
        + [jax.ShapeDtypeStruct((V7X_SUBLANES, V7X_LANES), F32)],
        input_output_aliases={t: 2 + t for t in range(2 * n)},
        compiler_params=_split_params(),
    )(*[pltpu.with_memory_space_constraint(b, pltpu.HBM) for b in list(send) + list(rx)])
    return (res[0], res[1]), list(res[2:2 + n]), list(res[2 + n:2 + 2 * n]), res[2 + 2 * n][0, 0]


def _scatter_wait(name, send, rx, ssem, rsem, after, items):
    n = len(send)

    def body(*refs):
        ss, rs = refs[2 * n], refs[2 * n + 1]
        s_out, r_out = refs[2 * n + 3:3 * n + 3], refs[3 * n + 3:]
        x, y, c, chips = _place()
        for i, (src, dst, si, ri, to) in enumerate(_scatter_items(s_out, r_out, items, c, chips, x, y)):
            arrival = i % N_PARTS
            landed = r_out[items[i // N_PARTS][0]].at[arrival, items[i // N_PARTS][1]]
            cp = pltpu.make_async_remote_copy(src_ref=src, dst_ref=landed, send_sem=ss.at[si],
                                              recv_sem=rs.at[N_PARTS * (i // N_PARTS) + arrival],
                                              device_id=to, device_id_type=MESH)
            cp.wait_send()
            cp.wait_recv()

    res = pl.pallas_call(
        body, name=name, in_specs=[HBM_SPEC] * (2 * n) + [SEM_SPEC, SEM_SPEC, ANY],
        out_specs=[HBM_SPEC] * (2 * n),
        out_shape=[pltpu.HBM(b.shape, b.dtype) for b in list(send) + list(rx)],
        input_output_aliases={t: t for t in range(2 * n)},
        compiler_params=_split_params(),
    )(*send, *rx, ssem, rsem, after)
    return list(res[:n]), list(res[n:])


def _chip_sum(name, g, rx):
    nl, _, r, c = g.shape
    h = r // 2
    tr = _tile(h, max(V7X_SUBLANES * 2, (1 << 19) // (2 * c) // 16 * 16), 16)
    nb = h // tr

    def body(g_ref, r_ref, o_ref):
        acc = g_ref[...].astype(F32)
        for k in range(N_PARTS):
            acc = acc + r_ref[k].astype(F32)
        o_ref[...] = acc

    return pl.pallas_call(
        body, name=name, grid=(nl, nb),
        in_specs=[pl.BlockSpec((None, None, tr, c), lambda l, i: (l, _my_shard(), _my_core() * nb + i, 0)),
                  pl.BlockSpec((N_PARTS, None, tr, c), lambda l, i: (0, l, i, 0))],
        out_specs=pl.BlockSpec((None, tr, c), lambda l, i: (l, i, 0)),
        out_shape=jax.ShapeDtypeStruct((nl, h, c), F32),
        compiler_params=_cp(2),
    )(g, rx)


def _send_half(name, fs):
    n = len(fs)

    def body(*refs):
        ins, outs = refs[:n], refs[n:2 * n]
        send_sems, recv_sems = refs[2 * n:]
        x, y, c, _ = _place()
        cps = []
        for t in range(n):
            cps.append(pltpu.make_async_remote_copy(
                src_ref=ins[t], dst_ref=outs[t], send_sem=send_sems.at[t], recv_sem=recv_sems.at[t],
                device_id=(x, y, 1 - c), device_id_type=MESH))
            cps[-1].start()
        for cp in cps:
            cp.wait()

    return pl.pallas_call(
        body, name=name, in_specs=[ANY] * n, out_specs=[ANY] * n,
        out_shape=[jax.ShapeDtypeStruct(f.shape, f.dtype) for f in fs],
        scratch_shapes=[pltpu.SemaphoreType.DMA((n,)), pltpu.SemaphoreType.DMA((n,))],
        compiler_params=_comm_params(),
    )(*fs)


def _peers(x, y, c):
    rel = [(dx, dy, dc) for dx in (0, 1) for dy in (0, 1) for dc in (0, 1) if (dx, dy, dc) != (0, 0, 0)]
    return [(1 - x if dx else x, 1 - y if dy else y, 1 - c if dc else c) for dx, dy, dc in rel]


def _share_start(name, v, land):
    def body(v_ref, land_ref, ssem, rsem, v_out, land_out, token):
        x, y, c, _ = _place()
        me = 4 * x + 2 * y + c
        for k, peer in enumerate(_peers(x, y, c)):
            pltpu.make_async_remote_copy(src_ref=v_out, dst_ref=land_out.at[me], send_sem=ssem.at[k],
                                         recv_sem=rsem.at[k], device_id=peer, device_id_type=MESH).start()
        token[...] = jnp.zeros((V7X_SUBLANES, V7X_LANES), F32)

    res = pl.pallas_call(
        body, name=name, in_specs=[HBM_SPEC, HBM_SPEC],
        out_specs=[SEM_SPEC, SEM_SPEC, HBM_SPEC, HBM_SPEC, pl.BlockSpec(memory_space=pltpu.VMEM)],
        out_shape=[pltpu.SemaphoreType.DMA((N_DEV - 1,)), pltpu.SemaphoreType.DMA((N_DEV - 1,)),
                   pltpu.HBM(v.shape, v.dtype), pltpu.HBM(land.shape, land.dtype),
                   jax.ShapeDtypeStruct((V7X_SUBLANES, V7X_LANES), F32)],
        input_output_aliases={0: 2, 1: 3},
        compiler_params=_split_params(),
    )(pltpu.with_memory_space_constraint(v, pltpu.HBM), pltpu.with_memory_space_constraint(land, pltpu.HBM))
    return (res[0], res[1]), res[2], res[3], res[4][0, 0]


def _share_wait(name, v, land, ssem, rsem, after):
    def body(v_ref, land_ref, ss, rs, after_ref, v_out, land_out):
        x, y, c, _ = _place()
        for k, (px, py, pc) in enumerate(_peers(x, y, c)):
            cp = pltpu.make_async_remote_copy(src_ref=v_out, dst_ref=land_out.at[4 * px + 2 * py + pc],
                                              send_sem=ss.at[k], recv_sem=rs.at[k], device_id=(px, py, pc),
                                              device_id_type=MESH)
            cp.wait_send()
            cp.wait_recv()

    res = pl.pallas_call(
        body, name=name, in_specs=[HBM_SPEC, HBM_SPEC, SEM_SPEC, SEM_SPEC, ANY],
        out_specs=[HBM_SPEC, HBM_SPEC],
        out_shape=[pltpu.HBM(v.shape, v.dtype), pltpu.HBM(land.shape, land.dtype)],
        input_output_aliases={0: 0, 1: 1},
        compiler_params=_split_params(),
    )(v, land, ssem, rsem, after)
    return res[0], res[1]


def _sum_devices(name, v, land):
    r, c = v.shape
    tr = _tile(r, 512)

    def body(v_ref, land_ref, o_ref):
        x, y, cc, _ = _place()
        me = 4 * x + 2 * y + cc
        own = v_ref[...]
        acc = jnp.where(me == 0, own, land_ref[0])
        for k in range(1, N_DEV):
            acc = acc + jnp.where(me == k, own, land_ref[k])
        o_ref[...] = acc

    return pl.pallas_call(
        body, grid=(r // tr,), name=name,
        in_specs=[pl.BlockSpec((tr, c), lambda i: (i, 0)), pl.BlockSpec((N_DEV, tr, c), lambda i: (0, i, 0))],
        out_specs=pl.BlockSpec((tr, c), lambda i: (i, 0)),
        out_shape=jax.ShapeDtypeStruct((r, c), F32), compiler_params=_cp(1),
    )(v, land)


def _rows_view(wall):
    nl, s, r, c = wall.shape
    return wall.reshape(nl, s * r, c)


def _ffn_fwd(tag, alpha, x, xb, w_in, w_out3, layer, g, b):
    h, a = _ffn_in(f"{tag}_in", xb, w_in, layer)
    y, xhat, rstd, yb = _mm_ln(f"{tag}_out", alpha, 0.5, a, w_out3, layer, x, g, b)
    return y, yb, (xb, h, a, xhat, rstd)


def _dx_ln(name, a, b, *, nk, a_blk, a_map, b_blk, b_map, alpha, dz, nxt):
    t, d = dz.shape
    tm = a_blk[-2]
    xhat, rstd, g = nxt
    through = xhat is not None

    def body(*refs):
        a_ref, b_ref, dz_ref = refs[:3]
        if through:
            xh_ref, rs_ref, g_ref, o_ref, dg_ref, db_ref, acc = refs[3:]
        else:
            g_ref, o_ref, acc = refs[3:]
        i, kk = pl.program_id(0), pl.program_id(1)
        p = lax.dot_general(a_ref[...].astype(BF16), b_ref[...].astype(BF16), _NT, preferred_element_type=F32)

        @pl.when(kk == 0)
        def _():
            acc[...] = p

        @pl.when(kk > 0)
        def _():
            acc[...] += p

        @pl.when(kk == nk - 1)
        def _():
            dy = alpha * dz_ref[...] + acc[...]
            if not through:
                o_ref[...] = dy + g_ref[...]
                return
            xh = xh_ref[...]
            dxh = dy * g_ref[...]
            m1 = jnp.mean(dxh, axis=-1, keepdims=True)
            m2 = jnp.mean(dxh * xh, axis=-1, keepdims=True)
            o_ref[...] = rs_ref[...] * (dxh - m1 - xh * m2)
            pg = jnp.sum(dy * xh, axis=0, keepdims=True)
            pb = jnp.sum(dy, axis=0, keepdims=True)

            @pl.when(i == 0)
            def _():
                dg_ref[...] = pg
                db_ref[...] = pb

            @pl.when(i > 0)
            def _():
                dg_ref[...] += pg
                db_ref[...] += pb

    row = lambda c: pl.BlockSpec((tm, c), lambda i, kk: (i, 0))
    vec = pl.BlockSpec((1, d), lambda i, kk: (0, 0))
    sd = jax.ShapeDtypeStruct
    in_specs = [pl.BlockSpec(a_blk, a_map), pl.BlockSpec(b_blk, b_map), row(d)]
    args = [a, b, dz]
    if through:
        in_specs += [row(d), row(1), vec]
        args += [xhat, rstd, g]
        out_specs, out_shape = [row(d), vec, vec], [sd((t, d), F32), sd((1, d), F32), sd((1, d), F32)]
    else:
        in_specs += [vec]
        args += [g]
        out_specs, out_shape = row(d), sd((t, d), F32)
    return pl.pallas_call(
        body, grid=(t // tm, nk), name=name, in_specs=in_specs, out_specs=out_specs, out_shape=out_shape,
        scratch_shapes=[pltpu.VMEM((tm, d), F32)], compiler_params=_cp(2),
    )(*args)


def _ffn_bwd(tag, alpha, dz, saved, w_in, w_out3, layer, g_win, g_wout3, grads_done, nxt, zrow):
    x, h, a, _, _ = saved
    t = x.shape[0]
    _, s, k, n = w_in.shape
    tm = _tile(t, 512)
    g_wout3 = _mm_tn(f"{tag}_dwout", a, dz, tm=n, scale=0.5, layer=layer, into=g_wout3)
    dh = _ffn_da(f"{tag}_da", dz, w_out3, layer, h, zrow)
    g_win = _mm(f"{tag}_dwin", x, dh, mode="tn", grid=(s, t // tm), kaxis=1,
                a_blk=(tm, k), a_map=lambda j, kk: (kk, 0),
                b_blk=(None, tm, n), b_map=lambda j, kk: (j // 2, kk, j % 2),
                o_shape=w_in.shape, o_blk=(None, None, k, n), o_map=lambda j, kk: (layer, j, 0, 0),
                o_dtype=g_win.dtype, into=g_win)
    zero = grads_done(g_win, g_wout3)
    return _dx_ln(f"{tag}_dx", dh, w_in, nk=s, a_blk=(None, tm, n), a_map=lambda i, kk: (kk // 2, i, kk % 2),
                  b_blk=(None, None, k, n), b_map=lambda i, kk: (layer, kk, 0, 0),
                  alpha=alpha, dz=dz, nxt=(nxt[0], nxt[1], nxt[2] + zero))


def _fox_fwd(tag, alpha, x, w_pad, bf, w_o3, layer, g, b):
    t, d = x.shape
    nh = bf.shape[1]
    tb = _tile(t, ATTN_BLOCK)
    nb = t // tb
    qkv = _mm_nn(f"{tag}_qkv", x, w_pad[:, :3 * d], o_dtype=BF16, tn=d).reshape(nb, tb, 3 * d)
    fl = _mm_nn(f"{tag}_gate", x, w_pad[:, 3 * d:])[:, :nh]
    cum = _fox_cumsum(f"{tag}_cum", fl, bf)
    ccol = cum.reshape(nb, tb, nh // 2, 2).transpose(2, 0, 1, 3)
    crow = cum.reshape(nb, tb, nh // 2, 2).transpose(2, 0, 3, 1)
    o, lse = _attn_fwd(f"{tag}_attn", qkv, ccol, crow, nh)
    o2 = o.reshape(t, d)
    y, xhat, rstd, yb = _mm_ln(f"{tag}_oproj", alpha, 1.0, o2, w_o3, layer, x, g, b)
    return y, yb, xhat, rstd, (x, qkv, ccol, crow, o, lse, fl, w_pad)


def _fox_bwd(tag, alpha, dm, saved, bf, w_o3, layer, g_wo3, grads_done, nxt):
    x, qkv, ccol, crow, o, lse, fl, w_pad = saved
    t, d = x.shape
    nh = bf.shape[1]
    nb, tb, _ = qkv.shape
    tm = _tile(t, 512)
    g_wo3 = _mm_tn(f"{tag}_dwo", o.reshape(t, d), dm, layer=layer, into=g_wo3)
    do = _mm_nt(f"{tag}_do", dm, w_o3, layer=layer).reshape(nb, tb, d)
    dq, dk, dv, drow, dcol = _attn_bwd(f"{tag}_attn_bwd", qkv, ccol, crow, o, lse, do, nh)
    dcum = drow.transpose(1, 2, 0, 3).reshape(t, nh) - dcol.transpose(1, 3, 0, 2).reshape(t, nh)
    dfl, dbf = _fox_cumsum_bwd(f"{tag}_cum_bwd", dcum, fl, bf)
    pad = w_pad.shape[1] - 3 * d - nh
    dproj = jnp.concatenate([dq.reshape(t, d), dk.reshape(t, d), dv.reshape(t, d),
                             dfl.astype(BF16), jnp.zeros((t, pad), BF16)], axis=1)
    d_wpad = _mm_tn(f"{tag}_dwin", x, dproj, tn=_tile(w_pad.shape[1], 640, V7X_LANES))
    zero = grads_done(g_wo3, d_wpad)
    cols = w_pad.shape[1]
    out = _dx_ln(f"{tag}_dx", dproj, w_pad, nk=1, a_blk=(tm, cols), a_map=lambda i, kk: (i, 0),
                 b_blk=(d, cols), b_map=lambda i, kk: (0, 0), alpha=alpha, dz=dm, nxt=(nxt[0], nxt[1], nxt[2] + zero))
    return out, dbf


def _to_segments(a):
    t, d = a.shape
    return a.reshape(SCAN_SEGMENTS, t // SCAN_SEGMENTS, d).transpose(1, 0, 2).reshape(t, d)


def _from_segments(a):
    t, d = a.shape
    return a.reshape(t // SCAN_SEGMENTS, SCAN_SEGMENTS, d).transpose(1, 0, 2).reshape(t, d)


def _s5_discretise(a_re, a_im, log_dt, b_re, b_im):
    dt = jnp.exp(log_dt)[:, None]
    mag = jnp.exp(a_re * dt)
    ang = a_im * dt
    lb_re = mag * jnp.cos(ang)
    lb_im = mag * jnp.sin(ang)
    den = a_re * a_re + a_im * a_im
    nr = lb_re - 1.0
    ni = lb_im
    z_re = (nr * a_re + ni * a_im) / den
    z_im = (ni * a_re - nr * a_im) / den
    bb_re = z_re[..., None] * b_re - z_im[..., None] * b_im
    bb_im = z_re[..., None] * b_im + z_im[..., None] * b_re
    return lb_re, lb_im, bb_re, bb_im


S5_BLOCK_GROUPS = 8


def _blockdiag_in(bb):
    g, p, h = bb.shape
    e = jnp.eye(S5_BLOCK_GROUPS, dtype=bb.dtype)
    b4 = bb.reshape(g // S5_BLOCK_GROUPS, S5_BLOCK_GROUPS, p, h)
    return jnp.einsum("jgph,gf->jghfp", b4, e).reshape(g // S5_BLOCK_GROUPS, S5_BLOCK_GROUPS * h, S5_BLOCK_GROUPS * p)


def _blockdiag_in_grad(d):
    nj, gh, gp = d.shape
    h, p = gh // S5_BLOCK_GROUPS, gp // S5_BLOCK_GROUPS
    e = jnp.eye(S5_BLOCK_GROUPS, dtype=d.dtype)
    d6 = d.reshape(nj, S5_BLOCK_GROUPS, h, S5_BLOCK_GROUPS, p)
    return jnp.einsum("jghfp,gf->jgph", d6, e).reshape(nj * S5_BLOCK_GROUPS, p, h)


def _blockdiag_out(cc):
    g, h, p = cc.shape
    e = jnp.eye(S5_BLOCK_GROUPS, dtype=cc.dtype)
    c4 = cc.reshape(g // S5_BLOCK_GROUPS, S5_BLOCK_GROUPS, h, p)
    return jnp.einsum("jghp,gf->jfpgh", c4, e).reshape(g // S5_BLOCK_GROUPS, S5_BLOCK_GROUPS * p, S5_BLOCK_GROUPS * h)


def _blockdiag_out_grad(d):
    nj, gp, gh = d.shape
    h, p = gh // S5_BLOCK_GROUPS, gp // S5_BLOCK_GROUPS
    e = jnp.eye(S5_BLOCK_GROUPS, dtype=d.dtype)
    d6 = d.reshape(nj, S5_BLOCK_GROUPS, p, S5_BLOCK_GROUPS, h)
    return jnp.einsum("jfpgh,gf->jghp", d6, e).reshape(nj * S5_BLOCK_GROUPS, h, p)


def _s5_fwd(tag, x, prm, w_out, layer):
    a_re, a_im, log_dt, b_re, b_im, c_re, c_im, d_skip = prm
    t, d = x.shape
    g, p = a_re.shape
    w = g * p
    nj = g // S5_BLOCK_GROUPS
    cw, sw = S5_BLOCK_GROUPS * S5_GROUP, S5_BLOCK_GROUPS * p
    seg = t // SCAN_SEGMENTS
    tm = _tile(t, 4096)
    lb_re, lb_im, bb_re, bb_im = _s5_discretise(a_re, a_im, log_dt, b_re, b_im)
    lam = jnp.stack([lb_re.reshape(1, w), lb_im.reshape(1, w)])
    bs = jnp.stack([_blockdiag_in(bb_re), _blockdiag_in(bb_im)]).astype(BF16)
    cs = jnp.stack([_blockdiag_out(c_re), -_blockdiag_out(c_im)]).astype(BF16)
    dvec = d_skip.reshape(1, d)
    u = _to_segments(x)
    bu = _mm(f"{tag}_bu", u, bs, mode="nn", grid=(2, nj, t // tm), kaxis=None,
             a_blk=(tm, cw), a_map=lambda r, j, i: (i, j),
             b_blk=(None, None, cw, sw), b_map=lambda r, j, i: (r, j, 0, 0),
             o_shape=(2, t, w), o_blk=(None, tm, sw), o_map=lambda r, j, i: (r, i, j))
    hs = _s5_scan(f"{tag}_scan", lam, bu.reshape(2, seg, SCAN_SEGMENTS, w)).reshape(2, t, w)
    ych = _mm(f"{tag}_ch", hs, cs, mode="nn", grid=(nj, t // tm, 2), kaxis=2,
              a_blk=(None, tm, sw), a_map=lambda j, i, r: (r, i, j),
              b_blk=(None, None, sw, cw), b_map=lambda j, i, r: (r, j, 0, 0),
              o_shape=(t, d), o_blk=(tm, cw), o_map=lambda j, i, r: (i, j))
    ypre, act = _s5_act_fwd(f"{tag}_act", ych, u, dvec)
    vg = _mm_shards_nn(f"{tag}_wout", act, w_out, layer, F32)
    m = _from_segments(_glu_fwd(f"{tag}_glu", vg))
    return m, (u, lam, bs, cs, dvec, hs, ypre, act, vg)


def _s5_bwd(tag, dm, saved, prm, w_out, layer, g_wout):
    a_re, a_im, log_dt, b_re, b_im, c_re, c_im, d_skip = prm
    u, lam, bs, cs, dvec, hs, ypre, act, vg = saved
    t, d = u.shape
    g, p = a_re.shape
    w = g * p
    nj = g // S5_BLOCK_GROUPS
    cw, sw = S5_BLOCK_GROUPS * S5_GROUP, S5_BLOCK_GROUPS * p
    seg = t // SCAN_SEGMENTS
    tm = _tile(t, 4096)
    dvg = _glu_bwd(f"{tag}_glu_bwd", _to_segments(dm), vg)
    g_wout = _mm_shards_tn(f"{tag}_dwout", act, dvg, layer, g_wout)
    dact = _mm_shards_nt(f"{tag}_dact", dvg, w_out, layer)
    dypre, duskip, dd = _s5_act_bwd(f"{tag}_act_bwd", dact, ypre, u, dvec)
    dh = _mm(f"{tag}_dh", dypre, cs, mode="nt", grid=(2, nj, t // tm), kaxis=None,
             a_blk=(tm, cw), a_map=lambda r, j, i: (i, j),
             b_blk=(None, None, sw, cw), b_map=lambda r, j, i: (r, j, 0, 0),
             o_shape=(2, t, w), o_blk=(None, tm, sw), o_map=lambda r, j, i: (r, i, j))
    dcs = _mm(f"{tag}_dc", hs, dypre, mode="tn", grid=(2, nj, t // tm), kaxis=2,
              a_blk=(None, tm, sw), a_map=lambda r, j, i: (r, i, j),
              b_blk=(tm, cw), b_map=lambda r, j, i: (i, j),
              o_shape=(2, nj, sw, cw), o_blk=(None, None, sw, cw), o_map=lambda r, j, i: (r, j, 0, 0))
    gs, dlam8 = _s5_scan(f"{tag}_scan_bwd", lam, dh.reshape(2, seg, SCAN_SEGMENTS, w),
                         hs.reshape(2, seg, SCAN_SEGMENTS, w))
    gs = gs.reshape(2, t, w)
    du = _mm(f"{tag}_du", gs, bs, mode="nt", grid=(nj, t // tm, 2), kaxis=2,
             a_blk=(None, tm, sw), a_map=lambda j, i, r: (r, i, j),
             b_blk=(None, None, cw, sw), b_map=lambda j, i, r: (r, j, 0, 0),
             o_shape=(t, d), o_blk=(tm, cw), o_map=lambda j, i, r: (i, j))
    dbs = _mm(f"{tag}_db", u, gs, mode="tn", grid=(2, nj, t // tm), kaxis=2,
              a_blk=(tm, cw), a_map=lambda r, j, i: (i, j),
              b_blk=(None, tm, sw), b_map=lambda r, j, i: (r, i, j),
              o_shape=(2, nj, cw, sw), o_blk=(None, None, cw, sw), o_map=lambda r, j, i: (r, j, 0, 0))
    dx = _from_segments(du + duskip)
    dlam = jnp.sum(dlam8, axis=1).reshape(2, g, p)
    small = dict(dlb_re=dlam[0], dlb_im=dlam[1],
                 dbb_re=_blockdiag_in_grad(dbs[0]), dbb_im=_blockdiag_in_grad(dbs[1]),
                 dc_re=_blockdiag_out_grad(dcs[0]), dc_im=-_blockdiag_out_grad(dcs[1]),
                 dd=dd.reshape(g, S5_GROUP))
    return dx, g_wout, small


def _pack(pieces):
    rows = []
    for p in pieces:
        flat = p.reshape(-1).astype(F32)
        n = flat.shape[0]
        rows.append(jnp.pad(flat, (0, -n % V7X_LANES)).reshape(-1, V7X_LANES))
    buf = jnp.concatenate(rows, axis=0)
    return jnp.pad(buf, ((0, -buf.shape[0] % V7X_SUBLANES), (0, 0)))


def _unpack(buf, shapes):
    out, row = [], 0
    for s in shapes:
        n = math.prod(s)
        nr = -(-n // V7X_LANES)
        out.append(buf[row:row + nr].reshape(-1)[:n].reshape(s))
        row += nr
    return out


def kernel(x, ffn1_w_in, ffn1_w_out, ln1_g, ln1_b, lnm_g, lnm_b, ffn2_w_in, ffn2_w_out, ln2_g, ln2_b, fox_w_in, fox_b_f, fox_w_o, s5_a_re, s5_a_im, s5_log_dt, s5_b_re, s5_b_im, s5_c_re, s5_c_im, s5_d, s5_w_out, loss_target, m_ffn1_w_in, m_ffn1_w_out, m_ln1_g, m_ln1_b, m_lnm_g, m_lnm_b, m_ffn2_w_in, m_ffn2_w_out, m_ln2_g, m_ln2_b, m_fox_w_in, m_fox_b_f, m_fox_w_o, m_s5_a_re, m_s5_a_im, m_s5_log_dt, m_s5_b_re, m_s5_b_im, m_s5_c_re, m_s5_c_im, m_s5_d, m_s5_w_out, v_ffn1_w_in, v_ffn1_w_out, v_ln1_g, v_ln1_b, v_lnm_g, v_lnm_b, v_ffn2_w_in, v_ffn2_w_out, v_ln2_g, v_ln2_b, v_fox_w_in, v_fox_b_f, v_fox_w_o, v_s5_a_re, v_s5_a_im, v_s5_log_dt, v_s5_b_re, v_s5_b_im, v_s5_c_re, v_s5_c_im, v_s5_d, v_s5_w_out):
    big_names = ["ffn1_w_in", "ffn1_w_out", "ffn2_w_in", "ffn2_w_out", "fox_w_in", "fox_w_o", "s5_w_out"]
    small_names = ["ln1_g", "ln1_b", "lnm_g", "lnm_b", "ln2_g", "ln2_b", "fox_b_f", "s5_a_re", "s5_a_im", "s5_log_dt",
                   "s5_b_re", "s5_b_im", "s5_c_re", "s5_c_im", "s5_d"]
    out_order = ["ffn1_w_in", "ffn1_w_out", "ln1_g", "ln1_b", "lnm_g", "lnm_b", "ffn2_w_in", "ffn2_w_out", "ln2_g",
                 "ln2_b", "fox_w_in", "fox_b_f", "fox_w_o", "s5_a_re", "s5_a_im", "s5_log_dt", "s5_b_re", "s5_b_im",
                 "s5_c_re", "s5_c_im", "s5_d", "s5_w_out"]
    env = dict(locals())
    w = {n: env[n] for n in out_order}
    mom = {n: env["m_" + n] for n in out_order}
    vel = {n: env["v_" + n] for n in out_order}

    depth, d = ln1_g.shape
    t = x.shape[1]
    alpha = (2.0 * depth) ** 0.25
    x0 = x.reshape(t, d)
    tgt = loss_target.reshape(t, d)

    tix = {n: k for k, n in enumerate(big_names)}
    groups = []
    for i in range(depth):
        j = i // 2
        groups.append([(tix["ffn1_w_in"], i), (tix["ffn1_w_out"], i)])
        mixer = [(tix["fox_w_in"], j), (tix["fox_w_o"], j)] if i % 2 == 0 else [(tix["s5_w_out"], j)]
        groups.append(mixer)
        groups.append([(tix["ffn2_w_in"], i), (tix["ffn2_w_out"], i)])
    first = groups[0]
    cast = {big_names[k]: _cast_place(f"cast_{big_names[k]}", w[big_names[k]]) for k, _ in first}
    sems, sent = _gather_start("gather_start_first", list(cast.values()), [[(p, layer) for p, (_, layer) in enumerate(first)]])
    cast = dict(zip(cast, sent))
    for n in big_names:
        if n not in cast:
            cast[n] = _cast_place(f"cast_{n}", w[n], after=sent[0])
    more, bufs = _gather_start("gather_start", [cast[n] for n in big_names], groups[1:])
    sems = sems + more
    full, rows3 = {}, {}

    def arrive(gi, after):
        nonlocal bufs
        bufs = _gather_wait(f"gather_wait_{gi}", bufs, sems[gi][0], sems[gi][1], after, groups[gi])
        full.update(zip(big_names, bufs))
        rows3.update({n: _rows_view(full[n]) for n in ("ffn1_w_out", "ffn2_w_out", "fox_w_o")})

    nh = fox_b_f.shape[1]
    fox_cols = 3 * d + nh
    fox_pad = -(-fox_cols // (5 * V7X_LANES)) * (5 * V7X_LANES)

    def fox_wpad(j):
        wf = full["fox_w_in"][j].transpose(1, 0, 2).reshape(d, fox_cols)
        return jnp.pad(wf, ((0, 0), (0, fox_pad - fox_cols)))

    def s5_params(j):
        return (s5_a_re[j], s5_a_im[j], s5_log_dt[j], s5_b_re[j], s5_b_im[j], s5_c_re[j], s5_c_im[j], s5_d[j])

    saved = []
    h = hb = x0
    for i in range(depth):
        j = i // 2
        arrive(3 * i, h)
        h, _, s1 = _ffn_fwd(f"l{i}_ffn1", alpha, h, hb, full["ffn1_w_in"], rows3["ffn1_w_out"], i,
                            ln1_g[i:i + 1], ln1_b[i:i + 1])
        arrive(3 * i + 1, h)
        if i % 2 == 0:
            h, hb, xhat_m, rstd_m, sm = _fox_fwd(f"l{i}_fox", alpha, h, fox_wpad(j), fox_b_f[j:j + 1],
                                                 rows3["fox_w_o"], j, lnm_g[i:i + 1], lnm_b[i:i + 1])
        else:
            m, sm = _s5_fwd(f"l{i}_s5", h, s5_params(j), full["s5_w_out"], j)
            h, xhat_m, rstd_m = _ln_fwd(f"l{i}_lnm", alpha, h, m, 1.0, lnm_g[i:i + 1], lnm_b[i:i + 1])
            hb = h
        arrive(3 * i + 2, h)
        h, hb, s2 = _ffn_fwd(f"l{i}_ffn2", alpha, h, hb, full["ffn2_w_in"], rows3["ffn2_w_out"], i,
                             ln2_g[i:i + 1], ln2_b[i:i + 1])
        saved.append((s1, sm, (xhat_m, rstd_m), s2))
    loss_part = _loss_sum("loss", h, tgt) * (0.5 / d)

    fox_in_names = [f"fox_w_in_l{j}" for j in range(fox_w_in.shape[0])]
    gshape = {n: full[n].shape for n in big_names if n != "fox_w_in"}
    gshape.update({n: (1,) + full["fox_w_in"].shape[1:] for n in fox_in_names})
    gbuf = {n: lax.empty(s, BF16) for n, s in gshape.items()}
    rxbuf = {n: lax.empty((N_PARTS, s[0], s[2] // 2, s[3]), BF16) for n, s in gshape.items()}
    pending = []

    zero = jnp.zeros((), F32)

    def scatter(tag, pairs):
        nonlocal zero
        names = list(dict.fromkeys(n for n, _ in pairs))
        items = [(names.index(n), layer) for n, layer in pairs]
        sem, send, rx, zero = _scatter_start(f"scatter_start_{tag}", [gbuf[n] for n in names],
                                             [rxbuf[n] for n in names], items)
        gbuf.update(zip(names, send))
        rxbuf.update(zip(names, rx))
        pending.append((tag, names, items, sem))

    gsmall = {n: [None] * w[n].shape[0] for n in small_names}
    s5_cot = [None] * s5_a_re.shape[0]
    cot_names = ["dlb_re", "dlb_im", "dbb_re", "dbb_im", "dc_re", "dc_im", "dd"]
    ln_names = ["ln1_g", "ln1_b", "lnm_g", "lnm_b", "ln2_g", "ln2_b"]

    def zrow():
        return jnp.zeros((1, d), F32) + zero

    def ffn_done(which, i):
        def done(g_win, g_wout3):
            gbuf[f"{which}_w_in"], gbuf[f"{which}_w_out"] = g_win, g_wout3.reshape(gshape[f"{which}_w_out"])
            scatter(f"l{i}_{which}", [(f"{which}_w_in", i), (f"{which}_w_out", i)])
            return zero
        return done

    _, _, _, (_, _, _, xhat_top, rstd_top) = saved[depth - 1]
    dz, dg, db = _ln_bwd("top_ln_bwd", [(h, 1.0 / d), (tgt, -1.0 / d)], xhat_top, rstd_top, ln2_g[depth - 1:depth])
    gsmall["ln2_g"][depth - 1], gsmall["ln2_b"][depth - 1] = dg, db
    grad_x = None
    for i in reversed(range(depth)):
        j = i // 2
        s1, sm, (xhat_m, rstd_m), s2 = saved[i]
        dz, dg, db = _ffn_bwd(f"l{i}_ffn2", alpha, dz, s2, full["ffn2_w_in"], rows3["ffn2_w_out"], i,
                              gbuf["ffn2_w_in"], _rows_view(gbuf["ffn2_w_out"]), ffn_done("ffn2", i),
                              (xhat_m, rstd_m, lnm_g[i:i + 1]), zrow())
        gsmall["lnm_g"][i], gsmall["lnm_b"][i] = dg, db
        ln1 = (s1[3], s1[4], ln1_g[i:i + 1])
        if i % 2 == 0:
            def fox_done(g_wo3, d_wpad, j=j, i=i):
                gbuf["fox_w_o"] = g_wo3.reshape(gshape["fox_w_o"])
                gbuf[fox_in_names[j]] = d_wpad[:, :fox_cols].reshape(d, N_CHIPS, -1).transpose(1, 0, 2)[None].astype(BF16)
                scatter(f"l{i}_fox", [("fox_w_o", j), (fox_in_names[j], 0)])
                return zero

            (dz, dg, db), gsmall["fox_b_f"][j] = _fox_bwd(
                f"l{i}_fox", alpha, dz, sm, fox_b_f[j:j + 1], rows3["fox_w_o"], j,
                _rows_view(gbuf["fox_w_o"]), fox_done, ln1)
        else:
            dx, gbuf["s5_w_out"], s5_cot[j] = _s5_bwd(f"l{i}_s5", dz, sm, s5_params(j), full["s5_w_out"], j,
                                                      gbuf["s5_w_out"])
            scatter(f"l{i}_s5", [("s5_w_out", j)])
            dz, dg, db = _ln_bwd(f"l{i}_ln1_bwd", [(dz, alpha), (dx, 1.0)], ln1[0], ln1[1], ln1[2] + zero)
        gsmall["ln1_g"][i], gsmall["ln1_b"][i] = dg, db
        if i > 0:
            below = saved[i - 1][3]
            dz, dg, db = _ffn_bwd(f"l{i}_ffn1", alpha, dz, s1, full["ffn1_w_in"], rows3["ffn1_w_out"], i,
                                  gbuf["ffn1_w_in"], _rows_view(gbuf["ffn1_w_out"]), ffn_done("ffn1", i),
                                  (below[3], below[4], ln2_g[i - 1:i]), zrow())
            gsmall["ln2_g"][i - 1], gsmall["ln2_b"][i - 1] = dg, db
        else:
            pieces = [loss_part + zero] + [jnp.concatenate(gsmall[n], axis=0) for n in ln_names + ["fox_b_f"]]
            pieces += [jnp.stack([s5_cot[k][n] for k in range(len(s5_cot))]) for n in cot_names]
            mine = _pack(pieces)
            share_sem, mine, land, zero = _share_start("small_share_start", mine, lax.empty((N_DEV,) + mine.shape, F32))
            grad_x = _ffn_bwd(f"l{i}_ffn1", alpha, dz, s1, full["ffn1_w_in"], rows3["ffn1_w_out"], i,
                              gbuf["ffn1_w_in"], _rows_view(gbuf["ffn1_w_out"]), ffn_done("ffn1", i),
                              (None, None, jnp.zeros((1, d), F32)), zrow()).reshape(x.shape)

    shapes = [p.shape for p in pieces]
    mine, land = _share_wait("small_share_wait", mine, land, share_sem[0], share_sem[1], grad_x)
    summed = _unpack(_sum_devices("small_sum", mine, land), shapes)
    loss = summed[0].reshape(())
    gs_final = dict(zip(ln_names + ["fox_b_f"], summed[1:8]))
    cot = dict(zip(cot_names, summed[8:]))
    prm_names = ["s5_a_re", "s5_a_im", "s5_log_dt", "s5_b_re", "s5_b_im"]
    _, disc_vjp = jax.vjp(jax.vmap(_s5_discretise), *[w[n] for n in prm_names])
    for n, gval in zip(prm_names, disc_vjp((cot["dlb_re"], cot["dlb_im"], cot["dbb_re"], cot["dbb_im"]))):
        gs_final[n] = gval
    gs_final["s5_c_re"], gs_final["s5_c_im"], gs_final["s5_d"] = cot["dc_re"], cot["dc_im"], cot["dd"]

    grads, deltas, new_m, new_v = {}, {}, {}, {}
    small_shapes = [w[n].shape for n in small_names]
    for n in small_names:
        grads[n] = gs_final[n].reshape(w[n].shape)
    packed = [_pack([src[n] for n in small_names]) for src in (w, grads, mom, vel)]
    small_out = _adamw("adamw_small", *packed)
    for dst, buf in zip((deltas, new_m, new_v), small_out):
        for n, val in zip(small_names, _unpack(buf, small_shapes)):
            dst[n] = val

    for tag, names, items, sem in pending:
        send, rx = _scatter_wait(f"scatter_wait_{tag}", [gbuf[n] for n in names], [rxbuf[n] for n in names],
                                 sem[0], sem[1], small_out[0], items)
        gbuf.update(zip(names, send))
        rxbuf.update(zip(names, rx))
    half = {n: _chip_sum(f"grad_chip_sum_{n}", gbuf[n], rxbuf[n]) for n in gshape}
    half["fox_w_in"] = jnp.concatenate([half[n] for n in fox_in_names], axis=0)
    halves = [half[n] for n in big_names]
    theirs = _send_half("grad_send_half", halves)
    for n, mine_h, their_h in zip(big_names, halves, theirs):
        grads[n], deltas[n], new_m[n], new_v[n] = _adamw_join(f"adamw_{n}", w[n], mine_h, their_h, mom[n], vel[n])
    return (loss, grad_x, *[grads[n] for n in out_order], *[deltas[n] for n in out_order],
            *[new_m[n] for n in out_order], *[new_v[n] for n in out_order])
```
